```python
import math
import jax
import jax.numpy as jnp
from jax import lax
import numpy as np

D_MODEL = 1024
BATCH = 16
SEQ = 2048
DEPTH = 1

CTX_LEN = 256
GRID_W = 64
EPS = 1e-6
CHUNK = 64

GDN_HEADS = 8
GDN_DK = 128
GDN_DV = 128
GDN_CONV = 5
MLSTM_HEADS = 8
MLSTM_DK = 64
MLSTM_DV = 128
N_EXPERTS = 256
TOP_K = 8
N_GROUPS = 8
TOPK_GROUPS = 4
D_EXPERT = 256
D_SHARED = 256
ROUTED_SCALE = 2.5
EXPERT_BLOCK = 128

GDN_QK = GDN_HEADS * GDN_DK
GDN_V = GDN_HEADS * GDN_DV
ML_QK = MLSTM_HEADS * MLSTM_DK
ML_V = MLSTM_HEADS * MLSTM_DV
IN_SIZES = (2 * GDN_QK + GDN_V, GDN_V, 4 * GDN_HEADS, ML_QK, ML_QK, ML_V, ML_V, 4 * MLSTM_HEADS, D_MODEL, D_MODEL)
D_IN = sum(IN_SIZES)

kernel_name = 'hybrid_gdn_mlstm_moe_prefix_dit_block'


def rms_norm(x, g):
    xf = x.astype(jnp.float32)
    y = xf * lax.rsqrt(jnp.mean(xf * xf, axis=-1, keepdims=True) + EPS)
    return (y * g.astype(jnp.float32)).astype(x.dtype)


def l2_norm(x):
    xf = x.astype(jnp.float32)
    return (xf * lax.rsqrt(jnp.sum(xf * xf, axis=-1, keepdims=True) + EPS)).astype(x.dtype)


def modulate(h, shift, scale):
    return h * (1.0 + scale) + shift


def centred_conv(x, w):
    k = w.shape[0]
    return lax.conv_general_dilated(
        x, w[:, None, :].astype(x.dtype), window_strides=(1,), padding=[(k // 2, k // 2)],
        dimension_numbers=('NWC', 'WIO', 'NWC'), feature_group_count=x.shape[-1])


def to_heads(t, n_heads):
    b, l, f = t.shape
    return t.reshape(b, l, n_heads, f // n_heads).transpose(0, 2, 1, 3)


def from_heads(t):
    b, h, l, d = t.shape
    return t.transpose(0, 2, 1, 3).reshape(b, l, h * d)


def to_col_major(t):
    b, l, f = t.shape
    rows = l // GRID_W
    return t.reshape(b, rows, GRID_W, f).transpose(0, 2, 1, 3).reshape(b, l, f)


def from_col_major(t):
    b, l, f = t.shape
    rows = l // GRID_W
    return t.reshape(b, GRID_W, rows, f).transpose(0, 2, 1, 3).reshape(b, l, f)


def keep_seq(t):
    return t


def flip_seq(t):
    return jnp.flip(t, axis=2)


def gated_delta_chunked(q, k, v, beta, g, s0):
    f32 = jnp.float32
    q, k, v, beta, g = (t.astype(f32) for t in (q, k, v, beta, g))
    b, h, l, _ = q.shape
    dv = v.shape[-1]
    n = l // CHUNK
    q, k, v = (t.reshape(b, h, n, CHUNK, -1) for t in (q, k, v))
    beta, g = (t.reshape(b, h, n, CHUNK) for t in (beta, g))
    cum_g = jnp.cumsum(g, axis=-1)
    incl = jnp.tril(jnp.ones((CHUNK, CHUNK), bool))
    strict = jnp.tril(jnp.ones((CHUNK, CHUNK), bool), -1)
    decay = jnp.exp(jnp.where(incl, cum_g[..., :, None] - cum_g[..., None, :], -jnp.inf))
    k_beta = k * beta[..., None]
    a_strict = jnp.where(strict, jnp.einsum('bhnid,bhnjd->bhnij', k_beta, k) * decay, 0.0)
    rhs = jnp.concatenate([v * beta[..., None], k_beta * jnp.exp(cum_g)[..., None]], axis=-1)
    sol = lax.linalg.triangular_solve(a_strict + jnp.eye(CHUNK, dtype=f32), rhs,
                                      left_side=True, lower=True, unit_diagonal=True)
    u, w = sol[..., :dv], sol[..., dv:]
    qk = jnp.einsum('bhnid,bhnjd->bhnij', q, k) * decay
    q_dec = q * jnp.exp(cum_g)[..., None]
    k_dec = k * jnp.exp(cum_g[..., -1:] - cum_g)[..., None]
    chunk_decay = jnp.exp(cum_g[..., -1])

    def step(s, inp):
        u_c, w_c, q_c, k_c, qk_c, d_c = inp
        v_new = u_c - jnp.einsum('bhcd,bhde->bhce', w_c, s)
        o_c = jnp.einsum('bhcd,bhde->bhce', q_c, s) + jnp.einsum('bhij,bhje->bhie', qk_c, v_new)
        s = s * d_c[..., None, None] + jnp.einsum('bhcd,bhce->bhde', k_c, v_new)
        return s, o_c

    xs = tuple(jnp.moveaxis(t, 2, 0) for t in (u, w, q_dec, k_dec, qk, chunk_decay))
    s_fin, o = lax.scan(step, s0.astype(f32), xs)
    return jnp.moveaxis(o, 0, 2).reshape(b, h, l, dv), s_fin


def mlstm_chunked(q, k, v, ig, lf, c0, n0, m0):
    f32 = jnp.float32
    q, k, v, ig, lf = (t.astype(f32) for t in (q, k, v, ig, lf))
    b, h, l, _ = q.shape
    dv = v.shape[-1]
    n = l // CHUNK
    q, k, v = (t.reshape(b, h, n, CHUNK, -1) for t in (q, k, v))
    ig, lf = (t.reshape(b, h, n, CHUNK) for t in (ig, lf))
    bcum = jnp.cumsum(lf, axis=-1)
    b_last = bcum[..., -1]
    a_log = b_last[..., None] - bcum + ig
    m_loc = jnp.max(a_log, axis=-1)
    wk = k * jnp.exp(a_log - m_loc[..., None])[..., None]
    c_loc = jnp.einsum('bhncd,bhnce->bhnde', wk, v)
    n_loc = jnp.sum(wk, axis=-2)

    def step(carry, inp):
        c, nv, m = carry
        bl, ml, cl, nl = inp
        m_new = jnp.maximum(bl + m, ml)
        s_prev = jnp.exp(bl + m - m_new)
        s_loc = jnp.exp(ml - m_new)
        c_new = s_prev[..., None, None] * c + s_loc[..., None, None] * cl
        n_new = s_prev[..., None] * nv + s_loc[..., None] * nl
        return (c_new, n_new, m_new), (c, nv, m)

    xs = tuple(jnp.moveaxis(t, 2, 0) for t in (b_last, m_loc, c_loc, n_loc))
    final, starts = lax.scan(step, (c0.astype(f32), n0.astype(f32), m0.astype(f32)), xs)
    c_s, n_s, m_s = (jnp.moveaxis(t, 0, 2) for t in starts)
    incl = jnp.tril(jnp.ones((CHUNK, CHUNK), bool))
    log_d = jnp.where(incl, bcum[..., :, None] - bcum[..., None, :] + ig[..., None, :], -jnp.inf)
    log_inter = bcum + m_s[..., None]
    m_t = jnp.maximum(log_inter, jnp.max(log_d, axis=-1))
    s_inter = jnp.exp(log_inter - m_t)
    p = jnp.exp(log_d - m_t[..., None]) * jnp.einsum('bhnid,bhnjd->bhnij', q, k)
    num = s_inter[..., None] * jnp.einsum('bhnid,bhnde->bhnie', q, c_s) + jnp.einsum('bhnij,bhnje->bhnie', p, v)
    den = s_inter * jnp.einsum('bhnid,bhnd->bhni', q, n_s) + jnp.sum(p, axis=-1)
    h_t = num / jnp.maximum(jnp.abs(den), jnp.exp(-m_t))[..., None]
    return h_t.reshape(b, h, l, dv), final


def gdn_prepare(qkv_raw, gate_raw, conv_w, a_log, dt_bias):
    b, l, _ = qkv_raw.shape
    qkv = jax.nn.silu(centred_conv(qkv_raw, conv_w))
    q, k, v = jnp.split(qkv, [GDN_QK, 2 * GDN_QK], axis=-1)
    q = l2_norm(to_heads(q, GDN_HEADS)) * (GDN_DK ** -0.5)
    k = l2_norm(to_heads(k, GDN_HEADS))
    v = to_heads(v, GDN_HEADS)
    gr = gate_raw.astype(jnp.float32).reshape(b, l, 4, GDN_HEADS).transpose(2, 0, 3, 1)
    beta = jax.nn.sigmoid(gr[:2])
    g = -jnp.exp(a_log.astype(jnp.float32))[:, None, :, None] * jax.nn.softplus(
        gr[2:] + dt_bias.astype(jnp.float32)[:, None, :, None])
    return q, k, v, beta, g


def gdn_bidirectional(ctx_in, lat_in):
    qc, kc, vc, bc, gc = ctx_in
    ql, kl, vl, bl, gl = lat_in
    bsz = qc.shape[0]
    out_c = 0.0
    out_l = 0.0
    for d in range(2):
        f = flip_seq if d else keep_seq
        s0 = jnp.zeros((bsz, GDN_HEADS, GDN_DK, GDN_DV), jnp.float32)
        oc, s_ctx = gated_delta_chunked(f(qc), f(kc), f(vc), f(bc[d]), f(gc[d]), s0)
        ol, _ = gated_delta_chunked(f(ql), f(kl), f(vl), f(bl[d]), f(gl[d]), s_ctx)
        out_c = out_c + f(oc)
        out_l = out_l + f(ol)
    return out_c, out_l


def gdn_output(o, z, norm_g):
    b, h, l, dv = o.shape
    y = rms_norm(o.transpose(0, 2, 1, 3), norm_g) * jax.nn.silu(z.reshape(b, l, h, dv))
    return y.reshape(b, l, h * dv)


def mlstm_prepare(q_raw, k_raw, v_raw, gate_raw, i_bias, f_bias):
    b, l, _ = q_raw.shape
    q = to_heads(q_raw, MLSTM_HEADS)
    k = to_heads(k_raw, MLSTM_HEADS) * (MLSTM_DK ** -0.5)
    v = to_heads(v_raw, MLSTM_HEADS)
    gr = gate_raw.astype(jnp.float32).reshape(b, l, 4, MLSTM_HEADS).transpose(2, 0, 3, 1)
    ig = gr[:2] + i_bias.astype(jnp.float32)[:, None, :, None]
    lf = jax.nn.log_sigmoid(gr[2:] + f_bias.astype(jnp.float32)[:, None, :, None])
    return q, k, v, ig, lf


def mlstm_bidirectional(ctx_in, lat_in):
    qc, kc, vc, ic, fc = ctx_in
    ql, kl, vl, il, fl = lat_in
    bsz = qc.shape[0]
    out_c = 0.0
    out_l = 0.0
    for d in range(2):
        f = flip_seq if d else keep_seq
        c0 = jnp.zeros((bsz, MLSTM_HEADS, MLSTM_DK, MLSTM_DV), jnp.float32)
        n0 = jnp.zeros((bsz, MLSTM_HEADS, MLSTM_DK), jnp.float32)
        m0 = jnp.zeros((bsz, MLSTM_HEADS), jnp.float32)
        hc, st = mlstm_chunked(f(qc), f(kc), f(vc), f(ic[d]), f(fc[d]), c0, n0, m0)
        hl, _ = mlstm_chunked(f(ql), f(kl), f(vl), f(il[d]), f(fl[d]), st[0], st[1], st[2])
        out_c = out_c + f(hc)
        out_l = out_l + f(hl)
    return out_c, out_l


def mlstm_output(hs, o_raw, norm_g):
    b, l, _ = hs.shape
    y = (jax.nn.sigmoid(o_raw) * hs).reshape(b, l, MLSTM_HEADS, MLSTM_DV)
    return rms_norm(y, norm_g.reshape(MLSTM_HEADS, MLSTM_DV)).reshape(b, l, ML_V)


def token_mixers(h_lat, h_ctx, w_in, conv_w, a_log, dt_bias, gdn_norm_g, i_bias, f_bias, ml_norm_g,
                 w_br_gdn, w_br_ml, w_out, with_ctx_out):
    offsets = np.cumsum(IN_SIZES)[:-1].tolist()
    lat = jnp.split(h_lat @ w_in, offsets, axis=-1)
    cx = jnp.split(h_ctx @ w_in, offsets, axis=-1)
    gdn_c, gdn_l = gdn_bidirectional(
        gdn_prepare(cx[0], cx[2], conv_w, a_log, dt_bias),
        gdn_prepare(lat[0], lat[2], conv_w, a_log, dt_bias))
    ml_c, ml_l = mlstm_bidirectional(
        mlstm_prepare(cx[3], cx[4], cx[5], cx[7], i_bias, f_bias),
        mlstm_prepare(*[to_col_major(t) for t in (lat[3], lat[4], lat[5], lat[7])], i_bias, f_bias))

    def merge(p, o_gdn, h_ml):
        y_gdn = gdn_output(o_gdn, p[1], gdn_norm_g) @ w_br_gdn
        y_ml = mlstm_output(h_ml, p[6], ml_norm_g) @ w_br_ml
        mixed = jax.nn.sigmoid(p[8]) * y_gdn + jax.nn.sigmoid(p[9]) * y_ml
        return mixed @ w_out

    y_lat = merge(lat, gdn_l, from_col_major(from_heads(ml_l))).astype(h_lat.dtype)
    y_ctx = merge(cx, gdn_c, from_heads(ml_c)).astype(h_ctx.dtype) if with_ctx_out else None
    return y_lat, y_ctx


def swiglu(x, w_gate_up, w_down):
    gate, up = jnp.split(x @ w_gate_up, 2, axis=-1)
    return (jax.nn.silu(gate) * up) @ w_down


def moe_ffn(h, w_router, router_bias, w_gu, w_down, w_sh_gu, w_sh_down):
    lead = h.shape[:-1]
    hf = h.reshape(-1, D_MODEL)
    t = hf.shape[0]
    scores = jax.nn.sigmoid((hf @ w_router).astype(jnp.float32))
    sel = scores + router_bias.astype(jnp.float32)
    grp_score = jnp.sum(lax.top_k(sel.reshape(t, N_GROUPS, -1), 2)[0], axis=-1)
    top_grp = lax.top_k(grp_score, TOPK_GROUPS)[1]
    grp_keep = jnp.any(top_grp[..., None] == jnp.arange(N_GROUPS), axis=1)
    sel = jnp.where(jnp.repeat(grp_keep, N_EXPERTS // N_GROUPS, axis=1), sel, -jnp.inf)
    expert_idx = lax.top_k(sel, TOP_K)[1]
    wts = jnp.take_along_axis(scores, expert_idx, axis=1)
    wts = wts / jnp.sum(wts, axis=-1, keepdims=True) * ROUTED_SCALE
    n_assign = t * TOP_K
    n_blocks = (n_assign + N_EXPERTS * (EXPERT_BLOCK - 1) + EXPERT_BLOCK - 1) // EXPERT_BLOCK
    n_slots = n_blocks * EXPERT_BLOCK
    e_flat = expert_idx.reshape(-1)
    order = jnp.argsort(e_flat)
    e_sorted = e_flat[order]
    counts = jnp.bincount(e_flat, length=N_EXPERTS)
    padded = (counts + EXPERT_BLOCK - 1) // EXPERT_BLOCK * EXPERT_BLOCK
    start = jnp.cumsum(counts) - counts
    pend = jnp.cumsum(padded)
    pstart = pend - padded
    dest = pstart[e_sorted] + jnp.arange(n_assign) - start[e_sorted]
    tok_slot = jnp.zeros((n_slots,), jnp.int32).at[dest].set((order // TOP_K).astype(jnp.int32))
    w_slot = jnp.zeros((n_slots,), jnp.float32).at[dest].set(wts.reshape(-1)[order])
    blk_expert = jnp.minimum(jnp.searchsorted(pend, jnp.arange(n_blocks) * EXPERT_BLOCK, side='right'),
                             N_EXPERTS - 1)
    xb = hf[tok_slot].reshape(n_blocks, EXPERT_BLOCK, D_MODEL)

    def expert_block(args):
        x_blk, e = args
        return swiglu(x_blk, w_gu[e], w_down[e])

    yb = lax.map(expert_block, (xb, blk_expert)).reshape(n_slots, D_MODEL)
    routed = jnp.zeros((t, D_MODEL), jnp.float32).at[tok_slot].add(yb.astype(jnp.float32) * w_slot[:, None])
    out = routed + swiglu(hf, w_sh_gu, w_sh_down).astype(jnp.float32)
    return out.astype(h.dtype).reshape(*lead, D_MODEL)


def setup_inputs(seed: int = 0) -> dict:
    key = jax.random.key(seed)
    ks = jax.random.split(key, 32)
    f32 = jnp.float32

    def nrm(k, shape, scale):
        return jax.random.normal(k, shape, f32) * scale

    def gain(k, shape):
        return 1.0 + 0.02 * jax.random.normal(k, shape, f32)

    dt = jnp.exp(jax.random.uniform(ks[11], (DEPTH, 2, GDN_HEADS), f32, math.log(1e-3), math.log(1e-1)))
    f_bias = jnp.linspace(3.0, 6.0, MLSTM_HEADS, dtype=f32)[None, None, :] + nrm(ks[14], (DEPTH, 2, MLSTM_HEADS), 0.1)
    return {
        'x': nrm(ks[0], (BATCH, SEQ, D_MODEL), 1.0),
        'c': nrm(ks[1], (BATCH, D_MODEL), 1.0),
        'ctx': nrm(ks[2], (BATCH, CTX_LEN, D_MODEL), 1.0),
        'c_ctx': nrm(ks[3], (D_MODEL,), 1.0),
        'w_ada': nrm(ks[4], (DEPTH, D_MODEL, 6 * D_MODEL), 0.5 * D_MODEL ** -0.5),
        'b_ada': nrm(ks[5], (DEPTH, 6 * D_MODEL), 0.02),
        'norm1_g': gain(ks[6], (DEPTH, D_MODEL)),
        'norm2_g': gain(ks[7], (DEPTH, D_MODEL)),
        'w_in': nrm(ks[8], (DEPTH, D_MODEL, D_IN), D_MODEL ** -0.5),
        'gdn_conv_w': nrm(ks[9], (DEPTH, GDN_CONV, 2 * GDN_QK + GDN_V), GDN_CONV ** -0.5),
        'gdn_a_log': jnp.log(jax.random.uniform(ks[10], (DEPTH, 2, GDN_HEADS), f32, 1.0, 16.0)),
        'gdn_dt_bias': dt + jnp.log(-jnp.expm1(-dt)),
        'gdn_norm_g': gain(ks[12], (DEPTH, GDN_DV)),
        'ml_i_bias': nrm(ks[13], (DEPTH, 2, MLSTM_HEADS), 0.1),
        'ml_f_bias': f_bias,
        'ml_norm_g': gain(ks[15], (DEPTH, ML_V)),
        'w_branch_gdn': nrm(ks[16], (DEPTH, GDN_V, D_MODEL), GDN_V ** -0.5),
        'w_branch_ml': nrm(ks[17], (DEPTH, ML_V, D_MODEL), ML_V ** -0.5),
        'w_out': nrm(ks[18], (DEPTH, D_MODEL, D_MODEL), D_MODEL ** -0.5),
        'w_router': nrm(ks[19], (DEPTH, D_MODEL, N_EXPERTS), D_MODEL ** -0.5),
        'router_bias': nrm(ks[20], (DEPTH, N_EXPERTS), 0.01),
        'w_exp_gate_up': nrm(ks[21], (DEPTH, N_EXPERTS, D_MODEL, 2 * D_EXPERT), D_MODEL ** -0.5),
        'w_exp_down': nrm(ks[22], (DEPTH, N_EXPERTS, D_EXPERT, D_MODEL), D_EXPERT ** -0.5),
        'w_sh_gate_up': nrm(ks[23], (DEPTH, D_MODEL, 2 * D_SHARED), D_MODEL ** -0.5),
        'w_sh_down': nrm(ks[24], (DEPTH, D_SHARED, D_MODEL), D_SHARED ** -0.5),
        'final_norm_g': gain(ks[25], (D_MODEL,)),
    }


def reference(x, c, ctx, c_ctx, w_ada, b_ada, norm1_g, norm2_g, w_in, gdn_conv_w, gdn_a_log, gdn_dt_bias,
              gdn_norm_g, ml_i_bias, ml_f_bias, ml_norm_g, w_branch_gdn, w_branch_ml, w_out, w_router,
              router_bias, w_exp_gate_up, w_exp_down, w_sh_gate_up, w_sh_down, final_norm_g):
    silu_c = jax.nn.silu(c)[:, None, :]
    silu_cc = jax.nn.silu(c_ctx)
    for layer in range(DEPTH):
        last = layer == DEPTH - 1
        mod = jnp.split(silu_c @ w_ada[layer] + b_ada[layer], 6, axis=-1)
        mod_c = jnp.split(silu_cc @ w_ada[layer] + b_ada[layer], 6, axis=-1)
        h = modulate(rms_norm(x, norm1_g[layer]), mod[0], mod[1])
        hc = modulate(rms_norm(ctx, norm1_g[layer]), mod_c[0], mod_c[1])
        y, yc = token_mixers(h, hc, w_in[layer], gdn_conv_w[layer], gdn_a_log[layer], gdn_dt_bias[layer],
                             gdn_norm_g[layer], ml_i_bias[layer], ml_f_bias[layer], ml_norm_g[layer],
                             w_branch_gdn[layer], w_branch_ml[layer], w_out[layer], not last)
        x = x + mod[2] * y
        h2 = modulate(rms_norm(x, norm2_g[layer]), mod[3], mod[4])
        x = x + mod[5] * moe_ffn(h2, w_router[layer], router_bias[layer], w_exp_gate_up[layer],
                                 w_exp_down[layer], w_sh_gate_up[layer], w_sh_down[layer])
        if not last:
            ctx = ctx + mod_c[2] * yc
            hc2 = modulate(rms_norm(ctx, norm2_g[layer]), mod_c[3], mod_c[4])
            ctx = ctx + mod_c[5] * moe_ffn(hc2, w_router[layer], router_bias[layer], w_exp_gate_up[layer],
                                           w_exp_down[layer], w_sh_gate_up[layer], w_sh_down[layer])
    return rms_norm(x, final_norm_g)
```

```python
import functools
import math

import jax
import jax.numpy as jnp
from jax import lax
from jax.experimental import pallas as pl
from jax.experimental.pallas import tpu as pltpu

F32 = jnp.float32
BF16 = jnp.bfloat16
HI = lax.Precision.HIGHEST

EPS = 1e-6
CHUNK = 64
GRID_W = 64
HEADS = 8
HEAD_V = 128
GDN_DK = 128
ML_DK = 64
GDN_CONV = 5
N_EXPERTS = 256
TOP_K = 8
N_GROUPS = 8
TOPK_GROUPS = 4
ROUTED_SCALE = 2.5
EXPERT_ROWS = 256
LANES = 128
VMEM_LIMIT = 56 * 1024 * 1024

COL_GDN_QKV = 0
COL_GDN_Z = 3072
COL_ML_Q = 4096
COL_ML_K = 4608
COL_ML_V = 5120
COL_ML_O = 6144
COL_MG_GDN = 7168
COL_MG_ML = 8192
N_MAIN = 9216
_ORIG = dict(gdn_qkv=(0, 3072), gdn_z=(3072, 4096), gdn_gate=(4096, 4128), ml_q=(4128, 4640),
             ml_k=(4640, 5152), ml_v=(5152, 6176), ml_o=(6176, 7200), ml_gate=(7200, 7232),
             mg_gdn=(7232, 8256), mg_ml=(8256, 9280))


def _params(sem, vmem=VMEM_LIMIT):
    return pltpu.CompilerParams(dimension_semantics=sem, vmem_limit_bytes=vmem)


def _dot(a, b, precision=None):
    return jnp.dot(a, b, preferred_element_type=F32, precision=precision)


def _dot_nt(a, b, precision=None):
    return lax.dot_general(a, b, (((1,), (1,)), ((), ())), preferred_element_type=F32, precision=precision)


def _dot_tn(a, b, precision=None):
    return lax.dot_general(a, b, (((0,), (0,)), ((), ())), preferred_element_type=F32, precision=precision)


def _sigmoid(x):
    return 1.0 / (1.0 + jnp.exp(-x))


def _softplus(x):
    return jnp.maximum(x, 0.0) + jnp.log(1.0 + jnp.exp(-jnp.abs(x)))


def _ada_kernel(c_ref, w_ref, b_ref, o_ref):
    c = c_ref[...]
    sc = c * _sigmoid(c)
    o_ref[...] = _dot(sc, w_ref[...], HI) + b_ref[...]


def _ada_mod(cc, w_ada, b_ada, tn=1536):
    r, d = cc.shape
    n = w_ada.shape[1]
    return pl.pallas_call(
        _ada_kernel,
        grid=(n // tn,),
        in_specs=[pl.BlockSpec((r, d), lambda j: (0, 0)),
                  pl.BlockSpec((d, tn), lambda j: (0, j)),
                  pl.BlockSpec((1, tn), lambda j: (0, j))],
        out_specs=pl.BlockSpec((r, tn), lambda j: (0, j)),
        out_shape=jax.ShapeDtypeStruct((r, n), F32),
        compiler_params=_params(("arbitrary",)),
    )(cc, w_ada, b_ada.reshape(1, n))


def _proj_kernel(x_ref, mod_ref, g_ref, w_ref, wg_ref, o_ref, og_ref, hn_ref):
    d = x_ref.shape[1]

    @pl.when(pl.program_id(1) == 0)
    def _():
        x = x_ref[...]
        y = x * lax.rsqrt(jnp.mean(x * x, axis=-1, keepdims=True) + EPS) * g_ref[...]
        shift = mod_ref[0, :, 0:d]
        scale = mod_ref[0, :, d:2 * d]
        h = (y * (1.0 + scale) + shift).astype(BF16)
        hn_ref[...] = h
        og_ref[...] = _dot(h, wg_ref[...])

    o_ref[...] = _dot(hn_ref[...], w_ref[...]).astype(o_ref.dtype)


def _project(x2d, mod3, mod_row_of_tile, norm_g, w_main, w_gate, tm, tn=1024):
    t, d = x2d.shape
    n = w_main.shape[1]
    return pl.pallas_call(
        _proj_kernel,
        grid=(t // tm, n // tn),
        in_specs=[pl.BlockSpec((tm, d), lambda i, j: (i, 0)),
                  pl.BlockSpec((1, 1, mod3.shape[2]), lambda i, j: (mod_row_of_tile(i), 0, 0)),
                  pl.BlockSpec((1, d), lambda i, j: (0, 0)),
                  pl.BlockSpec((d, tn), lambda i, j: (0, j)),
                  pl.BlockSpec((d, LANES), lambda i, j: (0, 0))],
        out_specs=[pl.BlockSpec((tm, tn), lambda i, j: (i, j)),
                   pl.BlockSpec((tm, LANES), lambda i, j: (i, 0))],
        out_shape=[jax.ShapeDtypeStruct((t, n), BF16), jax.ShapeDtypeStruct((t, LANES), F32)],
        scratch_shapes=[pltpu.VMEM((tm, d), BF16)],
        compiler_params=_params(("arbitrary", "arbitrary")),
    )(x2d, mod3, norm_g.reshape(1, d), w_main, w_gate)


def _gate_kernel(g_ref, p_ref, v_ref, cf_ref, cb_ref):
    rows = g_ref.shape[0]
    raw = g_ref[...] + p_ref[0:1, :]
    lane = lax.broadcasted_iota(jnp.int32, raw.shape, 1)
    sp = _softplus(raw)
    vals = jnp.where(lane < 16, _sigmoid(raw),
                     jnp.where(lane < 32, p_ref[1:2, :] * sp,
                               jnp.where(lane < 48, raw,
                                         jnp.where(lane < 64, -_softplus(-raw), 0.0))))
    v_ref[...] = vals
    ri = lax.broadcasted_iota(jnp.int32, (CHUNK, CHUNK), 0)
    ci = lax.broadcasted_iota(jnp.int32, (CHUNK, CHUNK), 1)
    tri_f = (ri >= ci).astype(F32)
    tri_b = (ri <= ci).astype(F32)
    for c in range(rows // CHUNK):
        blk = vals[c * CHUNK:(c + 1) * CHUNK, :]
        cf_ref[c * CHUNK:(c + 1) * CHUNK, :] = _dot(tri_f, blk, HI)
        cb_ref[c * CHUNK:(c + 1) * CHUNK, :] = _dot(tri_b, blk, HI)


def _gate_prep(graw, gparams, tm=256):
    t = graw.shape[0]
    spec = pl.BlockSpec((tm, LANES), lambda i: (i, 0))
    return pl.pallas_call(
        _gate_kernel,
        grid=(t // tm,),
        in_specs=[spec, pl.BlockSpec((8, LANES), lambda i: (0, 0))],
        out_specs=[spec, spec, spec],
        out_shape=[jax.ShapeDtypeStruct((t, LANES), F32)] * 3,
        compiler_params=_params(("arbitrary",)),
    )(graw, gparams)


def _dir_masks(direction):
    ri = lax.broadcasted_iota(jnp.int32, (CHUNK, CHUNK), 0)
    ci = lax.broadcasted_iota(jnp.int32, (CHUNK, CHUNK), 1)
    if direction == 0:
        return ri >= ci, ri > ci
    return ri <= ci, ri < ci


def _gdn_kernel(qc_ref, kc_ref, vc_ref, ql_ref, kl_ref, vl_ref, z_ref, cwq_ref, cwk_ref, cwv_ref,
                gc_ref, gr_ref, ng_ref, y_ref,
                xpad, qs, ks, vs, wq_s, u_s, kd_s, qk_s, out_s):
    lc = qc_ref.shape[1]
    ll = ql_ref.shape[1]
    lt = lc + ll
    n_c, n_l = lc // CHUNK, ll // CHUNK
    n_t = n_c + n_l
    rb = 256

    def l2n(x):
        return x * lax.rsqrt(jnp.sum(x * x, axis=-1, keepdims=True) + EPS)

    def prep(src_ref, cw_ref, dst, off, ls, kind):
        xpad[0:8, :] = jnp.zeros((8, LANES), F32)
        xpad[8:8 + ls, :] = src_ref[0].astype(F32)
        xpad[8 + ls:16 + ls, :] = jnp.zeros((8, LANES), F32)
        step = min(rb, ls)
        for r0 in range(0, ls, step):
            acc = jnp.zeros((step, LANES), F32)
            for t in range(GDN_CONV):
                s0 = r0 + 8 - GDN_CONV // 2 + t
                acc = acc + cw_ref[t:t + 1, :] * xpad[s0:s0 + step, :]
            y = acc * _sigmoid(acc)
            if kind == "q":
                y = l2n(y) * (GDN_DK ** -0.5)
            elif kind == "k":
                y = l2n(y)
            dst[off + r0:off + r0 + step, :] = y

    prep(qc_ref, cwq_ref, qs, 0, lc, "q")
    prep(kc_ref, cwk_ref, ks, 0, lc, "k")
    prep(vc_ref, cwv_ref, vs, 0, lc, "v")
    prep(ql_ref, cwq_ref, qs, lc, ll, "q")
    prep(kl_ref, cwk_ref, ks, lc, ll, "k")
    prep(vl_ref, cwv_ref, vs, lc, ll, "v")

    eye = (lax.broadcasted_iota(jnp.int32, (CHUNK, CHUNK), 0)
           == lax.broadcasted_iota(jnp.int32, (CHUNK, CHUNK), 1)).astype(F32)

    for d in range(2):
        incl, strict = _dir_masks(d)
        last = CHUNK - 1 if d == 0 else 0

        def chunk_body(c, carry, d=d, incl=incl, strict=strict, last=last):
            r0 = pl.multiple_of(c * CHUNK, CHUNK)
            q = qs[pl.ds(r0, CHUNK), :]
            k = ks[pl.ds(r0, CHUNK), :]
            v = vs[pl.ds(r0, CHUNK), :]
            gcol = gc_ref[0, 0, pl.ds(r0, CHUNK), :]
            grow = gr_ref[0, 0, c]
            beta = gcol[:, d:d + 1]
            cgc = gcol[:, 2 + d:3 + d]
            cgr = grow[2 + d:3 + d, :]
            cg_last = cgc[last:last + 1, :]
            decay = jnp.exp(jnp.where(incl, cgc - cgr, -jnp.inf))
            ecg = jnp.exp(cgc)
            kb = k * beta
            a = jnp.where(strict, _dot_nt(kb, k) * decay, 0.0)
            x = -a
            t = eye + x
            for _ in range(5):
                x = _dot(x, x, HI)
                t = t + _dot(t, x, HI)
            u = _dot(t, v * beta)
            w = _dot(t, kb * ecg)
            qk = _dot_nt(q, k) * decay
            base = pl.multiple_of((d * n_t + c) * CHUNK, CHUNK)
            base2 = pl.multiple_of((d * n_t + c) * 2 * CHUNK, 2 * CHUNK)
            wq_s[pl.ds(base2, CHUNK), :] = w
            wq_s[pl.ds(base2 + CHUNK, CHUNK), :] = q * ecg
            u_s[pl.ds(base, CHUNK), :] = u
            kd_s[pl.ds(base, CHUNK), :] = k * jnp.exp(cg_last - cgc)
            qk_s[pl.ds(base, CHUNK), :] = qk
            return carry

        lax.fori_loop(0, n_t, chunk_body, 0)

    out_s[...] = jnp.zeros(out_s.shape, F32)

    def scan_step(d, c, s, with_out):
        last = CHUNK - 1 if d == 0 else 0
        base = pl.multiple_of((d * n_t + c) * CHUNK, CHUNK)
        base2 = pl.multiple_of((d * n_t + c) * 2 * CHUNK, 2 * CHUNK)
        ws = _dot(wq_s[pl.ds(base2, 2 * CHUNK), :], s)
        v_new = u_s[pl.ds(base, CHUNK), :] - ws[0:CHUNK, :]
        if with_out:
            o = ws[CHUNK:2 * CHUNK, :] + _dot(qk_s[pl.ds(base, CHUNK), :], v_new)
            l0 = pl.multiple_of((c - n_c) * CHUNK, CHUNK)
            out_s[pl.ds(l0, CHUNK), :] += o
        cg_last = gc_ref[0, 0, pl.ds(c * CHUNK + last, 1), :][:, 2 + d:3 + d]
        return s * jnp.exp(cg_last) + _dot_tn(kd_s[pl.ds(base, CHUNK), :], v_new)

    def ctx_body(i, carry):
        sf, sb = carry
        return scan_step(0, i, sf, False), scan_step(1, n_c - 1 - i, sb, False)

    def lat_body(i, carry):
        sf, sb = carry
        return scan_step(0, n_c + i, sf, True), scan_step(1, n_t - 1 - i, sb, True)

    s0 = jnp.zeros((GDN_DK, HEAD_V), F32)
    carry = lax.fori_loop(0, n_c, ctx_body, (s0, s0))
    lax.fori_loop(0, n_l, lat_body, carry)

    def out_body(i, carry):
        r0 = pl.multiple_of(i * rb, rb)
        o = out_s[pl.ds(r0, rb), :]
        z = z_ref[0, pl.ds(r0, rb), :].astype(F32)
        y = o * lax.rsqrt(jnp.mean(o * o, axis=-1, keepdims=True) + EPS) * ng_ref[...]
        y_ref[0, pl.ds(r0, rb), :] = (y * (z * _sigmoid(z))).astype(y_ref.dtype)
        return carry

    lax.fori_loop(0, ll // rb, out_body, 0)


def _gdn(proj_c, proj_l, conv_w8, gcol, grow, norm_g):
    b, lc, _ = proj_c.shape
    ll = proj_l.shape[1]
    lt = lc + ll
    n_t = lt // CHUNK
    qb, kb_, vb, zb = (COL_GDN_QKV // LANES, COL_GDN_QKV // LANES + HEADS, COL_GDN_QKV // LANES + 2 * HEADS,
                       COL_GDN_Z // LANES)

    def seq_spec(l, col0):
        return pl.BlockSpec((1, l, LANES), lambda i, h: (i, 0, col0 + h))

    def cw_spec(col0):
        return pl.BlockSpec((8, LANES), lambda i, h: (0, col0 + h))

    return pl.pallas_call(
        _gdn_kernel,
        grid=(b, HEADS),
        in_specs=[seq_spec(lc, qb), seq_spec(lc, kb_), seq_spec(lc, vb),
                  seq_spec(ll, qb), seq_spec(ll, kb_), seq_spec(ll, vb), seq_spec(ll, zb),
                  cw_spec(0), cw_spec(HEADS), cw_spec(2 * HEADS),
                  pl.BlockSpec((1, 1, lt, 8), lambda i, h: (i, h, 0, 0)),
                  pl.BlockSpec((1, 1, n_t, 8, CHUNK), lambda i, h: (i, h, 0, 0, 0)),
                  pl.BlockSpec((1, LANES), lambda i, h: (0, 0))],
        out_specs=pl.BlockSpec((1, ll, LANES), lambda i, h: (i, 0, h)),
        out_shape=jax.ShapeDtypeStruct((b, ll, HEADS * HEAD_V), BF16),
        scratch_shapes=[pltpu.VMEM((max(lc, ll) + 16, LANES), F32),
                        pltpu.VMEM((lt, LANES), F32), pltpu.VMEM((lt, LANES), F32), pltpu.VMEM((lt, LANES), F32),
                        pltpu.VMEM((4 * lt, LANES), F32),
                        pltpu.VMEM((2 * lt, LANES), F32), pltpu.VMEM((2 * lt, LANES), F32),
                        pltpu.VMEM((2 * lt, CHUNK), F32),
                        pltpu.VMEM((ll, LANES), F32)],
        compiler_params=_params(("arbitrary", "arbitrary")),
    )(proj_c, proj_c, proj_c, proj_l, proj_l, proj_l, proj_l, conv_w8, conv_w8, conv_w8,
      gcol, grow, norm_g.reshape(1, LANES))


def _mlstm_kernel(qc_ref, kc_ref, vc_ref, ql_ref, kl_ref, vl_ref, mc_ref, mr_ref, h_ref, out_s):
    lc = qc_ref.shape[1]
    ll = ql_ref.shape[1]
    n_c, n_l = lc // CHUNK, ll // CHUNK
    n_t = n_c + n_l
    head = pl.program_id(1)
    lane = lax.broadcasted_iota(jnp.int32, (CHUNK, LANES), 1)
    hmask = ((lane // ML_DK) == (head % 2)).astype(F32)

    def load(ref_c, ref_l, c):
        def from_c():
            return ref_c[0, pl.ds(pl.multiple_of(c * CHUNK, CHUNK), CHUNK), :].astype(F32)

        def from_l():
            return ref_l[0, pl.ds(pl.multiple_of((c - n_c) * CHUNK, CHUNK), CHUNK), :].astype(F32)

        return from_c, from_l

    def step(d, c, state, is_ctx):
        cs, ns, ms = state
        incl, _ = _dir_masks(d)
        last = CHUNK - 1 if d == 0 else 0
        sel = 0 if is_ctx else 1
        q = load(qc_ref, ql_ref, c)[sel]() * hmask
        k = load(kc_ref, kl_ref, c)[sel]() * (hmask * (ML_DK ** -0.5))
        v = load(vc_ref, vl_ref, c)[sel]()
        r0 = pl.multiple_of(c * CHUNK, CHUNK)
        mcol = mc_ref[0, 0, pl.ds(r0, CHUNK), :]
        mrow = mr_ref[0, 0, c]
        igc = mcol[:, d:d + 1]
        bc = mcol[:, 2 + d:3 + d]
        igr = mrow[d:d + 1, :]
        br = mrow[2 + d:3 + d, :]
        b_last = bc[last:last + 1, :]
        log_d = jnp.where(incl, bc - br + igr, -jnp.inf)
        log_inter = bc + ms
        m_t = jnp.maximum(log_inter, jnp.max(log_d, axis=1, keepdims=True))
        s_inter = jnp.exp(log_inter - m_t)
        p = jnp.exp(log_d - m_t) * _dot_nt(q, k)
        num = s_inter * _dot(q, cs) + _dot(p, v)
        den = s_inter * jnp.sum(q * ns, axis=1, keepdims=True) + jnp.sum(p, axis=1, keepdims=True)
        if not is_ctx:
            hh = num / jnp.maximum(jnp.abs(den), jnp.exp(-m_t))
            l0 = pl.multiple_of((c - n_c) * CHUNK, CHUNK)
            out_s[pl.ds(l0, CHUNK), :] += hh
        a_log = b_last - bc + igc
        m_loc = jnp.max(a_log, axis=0, keepdims=True)
        wk = k * jnp.exp(a_log - m_loc)
        c_loc = _dot_tn(wk, v)
        n_loc = jnp.sum(wk, axis=0, keepdims=True)
        m_new = jnp.maximum(b_last + ms, m_loc)
        s_prev = jnp.exp(b_last + ms - m_new)
        s_loc = jnp.exp(m_loc - m_new)
        return s_prev * cs + s_loc * c_loc, s_prev * ns + s_loc * n_loc, m_new

    out_s[...] = jnp.zeros(out_s.shape, F32)

    def ctx_body(i, carry):
        sf, sb = carry
        return step(0, i, sf, True), step(1, n_c - 1 - i, sb, True)

    def lat_body(i, carry):
        sf, sb = carry
        return step(0, n_c + i, sf, False), step(1, n_t - 1 - i, sb, False)

    st0 = (jnp.zeros((LANES, HEAD_V), F32), jnp.zeros((1, LANES), F32), jnp.zeros((1, 1), F32))
    carry = lax.fori_loop(0, n_c, ctx_body, (st0, st0))
    lax.fori_loop(0, n_l, lat_body, carry)
    h_ref[0] = out_s[...].astype(h_ref.dtype)


def _mlstm(proj_c, q_l, k_l, v_l, mcol, mrow):
    b, lc, _ = proj_c.shape
    ll = q_l.shape[1]
    lt = lc + ll
    n_t = lt // CHUNK
    qb, kb_, vb = COL_ML_Q // LANES, COL_ML_K // LANES, COL_ML_V // LANES
    return pl.pallas_call(
        _mlstm_kernel,
        grid=(b, HEADS),
        in_specs=[pl.BlockSpec((1, lc, LANES), lambda i, h: (i, 0, qb + h // 2)),
                  pl.BlockSpec((1, lc, LANES), lambda i, h: (i, 0, kb_ + h // 2)),
                  pl.BlockSpec((1, lc, LANES), lambda i, h: (i, 0, vb + h)),
                  pl.BlockSpec((1, ll, LANES), lambda i, h: (i, 0, h // 2)),
                  pl.BlockSpec((1, ll, LANES), lambda i, h: (i, 0, h // 2)),
                  pl.BlockSpec((1, ll, LANES), lambda i, h: (i, 0, h)),
                  pl.BlockSpec((1, 1, lt, 8), lambda i, h: (i, h, 0, 0)),
                  pl.BlockSpec((1, 1, n_t, 8, CHUNK), lambda i, h: (i, h, 0, 0, 0))],
        out_specs=pl.BlockSpec((1, ll, LANES), lambda i, h: (i, 0, h)),
        out_shape=jax.ShapeDtypeStruct((b, ll, HEADS * HEAD_V), BF16),
        scratch_shapes=[pltpu.VMEM((ll, LANES), F32)],
        compiler_params=_params(("arbitrary", "arbitrary")),
    )(proj_c, proj_c, proj_c, q_l, k_l, v_l, mcol, mrow)


def _merge_kernel(yg_ref, hm_ref, o_ref, gg_ref, gm_ref, x_ref, mod_ref, mlg_ref, n2_ref,
                  wbg_ref, wbm_ref, wo_ref, wrh_ref, wrl_ref, x1_ref, h2_ref, sc_ref):
    d = x_ref.shape[1]
    o = o_ref[...].astype(F32)
    ym = _sigmoid(o) * hm_ref[...].astype(F32)
    segs = []
    for h in range(HEADS):
        seg = ym[:, h * HEAD_V:(h + 1) * HEAD_V]
        segs.append(seg * lax.rsqrt(jnp.mean(seg * seg, axis=-1, keepdims=True) + EPS))
    ymn = jnp.concatenate(segs, axis=1) * mlg_ref[...]
    y_gdn = _dot(yg_ref[...], wbg_ref[...])
    y_ml = _dot(ymn.astype(BF16), wbm_ref[...])
    mixed = _sigmoid(gg_ref[...].astype(F32)) * y_gdn + _sigmoid(gm_ref[...].astype(F32)) * y_ml
    y = _dot(mixed.astype(BF16), wo_ref[...])
    x1 = x_ref[...] + mod_ref[0, :, 2 * d:3 * d] * y
    x1_ref[...] = x1
    hn = x1 * lax.rsqrt(jnp.mean(x1 * x1, axis=-1, keepdims=True) + EPS) * n2_ref[...]
    h2 = hn * (1.0 + mod_ref[0, :, 4 * d:5 * d]) + mod_ref[0, :, 3 * d:4 * d]
    h2_hi = h2.astype(BF16)
    h2_ref[...] = h2_hi
    h2_lo = (h2 - h2_hi.astype(F32)).astype(BF16)
    logits = _dot(h2_hi, wrh_ref[...]) + (_dot(h2_hi, wrl_ref[...]) + _dot(h2_lo, wrh_ref[...]))
    sc_ref[...] = _sigmoid(logits)


def _merge(y_gdn, h_ml, proj_l2d, x2d, mod3, rows_per_mod, ml_norm_g, norm2_g, wbg, wbm, wo, wr_hi, wr_lo, tm=512):
    t, d = x2d.shape
    e = wr_hi.shape[1]
    row = lambda i: (i, 0)
    const = lambda i: (0, 0)
    return pl.pallas_call(
        _merge_kernel,
        grid=(t // tm,),
        in_specs=[pl.BlockSpec((tm, d), row), pl.BlockSpec((tm, d), row),
                  pl.BlockSpec((tm, d), lambda i: (i, COL_ML_O // d)),
                  pl.BlockSpec((tm, d), lambda i: (i, COL_MG_GDN // d)),
                  pl.BlockSpec((tm, d), lambda i: (i, COL_MG_ML // d)),
                  pl.BlockSpec((tm, d), row),
                  pl.BlockSpec((1, 1, mod3.shape[2]), lambda i: ((i * tm) // rows_per_mod, 0, 0)),
                  pl.BlockSpec((1, d), const), pl.BlockSpec((1, d), const),
                  pl.BlockSpec((d, d), const), pl.BlockSpec((d, d), const), pl.BlockSpec((d, d), const),
                  pl.BlockSpec((d, e), const), pl.BlockSpec((d, e), const)],
        out_specs=[pl.BlockSpec((tm, d), row), pl.BlockSpec((tm, d), row), pl.BlockSpec((tm, e), row)],
        out_shape=[jax.ShapeDtypeStruct((t, d), F32), jax.ShapeDtypeStruct((t, d), BF16),
                   jax.ShapeDtypeStruct((t, e), F32)],
        compiler_params=_params(("arbitrary",)),
    )(y_gdn, h_ml, proj_l2d, proj_l2d, proj_l2d, x2d, mod3, ml_norm_g.reshape(1, d), norm2_g.reshape(1, d),
      wbg, wbm, wo, wr_hi, wr_lo)


def _expert_kernel(be_ref, nu_ref, x_ref, wgu_ref, wd_ref, y_ref, wgu_s, wd_s):
    i = pl.program_id(0)
    de = wd_ref.shape[1]
    changed = jnp.logical_or(i == 0, be_ref[i] != be_ref[jnp.maximum(i - 1, 0)])

    @pl.when(changed)
    def _():
        wgu_s[...] = wgu_ref[0].astype(BF16)
        wd_s[...] = wd_ref[0].astype(BF16)

    @pl.when(i < nu_ref[0])
    def _():
        gu = _dot(x_ref[...], wgu_s[...])
        g = gu[:, 0:de]
        act = (g * _sigmoid(g)) * gu[:, de:2 * de]
        y_ref[...] = _dot(act.astype(BF16), wd_s[...]).astype(y_ref.dtype)

    @pl.when(i >= nu_ref[0])
    def _():
        y_ref[...] = jnp.zeros(y_ref.shape, y_ref.dtype)


def _experts(xb, blk_expert, n_used, w_gu, w_down):
    n_slots, d = xb.shape
    n_blocks = n_slots // EXPERT_ROWS
    e, _, de2 = w_gu.shape
    de = de2 // 2
    return pl.pallas_call(
        _expert_kernel,
        grid_spec=pltpu.PrefetchScalarGridSpec(
            num_scalar_prefetch=2,
            grid=(n_blocks,),
            in_specs=[pl.BlockSpec((EXPERT_ROWS, d), lambda i, be, nu: (i, 0)),
                      pl.BlockSpec((1, d, de2), lambda i, be, nu: (be[i], 0, 0)),
                      pl.BlockSpec((1, de, d), lambda i, be, nu: (be[i], 0, 0))],
            out_specs=pl.BlockSpec((EXPERT_ROWS, d), lambda i, be, nu: (i, 0)),
            scratch_shapes=[pltpu.VMEM((d, de2), BF16), pltpu.VMEM((de, d), BF16)]),
        out_shape=jax.ShapeDtypeStruct((n_slots, d), BF16),
        compiler_params=_params(("arbitrary",)),
    )(blk_expert, n_used, xb, w_gu, w_down)


def _final_kernel(x1_ref, h2_ref, yg_ref, wt_ref, mod_ref, wsg_ref, wsd_ref, fg_ref, o_ref):
    d = x1_ref.shape[1]
    ds_ = wsd_ref.shape[0]
    wt = wt_ref[...]
    routed = jnp.zeros(x1_ref.shape, F32)
    for k in range(TOP_K):
        routed = routed + wt[:, k:k + 1] * yg_ref[:, k * d:(k + 1) * d].astype(F32)
    gu = _dot(h2_ref[...], wsg_ref[...])
    g = gu[:, 0:ds_]
    sh = _dot(((g * _sigmoid(g)) * gu[:, ds_:2 * ds_]).astype(BF16), wsd_ref[...])
    x2 = x1_ref[...] + mod_ref[0, :, 5 * d:6 * d] * (routed + sh)
    o_ref[...] = x2 * lax.rsqrt(jnp.mean(x2 * x2, axis=-1, keepdims=True) + EPS) * fg_ref[...]


def _final(x1, h2, yg, wts, mod3, rows_per_mod, w_sh_gu, w_sh_down, final_g, tm=256):
    t, d = x1.shape
    row = lambda i: (i, 0)
    const = lambda i: (0, 0)
    return pl.pallas_call(
        _final_kernel,
        grid=(t // tm,),
        in_specs=[pl.BlockSpec((tm, d), row), pl.BlockSpec((tm, d), row),
                  pl.BlockSpec((tm, TOP_K * d), row), pl.BlockSpec((tm, TOP_K), row),
                  pl.BlockSpec((1, 1, mod3.shape[2]), lambda i: ((i * tm) // rows_per_mod, 0, 0)),
                  pl.BlockSpec(w_sh_gu.shape, const), pl.BlockSpec(w_sh_down.shape, const),
                  pl.BlockSpec((1, d), const)],
        out_specs=pl.BlockSpec((tm, d), row),
        out_shape=jax.ShapeDtypeStruct((t, d), F32),
        compiler_params=_params(("arbitrary",)),
    )(x1, h2, yg, wts, mod3, w_sh_gu, w_sh_down, final_g.reshape(1, d))


def _route(scores, router_bias):
    t = scores.shape[0]
    sel = scores + router_bias.astype(F32)
    grp_score = jnp.sum(lax.top_k(sel.reshape(t, N_GROUPS, -1), 2)[0], axis=-1)
    top_grp = lax.top_k(grp_score, TOPK_GROUPS)[1]
    grp_keep = jnp.any(top_grp[..., None] == jnp.arange(N_GROUPS), axis=1)
    sel = jnp.where(jnp.repeat(grp_keep, N_EXPERTS // N_GROUPS, axis=1), sel, -jnp.inf)
    expert_idx = lax.top_k(sel, TOP_K)[1]
    wts = jnp.take_along_axis(scores, expert_idx, axis=1)
    wts = wts / jnp.sum(wts, axis=-1, keepdims=True) * ROUTED_SCALE
    n_assign = t * TOP_K
    n_blocks = (n_assign + N_EXPERTS * (EXPERT_ROWS - 1)) // EXPERT_ROWS + 1
    n_slots = n_blocks * EXPERT_ROWS
    e_flat = expert_idx.reshape(-1).astype(jnp.int32)
    order = jnp.argsort(e_flat)
    e_sorted = e_flat[order]
    counts = jnp.bincount(e_flat, length=N_EXPERTS)
    padded = (counts + EXPERT_ROWS - 1) // EXPERT_ROWS * EXPERT_ROWS
    start = jnp.cumsum(counts) - counts
    pend = jnp.cumsum(padded)
    pstart = pend - padded
    dest = (pstart[e_sorted] + jnp.arange(n_assign) - start[e_sorted]).astype(jnp.int32)
    tok_slot = jnp.zeros((n_slots,), jnp.int32).at[dest].set((order // TOP_K).astype(jnp.int32))
    slot_of = jnp.zeros((n_assign,), jnp.int32).at[order].set(dest)
    blk_expert = jnp.minimum(jnp.searchsorted(pend, jnp.arange(n_blocks) * EXPERT_ROWS, side='right'),
                             N_EXPERTS - 1).astype(jnp.int32)
    n_used = (pend[-1] // EXPERT_ROWS).astype(jnp.int32).reshape(1)
    return wts, tok_slot, slot_of, blk_expert, n_used


def _col_major(t):
    b, l, f = t.shape
    rows = l // GRID_W
    return t.reshape(b, rows, GRID_W, f).transpose(0, 2, 1, 3).reshape(b, l, f)


def _row_major(t):
    b, l, f = t.shape
    rows = l // GRID_W
    return t.reshape(b, GRID_W, rows, f).transpose(0, 2, 1, 3).reshape(b, l, f)


def _per_head(vals_c, cf_c, cb_c, vals_l, cf_l, cb_l, cols, b):
    def pick(src_c, src_l, off):
        full = jnp.concatenate([src_c.reshape(b, -1, LANES), src_l.reshape(b, -1, LANES)], axis=1)
        return full[:, :, off:off + HEADS]

    srcs = dict(v=(vals_c, vals_l), f=(cf_c, cf_l), b=(cb_c, cb_l))
    picked = [pick(*srcs[name], off) for name, off in cols]
    tab = jnp.stack(picked + [jnp.zeros_like(picked[0])] * 4, axis=-1)
    col = tab.transpose(0, 2, 1, 3)
    lt = col.shape[2]
    row = col.reshape(b, HEADS, lt // CHUNK, CHUNK, 8).transpose(0, 1, 2, 4, 3)
    return col, row


def kernel(x, c, ctx, c_ctx, w_ada, b_ada, norm1_g, norm2_g, w_in, gdn_conv_w, gdn_a_log, gdn_dt_bias, gdn_norm_g,
           ml_i_bias, ml_f_bias, ml_norm_g, w_branch_gdn, w_branch_ml, w_out, w_router, router_bias, w_exp_gate_up,
           w_exp_down, w_sh_gate_up, w_sh_down, final_norm_g):
    b, l, d = x.shape
    lc = ctx.shape[1]
    t = b * l
    layer = 0

    w = w_in[layer]
    main_cols = [_ORIG[k] for k in ("gdn_qkv", "gdn_z", "ml_q", "ml_k", "ml_v", "ml_o", "mg_gdn", "mg_ml")]
    w_main = jnp.concatenate([w[:, a:e] for a, e in main_cols], axis=1).astype(BF16)
    w_gate = jnp.concatenate([w[:, _ORIG["gdn_gate"][0]:_ORIG["gdn_gate"][1]],
                              w[:, _ORIG["ml_gate"][0]:_ORIG["ml_gate"][1]],
                              jnp.zeros((d, LANES - 64), F32)], axis=1).astype(BF16)
    zeros16 = jnp.zeros((16,), F32)
    gp_add = jnp.concatenate([zeros16, gdn_dt_bias[layer].reshape(-1), ml_i_bias[layer].reshape(-1),
                              ml_f_bias[layer].reshape(-1), jnp.zeros((LANES - 64,), F32)])
    gp_mul = jnp.concatenate([zeros16, -jnp.exp(gdn_a_log[layer].astype(F32)).reshape(-1),
                              jnp.zeros((LANES - 32,), F32)])
    gparams = jnp.zeros((8, LANES), F32).at[0].set(gp_add).at[1].set(gp_mul)
    conv_w8 = jnp.zeros((8, gdn_conv_w.shape[2]), F32).at[0:GDN_CONV].set(gdn_conv_w[layer])
    wr = w_router[layer]
    wr_hi = wr.astype(BF16)
    wr_lo = (wr - wr_hi.astype(F32)).astype(BF16)

    n_mod_rows = -(-(b + 1) // 8) * 8
    cc = jnp.zeros((n_mod_rows, d), F32).at[0:b].set(c).at[b].set(c_ctx)
    mod = _ada_mod(cc, w_ada[layer], b_ada[layer])
    mod3 = mod.reshape(n_mod_rows, 1, 6 * d)

    x2d = x.reshape(t, d)
    tm_l = min(1024, l)
    proj_l, gate_l = _project(x2d, mod3, lambda i: (i * tm_l) // l, norm1_g[layer], w_main, w_gate, tm_l)
    tm_c = min(1024, b * lc)
    proj_c, gate_c = _project(ctx.reshape(b * lc, d), mod3, lambda i: b, norm1_g[layer], w_main, w_gate, tm_c)
    proj_l3 = proj_l.reshape(b, l, N_MAIN)
    proj_c3 = proj_c.reshape(b, lc, N_MAIN)

    gate_l_cm = _col_major(gate_l.reshape(b, l, LANES)).reshape(t, LANES)
    vals_c, cf_c, cb_c = _gate_prep(gate_c, gparams)
    vals_l, cf_l, cb_l = _gate_prep(gate_l, gparams)
    vals_m, cf_m, cb_m = _gate_prep(gate_l_cm, gparams)
    gcol, grow = _per_head(vals_c, cf_c, cb_c, vals_l, cf_l, cb_l,
                           (("v", 0), ("v", 8), ("f", 16), ("b", 24)), b)
    mcol, mrow = _per_head(vals_c, cf_c, cb_c, vals_m, cf_m, cb_m,
                           (("v", 32), ("v", 40), ("f", 48), ("b", 56)), b)

    y_gdn = _gdn(proj_c3, proj_l3, conv_w8, gcol, grow, gdn_norm_g[layer])
    q_cm = _col_major(proj_l3[:, :, COL_ML_Q:COL_ML_Q + HEADS * ML_DK])
    k_cm = _col_major(proj_l3[:, :, COL_ML_K:COL_ML_K + HEADS * ML_DK])
    v_cm = _col_major(proj_l3[:, :, COL_ML_V:COL_ML_V + HEADS * HEAD_V])
    h_ml = _row_major(_mlstm(proj_c3, q_cm, k_cm, v_cm, mcol, mrow))

    x1, h2, scores = _merge(y_gdn.reshape(t, d), h_ml.reshape(t, d), proj_l, x2d, mod3, l, ml_norm_g[layer],
                            norm2_g[layer], w_branch_gdn[layer].astype(BF16), w_branch_ml[layer].astype(BF16),
                            w_out[layer].astype(BF16), wr_hi, wr_lo, tm=min(512, l))

    wts, tok_slot, slot_of, blk_expert, n_used = _route(scores, router_bias[layer])
    xb = h2[tok_slot]
    yb = _experts(xb, blk_expert, n_used, w_exp_gate_up[layer], w_exp_down[layer])
    yg = yb[slot_of].reshape(t, TOP_K * d)
    out = _final(x1, h2, yg, wts, mod3, l, w_sh_gate_up[layer].astype(BF16), w_sh_down[layer].astype(BF16),
                 final_norm_g, tm=min(256, l))
    return out.reshape(b, l, d)
```

```python
import functools
import math

import jax
import jax.numpy as jnp
from jax import lax
from jax.experimental import pallas as pl
from jax.experimental.pallas import tpu as pltpu

F32 = jnp.float32
BF16 = jnp.bfloat16
HI = lax.Precision.HIGHEST

EPS = 1e-6
CHUNK = 64
GRID_W = 64
HEADS = 8
HEAD_V = 128
GDN_DK = 128
ML_DK = 64
GDN_CONV = 5
N_EXPERTS = 256
TOP_K = 8
N_GROUPS = 8
TOPK_GROUPS = 4
ROUTED_SCALE = 2.5
EXPERT_ROWS = 256
RING = 4
LANES = 128
VMEM_LIMIT = 56 * 1024 * 1024

COL_GDN_QKV = 0
COL_GDN_Z = 3072
COL_ML_Q = 4096
COL_ML_K = 4608
COL_ML_V = 5120
COL_ML_O = 6144
COL_MG_GDN = 7168
COL_MG_ML = 8192
N_MAIN = 9216
_ORIG = dict(gdn_qkv=(0, 3072), gdn_z=(3072, 4096), gdn_gate=(4096, 4128), ml_q=(4128, 4640),
             ml_k=(4640, 5152), ml_v=(5152, 6176), ml_o=(6176, 7200), ml_gate=(7200, 7232),
             mg_gdn=(7232, 8256), mg_ml=(8256, 9280))


def _params(sem, vmem=VMEM_LIMIT):
    return pltpu.CompilerParams(dimension_semantics=sem, vmem_limit_bytes=vmem)


def _dot(a, b, precision=None):
    return jnp.dot(a, b, preferred_element_type=F32, precision=precision)


def _dot_nt(a, b, precision=None):
    return lax.dot_general(a, b, (((1,), (1,)), ((), ())), preferred_element_type=F32, precision=precision)


def _dot_tn(a, b, precision=None):
    return lax.dot_general(a, b, (((0,), (0,)), ((), ())), preferred_element_type=F32, precision=precision)


def _split(a):
    ah = a.astype(BF16)
    return ah, (a - ah.astype(F32)).astype(BF16)


def _dot3(a, b):
    (ah, al), (bh, bl) = a, b
    return _dot(ah, bh) + (_dot(ah, bl) + _dot(al, bh))


def _sigmoid(x):
    return 1.0 / (1.0 + jnp.exp(-x))


def _softplus(x):
    return jnp.maximum(x, 0.0) + jnp.log(1.0 + jnp.exp(-jnp.abs(x)))


def _ada_kernel(c_ref, w_ref, b_ref, o_ref):
    c = c_ref[...]
    sc = c * _sigmoid(c)
    o_ref[...] = _dot(sc, w_ref[...], HI) + b_ref[...]


def _ada_mod(cc, w_ada, b_ada, tn=1536):
    r, d = cc.shape
    n = w_ada.shape[1]
    return pl.pallas_call(
        _ada_kernel,
        grid=(n // tn,),
        in_specs=[pl.BlockSpec((r, d), lambda j: (0, 0)),
                  pl.BlockSpec((d, tn), lambda j: (0, j)),
                  pl.BlockSpec((1, tn), lambda j: (0, j))],
        out_specs=pl.BlockSpec((r, tn), lambda j: (0, j)),
        out_shape=jax.ShapeDtypeStruct((r, n), F32),
        compiler_params=_params(("arbitrary",)),
    )(cc, w_ada, b_ada.reshape(1, n))


def _proj_kernel(x_ref, mod_ref, g_ref, w_ref, wg_ref, o_ref, og_ref, hn_ref):
    d = x_ref.shape[1]

    @pl.when(pl.program_id(1) == 0)
    def _():
        x = x_ref[...]
        y = x * lax.rsqrt(jnp.mean(x * x, axis=-1, keepdims=True) + EPS) * g_ref[...]
        shift = mod_ref[0, :, 0:d]
        scale = mod_ref[0, :, d:2 * d]
        h = (y * (1.0 + scale) + shift).astype(BF16)
        hn_ref[...] = h
        og_ref[...] = _dot(h, wg_ref[...])

    o_ref[...] = _dot(hn_ref[...], w_ref[...]).astype(o_ref.dtype)


def _project(x2d, mod3, mod_row_of_tile, norm_g, w_main, w_gate, tm, tn=1024):
    t, d = x2d.shape
    n = w_main.shape[1]
    return pl.pallas_call(
        _proj_kernel,
        grid=(t // tm, n // tn),
        in_specs=[pl.BlockSpec((tm, d), lambda i, j: (i, 0)),
                  pl.BlockSpec((1, 1, mod3.shape[2]), lambda i, j: (mod_row_of_tile(i), 0, 0)),
                  pl.BlockSpec((1, d), lambda i, j: (0, 0)),
                  pl.BlockSpec((d, tn), lambda i, j: (0, j)),
                  pl.BlockSpec((d, LANES), lambda i, j: (0, 0))],
        out_specs=[pl.BlockSpec((tm, tn), lambda i, j: (i, j)),
                   pl.BlockSpec((tm, LANES), lambda i, j: (i, 0))],
        out_shape=[jax.ShapeDtypeStruct((t, n), BF16), jax.ShapeDtypeStruct((t, LANES), F32)],
        scratch_shapes=[pltpu.VMEM((tm, d), BF16)],
        compiler_params=_params(("arbitrary", "arbitrary")),
    )(x2d, mod3, norm_g.reshape(1, d), w_main, w_gate)


def _gate_kernel(g_ref, p_ref, v_ref, cf_ref, cb_ref):
    rows = g_ref.shape[0]
    raw = g_ref[...] + p_ref[0:1, :]
    lane = lax.broadcasted_iota(jnp.int32, raw.shape, 1)
    sp = _softplus(raw)
    vals = jnp.where(lane < 16, _sigmoid(raw),
                     jnp.where(lane < 32, p_ref[1:2, :] * sp,
                               jnp.where(lane < 48, raw,
                                         jnp.where(lane < 64, -_softplus(-raw), 0.0))))
    v_ref[...] = vals
    ri = lax.broadcasted_iota(jnp.int32, (CHUNK, CHUNK), 0)
    ci = lax.broadcasted_iota(jnp.int32, (CHUNK, CHUNK), 1)
    tri_f = (ri >= ci).astype(F32)
    tri_b = (ri <= ci).astype(F32)
    for c in range(rows // CHUNK):
        blk = vals[c * CHUNK:(c + 1) * CHUNK, :]
        cf_ref[c * CHUNK:(c + 1) * CHUNK, :] = _dot(tri_f, blk, HI)
        cb_ref[c * CHUNK:(c + 1) * CHUNK, :] = _dot(tri_b, blk, HI)


def _gate_prep(graw, gparams, tm=256):
    t = graw.shape[0]
    spec = pl.BlockSpec((tm, LANES), lambda i: (i, 0))
    return pl.pallas_call(
        _gate_kernel,
        grid=(t // tm,),
        in_specs=[spec, pl.BlockSpec((8, LANES), lambda i: (0, 0))],
        out_specs=[spec, spec, spec],
        out_shape=[jax.ShapeDtypeStruct((t, LANES), F32)] * 3,
        compiler_params=_params(("arbitrary",)),
    )(graw, gparams)


def _dir_masks(direction):
    ri = lax.broadcasted_iota(jnp.int32, (CHUNK, CHUNK), 0)
    ci = lax.broadcasted_iota(jnp.int32, (CHUNK, CHUNK), 1)
    if direction == 0:
        return ri >= ci, ri > ci
    return ri <= ci, ri < ci


def _gdn_kernel(qc_ref, kc_ref, vc_ref, ql_ref, kl_ref, vl_ref, z_ref, cwq_ref, cwk_ref, cwv_ref,
                gc_ref, gr_ref, ng_ref, y_ref,
                xpad, qs, ks, vs, wq_r, u_r, kd_r, qk_r, dc_r, out_s):
    lc = qc_ref.shape[1]
    ll = ql_ref.shape[1]
    lt = lc + ll
    n_c, n_l = lc // CHUNK, ll // CHUNK
    n_t = n_c + n_l
    rb = 256

    def l2n(x):
        return x * lax.rsqrt(jnp.sum(x * x, axis=-1, keepdims=True) + EPS)

    def prep(src_ref, cw_ref, dst, off, ls, kind):
        xpad[0:8, :] = jnp.zeros((8, LANES), F32)
        xpad[8:8 + ls, :] = src_ref[0].astype(F32)
        xpad[8 + ls:16 + ls, :] = jnp.zeros((8, LANES), F32)
        step = min(rb, ls)
        for r0 in range(0, ls, step):
            acc = jnp.zeros((step, LANES), F32)
            for t in range(GDN_CONV):
                s0 = r0 + 8 - GDN_CONV // 2 + t
                acc = acc + cw_ref[t:t + 1, :] * xpad[s0:s0 + step, :]
            y = acc * _sigmoid(acc)
            if kind == "q":
                y = l2n(y) * (GDN_DK ** -0.5)
            elif kind == "k":
                y = l2n(y)
            dst[off + r0:off + r0 + step, :] = y

    prep(qc_ref, cwq_ref, qs, 0, lc, "q")
    prep(kc_ref, cwk_ref, ks, 0, lc, "k")
    prep(vc_ref, cwv_ref, vs, 0, lc, "v")
    prep(ql_ref, cwq_ref, qs, lc, ll, "q")
    prep(kl_ref, cwk_ref, ks, lc, ll, "k")
    prep(vl_ref, cwv_ref, vs, lc, ll, "v")

    eye = (lax.broadcasted_iota(jnp.int32, (CHUNK, CHUNK), 0)
           == lax.broadcasted_iota(jnp.int32, (CHUNK, CHUNK), 1)).astype(F32)

    masks = (_dir_masks(0), _dir_masks(1))

    def bwd_chunk(t):
        return jnp.where(t < n_c, n_c - 1 - t, n_t + n_c - 1 - t)

    def prep_chain(t, d):
        incl, strict = masks[d]
        last = CHUNK - 1 if d == 0 else 0
        tc = jnp.minimum(t, n_t - 1)
        c = tc if d == 0 else bwd_chunk(tc)
        r0 = pl.multiple_of(c * CHUNK, CHUNK)
        q = qs[pl.ds(r0, CHUNK), :]
        k = ks[pl.ds(r0, CHUNK), :]
        v = vs[pl.ds(r0, CHUNK), :]
        gcol = gc_ref[0, 0, pl.ds(r0, CHUNK), :]
        grow = gr_ref[0, 0, c]
        beta = gcol[:, d:d + 1]
        cgc = gcol[:, 2 + d:3 + d]
        cgr = grow[2 + d:3 + d, :]
        cg_last = cgc[last:last + 1, :]
        decay = jnp.exp(jnp.where(incl, cgc - cgr, -jnp.inf))
        ecg = jnp.exp(cgc)
        kb = k * beta
        slot = (t % RING) * 2 + d
        kk = _dot_nt(kb, k)
        qk_r[slot] = _dot_nt(q, k) * decay
        wq_r[slot, CHUNK:2 * CHUNK, :] = q * ecg
        kd_r[slot] = k * jnp.exp(cg_last - cgc)
        dc_r[slot] = jnp.broadcast_to(jnp.exp(cg_last), (8, LANES))
        yield
        x = jnp.where(strict, -kk * decay, 0.0)
        tinv = eye + x
        xs = _split(x)
        x = _dot3(xs, xs)
        yield
        for _ in range(4):
            xs = _split(x)
            tinv, x = tinv + _dot3(_split(tinv), xs), _dot3(xs, xs)
            yield
        tinv = tinv + _dot3(_split(tinv), _split(x))
        yield
        wq_r[slot, 0:CHUNK, :] = _dot(tinv, kb * ecg)
        u_r[slot] = _dot(tinv, v * beta)

    out_s[...] = jnp.zeros(out_s.shape, F32)

    def scan_chain(d, t0, s, with_out, result):
        for j in range(2):
            t = t0 + j
            slot = (t % RING) * 2 + d
            ws = _dot(wq_r[slot], s)
            yield
            v_new = u_r[slot] - ws[0:CHUNK, :]
            if with_out:
                c = t if d == 0 else bwd_chunk(t)
                o = ws[CHUNK:2 * CHUNK, :] + _dot(qk_r[slot], v_new)
                l0 = pl.multiple_of((c - n_c) * CHUNK, CHUNK)
                out_s[pl.ds(l0, CHUNK), :] += o
            s = s * dc_r[slot][0:1, :] + _dot_tn(kd_r[slot], v_new)
            yield
        result[d] = s

    def lockstep(chains):
        chains = list(chains)
        while chains:
            alive = []
            for ch in chains:
                try:
                    next(ch)
                    alive.append(ch)
                except StopIteration:
                    pass
            chains = alive

    def pair_body(i, carry, t_base, with_out):
        t0 = t_base + 2 * i
        result = [None, None]
        lockstep([scan_chain(0, t0, carry[0], with_out, result), scan_chain(1, t0, carry[1], with_out, result)]
                 + [prep_chain(t0 + 2 + j, d) for j in range(2) for d in range(2)])
        return result[0], result[1]

    lockstep([prep_chain(j, d) for j in range(2) for d in range(2)])
    s0 = jnp.zeros((GDN_DK, HEAD_V), F32)
    carry = lax.fori_loop(0, n_c // 2, functools.partial(pair_body, t_base=0, with_out=False), (s0, s0))
    lax.fori_loop(0, n_l // 2, functools.partial(pair_body, t_base=n_c, with_out=True), carry)


    def out_body(i, carry):
        r0 = pl.multiple_of(i * rb, rb)
        o = out_s[pl.ds(r0, rb), :]
        z = z_ref[0, pl.ds(r0, rb), :].astype(F32)
        y = o * lax.rsqrt(jnp.mean(o * o, axis=-1, keepdims=True) + EPS) * ng_ref[...]
        y_ref[0, pl.ds(r0, rb), :] = (y * (z * _sigmoid(z))).astype(y_ref.dtype)
        return carry

    lax.fori_loop(0, ll // rb, out_body, 0)


def _gdn(proj_c, proj_l, conv_w8, gcol, grow, norm_g):
    b, lc, _ = proj_c.shape
    ll = proj_l.shape[1]
    lt = lc + ll
    n_t = lt // CHUNK
    qb, kb_, vb, zb = (COL_GDN_QKV // LANES, COL_GDN_QKV // LANES + HEADS, COL_GDN_QKV // LANES + 2 * HEADS,
                       COL_GDN_Z // LANES)

    def seq_spec(l, col0):
        return pl.BlockSpec((1, l, LANES), lambda i, h: (i, 0, col0 + h))

    def cw_spec(col0):
        return pl.BlockSpec((8, LANES), lambda i, h: (0, col0 + h))

    return pl.pallas_call(
        _gdn_kernel,
        grid=(b, HEADS),
        in_specs=[seq_spec(lc, qb), seq_spec(lc, kb_), seq_spec(lc, vb),
                  seq_spec(ll, qb), seq_spec(ll, kb_), seq_spec(ll, vb), seq_spec(ll, zb),
                  cw_spec(0), cw_spec(HEADS), cw_spec(2 * HEADS),
                  pl.BlockSpec((1, 1, lt, 8), lambda i, h: (i, h, 0, 0)),
                  pl.BlockSpec((1, 1, n_t, 8, CHUNK), lambda i, h: (i, h, 0, 0, 0)),
                  pl.BlockSpec((1, LANES), lambda i, h: (0, 0))],
        out_specs=pl.BlockSpec((1, ll, LANES), lambda i, h: (i, 0, h)),
        out_shape=jax.ShapeDtypeStruct((b, ll, HEADS * HEAD_V), BF16),
        scratch_shapes=[pltpu.VMEM((max(lc, ll) + 16, LANES), F32),
                        pltpu.VMEM((lt, LANES), F32), pltpu.VMEM((lt, LANES), F32), pltpu.VMEM((lt, LANES), F32),
                        pltpu.VMEM((2 * RING, 2 * CHUNK, LANES), F32),
                        pltpu.VMEM((2 * RING, CHUNK, LANES), F32), pltpu.VMEM((2 * RING, CHUNK, LANES), F32),
                        pltpu.VMEM((2 * RING, CHUNK, CHUNK), F32),
                        pltpu.VMEM((2 * RING, 8, LANES), F32),
                        pltpu.VMEM((ll, LANES), F32)],
        compiler_params=_params(("arbitrary", "arbitrary")),
    )(proj_c, proj_c, proj_c, proj_l, proj_l, proj_l, proj_l, conv_w8, conv_w8, conv_w8,
      gcol, grow, norm_g.reshape(1, LANES))


def _mlstm_kernel(qc_ref, kc_ref, vc_ref, ql_ref, kl_ref, vl_ref, mc_ref, mr_ref, h_ref, out_s):
    lc = qc_ref.shape[1]
    ll = ql_ref.shape[1]
    n_c, n_l = lc // CHUNK, ll // CHUNK
    n_t = n_c + n_l
    head = pl.program_id(1)
    lane = lax.broadcasted_iota(jnp.int32, (CHUNK, LANES), 1)
    hmask = ((lane // ML_DK) == (head % 2)).astype(F32)

    def load(ref_c, ref_l, c):
        def from_c():
            return ref_c[0, pl.ds(pl.multiple_of(c * CHUNK, CHUNK), CHUNK), :].astype(F32)

        def from_l():
            return ref_l[0, pl.ds(pl.multiple_of((c - n_c) * CHUNK, CHUNK), CHUNK), :].astype(F32)

        return from_c, from_l

    def step(d, c, state, is_ctx):
        cs, ns, ms = state
        incl, _ = _dir_masks(d)
        last = CHUNK - 1 if d == 0 else 0
        sel = 0 if is_ctx else 1
        q = load(qc_ref, ql_ref, c)[sel]() * hmask
        k = load(kc_ref, kl_ref, c)[sel]() * (hmask * (ML_DK ** -0.5))
        v = load(vc_ref, vl_ref, c)[sel]()
        r0 = pl.multiple_of(c * CHUNK, CHUNK)
        mcol = mc_ref[0, 0, pl.ds(r0, CHUNK), :]
        mrow = mr_ref[0, 0, c]
        igc = mcol[:, d:d + 1]
        bc = mcol[:, 2 + d:3 + d]
        igr = mrow[d:d + 1, :]
        br = mrow[2 + d:3 + d, :]
        b_last = bc[last:last + 1, :]
        log_d = jnp.where(incl, bc - br + igr, -jnp.inf)
        log_inter = bc + ms
        m_t = jnp.maximum(log_inter, jnp.max(log_d, axis=1, keepdims=True))
        s_inter = jnp.exp(log_inter - m_t)
        p = jnp.exp(log_d - m_t) * _dot_nt(q, k)
        num = s_inter * _dot(q, cs) + _dot(p, v)
        den = s_inter * jnp.sum(q * ns, axis=1, keepdims=True) + jnp.sum(p, axis=1, keepdims=True)
        if not is_ctx:
            hh = num / jnp.maximum(jnp.abs(den), jnp.exp(-m_t))
            l0 = pl.multiple_of((c - n_c) * CHUNK, CHUNK)
            out_s[pl.ds(l0, CHUNK), :] += hh
        a_log = b_last - bc + igc
        m_loc = jnp.max(a_log, axis=0, keepdims=True)
        wk = k * jnp.exp(a_log - m_loc)
        c_loc = _dot_tn(wk, v)
        n_loc = jnp.sum(wk, axis=0, keepdims=True)
        m_new = jnp.maximum(b_last + ms, m_loc)
        s_prev = jnp.exp(b_last + ms - m_new)
        s_loc = jnp.exp(m_loc - m_new)
        return s_prev * cs + s_loc * c_loc, s_prev * ns + s_loc * n_loc, m_new

    out_s[...] = jnp.zeros(out_s.shape, F32)

    def ctx_body(i, carry):
        sf, sb = carry
        return step(0, i, sf, True), step(1, n_c - 1 - i, sb, True)

    def lat_body(i, carry):
        sf, sb = carry
        return step(0, n_c + i, sf, False), step(1, n_t - 1 - i, sb, False)

    st0 = (jnp.zeros((LANES, HEAD_V), F32), jnp.zeros((1, LANES), F32), jnp.zeros((1, 1), F32))
    carry = lax.fori_loop(0, n_c, ctx_body, (st0, st0))
    lax.fori_loop(0, n_l, lat_body, carry)
    h_ref[0] = out_s[...].astype(h_ref.dtype)


def _mlstm(proj_c, q_l, k_l, v_l, mcol, mrow):
    b, lc, _ = proj_c.shape
    ll = q_l.shape[1]
    lt = lc + ll
    n_t = lt // CHUNK
    qb, kb_, vb = COL_ML_Q // LANES, COL_ML_K // LANES, COL_ML_V // LANES
    return pl.pallas_call(
        _mlstm_kernel,
        grid=(b, HEADS),
        in_specs=[pl.BlockSpec((1, lc, LANES), lambda i, h: (i, 0, qb + h // 2)),
                  pl.BlockSpec((1, lc, LANES), lambda i, h: (i, 0, kb_ + h // 2)),
                  pl.BlockSpec((1, lc, LANES), lambda i, h: (i, 0, vb + h)),
                  pl.BlockSpec((1, ll, LANES), lambda i, h: (i, 0, h // 2)),
                  pl.BlockSpec((1, ll, LANES), lambda i, h: (i, 0, h // 2)),
                  pl.BlockSpec((1, ll, LANES), lambda i, h: (i, 0, h)),
                  pl.BlockSpec((1, 1, lt, 8), lambda i, h: (i, h, 0, 0)),
                  pl.BlockSpec((1, 1, n_t, 8, CHUNK), lambda i, h: (i, h, 0, 0, 0))],
        out_specs=pl.BlockSpec((1, ll, LANES), lambda i, h: (i, 0, h)),
        out_shape=jax.ShapeDtypeStruct((b, ll, HEADS * HEAD_V), BF16),
        scratch_shapes=[pltpu.VMEM((ll, LANES), F32)],
        compiler_params=_params(("arbitrary", "arbitrary")),
    )(proj_c, proj_c, proj_c, q_l, k_l, v_l, mcol, mrow)


def _merge_kernel(yg_ref, hm_ref, o_ref, gg_ref, gm_ref, x_ref, mod_ref, mlg_ref, n2_ref,
                  wbg_ref, wbm_ref, wo_ref, wrh_ref, wrl_ref, x1_ref, h2_ref, sc_ref):
    d = x_ref.shape[1]
    o = o_ref[...].astype(F32)
    ym = _sigmoid(o) * hm_ref[...].astype(F32)
    segs = []
    for h in range(HEADS):
        seg = ym[:, h * HEAD_V:(h + 1) * HEAD_V]
        segs.append(seg * lax.rsqrt(jnp.mean(seg * seg, axis=-1, keepdims=True) + EPS))
    ymn = jnp.concatenate(segs, axis=1) * mlg_ref[...]
    y_gdn = _dot(yg_ref[...], wbg_ref[...])
    y_ml = _dot(ymn.astype(BF16), wbm_ref[...])
    mixed = _sigmoid(gg_ref[...].astype(F32)) * y_gdn + _sigmoid(gm_ref[...].astype(F32)) * y_ml
    y = _dot(mixed.astype(BF16), wo_ref[...])
    x1 = x_ref[...] + mod_ref[0, :, 2 * d:3 * d] * y
    x1_ref[...] = x1
    hn = x1 * lax.rsqrt(jnp.mean(x1 * x1, axis=-1, keepdims=True) + EPS) * n2_ref[...]
    h2 = hn * (1.0 + mod_ref[0, :, 4 * d:5 * d]) + mod_ref[0, :, 3 * d:4 * d]
    h2_hi = h2.astype(BF16)
    h2_ref[...] = h2_hi
    h2_lo = (h2 - h2_hi.astype(F32)).astype(BF16)
    logits = _dot_nt(wrh_ref[...], h2_hi) + (_dot_nt(wrl_ref[...], h2_hi) + _dot_nt(wrh_ref[...], h2_lo))
    sc_ref[...] = _sigmoid(logits)


def _merge(y_gdn, h_ml, proj_l2d, x2d, mod3, rows_per_mod, ml_norm_g, norm2_g, wbg, wbm, wo, wr_hi, wr_lo, tm=512):
    t, d = x2d.shape
    e = wr_hi.shape[0]
    row = lambda i: (i, 0)
    const = lambda i: (0, 0)
    return pl.pallas_call(
        _merge_kernel,
        grid=(t // tm,),
        in_specs=[pl.BlockSpec((tm, d), row), pl.BlockSpec((tm, d), row),
                  pl.BlockSpec((tm, d), lambda i: (i, COL_ML_O // d)),
                  pl.BlockSpec((tm, d), lambda i: (i, COL_MG_GDN // d)),
                  pl.BlockSpec((tm, d), lambda i: (i, COL_MG_ML // d)),
                  pl.BlockSpec((tm, d), row),
                  pl.BlockSpec((1, 1, mod3.shape[2]), lambda i: ((i * tm) // rows_per_mod, 0, 0)),
                  pl.BlockSpec((1, d), const), pl.BlockSpec((1, d), const),
                  pl.BlockSpec((d, d), const), pl.BlockSpec((d, d), const), pl.BlockSpec((d, d), const),
                  pl.BlockSpec((e, d), const), pl.BlockSpec((e, d), const)],
        out_specs=[pl.BlockSpec((tm, d), row), pl.BlockSpec((tm, d), row), pl.BlockSpec((e, tm), lambda i: (0, i))],
        out_shape=[jax.ShapeDtypeStruct((t, d), F32), jax.ShapeDtypeStruct((t, d), BF16),
                   jax.ShapeDtypeStruct((e, t), F32)],
        compiler_params=_params(("arbitrary",)),
    )(y_gdn, h_ml, proj_l2d, proj_l2d, proj_l2d, x2d, mod3, ml_norm_g.reshape(1, d), norm2_g.reshape(1, d),
      wbg, wbm, wo, wr_hi, wr_lo)


def _expert_kernel(be_ref, nu_ref, x_ref, wgu_ref, wd_ref, y_ref, wgu_s, wd_s):
    i = pl.program_id(0)
    de = wd_ref.shape[1]
    changed = jnp.logical_or(i == 0, be_ref[i] != be_ref[jnp.maximum(i - 1, 0)])

    @pl.when(changed)
    def _():
        wgu_s[...] = wgu_ref[0].astype(BF16)
        wd_s[...] = wd_ref[0].astype(BF16)

    @pl.when(i < nu_ref[0])
    def _():
        gu = _dot(x_ref[...], wgu_s[...])
        g = gu[:, 0:de]
        act = (g * _sigmoid(g)) * gu[:, de:2 * de]
        y_ref[...] = _dot(act.astype(BF16), wd_s[...]).astype(y_ref.dtype)

    @pl.when(i >= nu_ref[0])
    def _():
        y_ref[...] = jnp.zeros(y_ref.shape, y_ref.dtype)


def _experts(xb, blk_expert, n_used, w_gu, w_down):
    n_slots, d = xb.shape
    n_blocks = n_slots // EXPERT_ROWS
    e, _, de2 = w_gu.shape
    de = de2 // 2
    return pl.pallas_call(
        _expert_kernel,
        grid_spec=pltpu.PrefetchScalarGridSpec(
            num_scalar_prefetch=2,
            grid=(n_blocks,),
            in_specs=[pl.BlockSpec((EXPERT_ROWS, d), lambda i, be, nu: (i, 0)),
                      pl.BlockSpec((1, d, de2), lambda i, be, nu: (be[i], 0, 0)),
                      pl.BlockSpec((1, de, d), lambda i, be, nu: (be[i], 0, 0))],
            out_specs=pl.BlockSpec((EXPERT_ROWS, d), lambda i, be, nu: (i, 0)),
            scratch_shapes=[pltpu.VMEM((d, de2), BF16), pltpu.VMEM((de, d), BF16)]),
        out_shape=jax.ShapeDtypeStruct((n_slots, d), BF16),
        compiler_params=_params(("arbitrary",)),
    )(blk_expert, n_used, xb, w_gu, w_down)


def _final_kernel(x1_ref, h2_ref, yg_ref, wt_ref, mod_ref, wsg_ref, wsd_ref, fg_ref, o_ref):
    d = x1_ref.shape[1]
    ds_ = wsd_ref.shape[0]
    wt = wt_ref[...]
    routed = jnp.zeros(x1_ref.shape, F32)
    for k in range(TOP_K):
        routed = routed + wt[:, k:k + 1] * yg_ref[:, k * d:(k + 1) * d].astype(F32)
    gu = _dot(h2_ref[...], wsg_ref[...])
    g = gu[:, 0:ds_]
    sh = _dot(((g * _sigmoid(g)) * gu[:, ds_:2 * ds_]).astype(BF16), wsd_ref[...])
    x2 = x1_ref[...] + mod_ref[0, :, 5 * d:6 * d] * (routed + sh)
    o_ref[...] = x2 * lax.rsqrt(jnp.mean(x2 * x2, axis=-1, keepdims=True) + EPS) * fg_ref[...]


def _final(x1, h2, yg, wts, mod3, rows_per_mod, w_sh_gu, w_sh_down, final_g, tm=256):
    t, d = x1.shape
    row = lambda i: (i, 0)
    const = lambda i: (0, 0)
    return pl.pallas_call(
        _final_kernel,
        grid=(t // tm,),
        in_specs=[pl.BlockSpec((tm, d), row), pl.BlockSpec((tm, d), row),
                  pl.BlockSpec((tm, TOP_K * d), row), pl.BlockSpec((tm, TOP_K), row),
                  pl.BlockSpec((1, 1, mod3.shape[2]), lambda i: ((i * tm) // rows_per_mod, 0, 0)),
                  pl.BlockSpec(w_sh_gu.shape, const), pl.BlockSpec(w_sh_down.shape, const),
                  pl.BlockSpec((1, d), const)],
        out_specs=pl.BlockSpec((tm, d), row),
        out_shape=jax.ShapeDtypeStruct((t, d), F32),
        compiler_params=_params(("arbitrary",)),
    )(x1, h2, yg, wts, mod3, w_sh_gu, w_sh_down, final_g.reshape(1, d))


def _route_kernel(sc_ref, bias_ref, tri_ref, idx_ref, wt_ref, rk_ref, cnt_ref, base_s):
    @pl.when(pl.program_id(0) == 0)
    def _():
        base_s[...] = jnp.zeros(base_s.shape, F32)

    scores = sc_ref[...]
    e, tn = scores.shape
    gsz = e // N_GROUPS
    sel3 = (scores + bias_ref[...]).reshape(N_GROUPS, gsz, tn)
    m1 = jnp.max(sel3, axis=1)
    is_max = sel3 == m1[:, None, :]
    n_max = jnp.sum(is_max.astype(F32), axis=1)
    m2 = jnp.max(jnp.where(is_max, -jnp.inf, sel3), axis=1)
    grp = m1 + jnp.where(n_max >= 2.0, m1, m2)
    gi = lax.broadcasted_iota(jnp.int32, (N_GROUPS, tn), 0)
    ahead = jnp.zeros((N_GROUPS, tn), F32)
    for g in range(N_GROUPS):
        row = grp[g:g + 1, :]
        ahead = ahead + jnp.logical_or(row > grp, jnp.logical_and(row == grp, g < gi)).astype(F32)
    ahead3 = jnp.broadcast_to(ahead[:, None, :], (N_GROUPS, gsz, tn))
    selm = jnp.where(ahead3 < float(TOPK_GROUPS), sel3, -jnp.inf).reshape(e, tn)
    ri = lax.broadcasted_iota(jnp.int32, (e, tn), 0).astype(F32)
    member = jnp.zeros((e, tn), F32)
    idxs, ws = [], []
    for _ in range(TOP_K):
        m = jnp.max(selm, axis=0, keepdims=True)
        idx = jnp.min(jnp.where(selm == m, ri, float(e)), axis=0, keepdims=True)
        hit = ri == idx
        ws.append(jnp.sum(jnp.where(hit, scores, 0.0), axis=0, keepdims=True))
        idxs.append(idx)
        selm = jnp.where(hit, -jnp.inf, selm)
        member = jnp.where(hit, 1.0, member)
    w = jnp.concatenate(ws, axis=0)
    wt_ref[...] = w / jnp.sum(w, axis=0, keepdims=True) * ROUTED_SCALE
    idx_ref[...] = jnp.concatenate(idxs, axis=0).astype(jnp.int32)
    cum = _dot(member.astype(BF16), tri_ref[...]) + base_s[...]
    rk_ref[...] = jnp.concatenate(
        [jnp.sum(jnp.where(ri == idx, cum, 0.0), axis=0, keepdims=True) for idx in idxs], axis=0).astype(jnp.int32)
    total = base_s[...] + jnp.sum(member, axis=1, keepdims=True)
    base_s[...] = total
    cnt_ref[...] = total


def _route(scores_t, router_bias):
    e, t = scores_t.shape
    tn = LANES
    bias = jnp.broadcast_to(router_bias.astype(F32)[:, None], (e, tn))
    tri = (jnp.arange(tn)[:, None] < jnp.arange(tn)[None, :]).astype(BF16)
    tok = pl.BlockSpec((TOP_K, tn), lambda i: (0, i))
    const = lambda i: (0, 0)
    return pl.pallas_call(
        _route_kernel,
        grid=(t // tn,),
        in_specs=[pl.BlockSpec((e, tn), lambda i: (0, i)), pl.BlockSpec((e, tn), const),
                  pl.BlockSpec((tn, tn), const)],
        out_specs=[tok, tok, tok, pl.BlockSpec((e, tn), const)],
        out_shape=[jax.ShapeDtypeStruct((TOP_K, t), jnp.int32), jax.ShapeDtypeStruct((TOP_K, t), F32),
                   jax.ShapeDtypeStruct((TOP_K, t), jnp.int32), jax.ShapeDtypeStruct((e, tn), F32)],
        scratch_shapes=[pltpu.VMEM((e, tn), F32)],
        compiler_params=_params(("arbitrary",)),
    )(scores_t, bias, tri)


def _slot_kernel(idx_ref, rk_ref, ps_ref, pos_ref):
    e, tn = ps_ref.shape
    ri = lax.broadcasted_iota(jnp.int32, (e, tn), 0)
    ps = ps_ref[...]
    rows = [jnp.sum(jnp.where(ri == idx_ref[k:k + 1, :], ps, 0.0), axis=0, keepdims=True) for k in range(TOP_K)]
    pos_ref[...] = rk_ref[...] + jnp.concatenate(rows, axis=0).astype(jnp.int32)


def _slots(idx, rank, pstart):
    k, t = idx.shape
    e = pstart.shape[0]
    tn = LANES
    tok = pl.BlockSpec((k, tn), lambda i: (0, i))
    return pl.pallas_call(
        _slot_kernel,
        grid=(t // tn,),
        in_specs=[tok, tok, pl.BlockSpec((e, tn), lambda i: (0, 0))],
        out_specs=tok,
        out_shape=jax.ShapeDtypeStruct((k, t), jnp.int32),
        compiler_params=_params(("arbitrary",)),
    )(idx, rank, jnp.broadcast_to(pstart.astype(F32)[:, None], (e, tn)))


def _block_table(counts, n_blocks):
    padded = (counts + EXPERT_ROWS - 1) // EXPERT_ROWS * EXPERT_ROWS
    pend = jnp.cumsum(padded)
    blk_expert = jnp.minimum(jnp.searchsorted(pend, jnp.arange(n_blocks) * EXPERT_ROWS, side='right'),
                             N_EXPERTS - 1).astype(jnp.int32)
    return pend - padded, blk_expert, (pend[-1] // EXPERT_ROWS).astype(jnp.int32).reshape(1)


def _col_major(t):
    b, l, f = t.shape
    rows = l // GRID_W
    return t.reshape(b, rows, GRID_W, f).transpose(0, 2, 1, 3).reshape(b, l, f)


def _row_major(t):
    b, l, f = t.shape
    rows = l // GRID_W
    return t.reshape(b, GRID_W, rows, f).transpose(0, 2, 1, 3).reshape(b, l, f)


def _per_head(vals_c, cf_c, cb_c, vals_l, cf_l, cb_l, cols, b):
    def pick(src_c, src_l, off):
        full = jnp.concatenate([src_c.reshape(b, -1, LANES), src_l.reshape(b, -1, LANES)], axis=1)
        return full[:, :, off:off + HEADS]

    srcs = dict(v=(vals_c, vals_l), f=(cf_c, cf_l), b=(cb_c, cb_l))
    picked = [pick(*srcs[name], off) for name, off in cols]
    tab = jnp.stack(picked + [jnp.zeros_like(picked[0])] * 4, axis=-1)
    col = tab.transpose(0, 2, 1, 3)
    lt = col.shape[2]
    row = col.reshape(b, HEADS, lt // CHUNK, CHUNK, 8).transpose(0, 1, 2, 4, 3)
    return col, row


def kernel(x, c, ctx, c_ctx, w_ada, b_ada, norm1_g, norm2_g, w_in, gdn_conv_w, gdn_a_log, gdn_dt_bias, gdn_norm_g,
           ml_i_bias, ml_f_bias, ml_norm_g, w_branch_gdn, w_branch_ml, w_out, w_router, router_bias, w_exp_gate_up,
           w_exp_down, w_sh_gate_up, w_sh_down, final_norm_g):
    b, l, d = x.shape
    lc = ctx.shape[1]
    t = b * l
    layer = 0

    w = w_in[layer]
    main_cols = [_ORIG[k] for k in ("gdn_qkv", "gdn_z", "ml_q", "ml_k", "ml_v", "ml_o", "mg_gdn", "mg_ml")]
    w_main = jnp.concatenate([w[:, a:e] for a, e in main_cols], axis=1).astype(BF16)
    w_gate = jnp.concatenate([w[:, _ORIG["gdn_gate"][0]:_ORIG["gdn_gate"][1]],
                              w[:, _ORIG["ml_gate"][0]:_ORIG["ml_gate"][1]],
                              jnp.zeros((d, LANES - 64), F32)], axis=1).astype(BF16)
    zeros16 = jnp.zeros((16,), F32)
    gp_add = jnp.concatenate([zeros16, gdn_dt_bias[layer].reshape(-1), ml_i_bias[layer].reshape(-1),
                              ml_f_bias[layer].reshape(-1), jnp.zeros((LANES - 64,), F32)])
    gp_mul = jnp.concatenate([zeros16, -jnp.exp(gdn_a_log[layer].astype(F32)).reshape(-1),
                              jnp.zeros((LANES - 32,), F32)])
    gparams = jnp.zeros((8, LANES), F32).at[0].set(gp_add).at[1].set(gp_mul)
    conv_w8 = jnp.zeros((8, gdn_conv_w.shape[2]), F32).at[0:GDN_CONV].set(gdn_conv_w[layer])
    wr = w_router[layer].T
    wr_hi = wr.astype(BF16)
    wr_lo = (wr - wr_hi.astype(F32)).astype(BF16)

    n_mod_rows = -(-(b + 1) // 8) * 8
    cc = jnp.zeros((n_mod_rows, d), F32).at[0:b].set(c).at[b].set(c_ctx)
    mod = _ada_mod(cc, w_ada[layer], b_ada[layer])
    mod3 = mod.reshape(n_mod_rows, 1, 6 * d)

    x2d = x.reshape(t, d)
    tm_l = min(1024, l)
    proj_l, gate_l = _project(x2d, mod3, lambda i: (i * tm_l) // l, norm1_g[layer], w_main, w_gate, tm_l)
    tm_c = min(1024, b * lc)
    proj_c, gate_c = _project(ctx.reshape(b * lc, d), mod3, lambda i: b, norm1_g[layer], w_main, w_gate, tm_c)
    proj_l3 = proj_l.reshape(b, l, N_MAIN)
    proj_c3 = proj_c.reshape(b, lc, N_MAIN)

    gate_l_cm = _col_major(gate_l.reshape(b, l, LANES)).reshape(t, LANES)
    vals_c, cf_c, cb_c = _gate_prep(gate_c, gparams)
    vals_l, cf_l, cb_l = _gate_prep(gate_l, gparams)
    vals_m, cf_m, cb_m = _gate_prep(gate_l_cm, gparams)
    gcol, grow = _per_head(vals_c, cf_c, cb_c, vals_l, cf_l, cb_l,
                           (("v", 0), ("v", 8), ("f", 16), ("b", 24)), b)
    mcol, mrow = _per_head(vals_c, cf_c, cb_c, vals_m, cf_m, cb_m,
                           (("v", 32), ("v", 40), ("f", 48), ("b", 56)), b)

    y_gdn = _gdn(proj_c3, proj_l3, conv_w8, gcol, grow, gdn_norm_g[layer])
    q_cm = _col_major(proj_l3[:, :, COL_ML_Q:COL_ML_Q + HEADS * ML_DK])
    k_cm = _col_major(proj_l3[:, :, COL_ML_K:COL_ML_K + HEADS * ML_DK])
    v_cm = _col_major(proj_l3[:, :, COL_ML_V:COL_ML_V + HEADS * HEAD_V])
    h_ml = _row_major(_mlstm(proj_c3, q_cm, k_cm, v_cm, mcol, mrow))

    x1, h2, scores_t = _merge(y_gdn.reshape(t, d), h_ml.reshape(t, d), proj_l, x2d, mod3, l, ml_norm_g[layer],
                            norm2_g[layer], w_branch_gdn[layer].astype(BF16), w_branch_ml[layer].astype(BF16),
                            w_out[layer].astype(BF16), wr_hi, wr_lo, tm=min(512, l))

    idx, wts, rank, cnt = _route(scores_t, router_bias[layer])
    n_assign = t * TOP_K
    n_blocks = (n_assign + N_EXPERTS * (EXPERT_ROWS - 1)) // EXPERT_ROWS + 1
    n_slots = n_blocks * EXPERT_ROWS
    pstart, blk_expert, n_used = _block_table(cnt[:, 0].astype(jnp.int32), n_blocks)
    pos = _slots(idx, rank, pstart).T.reshape(n_assign)
    tok_slot = (jnp.arange(n_slots, dtype=jnp.int32) % t).at[pos].set(jnp.arange(n_assign, dtype=jnp.int32) // TOP_K)
    xb = h2[tok_slot]
    yb = _experts(xb, blk_expert, n_used, w_exp_gate_up[layer], w_exp_down[layer])
    yg = yb[pos].reshape(t, TOP_K * d)
    out = _final(x1, h2, yg, wts.T, mod3, l, w_sh_gate_up[layer].astype(BF16), w_sh_down[layer].astype(BF16),
                 final_norm_g, tm=min(256, l))
    return out.reshape(b, l, d)
```

```python
import functools
import math

import jax
import jax.numpy as jnp
from jax import lax
from jax.experimental import pallas as pl
from jax.experimental.pallas import tpu as pltpu

F32 = jnp.float32
BF16 = jnp.bfloat16
HI = lax.Precision.HIGHEST

EPS = 1e-6
CHUNK = 64
GRID_W = 64
HEADS = 8
HEAD_V = 128
GDN_DK = 128
ML_DK = 64
GDN_CONV = 5
N_EXPERTS = 256
TOP_K = 8
N_GROUPS = 8
TOPK_GROUPS = 4
ROUTED_SCALE = 2.5
EXPERT_ROWS = 256
RING = 4
LANES = 128
VMEM_LIMIT = 56 * 1024 * 1024

COL_GDN_QKV = 0
COL_GDN_Z = 3072
COL_ML_Q = 4096
COL_ML_K = 4608
COL_ML_V = 5120
COL_ML_O = 6144
COL_MG_GDN = 7168
COL_MG_ML = 8192
N_MAIN = 9216
_ORIG = dict(gdn_qkv=(0, 3072), gdn_z=(3072, 4096), gdn_gate=(4096, 4128), ml_q=(4128, 4640),
             ml_k=(4640, 5152), ml_v=(5152, 6176), ml_o=(6176, 7200), ml_gate=(7200, 7232),
             mg_gdn=(7232, 8256), mg_ml=(8256, 9280))


def _params(sem, vmem=VMEM_LIMIT):
    return pltpu.CompilerParams(dimension_semantics=sem, vmem_limit_bytes=vmem)


def _dot(a, b, precision=None):
    return jnp.dot(a, b, preferred_element_type=F32, precision=precision)


def _dot_nt(a, b, precision=None):
    return lax.dot_general(a, b, (((1,), (1,)), ((), ())), preferred_element_type=F32, precision=precision)


def _dot_tn(a, b, precision=None):
    return lax.dot_general(a, b, (((0,), (0,)), ((), ())), preferred_element_type=F32, precision=precision)


def _split(a):
    ah = a.astype(BF16)
    return ah, (a - ah.astype(F32)).astype(BF16)


def _dot3(a, b):
    (ah, al), (bh, bl) = a, b
    return _dot(ah, bh) + (_dot(ah, bl) + _dot(al, bh))


def _sigmoid(x):
    return 1.0 / (1.0 + jnp.exp(-x))


def _softplus(x):
    return jnp.maximum(x, 0.0) + jnp.log(1.0 + jnp.exp(-jnp.abs(x)))


def _ada_kernel(c_ref, w_ref, b_ref, o_ref):
    c = c_ref[...]
    sc = c * _sigmoid(c)
    o_ref[...] = _dot(sc, w_ref[...], HI) + b_ref[...]


def _ada_mod(cc, w_ada, b_ada, tn=1536):
    r, d = cc.shape
    n = w_ada.shape[1]
    return pl.pallas_call(
        _ada_kernel,
        grid=(n // tn,),
        in_specs=[pl.BlockSpec((r, d), lambda j: (0, 0)),
                  pl.BlockSpec((d, tn), lambda j: (0, j)),
                  pl.BlockSpec((1, tn), lambda j: (0, j))],
        out_specs=pl.BlockSpec((r, tn), lambda j: (0, j)),
        out_shape=jax.ShapeDtypeStruct((r, n), F32),
        compiler_params=_params(("arbitrary",)),
    )(cc, w_ada, b_ada.reshape(1, n))


def _proj_kernel(x_ref, mod_ref, g_ref, w_ref, wg_ref, o_ref, og_ref, hn_ref):
    d = x_ref.shape[1]

    @pl.when(pl.program_id(1) == 0)
    def _():
        x = x_ref[...]
        y = x * lax.rsqrt(jnp.mean(x * x, axis=-1, keepdims=True) + EPS) * g_ref[...]
        shift = mod_ref[0, :, 0:d]
        scale = mod_ref[0, :, d:2 * d]
        h = (y * (1.0 + scale) + shift).astype(BF16)
        hn_ref[...] = h
        og_ref[...] = _dot(h, wg_ref[...])

    o_ref[...] = _dot(hn_ref[...], w_ref[...]).astype(o_ref.dtype)


def _project(x2d, mod3, mod_row_of_tile, norm_g, w_main, w_gate, tm, tn=1024):
    t, d = x2d.shape
    n = w_main.shape[1]
    return pl.pallas_call(
        _proj_kernel,
        grid=(t // tm, n // tn),
        in_specs=[pl.BlockSpec((tm, d), lambda i, j: (i, 0)),
                  pl.BlockSpec((1, 1, mod3.shape[2]), lambda i, j: (mod_row_of_tile(i), 0, 0)),
                  pl.BlockSpec((1, d), lambda i, j: (0, 0)),
                  pl.BlockSpec((d, tn), lambda i, j: (0, j)),
                  pl.BlockSpec((d, LANES), lambda i, j: (0, 0))],
        out_specs=[pl.BlockSpec((tm, tn), lambda i, j: (i, j)),
                   pl.BlockSpec((tm, LANES), lambda i, j: (i, 0))],
        out_shape=[jax.ShapeDtypeStruct((t, n), BF16), jax.ShapeDtypeStruct((t, LANES), F32)],
        scratch_shapes=[pltpu.VMEM((tm, d), BF16)],
        compiler_params=_params(("arbitrary", "arbitrary")),
    )(x2d, mod3, norm_g.reshape(1, d), w_main, w_gate)


def _gate_kernel(g_ref, p_ref, gd_ref, ml_ref):
    rows = g_ref.shape[0]
    raw = g_ref[...] + p_ref[0:1, :]
    lane = lax.broadcasted_iota(jnp.int32, raw.shape, 1)
    sp = _softplus(raw)
    vals = jnp.where(lane < 16, _sigmoid(raw),
                     jnp.where(lane < 32, p_ref[1:2, :] * sp,
                               jnp.where(lane < 48, raw,
                                         jnp.where(lane < 64, -_softplus(-raw), 0.0))))
    ri = lax.broadcasted_iota(jnp.int32, (CHUNK, CHUNK), 0)
    ci = lax.broadcasted_iota(jnp.int32, (CHUNK, CHUNK), 1)
    tri_f = (ri >= ci).astype(F32)
    tri_b = (ri <= ci).astype(F32)
    lane_c = lax.broadcasted_iota(jnp.int32, (CHUNK, LANES), 1)
    row_c = lax.broadcasted_iota(jnp.int32, (CHUNK, LANES), 0)
    fwd_lane = (lane_c % 16) < 8
    for c in range(rows // CHUNK):
        blk = vals[c * CHUNK:(c + 1) * CHUNK, :]
        cum = jnp.where(fwd_lane, _dot(tri_f, blk, HI), _dot(tri_b, blk, HI))
        gd_ref[c * CHUNK:(c + 1) * CHUNK, :] = jnp.where(lane_c < 16, blk, jnp.where(lane_c < 32, cum, 0.0))
        bcum = pltpu.roll(cum, LANES - 16, axis=1)
        gmb = blk - bcum
        cmf, cmb = gmb, gmb
        for s in (1, 2, 4, 8, 16, 32):
            cmf = jnp.maximum(cmf, jnp.where(row_c >= s, pltpu.roll(cmf, s, axis=0), -jnp.inf))
            cmb = jnp.maximum(cmb, jnp.where(row_c < CHUNK - s, pltpu.roll(cmb, CHUNK - s, axis=0), -jnp.inf))
        cm = jnp.where(fwd_lane, cmf, cmb)
        ml = jnp.where(lane_c < 16, pltpu.roll(gmb, LANES - 32, axis=1),
                       jnp.where(lane_c < 32, pltpu.roll(cm, LANES - 16, axis=1),
                                 jnp.where(lane_c < 48, bcum, 0.0)))
        ml_ref[c * CHUNK:(c + 1) * CHUNK, :] = ml


def _gate_prep(graw, gparams, tm=256):
    t = graw.shape[0]
    spec = pl.BlockSpec((tm, LANES), lambda i: (i, 0))
    return pl.pallas_call(
        _gate_kernel,
        grid=(t // tm,),
        in_specs=[spec, pl.BlockSpec((8, LANES), lambda i: (0, 0))],
        out_specs=[spec, spec],
        out_shape=[jax.ShapeDtypeStruct((t, LANES), F32)] * 2,
        compiler_params=_params(("arbitrary",)),
    )(graw, gparams)


def _split3(a):
    h = a.astype(BF16)
    r = a - h.astype(F32)
    m = r.astype(BF16)
    return h, m, (r - m.astype(F32)).astype(BF16)


def _lane_pick(parts, lane_sel):
    h, m, lo = parts
    return _dot(h, lane_sel) + (_dot(m, lane_sel) + _dot(lo, lane_sel))


def _lane_pick_t(parts, row_sel):
    h, m, lo = parts
    return _dot_nt(row_sel, h) + (_dot_nt(row_sel, m) + _dot_nt(row_sel, lo))


def _selectors(lane):
    li = lax.broadcasted_iota(jnp.int32, (LANES, LANES), 0)
    ai = lax.broadcasted_iota(jnp.int32, (CHUNK, LANES), 1)
    return (li == lane).astype(BF16), (ai == lane).astype(BF16)


def _dir_masks(direction):
    ri = lax.broadcasted_iota(jnp.int32, (CHUNK, CHUNK), 0)
    ci = lax.broadcasted_iota(jnp.int32, (CHUNK, CHUNK), 1)
    if direction == 0:
        return ri >= ci, ri > ci
    return ri <= ci, ri < ci


def _gdn_kernel(qc_ref, kc_ref, vc_ref, ql_ref, kl_ref, vl_ref, z_ref, cwq_ref, cwk_ref, cwv_ref,
                gdc_ref, gdl_ref, ng_ref, y_ref,
                xpad, qs, ks, vs, gds, wq_r, u_r, kd_r, qk_r, dc_r, out_s):
    lc = qc_ref.shape[1]
    ll = ql_ref.shape[1]
    lt = lc + ll
    n_c, n_l = lc // CHUNK, ll // CHUNK
    n_t = n_c + n_l
    rb = 256

    def l2n(x):
        return x * lax.rsqrt(jnp.sum(x * x, axis=-1, keepdims=True) + EPS)

    def prep(src_ref, cw_ref, dst, off, ls, kind):
        xpad[0:8, :] = jnp.zeros((8, LANES), F32)
        xpad[8:8 + ls, :] = src_ref[0].astype(F32)
        xpad[8 + ls:16 + ls, :] = jnp.zeros((8, LANES), F32)
        step = min(rb, ls)
        for r0 in range(0, ls, step):
            acc = jnp.zeros((step, LANES), F32)
            for t in range(GDN_CONV):
                s0 = r0 + 8 - GDN_CONV // 2 + t
                acc = acc + cw_ref[t:t + 1, :] * xpad[s0:s0 + step, :]
            y = acc * _sigmoid(acc)
            if kind == "q":
                y = l2n(y) * (GDN_DK ** -0.5)
            elif kind == "k":
                y = l2n(y)
            dst[off + r0:off + r0 + step, :] = y

    prep(qc_ref, cwq_ref, qs, 0, lc, "q")
    prep(kc_ref, cwk_ref, ks, 0, lc, "k")
    prep(vc_ref, cwv_ref, vs, 0, lc, "v")
    prep(ql_ref, cwq_ref, qs, lc, ll, "q")
    prep(kl_ref, cwk_ref, ks, lc, ll, "k")
    prep(vl_ref, cwv_ref, vs, lc, ll, "v")

    eye = (lax.broadcasted_iota(jnp.int32, (CHUNK, CHUNK), 0)
           == lax.broadcasted_iota(jnp.int32, (CHUNK, CHUNK), 1)).astype(F32)

    masks = (_dir_masks(0), _dir_masks(1))
    head = pl.program_id(1)
    sel_beta = [_selectors(8 * d + head) for d in range(2)]
    sel_cg = [_selectors(16 + 8 * d + head) for d in range(2)]
    gds[0:lc, :] = gdc_ref[0]
    gds[lc:lt, :] = gdl_ref[0]

    def bwd_chunk(t):
        return jnp.where(t < n_c, n_c - 1 - t, n_t + n_c - 1 - t)

    def prep_chain(t, d):
        incl, strict = masks[d]
        last = CHUNK - 1 if d == 0 else 0
        tc = jnp.minimum(t, n_t - 1)
        c = tc if d == 0 else bwd_chunk(tc)
        r0 = pl.multiple_of(c * CHUNK, CHUNK)
        q = qs[pl.ds(r0, CHUNK), :]
        k = ks[pl.ds(r0, CHUNK), :]
        v = vs[pl.ds(r0, CHUNK), :]
        gparts = _split3(gds[pl.ds(r0, CHUNK), :])
        beta = _lane_pick(gparts, sel_beta[d][0])
        cgc = _lane_pick(gparts, sel_cg[d][0])
        cgr = _lane_pick_t(gparts, sel_cg[d][1])
        kk = _dot_nt(k, k)
        qk = _dot_nt(q, k)
        slot = (t % RING) * 2 + d
        yield
        cg_last = cgc[last:last + 1, :]
        decay = jnp.exp(jnp.where(incl, cgc[:, 0:CHUNK] - cgr, -jnp.inf))
        ecg = jnp.exp(cgc)
        kb = k * beta
        qk_r[slot] = qk * decay
        wq_r[slot, CHUNK:2 * CHUNK, :] = q * ecg
        kd_r[slot] = k * jnp.exp(cg_last - cgc)
        dc_r[slot] = jnp.broadcast_to(jnp.exp(cg_last), (8, LANES))
        x = jnp.where(strict, -(beta[:, 0:CHUNK] * kk) * decay, 0.0)
        tinv = eye + x
        xs = _split(x)
        x = _dot3(xs, xs)
        yield
        for _ in range(4):
            xs = _split(x)
            tinv, x = tinv + _dot3(_split(tinv), xs), _dot3(xs, xs)
            yield
        tinv = tinv + _dot3(_split(tinv), _split(x))
        yield
        wq_r[slot, 0:CHUNK, :] = _dot(tinv, kb * ecg)
        u_r[slot] = _dot(tinv, v * beta)

    out_s[...] = jnp.zeros(out_s.shape, F32)

    def scan_chain(d, t0, s, with_out, result):
        for j in range(2):
            t = t0 + j
            slot = (t % RING) * 2 + d
            ws = _dot(wq_r[slot], s)
            yield
            v_new = u_r[slot] - ws[0:CHUNK, :]
            if with_out:
                c = t if d == 0 else bwd_chunk(t)
                o = ws[CHUNK:2 * CHUNK, :] + _dot(qk_r[slot], v_new)
                l0 = pl.multiple_of((c - n_c) * CHUNK, CHUNK)
                out_s[pl.ds(l0, CHUNK), :] += o
            s = s * dc_r[slot][0:1, :] + _dot_tn(kd_r[slot], v_new)
            yield
        result[d] = s

    def lockstep(chains):
        chains = list(chains)
        while chains:
            alive = []
            for ch in chains:
                try:
                    next(ch)
                    alive.append(ch)
                except StopIteration:
                    pass
            chains = alive

    def pair_body(i, carry, t_base, with_out):
        t0 = t_base + 2 * i
        result = [None, None]
        lockstep([scan_chain(0, t0, carry[0], with_out, result), scan_chain(1, t0, carry[1], with_out, result)]
                 + [prep_chain(t0 + 2 + j, d) for j in range(2) for d in range(2)])
        return result[0], result[1]

    lockstep([prep_chain(j, d) for j in range(2) for d in range(2)])
    s0 = jnp.zeros((GDN_DK, HEAD_V), F32)
    carry = lax.fori_loop(0, n_c // 2, functools.partial(pair_body, t_base=0, with_out=False), (s0, s0))
    lax.fori_loop(0, n_l // 2, functools.partial(pair_body, t_base=n_c, with_out=True), carry)


    def out_body(i, carry):
        r0 = pl.multiple_of(i * rb, rb)
        o = out_s[pl.ds(r0, rb), :]
        z = z_ref[0, pl.ds(r0, rb), :].astype(F32)
        y = o * lax.rsqrt(jnp.mean(o * o, axis=-1, keepdims=True) + EPS) * ng_ref[...]
        y_ref[0, pl.ds(r0, rb), :] = (y * (z * _sigmoid(z))).astype(y_ref.dtype)
        return carry

    lax.fori_loop(0, ll // rb, out_body, 0)


def _gdn(proj_c, proj_l, conv_w8, gd_c, gd_l, norm_g):
    b, lc, _ = proj_c.shape
    ll = proj_l.shape[1]
    lt = lc + ll
    n_t = lt // CHUNK
    qb, kb_, vb, zb = (COL_GDN_QKV // LANES, COL_GDN_QKV // LANES + HEADS, COL_GDN_QKV // LANES + 2 * HEADS,
                       COL_GDN_Z // LANES)

    def seq_spec(l, col0):
        return pl.BlockSpec((1, l, LANES), lambda i, h: (i, 0, col0 + h))

    def cw_spec(col0):
        return pl.BlockSpec((8, LANES), lambda i, h: (0, col0 + h))

    return pl.pallas_call(
        _gdn_kernel,
        grid=(b, HEADS),
        in_specs=[seq_spec(lc, qb), seq_spec(lc, kb_), seq_spec(lc, vb),
                  seq_spec(ll, qb), seq_spec(ll, kb_), seq_spec(ll, vb), seq_spec(ll, zb),
                  cw_spec(0), cw_spec(HEADS), cw_spec(2 * HEADS),
                  pl.BlockSpec((1, lc, LANES), lambda i, h: (i, 0, 0)),
                  pl.BlockSpec((1, ll, LANES), lambda i, h: (i, 0, 0)),
                  pl.BlockSpec((1, LANES), lambda i, h: (0, 0))],
        out_specs=pl.BlockSpec((1, ll, LANES), lambda i, h: (i, 0, h)),
        out_shape=jax.ShapeDtypeStruct((b, ll, HEADS * HEAD_V), BF16),
        scratch_shapes=[pltpu.VMEM((max(lc, ll) + 16, LANES), F32),
                        pltpu.VMEM((lt, LANES), F32), pltpu.VMEM((lt, LANES), F32), pltpu.VMEM((lt, LANES), F32),
                        pltpu.VMEM((lt, LANES), F32),
                        pltpu.VMEM((2 * RING, 2 * CHUNK, LANES), F32),
                        pltpu.VMEM((2 * RING, CHUNK, LANES), F32), pltpu.VMEM((2 * RING, CHUNK, LANES), F32),
                        pltpu.VMEM((2 * RING, CHUNK, CHUNK), F32),
                        pltpu.VMEM((2 * RING, 8, LANES), F32),
                        pltpu.VMEM((ll, LANES), F32)],
        compiler_params=_params(("arbitrary", "arbitrary")),
    )(proj_c, proj_c, proj_c, proj_l, proj_l, proj_l, proj_l, conv_w8, conv_w8, conv_w8,
      gd_c, gd_l, norm_g.reshape(1, LANES))


def _mlstm_kernel(qc_ref, kc_ref, vc_ref, ql_ref, kl_ref, vl_ref, mlc_ref, mll_ref, h_ref, out_s, mls):
    lc = qc_ref.shape[1]
    ll = ql_ref.shape[1]
    n_c, n_l = lc // CHUNK, ll // CHUNK
    n_t = n_c + n_l
    lt = lc + ll
    pair = pl.program_id(1)
    lane = lax.broadcasted_iota(jnp.int32, (CHUNK, LANES), 1)
    ones_v = jnp.ones((CHUNK, HEAD_V), BF16)
    mls[0:lc, :] = mlc_ref[0]
    mls[lc:lt, :] = mll_ref[0]
    chains = [(hh, d) for hh in range(2) for d in range(2)]
    hmask = [((lane // ML_DK) == hh).astype(F32) for hh in range(2)]
    sels = {}
    for hh, d in chains:
        base = 8 * d + 2 * pair + hh
        sels[hh, d] = (_selectors(base), _selectors(16 + base)[0], _selectors(32 + base)[0])
    incl = [_dir_masks(d)[0] for d in range(2)]

    def wide(a):
        return jnp.concatenate([a, a], axis=1)

    def chain(hh, d, c, state, is_ctx, result):
        cs, ms = state
        last = CHUNK - 1 if d == 0 else 0
        if is_ctx:
            rows = pl.ds(pl.multiple_of(c * CHUNK, CHUNK), CHUNK)
            q_ref, k_ref, v_ref = qc_ref, kc_ref, vc_ref
        else:
            rows = pl.ds(pl.multiple_of((c - n_c) * CHUNK, CHUNK), CHUNK)
            q_ref, k_ref, v_ref = ql_ref, kl_ref, vl_ref
        q = (q_ref[0, rows, :].astype(F32) * hmask[hh]).astype(BF16)
        k = k_ref[0, rows, :].astype(F32) * (hmask[hh] * (ML_DK ** -0.5))
        v = jnp.concatenate([v_ref[0, rows, hh * HEAD_V:(hh + 1) * HEAD_V], ones_v], axis=1)
        parts = _split3(mls[pl.ds(pl.multiple_of(c * CHUNK, CHUNK), CHUNK), :])
        (sel_g, sel_gt), sel_m, sel_b = sels[hh, d]
        gmb = _lane_pick(parts, sel_g)
        gmb_t = _lane_pick_t(parts, sel_gt)
        cm = _lane_pick(parts, sel_m)
        bc = _lane_pick(parts, sel_b)
        qk = _dot_nt(q, k.astype(BF16))
        yield
        cm_last = cm[last:last + 1, :]
        b_last = bc[last:last + 1, :]
        mm = jnp.maximum(ms, cm)
        p = jnp.where(incl[d], jnp.exp(gmb_t - mm[:, 0:CHUNK]), 0.0) * qk
        wk = (k * jnp.exp(gmb - cm_last)).astype(BF16)
        inter = _dot(q, cs.astype(BF16))
        intra = _dot(p.astype(BF16), v)
        c_loc = _dot_tn(wk, v)
        yield
        if not is_ctx:
            nd = wide(jnp.exp(ms - mm)) * inter + intra
            hv = nd[:, 0:HEAD_V] / jnp.maximum(jnp.abs(nd[:, HEAD_V:2 * HEAD_V]), jnp.exp(-(bc + mm)))
            l0 = pl.multiple_of((c - n_c) * CHUNK, CHUNK)
            out_s[pl.ds(l0, CHUNK), hh * HEAD_V:(hh + 1) * HEAD_V] += hv
        mx = jnp.maximum(ms, cm_last)
        result[hh, d] = (wide(jnp.exp(ms - mx)) * cs + wide(jnp.exp(cm_last - mx)) * c_loc, b_last + mx)

    out_s[...] = jnp.zeros(out_s.shape, F32)

    def run(chains_iter):
        live = list(chains_iter)
        while live:
            alive = []
            for ch in live:
                try:
                    next(ch)
                    alive.append(ch)
                except StopIteration:
                    pass
            live = alive

    def body(i, carry, is_ctx):
        result = {}
        gens = []
        for n, (hh, d) in enumerate(chains):
            if is_ctx:
                c = i if d == 0 else n_c - 1 - i
            else:
                c = n_c + i if d == 0 else n_t - 1 - i
            gens.append(chain(hh, d, c, carry[n], is_ctx, result))
        run(gens)
        return tuple(result[hd] for hd in chains)

    st0 = (jnp.zeros((LANES, 2 * HEAD_V), F32), jnp.zeros((1, LANES), F32))
    carry = lax.fori_loop(0, n_c, functools.partial(body, is_ctx=True), (st0,) * 4)
    lax.fori_loop(0, n_l, functools.partial(body, is_ctx=False), carry)
    h_ref[0] = out_s[...].astype(h_ref.dtype)


def _mlstm(proj_c, q_l, k_l, v_l, ml_c, ml_l):
    b, lc, _ = proj_c.shape
    ll = q_l.shape[1]
    lt = lc + ll
    qb, kb_, vb = COL_ML_Q // LANES, COL_ML_K // LANES, COL_ML_V // (2 * HEAD_V)
    return pl.pallas_call(
        _mlstm_kernel,
        grid=(b, HEADS // 2),
        in_specs=[pl.BlockSpec((1, lc, LANES), lambda i, p: (i, 0, qb + p)),
                  pl.BlockSpec((1, lc, LANES), lambda i, p: (i, 0, kb_ + p)),
                  pl.BlockSpec((1, lc, 2 * HEAD_V), lambda i, p: (i, 0, vb + p)),
                  pl.BlockSpec((1, ll, LANES), lambda i, p: (i, 0, p)),
                  pl.BlockSpec((1, ll, LANES), lambda i, p: (i, 0, p)),
                  pl.BlockSpec((1, ll, 2 * HEAD_V), lambda i, p: (i, 0, p)),
                  pl.BlockSpec((1, lc, LANES), lambda i, p: (i, 0, 0)),
                  pl.BlockSpec((1, ll, LANES), lambda i, p: (i, 0, 0))],
        out_specs=pl.BlockSpec((1, ll, 2 * HEAD_V), lambda i, p: (i, 0, p)),
        out_shape=jax.ShapeDtypeStruct((b, ll, HEADS * HEAD_V), BF16),
        scratch_shapes=[pltpu.VMEM((ll, 2 * HEAD_V), F32), pltpu.VMEM((lt, LANES), F32)],
        compiler_params=_params(("arbitrary", "arbitrary")),
    )(proj_c, proj_c, proj_c, q_l, k_l, v_l, ml_c, ml_l)


def _merge_kernel(yg_ref, hm_ref, o_ref, gg_ref, gm_ref, x_ref, mod_ref, mlg_ref, n2_ref,
                  wbg_ref, wbm_ref, wo_ref, wrh_ref, wrl_ref, x1_ref, h2_ref, sc_ref):
    d = x_ref.shape[1]
    o = o_ref[...].astype(F32)
    ym = _sigmoid(o) * hm_ref[...].astype(F32)
    segs = []
    for h in range(HEADS):
        seg = ym[:, h * HEAD_V:(h + 1) * HEAD_V]
        segs.append(seg * lax.rsqrt(jnp.mean(seg * seg, axis=-1, keepdims=True) + EPS))
    ymn = jnp.concatenate(segs, axis=1) * mlg_ref[...]
    y_gdn = _dot(yg_ref[...], wbg_ref[...])
    y_ml = _dot(ymn.astype(BF16), wbm_ref[...])
    mixed = _sigmoid(gg_ref[...].astype(F32)) * y_gdn + _sigmoid(gm_ref[...].astype(F32)) * y_ml
    y = _dot(mixed.astype(BF16), wo_ref[...])
    x1 = x_ref[...] + mod_ref[0, :, 2 * d:3 * d] * y
    x1_ref[...] = x1
    hn = x1 * lax.rsqrt(jnp.mean(x1 * x1, axis=-1, keepdims=True) + EPS) * n2_ref[...]
    h2 = hn * (1.0 + mod_ref[0, :, 4 * d:5 * d]) + mod_ref[0, :, 3 * d:4 * d]
    h2_hi = h2.astype(BF16)
    h2_ref[...] = h2_hi
    h2_lo = (h2 - h2_hi.astype(F32)).astype(BF16)
    logits = _dot_nt(wrh_ref[...], h2_hi) + (_dot_nt(wrl_ref[...], h2_hi) + _dot_nt(wrh_ref[...], h2_lo))
    sc_ref[...] = _sigmoid(logits)


def _merge(y_gdn, h_ml, proj_l2d, x2d, mod3, rows_per_mod, ml_norm_g, norm2_g, wbg, wbm, wo, wr_hi, wr_lo, tm=512):
    t, d = x2d.shape
    e = wr_hi.shape[0]
    row = lambda i: (i, 0)
    const = lambda i: (0, 0)
    return pl.pallas_call(
        _merge_kernel,
        grid=(t // tm,),
        in_specs=[pl.BlockSpec((tm, d), row), pl.BlockSpec((tm, d), row),
                  pl.BlockSpec((tm, d), lambda i: (i, COL_ML_O // d)),
                  pl.BlockSpec((tm, d), lambda i: (i, COL_MG_GDN // d)),
                  pl.BlockSpec((tm, d), lambda i: (i, COL_MG_ML // d)),
                  pl.BlockSpec((tm, d), row),
                  pl.BlockSpec((1, 1, mod3.shape[2]), lambda i: ((i * tm) // rows_per_mod, 0, 0)),
                  pl.BlockSpec((1, d), const), pl.BlockSpec((1, d), const),
                  pl.BlockSpec((d, d), const), pl.BlockSpec((d, d), const), pl.BlockSpec((d, d), const),
                  pl.BlockSpec((e, d), const), pl.BlockSpec((e, d), const)],
        out_specs=[pl.BlockSpec((tm, d), row), pl.BlockSpec((tm, d), row), pl.BlockSpec((e, tm), lambda i: (0, i))],
        out_shape=[jax.ShapeDtypeStruct((t, d), F32), jax.ShapeDtypeStruct((t, d), BF16),
                   jax.ShapeDtypeStruct((e, t), F32)],
        compiler_params=_params(("arbitrary",)),
    )(y_gdn, h_ml, proj_l2d, proj_l2d, proj_l2d, x2d, mod3, ml_norm_g.reshape(1, d), norm2_g.reshape(1, d),
      wbg, wbm, wo, wr_hi, wr_lo)


def _expert_kernel(be_ref, nu_ref, x_ref, wgu_ref, wd_ref, y_ref, wgu_s, wd_s):
    i = pl.program_id(0)
    de = wd_ref.shape[1]
    changed = jnp.logical_or(i == 0, be_ref[i] != be_ref[jnp.maximum(i - 1, 0)])

    @pl.when(changed)
    def _():
        wgu_s[...] = wgu_ref[0].astype(BF16)
        wd_s[...] = wd_ref[0].astype(BF16)

    @pl.when(i < nu_ref[0])
    def _():
        gu = _dot(x_ref[...], wgu_s[...])
        g = gu[:, 0:de]
        act = (g * _sigmoid(g)) * gu[:, de:2 * de]
        y_ref[...] = _dot(act.astype(BF16), wd_s[...]).astype(y_ref.dtype)

    @pl.when(i >= nu_ref[0])
    def _():
        y_ref[...] = jnp.zeros(y_ref.shape, y_ref.dtype)


def _experts(xb, blk_expert, n_used, w_gu, w_down):
    n_slots, d = xb.shape
    n_blocks = n_slots // EXPERT_ROWS
    e, _, de2 = w_gu.shape
    de = de2 // 2
    return pl.pallas_call(
        _expert_kernel,
        grid_spec=pltpu.PrefetchScalarGridSpec(
            num_scalar_prefetch=2,
            grid=(n_blocks,),
            in_specs=[pl.BlockSpec((EXPERT_ROWS, d), lambda i, be, nu: (i, 0)),
                      pl.BlockSpec((1, d, de2), lambda i, be, nu: (be[i], 0, 0)),
                      pl.BlockSpec((1, de, d), lambda i, be, nu: (be[i], 0, 0))],
            out_specs=pl.BlockSpec((EXPERT_ROWS, d), lambda i, be, nu: (i, 0)),
            scratch_shapes=[pltpu.VMEM((d, de2), BF16), pltpu.VMEM((de, d), BF16)]),
        out_shape=jax.ShapeDtypeStruct((n_slots, d), BF16),
        compiler_params=_params(("arbitrary",)),
    )(blk_expert, n_used, xb, w_gu, w_down)


def _final_kernel(x1_ref, h2_ref, yg_ref, wt_ref, mod_ref, wsg_ref, wsd_ref, fg_ref, o_ref):
    d = x1_ref.shape[1]
    ds_ = wsd_ref.shape[0]
    wt = wt_ref[...]
    routed = jnp.zeros(x1_ref.shape, F32)
    for k in range(TOP_K):
        routed = routed + wt[:, k:k + 1] * yg_ref[k].astype(F32)
    gu = _dot(h2_ref[...], wsg_ref[...])
    g = gu[:, 0:ds_]
    sh = _dot(((g * _sigmoid(g)) * gu[:, ds_:2 * ds_]).astype(BF16), wsd_ref[...])
    x2 = x1_ref[...] + mod_ref[0, :, 5 * d:6 * d] * (routed + sh)
    o_ref[...] = x2 * lax.rsqrt(jnp.mean(x2 * x2, axis=-1, keepdims=True) + EPS) * fg_ref[...]


def _final(x1, h2, yg, wts, mod3, rows_per_mod, w_sh_gu, w_sh_down, final_g, tm=256):
    t, d = x1.shape
    row = lambda i: (i, 0)
    const = lambda i: (0, 0)
    return pl.pallas_call(
        _final_kernel,
        grid=(t // tm,),
        in_specs=[pl.BlockSpec((tm, d), row), pl.BlockSpec((tm, d), row),
                  pl.BlockSpec((TOP_K, tm, d), lambda i: (0, i, 0)), pl.BlockSpec((tm, TOP_K), row),
                  pl.BlockSpec((1, 1, mod3.shape[2]), lambda i: ((i * tm) // rows_per_mod, 0, 0)),
                  pl.BlockSpec(w_sh_gu.shape, const), pl.BlockSpec(w_sh_down.shape, const),
                  pl.BlockSpec((1, d), const)],
        out_specs=pl.BlockSpec((tm, d), row),
        out_shape=jax.ShapeDtypeStruct((t, d), F32),
        compiler_params=_params(("arbitrary",)),
    )(x1, h2, yg, wts, mod3, w_sh_gu, w_sh_down, final_g.reshape(1, d))


def _route_kernel(sc_ref, bias_ref, tri_ref, idx_ref, wt_ref, rk_ref, cnt_ref, base_s):
    @pl.when(pl.program_id(0) == 0)
    def _():
        base_s[...] = jnp.zeros(base_s.shape, F32)

    scores = sc_ref[...]
    e, tn = scores.shape
    gsz = e // N_GROUPS
    sel3 = (scores + bias_ref[...]).reshape(N_GROUPS, gsz, tn)
    m1 = jnp.max(sel3, axis=1)
    is_max = sel3 == m1[:, None, :]
    n_max = jnp.sum(is_max.astype(F32), axis=1)
    m2 = jnp.max(jnp.where(is_max, -jnp.inf, sel3), axis=1)
    grp = m1 + jnp.where(n_max >= 2.0, m1, m2)
    gi = lax.broadcasted_iota(jnp.int32, (N_GROUPS, tn), 0)
    ahead = jnp.zeros((N_GROUPS, tn), F32)
    for g in range(N_GROUPS):
        row = grp[g:g + 1, :]
        ahead = ahead + jnp.logical_or(row > grp, jnp.logical_and(row == grp, g < gi)).astype(F32)
    ahead3 = jnp.broadcast_to(ahead[:, None, :], (N_GROUPS, gsz, tn))
    selm = jnp.where(ahead3 < float(TOPK_GROUPS), sel3, -jnp.inf).reshape(e, tn)
    ri = lax.broadcasted_iota(jnp.int32, (e, tn), 0).astype(F32)
    member = jnp.zeros((e, tn), F32)
    idxs, ws = [], []
    for _ in range(TOP_K):
        m = jnp.max(selm, axis=0, keepdims=True)
        idx = jnp.min(jnp.where(selm == m, ri, float(e)), axis=0, keepdims=True)
        hit = ri == idx
        ws.append(jnp.sum(jnp.where(hit, scores, 0.0), axis=0, keepdims=True))
        idxs.append(idx)
        selm = jnp.where(hit, -jnp.inf, selm)
        member = jnp.where(hit, 1.0, member)
    w = jnp.concatenate(ws, axis=0)
    wt_ref[...] = w / jnp.sum(w, axis=0, keepdims=True) * ROUTED_SCALE
    idx_ref[...] = jnp.concatenate(idxs, axis=0).astype(jnp.int32)
    cum = _dot(member.astype(BF16), tri_ref[...]) + base_s[...]
    rk_ref[...] = jnp.concatenate(
        [jnp.sum(jnp.where(ri == idx, cum, 0.0), axis=0, keepdims=True) for idx in idxs], axis=0).astype(jnp.int32)
    total = base_s[...] + jnp.sum(member, axis=1, keepdims=True)
    base_s[...] = total
    cnt_ref[...] = total


def _route(scores_t, router_bias):
    e, t = scores_t.shape
    tn = LANES
    bias = jnp.broadcast_to(router_bias.astype(F32)[:, None], (e, tn))
    tri = (jnp.arange(tn)[:, None] < jnp.arange(tn)[None, :]).astype(BF16)
    tok = pl.BlockSpec((TOP_K, tn), lambda i: (0, i))
    const = lambda i: (0, 0)
    return pl.pallas_call(
        _route_kernel,
        grid=(t // tn,),
        in_specs=[pl.BlockSpec((e, tn), lambda i: (0, i)), pl.BlockSpec((e, tn), const),
                  pl.BlockSpec((tn, tn), const)],
        out_specs=[tok, tok, tok, pl.BlockSpec((e, tn), const)],
        out_shape=[jax.ShapeDtypeStruct((TOP_K, t), jnp.int32), jax.ShapeDtypeStruct((TOP_K, t), F32),
                   jax.ShapeDtypeStruct((TOP_K, t), jnp.int32), jax.ShapeDtypeStruct((e, tn), F32)],
        scratch_shapes=[pltpu.VMEM((e, tn), F32)],
        compiler_params=_params(("arbitrary",)),
    )(scores_t, bias, tri)


def _slot_kernel(idx_ref, rk_ref, ps_ref, pos_ref):
    e, tn = ps_ref.shape
    ri = lax.broadcasted_iota(jnp.int32, (e, tn), 0)
    ps = ps_ref[...]
    rows = [jnp.sum(jnp.where(ri == idx_ref[k:k + 1, :], ps, 0.0), axis=0, keepdims=True) for k in range(TOP_K)]
    pos_ref[...] = rk_ref[...] + jnp.concatenate(rows, axis=0).astype(jnp.int32)


def _slots(idx, rank, pstart):
    k, t = idx.shape
    e = pstart.shape[0]
    tn = LANES
    tok = pl.BlockSpec((k, tn), lambda i: (0, i))
    return pl.pallas_call(
        _slot_kernel,
        grid=(t // tn,),
        in_specs=[tok, tok, pl.BlockSpec((e, tn), lambda i: (0, 0))],
        out_specs=tok,
        out_shape=jax.ShapeDtypeStruct((k, t), jnp.int32),
        compiler_params=_params(("arbitrary",)),
    )(idx, rank, jnp.broadcast_to(pstart.astype(F32)[:, None], (e, tn)))


def _block_table(counts, n_blocks):
    padded = (counts + EXPERT_ROWS - 1) // EXPERT_ROWS * EXPERT_ROWS
    pend = jnp.cumsum(padded)
    blk_expert = jnp.minimum(jnp.searchsorted(pend, jnp.arange(n_blocks) * EXPERT_ROWS, side='right'),
                             N_EXPERTS - 1).astype(jnp.int32)
    return pend - padded, blk_expert, (pend[-1] // EXPERT_ROWS).astype(jnp.int32).reshape(1)


def _col_major(t):
    b, l, f = t.shape
    rows = l // GRID_W
    return t.reshape(b, rows, GRID_W, f).transpose(0, 2, 1, 3).reshape(b, l, f)


def _row_major(t):
    b, l, f = t.shape
    rows = l // GRID_W
    return t.reshape(b, GRID_W, rows, f).transpose(0, 2, 1, 3).reshape(b, l, f)


def kernel(x, c, ctx, c_ctx, w_ada, b_ada, norm1_g, norm2_g, w_in, gdn_conv_w, gdn_a_log, gdn_dt_bias, gdn_norm_g,
           ml_i_bias, ml_f_bias, ml_norm_g, w_branch_gdn, w_branch_ml, w_out, w_router, router_bias, w_exp_gate_up,
           w_exp_down, w_sh_gate_up, w_sh_down, final_norm_g):
    b, l, d = x.shape
    lc = ctx.shape[1]
    t = b * l
    layer = 0

    w = w_in[layer]
    main_cols = [_ORIG[k] for k in ("gdn_qkv", "gdn_z", "ml_q", "ml_k", "ml_v", "ml_o", "mg_gdn", "mg_ml")]
    w_main = jnp.concatenate([w[:, a:e] for a, e in main_cols], axis=1).astype(BF16)
    w_gate = jnp.concatenate([w[:, _ORIG["gdn_gate"][0]:_ORIG["gdn_gate"][1]],
                              w[:, _ORIG["ml_gate"][0]:_ORIG["ml_gate"][1]],
                              jnp.zeros((d, LANES - 64), F32)], axis=1).astype(BF16)
    zeros16 = jnp.zeros((16,), F32)
    gp_add = jnp.concatenate([zeros16, gdn_dt_bias[layer].reshape(-1), ml_i_bias[layer].reshape(-1),
                              ml_f_bias[layer].reshape(-1), jnp.zeros((LANES - 64,), F32)])
    gp_mul = jnp.concatenate([zeros16, -jnp.exp(gdn_a_log[layer].astype(F32)).reshape(-1),
                              jnp.zeros((LANES - 32,), F32)])
    gparams = jnp.zeros((8, LANES), F32).at[0].set(gp_add).at[1].set(gp_mul)
    conv_w8 = jnp.zeros((8, gdn_conv_w.shape[2]), F32).at[0:GDN_CONV].set(gdn_conv_w[layer])
    wr = w_router[layer].T
    wr_hi = wr.astype(BF16)
    wr_lo = (wr - wr_hi.astype(F32)).astype(BF16)

    n_mod_rows = -(-(b + 1) // 8) * 8
    cc = jnp.zeros((n_mod_rows, d), F32).at[0:b].set(c).at[b].set(c_ctx)
    mod = _ada_mod(cc, w_ada[layer], b_ada[layer])
    mod3 = mod.reshape(n_mod_rows, 1, 6 * d)

    x2d = x.reshape(t, d)
    tm_l = min(1024, l)
    proj_l, gate_l = _project(x2d, mod3, lambda i: (i * tm_l) // l, norm1_g[layer], w_main, w_gate, tm_l)
    tm_c = min(1024, b * lc)
    proj_c, gate_c = _project(ctx.reshape(b * lc, d), mod3, lambda i: b, norm1_g[layer], w_main, w_gate, tm_c)
    proj_l3 = proj_l.reshape(b, l, N_MAIN)
    proj_c3 = proj_c.reshape(b, lc, N_MAIN)

    gate_l_cm = _col_major(gate_l.reshape(b, l, LANES)).reshape(t, LANES)
    gd_c, ml_c = _gate_prep(gate_c, gparams)
    gd_l, _ = _gate_prep(gate_l, gparams)
    _, ml_l = _gate_prep(gate_l_cm, gparams)

    y_gdn = _gdn(proj_c3, proj_l3, conv_w8, gd_c.reshape(b, lc, LANES), gd_l.reshape(b, l, LANES),
                 gdn_norm_g[layer])
    q_cm = _col_major(proj_l3[:, :, COL_ML_Q:COL_ML_Q + HEADS * ML_DK])
    k_cm = _col_major(proj_l3[:, :, COL_ML_K:COL_ML_K + HEADS * ML_DK])
    v_cm = _col_major(proj_l3[:, :, COL_ML_V:COL_ML_V + HEADS * HEAD_V])
    h_ml = _row_major(_mlstm(proj_c3, q_cm, k_cm, v_cm, ml_c.reshape(b, lc, LANES), ml_l.reshape(b, l, LANES)))

    x1, h2, scores_t = _merge(y_gdn.reshape(t, d), h_ml.reshape(t, d), proj_l, x2d, mod3, l, ml_norm_g[layer],
                            norm2_g[layer], w_branch_gdn[layer].astype(BF16), w_branch_ml[layer].astype(BF16),
                            w_out[layer].astype(BF16), wr_hi, wr_lo, tm=min(512, l))

    idx, wts, rank, cnt = _route(scores_t, router_bias[layer])
    n_assign = t * TOP_K
    n_blocks = (n_assign + N_EXPERTS * (EXPERT_ROWS - 1)) // EXPERT_ROWS + 1
    n_slots = n_blocks * EXPERT_ROWS
    pstart, blk_expert, n_used = _block_table(cnt[:, 0].astype(jnp.int32), n_blocks)
    pos = _slots(idx, rank, pstart).reshape(n_assign)
    tok_slot = (jnp.arange(n_slots, dtype=jnp.int32) % t).at[pos].set(jnp.arange(n_assign, dtype=jnp.int32) % t)
    xb = h2[tok_slot]
    yb = _experts(xb, blk_expert, n_used, w_exp_gate_up[layer], w_exp_down[layer])
    yg = yb[pos].reshape(TOP_K, t, d)
    out = _final(x1, h2, yg, wts.T, mod3, l, w_sh_gate_up[layer].astype(BF16), w_sh_down[layer].astype(BF16),
                 final_norm_g, tm=min(256, l))
    return out.reshape(b, l, d)
```

```python
import functools
import math

import jax
import jax.numpy as jnp
from jax import lax
from jax.experimental import pallas as pl
from jax.experimental.pallas import tpu as pltpu

F32 = jnp.float32
BF16 = jnp.bfloat16
HI = lax.Precision.HIGHEST

EPS = 1e-6
CHUNK = 64
GRID_W = 64
HEADS = 8
HEAD_V = 128
GDN_DK = 128
ML_DK = 64
GDN_CONV = 5
N_EXPERTS = 256
TOP_K = 8
N_GROUPS = 8
TOPK_GROUPS = 4
ROUTED_SCALE = 2.5
EXPERT_ROWS = 256
GDN_STEPS = 4
RING = 2 * GDN_STEPS
LANES = 128
VMEM_LIMIT = 56 * 1024 * 1024

COL_GDN_QKV = 0
COL_GDN_Z = 3072
COL_ML_Q = 4096
COL_ML_K = 4608
COL_ML_V = 5120
COL_ML_O = 6144
COL_MG_GDN = 7168
COL_MG_ML = 8192
N_MAIN = 9216
_ORIG = dict(gdn_qkv=(0, 3072), gdn_z=(3072, 4096), gdn_gate=(4096, 4128), ml_q=(4128, 4640),
             ml_k=(4640, 5152), ml_v=(5152, 6176), ml_o=(6176, 7200), ml_gate=(7200, 7232),
             mg_gdn=(7232, 8256), mg_ml=(8256, 9280))


def _params(sem, vmem=VMEM_LIMIT):
    return pltpu.CompilerParams(dimension_semantics=sem, vmem_limit_bytes=vmem)


def _dot(a, b, precision=None):
    return jnp.dot(a, b, preferred_element_type=F32, precision=precision)


def _dot_nt(a, b, precision=None):
    return lax.dot_general(a, b, (((1,), (1,)), ((), ())), preferred_element_type=F32, precision=precision)


def _dot_tn(a, b, precision=None):
    return lax.dot_general(a, b, (((0,), (0,)), ((), ())), preferred_element_type=F32, precision=precision)


def _split(a):
    ah = a.astype(BF16)
    return ah, (a - ah.astype(F32)).astype(BF16)


def _dot3(a, b):
    (ah, al), (bh, bl) = a, b
    return _dot(ah, bh) + (_dot(ah, bl) + _dot(al, bh))


def _sigmoid(x):
    return 1.0 / (1.0 + jnp.exp(-x))


def _softplus(x):
    return jnp.maximum(x, 0.0) + jnp.log(1.0 + jnp.exp(-jnp.abs(x)))


def _ada_kernel(c_ref, w_ref, b_ref, o_ref):
    c = c_ref[...]
    sc = c * _sigmoid(c)
    o_ref[...] = _dot(sc, w_ref[...], HI) + b_ref[...]


def _ada_mod(cc, w_ada, b_ada, tn=1536):
    r, d = cc.shape
    n = w_ada.shape[1]
    return pl.pallas_call(
        _ada_kernel,
        grid=(n // tn,),
        in_specs=[pl.BlockSpec((r, d), lambda j: (0, 0)),
                  pl.BlockSpec((d, tn), lambda j: (0, j)),
                  pl.BlockSpec((1, tn), lambda j: (0, j))],
        out_specs=pl.BlockSpec((r, tn), lambda j: (0, j)),
        out_shape=jax.ShapeDtypeStruct((r, n), F32),
        compiler_params=_params(("arbitrary",)),
    )(cc, w_ada, b_ada.reshape(1, n))


def _proj_kernel(x_ref, mod_ref, g_ref, w_ref, wg_ref, o_ref, og_ref, hn_ref):
    d = x_ref.shape[1]

    @pl.when(pl.program_id(1) == 0)
    def _():
        x = x_ref[...]
        y = x * lax.rsqrt(jnp.mean(x * x, axis=-1, keepdims=True) + EPS) * g_ref[...]
        shift = mod_ref[0, :, 0:d]
        scale = mod_ref[0, :, d:2 * d]
        h = (y * (1.0 + scale) + shift).astype(BF16)
        hn_ref[...] = h
        og_ref[...] = _dot(h, wg_ref[...])

    o_ref[...] = _dot(hn_ref[...], w_ref[...]).astype(o_ref.dtype)


def _project(x2d, mod3, mod_row_of_tile, norm_g, w_main, w_gate, tm, tn=1024):
    t, d = x2d.shape
    n = w_main.shape[1]
    return pl.pallas_call(
        _proj_kernel,
        grid=(t // tm, n // tn),
        in_specs=[pl.BlockSpec((tm, d), lambda i, j: (i, 0)),
                  pl.BlockSpec((1, 1, mod3.shape[2]), lambda i, j: (mod_row_of_tile(i), 0, 0)),
                  pl.BlockSpec((1, d), lambda i, j: (0, 0)),
                  pl.BlockSpec((d, tn), lambda i, j: (0, j)),
                  pl.BlockSpec((d, LANES), lambda i, j: (0, 0))],
        out_specs=[pl.BlockSpec((tm, tn), lambda i, j: (i, j)),
                   pl.BlockSpec((tm, LANES), lambda i, j: (i, 0))],
        out_shape=[jax.ShapeDtypeStruct((t, n), BF16), jax.ShapeDtypeStruct((t, LANES), F32)],
        scratch_shapes=[pltpu.VMEM((tm, d), BF16)],
        compiler_params=_params(("arbitrary", "arbitrary")),
    )(x2d, mod3, norm_g.reshape(1, d), w_main, w_gate)


def _gate_kernel(g_ref, p_ref, gd_ref, ml_ref):
    rows = g_ref.shape[0]
    raw = g_ref[...] + p_ref[0:1, :]
    lane = lax.broadcasted_iota(jnp.int32, raw.shape, 1)
    sp = _softplus(raw)
    vals = jnp.where(lane < 16, _sigmoid(raw),
                     jnp.where(lane < 32, p_ref[1:2, :] * sp,
                               jnp.where(lane < 48, raw,
                                         jnp.where(lane < 64, -_softplus(-raw), 0.0))))
    ri = lax.broadcasted_iota(jnp.int32, (CHUNK, CHUNK), 0)
    ci = lax.broadcasted_iota(jnp.int32, (CHUNK, CHUNK), 1)
    tri_f = (ri >= ci).astype(F32)
    tri_b = (ri <= ci).astype(F32)
    lane_c = lax.broadcasted_iota(jnp.int32, (CHUNK, LANES), 1)
    row_c = lax.broadcasted_iota(jnp.int32, (CHUNK, LANES), 0)
    fwd_lane = (lane_c % 16) < 8
    for c in range(rows // CHUNK):
        blk = vals[c * CHUNK:(c + 1) * CHUNK, :]
        cum = jnp.where(fwd_lane, _dot(tri_f, blk, HI), _dot(tri_b, blk, HI))
        gd_ref[c * CHUNK:(c + 1) * CHUNK, :] = jnp.where(lane_c < 16, blk, jnp.where(lane_c < 32, cum, 0.0))
        bcum = pltpu.roll(cum, LANES - 16, axis=1)
        gmb = blk - bcum
        cmf, cmb = gmb, gmb
        for s in (1, 2, 4, 8, 16, 32):
            cmf = jnp.maximum(cmf, jnp.where(row_c >= s, pltpu.roll(cmf, s, axis=0), -jnp.inf))
            cmb = jnp.maximum(cmb, jnp.where(row_c < CHUNK - s, pltpu.roll(cmb, CHUNK - s, axis=0), -jnp.inf))
        cm = jnp.where(fwd_lane, cmf, cmb)
        ml = jnp.where(lane_c < 16, pltpu.roll(gmb, LANES - 32, axis=1),
                       jnp.where(lane_c < 32, pltpu.roll(cm, LANES - 16, axis=1),
                                 jnp.where(lane_c < 48, bcum, 0.0)))
        ml_ref[c * CHUNK:(c + 1) * CHUNK, :] = ml


def _gate_prep(graw, gparams, tm=256):
    t = graw.shape[0]
    spec = pl.BlockSpec((tm, LANES), lambda i: (i, 0))
    return pl.pallas_call(
        _gate_kernel,
        grid=(t // tm,),
        in_specs=[spec, pl.BlockSpec((8, LANES), lambda i: (0, 0))],
        out_specs=[spec, spec],
        out_shape=[jax.ShapeDtypeStruct((t, LANES), F32)] * 2,
        compiler_params=_params(("arbitrary",)),
    )(graw, gparams)


def _split3(a):
    h = a.astype(BF16)
    r = a - h.astype(F32)
    m = r.astype(BF16)
    return h, m, (r - m.astype(F32)).astype(BF16)


def _lane_pick2(parts, lane_a, lane_b):
    li = lax.broadcasted_iota(jnp.int32, (LANES, 2 * LANES), 0)
    ci = lax.broadcasted_iota(jnp.int32, (LANES, 2 * LANES), 1)
    sel = (li == jnp.where(ci < LANES, lane_a, lane_b)).astype(BF16)
    h, m, lo = parts
    both = _dot(h, sel) + (_dot(m, sel) + _dot(lo, sel))
    return both[:, 0:LANES], both[:, LANES:2 * LANES]


def _dir_masks(direction):
    ri = lax.broadcasted_iota(jnp.int32, (CHUNK, CHUNK), 0)
    ci = lax.broadcasted_iota(jnp.int32, (CHUNK, CHUNK), 1)
    if direction == 0:
        return ri >= ci, ri > ci
    return ri <= ci, ri < ci


def _gdn_kernel(qc_ref, kc_ref, vc_ref, ql_ref, kl_ref, vl_ref, z_ref, cwq_ref, cwk_ref, cwv_ref,
                gdc_ref, gdl_ref, ng_ref, y_ref,
                xpad, qs, ks, vs, beta_t, cg_t, wq_r, u_r, kd_r, qk_r, dc_r, out_s):
    lc = qc_ref.shape[1]
    ll = ql_ref.shape[1]
    lt = lc + ll
    n_c, n_l = lc // CHUNK, ll // CHUNK
    n_t = n_c + n_l
    rb = 256

    def l2n(x):
        return x * lax.rsqrt(jnp.sum(x * x, axis=-1, keepdims=True) + EPS)

    def prep(src_ref, cw_ref, dst, off, ls, kind):
        xpad[0:8, :] = jnp.zeros((8, LANES), F32)
        xpad[8:8 + ls, :] = src_ref[0].astype(F32)
        xpad[8 + ls:16 + ls, :] = jnp.zeros((8, LANES), F32)
        step = min(rb, ls)
        for r0 in range(0, ls, step):
            acc = jnp.zeros((step, LANES), F32)
            for t in range(GDN_CONV):
                s0 = r0 + 8 - GDN_CONV // 2 + t
                acc = acc + cw_ref[t:t + 1, :] * xpad[s0:s0 + step, :]
            y = acc * _sigmoid(acc)
            if kind == "q":
                y = l2n(y) * (GDN_DK ** -0.5)
            elif kind == "k":
                y = l2n(y)
            dst[off + r0:off + r0 + step, :] = y

    prep(qc_ref, cwq_ref, qs, 0, lc, "q")
    prep(kc_ref, cwk_ref, ks, 0, lc, "k")
    prep(vc_ref, cwv_ref, vs, 0, lc, "v")
    prep(ql_ref, cwq_ref, qs, lc, ll, "q")
    prep(kl_ref, cwk_ref, ks, lc, ll, "k")
    prep(vl_ref, cwv_ref, vs, lc, ll, "v")

    eye = (lax.broadcasted_iota(jnp.int32, (CHUNK, CHUNK), 0)
           == lax.broadcasted_iota(jnp.int32, (CHUNK, CHUNK), 1)).astype(F32)

    masks = (_dir_masks(0), _dir_masks(1))
    head = pl.program_id(1)

    def build_tables(src_ref, off, ls):
        step = min(rb, ls)
        for r0 in range(0, ls, step):
            parts = _split3(src_ref[0, r0:r0 + step, :])
            for d in range(2):
                beta, cg = _lane_pick2(parts, 8 * d + head, 16 + 8 * d + head)
                beta_t[d, off + r0:off + r0 + step, :] = beta
                cg_t[d, off + r0:off + r0 + step, :] = cg

    build_tables(gdc_ref, 0, lc)
    build_tables(gdl_ref, lc, ll)

    def bwd_chunk(t):
        return jnp.where(t < n_c, n_c - 1 - t, n_t + n_c - 1 - t)

    def prep_chain(t, d):
        incl, strict = masks[d]
        last = CHUNK - 1 if d == 0 else 0
        tc = jnp.minimum(t, n_t - 1)
        c = tc if d == 0 else bwd_chunk(tc)
        r0 = pl.multiple_of(c * CHUNK, CHUNK)
        q = qs[pl.ds(r0, CHUNK), :]
        k = ks[pl.ds(r0, CHUNK), :]
        v = vs[pl.ds(r0, CHUNK), :]
        beta = beta_t[d, pl.ds(r0, CHUNK), :]
        cgc = cg_t[d, pl.ds(r0, CHUNK), :]
        cgr = jnp.transpose(cgc)[0:CHUNK, :]
        kk = _dot_nt(k, k)
        qk = _dot_nt(q, k)
        slot = (t % RING) * 2 + d
        yield
        cg_last = cgc[last:last + 1, :]
        decay = jnp.exp(jnp.where(incl, cgc[:, 0:CHUNK] - cgr, -jnp.inf))
        ecg = jnp.exp(cgc)
        kb = k * beta
        qk_r[slot] = qk * decay
        wq_r[slot, CHUNK:2 * CHUNK, :] = q * ecg
        kd_r[slot] = k * jnp.exp(cg_last - cgc)
        dc_r[slot] = jnp.broadcast_to(jnp.exp(cg_last), (8, LANES))
        x = jnp.where(strict, -(beta[:, 0:CHUNK] * kk) * decay, 0.0)
        tinv = eye + x
        xs = _split(x)
        x = _dot3(xs, xs)
        yield
        for _ in range(4):
            xs = _split(x)
            tinv, x = tinv + _dot3(_split(tinv), xs), _dot3(xs, xs)
            yield
        tinv = tinv + _dot3(_split(tinv), _split(x))
        yield
        wq_r[slot, 0:CHUNK, :] = _dot(tinv, kb * ecg)
        u_r[slot] = _dot(tinv, v * beta)

    out_s[...] = jnp.zeros(out_s.shape, F32)

    def scan_chain(d, t0, s, with_out, result):
        for j in range(GDN_STEPS):
            t = t0 + j
            slot = (t % RING) * 2 + d
            ws = _dot(wq_r[slot], s)
            yield
            v_new = u_r[slot] - ws[0:CHUNK, :]
            if with_out:
                c = t if d == 0 else bwd_chunk(t)
                o = ws[CHUNK:2 * CHUNK, :] + _dot(qk_r[slot], v_new)
                l0 = pl.multiple_of((c - n_c) * CHUNK, CHUNK)
                out_s[pl.ds(l0, CHUNK), :] += o
            s = s * dc_r[slot][0:1, :] + _dot_tn(kd_r[slot], v_new)
            yield
        result[d] = s

    def lockstep(chains):
        chains = list(chains)
        while chains:
            alive = []
            for ch in chains:
                try:
                    next(ch)
                    alive.append(ch)
                except StopIteration:
                    pass
            chains = alive

    def pair_body(i, carry, t_base, with_out):
        t0 = t_base + GDN_STEPS * i
        result = [None, None]
        lockstep([scan_chain(0, t0, carry[0], with_out, result), scan_chain(1, t0, carry[1], with_out, result)]
                 + [prep_chain(t0 + GDN_STEPS + j, d) for j in range(GDN_STEPS) for d in range(2)])
        return result[0], result[1]

    lockstep([prep_chain(j, d) for j in range(GDN_STEPS) for d in range(2)])
    s0 = jnp.zeros((GDN_DK, HEAD_V), F32)
    carry = lax.fori_loop(0, n_c // GDN_STEPS, functools.partial(pair_body, t_base=0, with_out=False), (s0, s0))
    lax.fori_loop(0, n_l // GDN_STEPS, functools.partial(pair_body, t_base=n_c, with_out=True), carry)


    def out_body(i, carry):
        r0 = pl.multiple_of(i * rb, rb)
        o = out_s[pl.ds(r0, rb), :]
        z = z_ref[0, pl.ds(r0, rb), :].astype(F32)
        y = o * lax.rsqrt(jnp.mean(o * o, axis=-1, keepdims=True) + EPS) * ng_ref[...]
        y_ref[0, pl.ds(r0, rb), :] = (y * (z * _sigmoid(z))).astype(y_ref.dtype)
        return carry

    lax.fori_loop(0, ll // rb, out_body, 0)


def _gdn(proj_c, proj_l, conv_w8, gd_c, gd_l, norm_g):
    b, lc, _ = proj_c.shape
    ll = proj_l.shape[1]
    lt = lc + ll
    n_t = lt // CHUNK
    qb, kb_, vb, zb = (COL_GDN_QKV // LANES, COL_GDN_QKV // LANES + HEADS, COL_GDN_QKV // LANES + 2 * HEADS,
                       COL_GDN_Z // LANES)

    def seq_spec(l, col0):
        return pl.BlockSpec((1, l, LANES), lambda i, h: (i, 0, col0 + h))

    def cw_spec(col0):
        return pl.BlockSpec((8, LANES), lambda i, h: (0, col0 + h))

    return pl.pallas_call(
        _gdn_kernel,
        grid=(b, HEADS),
        in_specs=[seq_spec(lc, qb), seq_spec(lc, kb_), seq_spec(lc, vb),
                  seq_spec(ll, qb), seq_spec(ll, kb_), seq_spec(ll, vb), seq_spec(ll, zb),
                  cw_spec(0), cw_spec(HEADS), cw_spec(2 * HEADS),
                  pl.BlockSpec((1, lc, LANES), lambda i, h: (i, 0, 0)),
                  pl.BlockSpec((1, ll, LANES), lambda i, h: (i, 0, 0)),
                  pl.BlockSpec((1, LANES), lambda i, h: (0, 0))],
        out_specs=pl.BlockSpec((1, ll, LANES), lambda i, h: (i, 0, h)),
        out_shape=jax.ShapeDtypeStruct((b, ll, HEADS * HEAD_V), BF16),
        scratch_shapes=[pltpu.VMEM((max(lc, ll) + 16, LANES), F32),
                        pltpu.VMEM((lt, LANES), F32), pltpu.VMEM((lt, LANES), F32), pltpu.VMEM((lt, LANES), F32),
                        pltpu.VMEM((2, lt, LANES), F32), pltpu.VMEM((2, lt, LANES), F32),
                        pltpu.VMEM((2 * RING, 2 * CHUNK, LANES), F32),
                        pltpu.VMEM((2 * RING, CHUNK, LANES), F32), pltpu.VMEM((2 * RING, CHUNK, LANES), F32),
                        pltpu.VMEM((2 * RING, CHUNK, CHUNK), F32),
                        pltpu.VMEM((2 * RING, 8, LANES), F32),
                        pltpu.VMEM((ll, LANES), F32)],
        compiler_params=_params(("arbitrary", "arbitrary")),
    )(proj_c, proj_c, proj_c, proj_l, proj_l, proj_l, proj_l, conv_w8, conv_w8, conv_w8,
      gd_c, gd_l, norm_g.reshape(1, LANES))


def _mlstm_kernel(qc_ref, kc_ref, vc_ref, ql_ref, kl_ref, vl_ref, mlc_ref, mll_ref, h_ref, out_s, tabs):
    lc = qc_ref.shape[1]
    ll = ql_ref.shape[1]
    n_c, n_l = lc // CHUNK, ll // CHUNK
    n_t = n_c + n_l
    lt = lc + ll
    pair = pl.program_id(1)
    lane = lax.broadcasted_iota(jnp.int32, (CHUNK, LANES), 1)
    ones_v = jnp.ones((CHUNK, HEAD_V), BF16)
    chains = [(hh, d) for hh in range(2) for d in range(2)]
    hmask = [((lane // ML_DK) == hh).astype(F32) for hh in range(2)]
    incl = [_dir_masks(d)[0] for d in range(2)]

    def build_tables(src_ref, off, ls):
        step = min(256, ls)
        for r0 in range(0, ls, step):
            parts = _split3(src_ref[0, r0:r0 + step, :])
            lanes = [16 * j + 8 * d + 2 * pair + hh for hh, d in chains for j in range(3)]
            for g in range(0, len(lanes), 2):
                ta, tb = _lane_pick2(parts, lanes[g], lanes[g + 1])
                tabs[g, off + r0:off + r0 + step, :] = ta
                tabs[g + 1, off + r0:off + r0 + step, :] = tb

    build_tables(mlc_ref, 0, lc)
    build_tables(mll_ref, lc, ll)

    def wide(a):
        return jnp.concatenate([a, a], axis=1)

    def chain(hh, d, c, state, is_ctx, result):
        cs, ms = state
        last = CHUNK - 1 if d == 0 else 0
        if is_ctx:
            rows = pl.ds(pl.multiple_of(c * CHUNK, CHUNK), CHUNK)
            q_ref, k_ref, v_ref = qc_ref, kc_ref, vc_ref
        else:
            rows = pl.ds(pl.multiple_of((c - n_c) * CHUNK, CHUNK), CHUNK)
            q_ref, k_ref, v_ref = ql_ref, kl_ref, vl_ref
        q = (q_ref[0, rows, :].astype(F32) * hmask[hh]).astype(BF16)
        k = k_ref[0, rows, :].astype(F32) * (hmask[hh] * (ML_DK ** -0.5))
        v = jnp.concatenate([v_ref[0, rows, hh * HEAD_V:(hh + 1) * HEAD_V], ones_v], axis=1)
        n = chains.index((hh, d))
        trows = pl.ds(pl.multiple_of(c * CHUNK, CHUNK), CHUNK)
        gmb = tabs[3 * n, trows, :]
        gmb_t = jnp.transpose(gmb)[0:CHUNK, :]
        cm = tabs[3 * n + 1, trows, :]
        bc = tabs[3 * n + 2, trows, :]
        qk = _dot_nt(q, k.astype(BF16))
        yield
        cm_last = cm[last:last + 1, :]
        b_last = bc[last:last + 1, :]
        mm = jnp.maximum(ms, cm)
        p = jnp.where(incl[d], jnp.exp(gmb_t - mm[:, 0:CHUNK]), 0.0) * qk
        wk = (k * jnp.exp(gmb - cm_last)).astype(BF16)
        inter = _dot(q, cs.astype(BF16))
        intra = _dot(p.astype(BF16), v)
        c_loc = _dot_tn(wk, v)
        yield
        if not is_ctx:
            nd = wide(jnp.exp(ms - mm)) * inter + intra
            hv = nd[:, 0:HEAD_V] / jnp.maximum(jnp.abs(nd[:, HEAD_V:2 * HEAD_V]), jnp.exp(-(bc + mm)))
            l0 = pl.multiple_of((c - n_c) * CHUNK, CHUNK)
            out_s[pl.ds(l0, CHUNK), hh * HEAD_V:(hh + 1) * HEAD_V] += hv
        mx = jnp.maximum(ms, cm_last)
        result[hh, d] = (wide(jnp.exp(ms - mx)) * cs + wide(jnp.exp(cm_last - mx)) * c_loc, b_last + mx)

    out_s[...] = jnp.zeros(out_s.shape, F32)

    def run(chains_iter):
        live = list(chains_iter)
        while live:
            alive = []
            for ch in live:
                try:
                    next(ch)
                    alive.append(ch)
                except StopIteration:
                    pass
            live = alive

    def body(i, carry, is_ctx):
        result = {}
        gens = []
        for n, (hh, d) in enumerate(chains):
            if is_ctx:
                c = i if d == 0 else n_c - 1 - i
            else:
                c = n_c + i if d == 0 else n_t - 1 - i
            gens.append(chain(hh, d, c, carry[n], is_ctx, result))
        run(gens)
        return tuple(result[hd] for hd in chains)

    st0 = (jnp.zeros((LANES, 2 * HEAD_V), F32), jnp.zeros((1, LANES), F32))
    carry = lax.fori_loop(0, n_c, functools.partial(body, is_ctx=True), (st0,) * 4)
    lax.fori_loop(0, n_l, functools.partial(body, is_ctx=False), carry)
    h_ref[0] = out_s[...].astype(h_ref.dtype)


def _mlstm(proj_c, q_l, k_l, v_l, ml_c, ml_l):
    b, lc, _ = proj_c.shape
    ll = q_l.shape[1]
    lt = lc + ll
    qb, kb_, vb = COL_ML_Q // LANES, COL_ML_K // LANES, COL_ML_V // (2 * HEAD_V)
    return pl.pallas_call(
        _mlstm_kernel,
        grid=(b, HEADS // 2),
        in_specs=[pl.BlockSpec((1, lc, LANES), lambda i, p: (i, 0, qb + p)),
                  pl.BlockSpec((1, lc, LANES), lambda i, p: (i, 0, kb_ + p)),
                  pl.BlockSpec((1, lc, 2 * HEAD_V), lambda i, p: (i, 0, vb + p)),
                  pl.BlockSpec((1, ll, LANES), lambda i, p: (i, 0, p)),
                  pl.BlockSpec((1, ll, LANES), lambda i, p: (i, 0, p)),
                  pl.BlockSpec((1, ll, 2 * HEAD_V), lambda i, p: (i, 0, p)),
                  pl.BlockSpec((1, lc, LANES), lambda i, p: (i, 0, 0)),
                  pl.BlockSpec((1, ll, LANES), lambda i, p: (i, 0, 0))],
        out_specs=pl.BlockSpec((1, ll, 2 * HEAD_V), lambda i, p: (i, 0, p)),
        out_shape=jax.ShapeDtypeStruct((b, ll, HEADS * HEAD_V), BF16),
        scratch_shapes=[pltpu.VMEM((ll, 2 * HEAD_V), F32), pltpu.VMEM((12, lt, LANES), F32)],
        compiler_params=_params(("arbitrary", "arbitrary")),
    )(proj_c, proj_c, proj_c, q_l, k_l, v_l, ml_c, ml_l)


def _merge_kernel(yg_ref, hm_ref, o_ref, gg_ref, gm_ref, x_ref, mod_ref, mlg_ref, n2_ref,
                  wbg_ref, wbm_ref, wo_ref, wrh_ref, wrl_ref, x1_ref, h2_ref, sc_ref):
    d = x_ref.shape[1]
    o = o_ref[...].astype(F32)
    ym = _sigmoid(o) * hm_ref[...].astype(F32)
    segs = []
    for h in range(HEADS):
        seg = ym[:, h * HEAD_V:(h + 1) * HEAD_V]
        segs.append(seg * lax.rsqrt(jnp.mean(seg * seg, axis=-1, keepdims=True) + EPS))
    ymn = jnp.concatenate(segs, axis=1) * mlg_ref[...]
    y_gdn = _dot(yg_ref[...], wbg_ref[...])
    y_ml = _dot(ymn.astype(BF16), wbm_ref[...])
    mixed = _sigmoid(gg_ref[...].astype(F32)) * y_gdn + _sigmoid(gm_ref[...].astype(F32)) * y_ml
    y = _dot(mixed.astype(BF16), wo_ref[...])
    x1 = x_ref[...] + mod_ref[0, :, 2 * d:3 * d] * y
    x1_ref[...] = x1
    hn = x1 * lax.rsqrt(jnp.mean(x1 * x1, axis=-1, keepdims=True) + EPS) * n2_ref[...]
    h2 = hn * (1.0 + mod_ref[0, :, 4 * d:5 * d]) + mod_ref[0, :, 3 * d:4 * d]
    h2_hi = h2.astype(BF16)
    h2_ref[...] = h2_hi
    h2_lo = (h2 - h2_hi.astype(F32)).astype(BF16)
    logits = _dot_nt(wrh_ref[...], h2_hi) + (_dot_nt(wrl_ref[...], h2_hi) + _dot_nt(wrh_ref[...], h2_lo))
    sc_ref[...] = _sigmoid(logits)


def _merge(y_gdn, h_ml, proj_l2d, x2d, mod3, rows_per_mod, ml_norm_g, norm2_g, wbg, wbm, wo, wr_hi, wr_lo, tm=512):
    t, d = x2d.shape
    e = wr_hi.shape[0]
    row = lambda i: (i, 0)
    const = lambda i: (0, 0)
    return pl.pallas_call(
        _merge_kernel,
        grid=(t // tm,),
        in_specs=[pl.BlockSpec((tm, d), row), pl.BlockSpec((tm, d), row),
                  pl.BlockSpec((tm, d), lambda i: (i, COL_ML_O // d)),
                  pl.BlockSpec((tm, d), lambda i: (i, COL_MG_GDN // d)),
                  pl.BlockSpec((tm, d), lambda i: (i, COL_MG_ML // d)),
                  pl.BlockSpec((tm, d), row),
                  pl.BlockSpec((1, 1, mod3.shape[2]), lambda i: ((i * tm) // rows_per_mod, 0, 0)),
                  pl.BlockSpec((1, d), const), pl.BlockSpec((1, d), const),
                  pl.BlockSpec((d, d), const), pl.BlockSpec((d, d), const), pl.BlockSpec((d, d), const),
                  pl.BlockSpec((e, d), const), pl.BlockSpec((e, d), const)],
        out_specs=[pl.BlockSpec((tm, d), row), pl.BlockSpec((tm, d), row), pl.BlockSpec((e, tm), lambda i: (0, i))],
        out_shape=[jax.ShapeDtypeStruct((t, d), F32), jax.ShapeDtypeStruct((t, d), BF16),
                   jax.ShapeDtypeStruct((e, t), F32)],
        compiler_params=_params(("arbitrary",)),
    )(y_gdn, h_ml, proj_l2d, proj_l2d, proj_l2d, x2d, mod3, ml_norm_g.reshape(1, d), norm2_g.reshape(1, d),
      wbg, wbm, wo, wr_hi, wr_lo)


def _expert_kernel(be_ref, nu_ref, x_ref, wgu_ref, wd_ref, y_ref, wgu_s, wd_s):
    i = pl.program_id(0)
    de = wd_ref.shape[1]
    changed = jnp.logical_or(i == 0, be_ref[i] != be_ref[jnp.maximum(i - 1, 0)])

    @pl.when(changed)
    def _():
        wgu_s[...] = wgu_ref[0].astype(BF16)
        wd_s[...] = wd_ref[0].astype(BF16)

    @pl.when(i < nu_ref[0])
    def _():
        gu = _dot(x_ref[...], wgu_s[...])
        g = gu[:, 0:de]
        act = (g * _sigmoid(g)) * gu[:, de:2 * de]
        y_ref[...] = _dot(act.astype(BF16), wd_s[...]).astype(y_ref.dtype)

    @pl.when(i >= nu_ref[0])
    def _():
        y_ref[...] = jnp.zeros(y_ref.shape, y_ref.dtype)


def _experts(xb, blk_expert, n_used, w_gu, w_down):
    n_slots, d = xb.shape
    n_blocks = n_slots // EXPERT_ROWS
    e, _, de2 = w_gu.shape
    de = de2 // 2
    return pl.pallas_call(
        _expert_kernel,
        grid_spec=pltpu.PrefetchScalarGridSpec(
            num_scalar_prefetch=2,
            grid=(n_blocks,),
            in_specs=[pl.BlockSpec((EXPERT_ROWS, d), lambda i, be, nu: (i, 0)),
                      pl.BlockSpec((1, d, de2), lambda i, be, nu: (be[i], 0, 0)),
                      pl.BlockSpec((1, de, d), lambda i, be, nu: (be[i], 0, 0))],
            out_specs=pl.BlockSpec((EXPERT_ROWS, d), lambda i, be, nu: (i, 0)),
            scratch_shapes=[pltpu.VMEM((d, de2), BF16), pltpu.VMEM((de, d), BF16)]),
        out_shape=jax.ShapeDtypeStruct((n_slots, d), BF16),
        compiler_params=_params(("arbitrary",)),
    )(blk_expert, n_used, xb, w_gu, w_down)


def _final_kernel(x1_ref, h2_ref, yg_ref, wt_ref, mod_ref, wsg_ref, wsd_ref, fg_ref, o_ref):
    d = x1_ref.shape[1]
    ds_ = wsd_ref.shape[0]
    wt = wt_ref[...]
    routed = jnp.zeros(x1_ref.shape, F32)
    for k in range(TOP_K):
        routed = routed + wt[:, k:k + 1] * yg_ref[k].astype(F32)
    gu = _dot(h2_ref[...], wsg_ref[...])
    g = gu[:, 0:ds_]
    sh = _dot(((g * _sigmoid(g)) * gu[:, ds_:2 * ds_]).astype(BF16), wsd_ref[...])
    x2 = x1_ref[...] + mod_ref[0, :, 5 * d:6 * d] * (routed + sh)
    o_ref[...] = x2 * lax.rsqrt(jnp.mean(x2 * x2, axis=-1, keepdims=True) + EPS) * fg_ref[...]


def _final(x1, h2, yg, wts, mod3, rows_per_mod, w_sh_gu, w_sh_down, final_g, tm=256):
    t, d = x1.shape
    row = lambda i: (i, 0)
    const = lambda i: (0, 0)
    return pl.pallas_call(
        _final_kernel,
        grid=(t // tm,),
        in_specs=[pl.BlockSpec((tm, d), row), pl.BlockSpec((tm, d), row),
                  pl.BlockSpec((TOP_K, tm, d), lambda i: (0, i, 0)), pl.BlockSpec((tm, TOP_K), row),
                  pl.BlockSpec((1, 1, mod3.shape[2]), lambda i: ((i * tm) // rows_per_mod, 0, 0)),
                  pl.BlockSpec(w_sh_gu.shape, const), pl.BlockSpec(w_sh_down.shape, const),
                  pl.BlockSpec((1, d), const)],
        out_specs=pl.BlockSpec((tm, d), row),
        out_shape=jax.ShapeDtypeStruct((t, d), F32),
        compiler_params=_params(("arbitrary",)),
    )(x1, h2, yg, wts, mod3, w_sh_gu, w_sh_down, final_g.reshape(1, d))


def _route_kernel(sc_ref, bias_ref, tri_ref, idx_ref, wt_ref, rk_ref, cnt_ref, base_s):
    @pl.when(pl.program_id(0) == 0)
    def _():
        base_s[...] = jnp.zeros(base_s.shape, F32)

    scores = sc_ref[...]
    e, tn = scores.shape
    gsz = e // N_GROUPS
    sel3 = (scores + bias_ref[...]).reshape(N_GROUPS, gsz, tn)
    m1 = jnp.max(sel3, axis=1)
    is_max = sel3 == m1[:, None, :]
    n_max = jnp.sum(is_max.astype(F32), axis=1)
    m2 = jnp.max(jnp.where(is_max, -jnp.inf, sel3), axis=1)
    grp = m1 + jnp.where(n_max >= 2.0, m1, m2)
    gi = lax.broadcasted_iota(jnp.int32, (N_GROUPS, tn), 0)
    ahead = jnp.zeros((N_GROUPS, tn), F32)
    for g in range(N_GROUPS):
        row = grp[g:g + 1, :]
        ahead = ahead + jnp.logical_or(row > grp, jnp.logical_and(row == grp, g < gi)).astype(F32)
    ahead3 = jnp.broadcast_to(ahead[:, None, :], (N_GROUPS, gsz, tn))
    selm = jnp.where(ahead3 < float(TOPK_GROUPS), sel3, -jnp.inf).reshape(e, tn)
    ri = lax.broadcasted_iota(jnp.int32, (e, tn), 0).astype(F32)
    member = jnp.zeros((e, tn), F32)
    idxs, ws = [], []
    for _ in range(TOP_K):
        m = jnp.max(selm, axis=0, keepdims=True)
        idx = jnp.min(jnp.where(selm == m, ri, float(e)), axis=0, keepdims=True)
        hit = ri == idx
        ws.append(jnp.sum(jnp.where(hit, scores, 0.0), axis=0, keepdims=True))
        idxs.append(idx)
        selm = jnp.where(hit, -jnp.inf, selm)
        member = jnp.where(hit, 1.0, member)
    w = jnp.concatenate(ws, axis=0)
    wt_ref[...] = w / jnp.sum(w, axis=0, keepdims=True) * ROUTED_SCALE
    idx_ref[...] = jnp.concatenate(idxs, axis=0).astype(jnp.int32)
    cum = _dot(member.astype(BF16), tri_ref[...]) + base_s[...]
    rk_ref[...] = jnp.concatenate(
        [jnp.sum(jnp.where(ri == idx, cum, 0.0), axis=0, keepdims=True) for idx in idxs], axis=0).astype(jnp.int32)
    total = base_s[...] + jnp.sum(member, axis=1, keepdims=True)
    base_s[...] = total
    cnt_ref[...] = total


def _route(scores_t, router_bias):
    e, t = scores_t.shape
    tn = LANES
    bias = jnp.broadcast_to(router_bias.astype(F32)[:, None], (e, tn))
    tri = (jnp.arange(tn)[:, None] < jnp.arange(tn)[None, :]).astype(BF16)
    tok = pl.BlockSpec((TOP_K, tn), lambda i: (0, i))
    const = lambda i: (0, 0)
    return pl.pallas_call(
        _route_kernel,
        grid=(t // tn,),
        in_specs=[pl.BlockSpec((e, tn), lambda i: (0, i)), pl.BlockSpec((e, tn), const),
                  pl.BlockSpec((tn, tn), const)],
        out_specs=[tok, tok, tok, pl.BlockSpec((e, tn), const)],
        out_shape=[jax.ShapeDtypeStruct((TOP_K, t), jnp.int32), jax.ShapeDtypeStruct((TOP_K, t), F32),
                   jax.ShapeDtypeStruct((TOP_K, t), jnp.int32), jax.ShapeDtypeStruct((e, tn), F32)],
        scratch_shapes=[pltpu.VMEM((e, tn), F32)],
        compiler_params=_params(("arbitrary",)),
    )(scores_t, bias, tri)


def _slot_kernel(idx_ref, rk_ref, ps_ref, pos_ref):
    e, tn = ps_ref.shape
    ri = lax.broadcasted_iota(jnp.int32, (e, tn), 0)
    ps = ps_ref[...]
    rows = [jnp.sum(jnp.where(ri == idx_ref[k:k + 1, :], ps, 0.0), axis=0, keepdims=True) for k in range(TOP_K)]
    pos_ref[...] = rk_ref[...] + jnp.concatenate(rows, axis=0).astype(jnp.int32)


def _slots(idx, rank, pstart):
    k, t = idx.shape
    e = pstart.shape[0]
    tn = LANES
    tok = pl.BlockSpec((k, tn), lambda i: (0, i))
    return pl.pallas_call(
        _slot_kernel,
        grid=(t // tn,),
        in_specs=[tok, tok, pl.BlockSpec((e, tn), lambda i: (0, 0))],
        out_specs=tok,
        out_shape=jax.ShapeDtypeStruct((k, t), jnp.int32),
        compiler_params=_params(("arbitrary",)),
    )(idx, rank, jnp.broadcast_to(pstart.astype(F32)[:, None], (e, tn)))


def _block_table(counts, n_blocks):
    padded = (counts + EXPERT_ROWS - 1) // EXPERT_ROWS * EXPERT_ROWS
    pend = jnp.cumsum(padded)
    blk_expert = jnp.minimum(jnp.searchsorted(pend, jnp.arange(n_blocks) * EXPERT_ROWS, side='right'),
                             N_EXPERTS - 1).astype(jnp.int32)
    return pend - padded, blk_expert, (pend[-1] // EXPERT_ROWS).astype(jnp.int32).reshape(1)


def _col_major(t):
    b, l, f = t.shape
    rows = l // GRID_W
    return t.reshape(b, rows, GRID_W, f).transpose(0, 2, 1, 3).reshape(b, l, f)


def _row_major(t):
    b, l, f = t.shape
    rows = l // GRID_W
    return t.reshape(b, GRID_W, rows, f).transpose(0, 2, 1, 3).reshape(b, l, f)


def kernel(x, c, ctx, c_ctx, w_ada, b_ada, norm1_g, norm2_g, w_in, gdn_conv_w, gdn_a_log, gdn_dt_bias, gdn_norm_g,
           ml_i_bias, ml_f_bias, ml_norm_g, w_branch_gdn, w_branch_ml, w_out, w_router, router_bias, w_exp_gate_up,
           w_exp_down, w_sh_gate_up, w_sh_down, final_norm_g):
    b, l, d = x.shape
    lc = ctx.shape[1]
    t = b * l
    layer = 0

    w = w_in[layer]
    main_cols = [_ORIG[k] for k in ("gdn_qkv", "gdn_z", "ml_q", "ml_k", "ml_v", "ml_o", "mg_gdn", "mg_ml")]
    w_main = jnp.concatenate([w[:, a:e] for a, e in main_cols], axis=1).astype(BF16)
    w_gate = jnp.concatenate([w[:, _ORIG["gdn_gate"][0]:_ORIG["gdn_gate"][1]],
                              w[:, _ORIG["ml_gate"][0]:_ORIG["ml_gate"][1]],
                              jnp.zeros((d, LANES - 64), F32)], axis=1).astype(BF16)
    zeros16 = jnp.zeros((16,), F32)
    gp_add = jnp.concatenate([zeros16, gdn_dt_bias[layer].reshape(-1), ml_i_bias[layer].reshape(-1),
                              ml_f_bias[layer].reshape(-1), jnp.zeros((LANES - 64,), F32)])
    gp_mul = jnp.concatenate([zeros16, -jnp.exp(gdn_a_log[layer].astype(F32)).reshape(-1),
                              jnp.zeros((LANES - 32,), F32)])
    gparams = jnp.zeros((8, LANES), F32).at[0].set(gp_add).at[1].set(gp_mul)
    conv_w8 = jnp.zeros((8, gdn_conv_w.shape[2]), F32).at[0:GDN_CONV].set(gdn_conv_w[layer])
    wr = w_router[layer].T
    wr_hi = wr.astype(BF16)
    wr_lo = (wr - wr_hi.astype(F32)).astype(BF16)

    n_mod_rows = -(-(b + 1) // 8) * 8
    cc = jnp.zeros((n_mod_rows, d), F32).at[0:b].set(c).at[b].set(c_ctx)
    mod = _ada_mod(cc, w_ada[layer], b_ada[layer])
    mod3 = mod.reshape(n_mod_rows, 1, 6 * d)

    x2d = x.reshape(t, d)
    tm_l = min(1024, l)
    proj_l, gate_l = _project(x2d, mod3, lambda i: (i * tm_l) // l, norm1_g[layer], w_main, w_gate, tm_l)
    tm_c = min(1024, b * lc)
    proj_c, gate_c = _project(ctx.reshape(b * lc, d), mod3, lambda i: b, norm1_g[layer], w_main, w_gate, tm_c)
    proj_l3 = proj_l.reshape(b, l, N_MAIN)
    proj_c3 = proj_c.reshape(b, lc, N_MAIN)

    gate_l_cm = _col_major(gate_l.reshape(b, l, LANES)).reshape(t, LANES)
    gd_c, ml_c = _gate_prep(gate_c, gparams)
    gd_l, _ = _gate_prep(gate_l, gparams)
    _, ml_l = _gate_prep(gate_l_cm, gparams)

    y_gdn = _gdn(proj_c3, proj_l3, conv_w8, gd_c.reshape(b, lc, LANES), gd_l.reshape(b, l, LANES),
                 gdn_norm_g[layer])
    q_cm = _col_major(proj_l3[:, :, COL_ML_Q:COL_ML_Q + HEADS * ML_DK])
    k_cm = _col_major(proj_l3[:, :, COL_ML_K:COL_ML_K + HEADS * ML_DK])
    v_cm = _col_major(proj_l3[:, :, COL_ML_V:COL_ML_V + HEADS * HEAD_V])
    h_ml = _row_major(_mlstm(proj_c3, q_cm, k_cm, v_cm, ml_c.reshape(b, lc, LANES), ml_l.reshape(b, l, LANES)))

    x1, h2, scores_t = _merge(y_gdn.reshape(t, d), h_ml.reshape(t, d), proj_l, x2d, mod3, l, ml_norm_g[layer],
                            norm2_g[layer], w_branch_gdn[layer].astype(BF16), w_branch_ml[layer].astype(BF16),
                            w_out[layer].astype(BF16), wr_hi, wr_lo, tm=min(512, l))

    idx, wts, rank, cnt = _route(scores_t, router_bias[layer])
    n_assign = t * TOP_K
    n_blocks = (n_assign + N_EXPERTS * (EXPERT_ROWS - 1)) // EXPERT_ROWS + 1
    n_slots = n_blocks * EXPERT_ROWS
    pstart, blk_expert, n_used = _block_table(cnt[:, 0].astype(jnp.int32), n_blocks)
    pos = _slots(idx, rank, pstart).reshape(n_assign)
    tok_slot = (jnp.arange(n_slots, dtype=jnp.int32) % t).at[pos].set(jnp.arange(n_assign, dtype=jnp.int32) % t)
    xb = h2[tok_slot]
    yb = _experts(xb, blk_expert, n_used, w_exp_gate_up[layer], w_exp_down[layer])
    yg = yb[pos].reshape(TOP_K, t, d)
    out = _final(x1, h2, yg, wts.T, mod3, l, w_sh_gate_up[layer].astype(BF16), w_sh_down[layer].astype(BF16),
                 final_norm_g, tm=min(256, l))
    return out.reshape(b, l, d)
```

```python
import functools
import math

import jax
import jax.numpy as jnp
from jax import lax
from jax.experimental import pallas as pl
from jax.experimental.pallas import tpu as pltpu

F32 = jnp.float32
BF16 = jnp.bfloat16
HI = lax.Precision.HIGHEST

EPS = 1e-6
CHUNK = 64
GRID_W = 64
HEADS = 8
HEAD_V = 128
GDN_DK = 128
ML_DK = 64
GDN_CONV = 5
N_EXPERTS = 256
TOP_K = 8
N_GROUPS = 8
TOPK_GROUPS = 4
ROUTED_SCALE = 2.5
EXPERT_ROWS = 512
GDN_STEPS = 4
RING = 2 * GDN_STEPS
LANES = 128
VMEM_LIMIT = 56 * 1024 * 1024

COL_GDN_QKV = 0
COL_GDN_Z = 3072
COL_ML_Q = 4096
COL_ML_K = 4608
COL_ML_V = 5120
COL_ML_O = 6144
COL_MG_GDN = 7168
COL_MG_ML = 8192
N_MAIN = 9216
_ORIG = dict(gdn_qkv=(0, 3072), gdn_z=(3072, 4096), gdn_gate=(4096, 4128), ml_q=(4128, 4640),
             ml_k=(4640, 5152), ml_v=(5152, 6176), ml_o=(6176, 7200), ml_gate=(7200, 7232),
             mg_gdn=(7232, 8256), mg_ml=(8256, 9280))


def _params(sem, vmem=VMEM_LIMIT):
    return pltpu.CompilerParams(dimension_semantics=sem, vmem_limit_bytes=vmem)


def _dot(a, b, precision=None):
    return jnp.dot(a, b, preferred_element_type=F32, precision=precision)


def _dot_nt(a, b, precision=None):
    return lax.dot_general(a, b, (((1,), (1,)), ((), ())), preferred_element_type=F32, precision=precision)


def _dot_tn(a, b, precision=None):
    return lax.dot_general(a, b, (((0,), (0,)), ((), ())), preferred_element_type=F32, precision=precision)


def _sigmoid(x):
    return 1.0 / (1.0 + jnp.exp(-x))


def _softplus(x):
    return jnp.maximum(x, 0.0) + jnp.log(1.0 + jnp.exp(-jnp.abs(x)))


def _ada_kernel(c_ref, w_ref, b_ref, o_ref):
    c = c_ref[...]
    sc = c * _sigmoid(c)
    o_ref[...] = _dot(sc, w_ref[...], HI) + b_ref[...]


def _ada_mod(cc, w_ada, b_ada, tn=1536):
    r, d = cc.shape
    n = w_ada.shape[1]
    return pl.pallas_call(
        _ada_kernel,
        grid=(n // tn,),
        in_specs=[pl.BlockSpec((r, d), lambda j: (0, 0)),
                  pl.BlockSpec((d, tn), lambda j: (0, j)),
                  pl.BlockSpec((1, tn), lambda j: (0, j))],
        out_specs=pl.BlockSpec((r, tn), lambda j: (0, j)),
        out_shape=jax.ShapeDtypeStruct((r, n), F32),
        compiler_params=_params(("arbitrary",)),
    )(cc, w_ada, b_ada.reshape(1, n))


def _proj_kernel(x_ref, mod_ref, g_ref, w_ref, wg_ref, o_ref, og_ref, hn_ref):
    d = x_ref.shape[1]

    @pl.when(pl.program_id(1) == 0)
    def _():
        x = x_ref[...]
        y = x * lax.rsqrt(jnp.mean(x * x, axis=-1, keepdims=True) + EPS) * g_ref[...]
        shift = mod_ref[0, :, 0:d]
        scale = mod_ref[0, :, d:2 * d]
        h = (y * (1.0 + scale) + shift).astype(BF16)
        hn_ref[...] = h
        og_ref[...] = _dot(h, wg_ref[...])

    o_ref[...] = _dot(hn_ref[...], w_ref[...]).astype(o_ref.dtype)


def _project(x2d, mod3, mod_row_of_tile, norm_g, w_main, w_gate, tm, tn=1024):
    t, d = x2d.shape
    n = w_main.shape[1]
    return pl.pallas_call(
        _proj_kernel,
        grid=(t // tm, n // tn),
        in_specs=[pl.BlockSpec((tm, d), lambda i, j: (i, 0)),
                  pl.BlockSpec((1, 1, mod3.shape[2]), lambda i, j: (mod_row_of_tile(i), 0, 0)),
                  pl.BlockSpec((1, d), lambda i, j: (0, 0)),
                  pl.BlockSpec((d, tn), lambda i, j: (0, j)),
                  pl.BlockSpec((d, LANES), lambda i, j: (0, 0))],
        out_specs=[pl.BlockSpec((tm, tn), lambda i, j: (i, j)),
                   pl.BlockSpec((tm, LANES), lambda i, j: (i, 0))],
        out_shape=[jax.ShapeDtypeStruct((t, n), BF16), jax.ShapeDtypeStruct((t, LANES), F32)],
        scratch_shapes=[pltpu.VMEM((tm, d), BF16)],
        compiler_params=_params(("arbitrary", "arbitrary")),
    )(x2d, mod3, norm_g.reshape(1, d), w_main, w_gate)


def _gate_kernel(g_ref, p_ref, gd_ref, ml_ref):
    rows = g_ref.shape[0]
    raw = g_ref[...] + p_ref[0:1, :]
    lane = lax.broadcasted_iota(jnp.int32, raw.shape, 1)
    sp = _softplus(raw)
    vals = jnp.where(lane < 16, _sigmoid(raw),
                     jnp.where(lane < 32, p_ref[1:2, :] * sp,
                               jnp.where(lane < 48, raw,
                                         jnp.where(lane < 64, -_softplus(-raw), 0.0))))
    ri = lax.broadcasted_iota(jnp.int32, (CHUNK, CHUNK), 0)
    ci = lax.broadcasted_iota(jnp.int32, (CHUNK, CHUNK), 1)
    tri_f = (ri >= ci).astype(F32)
    tri_b = (ri <= ci).astype(F32)
    lane_c = lax.broadcasted_iota(jnp.int32, (CHUNK, LANES), 1)
    row_c = lax.broadcasted_iota(jnp.int32, (CHUNK, LANES), 0)
    fwd_lane = (lane_c % 16) < 8
    for c in range(rows // CHUNK):
        blk = vals[c * CHUNK:(c + 1) * CHUNK, :]
        cum = jnp.where(fwd_lane, _dot(tri_f, blk, HI), _dot(tri_b, blk, HI))
        gd_ref[c * CHUNK:(c + 1) * CHUNK, :] = jnp.where(lane_c < 16, blk, jnp.where(lane_c < 32, cum, 0.0))
        bcum = pltpu.roll(cum, LANES - 16, axis=1)
        gmb = blk - bcum
        cmf, cmb = gmb, gmb
        for s in (1, 2, 4, 8, 16, 32):
            cmf = jnp.maximum(cmf, jnp.where(row_c >= s, pltpu.roll(cmf, s, axis=0), -jnp.inf))
            cmb = jnp.maximum(cmb, jnp.where(row_c < CHUNK - s, pltpu.roll(cmb, CHUNK - s, axis=0), -jnp.inf))
        cm = jnp.where(fwd_lane, cmf, cmb)
        ml = jnp.where(lane_c < 16, pltpu.roll(gmb, LANES - 32, axis=1),
                       jnp.where(lane_c < 32, pltpu.roll(cm, LANES - 16, axis=1),
                                 jnp.where(lane_c < 48, bcum, 0.0)))
        ml_ref[c * CHUNK:(c + 1) * CHUNK, :] = ml


def _gate_prep(graw, gparams, tm=256):
    t = graw.shape[0]
    spec = pl.BlockSpec((tm, LANES), lambda i: (i, 0))
    return pl.pallas_call(
        _gate_kernel,
        grid=(t // tm,),
        in_specs=[spec, pl.BlockSpec((8, LANES), lambda i: (0, 0))],
        out_specs=[spec, spec],
        out_shape=[jax.ShapeDtypeStruct((t, LANES), F32)] * 2,
        compiler_params=_params(("arbitrary",)),
    )(graw, gparams)


def _split3(a):
    h = a.astype(BF16)
    r = a - h.astype(F32)
    m = r.astype(BF16)
    return h, m, (r - m.astype(F32)).astype(BF16)


def _lane_picks(x, lanes):
    li = lax.broadcasted_iota(jnp.int32, (LANES, LANES), 0)
    ci = lax.broadcasted_iota(jnp.int32, (LANES, LANES), 1)
    want = jnp.full((LANES, LANES), -1, jnp.int32)
    for j, lane in enumerate(lanes):
        want = jnp.where(ci == j, lane, want)
    sel = (li == want).astype(BF16)
    h, m, lo = _split3(x)
    cols = _dot(h, sel) + (_dot(m, sel) + _dot(lo, sel))
    return [jnp.broadcast_to(cols[:, j:j + 1], x.shape) for j in range(len(lanes))]


def _dir_masks(direction):
    ri = lax.broadcasted_iota(jnp.int32, (CHUNK, CHUNK), 0)
    ci = lax.broadcasted_iota(jnp.int32, (CHUNK, CHUNK), 1)
    if direction == 0:
        return ri >= ci, ri > ci
    return ri <= ci, ri < ci


def _gdn_kernel(qc_ref, kc_ref, vc_ref, ql_ref, kl_ref, vl_ref, z_ref, cwq_ref, cwk_ref, cwv_ref,
                gdc_ref, gdl_ref, ng_ref, y_ref,
                xpad, qs, ks, vs, beta_t, cg_t, wq_r, u_r, kd_r, qk_r, dc_r, out_s):
    lc = qc_ref.shape[1]
    ll = ql_ref.shape[1]
    lt = lc + ll
    n_c, n_l = lc // CHUNK, ll // CHUNK
    n_t = n_c + n_l
    rb = 256

    def l2n(x):
        return x * lax.rsqrt(jnp.sum(x * x, axis=-1, keepdims=True) + EPS)

    def prep(src_ref, cw_ref, dst, off, ls, kind):
        xpad[0:8, :] = jnp.zeros((8, LANES), F32)
        xpad[8:8 + ls, :] = src_ref[0].astype(F32)
        xpad[8 + ls:16 + ls, :] = jnp.zeros((8, LANES), F32)
        step = min(rb, ls)
        for r0 in range(0, ls, step):
            acc = jnp.zeros((step, LANES), F32)
            for t in range(GDN_CONV):
                s0 = r0 + 8 - GDN_CONV // 2 + t
                acc = acc + cw_ref[t:t + 1, :] * xpad[s0:s0 + step, :]
            y = acc * _sigmoid(acc)
            if kind == "q":
                y = l2n(y) * (GDN_DK ** -0.5)
            elif kind == "k":
                y = l2n(y)
            dst[off + r0:off + r0 + step, :] = y

    prep(qc_ref, cwq_ref, qs, 0, lc, "q")
    prep(kc_ref, cwk_ref, ks, 0, lc, "k")
    prep(vc_ref, cwv_ref, vs, 0, lc, "v")
    prep(ql_ref, cwq_ref, qs, lc, ll, "q")
    prep(kl_ref, cwk_ref, ks, lc, ll, "k")
    prep(vl_ref, cwv_ref, vs, lc, ll, "v")

    eye = (lax.broadcasted_iota(jnp.int32, (CHUNK, CHUNK), 0)
           == lax.broadcasted_iota(jnp.int32, (CHUNK, CHUNK), 1)).astype(F32)

    masks = (_dir_masks(0), _dir_masks(1))
    head = pl.program_id(1)

    def build_tables(src_ref, off, ls):
        step = min(rb, ls)
        for r0 in range(0, ls, step):
            picked = _lane_picks(src_ref[0, r0:r0 + step, :], [8 * d + head for d in range(2)]
                                 + [16 + 8 * d + head for d in range(2)])
            for d in range(2):
                beta_t[d, off + r0:off + r0 + step, :] = picked[d]
                cg_t[d, off + r0:off + r0 + step, :] = picked[2 + d]

    build_tables(gdc_ref, 0, lc)
    build_tables(gdl_ref, lc, ll)

    def bwd_chunk(t):
        return jnp.where(t < n_c, n_c - 1 - t, n_t + n_c - 1 - t)

    def prep_chain(t, d):
        incl, strict = masks[d]
        last = CHUNK - 1 if d == 0 else 0
        tc = jnp.minimum(t, n_t - 1)
        c = tc if d == 0 else bwd_chunk(tc)
        r0 = pl.multiple_of(c * CHUNK, CHUNK)
        q = qs[pl.ds(r0, CHUNK), :]
        k = ks[pl.ds(r0, CHUNK), :]
        v = vs[pl.ds(r0, CHUNK), :]
        beta = beta_t[d, pl.ds(r0, CHUNK), :]
        cgc = cg_t[d, pl.ds(r0, CHUNK), :]
        cgr = jnp.transpose(cgc)[0:CHUNK, :]
        kk = _dot_nt(k, k)
        qk = _dot_nt(q, k)
        slot = (t % RING) * 2 + d
        yield
        cg_last = cgc[last:last + 1, :]
        decay = jnp.exp(jnp.where(incl, cgc[:, 0:CHUNK] - cgr, -jnp.inf))
        ecg = jnp.exp(cgc)
        kb = k * beta
        qk_r[slot] = qk * decay
        wq_r[slot, CHUNK:2 * CHUNK, :] = q * ecg
        kd_r[slot] = k * jnp.exp(cg_last - cgc)
        dc_r[slot] = jnp.broadcast_to(jnp.exp(cg_last), (8, LANES))
        x = jnp.where(strict, -(beta[:, 0:CHUNK] * kk) * decay, 0.0)
        tinv = eye + x
        xb = x.astype(BF16)
        x = _dot(xb, xb)
        yield
        for _ in range(4):
            xb = x.astype(BF16)
            both = _dot(jnp.concatenate([tinv.astype(BF16), xb], axis=0), xb)
            tinv, x = tinv + both[0:CHUNK, :], both[CHUNK:2 * CHUNK, :]
            yield
        tinv = tinv + _dot(tinv.astype(BF16), x.astype(BF16))
        yield
        wq_r[slot, 0:CHUNK, :] = _dot(tinv, kb * ecg)
        u_r[slot] = _dot(tinv, v * beta)

    out_s[...] = jnp.zeros(out_s.shape, F32)

    def scan_chain(d, t0, s, with_out, result):
        for j in range(GDN_STEPS):
            t = t0 + j
            slot = (t % RING) * 2 + d
            ws = _dot(wq_r[slot], s)
            yield
            v_new = u_r[slot] - ws[0:CHUNK, :]
            if with_out:
                c = t if d == 0 else bwd_chunk(t)
                o = ws[CHUNK:2 * CHUNK, :] + _dot(qk_r[slot], v_new)
                l0 = pl.multiple_of((c - n_c) * CHUNK, CHUNK)
                out_s[pl.ds(l0, CHUNK), :] += o
            s = s * dc_r[slot][0:1, :] + _dot_tn(kd_r[slot], v_new)
            yield
        result[d] = s

    def lockstep(chains):
        chains = list(chains)
        while chains:
            alive = []
            for ch in chains:
                try:
                    next(ch)
                    alive.append(ch)
                except StopIteration:
                    pass
            chains = alive

    def pair_body(i, carry, t_base, with_out):
        t0 = t_base + GDN_STEPS * i
        result = [None, None]
        lockstep([scan_chain(0, t0, carry[0], with_out, result), scan_chain(1, t0, carry[1], with_out, result)]
                 + [prep_chain(t0 + GDN_STEPS + j, d) for j in range(GDN_STEPS) for d in range(2)])
        return result[0], result[1]

    lockstep([prep_chain(j, d) for j in range(GDN_STEPS) for d in range(2)])
    s0 = jnp.zeros((GDN_DK, HEAD_V), F32)
    carry = lax.fori_loop(0, n_c // GDN_STEPS, functools.partial(pair_body, t_base=0, with_out=False), (s0, s0))
    lax.fori_loop(0, n_l // GDN_STEPS, functools.partial(pair_body, t_base=n_c, with_out=True), carry)


    def out_body(i, carry):
        r0 = pl.multiple_of(i * rb, rb)
        o = out_s[pl.ds(r0, rb), :]
        z = z_ref[0, pl.ds(r0, rb), :].astype(F32)
        y = o * lax.rsqrt(jnp.mean(o * o, axis=-1, keepdims=True) + EPS) * ng_ref[...]
        y_ref[0, pl.ds(r0, rb), :] = (y * (z * _sigmoid(z))).astype(y_ref.dtype)
        return carry

    lax.fori_loop(0, ll // rb, out_body, 0)


def _gdn(proj_c, proj_l, conv_w8, gd_c, gd_l, norm_g):
    b, lc, _ = proj_c.shape
    ll = proj_l.shape[1]
    lt = lc + ll
    n_t = lt // CHUNK
    qb, kb_, vb, zb = (COL_GDN_QKV // LANES, COL_GDN_QKV // LANES + HEADS, COL_GDN_QKV // LANES + 2 * HEADS,
                       COL_GDN_Z // LANES)

    def seq_spec(l, col0):
        return pl.BlockSpec((1, l, LANES), lambda i, h: (i, 0, col0 + h))

    def cw_spec(col0):
        return pl.BlockSpec((8, LANES), lambda i, h: (0, col0 + h))

    return pl.pallas_call(
        _gdn_kernel,
        grid=(b, HEADS),
        in_specs=[seq_spec(lc, qb), seq_spec(lc, kb_), seq_spec(lc, vb),
                  seq_spec(ll, qb), seq_spec(ll, kb_), seq_spec(ll, vb), seq_spec(ll, zb),
                  cw_spec(0), cw_spec(HEADS), cw_spec(2 * HEADS),
                  pl.BlockSpec((1, lc, LANES), lambda i, h: (i, 0, 0)),
                  pl.BlockSpec((1, ll, LANES), lambda i, h: (i, 0, 0)),
                  pl.BlockSpec((1, LANES), lambda i, h: (0, 0))],
        out_specs=pl.BlockSpec((1, ll, LANES), lambda i, h: (i, 0, h)),
        out_shape=jax.ShapeDtypeStruct((b, ll, HEADS * HEAD_V), BF16),
        scratch_shapes=[pltpu.VMEM((max(lc, ll) + 16, LANES), F32),
                        pltpu.VMEM((lt, LANES), F32), pltpu.VMEM((lt, LANES), F32), pltpu.VMEM((lt, LANES), F32),
                        pltpu.VMEM((2, lt, LANES), F32), pltpu.VMEM((2, lt, LANES), F32),
                        pltpu.VMEM((2 * RING, 2 * CHUNK, LANES), F32),
                        pltpu.VMEM((2 * RING, CHUNK, LANES), F32), pltpu.VMEM((2 * RING, CHUNK, LANES), F32),
                        pltpu.VMEM((2 * RING, CHUNK, CHUNK), F32),
                        pltpu.VMEM((2 * RING, 8, LANES), F32),
                        pltpu.VMEM((ll, LANES), F32)],
        compiler_params=_params(("arbitrary", "arbitrary")),
    )(proj_c, proj_c, proj_c, proj_l, proj_l, proj_l, proj_l, conv_w8, conv_w8, conv_w8,
      gd_c, gd_l, norm_g.reshape(1, LANES))


def _mlstm_kernel(qc_ref, kc_ref, vc_ref, ql_ref, kl_ref, vl_ref, mlc_ref, mll_ref, h_ref, out_s, tabs):
    lc = qc_ref.shape[1]
    ll = ql_ref.shape[1]
    n_c, n_l = lc // CHUNK, ll // CHUNK
    n_t = n_c + n_l
    lt = lc + ll
    pair = pl.program_id(1)
    lane = lax.broadcasted_iota(jnp.int32, (CHUNK, LANES), 1)
    ones_v = jnp.ones((CHUNK, HEAD_V), BF16)
    chains = [(hh, d) for hh in range(2) for d in range(2)]
    hmask = [((lane // ML_DK) == hh).astype(F32) for hh in range(2)]
    incl = [_dir_masks(d)[0] for d in range(2)]

    def build_tables(src_ref, off, ls):
        step = min(256, ls)
        for r0 in range(0, ls, step):
            lanes = [16 * j + 8 * d + 2 * pair + hh for hh, d in chains for j in range(3)]
            for g, tab in enumerate(_lane_picks(src_ref[0, r0:r0 + step, :], lanes)):
                tabs[g, off + r0:off + r0 + step, :] = tab

    build_tables(mlc_ref, 0, lc)
    build_tables(mll_ref, lc, ll)

    def wide(a):
        return jnp.concatenate([a, a], axis=1)

    def chain(hh, d, c, state, is_ctx, result):
        cs, ms = state
        last = CHUNK - 1 if d == 0 else 0
        if is_ctx:
            rows = pl.ds(pl.multiple_of(c * CHUNK, CHUNK), CHUNK)
            q_ref, k_ref, v_ref = qc_ref, kc_ref, vc_ref
        else:
            rows = pl.ds(pl.multiple_of((c - n_c) * CHUNK, CHUNK), CHUNK)
            q_ref, k_ref, v_ref = ql_ref, kl_ref, vl_ref
        q = (q_ref[0, rows, :].astype(F32) * hmask[hh]).astype(BF16)
        k = k_ref[0, rows, :].astype(F32) * (hmask[hh] * (ML_DK ** -0.5))
        v = jnp.concatenate([v_ref[0, rows, hh * HEAD_V:(hh + 1) * HEAD_V], ones_v], axis=1)
        n = chains.index((hh, d))
        trows = pl.ds(pl.multiple_of(c * CHUNK, CHUNK), CHUNK)
        gmb = tabs[3 * n, trows, :]
        gmb_t = jnp.transpose(gmb)[0:CHUNK, :]
        cm = tabs[3 * n + 1, trows, :]
        bc = tabs[3 * n + 2, trows, :]
        qk = _dot_nt(q, k.astype(BF16))
        yield
        cm_last = cm[last:last + 1, :]
        b_last = bc[last:last + 1, :]
        mm = jnp.maximum(ms, cm)
        p = jnp.where(incl[d], jnp.exp(gmb_t - mm[:, 0:CHUNK]), 0.0) * qk
        wk = (k * jnp.exp(gmb - cm_last)).astype(BF16)
        inter = _dot(q, cs.astype(BF16))
        intra = _dot(p.astype(BF16), v)
        c_loc = _dot_tn(wk, v)
        yield
        if not is_ctx:
            nd = wide(jnp.exp(ms - mm)) * inter + intra
            hv = nd[:, 0:HEAD_V] / jnp.maximum(jnp.abs(nd[:, HEAD_V:2 * HEAD_V]), jnp.exp(-(bc + mm)))
            l0 = pl.multiple_of((c - n_c) * CHUNK, CHUNK)
            out_s[pl.ds(l0, CHUNK), hh * HEAD_V:(hh + 1) * HEAD_V] += hv
        mx = jnp.maximum(ms, cm_last)
        result[hh, d] = (wide(jnp.exp(ms - mx)) * cs + wide(jnp.exp(cm_last - mx)) * c_loc, b_last + mx)

    out_s[...] = jnp.zeros(out_s.shape, F32)

    def run(chains_iter):
        live = list(chains_iter)
        while live:
            alive = []
            for ch in live:
                try:
                    next(ch)
                    alive.append(ch)
                except StopIteration:
                    pass
            live = alive

    def body(i, carry, is_ctx):
        result = {}
        gens = []
        for n, (hh, d) in enumerate(chains):
            if is_ctx:
                c = i if d == 0 else n_c - 1 - i
            else:
                c = n_c + i if d == 0 else n_t - 1 - i
            gens.append(chain(hh, d, c, carry[n], is_ctx, result))
        run(gens)
        return tuple(result[hd] for hd in chains)

    st0 = (jnp.zeros((LANES, 2 * HEAD_V), F32), jnp.zeros((1, LANES), F32))
    carry = lax.fori_loop(0, n_c, functools.partial(body, is_ctx=True), (st0,) * 4)
    lax.fori_loop(0, n_l, functools.partial(body, is_ctx=False), carry)
    h_ref[0] = out_s[...].astype(h_ref.dtype)


def _mlstm(proj_c, q_l, k_l, v_l, ml_c, ml_l):
    b, lc, _ = proj_c.shape
    ll = q_l.shape[1]
    lt = lc + ll
    qb, kb_, vb = COL_ML_Q // LANES, COL_ML_K // LANES, COL_ML_V // (2 * HEAD_V)
    return pl.pallas_call(
        _mlstm_kernel,
        grid=(b, HEADS // 2),
        in_specs=[pl.BlockSpec((1, lc, LANES), lambda i, p: (i, 0, qb + p)),
                  pl.BlockSpec((1, lc, LANES), lambda i, p: (i, 0, kb_ + p)),
                  pl.BlockSpec((1, lc, 2 * HEAD_V), lambda i, p: (i, 0, vb + p)),
                  pl.BlockSpec((1, ll, LANES), lambda i, p: (i, 0, p)),
                  pl.BlockSpec((1, ll, LANES), lambda i, p: (i, 0, p)),
                  pl.BlockSpec((1, ll, 2 * HEAD_V), lambda i, p: (i, 0, p)),
                  pl.BlockSpec((1, lc, LANES), lambda i, p: (i, 0, 0)),
                  pl.BlockSpec((1, ll, LANES), lambda i, p: (i, 0, 0))],
        out_specs=pl.BlockSpec((1, ll, 2 * HEAD_V), lambda i, p: (i, 0, p)),
        out_shape=jax.ShapeDtypeStruct((b, ll, HEADS * HEAD_V), BF16),
        scratch_shapes=[pltpu.VMEM((ll, 2 * HEAD_V), F32), pltpu.VMEM((12, lt, LANES), F32)],
        compiler_params=_params(("arbitrary", "arbitrary")),
    )(proj_c, proj_c, proj_c, q_l, k_l, v_l, ml_c, ml_l)


def _merge_kernel(yg_ref, hm_ref, o_ref, gg_ref, gm_ref, x_ref, mod_ref, mlg_ref, n2_ref,
                  wbg_ref, wbm_ref, wo_ref, wrh_ref, wrl_ref, x1_ref, h2_ref, sc_ref):
    d = x_ref.shape[1]
    o = o_ref[...].astype(F32)
    ym = _sigmoid(o) * hm_ref[...].astype(F32)
    segs = []
    for h in range(HEADS):
        seg = ym[:, h * HEAD_V:(h + 1) * HEAD_V]
        segs.append(seg * lax.rsqrt(jnp.mean(seg * seg, axis=-1, keepdims=True) + EPS))
    ymn = jnp.concatenate(segs, axis=1) * mlg_ref[...]
    y_gdn = _dot(yg_ref[...], wbg_ref[...])
    y_ml = _dot(ymn.astype(BF16), wbm_ref[...])
    mixed = _sigmoid(gg_ref[...].astype(F32)) * y_gdn + _sigmoid(gm_ref[...].astype(F32)) * y_ml
    y = _dot(mixed.astype(BF16), wo_ref[...])
    x1 = x_ref[...] + mod_ref[0, :, 2 * d:3 * d] * y
    x1_ref[...] = x1
    hn = x1 * lax.rsqrt(jnp.mean(x1 * x1, axis=-1, keepdims=True) + EPS) * n2_ref[...]
    h2 = hn * (1.0 + mod_ref[0, :, 4 * d:5 * d]) + mod_ref[0, :, 3 * d:4 * d]
    h2_hi = h2.astype(BF16)
    h2_ref[...] = h2_hi
    h2_lo = (h2 - h2_hi.astype(F32)).astype(BF16)
    logits = _dot_nt(wrh_ref[...], h2_hi) + (_dot_nt(wrl_ref[...], h2_hi) + _dot_nt(wrh_ref[...], h2_lo))
    sc_ref[...] = _sigmoid(logits)


def _merge(y_gdn, h_ml, proj_l2d, x2d, mod3, rows_per_mod, ml_norm_g, norm2_g, wbg, wbm, wo, wr_hi, wr_lo, tm=512):
    t, d = x2d.shape
    e = wr_hi.shape[0]
    row = lambda i: (i, 0)
    const = lambda i: (0, 0)
    return pl.pallas_call(
        _merge_kernel,
        grid=(t // tm,),
        in_specs=[pl.BlockSpec((tm, d), row), pl.BlockSpec((tm, d), row),
                  pl.BlockSpec((tm, d), lambda i: (i, COL_ML_O // d)),
                  pl.BlockSpec((tm, d), lambda i: (i, COL_MG_GDN // d)),
                  pl.BlockSpec((tm, d), lambda i: (i, COL_MG_ML // d)),
                  pl.BlockSpec((tm, d), row),
                  pl.BlockSpec((1, 1, mod3.shape[2]), lambda i: ((i * tm) // rows_per_mod, 0, 0)),
                  pl.BlockSpec((1, d), const), pl.BlockSpec((1, d), const),
                  pl.BlockSpec((d, d), const), pl.BlockSpec((d, d), const), pl.BlockSpec((d, d), const),
                  pl.BlockSpec((e, d), const), pl.BlockSpec((e, d), const)],
        out_specs=[pl.BlockSpec((tm, d), row), pl.BlockSpec((tm, d), row), pl.BlockSpec((e, tm), lambda i: (0, i))],
        out_shape=[jax.ShapeDtypeStruct((t, d), F32), jax.ShapeDtypeStruct((t, d), BF16),
                   jax.ShapeDtypeStruct((e, t), F32)],
        compiler_params=_params(("arbitrary",)),
    )(y_gdn, h_ml, proj_l2d, proj_l2d, proj_l2d, x2d, mod3, ml_norm_g.reshape(1, d), norm2_g.reshape(1, d),
      wbg, wbm, wo, wr_hi, wr_lo)


def _expert_kernel(be_ref, nu_ref, x_ref, wgu_ref, wd_ref, y_ref, wgu_s, wd_s):
    i = pl.program_id(0)
    de = wd_ref.shape[1]
    changed = jnp.logical_or(i == 0, be_ref[i] != be_ref[jnp.maximum(i - 1, 0)])

    @pl.when(changed)
    def _():
        wgu_s[...] = wgu_ref[0].astype(BF16)
        wd_s[...] = wd_ref[0].astype(BF16)

    @pl.when(i < nu_ref[0])
    def _():
        gu = _dot(x_ref[...], wgu_s[...])
        g = gu[:, 0:de]
        act = (g * _sigmoid(g)) * gu[:, de:2 * de]
        y_ref[...] = _dot(act.astype(BF16), wd_s[...]).astype(y_ref.dtype)

    @pl.when(i >= nu_ref[0])
    def _():
        y_ref[...] = jnp.zeros(y_ref.shape, y_ref.dtype)


def _experts(xb, blk_expert, n_used, w_gu, w_down):
    n_slots, d = xb.shape
    n_blocks = n_slots // EXPERT_ROWS
    e, _, de2 = w_gu.shape
    de = de2 // 2
    return pl.pallas_call(
        _expert_kernel,
        grid_spec=pltpu.PrefetchScalarGridSpec(
            num_scalar_prefetch=2,
            grid=(n_blocks,),
            in_specs=[pl.BlockSpec((EXPERT_ROWS, d), lambda i, be, nu: (i, 0)),
                      pl.BlockSpec((1, d, de2), lambda i, be, nu: (be[i], 0, 0)),
                      pl.BlockSpec((1, de, d), lambda i, be, nu: (be[i], 0, 0))],
            out_specs=pl.BlockSpec((EXPERT_ROWS, d), lambda i, be, nu: (i, 0)),
            scratch_shapes=[pltpu.VMEM((d, de2), BF16), pltpu.VMEM((de, d), BF16)]),
        out_shape=jax.ShapeDtypeStruct((n_slots, d), BF16),
        compiler_params=_params(("arbitrary",)),
    )(blk_expert, n_used, xb, w_gu, w_down)


def _final_kernel(x1_ref, h2_ref, yg_ref, wt_ref, mod_ref, wsg_ref, wsd_ref, fg_ref, o_ref):
    d = x1_ref.shape[1]
    ds_ = wsd_ref.shape[0]
    wt = wt_ref[...]
    routed = jnp.zeros(x1_ref.shape, F32)
    for k in range(TOP_K):
        routed = routed + wt[:, k:k + 1] * yg_ref[k].astype(F32)
    gu = _dot(h2_ref[...], wsg_ref[...])
    g = gu[:, 0:ds_]
    sh = _dot(((g * _sigmoid(g)) * gu[:, ds_:2 * ds_]).astype(BF16), wsd_ref[...])
    x2 = x1_ref[...] + mod_ref[0, :, 5 * d:6 * d] * (routed + sh)
    o_ref[...] = x2 * lax.rsqrt(jnp.mean(x2 * x2, axis=-1, keepdims=True) + EPS) * fg_ref[...]


def _final(x1, h2, yg, wts, mod3, rows_per_mod, w_sh_gu, w_sh_down, final_g, tm=256):
    t, d = x1.shape
    row = lambda i: (i, 0)
    const = lambda i: (0, 0)
    return pl.pallas_call(
        _final_kernel,
        grid=(t // tm,),
        in_specs=[pl.BlockSpec((tm, d), row), pl.BlockSpec((tm, d), row),
                  pl.BlockSpec((TOP_K, tm, d), lambda i: (0, i, 0)), pl.BlockSpec((tm, TOP_K), row),
                  pl.BlockSpec((1, 1, mod3.shape[2]), lambda i: ((i * tm) // rows_per_mod, 0, 0)),
                  pl.BlockSpec(w_sh_gu.shape, const), pl.BlockSpec(w_sh_down.shape, const),
                  pl.BlockSpec((1, d), const)],
        out_specs=pl.BlockSpec((tm, d), row),
        out_shape=jax.ShapeDtypeStruct((t, d), F32),
        compiler_params=_params(("arbitrary",)),
    )(x1, h2, yg, wts, mod3, w_sh_gu, w_sh_down, final_g.reshape(1, d))


def _route_kernel(sc_ref, bias_ref, tri_ref, idx_ref, wt_ref, rk_ref, cnt_ref, base_s):
    @pl.when(pl.program_id(0) == 0)
    def _():
        base_s[...] = jnp.zeros(base_s.shape, F32)

    scores = sc_ref[...]
    e, tn = scores.shape
    gsz = e // N_GROUPS
    sel3 = (scores + bias_ref[...]).reshape(N_GROUPS, gsz, tn)
    m1 = jnp.max(sel3, axis=1)
    is_max = sel3 == m1[:, None, :]
    n_max = jnp.sum(is_max.astype(F32), axis=1)
    m2 = jnp.max(jnp.where(is_max, -jnp.inf, sel3), axis=1)
    grp = m1 + jnp.where(n_max >= 2.0, m1, m2)
    gi = lax.broadcasted_iota(jnp.int32, (N_GROUPS, tn), 0)
    ahead = jnp.zeros((N_GROUPS, tn), F32)
    for g in range(N_GROUPS):
        row = grp[g:g + 1, :]
        ahead = ahead + jnp.logical_or(row > grp, jnp.logical_and(row == grp, g < gi)).astype(F32)
    ahead3 = jnp.broadcast_to(ahead[:, None, :], (N_GROUPS, gsz, tn))
    selm = jnp.where(ahead3 < float(TOPK_GROUPS), sel3, -jnp.inf).reshape(e, tn)
    ri = lax.broadcasted_iota(jnp.int32, (e, tn), 0).astype(F32)
    member = jnp.zeros((e, tn), F32)
    idxs, ws = [], []
    for _ in range(TOP_K):
        m = jnp.max(selm, axis=0, keepdims=True)
        idx = jnp.min(jnp.where(selm == m, ri, float(e)), axis=0, keepdims=True)
        hit = ri == idx
        ws.append(jnp.sum(jnp.where(hit, scores, 0.0), axis=0, keepdims=True))
        idxs.append(idx)
        selm = jnp.where(hit, -jnp.inf, selm)
        member = jnp.where(hit, 1.0, member)
    w = jnp.concatenate(ws, axis=0)
    wt_ref[...] = w / jnp.sum(w, axis=0, keepdims=True) * ROUTED_SCALE
    idx_ref[...] = jnp.concatenate(idxs, axis=0).astype(jnp.int32)
    cum = _dot(member.astype(BF16), tri_ref[...]) + base_s[...]
    rk_ref[...] = jnp.concatenate(
        [jnp.sum(jnp.where(ri == idx, cum, 0.0), axis=0, keepdims=True) for idx in idxs], axis=0).astype(jnp.int32)
    total = base_s[...] + jnp.sum(member, axis=1, keepdims=True)
    base_s[...] = total
    cnt_ref[...] = total


def _route(scores_t, router_bias):
    e, t = scores_t.shape
    tn = LANES
    bias = jnp.broadcast_to(router_bias.astype(F32)[:, None], (e, tn))
    tri = (jnp.arange(tn)[:, None] < jnp.arange(tn)[None, :]).astype(BF16)
    tok = pl.BlockSpec((TOP_K, tn), lambda i: (0, i))
    const = lambda i: (0, 0)
    return pl.pallas_call(
        _route_kernel,
        grid=(t // tn,),
        in_specs=[pl.BlockSpec((e, tn), lambda i: (0, i)), pl.BlockSpec((e, tn), const),
                  pl.BlockSpec((tn, tn), const)],
        out_specs=[tok, tok, tok, pl.BlockSpec((e, tn), const)],
        out_shape=[jax.ShapeDtypeStruct((TOP_K, t), jnp.int32), jax.ShapeDtypeStruct((TOP_K, t), F32),
                   jax.ShapeDtypeStruct((TOP_K, t), jnp.int32), jax.ShapeDtypeStruct((e, tn), F32)],
        scratch_shapes=[pltpu.VMEM((e, tn), F32)],
        compiler_params=_params(("arbitrary",)),
    )(scores_t, bias, tri)


def _slot_kernel(idx_ref, rk_ref, ps_ref, pos_ref):
    e, tn = ps_ref.shape
    ri = lax.broadcasted_iota(jnp.int32, (e, tn), 0)
    ps = ps_ref[...]
    rows = [jnp.sum(jnp.where(ri == idx_ref[k:k + 1, :], ps, 0.0), axis=0, keepdims=True) for k in range(TOP_K)]
    pos_ref[...] = rk_ref[...] + jnp.concatenate(rows, axis=0).astype(jnp.int32)


def _slots(idx, rank, pstart):
    k, t = idx.shape
    e = pstart.shape[0]
    tn = LANES
    tok = pl.BlockSpec((k, tn), lambda i: (0, i))
    return pl.pallas_call(
        _slot_kernel,
        grid=(t // tn,),
        in_specs=[tok, tok, pl.BlockSpec((e, tn), lambda i: (0, 0))],
        out_specs=tok,
        out_shape=jax.ShapeDtypeStruct((k, t), jnp.int32),
        compiler_params=_params(("arbitrary",)),
    )(idx, rank, jnp.broadcast_to(pstart.astype(F32)[:, None], (e, tn)))


def _block_table(counts, n_blocks):
    padded = (counts + EXPERT_ROWS - 1) // EXPERT_ROWS * EXPERT_ROWS
    pend = jnp.cumsum(padded)
    blk_expert = jnp.minimum(jnp.searchsorted(pend, jnp.arange(n_blocks) * EXPERT_ROWS, side='right'),
                             N_EXPERTS - 1).astype(jnp.int32)
    return pend - padded, blk_expert, (pend[-1] // EXPERT_ROWS).astype(jnp.int32).reshape(1)


def _col_major(t):
    b, l, f = t.shape
    rows = l // GRID_W
    return t.reshape(b, rows, GRID_W, f).transpose(0, 2, 1, 3).reshape(b, l, f)


def _row_major(t):
    b, l, f = t.shape
    rows = l // GRID_W
    return t.reshape(b, GRID_W, rows, f).transpose(0, 2, 1, 3).reshape(b, l, f)


def kernel(x, c, ctx, c_ctx, w_ada, b_ada, norm1_g, norm2_g, w_in, gdn_conv_w, gdn_a_log, gdn_dt_bias, gdn_norm_g,
           ml_i_bias, ml_f_bias, ml_norm_g, w_branch_gdn, w_branch_ml, w_out, w_router, router_bias, w_exp_gate_up,
           w_exp_down, w_sh_gate_up, w_sh_down, final_norm_g):
    b, l, d = x.shape
    lc = ctx.shape[1]
    t = b * l
    layer = 0

    w = w_in[layer]
    main_cols = [_ORIG[k] for k in ("gdn_qkv", "gdn_z", "ml_q", "ml_k", "ml_v", "ml_o", "mg_gdn", "mg_ml")]
    w_main = jnp.concatenate([w[:, a:e] for a, e in main_cols], axis=1).astype(BF16)
    w_gate = jnp.concatenate([w[:, _ORIG["gdn_gate"][0]:_ORIG["gdn_gate"][1]],
                              w[:, _ORIG["ml_gate"][0]:_ORIG["ml_gate"][1]],
                              jnp.zeros((d, LANES - 64), F32)], axis=1).astype(BF16)
    zeros16 = jnp.zeros((16,), F32)
    gp_add = jnp.concatenate([zeros16, gdn_dt_bias[layer].reshape(-1), ml_i_bias[layer].reshape(-1),
                              ml_f_bias[layer].reshape(-1), jnp.zeros((LANES - 64,), F32)])
    gp_mul = jnp.concatenate([zeros16, -jnp.exp(gdn_a_log[layer].astype(F32)).reshape(-1),
                              jnp.zeros((LANES - 32,), F32)])
    gparams = jnp.zeros((8, LANES), F32).at[0].set(gp_add).at[1].set(gp_mul)
    conv_w8 = jnp.zeros((8, gdn_conv_w.shape[2]), F32).at[0:GDN_CONV].set(gdn_conv_w[layer])
    wr = w_router[layer].T
    wr_hi = wr.astype(BF16)
    wr_lo = (wr - wr_hi.astype(F32)).astype(BF16)

    n_mod_rows = -(-(b + 1) // 8) * 8
    cc = jnp.zeros((n_mod_rows, d), F32).at[0:b].set(c).at[b].set(c_ctx)
    mod = _ada_mod(cc, w_ada[layer], b_ada[layer])
    mod3 = mod.reshape(n_mod_rows, 1, 6 * d)

    x2d = x.reshape(t, d)
    tm_l = min(1024, l)
    proj_l, gate_l = _project(x2d, mod3, lambda i: (i * tm_l) // l, norm1_g[layer], w_main, w_gate, tm_l)
    tm_c = min(1024, b * lc)
    proj_c, gate_c = _project(ctx.reshape(b * lc, d), mod3, lambda i: b, norm1_g[layer], w_main, w_gate, tm_c)
    proj_l3 = proj_l.reshape(b, l, N_MAIN)
    proj_c3 = proj_c.reshape(b, lc, N_MAIN)

    gate_l_cm = _col_major(gate_l.reshape(b, l, LANES)).reshape(t, LANES)
    gd_c, ml_c = _gate_prep(gate_c, gparams)
    gd_l, _ = _gate_prep(gate_l, gparams)
    _, ml_l = _gate_prep(gate_l_cm, gparams)

    y_gdn = _gdn(proj_c3, proj_l3, conv_w8, gd_c.reshape(b, lc, LANES), gd_l.reshape(b, l, LANES),
                 gdn_norm_g[layer])
    q_cm = _col_major(proj_l3[:, :, COL_ML_Q:COL_ML_Q + HEADS * ML_DK])
    k_cm = _col_major(proj_l3[:, :, COL_ML_K:COL_ML_K + HEADS * ML_DK])
    v_cm = _col_major(proj_l3[:, :, COL_ML_V:COL_ML_V + HEADS * HEAD_V])
    h_ml = _row_major(_mlstm(proj_c3, q_cm, k_cm, v_cm, ml_c.reshape(b, lc, LANES), ml_l.reshape(b, l, LANES)))

    x1, h2, scores_t = _merge(y_gdn.reshape(t, d), h_ml.reshape(t, d), proj_l, x2d, mod3, l, ml_norm_g[layer],
                            norm2_g[layer], w_branch_gdn[layer].astype(BF16), w_branch_ml[layer].astype(BF16),
                            w_out[layer].astype(BF16), wr_hi, wr_lo, tm=min(512, l))

    idx, wts, rank, cnt = _route(scores_t, router_bias[layer])
    n_assign = t * TOP_K
    n_blocks = (n_assign + N_EXPERTS * (EXPERT_ROWS - 1)) // EXPERT_ROWS + 1
    n_slots = n_blocks * EXPERT_ROWS
    pstart, blk_expert, n_used = _block_table(cnt[:, 0].astype(jnp.int32), n_blocks)
    pos = _slots(idx, rank, pstart).reshape(n_assign)
    tok_slot = (jnp.arange(n_slots, dtype=jnp.int32) % t).at[pos].set(jnp.arange(n_assign, dtype=jnp.int32) % t)
    xb = h2[tok_slot]
    yb = _experts(xb, blk_expert, n_used, w_exp_gate_up[layer], w_exp_down[layer])
    yg = yb[pos].reshape(TOP_K, t, d)
    out = _final(x1, h2, yg, wts.T, mod3, l, w_sh_gate_up[layer].astype(BF16), w_sh_down[layer].astype(BF16),
                 final_norm_g, tm=min(256, l))
    return out.reshape(b, l, d)
```

```python
import functools
import math

import jax
import jax.numpy as jnp
from jax import lax
from jax.experimental import pallas as pl
from jax.experimental.pallas import tpu as pltpu

F32 = jnp.float32
BF16 = jnp.bfloat16
HI = lax.Precision.HIGHEST

EPS = 1e-6
CHUNK = 64
GRID_W = 64
HEADS = 8
HEAD_V = 128
GDN_DK = 128
ML_DK = 64
GDN_CONV = 5
N_EXPERTS = 256
TOP_K = 8
N_GROUPS = 8
TOPK_GROUPS = 4
ROUTED_SCALE = 2.5
EXPERT_ROWS = 512
GDN_STEPS = 4
RING = 2 * GDN_STEPS
LANES = 128
VMEM_LIMIT = 56 * 1024 * 1024

COL_GDN_QKV = 0
COL_GDN_Z = 3072
COL_ML_Q = 4096
COL_ML_K = 4608
COL_ML_V = 5120
COL_ML_O = 6144
COL_MG_GDN = 7168
COL_MG_ML = 8192
N_MAIN = 9216
_ORIG = dict(gdn_qkv=(0, 3072), gdn_z=(3072, 4096), gdn_gate=(4096, 4128), ml_q=(4128, 4640),
             ml_k=(4640, 5152), ml_v=(5152, 6176), ml_o=(6176, 7200), ml_gate=(7200, 7232),
             mg_gdn=(7232, 8256), mg_ml=(8256, 9280))


def _params(sem, vmem=VMEM_LIMIT):
    return pltpu.CompilerParams(dimension_semantics=sem, vmem_limit_bytes=vmem)


def _dot(a, b, precision=None):
    return jnp.dot(a, b, preferred_element_type=F32, precision=precision)


def _dot_nt(a, b, precision=None):
    return lax.dot_general(a, b, (((1,), (1,)), ((), ())), preferred_element_type=F32, precision=precision)


def _dot_tn(a, b, precision=None):
    return lax.dot_general(a, b, (((0,), (0,)), ((), ())), preferred_element_type=F32, precision=precision)


def _times3(a, b):
    ah = a.astype(BF16)
    al = (a - ah.astype(F32)).astype(BF16)
    bh = b.astype(BF16)
    bl = (b - bh.astype(F32)).astype(BF16)
    return _dot(ah, bh) + (_dot(ah, bl) + _dot(al, bh))


def _sigmoid(x):
    return 1.0 / (1.0 + jnp.exp(-x))


def _softplus(x):
    return jnp.maximum(x, 0.0) + jnp.log(1.0 + jnp.exp(-jnp.abs(x)))


def _ada_kernel(c_ref, w_ref, b_ref, o_ref):
    c = c_ref[...]
    sc = c * _sigmoid(c)
    o_ref[...] = _dot(sc, w_ref[...], HI) + b_ref[...]


def _ada_mod(cc, w_ada, b_ada, tn=1536):
    r, d = cc.shape
    n = w_ada.shape[1]
    return pl.pallas_call(
        _ada_kernel,
        grid=(n // tn,),
        in_specs=[pl.BlockSpec((r, d), lambda j: (0, 0)),
                  pl.BlockSpec((d, tn), lambda j: (0, j)),
                  pl.BlockSpec((1, tn), lambda j: (0, j))],
        out_specs=pl.BlockSpec((r, tn), lambda j: (0, j)),
        out_shape=jax.ShapeDtypeStruct((r, n), F32),
        compiler_params=_params(("arbitrary",)),
    )(cc, w_ada, b_ada.reshape(1, n))


def _proj_kernel(x_ref, mod_ref, g_ref, w_ref, wg_ref, o_ref, og_ref, hn_ref):
    d = x_ref.shape[1]

    @pl.when(pl.program_id(1) == 0)
    def _():
        x = x_ref[...]
        y = x * lax.rsqrt(jnp.mean(x * x, axis=-1, keepdims=True) + EPS) * g_ref[...]
        shift = mod_ref[0, :, 0:d]
        scale = mod_ref[0, :, d:2 * d]
        h = (y * (1.0 + scale) + shift).astype(BF16)
        hn_ref[...] = h
        og_ref[...] = _dot(h, wg_ref[...])

    o_ref[...] = _dot(hn_ref[...], w_ref[...]).astype(o_ref.dtype)


def _project(x2d, mod3, mod_row_of_tile, norm_g, w_main, w_gate, tm, tn=1024):
    t, d = x2d.shape
    n = w_main.shape[1]
    return pl.pallas_call(
        _proj_kernel,
        grid=(t // tm, n // tn),
        in_specs=[pl.BlockSpec((tm, d), lambda i, j: (i, 0)),
                  pl.BlockSpec((1, 1, mod3.shape[2]), lambda i, j: (mod_row_of_tile(i), 0, 0)),
                  pl.BlockSpec((1, d), lambda i, j: (0, 0)),
                  pl.BlockSpec((d, tn), lambda i, j: (0, j)),
                  pl.BlockSpec((d, LANES), lambda i, j: (0, 0))],
        out_specs=[pl.BlockSpec((tm, tn), lambda i, j: (i, j)),
                   pl.BlockSpec((tm, LANES), lambda i, j: (i, 0))],
        out_shape=[jax.ShapeDtypeStruct((t, n), BF16), jax.ShapeDtypeStruct((t, LANES), F32)],
        scratch_shapes=[pltpu.VMEM((tm, d), BF16)],
        compiler_params=_params(("arbitrary", "arbitrary")),
    )(x2d, mod3, norm_g.reshape(1, d), w_main, w_gate)


def _gate_kernel(g_ref, p_ref, gd_ref, ml_ref):
    rows = g_ref.shape[0]
    raw = g_ref[...] + p_ref[0:1, :]
    lane = lax.broadcasted_iota(jnp.int32, raw.shape, 1)
    sp = _softplus(raw)
    vals = jnp.where(lane < 16, _sigmoid(raw),
                     jnp.where(lane < 32, p_ref[1:2, :] * sp,
                               jnp.where(lane < 48, raw,
                                         jnp.where(lane < 64, -_softplus(-raw), 0.0))))
    ri = lax.broadcasted_iota(jnp.int32, (CHUNK, CHUNK), 0)
    ci = lax.broadcasted_iota(jnp.int32, (CHUNK, CHUNK), 1)
    tri_f = (ri >= ci).astype(F32)
    tri_b = (ri <= ci).astype(F32)
    lane_c = lax.broadcasted_iota(jnp.int32, (CHUNK, LANES), 1)
    row_c = lax.broadcasted_iota(jnp.int32, (CHUNK, LANES), 0)
    fwd_lane = (lane_c % 16) < 8
    for c in range(rows // CHUNK):
        blk = vals[c * CHUNK:(c + 1) * CHUNK, :]
        cum = jnp.where(fwd_lane, _dot(tri_f, blk, HI), _dot(tri_b, blk, HI))
        gd_ref[c * CHUNK:(c + 1) * CHUNK, :] = jnp.where(lane_c < 16, blk, jnp.where(lane_c < 32, cum, 0.0))
        bcum = pltpu.roll(cum, LANES - 16, axis=1)
        gmb = blk - bcum
        cmf, cmb = gmb, gmb
        for s in (1, 2, 4, 8, 16, 32):
            cmf = jnp.maximum(cmf, jnp.where(row_c >= s, pltpu.roll(cmf, s, axis=0), -jnp.inf))
            cmb = jnp.maximum(cmb, jnp.where(row_c < CHUNK - s, pltpu.roll(cmb, CHUNK - s, axis=0), -jnp.inf))
        cm = jnp.where(fwd_lane, cmf, cmb)
        ml = jnp.where(lane_c < 16, pltpu.roll(gmb, LANES - 32, axis=1),
                       jnp.where(lane_c < 32, pltpu.roll(cm, LANES - 16, axis=1),
                                 jnp.where(lane_c < 48, bcum, 0.0)))
        ml_ref[c * CHUNK:(c + 1) * CHUNK, :] = ml


def _gate_prep(graw, gparams, tm=256):
    t = graw.shape[0]
    spec = pl.BlockSpec((tm, LANES), lambda i: (i, 0))
    return pl.pallas_call(
        _gate_kernel,
        grid=(t // tm,),
        in_specs=[spec, pl.BlockSpec((8, LANES), lambda i: (0, 0))],
        out_specs=[spec, spec],
        out_shape=[jax.ShapeDtypeStruct((t, LANES), F32)] * 2,
        compiler_params=_params(("arbitrary",)),
    )(graw, gparams)


def _split3(a):
    h = a.astype(BF16)
    r = a - h.astype(F32)
    m = r.astype(BF16)
    return h, m, (r - m.astype(F32)).astype(BF16)


def _lane_picks(x, lanes):
    li = lax.broadcasted_iota(jnp.int32, (LANES, LANES), 0)
    ci = lax.broadcasted_iota(jnp.int32, (LANES, LANES), 1)
    want = jnp.full((LANES, LANES), -1, jnp.int32)
    for j, lane in enumerate(lanes):
        want = jnp.where(ci == j, lane, want)
    sel = (li == want).astype(BF16)
    h, m, lo = _split3(x)
    cols = _dot(h, sel) + (_dot(m, sel) + _dot(lo, sel))
    return [jnp.broadcast_to(cols[:, j:j + 1], x.shape) for j in range(len(lanes))]


def _dir_masks(direction):
    ri = lax.broadcasted_iota(jnp.int32, (CHUNK, CHUNK), 0)
    ci = lax.broadcasted_iota(jnp.int32, (CHUNK, CHUNK), 1)
    if direction == 0:
        return ri >= ci, ri > ci
    return ri <= ci, ri < ci


def _gdn_kernel(qc_ref, kc_ref, vc_ref, ql_ref, kl_ref, vl_ref, z_ref, cwq_ref, cwk_ref, cwv_ref,
                gdc_ref, gdl_ref, ng_ref, y_ref,
                xpad, qs, ks, vs, beta_t, cg_t, wq_r, u_r, kd_r, qk_r, dc_r, out_s):
    lc = qc_ref.shape[1]
    ll = ql_ref.shape[1]
    lt = lc + ll
    n_c, n_l = lc // CHUNK, ll // CHUNK
    n_t = n_c + n_l
    rb = 256

    def l2n(x):
        return x * lax.rsqrt(jnp.sum(x * x, axis=-1, keepdims=True) + EPS)

    def prep(src_ref, cw_ref, dst, off, ls, kind):
        xpad[0:8, :] = jnp.zeros((8, LANES), F32)
        xpad[8:8 + ls, :] = src_ref[0].astype(F32)
        xpad[8 + ls:16 + ls, :] = jnp.zeros((8, LANES), F32)
        step = min(rb, ls)
        for r0 in range(0, ls, step):
            acc = jnp.zeros((step, LANES), F32)
            for t in range(GDN_CONV):
                s0 = r0 + 8 - GDN_CONV // 2 + t
                acc = acc + cw_ref[t:t + 1, :] * xpad[s0:s0 + step, :]
            y = acc * _sigmoid(acc)
            if kind == "q":
                y = l2n(y) * (GDN_DK ** -0.5)
            elif kind == "k":
                y = l2n(y)
            dst[off + r0:off + r0 + step, :] = y

    prep(qc_ref, cwq_ref, qs, 0, lc, "q")
    prep(kc_ref, cwk_ref, ks, 0, lc, "k")
    prep(vc_ref, cwv_ref, vs, 0, lc, "v")
    prep(ql_ref, cwq_ref, qs, lc, ll, "q")
    prep(kl_ref, cwk_ref, ks, lc, ll, "k")
    prep(vl_ref, cwv_ref, vs, lc, ll, "v")

    eye = (lax.broadcasted_iota(jnp.int32, (CHUNK, CHUNK), 0)
           == lax.broadcasted_iota(jnp.int32, (CHUNK, CHUNK), 1)).astype(F32)

    masks = (_dir_masks(0), _dir_masks(1))
    head = pl.program_id(1)

    def build_tables(src_ref, off, ls):
        step = min(rb, ls)
        for r0 in range(0, ls, step):
            picked = _lane_picks(src_ref[0, r0:r0 + step, :], [8 * d + head for d in range(2)]
                                 + [16 + 8 * d + head for d in range(2)])
            for d in range(2):
                beta_t[d, off + r0:off + r0 + step, :] = picked[d]
                cg_t[d, off + r0:off + r0 + step, :] = picked[2 + d]

    build_tables(gdc_ref, 0, lc)
    build_tables(gdl_ref, lc, ll)

    def bwd_chunk(t):
        return jnp.where(t < n_c, n_c - 1 - t, n_t + n_c - 1 - t)

    def prep_chain(t, d):
        incl, strict = masks[d]
        last = CHUNK - 1 if d == 0 else 0
        tc = jnp.minimum(t, n_t - 1)
        c = tc if d == 0 else bwd_chunk(tc)
        r0 = pl.multiple_of(c * CHUNK, CHUNK)
        q = qs[pl.ds(r0, CHUNK), :]
        k = ks[pl.ds(r0, CHUNK), :]
        v = vs[pl.ds(r0, CHUNK), :]
        beta = beta_t[d, pl.ds(r0, CHUNK), :]
        cgc = cg_t[d, pl.ds(r0, CHUNK), :]
        cgr = jnp.transpose(cgc)[0:CHUNK, :]
        kk = _dot_nt(k, k)
        qk = _dot_nt(q, k)
        slot = (t % RING) * 2 + d
        yield
        cg_last = cgc[last:last + 1, :]
        decay = jnp.exp(jnp.where(incl, cgc[:, 0:CHUNK] - cgr, -jnp.inf))
        ecg = jnp.exp(cgc)
        kb = k * beta
        qk_r[slot] = qk * decay
        wq_r[slot, CHUNK:2 * CHUNK, :] = q * ecg
        kd_r[slot] = k * jnp.exp(cg_last - cgc)
        dc_r[slot] = jnp.broadcast_to(jnp.exp(cg_last), (8, LANES))
        x = jnp.where(strict, -(beta[:, 0:CHUNK] * kk) * decay, 0.0)
        tinv = eye + x
        x = _times3(x, x)
        yield
        for _ in range(4):
            both = _times3(jnp.concatenate([tinv, x], axis=0), x)
            tinv, x = tinv + both[0:CHUNK, :], both[CHUNK:2 * CHUNK, :]
            yield
        tinv = tinv + _times3(tinv, x)
        yield
        wq_r[slot, 0:CHUNK, :] = _dot(tinv, kb * ecg)
        u_r[slot] = _dot(tinv, v * beta)

    out_s[...] = jnp.zeros(out_s.shape, F32)

    def scan_chain(d, t0, s, with_out, result):
        for j in range(GDN_STEPS):
            t = t0 + j
            slot = (t % RING) * 2 + d
            ws = _dot(wq_r[slot], s)
            yield
            v_new = u_r[slot] - ws[0:CHUNK, :]
            if with_out:
                c = t if d == 0 else bwd_chunk(t)
                o = ws[CHUNK:2 * CHUNK, :] + _dot(qk_r[slot], v_new)
                l0 = pl.multiple_of((c - n_c) * CHUNK, CHUNK)
                out_s[pl.ds(l0, CHUNK), :] += o
            s = s * dc_r[slot][0:1, :] + _dot_tn(kd_r[slot], v_new)
            yield
        result[d] = s

    def lockstep(chains):
        chains = list(chains)
        while chains:
            alive = []
            for ch in chains:
                try:
                    next(ch)
                    alive.append(ch)
                except StopIteration:
                    pass
            chains = alive

    def pair_body(i, carry, t_base, with_out):
        t0 = t_base + GDN_STEPS * i
        result = [None, None]
        lockstep([scan_chain(0, t0, carry[0], with_out, result), scan_chain(1, t0, carry[1], with_out, result)]
                 + [prep_chain(t0 + GDN_STEPS + j, d) for j in range(GDN_STEPS) for d in range(2)])
        return result[0], result[1]

    lockstep([prep_chain(j, d) for j in range(GDN_STEPS) for d in range(2)])
    s0 = jnp.zeros((GDN_DK, HEAD_V), F32)
    carry = lax.fori_loop(0, n_c // GDN_STEPS, functools.partial(pair_body, t_base=0, with_out=False), (s0, s0))
    lax.fori_loop(0, n_l // GDN_STEPS, functools.partial(pair_body, t_base=n_c, with_out=True), carry)


    def out_body(i, carry):
        r0 = pl.multiple_of(i * rb, rb)
        o = out_s[pl.ds(r0, rb), :]
        z = z_ref[0, pl.ds(r0, rb), :].astype(F32)
        y = o * lax.rsqrt(jnp.mean(o * o, axis=-1, keepdims=True) + EPS) * ng_ref[...]
        y_ref[0, pl.ds(r0, rb), :] = (y * (z * _sigmoid(z))).astype(y_ref.dtype)
        return carry

    lax.fori_loop(0, ll // rb, out_body, 0)


def _gdn(proj_c, proj_l, conv_w8, gd_c, gd_l, norm_g):
    b, lc, _ = proj_c.shape
    ll = proj_l.shape[1]
    lt = lc + ll
    n_t = lt // CHUNK
    qb, kb_, vb, zb = (COL_GDN_QKV // LANES, COL_GDN_QKV // LANES + HEADS, COL_GDN_QKV // LANES + 2 * HEADS,
                       COL_GDN_Z // LANES)

    def seq_spec(l, col0):
        return pl.BlockSpec((1, l, LANES), lambda i, h: (i, 0, col0 + h))

    def cw_spec(col0):
        return pl.BlockSpec((8, LANES), lambda i, h: (0, col0 + h))

    return pl.pallas_call(
        _gdn_kernel,
        grid=(b, HEADS),
        in_specs=[seq_spec(lc, qb), seq_spec(lc, kb_), seq_spec(lc, vb),
                  seq_spec(ll, qb), seq_spec(ll, kb_), seq_spec(ll, vb), seq_spec(ll, zb),
                  cw_spec(0), cw_spec(HEADS), cw_spec(2 * HEADS),
                  pl.BlockSpec((1, lc, LANES), lambda i, h: (i, 0, 0)),
                  pl.BlockSpec((1, ll, LANES), lambda i, h: (i, 0, 0)),
                  pl.BlockSpec((1, LANES), lambda i, h: (0, 0))],
        out_specs=pl.BlockSpec((1, ll, LANES), lambda i, h: (i, 0, h)),
        out_shape=jax.ShapeDtypeStruct((b, ll, HEADS * HEAD_V), BF16),
        scratch_shapes=[pltpu.VMEM((max(lc, ll) + 16, LANES), F32),
                        pltpu.VMEM((lt, LANES), F32), pltpu.VMEM((lt, LANES), F32), pltpu.VMEM((lt, LANES), F32),
                        pltpu.VMEM((2, lt, LANES), F32), pltpu.VMEM((2, lt, LANES), F32),
                        pltpu.VMEM((2 * RING, 2 * CHUNK, LANES), F32),
                        pltpu.VMEM((2 * RING, CHUNK, LANES), F32), pltpu.VMEM((2 * RING, CHUNK, LANES), F32),
                        pltpu.VMEM((2 * RING, CHUNK, CHUNK), F32),
                        pltpu.VMEM((2 * RING, 8, LANES), F32),
                        pltpu.VMEM((ll, LANES), F32)],
        compiler_params=_params(("arbitrary", "arbitrary")),
    )(proj_c, proj_c, proj_c, proj_l, proj_l, proj_l, proj_l, conv_w8, conv_w8, conv_w8,
      gd_c, gd_l, norm_g.reshape(1, LANES))


def _mlstm_kernel(qc_ref, kc_ref, vc_ref, ql_ref, kl_ref, vl_ref, mlc_ref, mll_ref, h_ref, out_s, tabs):
    lc = qc_ref.shape[1]
    ll = ql_ref.shape[1]
    n_c, n_l = lc // CHUNK, ll // CHUNK
    n_t = n_c + n_l
    lt = lc + ll
    pair = pl.program_id(1)
    lane = lax.broadcasted_iota(jnp.int32, (CHUNK, LANES), 1)
    ones_v = jnp.ones((CHUNK, HEAD_V), BF16)
    chains = [(hh, d) for hh in range(2) for d in range(2)]
    hmask = [((lane // ML_DK) == hh).astype(F32) for hh in range(2)]
    incl = [_dir_masks(d)[0] for d in range(2)]

    def build_tables(src_ref, off, ls):
        step = min(256, ls)
        for r0 in range(0, ls, step):
            lanes = [16 * j + 8 * d + 2 * pair + hh for hh, d in chains for j in range(3)]
            for g, tab in enumerate(_lane_picks(src_ref[0, r0:r0 + step, :], lanes)):
                tabs[g, off + r0:off + r0 + step, :] = tab

    build_tables(mlc_ref, 0, lc)
    build_tables(mll_ref, lc, ll)

    def wide(a):
        return jnp.concatenate([a, a], axis=1)

    def chain(hh, d, c, state, is_ctx, result):
        cs, ms = state
        last = CHUNK - 1 if d == 0 else 0
        if is_ctx:
            rows = pl.ds(pl.multiple_of(c * CHUNK, CHUNK), CHUNK)
            q_ref, k_ref, v_ref = qc_ref, kc_ref, vc_ref
        else:
            rows = pl.ds(pl.multiple_of((c - n_c) * CHUNK, CHUNK), CHUNK)
            q_ref, k_ref, v_ref = ql_ref, kl_ref, vl_ref
        q = (q_ref[0, rows, :].astype(F32) * hmask[hh]).astype(BF16)
        k = k_ref[0, rows, :].astype(F32) * (hmask[hh] * (ML_DK ** -0.5))
        v = jnp.concatenate([v_ref[0, rows, hh * HEAD_V:(hh + 1) * HEAD_V], ones_v], axis=1)
        n = chains.index((hh, d))
        trows = pl.ds(pl.multiple_of(c * CHUNK, CHUNK), CHUNK)
        gmb = tabs[3 * n, trows, :]
        gmb_t = jnp.transpose(gmb)[0:CHUNK, :]
        cm = tabs[3 * n + 1, trows, :]
        bc = tabs[3 * n + 2, trows, :]
        qk = _dot_nt(q, k.astype(BF16))
        yield
        cm_last = cm[last:last + 1, :]
        b_last = bc[last:last + 1, :]
        mm = jnp.maximum(ms, cm)
        p = jnp.where(incl[d], jnp.exp(gmb_t - mm[:, 0:CHUNK]), 0.0) * qk
        wk = (k * jnp.exp(gmb - cm_last)).astype(BF16)
        inter = _dot(q, cs.astype(BF16))
        intra = _dot(p.astype(BF16), v)
        c_loc = _dot_tn(wk, v)
        yield
        if not is_ctx:
            nd = wide(jnp.exp(ms - mm)) * inter + intra
            hv = nd[:, 0:HEAD_V] / jnp.maximum(jnp.abs(nd[:, HEAD_V:2 * HEAD_V]), jnp.exp(-(bc + mm)))
            l0 = pl.multiple_of((c - n_c) * CHUNK, CHUNK)
            out_s[pl.ds(l0, CHUNK), hh * HEAD_V:(hh + 1) * HEAD_V] += hv
        mx = jnp.maximum(ms, cm_last)
        result[hh, d] = (wide(jnp.exp(ms - mx)) * cs + wide(jnp.exp(cm_last - mx)) * c_loc, b_last + mx)

    out_s[...] = jnp.zeros(out_s.shape, F32)

    def run(chains_iter):
        live = list(chains_iter)
        while live:
            alive = []
            for ch in live:
                try:
                    next(ch)
                    alive.append(ch)
                except StopIteration:
                    pass
            live = alive

    def body(i, carry, is_ctx):
        result = {}
        gens = []
        for n, (hh, d) in enumerate(chains):
            if is_ctx:
                c = i if d == 0 else n_c - 1 - i
            else:
                c = n_c + i if d == 0 else n_t - 1 - i
            gens.append(chain(hh, d, c, carry[n], is_ctx, result))
        run(gens)
        return tuple(result[hd] for hd in chains)

    st0 = (jnp.zeros((LANES, 2 * HEAD_V), F32), jnp.zeros((1, LANES), F32))
    carry = lax.fori_loop(0, n_c, functools.partial(body, is_ctx=True), (st0,) * 4)
    lax.fori_loop(0, n_l, functools.partial(body, is_ctx=False), carry)
    h_ref[0] = out_s[...].astype(h_ref.dtype)


def _mlstm(proj_c, q_l, k_l, v_l, ml_c, ml_l):
    b, lc, _ = proj_c.shape
    ll = q_l.shape[1]
    lt = lc + ll
    qb, kb_, vb = COL_ML_Q // LANES, COL_ML_K // LANES, COL_ML_V // (2 * HEAD_V)
    return pl.pallas_call(
        _mlstm_kernel,
        grid=(b, HEADS // 2),
        in_specs=[pl.BlockSpec((1, lc, LANES), lambda i, p: (i, 0, qb + p)),
                  pl.BlockSpec((1, lc, LANES), lambda i, p: (i, 0, kb_ + p)),
                  pl.BlockSpec((1, lc, 2 * HEAD_V), lambda i, p: (i, 0, vb + p)),
                  pl.BlockSpec((1, ll, LANES), lambda i, p: (i, 0, p)),
                  pl.BlockSpec((1, ll, LANES), lambda i, p: (i, 0, p)),
                  pl.BlockSpec((1, ll, 2 * HEAD_V), lambda i, p: (i, 0, p)),
                  pl.BlockSpec((1, lc, LANES), lambda i, p: (i, 0, 0)),
                  pl.BlockSpec((1, ll, LANES), lambda i, p: (i, 0, 0))],
        out_specs=pl.BlockSpec((1, ll, 2 * HEAD_V), lambda i, p: (i, 0, p)),
        out_shape=jax.ShapeDtypeStruct((b, ll, HEADS * HEAD_V), BF16),
        scratch_shapes=[pltpu.VMEM((ll, 2 * HEAD_V), F32), pltpu.VMEM((12, lt, LANES), F32)],
        compiler_params=_params(("arbitrary", "arbitrary")),
    )(proj_c, proj_c, proj_c, q_l, k_l, v_l, ml_c, ml_l)


def _merge_kernel(yg_ref, hm_ref, o_ref, gg_ref, gm_ref, x_ref, mod_ref, mlg_ref, n2_ref,
                  wbg_ref, wbm_ref, wo_ref, wrh_ref, wrl_ref, x1_ref, h2_ref, sc_ref):
    d = x_ref.shape[1]
    o = o_ref[...].astype(F32)
    ym = _sigmoid(o) * hm_ref[...].astype(F32)
    segs = []
    for h in range(HEADS):
        seg = ym[:, h * HEAD_V:(h + 1) * HEAD_V]
        segs.append(seg * lax.rsqrt(jnp.mean(seg * seg, axis=-1, keepdims=True) + EPS))
    ymn = jnp.concatenate(segs, axis=1) * mlg_ref[...]
    y_gdn = _dot(yg_ref[...], wbg_ref[...])
    y_ml = _dot(ymn.astype(BF16), wbm_ref[...])
    mixed = _sigmoid(gg_ref[...].astype(F32)) * y_gdn + _sigmoid(gm_ref[...].astype(F32)) * y_ml
    y = _dot(mixed.astype(BF16), wo_ref[...])
    x1 = x_ref[...] + mod_ref[0, :, 2 * d:3 * d] * y
    x1_ref[...] = x1
    hn = x1 * lax.rsqrt(jnp.mean(x1 * x1, axis=-1, keepdims=True) + EPS) * n2_ref[...]
    h2 = hn * (1.0 + mod_ref[0, :, 4 * d:5 * d]) + mod_ref[0, :, 3 * d:4 * d]
    h2_hi = h2.astype(BF16)
    h2_ref[...] = h2_hi
    h2_lo = (h2 - h2_hi.astype(F32)).astype(BF16)
    logits = _dot_nt(wrh_ref[...], h2_hi) + (_dot_nt(wrl_ref[...], h2_hi) + _dot_nt(wrh_ref[...], h2_lo))
    sc_ref[...] = _sigmoid(logits)


def _merge(y_gdn, h_ml, proj_l2d, x2d, mod3, rows_per_mod, ml_norm_g, norm2_g, wbg, wbm, wo, wr_hi, wr_lo, tm=512):
    t, d = x2d.shape
    e = wr_hi.shape[0]
    row = lambda i: (i, 0)
    const = lambda i: (0, 0)
    return pl.pallas_call(
        _merge_kernel,
        grid=(t // tm,),
        in_specs=[pl.BlockSpec((tm, d), row), pl.BlockSpec((tm, d), row),
                  pl.BlockSpec((tm, d), lambda i: (i, COL_ML_O // d)),
                  pl.BlockSpec((tm, d), lambda i: (i, COL_MG_GDN // d)),
                  pl.BlockSpec((tm, d), lambda i: (i, COL_MG_ML // d)),
                  pl.BlockSpec((tm, d), row),
                  pl.BlockSpec((1, 1, mod3.shape[2]), lambda i: ((i * tm) // rows_per_mod, 0, 0)),
                  pl.BlockSpec((1, d), const), pl.BlockSpec((1, d), const),
                  pl.BlockSpec((d, d), const), pl.BlockSpec((d, d), const), pl.BlockSpec((d, d), const),
                  pl.BlockSpec((e, d), const), pl.BlockSpec((e, d), const)],
        out_specs=[pl.BlockSpec((tm, d), row), pl.BlockSpec((tm, d), row), pl.BlockSpec((e, tm), lambda i: (0, i))],
        out_shape=[jax.ShapeDtypeStruct((t, d), F32), jax.ShapeDtypeStruct((t, d), BF16),
                   jax.ShapeDtypeStruct((e, t), F32)],
        compiler_params=_params(("arbitrary",)),
    )(y_gdn, h_ml, proj_l2d, proj_l2d, proj_l2d, x2d, mod3, ml_norm_g.reshape(1, d), norm2_g.reshape(1, d),
      wbg, wbm, wo, wr_hi, wr_lo)


def _expert_kernel(be_ref, nu_ref, x_ref, wgu_ref, wd_ref, y_ref, wgu_s, wd_s):
    i = pl.program_id(0)
    de = wd_ref.shape[1]
    changed = jnp.logical_or(i == 0, be_ref[i] != be_ref[jnp.maximum(i - 1, 0)])

    @pl.when(changed)
    def _():
        wgu_s[...] = wgu_ref[0].astype(BF16)
        wd_s[...] = wd_ref[0].astype(BF16)

    @pl.when(i < nu_ref[0])
    def _():
        gu = _dot(x_ref[...], wgu_s[...])
        g = gu[:, 0:de]
        act = (g * _sigmoid(g)) * gu[:, de:2 * de]
        y_ref[...] = _dot(act.astype(BF16), wd_s[...]).astype(y_ref.dtype)

    @pl.when(i >= nu_ref[0])
    def _():
        y_ref[...] = jnp.zeros(y_ref.shape, y_ref.dtype)


def _experts(xb, blk_expert, n_used, w_gu, w_down):
    n_slots, d = xb.shape
    n_blocks = n_slots // EXPERT_ROWS
    e, _, de2 = w_gu.shape
    de = de2 // 2
    return pl.pallas_call(
        _expert_kernel,
        grid_spec=pltpu.PrefetchScalarGridSpec(
            num_scalar_prefetch=2,
            grid=(n_blocks,),
            in_specs=[pl.BlockSpec((EXPERT_ROWS, d), lambda i, be, nu: (i, 0)),
                      pl.BlockSpec((1, d, de2), lambda i, be, nu: (be[i], 0, 0)),
                      pl.BlockSpec((1, de, d), lambda i, be, nu: (be[i], 0, 0))],
            out_specs=pl.BlockSpec((EXPERT_ROWS, d), lambda i, be, nu: (i, 0)),
            scratch_shapes=[pltpu.VMEM((d, de2), BF16), pltpu.VMEM((de, d), BF16)]),
        out_shape=jax.ShapeDtypeStruct((n_slots, d), BF16),
        compiler_params=_params(("arbitrary",)),
    )(blk_expert, n_used, xb, w_gu, w_down)


def _final_kernel(x1_ref, h2_ref, yg_ref, wt_ref, mod_ref, wsg_ref, wsd_ref, fg_ref, o_ref):
    d = x1_ref.shape[1]
    ds_ = wsd_ref.shape[0]
    wt = wt_ref[...]
    routed = jnp.zeros(x1_ref.shape, F32)
    for k in range(TOP_K):
        routed = routed + wt[:, k:k + 1] * yg_ref[k].astype(F32)
    gu = _dot(h2_ref[...], wsg_ref[...])
    g = gu[:, 0:ds_]
    sh = _dot(((g * _sigmoid(g)) * gu[:, ds_:2 * ds_]).astype(BF16), wsd_ref[...])
    x2 = x1_ref[...] + mod_ref[0, :, 5 * d:6 * d] * (routed + sh)
    o_ref[...] = x2 * lax.rsqrt(jnp.mean(x2 * x2, axis=-1, keepdims=True) + EPS) * fg_ref[...]


def _final(x1, h2, yg, wts, mod3, rows_per_mod, w_sh_gu, w_sh_down, final_g, tm=256):
    t, d = x1.shape
    row = lambda i: (i, 0)
    const = lambda i: (0, 0)
    return pl.pallas_call(
        _final_kernel,
        grid=(t // tm,),
        in_specs=[pl.BlockSpec((tm, d), row), pl.BlockSpec((tm, d), row),
                  pl.BlockSpec((TOP_K, tm, d), lambda i: (0, i, 0)), pl.BlockSpec((tm, TOP_K), row),
                  pl.BlockSpec((1, 1, mod3.shape[2]), lambda i: ((i * tm) // rows_per_mod, 0, 0)),
                  pl.BlockSpec(w_sh_gu.shape, const), pl.BlockSpec(w_sh_down.shape, const),
                  pl.BlockSpec((1, d), const)],
        out_specs=pl.BlockSpec((tm, d), row),
        out_shape=jax.ShapeDtypeStruct((t, d), F32),
        compiler_params=_params(("arbitrary",)),
    )(x1, h2, yg, wts, mod3, w_sh_gu, w_sh_down, final_g.reshape(1, d))


def _route_kernel(sc_ref, bias_ref, tri_ref, idx_ref, wt_ref, rk_ref, cnt_ref, base_s):
    @pl.when(pl.program_id(0) == 0)
    def _():
        base_s[...] = jnp.zeros(base_s.shape, F32)

    scores = sc_ref[...]
    e, tn = scores.shape
    gsz = e // N_GROUPS
    sel3 = (scores + bias_ref[...]).reshape(N_GROUPS, gsz, tn)
    m1 = jnp.max(sel3, axis=1)
    is_max = sel3 == m1[:, None, :]
    n_max = jnp.sum(is_max.astype(F32), axis=1)
    m2 = jnp.max(jnp.where(is_max, -jnp.inf, sel3), axis=1)
    grp = m1 + jnp.where(n_max >= 2.0, m1, m2)
    gi = lax.broadcasted_iota(jnp.int32, (N_GROUPS, tn), 0)
    ahead = jnp.zeros((N_GROUPS, tn), F32)
    for g in range(N_GROUPS):
        row = grp[g:g + 1, :]
        ahead = ahead + jnp.logical_or(row > grp, jnp.logical_and(row == grp, g < gi)).astype(F32)
    ahead3 = jnp.broadcast_to(ahead[:, None, :], (N_GROUPS, gsz, tn))
    selm = jnp.where(ahead3 < float(TOPK_GROUPS), sel3, -jnp.inf).reshape(e, tn)
    ri = lax.broadcasted_iota(jnp.int32, (e, tn), 0).astype(F32)
    member = jnp.zeros((e, tn), F32)
    idxs, ws = [], []
    for _ in range(TOP_K):
        m = jnp.max(selm, axis=0, keepdims=True)
        idx = jnp.min(jnp.where(selm == m, ri, float(e)), axis=0, keepdims=True)
        hit = ri == idx
        ws.append(jnp.sum(jnp.where(hit, scores, 0.0), axis=0, keepdims=True))
        idxs.append(idx)
        selm = jnp.where(hit, -jnp.inf, selm)
        member = jnp.where(hit, 1.0, member)
    w = jnp.concatenate(ws, axis=0)
    wt_ref[...] = w / jnp.sum(w, axis=0, keepdims=True) * ROUTED_SCALE
    idx_ref[...] = jnp.concatenate(idxs, axis=0).astype(jnp.int32)
    cum = _dot(member.astype(BF16), tri_ref[...]) + base_s[...]
    rk_ref[...] = jnp.concatenate(
        [jnp.sum(jnp.where(ri == idx, cum, 0.0), axis=0, keepdims=True) for idx in idxs], axis=0).astype(jnp.int32)
    total = base_s[...] + jnp.sum(member, axis=1, keepdims=True)
    base_s[...] = total
    cnt_ref[...] = total


def _route(scores_t, router_bias):
    e, t = scores_t.shape
    tn = LANES
    bias = jnp.broadcast_to(router_bias.astype(F32)[:, None], (e, tn))
    tri = (jnp.arange(tn)[:, None] < jnp.arange(tn)[None, :]).astype(BF16)
    tok = pl.BlockSpec((TOP_K, tn), lambda i: (0, i))
    const = lambda i: (0, 0)
    return pl.pallas_call(
        _route_kernel,
        grid=(t // tn,),
        in_specs=[pl.BlockSpec((e, tn), lambda i: (0, i)), pl.BlockSpec((e, tn), const),
                  pl.BlockSpec((tn, tn), const)],
        out_specs=[tok, tok, tok, pl.BlockSpec((e, tn), const)],
        out_shape=[jax.ShapeDtypeStruct((TOP_K, t), jnp.int32), jax.ShapeDtypeStruct((TOP_K, t), F32),
                   jax.ShapeDtypeStruct((TOP_K, t), jnp.int32), jax.ShapeDtypeStruct((e, tn), F32)],
        scratch_shapes=[pltpu.VMEM((e, tn), F32)],
        compiler_params=_params(("arbitrary",)),
    )(scores_t, bias, tri)


def _slot_kernel(idx_ref, rk_ref, ps_ref, pos_ref):
    e, tn = ps_ref.shape
    ri = lax.broadcasted_iota(jnp.int32, (e, tn), 0)
    ps = ps_ref[...]
    rows = [jnp.sum(jnp.where(ri == idx_ref[k:k + 1, :], ps, 0.0), axis=0, keepdims=True) for k in range(TOP_K)]
    pos_ref[...] = rk_ref[...] + jnp.concatenate(rows, axis=0).astype(jnp.int32)


def _slots(idx, rank, pstart):
    k, t = idx.shape
    e = pstart.shape[0]
    tn = LANES
    tok = pl.BlockSpec((k, tn), lambda i: (0, i))
    return pl.pallas_call(
        _slot_kernel,
        grid=(t // tn,),
        in_specs=[tok, tok, pl.BlockSpec((e, tn), lambda i: (0, 0))],
        out_specs=tok,
        out_shape=jax.ShapeDtypeStruct((k, t), jnp.int32),
        compiler_params=_params(("arbitrary",)),
    )(idx, rank, jnp.broadcast_to(pstart.astype(F32)[:, None], (e, tn)))


def _block_table(counts, n_blocks):
    padded = (counts + EXPERT_ROWS - 1) // EXPERT_ROWS * EXPERT_ROWS
    pend = jnp.cumsum(padded)
    blk_expert = jnp.minimum(jnp.searchsorted(pend, jnp.arange(n_blocks) * EXPERT_ROWS, side='right'),
                             N_EXPERTS - 1).astype(jnp.int32)
    return pend - padded, blk_expert, (pend[-1] // EXPERT_ROWS).astype(jnp.int32).reshape(1)


def _col_major(t):
    b, l, f = t.shape
    rows = l // GRID_W
    return t.reshape(b, rows, GRID_W, f).transpose(0, 2, 1, 3).reshape(b, l, f)


def _row_major(t):
    b, l, f = t.shape
    rows = l // GRID_W
    return t.reshape(b, GRID_W, rows, f).transpose(0, 2, 1, 3).reshape(b, l, f)


def kernel(x, c, ctx, c_ctx, w_ada, b_ada, norm1_g, norm2_g, w_in, gdn_conv_w, gdn_a_log, gdn_dt_bias, gdn_norm_g,
           ml_i_bias, ml_f_bias, ml_norm_g, w_branch_gdn, w_branch_ml, w_out, w_router, router_bias, w_exp_gate_up,
           w_exp_down, w_sh_gate_up, w_sh_down, final_norm_g):
    b, l, d = x.shape
    lc = ctx.shape[1]
    t = b * l
    layer = 0

    w = w_in[layer]
    main_cols = [_ORIG[k] for k in ("gdn_qkv", "gdn_z", "ml_q", "ml_k", "ml_v", "ml_o", "mg_gdn", "mg_ml")]
    w_main = jnp.concatenate([w[:, a:e] for a, e in main_cols], axis=1).astype(BF16)
    w_gate = jnp.concatenate([w[:, _ORIG["gdn_gate"][0]:_ORIG["gdn_gate"][1]],
                              w[:, _ORIG["ml_gate"][0]:_ORIG["ml_gate"][1]],
                              jnp.zeros((d, LANES - 64), F32)], axis=1).astype(BF16)
    zeros16 = jnp.zeros((16,), F32)
    gp_add = jnp.concatenate([zeros16, gdn_dt_bias[layer].reshape(-1), ml_i_bias[layer].reshape(-1),
                              ml_f_bias[layer].reshape(-1), jnp.zeros((LANES - 64,), F32)])
    gp_mul = jnp.concatenate([zeros16, -jnp.exp(gdn_a_log[layer].astype(F32)).reshape(-1),
                              jnp.zeros((LANES - 32,), F32)])
    gparams = jnp.zeros((8, LANES), F32).at[0].set(gp_add).at[1].set(gp_mul)
    conv_w8 = jnp.zeros((8, gdn_conv_w.shape[2]), F32).at[0:GDN_CONV].set(gdn_conv_w[layer])
    wr = w_router[layer].T
    wr_hi = wr.astype(BF16)
    wr_lo = (wr - wr_hi.astype(F32)).astype(BF16)

    n_mod_rows = -(-(b + 1) // 8) * 8
    cc = jnp.zeros((n_mod_rows, d), F32).at[0:b].set(c).at[b].set(c_ctx)
    mod = _ada_mod(cc, w_ada[layer], b_ada[layer])
    mod3 = mod.reshape(n_mod_rows, 1, 6 * d)

    x2d = x.reshape(t, d)
    tm_l = min(1024, l)
    proj_l, gate_l = _project(x2d, mod3, lambda i: (i * tm_l) // l, norm1_g[layer], w_main, w_gate, tm_l)
    tm_c = min(1024, b * lc)
    proj_c, gate_c = _project(ctx.reshape(b * lc, d), mod3, lambda i: b, norm1_g[layer], w_main, w_gate, tm_c)
    proj_l3 = proj_l.reshape(b, l, N_MAIN)
    proj_c3 = proj_c.reshape(b, lc, N_MAIN)

    gate_l_cm = _col_major(gate_l.reshape(b, l, LANES)).reshape(t, LANES)
    gd_c, ml_c = _gate_prep(gate_c, gparams)
    gd_l, _ = _gate_prep(gate_l, gparams)
    _, ml_l = _gate_prep(gate_l_cm, gparams)

    y_gdn = _gdn(proj_c3, proj_l3, conv_w8, gd_c.reshape(b, lc, LANES), gd_l.reshape(b, l, LANES),
                 gdn_norm_g[layer])
    q_cm = _col_major(proj_l3[:, :, COL_ML_Q:COL_ML_Q + HEADS * ML_DK])
    k_cm = _col_major(proj_l3[:, :, COL_ML_K:COL_ML_K + HEADS * ML_DK])
    v_cm = _col_major(proj_l3[:, :, COL_ML_V:COL_ML_V + HEADS * HEAD_V])
    h_ml = _row_major(_mlstm(proj_c3, q_cm, k_cm, v_cm, ml_c.reshape(b, lc, LANES), ml_l.reshape(b, l, LANES)))

    x1, h2, scores_t = _merge(y_gdn.reshape(t, d), h_ml.reshape(t, d), proj_l, x2d, mod3, l, ml_norm_g[layer],
                            norm2_g[layer], w_branch_gdn[layer].astype(BF16), w_branch_ml[layer].astype(BF16),
                            w_out[layer].astype(BF16), wr_hi, wr_lo, tm=min(512, l))

    idx, wts, rank, cnt = _route(scores_t, router_bias[layer])
    n_assign = t * TOP_K
    n_blocks = (n_assign + N_EXPERTS * (EXPERT_ROWS - 1)) // EXPERT_ROWS + 1
    n_slots = n_blocks * EXPERT_ROWS
    pstart, blk_expert, n_used = _block_table(cnt[:, 0].astype(jnp.int32), n_blocks)
    pos = _slots(idx, rank, pstart).reshape(n_assign)
    tok_slot = (jnp.arange(n_slots, dtype=jnp.int32) % t).at[pos].set(
        jnp.arange(n_assign, dtype=jnp.int32) % t, unique_indices=True, mode="promise_in_bounds")
    xb = h2.at[tok_slot].get(mode="promise_in_bounds")
    yb = _experts(xb, blk_expert, n_used, w_exp_gate_up[layer], w_exp_down[layer])
    yg = yb.at[pos].get(mode="promise_in_bounds", unique_indices=True).reshape(TOP_K, t, d)
    out = _final(x1, h2, yg, wts.T, mod3, l, w_sh_gate_up[layer].astype(BF16), w_sh_down[layer].astype(BF16),
                 final_norm_g, tm=min(256, l))
    return out.reshape(b, l, d)
```

```python
import functools
import math

import jax
import jax.numpy as jnp
from jax import lax
from jax.experimental import pallas as pl
from jax.experimental.pallas import tpu as pltpu

F32 = jnp.float32
BF16 = jnp.bfloat16
HI = lax.Precision.HIGHEST

EPS = 1e-6
CHUNK = 64
GRID_W = 64
HEADS = 8
HEAD_V = 128
GDN_DK = 128
ML_DK = 64
GDN_CONV = 5
N_EXPERTS = 256
TOP_K = 8
N_GROUPS = 8
TOPK_GROUPS = 4
ROUTED_SCALE = 2.5
EXPERT_ROWS = 256
MOE_PARTS = 2
GDN_STEPS = 4
RING = 2 * GDN_STEPS
LANES = 128
VMEM_LIMIT = 56 * 1024 * 1024

COL_GDN_QKV = 0
COL_GDN_Z = 3072
COL_ML_Q = 4096
COL_ML_K = 4608
COL_ML_V = 5120
COL_ML_O = 6144
COL_MG_GDN = 7168
COL_MG_ML = 8192
N_MAIN = 9216
_ORIG = dict(gdn_qkv=(0, 3072), gdn_z=(3072, 4096), gdn_gate=(4096, 4128), ml_q=(4128, 4640),
             ml_k=(4640, 5152), ml_v=(5152, 6176), ml_o=(6176, 7200), ml_gate=(7200, 7232),
             mg_gdn=(7232, 8256), mg_ml=(8256, 9280))


def _params(sem, vmem=VMEM_LIMIT):
    return pltpu.CompilerParams(dimension_semantics=sem, vmem_limit_bytes=vmem)


def _dot(a, b, precision=None):
    return jnp.dot(a, b, preferred_element_type=F32, precision=precision)


def _dot_nt(a, b, precision=None):
    return lax.dot_general(a, b, (((1,), (1,)), ((), ())), preferred_element_type=F32, precision=precision)


def _dot_tn(a, b, precision=None):
    return lax.dot_general(a, b, (((0,), (0,)), ((), ())), preferred_element_type=F32, precision=precision)


def _times3(a, b):
    ah = a.astype(BF16)
    al = (a - ah.astype(F32)).astype(BF16)
    bh = b.astype(BF16)
    bl = (b - bh.astype(F32)).astype(BF16)
    return _dot(ah, bh) + (_dot(ah, bl) + _dot(al, bh))


def _sigmoid(x):
    return 1.0 / (1.0 + jnp.exp(-x))


def _softplus(x):
    return jnp.maximum(x, 0.0) + jnp.log(1.0 + jnp.exp(-jnp.abs(x)))


def _ada_kernel(c_ref, w_ref, b_ref, o_ref):
    c = c_ref[...]
    sc = c * _sigmoid(c)
    o_ref[...] = _dot(sc, w_ref[...], HI) + b_ref[...]


def _ada_mod(cc, w_ada, b_ada, tn=1536):
    r, d = cc.shape
    n = w_ada.shape[1]
    return pl.pallas_call(
        _ada_kernel,
        grid=(n // tn,),
        in_specs=[pl.BlockSpec((r, d), lambda j: (0, 0)),
                  pl.BlockSpec((d, tn), lambda j: (0, j)),
                  pl.BlockSpec((1, tn), lambda j: (0, j))],
        out_specs=pl.BlockSpec((r, tn), lambda j: (0, j)),
        out_shape=jax.ShapeDtypeStruct((r, n), F32),
        compiler_params=_params(("arbitrary",)),
    )(cc, w_ada, b_ada.reshape(1, n))


def _proj_kernel(x_ref, mod_ref, g_ref, w_ref, wg_ref, o_ref, og_ref, hn_ref):
    d = x_ref.shape[1]

    @pl.when(pl.program_id(1) == 0)
    def _():
        x = x_ref[...]
        y = x * lax.rsqrt(jnp.mean(x * x, axis=-1, keepdims=True) + EPS) * g_ref[...]
        shift = mod_ref[0, :, 0:d]
        scale = mod_ref[0, :, d:2 * d]
        h = (y * (1.0 + scale) + shift).astype(BF16)
        hn_ref[...] = h
        og_ref[...] = _dot(h, wg_ref[...])

    o_ref[...] = _dot(hn_ref[...], w_ref[...]).astype(o_ref.dtype)


def _project(x2d, mod3, mod_row_of_tile, norm_g, w_main, w_gate, tm, tn=1024):
    t, d = x2d.shape
    n = w_main.shape[1]
    return pl.pallas_call(
        _proj_kernel,
        grid=(t // tm, n // tn),
        in_specs=[pl.BlockSpec((tm, d), lambda i, j: (i, 0)),
                  pl.BlockSpec((1, 1, mod3.shape[2]), lambda i, j: (mod_row_of_tile(i), 0, 0)),
                  pl.BlockSpec((1, d), lambda i, j: (0, 0)),
                  pl.BlockSpec((d, tn), lambda i, j: (0, j)),
                  pl.BlockSpec((d, LANES), lambda i, j: (0, 0))],
        out_specs=[pl.BlockSpec((tm, tn), lambda i, j: (i, j)),
                   pl.BlockSpec((tm, LANES), lambda i, j: (i, 0))],
        out_shape=[jax.ShapeDtypeStruct((t, n), BF16), jax.ShapeDtypeStruct((t, LANES), F32)],
        scratch_shapes=[pltpu.VMEM((tm, d), BF16)],
        compiler_params=_params(("arbitrary", "arbitrary")),
    )(x2d, mod3, norm_g.reshape(1, d), w_main, w_gate)


def _gate_kernel(g_ref, p_ref, gd_ref, ml_ref):
    rows = g_ref.shape[0]
    raw = g_ref[...] + p_ref[0:1, :]
    lane = lax.broadcasted_iota(jnp.int32, raw.shape, 1)
    sp = _softplus(raw)
    vals = jnp.where(lane < 16, _sigmoid(raw),
                     jnp.where(lane < 32, p_ref[1:2, :] * sp,
                               jnp.where(lane < 48, raw,
                                         jnp.where(lane < 64, -_softplus(-raw), 0.0))))
    ri = lax.broadcasted_iota(jnp.int32, (CHUNK, CHUNK), 0)
    ci = lax.broadcasted_iota(jnp.int32, (CHUNK, CHUNK), 1)
    tri_f = (ri >= ci).astype(F32)
    tri_b = (ri <= ci).astype(F32)
    lane_c = lax.broadcasted_iota(jnp.int32, (CHUNK, LANES), 1)
    row_c = lax.broadcasted_iota(jnp.int32, (CHUNK, LANES), 0)
    fwd_lane = (lane_c % 16) < 8
    for c in range(rows // CHUNK):
        blk = vals[c * CHUNK:(c + 1) * CHUNK, :]
        cum = jnp.where(fwd_lane, _dot(tri_f, blk, HI), _dot(tri_b, blk, HI))
        gd_ref[c * CHUNK:(c + 1) * CHUNK, :] = jnp.where(lane_c < 16, blk, jnp.where(lane_c < 32, cum, 0.0))
        bcum = pltpu.roll(cum, LANES - 16, axis=1)
        gmb = blk - bcum
        cmf, cmb = gmb, gmb
        for s in (1, 2, 4, 8, 16, 32):
            cmf = jnp.maximum(cmf, jnp.where(row_c >= s, pltpu.roll(cmf, s, axis=0), -jnp.inf))
            cmb = jnp.maximum(cmb, jnp.where(row_c < CHUNK - s, pltpu.roll(cmb, CHUNK - s, axis=0), -jnp.inf))
        cm = jnp.where(fwd_lane, cmf, cmb)
        ml = jnp.where(lane_c < 16, pltpu.roll(gmb, LANES - 32, axis=1),
                       jnp.where(lane_c < 32, pltpu.roll(cm, LANES - 16, axis=1),
                                 jnp.where(lane_c < 48, bcum, 0.0)))
        ml_ref[c * CHUNK:(c + 1) * CHUNK, :] = ml


def _gate_prep(graw, gparams, tm=256):
    t = graw.shape[0]
    spec = pl.BlockSpec((tm, LANES), lambda i: (i, 0))
    return pl.pallas_call(
        _gate_kernel,
        grid=(t // tm,),
        in_specs=[spec, pl.BlockSpec((8, LANES), lambda i: (0, 0))],
        out_specs=[spec, spec],
        out_shape=[jax.ShapeDtypeStruct((t, LANES), F32)] * 2,
        compiler_params=_params(("arbitrary",)),
    )(graw, gparams)


def _split3(a):
    h = a.astype(BF16)
    r = a - h.astype(F32)
    m = r.astype(BF16)
    return h, m, (r - m.astype(F32)).astype(BF16)


def _lane_picks(x, lanes):
    li = lax.broadcasted_iota(jnp.int32, (LANES, LANES), 0)
    ci = lax.broadcasted_iota(jnp.int32, (LANES, LANES), 1)
    want = jnp.full((LANES, LANES), -1, jnp.int32)
    for j, lane in enumerate(lanes):
        want = jnp.where(ci == j, lane, want)
    sel = (li == want).astype(BF16)
    h, m, lo = _split3(x)
    cols = _dot(h, sel) + (_dot(m, sel) + _dot(lo, sel))
    return [jnp.broadcast_to(cols[:, j:j + 1], x.shape) for j in range(len(lanes))]


def _dir_masks(direction):
    ri = lax.broadcasted_iota(jnp.int32, (CHUNK, CHUNK), 0)
    ci = lax.broadcasted_iota(jnp.int32, (CHUNK, CHUNK), 1)
    if direction == 0:
        return ri >= ci, ri > ci
    return ri <= ci, ri < ci


def _gdn_kernel(qc_ref, kc_ref, vc_ref, ql_ref, kl_ref, vl_ref, z_ref, cwq_ref, cwk_ref, cwv_ref,
                gdc_ref, gdl_ref, ng_ref, y_ref,
                xpad, qs, ks, vs, beta_t, cg_t, wq_r, u_r, kd_r, qk_r, dc_r, out_s):
    lc = qc_ref.shape[1]
    ll = ql_ref.shape[1]
    lt = lc + ll
    n_c, n_l = lc // CHUNK, ll // CHUNK
    n_t = n_c + n_l
    rb = 256

    def l2n(x):
        return x * lax.rsqrt(jnp.sum(x * x, axis=-1, keepdims=True) + EPS)

    def prep(src_ref, cw_ref, dst, off, ls, kind):
        xpad[0:8, :] = jnp.zeros((8, LANES), F32)
        xpad[8:8 + ls, :] = src_ref[0].astype(F32)
        xpad[8 + ls:16 + ls, :] = jnp.zeros((8, LANES), F32)
        step = min(rb, ls)
        for r0 in range(0, ls, step):
            acc = jnp.zeros((step, LANES), F32)
            for t in range(GDN_CONV):
                s0 = r0 + 8 - GDN_CONV // 2 + t
                acc = acc + cw_ref[t:t + 1, :] * xpad[s0:s0 + step, :]
            y = acc * _sigmoid(acc)
            if kind == "q":
                y = l2n(y) * (GDN_DK ** -0.5)
            elif kind == "k":
                y = l2n(y)
            dst[off + r0:off + r0 + step, :] = y

    prep(qc_ref, cwq_ref, qs, 0, lc, "q")
    prep(kc_ref, cwk_ref, ks, 0, lc, "k")
    prep(vc_ref, cwv_ref, vs, 0, lc, "v")
    prep(ql_ref, cwq_ref, qs, lc, ll, "q")
    prep(kl_ref, cwk_ref, ks, lc, ll, "k")
    prep(vl_ref, cwv_ref, vs, lc, ll, "v")

    eye = (lax.broadcasted_iota(jnp.int32, (CHUNK, CHUNK), 0)
           == lax.broadcasted_iota(jnp.int32, (CHUNK, CHUNK), 1)).astype(F32)

    masks = (_dir_masks(0), _dir_masks(1))
    head = pl.program_id(1)

    def build_tables(src_ref, off, ls):
        step = min(rb, ls)
        for r0 in range(0, ls, step):
            picked = _lane_picks(src_ref[0, r0:r0 + step, :], [8 * d + head for d in range(2)]
                                 + [16 + 8 * d + head for d in range(2)])
            for d in range(2):
                beta_t[d, off + r0:off + r0 + step, :] = picked[d]
                cg_t[d, off + r0:off + r0 + step, :] = picked[2 + d]

    build_tables(gdc_ref, 0, lc)
    build_tables(gdl_ref, lc, ll)

    def bwd_chunk(t):
        return jnp.where(t < n_c, n_c - 1 - t, n_t + n_c - 1 - t)

    def prep_chain(t, d):
        incl, strict = masks[d]
        last = CHUNK - 1 if d == 0 else 0
        tc = jnp.minimum(t, n_t - 1)
        c = tc if d == 0 else bwd_chunk(tc)
        r0 = pl.multiple_of(c * CHUNK, CHUNK)
        q = qs[pl.ds(r0, CHUNK), :]
        k = ks[pl.ds(r0, CHUNK), :]
        v = vs[pl.ds(r0, CHUNK), :]
        beta = beta_t[d, pl.ds(r0, CHUNK), :]
        cgc = cg_t[d, pl.ds(r0, CHUNK), :]
        cgr = jnp.transpose(cgc)[0:CHUNK, :]
        kk = _dot_nt(k, k)
        qk = _dot_nt(q, k)
        slot = (t % RING) * 2 + d
        yield
        cg_last = cgc[last:last + 1, :]
        decay = jnp.exp(jnp.where(incl, cgc[:, 0:CHUNK] - cgr, -jnp.inf))
        ecg = jnp.exp(cgc)
        kb = k * beta
        qk_r[slot] = qk * decay
        wq_r[slot, CHUNK:2 * CHUNK, :] = q * ecg
        kd_r[slot] = k * jnp.exp(cg_last - cgc)
        dc_r[slot] = jnp.broadcast_to(jnp.exp(cg_last), (8, LANES))
        x = jnp.where(strict, -(beta[:, 0:CHUNK] * kk) * decay, 0.0)
        tinv = eye + x
        x = _times3(x, x)
        yield
        for _ in range(4):
            both = _times3(jnp.concatenate([tinv, x], axis=0), x)
            tinv, x = tinv + both[0:CHUNK, :], both[CHUNK:2 * CHUNK, :]
            yield
        tinv = tinv + _times3(tinv, x)
        yield
        wq_r[slot, 0:CHUNK, :] = _dot(tinv, kb * ecg)
        u_r[slot] = _dot(tinv, v * beta)

    out_s[...] = jnp.zeros(out_s.shape, F32)

    def scan_chain(d, t0, s, with_out, result):
        for j in range(GDN_STEPS):
            t = t0 + j
            slot = (t % RING) * 2 + d
            ws = _dot(wq_r[slot], s)
            yield
            v_new = u_r[slot] - ws[0:CHUNK, :]
            if with_out:
                c = t if d == 0 else bwd_chunk(t)
                o = ws[CHUNK:2 * CHUNK, :] + _dot(qk_r[slot], v_new)
                l0 = pl.multiple_of((c - n_c) * CHUNK, CHUNK)
                out_s[pl.ds(l0, CHUNK), :] += o
            s = s * dc_r[slot][0:1, :] + _dot_tn(kd_r[slot], v_new)
            yield
        result[d] = s

    def lockstep(chains):
        chains = list(chains)
        while chains:
            alive = []
            for ch in chains:
                try:
                    next(ch)
                    alive.append(ch)
                except StopIteration:
                    pass
            chains = alive

    def pair_body(i, carry, t_base, with_out):
        t0 = t_base + GDN_STEPS * i
        result = [None, None]
        lockstep([scan_chain(0, t0, carry[0], with_out, result), scan_chain(1, t0, carry[1], with_out, result)]
                 + [prep_chain(t0 + GDN_STEPS + j, d) for j in range(GDN_STEPS) for d in range(2)])
        return result[0], result[1]

    lockstep([prep_chain(j, d) for j in range(GDN_STEPS) for d in range(2)])
    s0 = jnp.zeros((GDN_DK, HEAD_V), F32)
    carry = lax.fori_loop(0, n_c // GDN_STEPS, functools.partial(pair_body, t_base=0, with_out=False), (s0, s0))
    lax.fori_loop(0, n_l // GDN_STEPS, functools.partial(pair_body, t_base=n_c, with_out=True), carry)


    def out_body(i, carry):
        r0 = pl.multiple_of(i * rb, rb)
        o = out_s[pl.ds(r0, rb), :]
        z = z_ref[0, pl.ds(r0, rb), :].astype(F32)
        y = o * lax.rsqrt(jnp.mean(o * o, axis=-1, keepdims=True) + EPS) * ng_ref[...]
        y_ref[0, pl.ds(r0, rb), :] = (y * (z * _sigmoid(z))).astype(y_ref.dtype)
        return carry

    lax.fori_loop(0, ll // rb, out_body, 0)


def _gdn(proj_c, proj_l, conv_w8, gd_c, gd_l, norm_g):
    b, lc, _ = proj_c.shape
    ll = proj_l.shape[1]
    lt = lc + ll
    n_t = lt // CHUNK
    qb, kb_, vb, zb = (COL_GDN_QKV // LANES, COL_GDN_QKV // LANES + HEADS, COL_GDN_QKV // LANES + 2 * HEADS,
                       COL_GDN_Z // LANES)

    def seq_spec(l, col0):
        return pl.BlockSpec((1, l, LANES), lambda i, h: (i, 0, col0 + h))

    def cw_spec(col0):
        return pl.BlockSpec((8, LANES), lambda i, h: (0, col0 + h))

    return pl.pallas_call(
        _gdn_kernel,
        grid=(b, HEADS),
        in_specs=[seq_spec(lc, qb), seq_spec(lc, kb_), seq_spec(lc, vb),
                  seq_spec(ll, qb), seq_spec(ll, kb_), seq_spec(ll, vb), seq_spec(ll, zb),
                  cw_spec(0), cw_spec(HEADS), cw_spec(2 * HEADS),
                  pl.BlockSpec((1, lc, LANES), lambda i, h: (i, 0, 0)),
                  pl.BlockSpec((1, ll, LANES), lambda i, h: (i, 0, 0)),
                  pl.BlockSpec((1, LANES), lambda i, h: (0, 0))],
        out_specs=pl.BlockSpec((1, ll, LANES), lambda i, h: (i, 0, h)),
        out_shape=jax.ShapeDtypeStruct((b, ll, HEADS * HEAD_V), BF16),
        scratch_shapes=[pltpu.VMEM((max(lc, ll) + 16, LANES), F32),
                        pltpu.VMEM((lt, LANES), F32), pltpu.VMEM((lt, LANES), F32), pltpu.VMEM((lt, LANES), F32),
                        pltpu.VMEM((2, lt, LANES), F32), pltpu.VMEM((2, lt, LANES), F32),
                        pltpu.VMEM((2 * RING, 2 * CHUNK, LANES), F32),
                        pltpu.VMEM((2 * RING, CHUNK, LANES), F32), pltpu.VMEM((2 * RING, CHUNK, LANES), F32),
                        pltpu.VMEM((2 * RING, CHUNK, CHUNK), F32),
                        pltpu.VMEM((2 * RING, 8, LANES), F32),
                        pltpu.VMEM((ll, LANES), F32)],
        compiler_params=_params(("arbitrary", "arbitrary")),
    )(proj_c, proj_c, proj_c, proj_l, proj_l, proj_l, proj_l, conv_w8, conv_w8, conv_w8,
      gd_c, gd_l, norm_g.reshape(1, LANES))


def _mlstm_kernel(qc_ref, kc_ref, vc_ref, ql_ref, kl_ref, vl_ref, mlc_ref, mll_ref, h_ref, out_s, tabs):
    lc = qc_ref.shape[1]
    ll = ql_ref.shape[1]
    n_c, n_l = lc // CHUNK, ll // CHUNK
    n_t = n_c + n_l
    lt = lc + ll
    pair = pl.program_id(1)
    lane = lax.broadcasted_iota(jnp.int32, (CHUNK, LANES), 1)
    ones_v = jnp.ones((CHUNK, HEAD_V), BF16)
    chains = [(hh, d) for hh in range(2) for d in range(2)]
    hmask = [((lane // ML_DK) == hh).astype(F32) for hh in range(2)]
    incl = [_dir_masks(d)[0] for d in range(2)]

    def build_tables(src_ref, off, ls):
        step = min(256, ls)
        for r0 in range(0, ls, step):
            lanes = [16 * j + 8 * d + 2 * pair + hh for hh, d in chains for j in range(3)]
            for g, tab in enumerate(_lane_picks(src_ref[0, r0:r0 + step, :], lanes)):
                tabs[g, off + r0:off + r0 + step, :] = tab

    build_tables(mlc_ref, 0, lc)
    build_tables(mll_ref, lc, ll)

    def wide(a):
        return jnp.concatenate([a, a], axis=1)

    def chain(hh, d, c, state, is_ctx, result):
        cs, ms = state
        last = CHUNK - 1 if d == 0 else 0
        if is_ctx:
            rows = pl.ds(pl.multiple_of(c * CHUNK, CHUNK), CHUNK)
            q_ref, k_ref, v_ref = qc_ref, kc_ref, vc_ref
        else:
            rows = pl.ds(pl.multiple_of((c - n_c) * CHUNK, CHUNK), CHUNK)
            q_ref, k_ref, v_ref = ql_ref, kl_ref, vl_ref
        q = (q_ref[0, rows, :].astype(F32) * hmask[hh]).astype(BF16)
        k = k_ref[0, rows, :].astype(F32) * (hmask[hh] * (ML_DK ** -0.5))
        v = jnp.concatenate([v_ref[0, rows, hh * HEAD_V:(hh + 1) * HEAD_V], ones_v], axis=1)
        n = chains.index((hh, d))
        trows = pl.ds(pl.multiple_of(c * CHUNK, CHUNK), CHUNK)
        gmb = tabs[3 * n, trows, :]
        gmb_t = jnp.transpose(gmb)[0:CHUNK, :]
        cm = tabs[3 * n + 1, trows, :]
        bc = tabs[3 * n + 2, trows, :]
        qk = _dot_nt(q, k.astype(BF16))
        yield
        cm_last = cm[last:last + 1, :]
        b_last = bc[last:last + 1, :]
        mm = jnp.maximum(ms, cm)
        p = jnp.where(incl[d], jnp.exp(gmb_t - mm[:, 0:CHUNK]), 0.0) * qk
        wk = (k * jnp.exp(gmb - cm_last)).astype(BF16)
        inter = _dot(q, cs.astype(BF16))
        intra = _dot(p.astype(BF16), v)
        c_loc = _dot_tn(wk, v)
        yield
        if not is_ctx:
            nd = wide(jnp.exp(ms - mm)) * inter + intra
            hv = nd[:, 0:HEAD_V] / jnp.maximum(jnp.abs(nd[:, HEAD_V:2 * HEAD_V]), jnp.exp(-(bc + mm)))
            l0 = pl.multiple_of((c - n_c) * CHUNK, CHUNK)
            out_s[pl.ds(l0, CHUNK), hh * HEAD_V:(hh + 1) * HEAD_V] += hv
        mx = jnp.maximum(ms, cm_last)
        result[hh, d] = (wide(jnp.exp(ms - mx)) * cs + wide(jnp.exp(cm_last - mx)) * c_loc, b_last + mx)

    out_s[...] = jnp.zeros(out_s.shape, F32)

    def run(chains_iter):
        live = list(chains_iter)
        while live:
            alive = []
            for ch in live:
                try:
                    next(ch)
                    alive.append(ch)
                except StopIteration:
                    pass
            live = alive

    def body(i, carry, is_ctx):
        result = {}
        gens = []
        for n, (hh, d) in enumerate(chains):
            if is_ctx:
                c = i if d == 0 else n_c - 1 - i
            else:
                c = n_c + i if d == 0 else n_t - 1 - i
            gens.append(chain(hh, d, c, carry[n], is_ctx, result))
        run(gens)
        return tuple(result[hd] for hd in chains)

    st0 = (jnp.zeros((LANES, 2 * HEAD_V), F32), jnp.zeros((1, LANES), F32))
    carry = lax.fori_loop(0, n_c, functools.partial(body, is_ctx=True), (st0,) * 4)
    lax.fori_loop(0, n_l, functools.partial(body, is_ctx=False), carry)
    h_ref[0] = out_s[...].astype(h_ref.dtype)


def _mlstm(proj_c, q_l, k_l, v_l, ml_c, ml_l):
    b, lc, _ = proj_c.shape
    ll = q_l.shape[1]
    lt = lc + ll
    qb, kb_, vb = COL_ML_Q // LANES, COL_ML_K // LANES, COL_ML_V // (2 * HEAD_V)
    return pl.pallas_call(
        _mlstm_kernel,
        grid=(b, HEADS // 2),
        in_specs=[pl.BlockSpec((1, lc, LANES), lambda i, p: (i, 0, qb + p)),
                  pl.BlockSpec((1, lc, LANES), lambda i, p: (i, 0, kb_ + p)),
                  pl.BlockSpec((1, lc, 2 * HEAD_V), lambda i, p: (i, 0, vb + p)),
                  pl.BlockSpec((1, ll, LANES), lambda i, p: (i, 0, p)),
                  pl.BlockSpec((1, ll, LANES), lambda i, p: (i, 0, p)),
                  pl.BlockSpec((1, ll, 2 * HEAD_V), lambda i, p: (i, 0, p)),
                  pl.BlockSpec((1, lc, LANES), lambda i, p: (i, 0, 0)),
                  pl.BlockSpec((1, ll, LANES), lambda i, p: (i, 0, 0))],
        out_specs=pl.BlockSpec((1, ll, 2 * HEAD_V), lambda i, p: (i, 0, p)),
        out_shape=jax.ShapeDtypeStruct((b, ll, HEADS * HEAD_V), BF16),
        scratch_shapes=[pltpu.VMEM((ll, 2 * HEAD_V), F32), pltpu.VMEM((12, lt, LANES), F32)],
        compiler_params=_params(("arbitrary", "arbitrary")),
    )(proj_c, proj_c, proj_c, q_l, k_l, v_l, ml_c, ml_l)


def _merge_kernel(yg_ref, hm_ref, o_ref, gg_ref, gm_ref, x_ref, mod_ref, mlg_ref, n2_ref,
                  wbg_ref, wbm_ref, wo_ref, wrh_ref, wrl_ref, x1_ref, h2_ref, sc_ref):
    d = x_ref.shape[1]
    o = o_ref[...].astype(F32)
    ym = _sigmoid(o) * hm_ref[...].astype(F32)
    segs = []
    for h in range(HEADS):
        seg = ym[:, h * HEAD_V:(h + 1) * HEAD_V]
        segs.append(seg * lax.rsqrt(jnp.mean(seg * seg, axis=-1, keepdims=True) + EPS))
    ymn = jnp.concatenate(segs, axis=1) * mlg_ref[...]
    y_gdn = _dot(yg_ref[...], wbg_ref[...])
    y_ml = _dot(ymn.astype(BF16), wbm_ref[...])
    mixed = _sigmoid(gg_ref[...].astype(F32)) * y_gdn + _sigmoid(gm_ref[...].astype(F32)) * y_ml
    y = _dot(mixed.astype(BF16), wo_ref[...])
    x1 = x_ref[...] + mod_ref[0, :, 2 * d:3 * d] * y
    x1_ref[...] = x1
    hn = x1 * lax.rsqrt(jnp.mean(x1 * x1, axis=-1, keepdims=True) + EPS) * n2_ref[...]
    h2 = hn * (1.0 + mod_ref[0, :, 4 * d:5 * d]) + mod_ref[0, :, 3 * d:4 * d]
    h2_hi = h2.astype(BF16)
    h2_ref[...] = h2_hi
    h2_lo = (h2 - h2_hi.astype(F32)).astype(BF16)
    logits = _dot_nt(wrh_ref[...], h2_hi) + (_dot_nt(wrl_ref[...], h2_hi) + _dot_nt(wrh_ref[...], h2_lo))
    sc_ref[...] = _sigmoid(logits)


def _merge(y_gdn, h_ml, proj_l2d, x2d, mod3, rows_per_mod, ml_norm_g, norm2_g, wbg, wbm, wo, wr_hi, wr_lo, tm=512):
    t, d = x2d.shape
    e = wr_hi.shape[0]
    row = lambda i: (i, 0)
    const = lambda i: (0, 0)
    return pl.pallas_call(
        _merge_kernel,
        grid=(t // tm,),
        in_specs=[pl.BlockSpec((tm, d), row), pl.BlockSpec((tm, d), row),
                  pl.BlockSpec((tm, d), lambda i: (i, COL_ML_O // d)),
                  pl.BlockSpec((tm, d), lambda i: (i, COL_MG_GDN // d)),
                  pl.BlockSpec((tm, d), lambda i: (i, COL_MG_ML // d)),
                  pl.BlockSpec((tm, d), row),
                  pl.BlockSpec((1, 1, mod3.shape[2]), lambda i: ((i * tm) // rows_per_mod, 0, 0)),
                  pl.BlockSpec((1, d), const), pl.BlockSpec((1, d), const),
                  pl.BlockSpec((d, d), const), pl.BlockSpec((d, d), const), pl.BlockSpec((d, d), const),
                  pl.BlockSpec((e, d), const), pl.BlockSpec((e, d), const)],
        out_specs=[pl.BlockSpec((tm, d), row), pl.BlockSpec((tm, d), row), pl.BlockSpec((e, tm), lambda i: (0, i))],
        out_shape=[jax.ShapeDtypeStruct((t, d), F32), jax.ShapeDtypeStruct((t, d), BF16),
                   jax.ShapeDtypeStruct((e, t), F32)],
        compiler_params=_params(("arbitrary",)),
    )(y_gdn, h_ml, proj_l2d, proj_l2d, proj_l2d, x2d, mod3, ml_norm_g.reshape(1, d), norm2_g.reshape(1, d),
      wbg, wbm, wo, wr_hi, wr_lo)


def _expert_kernel(be_ref, nu_ref, x_ref, wgu_ref, wd_ref, y_ref, wgu_s, wd_s):
    i = pl.program_id(0)
    de = wd_ref.shape[1]
    changed = jnp.logical_or(i == 0, be_ref[i] != be_ref[jnp.maximum(i - 1, 0)])

    @pl.when(changed)
    def _():
        wgu_s[...] = wgu_ref[0].astype(BF16)
        wd_s[...] = wd_ref[0].astype(BF16)

    @pl.when(i < nu_ref[0])
    def _():
        gu = _dot(x_ref[...], wgu_s[...])
        g = gu[:, 0:de]
        act = (g * _sigmoid(g)) * gu[:, de:2 * de]
        y_ref[...] = _dot(act.astype(BF16), wd_s[...]).astype(y_ref.dtype)

    @pl.when(i >= nu_ref[0])
    def _():
        y_ref[...] = jnp.zeros(y_ref.shape, y_ref.dtype)


def _experts(xb, blk_expert, n_used, w_gu, w_down):
    n_slots, d = xb.shape
    n_blocks = n_slots // EXPERT_ROWS
    e, _, de2 = w_gu.shape
    de = de2 // 2
    return pl.pallas_call(
        _expert_kernel,
        grid_spec=pltpu.PrefetchScalarGridSpec(
            num_scalar_prefetch=2,
            grid=(n_blocks,),
            in_specs=[pl.BlockSpec((EXPERT_ROWS, d), lambda i, be, nu: (i, 0)),
                      pl.BlockSpec((1, d, de2), lambda i, be, nu: (be[i], 0, 0)),
                      pl.BlockSpec((1, de, d), lambda i, be, nu: (be[i], 0, 0))],
            out_specs=pl.BlockSpec((EXPERT_ROWS, d), lambda i, be, nu: (i, 0)),
            scratch_shapes=[pltpu.VMEM((d, de2), BF16), pltpu.VMEM((de, d), BF16)]),
        out_shape=jax.ShapeDtypeStruct((n_slots, d), BF16),
        compiler_params=_params(("arbitrary",)),
    )(blk_expert, n_used, xb, w_gu, w_down)


def _final_kernel(x1_ref, h2_ref, yg_ref, wt_ref, mod_ref, wsg_ref, wsd_ref, fg_ref, o_ref):
    d = x1_ref.shape[1]
    ds_ = wsd_ref.shape[0]
    wt = wt_ref[...]
    routed = jnp.zeros(x1_ref.shape, F32)
    for k in range(TOP_K):
        routed = routed + wt[:, k:k + 1] * yg_ref[k].astype(F32)
    gu = _dot(h2_ref[...], wsg_ref[...])
    g = gu[:, 0:ds_]
    sh = _dot(((g * _sigmoid(g)) * gu[:, ds_:2 * ds_]).astype(BF16), wsd_ref[...])
    x2 = x1_ref[...] + mod_ref[0, :, 5 * d:6 * d] * (routed + sh)
    o_ref[...] = x2 * lax.rsqrt(jnp.mean(x2 * x2, axis=-1, keepdims=True) + EPS) * fg_ref[...]


def _final(x1, h2, yg, wts, mod3, rows_per_mod, w_sh_gu, w_sh_down, final_g, tok0, tm=256):
    tp = yg.shape[1]
    d = x1.shape[1]
    off = tok0 // tm
    row = lambda i: (i, 0)
    row_off = lambda i: (i + off, 0)
    const = lambda i: (0, 0)
    return pl.pallas_call(
        _final_kernel,
        grid=(tp // tm,),
        in_specs=[pl.BlockSpec((tm, d), row_off), pl.BlockSpec((tm, d), row_off),
                  pl.BlockSpec((TOP_K, tm, d), lambda i: (0, i, 0)), pl.BlockSpec((tm, TOP_K), row),
                  pl.BlockSpec((1, 1, mod3.shape[2]), lambda i: (((i + off) * tm) // rows_per_mod, 0, 0)),
                  pl.BlockSpec(w_sh_gu.shape, const), pl.BlockSpec(w_sh_down.shape, const),
                  pl.BlockSpec((1, d), const)],
        out_specs=pl.BlockSpec((tm, d), row),
        out_shape=jax.ShapeDtypeStruct((tp, d), F32),
        compiler_params=_params(("arbitrary",)),
    )(x1, h2, yg, wts, mod3, w_sh_gu, w_sh_down, final_g.reshape(1, d))


def _route_kernel(sc_ref, bias_ref, tri_ref, idx_ref, wt_ref, rk_ref, cnt_ref, base_s):
    @pl.when(pl.program_id(0) == 0)
    def _():
        base_s[...] = jnp.zeros(base_s.shape, F32)

    scores = sc_ref[...]
    e, tn = scores.shape
    gsz = e // N_GROUPS
    sel3 = (scores + bias_ref[...]).reshape(N_GROUPS, gsz, tn)
    m1 = jnp.max(sel3, axis=1)
    is_max = sel3 == m1[:, None, :]
    n_max = jnp.sum(is_max.astype(F32), axis=1)
    m2 = jnp.max(jnp.where(is_max, -jnp.inf, sel3), axis=1)
    grp = m1 + jnp.where(n_max >= 2.0, m1, m2)
    gi = lax.broadcasted_iota(jnp.int32, (N_GROUPS, tn), 0)
    ahead = jnp.zeros((N_GROUPS, tn), F32)
    for g in range(N_GROUPS):
        row = grp[g:g + 1, :]
        ahead = ahead + jnp.logical_or(row > grp, jnp.logical_and(row == grp, g < gi)).astype(F32)
    ahead3 = jnp.broadcast_to(ahead[:, None, :], (N_GROUPS, gsz, tn))
    selm = jnp.where(ahead3 < float(TOPK_GROUPS), sel3, -jnp.inf).reshape(e, tn)
    ri = lax.broadcasted_iota(jnp.int32, (e, tn), 0).astype(F32)
    member = jnp.zeros((e, tn), F32)
    idxs, ws = [], []
    for _ in range(TOP_K):
        m = jnp.max(selm, axis=0, keepdims=True)
        idx = jnp.min(jnp.where(selm == m, ri, float(e)), axis=0, keepdims=True)
        hit = ri == idx
        ws.append(jnp.sum(jnp.where(hit, scores, 0.0), axis=0, keepdims=True))
        idxs.append(idx)
        selm = jnp.where(hit, -jnp.inf, selm)
        member = jnp.where(hit, 1.0, member)
    w = jnp.concatenate(ws, axis=0)
    wt_ref[...] = w / jnp.sum(w, axis=0, keepdims=True) * ROUTED_SCALE
    idx_ref[...] = jnp.concatenate(idxs, axis=0).astype(jnp.int32)
    cum = _dot(member.astype(BF16), tri_ref[...]) + base_s[...]
    rk_ref[...] = jnp.concatenate(
        [jnp.sum(jnp.where(ri == idx, cum, 0.0), axis=0, keepdims=True) for idx in idxs], axis=0).astype(jnp.int32)
    total = base_s[...] + jnp.sum(member, axis=1, keepdims=True)
    base_s[...] = total
    cnt_ref[...] = total


def _route(scores_t, router_bias):
    e, t = scores_t.shape
    tn = LANES
    bias = jnp.broadcast_to(router_bias.astype(F32)[:, None], (e, tn))
    tri = (jnp.arange(tn)[:, None] < jnp.arange(tn)[None, :]).astype(BF16)
    tok = pl.BlockSpec((TOP_K, tn), lambda i: (0, i))
    const = lambda i: (0, 0)
    return pl.pallas_call(
        _route_kernel,
        grid=(t // tn,),
        in_specs=[pl.BlockSpec((e, tn), lambda i: (0, i)), pl.BlockSpec((e, tn), const),
                  pl.BlockSpec((tn, tn), const)],
        out_specs=[tok, tok, tok, pl.BlockSpec((e, tn), const)],
        out_shape=[jax.ShapeDtypeStruct((TOP_K, t), jnp.int32), jax.ShapeDtypeStruct((TOP_K, t), F32),
                   jax.ShapeDtypeStruct((TOP_K, t), jnp.int32), jax.ShapeDtypeStruct((e, tn), F32)],
        scratch_shapes=[pltpu.VMEM((e, tn), F32)],
        compiler_params=_params(("arbitrary",)),
    )(scores_t, bias, tri)


def _slot_kernel(idx_ref, rk_ref, ps_ref, pos_ref):
    e, tn = ps_ref.shape
    ri = lax.broadcasted_iota(jnp.int32, (e, tn), 0)
    ps = ps_ref[...]
    rows = [jnp.sum(jnp.where(ri == idx_ref[k:k + 1, :], ps, 0.0), axis=0, keepdims=True) for k in range(TOP_K)]
    pos_ref[...] = rk_ref[...] + jnp.concatenate(rows, axis=0).astype(jnp.int32)


def _slots(idx, rank, pstart):
    k, t = idx.shape
    e = pstart.shape[0]
    tn = LANES
    tok = pl.BlockSpec((k, tn), lambda i: (0, i))
    return pl.pallas_call(
        _slot_kernel,
        grid=(t // tn,),
        in_specs=[tok, tok, pl.BlockSpec((e, tn), lambda i: (0, 0))],
        out_specs=tok,
        out_shape=jax.ShapeDtypeStruct((k, t), jnp.int32),
        compiler_params=_params(("arbitrary",)),
    )(idx, rank, jnp.broadcast_to(pstart.astype(F32)[:, None], (e, tn)))


def _block_table(counts, n_blocks):
    padded = (counts + EXPERT_ROWS - 1) // EXPERT_ROWS * EXPERT_ROWS
    pend = jnp.cumsum(padded)
    blk_expert = jnp.minimum(jnp.searchsorted(pend, jnp.arange(n_blocks) * EXPERT_ROWS, side='right'),
                             N_EXPERTS - 1).astype(jnp.int32)
    return pend - padded, blk_expert, (pend[-1] // EXPERT_ROWS).astype(jnp.int32).reshape(1)


def _col_major(t):
    b, l, f = t.shape
    rows = l // GRID_W
    return t.reshape(b, rows, GRID_W, f).transpose(0, 2, 1, 3).reshape(b, l, f)


def _row_major(t):
    b, l, f = t.shape
    rows = l // GRID_W
    return t.reshape(b, GRID_W, rows, f).transpose(0, 2, 1, 3).reshape(b, l, f)


def kernel(x, c, ctx, c_ctx, w_ada, b_ada, norm1_g, norm2_g, w_in, gdn_conv_w, gdn_a_log, gdn_dt_bias, gdn_norm_g,
           ml_i_bias, ml_f_bias, ml_norm_g, w_branch_gdn, w_branch_ml, w_out, w_router, router_bias, w_exp_gate_up,
           w_exp_down, w_sh_gate_up, w_sh_down, final_norm_g):
    b, l, d = x.shape
    lc = ctx.shape[1]
    t = b * l
    layer = 0

    w = w_in[layer]
    main_cols = [_ORIG[k] for k in ("gdn_qkv", "gdn_z", "ml_q", "ml_k", "ml_v", "ml_o", "mg_gdn", "mg_ml")]
    w_main = jnp.concatenate([w[:, a:e] for a, e in main_cols], axis=1).astype(BF16)
    w_gate = jnp.concatenate([w[:, _ORIG["gdn_gate"][0]:_ORIG["gdn_gate"][1]],
                              w[:, _ORIG["ml_gate"][0]:_ORIG["ml_gate"][1]],
                              jnp.zeros((d, LANES - 64), F32)], axis=1).astype(BF16)
    zeros16 = jnp.zeros((16,), F32)
    gp_add = jnp.concatenate([zeros16, gdn_dt_bias[layer].reshape(-1), ml_i_bias[layer].reshape(-1),
                              ml_f_bias[layer].reshape(-1), jnp.zeros((LANES - 64,), F32)])
    gp_mul = jnp.concatenate([zeros16, -jnp.exp(gdn_a_log[layer].astype(F32)).reshape(-1),
                              jnp.zeros((LANES - 32,), F32)])
    gparams = jnp.zeros((8, LANES), F32).at[0].set(gp_add).at[1].set(gp_mul)
    conv_w8 = jnp.zeros((8, gdn_conv_w.shape[2]), F32).at[0:GDN_CONV].set(gdn_conv_w[layer])
    wr = w_router[layer].T
    wr_hi = wr.astype(BF16)
    wr_lo = (wr - wr_hi.astype(F32)).astype(BF16)

    n_mod_rows = -(-(b + 1) // 8) * 8
    cc = jnp.zeros((n_mod_rows, d), F32).at[0:b].set(c).at[b].set(c_ctx)
    mod = _ada_mod(cc, w_ada[layer], b_ada[layer])
    mod3 = mod.reshape(n_mod_rows, 1, 6 * d)

    x2d = x.reshape(t, d)
    tm_l = min(1024, l)
    proj_l, gate_l = _project(x2d, mod3, lambda i: (i * tm_l) // l, norm1_g[layer], w_main, w_gate, tm_l)
    tm_c = min(1024, b * lc)
    proj_c, gate_c = _project(ctx.reshape(b * lc, d), mod3, lambda i: b, norm1_g[layer], w_main, w_gate, tm_c)
    proj_l3 = proj_l.reshape(b, l, N_MAIN)
    proj_c3 = proj_c.reshape(b, lc, N_MAIN)

    gate_l_cm = _col_major(gate_l.reshape(b, l, LANES)).reshape(t, LANES)
    gd_c, ml_c = _gate_prep(gate_c, gparams)
    gd_l, _ = _gate_prep(gate_l, gparams)
    _, ml_l = _gate_prep(gate_l_cm, gparams)

    y_gdn = _gdn(proj_c3, proj_l3, conv_w8, gd_c.reshape(b, lc, LANES), gd_l.reshape(b, l, LANES),
                 gdn_norm_g[layer])
    q_cm = _col_major(proj_l3[:, :, COL_ML_Q:COL_ML_Q + HEADS * ML_DK])
    k_cm = _col_major(proj_l3[:, :, COL_ML_K:COL_ML_K + HEADS * ML_DK])
    v_cm = _col_major(proj_l3[:, :, COL_ML_V:COL_ML_V + HEADS * HEAD_V])
    h_ml = _row_major(_mlstm(proj_c3, q_cm, k_cm, v_cm, ml_c.reshape(b, lc, LANES), ml_l.reshape(b, l, LANES)))

    x1, h2, scores_t = _merge(y_gdn.reshape(t, d), h_ml.reshape(t, d), proj_l, x2d, mod3, l, ml_norm_g[layer],
                            norm2_g[layer], w_branch_gdn[layer].astype(BF16), w_branch_ml[layer].astype(BF16),
                            w_out[layer].astype(BF16), wr_hi, wr_lo, tm=min(512, l))

    tp = t // MOE_PARTS
    n_assign = tp * TOP_K
    n_blocks = (n_assign + N_EXPERTS * (EXPERT_ROWS - 1)) // EXPERT_ROWS + 1
    n_slots = n_blocks * EXPERT_ROWS
    w_sh_gu, w_sh_dn = w_sh_gate_up[layer].astype(BF16), w_sh_down[layer].astype(BF16)
    outs = []
    for part in range(MOE_PARTS):
        tok0 = part * tp
        idx, wts, rank, cnt = _route(scores_t[:, tok0:tok0 + tp], router_bias[layer])
        pstart, blk_expert, n_used = _block_table(cnt[:, 0].astype(jnp.int32), n_blocks)
        pos = _slots(idx, rank, pstart).reshape(n_assign)
        tok_slot = (jnp.arange(n_slots, dtype=jnp.int32) % tp).at[pos].set(
            jnp.arange(n_assign, dtype=jnp.int32) % tp, unique_indices=True, mode="promise_in_bounds")
        xb = h2.at[tok_slot + tok0].get(mode="promise_in_bounds")
        yb = _experts(xb, blk_expert, n_used, w_exp_gate_up[layer], w_exp_down[layer])
        yg = yb.at[pos].get(mode="promise_in_bounds", unique_indices=True).reshape(TOP_K, tp, d)
        outs.append(_final(x1, h2, yg, wts.T, mod3, l, w_sh_gu, w_sh_dn, final_norm_g, tok0, tm=min(256, l)))
    return jnp.concatenate(outs, axis=0).reshape(b, l, d)
```

```python
import functools
import math

import jax
import jax.numpy as jnp
from jax import lax
from jax.experimental import pallas as pl
from jax.experimental.pallas import tpu as pltpu
from jax.experimental.pallas import tpu_sc as plsc

F32 = jnp.float32
BF16 = jnp.bfloat16
HI = lax.Precision.HIGHEST

EPS = 1e-6
CHUNK = 64
GRID_W = 64
HEADS = 8
HEAD_V = 128
GDN_DK = 128
ML_DK = 64
GDN_CONV = 5
N_EXPERTS = 256
TOP_K = 8
N_GROUPS = 8
TOPK_GROUPS = 4
ROUTED_SCALE = 2.5
EXPERT_ROWS = 512
MOE_PARTS = 1
GDN_STEPS = 4
RING = 2 * GDN_STEPS
LANES = 128
VMEM_LIMIT = 56 * 1024 * 1024

COL_GDN_QKV = 0
COL_GDN_Z = 3072
COL_ML_Q = 4096
COL_ML_K = 4608
COL_ML_V = 5120
COL_ML_O = 6144
COL_MG_GDN = 7168
COL_MG_ML = 8192
N_MAIN = 9216
_ORIG = dict(gdn_qkv=(0, 3072), gdn_z=(3072, 4096), gdn_gate=(4096, 4128), ml_q=(4128, 4640),
             ml_k=(4640, 5152), ml_v=(5152, 6176), ml_o=(6176, 7200), ml_gate=(7200, 7232),
             mg_gdn=(7232, 8256), mg_ml=(8256, 9280))


def _params(sem, vmem=VMEM_LIMIT):
    return pltpu.CompilerParams(dimension_semantics=sem, vmem_limit_bytes=vmem)


def _dot(a, b, precision=None):
    return jnp.dot(a, b, preferred_element_type=F32, precision=precision)


def _dot_nt(a, b, precision=None):
    return lax.dot_general(a, b, (((1,), (1,)), ((), ())), preferred_element_type=F32, precision=precision)


def _dot_tn(a, b, precision=None):
    return lax.dot_general(a, b, (((0,), (0,)), ((), ())), preferred_element_type=F32, precision=precision)


def _times3(a, b):
    ah = a.astype(BF16)
    al = (a - ah.astype(F32)).astype(BF16)
    bh = b.astype(BF16)
    bl = (b - bh.astype(F32)).astype(BF16)
    return _dot(ah, bh) + (_dot(ah, bl) + _dot(al, bh))


def _sigmoid(x):
    return 1.0 / (1.0 + jnp.exp(-x))


def _softplus(x):
    return jnp.maximum(x, 0.0) + jnp.log(1.0 + jnp.exp(-jnp.abs(x)))


def _ada_kernel(c_ref, w_ref, b_ref, o_ref):
    c = c_ref[...]
    sc = c * _sigmoid(c)
    o_ref[...] = _dot(sc, w_ref[...], HI) + b_ref[...]


def _ada_mod(cc, w_ada, b_ada, tn=1536):
    r, d = cc.shape
    n = w_ada.shape[1]
    return pl.pallas_call(
        _ada_kernel,
        grid=(n // tn,),
        in_specs=[pl.BlockSpec((r, d), lambda j: (0, 0)),
                  pl.BlockSpec((d, tn), lambda j: (0, j)),
                  pl.BlockSpec((1, tn), lambda j: (0, j))],
        out_specs=pl.BlockSpec((r, tn), lambda j: (0, j)),
        out_shape=jax.ShapeDtypeStruct((r, n), F32),
        compiler_params=_params(("arbitrary",)),
    )(cc, w_ada, b_ada.reshape(1, n))


def _proj_kernel(x_ref, mod_ref, g_ref, w_ref, wg_ref, o_ref, og_ref, hn_ref):
    d = x_ref.shape[1]

    @pl.when(pl.program_id(1) == 0)
    def _():
        x = x_ref[...]
        y = x * lax.rsqrt(jnp.mean(x * x, axis=-1, keepdims=True) + EPS) * g_ref[...]
        shift = mod_ref[0, :, 0:d]
        scale = mod_ref[0, :, d:2 * d]
        h = (y * (1.0 + scale) + shift).astype(BF16)
        hn_ref[...] = h
        og_ref[...] = _dot(h, wg_ref[...])

    o_ref[...] = _dot(hn_ref[...], w_ref[...]).astype(o_ref.dtype)


def _project(x2d, mod3, mod_row_of_tile, norm_g, w_main, w_gate, tm, tn=1024):
    t, d = x2d.shape
    n = w_main.shape[1]
    return pl.pallas_call(
        _proj_kernel,
        grid=(t // tm, n // tn),
        in_specs=[pl.BlockSpec((tm, d), lambda i, j: (i, 0)),
                  pl.BlockSpec((1, 1, mod3.shape[2]), lambda i, j: (mod_row_of_tile(i), 0, 0)),
                  pl.BlockSpec((1, d), lambda i, j: (0, 0)),
                  pl.BlockSpec((d, tn), lambda i, j: (0, j)),
                  pl.BlockSpec((d, LANES), lambda i, j: (0, 0))],
        out_specs=[pl.BlockSpec((tm, tn), lambda i, j: (i, j)),
                   pl.BlockSpec((tm, LANES), lambda i, j: (i, 0))],
        out_shape=[jax.ShapeDtypeStruct((t, n), BF16), jax.ShapeDtypeStruct((t, LANES), F32)],
        scratch_shapes=[pltpu.VMEM((tm, d), BF16)],
        compiler_params=_params(("arbitrary", "arbitrary")),
    )(x2d, mod3, norm_g.reshape(1, d), w_main, w_gate)


def _gate_kernel(g_ref, p_ref, gd_ref, ml_ref):
    rows = g_ref.shape[0]
    raw = g_ref[...] + p_ref[0:1, :]
    lane = lax.broadcasted_iota(jnp.int32, raw.shape, 1)
    sp = _softplus(raw)
    vals = jnp.where(lane < 16, _sigmoid(raw),
                     jnp.where(lane < 32, p_ref[1:2, :] * sp,
                               jnp.where(lane < 48, raw,
                                         jnp.where(lane < 64, -_softplus(-raw), 0.0))))
    ri = lax.broadcasted_iota(jnp.int32, (CHUNK, CHUNK), 0)
    ci = lax.broadcasted_iota(jnp.int32, (CHUNK, CHUNK), 1)
    tri_f = (ri >= ci).astype(F32)
    tri_b = (ri <= ci).astype(F32)
    lane_c = lax.broadcasted_iota(jnp.int32, (CHUNK, LANES), 1)
    row_c = lax.broadcasted_iota(jnp.int32, (CHUNK, LANES), 0)
    fwd_lane = (lane_c % 16) < 8
    for c in range(rows // CHUNK):
        blk = vals[c * CHUNK:(c + 1) * CHUNK, :]
        cum = jnp.where(fwd_lane, _dot(tri_f, blk, HI), _dot(tri_b, blk, HI))
        gd_ref[c * CHUNK:(c + 1) * CHUNK, :] = jnp.where(lane_c < 16, blk, jnp.where(lane_c < 32, cum, 0.0))
        bcum = pltpu.roll(cum, LANES - 16, axis=1)
        gmb = blk - bcum
        cmf, cmb = gmb, gmb
        for s in (1, 2, 4, 8, 16, 32):
            cmf = jnp.maximum(cmf, jnp.where(row_c >= s, pltpu.roll(cmf, s, axis=0), -jnp.inf))
            cmb = jnp.maximum(cmb, jnp.where(row_c < CHUNK - s, pltpu.roll(cmb, CHUNK - s, axis=0), -jnp.inf))
        cm = jnp.where(fwd_lane, cmf, cmb)
        ml = jnp.where(lane_c < 16, pltpu.roll(gmb, LANES - 32, axis=1),
                       jnp.where(lane_c < 32, pltpu.roll(cm, LANES - 16, axis=1),
                                 jnp.where(lane_c < 48, bcum, 0.0)))
        ml_ref[c * CHUNK:(c + 1) * CHUNK, :] = ml


def _gate_prep(graw, gparams, tm=256):
    t = graw.shape[0]
    spec = pl.BlockSpec((tm, LANES), lambda i: (i, 0))
    return pl.pallas_call(
        _gate_kernel,
        grid=(t // tm,),
        in_specs=[spec, pl.BlockSpec((8, LANES), lambda i: (0, 0))],
        out_specs=[spec, spec],
        out_shape=[jax.ShapeDtypeStruct((t, LANES), F32)] * 2,
        compiler_params=_params(("arbitrary",)),
    )(graw, gparams)


def _split3(a):
    h = a.astype(BF16)
    r = a - h.astype(F32)
    m = r.astype(BF16)
    return h, m, (r - m.astype(F32)).astype(BF16)


def _lane_picks(x, lanes):
    li = lax.broadcasted_iota(jnp.int32, (LANES, LANES), 0)
    ci = lax.broadcasted_iota(jnp.int32, (LANES, LANES), 1)
    want = jnp.full((LANES, LANES), -1, jnp.int32)
    for j, lane in enumerate(lanes):
        want = jnp.where(ci == j, lane, want)
    sel = (li == want).astype(BF16)
    h, m, lo = _split3(x)
    cols = _dot(h, sel) + (_dot(m, sel) + _dot(lo, sel))
    return [jnp.broadcast_to(cols[:, j:j + 1], x.shape) for j in range(len(lanes))]


def _dir_masks(direction):
    ri = lax.broadcasted_iota(jnp.int32, (CHUNK, CHUNK), 0)
    ci = lax.broadcasted_iota(jnp.int32, (CHUNK, CHUNK), 1)
    if direction == 0:
        return ri >= ci, ri > ci
    return ri <= ci, ri < ci


def _gdn_kernel(qc_ref, kc_ref, vc_ref, ql_ref, kl_ref, vl_ref, z_ref, cwq_ref, cwk_ref, cwv_ref,
                gdc_ref, gdl_ref, ng_ref, y_ref,
                xpad, qs, ks, vs, beta_t, cg_t, wq_r, u_r, kd_r, qk_r, dc_r, out_s):
    lc = qc_ref.shape[1]
    ll = ql_ref.shape[1]
    lt = lc + ll
    n_c, n_l = lc // CHUNK, ll // CHUNK
    n_t = n_c + n_l
    rb = 256

    def l2n(x):
        return x * lax.rsqrt(jnp.sum(x * x, axis=-1, keepdims=True) + EPS)

    def prep(src_ref, cw_ref, dst, off, ls, kind):
        xpad[0:8, :] = jnp.zeros((8, LANES), F32)
        xpad[8:8 + ls, :] = src_ref[0].astype(F32)
        xpad[8 + ls:16 + ls, :] = jnp.zeros((8, LANES), F32)
        step = min(rb, ls)
        for r0 in range(0, ls, step):
            acc = jnp.zeros((step, LANES), F32)
            for t in range(GDN_CONV):
                s0 = r0 + 8 - GDN_CONV // 2 + t
                acc = acc + cw_ref[t:t + 1, :] * xpad[s0:s0 + step, :]
            y = acc * _sigmoid(acc)
            if kind == "q":
                y = l2n(y) * (GDN_DK ** -0.5)
            elif kind == "k":
                y = l2n(y)
            dst[off + r0:off + r0 + step, :] = y

    prep(qc_ref, cwq_ref, qs, 0, lc, "q")
    prep(kc_ref, cwk_ref, ks, 0, lc, "k")
    prep(vc_ref, cwv_ref, vs, 0, lc, "v")
    prep(ql_ref, cwq_ref, qs, lc, ll, "q")
    prep(kl_ref, cwk_ref, ks, lc, ll, "k")
    prep(vl_ref, cwv_ref, vs, lc, ll, "v")

    eye = (lax.broadcasted_iota(jnp.int32, (CHUNK, CHUNK), 0)
           == lax.broadcasted_iota(jnp.int32, (CHUNK, CHUNK), 1)).astype(F32)

    masks = (_dir_masks(0), _dir_masks(1))
    head = pl.program_id(1)

    def build_tables(src_ref, off, ls):
        step = min(rb, ls)
        for r0 in range(0, ls, step):
            picked = _lane_picks(src_ref[0, r0:r0 + step, :], [8 * d + head for d in range(2)]
                                 + [16 + 8 * d + head for d in range(2)])
            for d in range(2):
                beta_t[d, off + r0:off + r0 + step, :] = picked[d]
                cg_t[d, off + r0:off + r0 + step, :] = picked[2 + d]

    build_tables(gdc_ref, 0, lc)
    build_tables(gdl_ref, lc, ll)

    def bwd_chunk(t):
        return jnp.where(t < n_c, n_c - 1 - t, n_t + n_c - 1 - t)

    def prep_chain(t, d):
        incl, strict = masks[d]
        last = CHUNK - 1 if d == 0 else 0
        tc = jnp.minimum(t, n_t - 1)
        c = tc if d == 0 else bwd_chunk(tc)
        r0 = pl.multiple_of(c * CHUNK, CHUNK)
        q = qs[pl.ds(r0, CHUNK), :]
        k = ks[pl.ds(r0, CHUNK), :]
        v = vs[pl.ds(r0, CHUNK), :]
        beta = beta_t[d, pl.ds(r0, CHUNK), :]
        cgc = cg_t[d, pl.ds(r0, CHUNK), :]
        cgr = jnp.transpose(cgc)[0:CHUNK, :]
        kk = _dot_nt(k, k)
        qk = _dot_nt(q, k)
        slot = (t % RING) * 2 + d
        yield
        cg_last = cgc[last:last + 1, :]
        decay = jnp.exp(jnp.where(incl, cgc[:, 0:CHUNK] - cgr, -jnp.inf))
        ecg = jnp.exp(cgc)
        kb = k * beta
        qk_r[slot] = qk * decay
        wq_r[slot, CHUNK:2 * CHUNK, :] = q * ecg
        kd_r[slot] = k * jnp.exp(cg_last - cgc)
        dc_r[slot] = jnp.broadcast_to(jnp.exp(cg_last), (8, LANES))
        x = jnp.where(strict, -(beta[:, 0:CHUNK] * kk) * decay, 0.0)
        tinv = eye + x
        x = _times3(x, x)
        yield
        for _ in range(4):
            both = _times3(jnp.concatenate([tinv, x], axis=0), x)
            tinv, x = tinv + both[0:CHUNK, :], both[CHUNK:2 * CHUNK, :]
            yield
        tinv = tinv + _times3(tinv, x)
        yield
        wq_r[slot, 0:CHUNK, :] = _dot(tinv, kb * ecg)
        u_r[slot] = _dot(tinv, v * beta)

    out_s[...] = jnp.zeros(out_s.shape, F32)

    def scan_chain(d, t0, s, with_out, result):
        for j in range(GDN_STEPS):
            t = t0 + j
            slot = (t % RING) * 2 + d
            ws = _dot(wq_r[slot], s)
            yield
            v_new = u_r[slot] - ws[0:CHUNK, :]
            if with_out:
                c = t if d == 0 else bwd_chunk(t)
                o = ws[CHUNK:2 * CHUNK, :] + _dot(qk_r[slot], v_new)
                l0 = pl.multiple_of((c - n_c) * CHUNK, CHUNK)
                out_s[pl.ds(l0, CHUNK), :] += o
            s = s * dc_r[slot][0:1, :] + _dot_tn(kd_r[slot], v_new)
            yield
        result[d] = s

    def lockstep(chains):
        chains = list(chains)
        while chains:
            alive = []
            for ch in chains:
                try:
                    next(ch)
                    alive.append(ch)
                except StopIteration:
                    pass
            chains = alive

    def pair_body(i, carry, t_base, with_out):
        t0 = t_base + GDN_STEPS * i
        result = [None, None]
        lockstep([scan_chain(0, t0, carry[0], with_out, result), scan_chain(1, t0, carry[1], with_out, result)]
                 + [prep_chain(t0 + GDN_STEPS + j, d) for j in range(GDN_STEPS) for d in range(2)])
        return result[0], result[1]

    lockstep([prep_chain(j, d) for j in range(GDN_STEPS) for d in range(2)])
    s0 = jnp.zeros((GDN_DK, HEAD_V), F32)
    carry = lax.fori_loop(0, n_c // GDN_STEPS, functools.partial(pair_body, t_base=0, with_out=False), (s0, s0))
    lax.fori_loop(0, n_l // GDN_STEPS, functools.partial(pair_body, t_base=n_c, with_out=True), carry)


    def out_body(i, carry):
        r0 = pl.multiple_of(i * rb, rb)
        o = out_s[pl.ds(r0, rb), :]
        z = z_ref[0, pl.ds(r0, rb), :].astype(F32)
        y = o * lax.rsqrt(jnp.mean(o * o, axis=-1, keepdims=True) + EPS) * ng_ref[...]
        y_ref[0, pl.ds(r0, rb), :] = (y * (z * _sigmoid(z))).astype(y_ref.dtype)
        return carry

    lax.fori_loop(0, ll // rb, out_body, 0)


def _gdn(proj_c, proj_l, conv_w8, gd_c, gd_l, norm_g):
    b, lc, _ = proj_c.shape
    ll = proj_l.shape[1]
    lt = lc + ll
    n_t = lt // CHUNK
    qb, kb_, vb, zb = (COL_GDN_QKV // LANES, COL_GDN_QKV // LANES + HEADS, COL_GDN_QKV // LANES + 2 * HEADS,
                       COL_GDN_Z // LANES)

    def seq_spec(l, col0):
        return pl.BlockSpec((1, l, LANES), lambda i, h: (i, 0, col0 + h))

    def cw_spec(col0):
        return pl.BlockSpec((8, LANES), lambda i, h: (0, col0 + h))

    return pl.pallas_call(
        _gdn_kernel,
        grid=(b, HEADS),
        in_specs=[seq_spec(lc, qb), seq_spec(lc, kb_), seq_spec(lc, vb),
                  seq_spec(ll, qb), seq_spec(ll, kb_), seq_spec(ll, vb), seq_spec(ll, zb),
                  cw_spec(0), cw_spec(HEADS), cw_spec(2 * HEADS),
                  pl.BlockSpec((1, lc, LANES), lambda i, h: (i, 0, 0)),
                  pl.BlockSpec((1, ll, LANES), lambda i, h: (i, 0, 0)),
                  pl.BlockSpec((1, LANES), lambda i, h: (0, 0))],
        out_specs=pl.BlockSpec((1, ll, LANES), lambda i, h: (i, 0, h)),
        out_shape=jax.ShapeDtypeStruct((b, ll, HEADS * HEAD_V), BF16),
        scratch_shapes=[pltpu.VMEM((max(lc, ll) + 16, LANES), F32),
                        pltpu.VMEM((lt, LANES), F32), pltpu.VMEM((lt, LANES), F32), pltpu.VMEM((lt, LANES), F32),
                        pltpu.VMEM((2, lt, LANES), F32), pltpu.VMEM((2, lt, LANES), F32),
                        pltpu.VMEM((2 * RING, 2 * CHUNK, LANES), F32),
                        pltpu.VMEM((2 * RING, CHUNK, LANES), F32), pltpu.VMEM((2 * RING, CHUNK, LANES), F32),
                        pltpu.VMEM((2 * RING, CHUNK, CHUNK), F32),
                        pltpu.VMEM((2 * RING, 8, LANES), F32),
                        pltpu.VMEM((ll, LANES), F32)],
        compiler_params=_params(("arbitrary", "arbitrary")),
    )(proj_c, proj_c, proj_c, proj_l, proj_l, proj_l, proj_l, conv_w8, conv_w8, conv_w8,
      gd_c, gd_l, norm_g.reshape(1, LANES))


def _mlstm_kernel(qc_ref, kc_ref, vc_ref, ql_ref, kl_ref, vl_ref, mlc_ref, mll_ref, h_ref, out_s, tabs):
    lc = qc_ref.shape[1]
    ll = ql_ref.shape[1]
    n_c, n_l = lc // CHUNK, ll // CHUNK
    n_t = n_c + n_l
    lt = lc + ll
    pair = pl.program_id(1)
    lane = lax.broadcasted_iota(jnp.int32, (CHUNK, LANES), 1)
    ones_v = jnp.ones((CHUNK, HEAD_V), BF16)
    chains = [(hh, d) for hh in range(2) for d in range(2)]
    hmask = [((lane // ML_DK) == hh).astype(F32) for hh in range(2)]
    incl = [_dir_masks(d)[0] for d in range(2)]

    def build_tables(src_ref, off, ls):
        step = min(256, ls)
        for r0 in range(0, ls, step):
            lanes = [16 * j + 8 * d + 2 * pair + hh for hh, d in chains for j in range(3)]
            for g, tab in enumerate(_lane_picks(src_ref[0, r0:r0 + step, :], lanes)):
                tabs[g, off + r0:off + r0 + step, :] = tab

    build_tables(mlc_ref, 0, lc)
    build_tables(mll_ref, lc, ll)

    def wide(a):
        return jnp.concatenate([a, a], axis=1)

    def chain(hh, d, c, state, is_ctx, result):
        cs, ms = state
        last = CHUNK - 1 if d == 0 else 0
        if is_ctx:
            rows = pl.ds(pl.multiple_of(c * CHUNK, CHUNK), CHUNK)
            q_ref, k_ref, v_ref = qc_ref, kc_ref, vc_ref
        else:
            rows = pl.ds(pl.multiple_of((c - n_c) * CHUNK, CHUNK), CHUNK)
            q_ref, k_ref, v_ref = ql_ref, kl_ref, vl_ref
        q = (q_ref[0, rows, :].astype(F32) * hmask[hh]).astype(BF16)
        k = k_ref[0, rows, :].astype(F32) * (hmask[hh] * (ML_DK ** -0.5))
        v = jnp.concatenate([v_ref[0, rows, hh * HEAD_V:(hh + 1) * HEAD_V], ones_v], axis=1)
        n = chains.index((hh, d))
        trows = pl.ds(pl.multiple_of(c * CHUNK, CHUNK), CHUNK)
        gmb = tabs[3 * n, trows, :]
        gmb_t = jnp.transpose(gmb)[0:CHUNK, :]
        cm = tabs[3 * n + 1, trows, :]
        bc = tabs[3 * n + 2, trows, :]
        qk = _dot_nt(q, k.astype(BF16))
        yield
        cm_last = cm[last:last + 1, :]
        b_last = bc[last:last + 1, :]
        mm = jnp.maximum(ms, cm)
        p = jnp.where(incl[d], jnp.exp(gmb_t - mm[:, 0:CHUNK]), 0.0) * qk
        wk = (k * jnp.exp(gmb - cm_last)).astype(BF16)
        inter = _dot(q, cs.astype(BF16))
        intra = _dot(p.astype(BF16), v)
        c_loc = _dot_tn(wk, v)
        yield
        if not is_ctx:
            nd = wide(jnp.exp(ms - mm)) * inter + intra
            hv = nd[:, 0:HEAD_V] / jnp.maximum(jnp.abs(nd[:, HEAD_V:2 * HEAD_V]), jnp.exp(-(bc + mm)))
            l0 = pl.multiple_of((c - n_c) * CHUNK, CHUNK)
            out_s[pl.ds(l0, CHUNK), hh * HEAD_V:(hh + 1) * HEAD_V] += hv
        mx = jnp.maximum(ms, cm_last)
        result[hh, d] = (wide(jnp.exp(ms - mx)) * cs + wide(jnp.exp(cm_last - mx)) * c_loc, b_last + mx)

    out_s[...] = jnp.zeros(out_s.shape, F32)

    def run(chains_iter):
        live = list(chains_iter)
        while live:
            alive = []
            for ch in live:
                try:
                    next(ch)
                    alive.append(ch)
                except StopIteration:
                    pass
            live = alive

    def body(i, carry, is_ctx):
        result = {}
        gens = []
        for n, (hh, d) in enumerate(chains):
            if is_ctx:
                c = i if d == 0 else n_c - 1 - i
            else:
                c = n_c + i if d == 0 else n_t - 1 - i
            gens.append(chain(hh, d, c, carry[n], is_ctx, result))
        run(gens)
        return tuple(result[hd] for hd in chains)

    st0 = (jnp.zeros((LANES, 2 * HEAD_V), F32), jnp.zeros((1, LANES), F32))
    carry = lax.fori_loop(0, n_c, functools.partial(body, is_ctx=True), (st0,) * 4)
    lax.fori_loop(0, n_l, functools.partial(body, is_ctx=False), carry)
    h_ref[0] = out_s[...].astype(h_ref.dtype)


def _mlstm(proj_c, q_l, k_l, v_l, ml_c, ml_l):
    b, lc, _ = proj_c.shape
    ll = q_l.shape[1]
    lt = lc + ll
    qb, kb_, vb = COL_ML_Q // LANES, COL_ML_K // LANES, COL_ML_V // (2 * HEAD_V)
    return pl.pallas_call(
        _mlstm_kernel,
        grid=(b, HEADS // 2),
        in_specs=[pl.BlockSpec((1, lc, LANES), lambda i, p: (i, 0, qb + p)),
                  pl.BlockSpec((1, lc, LANES), lambda i, p: (i, 0, kb_ + p)),
                  pl.BlockSpec((1, lc, 2 * HEAD_V), lambda i, p: (i, 0, vb + p)),
                  pl.BlockSpec((1, ll, LANES), lambda i, p: (i, 0, p)),
                  pl.BlockSpec((1, ll, LANES), lambda i, p: (i, 0, p)),
                  pl.BlockSpec((1, ll, 2 * HEAD_V), lambda i, p: (i, 0, p)),
                  pl.BlockSpec((1, lc, LANES), lambda i, p: (i, 0, 0)),
                  pl.BlockSpec((1, ll, LANES), lambda i, p: (i, 0, 0))],
        out_specs=pl.BlockSpec((1, ll, 2 * HEAD_V), lambda i, p: (i, 0, p)),
        out_shape=jax.ShapeDtypeStruct((b, ll, HEADS * HEAD_V), BF16),
        scratch_shapes=[pltpu.VMEM((ll, 2 * HEAD_V), F32), pltpu.VMEM((12, lt, LANES), F32)],
        compiler_params=_params(("arbitrary", "arbitrary")),
    )(proj_c, proj_c, proj_c, q_l, k_l, v_l, ml_c, ml_l)


def _merge_kernel(yg_ref, hm_ref, o_ref, gg_ref, gm_ref, x_ref, mod_ref, mlg_ref, n2_ref,
                  wbg_ref, wbm_ref, wo_ref, wrh_ref, wrl_ref, x1_ref, h2_ref, sc_ref):
    d = x_ref.shape[1]
    o = o_ref[...].astype(F32)
    ym = _sigmoid(o) * hm_ref[...].astype(F32)
    segs = []
    for h in range(HEADS):
        seg = ym[:, h * HEAD_V:(h + 1) * HEAD_V]
        segs.append(seg * lax.rsqrt(jnp.mean(seg * seg, axis=-1, keepdims=True) + EPS))
    ymn = jnp.concatenate(segs, axis=1) * mlg_ref[...]
    y_gdn = _dot(yg_ref[...], wbg_ref[...])
    y_ml = _dot(ymn.astype(BF16), wbm_ref[...])
    mixed = _sigmoid(gg_ref[...].astype(F32)) * y_gdn + _sigmoid(gm_ref[...].astype(F32)) * y_ml
    y = _dot(mixed.astype(BF16), wo_ref[...])
    x1 = x_ref[...] + mod_ref[0, :, 2 * d:3 * d] * y
    x1_ref[...] = x1
    hn = x1 * lax.rsqrt(jnp.mean(x1 * x1, axis=-1, keepdims=True) + EPS) * n2_ref[...]
    h2 = hn * (1.0 + mod_ref[0, :, 4 * d:5 * d]) + mod_ref[0, :, 3 * d:4 * d]
    h2_hi = h2.astype(BF16)
    h2_ref[...] = h2_hi
    h2_lo = (h2 - h2_hi.astype(F32)).astype(BF16)
    logits = _dot_nt(wrh_ref[...], h2_hi) + (_dot_nt(wrl_ref[...], h2_hi) + _dot_nt(wrh_ref[...], h2_lo))
    sc_ref[...] = _sigmoid(logits)


def _merge(y_gdn, h_ml, proj_l2d, x2d, mod3, rows_per_mod, ml_norm_g, norm2_g, wbg, wbm, wo, wr_hi, wr_lo, tm=512):
    t, d = x2d.shape
    e = wr_hi.shape[0]
    row = lambda i: (i, 0)
    const = lambda i: (0, 0)
    return pl.pallas_call(
        _merge_kernel,
        grid=(t // tm,),
        in_specs=[pl.BlockSpec((tm, d), row), pl.BlockSpec((tm, d), row),
                  pl.BlockSpec((tm, d), lambda i: (i, COL_ML_O // d)),
                  pl.BlockSpec((tm, d), lambda i: (i, COL_MG_GDN // d)),
                  pl.BlockSpec((tm, d), lambda i: (i, COL_MG_ML // d)),
                  pl.BlockSpec((tm, d), row),
                  pl.BlockSpec((1, 1, mod3.shape[2]), lambda i: ((i * tm) // rows_per_mod, 0, 0)),
                  pl.BlockSpec((1, d), const), pl.BlockSpec((1, d), const),
                  pl.BlockSpec((d, d), const), pl.BlockSpec((d, d), const), pl.BlockSpec((d, d), const),
                  pl.BlockSpec((e, d), const), pl.BlockSpec((e, d), const)],
        out_specs=[pl.BlockSpec((tm, d), row), pl.BlockSpec((tm, d), row), pl.BlockSpec((e, tm), lambda i: (0, i))],
        out_shape=[jax.ShapeDtypeStruct((t, d), F32), jax.ShapeDtypeStruct((t, d), BF16),
                   jax.ShapeDtypeStruct((e, t), F32)],
        compiler_params=_params(("arbitrary",)),
    )(y_gdn, h_ml, proj_l2d, proj_l2d, proj_l2d, x2d, mod3, ml_norm_g.reshape(1, d), norm2_g.reshape(1, d),
      wbg, wbm, wo, wr_hi, wr_lo)


def _expert_kernel(be_ref, nu_ref, x_ref, wgu_ref, wd_ref, y_ref, wgu_s, wd_s):
    i = pl.program_id(0)
    de = wd_ref.shape[1]
    changed = jnp.logical_or(i == 0, be_ref[i] != be_ref[jnp.maximum(i - 1, 0)])

    @pl.when(changed)
    def _():
        wgu_s[...] = wgu_ref[0].astype(BF16)
        wd_s[...] = wd_ref[0].astype(BF16)

    @pl.when(i < nu_ref[0])
    def _():
        gu = _dot(x_ref[...], wgu_s[...])
        g = gu[:, 0:de]
        act = (g * _sigmoid(g)) * gu[:, de:2 * de]
        y_ref[...] = _dot(act.astype(BF16), wd_s[...]).astype(y_ref.dtype)

    @pl.when(i >= nu_ref[0])
    def _():
        y_ref[...] = jnp.zeros(y_ref.shape, y_ref.dtype)


def _experts(xb, blk_expert, n_used, w_gu, w_down):
    n_slots, d = xb.shape
    n_blocks = n_slots // EXPERT_ROWS
    e, _, de2 = w_gu.shape
    de = de2 // 2
    return pl.pallas_call(
        _expert_kernel,
        grid_spec=pltpu.PrefetchScalarGridSpec(
            num_scalar_prefetch=2,
            grid=(n_blocks,),
            in_specs=[pl.BlockSpec((EXPERT_ROWS, d), lambda i, be, nu: (i, 0)),
                      pl.BlockSpec((1, d, de2), lambda i, be, nu: (be[i], 0, 0)),
                      pl.BlockSpec((1, de, d), lambda i, be, nu: (be[i], 0, 0))],
            out_specs=pl.BlockSpec((EXPERT_ROWS, d), lambda i, be, nu: (i, 0)),
            scratch_shapes=[pltpu.VMEM((d, de2), BF16), pltpu.VMEM((de, d), BF16)]),
        out_shape=jax.ShapeDtypeStruct((n_slots, d), BF16),
        compiler_params=_params(("arbitrary",)),
    )(blk_expert, n_used, xb, w_gu, w_down)


def _final_kernel(x1_ref, h2_ref, yg_ref, wt_ref, mod_ref, wsg_ref, wsd_ref, fg_ref, o_ref):
    d = x1_ref.shape[1]
    ds_ = wsd_ref.shape[0]
    wt = wt_ref[...]
    routed = jnp.zeros(x1_ref.shape, F32)
    for k in range(TOP_K):
        routed = routed + wt[:, k:k + 1] * yg_ref[k].astype(F32)
    gu = _dot(h2_ref[...], wsg_ref[...])
    g = gu[:, 0:ds_]
    sh = _dot(((g * _sigmoid(g)) * gu[:, ds_:2 * ds_]).astype(BF16), wsd_ref[...])
    x2 = x1_ref[...] + mod_ref[0, :, 5 * d:6 * d] * (routed + sh)
    o_ref[...] = x2 * lax.rsqrt(jnp.mean(x2 * x2, axis=-1, keepdims=True) + EPS) * fg_ref[...]


def _final(x1, h2, yg, wts, mod3, rows_per_mod, w_sh_gu, w_sh_down, final_g, tok0, tm=256):
    tp = yg.shape[1]
    d = x1.shape[1]
    off = tok0 // tm
    row = lambda i: (i, 0)
    row_off = lambda i: (i + off, 0)
    const = lambda i: (0, 0)
    return pl.pallas_call(
        _final_kernel,
        grid=(tp // tm,),
        in_specs=[pl.BlockSpec((tm, d), row_off), pl.BlockSpec((tm, d), row_off),
                  pl.BlockSpec((TOP_K, tm, d), lambda i: (0, i, 0)), pl.BlockSpec((tm, TOP_K), row),
                  pl.BlockSpec((1, 1, mod3.shape[2]), lambda i: (((i + off) * tm) // rows_per_mod, 0, 0)),
                  pl.BlockSpec(w_sh_gu.shape, const), pl.BlockSpec(w_sh_down.shape, const),
                  pl.BlockSpec((1, d), const)],
        out_specs=pl.BlockSpec((tm, d), row),
        out_shape=jax.ShapeDtypeStruct((tp, d), F32),
        compiler_params=_params(("arbitrary",)),
    )(x1, h2, yg, wts, mod3, w_sh_gu, w_sh_down, final_g.reshape(1, d))


def _route_kernel(sc_ref, bias_ref, tri_ref, idx_ref, wt_ref, rk_ref, cnt_ref, base_s):
    @pl.when(pl.program_id(0) == 0)
    def _():
        base_s[...] = jnp.zeros(base_s.shape, F32)

    scores = sc_ref[...]
    e, tn = scores.shape
    gsz = e // N_GROUPS
    sel3 = (scores + bias_ref[...]).reshape(N_GROUPS, gsz, tn)
    m1 = jnp.max(sel3, axis=1)
    is_max = sel3 == m1[:, None, :]
    n_max = jnp.sum(is_max.astype(F32), axis=1)
    m2 = jnp.max(jnp.where(is_max, -jnp.inf, sel3), axis=1)
    grp = m1 + jnp.where(n_max >= 2.0, m1, m2)
    gi = lax.broadcasted_iota(jnp.int32, (N_GROUPS, tn), 0)
    ahead = jnp.zeros((N_GROUPS, tn), F32)
    for g in range(N_GROUPS):
        row = grp[g:g + 1, :]
        ahead = ahead + jnp.logical_or(row > grp, jnp.logical_and(row == grp, g < gi)).astype(F32)
    ahead3 = jnp.broadcast_to(ahead[:, None, :], (N_GROUPS, gsz, tn))
    selm = jnp.where(ahead3 < float(TOPK_GROUPS), sel3, -jnp.inf).reshape(e, tn)
    ri = lax.broadcasted_iota(jnp.int32, (e, tn), 0).astype(F32)
    member = jnp.zeros((e, tn), F32)
    idxs, ws = [], []
    for _ in range(TOP_K):
        m = jnp.max(selm, axis=0, keepdims=True)
        idx = jnp.min(jnp.where(selm == m, ri, float(e)), axis=0, keepdims=True)
        hit = ri == idx
        ws.append(jnp.sum(jnp.where(hit, scores, 0.0), axis=0, keepdims=True))
        idxs.append(idx)
        selm = jnp.where(hit, -jnp.inf, selm)
        member = jnp.where(hit, 1.0, member)
    w = jnp.concatenate(ws, axis=0)
    wt_ref[...] = w / jnp.sum(w, axis=0, keepdims=True) * ROUTED_SCALE
    idx_ref[...] = jnp.concatenate(idxs, axis=0).astype(jnp.int32)
    cum = _dot(member.astype(BF16), tri_ref[...]) + base_s[...]
    rk_ref[...] = jnp.concatenate(
        [jnp.sum(jnp.where(ri == idx, cum, 0.0), axis=0, keepdims=True) for idx in idxs], axis=0).astype(jnp.int32)
    total = base_s[...] + jnp.sum(member, axis=1, keepdims=True)
    base_s[...] = total
    cnt_ref[...] = total


def _route(scores_t, router_bias):
    e, t = scores_t.shape
    tn = LANES
    bias = jnp.broadcast_to(router_bias.astype(F32)[:, None], (e, tn))
    tri = (jnp.arange(tn)[:, None] < jnp.arange(tn)[None, :]).astype(BF16)
    tok = pl.BlockSpec((TOP_K, tn), lambda i: (0, i))
    const = lambda i: (0, 0)
    return pl.pallas_call(
        _route_kernel,
        grid=(t // tn,),
        in_specs=[pl.BlockSpec((e, tn), lambda i: (0, i)), pl.BlockSpec((e, tn), const),
                  pl.BlockSpec((tn, tn), const)],
        out_specs=[tok, tok, tok, pl.BlockSpec((e, tn), const)],
        out_shape=[jax.ShapeDtypeStruct((TOP_K, t), jnp.int32), jax.ShapeDtypeStruct((TOP_K, t), F32),
                   jax.ShapeDtypeStruct((TOP_K, t), jnp.int32), jax.ShapeDtypeStruct((e, tn), F32)],
        scratch_shapes=[pltpu.VMEM((e, tn), F32)],
        compiler_params=_params(("arbitrary",)),
    )(scores_t, bias, tri)


def _slot_kernel(idx_ref, rk_ref, ps_ref, pos_ref):
    e, tn = ps_ref.shape
    ri = lax.broadcasted_iota(jnp.int32, (e, tn), 0)
    ps = ps_ref[...]
    rows = [jnp.sum(jnp.where(ri == idx_ref[k:k + 1, :], ps, 0.0), axis=0, keepdims=True) for k in range(TOP_K)]
    pos_ref[...] = rk_ref[...] + jnp.concatenate(rows, axis=0).astype(jnp.int32)


def _slots(idx, rank, pstart):
    k, t = idx.shape
    e = pstart.shape[0]
    tn = LANES
    tok = pl.BlockSpec((k, tn), lambda i: (0, i))
    return pl.pallas_call(
        _slot_kernel,
        grid=(t // tn,),
        in_specs=[tok, tok, pl.BlockSpec((e, tn), lambda i: (0, 0))],
        out_specs=tok,
        out_shape=jax.ShapeDtypeStruct((k, t), jnp.int32),
        compiler_params=_params(("arbitrary",)),
    )(idx, rank, jnp.broadcast_to(pstart.astype(F32)[:, None], (e, tn)))


def _sc_scatter_rows(vals, idx, n_rows, window=LANES):
    n, width = vals.shape
    mesh = plsc.VectorSubcoreMesh(core_axis_name="core", subcore_axis_name="subcore")

    @pl.kernel(out_type=jax.ShapeDtypeStruct((n_rows, width), vals.dtype), mesh=mesh, scratch_types=[])
    def scatter(v_hbm, i_hbm, o_hbm):
        def body(v_vmem, i_vmem):
            pltpu.sync_copy(v_vmem, o_hbm.at[i_vmem.at[0]])

        pltpu.emit_pipeline(
            body,
            grid=(n // window,),
            in_specs=[pl.BlockSpec((window, width), lambda i: (i, 0)),
                      pl.BlockSpec((1, window), lambda i: (0, i))],
            out_specs=[],
            core_axis_name=("core", "subcore"),
            dimension_semantics=(pltpu.PARALLEL,),
        )(v_hbm, i_hbm)

    return scatter(vals, idx.reshape(1, n))


def _block_table(counts, n_blocks):
    padded = (counts + EXPERT_ROWS - 1) // EXPERT_ROWS * EXPERT_ROWS
    pend = jnp.cumsum(padded)
    blk_expert = jnp.minimum(jnp.searchsorted(pend, jnp.arange(n_blocks) * EXPERT_ROWS, side='right'),
                             N_EXPERTS - 1).astype(jnp.int32)
    return pend - padded, blk_expert, (pend[-1] // EXPERT_ROWS).astype(jnp.int32).reshape(1)


def _col_major(t):
    b, l, f = t.shape
    rows = l // GRID_W
    return t.reshape(b, rows, GRID_W, f).transpose(0, 2, 1, 3).reshape(b, l, f)


def _row_major(t):
    b, l, f = t.shape
    rows = l // GRID_W
    return t.reshape(b, GRID_W, rows, f).transpose(0, 2, 1, 3).reshape(b, l, f)


def kernel(x, c, ctx, c_ctx, w_ada, b_ada, norm1_g, norm2_g, w_in, gdn_conv_w, gdn_a_log, gdn_dt_bias, gdn_norm_g,
           ml_i_bias, ml_f_bias, ml_norm_g, w_branch_gdn, w_branch_ml, w_out, w_router, router_bias, w_exp_gate_up,
           w_exp_down, w_sh_gate_up, w_sh_down, final_norm_g):
    b, l, d = x.shape
    lc = ctx.shape[1]
    t = b * l
    layer = 0

    w = w_in[layer]
    main_cols = [_ORIG[k] for k in ("gdn_qkv", "gdn_z", "ml_q", "ml_k", "ml_v", "ml_o", "mg_gdn", "mg_ml")]
    w_main = jnp.concatenate([w[:, a:e] for a, e in main_cols], axis=1).astype(BF16)
    w_gate = jnp.concatenate([w[:, _ORIG["gdn_gate"][0]:_ORIG["gdn_gate"][1]],
                              w[:, _ORIG["ml_gate"][0]:_ORIG["ml_gate"][1]],
                              jnp.zeros((d, LANES - 64), F32)], axis=1).astype(BF16)
    zeros16 = jnp.zeros((16,), F32)
    gp_add = jnp.concatenate([zeros16, gdn_dt_bias[layer].reshape(-1), ml_i_bias[layer].reshape(-1),
                              ml_f_bias[layer].reshape(-1), jnp.zeros((LANES - 64,), F32)])
    gp_mul = jnp.concatenate([zeros16, -jnp.exp(gdn_a_log[layer].astype(F32)).reshape(-1),
                              jnp.zeros((LANES - 32,), F32)])
    gparams = jnp.zeros((8, LANES), F32).at[0].set(gp_add).at[1].set(gp_mul)
    conv_w8 = jnp.zeros((8, gdn_conv_w.shape[2]), F32).at[0:GDN_CONV].set(gdn_conv_w[layer])
    wr = w_router[layer].T
    wr_hi = wr.astype(BF16)
    wr_lo = (wr - wr_hi.astype(F32)).astype(BF16)

    n_mod_rows = -(-(b + 1) // 8) * 8
    cc = jnp.zeros((n_mod_rows, d), F32).at[0:b].set(c).at[b].set(c_ctx)
    mod = _ada_mod(cc, w_ada[layer], b_ada[layer])
    mod3 = mod.reshape(n_mod_rows, 1, 6 * d)

    x2d = x.reshape(t, d)
    tm_l = min(1024, l)
    proj_l, gate_l = _project(x2d, mod3, lambda i: (i * tm_l) // l, norm1_g[layer], w_main, w_gate, tm_l)
    tm_c = min(1024, b * lc)
    proj_c, gate_c = _project(ctx.reshape(b * lc, d), mod3, lambda i: b, norm1_g[layer], w_main, w_gate, tm_c)
    proj_l3 = proj_l.reshape(b, l, N_MAIN)
    proj_c3 = proj_c.reshape(b, lc, N_MAIN)

    gate_l_cm = _col_major(gate_l.reshape(b, l, LANES)).reshape(t, LANES)
    gd_c, ml_c = _gate_prep(gate_c, gparams)
    gd_l, _ = _gate_prep(gate_l, gparams)
    _, ml_l = _gate_prep(gate_l_cm, gparams)

    y_gdn = _gdn(proj_c3, proj_l3, conv_w8, gd_c.reshape(b, lc, LANES), gd_l.reshape(b, l, LANES),
                 gdn_norm_g[layer])
    q_cm = _col_major(proj_l3[:, :, COL_ML_Q:COL_ML_Q + HEADS * ML_DK])
    k_cm = _col_major(proj_l3[:, :, COL_ML_K:COL_ML_K + HEADS * ML_DK])
    v_cm = _col_major(proj_l3[:, :, COL_ML_V:COL_ML_V + HEADS * HEAD_V])
    h_ml = _row_major(_mlstm(proj_c3, q_cm, k_cm, v_cm, ml_c.reshape(b, lc, LANES), ml_l.reshape(b, l, LANES)))

    x1, h2, scores_t = _merge(y_gdn.reshape(t, d), h_ml.reshape(t, d), proj_l, x2d, mod3, l, ml_norm_g[layer],
                            norm2_g[layer], w_branch_gdn[layer].astype(BF16), w_branch_ml[layer].astype(BF16),
                            w_out[layer].astype(BF16), wr_hi, wr_lo, tm=min(512, l))

    tp = t // MOE_PARTS
    n_assign = tp * TOP_K
    n_blocks = (n_assign + N_EXPERTS * (EXPERT_ROWS - 1)) // EXPERT_ROWS + 1
    n_slots = n_blocks * EXPERT_ROWS
    w_sh_gu, w_sh_dn = w_sh_gate_up[layer].astype(BF16), w_sh_down[layer].astype(BF16)
    outs = []
    for part in range(MOE_PARTS):
        tok0 = part * tp
        idx, wts, rank, cnt = _route(scores_t[:, tok0:tok0 + tp], router_bias[layer])
        counts = cnt[:, 0].astype(jnp.int32)
        pstart, blk_expert, n_used = _block_table(counts, n_blocks)
        pos = _slots(idx, rank, pstart).reshape(n_assign)
        tok_ids = jnp.broadcast_to((jnp.arange(n_assign, dtype=jnp.int32) % tp)[:, None], (n_assign, LANES))
        scattered = _sc_scatter_rows(tok_ids, pos, n_slots)[:, 0]
        in_expert = (jnp.arange(n_slots, dtype=jnp.int32).reshape(n_blocks, EXPERT_ROWS)
                     - pstart[blk_expert][:, None])
        valid = (in_expert < counts[blk_expert][:, None]).reshape(n_slots)
        tok_slot = jnp.where(valid, scattered, jnp.arange(n_slots, dtype=jnp.int32) % tp)
        xb = h2.at[tok_slot + tok0].get(mode="promise_in_bounds")
        yb = _experts(xb, blk_expert, n_used, w_exp_gate_up[layer], w_exp_down[layer])
        yg = yb.at[pos].get(mode="promise_in_bounds", unique_indices=True).reshape(TOP_K, tp, d)
        outs.append(_final(x1, h2, yg, wts.T, mod3, l, w_sh_gu, w_sh_dn, final_norm_g, tok0, tm=min(256, l)))
    return jnp.concatenate(outs, axis=0).reshape(b, l, d)
```

```python
import functools
import math

import jax
import jax.numpy as jnp
from jax import lax
from jax.experimental import pallas as pl
from jax.experimental.pallas import tpu as pltpu
from jax.experimental.pallas import tpu_sc as plsc

F32 = jnp.float32
BF16 = jnp.bfloat16
HI = lax.Precision.HIGHEST

EPS = 1e-6
CHUNK = 64
GRID_W = 64
HEADS = 8
HEAD_V = 128
GDN_DK = 128
ML_DK = 64
GDN_CONV = 5
N_EXPERTS = 256
TOP_K = 8
N_GROUPS = 8
TOPK_GROUPS = 4
ROUTED_SCALE = 2.5
EXPERT_ROWS = 512
MOE_PARTS = 1
GDN_STEPS = 4
RING = 2 * GDN_STEPS
LANES = 128
VMEM_LIMIT = 56 * 1024 * 1024

COL_GDN_QKV = 0
COL_GDN_Z = 3072
COL_ML_Q = 4096
COL_ML_K = 4608
COL_ML_V = 5120
COL_ML_O = 6144
COL_MG_GDN = 7168
COL_MG_ML = 8192
N_MAIN = 9216
_ORIG = dict(gdn_qkv=(0, 3072), gdn_z=(3072, 4096), gdn_gate=(4096, 4128), ml_q=(4128, 4640),
             ml_k=(4640, 5152), ml_v=(5152, 6176), ml_o=(6176, 7200), ml_gate=(7200, 7232),
             mg_gdn=(7232, 8256), mg_ml=(8256, 9280))


def _params(sem, vmem=VMEM_LIMIT):
    return pltpu.CompilerParams(dimension_semantics=sem, vmem_limit_bytes=vmem)


def _dot(a, b, precision=None):
    return jnp.dot(a, b, preferred_element_type=F32, precision=precision)


def _dot_nt(a, b, precision=None):
    return lax.dot_general(a, b, (((1,), (1,)), ((), ())), preferred_element_type=F32, precision=precision)


def _dot_tn(a, b, precision=None):
    return lax.dot_general(a, b, (((0,), (0,)), ((), ())), preferred_element_type=F32, precision=precision)


def _times3(a, b):
    ah = a.astype(BF16)
    al = (a - ah.astype(F32)).astype(BF16)
    bh = b.astype(BF16)
    bl = (b - bh.astype(F32)).astype(BF16)
    return _dot(ah, bh) + (_dot(ah, bl) + _dot(al, bh))


def _sigmoid(x):
    return 1.0 / (1.0 + jnp.exp(-x))


def _softplus(x):
    return jnp.maximum(x, 0.0) + jnp.log(1.0 + jnp.exp(-jnp.abs(x)))


def _ada_kernel(c_ref, w_ref, b_ref, o_ref):
    c = c_ref[...]
    sc = c * _sigmoid(c)
    o_ref[...] = _dot(sc, w_ref[...], HI) + b_ref[...]


def _ada_mod(cc, w_ada, b_ada, tn=1536):
    r, d = cc.shape
    n = w_ada.shape[1]
    return pl.pallas_call(
        _ada_kernel,
        grid=(n // tn,),
        in_specs=[pl.BlockSpec((r, d), lambda j: (0, 0)),
                  pl.BlockSpec((d, tn), lambda j: (0, j)),
                  pl.BlockSpec((1, tn), lambda j: (0, j))],
        out_specs=pl.BlockSpec((r, tn), lambda j: (0, j)),
        out_shape=jax.ShapeDtypeStruct((r, n), F32),
        compiler_params=_params(("arbitrary",)),
    )(cc, w_ada, b_ada.reshape(1, n))


def _proj_kernel(x_ref, mod_ref, g_ref, w_ref, wg_ref, o_ref, og_ref, hn_ref):
    d = x_ref.shape[1]

    @pl.when(pl.program_id(1) == 0)
    def _():
        x = x_ref[...]
        y = x * lax.rsqrt(jnp.mean(x * x, axis=-1, keepdims=True) + EPS) * g_ref[...]
        shift = mod_ref[0, :, 0:d]
        scale = mod_ref[0, :, d:2 * d]
        h = (y * (1.0 + scale) + shift).astype(BF16)
        hn_ref[...] = h
        og_ref[...] = _dot(h, wg_ref[...])

    o_ref[...] = _dot(hn_ref[...], w_ref[...]).astype(o_ref.dtype)


def _project(x2d, mod3, mod_row_of_tile, norm_g, w_main, w_gate, tm, tn=1024):
    t, d = x2d.shape
    n = w_main.shape[1]
    return pl.pallas_call(
        _proj_kernel,
        grid=(t // tm, n // tn),
        in_specs=[pl.BlockSpec((tm, d), lambda i, j: (i, 0)),
                  pl.BlockSpec((1, 1, mod3.shape[2]), lambda i, j: (mod_row_of_tile(i), 0, 0)),
                  pl.BlockSpec((1, d), lambda i, j: (0, 0)),
                  pl.BlockSpec((d, tn), lambda i, j: (0, j)),
                  pl.BlockSpec((d, LANES), lambda i, j: (0, 0))],
        out_specs=[pl.BlockSpec((tm, tn), lambda i, j: (i, j)),
                   pl.BlockSpec((tm, LANES), lambda i, j: (i, 0))],
        out_shape=[jax.ShapeDtypeStruct((t, n), BF16), jax.ShapeDtypeStruct((t, LANES), F32)],
        scratch_shapes=[pltpu.VMEM((tm, d), BF16)],
        compiler_params=_params(("arbitrary", "arbitrary")),
    )(x2d, mod3, norm_g.reshape(1, d), w_main, w_gate)


def _gate_kernel(g_ref, p_ref, gd_ref, ml_ref):
    rows = g_ref.shape[0]
    raw = g_ref[...] + p_ref[0:1, :]
    lane = lax.broadcasted_iota(jnp.int32, raw.shape, 1)
    sp = _softplus(raw)
    vals = jnp.where(lane < 16, _sigmoid(raw),
                     jnp.where(lane < 32, p_ref[1:2, :] * sp,
                               jnp.where(lane < 48, raw,
                                         jnp.where(lane < 64, -_softplus(-raw), 0.0))))
    ri = lax.broadcasted_iota(jnp.int32, (CHUNK, CHUNK), 0)
    ci = lax.broadcasted_iota(jnp.int32, (CHUNK, CHUNK), 1)
    tri_f = (ri >= ci).astype(F32)
    tri_b = (ri <= ci).astype(F32)
    lane_c = lax.broadcasted_iota(jnp.int32, (CHUNK, LANES), 1)
    row_c = lax.broadcasted_iota(jnp.int32, (CHUNK, LANES), 0)
    fwd_lane = (lane_c % 16) < 8
    for c in range(rows // CHUNK):
        blk = vals[c * CHUNK:(c + 1) * CHUNK, :]
        cum = jnp.where(fwd_lane, _dot(tri_f, blk, HI), _dot(tri_b, blk, HI))
        gd_ref[c * CHUNK:(c + 1) * CHUNK, :] = jnp.where(lane_c < 16, blk, jnp.where(lane_c < 32, cum, 0.0))
        bcum = pltpu.roll(cum, LANES - 16, axis=1)
        gmb = blk - bcum
        cmf, cmb = gmb, gmb
        for s in (1, 2, 4, 8, 16, 32):
            cmf = jnp.maximum(cmf, jnp.where(row_c >= s, pltpu.roll(cmf, s, axis=0), -jnp.inf))
            cmb = jnp.maximum(cmb, jnp.where(row_c < CHUNK - s, pltpu.roll(cmb, CHUNK - s, axis=0), -jnp.inf))
        cm = jnp.where(fwd_lane, cmf, cmb)
        ml = jnp.where(lane_c < 16, pltpu.roll(gmb, LANES - 32, axis=1),
                       jnp.where(lane_c < 32, pltpu.roll(cm, LANES - 16, axis=1),
                                 jnp.where(lane_c < 48, bcum, 0.0)))
        ml_ref[c * CHUNK:(c + 1) * CHUNK, :] = ml


def _gate_prep(graw, gparams, tm=256):
    t = graw.shape[0]
    spec = pl.BlockSpec((tm, LANES), lambda i: (i, 0))
    return pl.pallas_call(
        _gate_kernel,
        grid=(t // tm,),
        in_specs=[spec, pl.BlockSpec((8, LANES), lambda i: (0, 0))],
        out_specs=[spec, spec],
        out_shape=[jax.ShapeDtypeStruct((t, LANES), F32)] * 2,
        compiler_params=_params(("arbitrary",)),
    )(graw, gparams)


def _split3(a):
    h = a.astype(BF16)
    r = a - h.astype(F32)
    m = r.astype(BF16)
    return h, m, (r - m.astype(F32)).astype(BF16)


def _lane_picks(x, lanes):
    li = lax.broadcasted_iota(jnp.int32, (LANES, LANES), 0)
    ci = lax.broadcasted_iota(jnp.int32, (LANES, LANES), 1)
    want = jnp.full((LANES, LANES), -1, jnp.int32)
    for j, lane in enumerate(lanes):
        want = jnp.where(ci == j, lane, want)
    sel = (li == want).astype(BF16)
    h, m, lo = _split3(x)
    cols = _dot(h, sel) + (_dot(m, sel) + _dot(lo, sel))
    return [jnp.broadcast_to(cols[:, j:j + 1], x.shape) for j in range(len(lanes))]


def _dir_masks(direction):
    ri = lax.broadcasted_iota(jnp.int32, (CHUNK, CHUNK), 0)
    ci = lax.broadcasted_iota(jnp.int32, (CHUNK, CHUNK), 1)
    if direction == 0:
        return ri >= ci, ri > ci
    return ri <= ci, ri < ci


def _gdn_kernel(qc_ref, kc_ref, vc_ref, ql_ref, kl_ref, vl_ref, z_ref, cwq_ref, cwk_ref, cwv_ref,
                gdc_ref, gdl_ref, ng_ref, y_ref,
                xpad, qs, ks, vs, beta_t, cg_t, wq_r, u_r, kd_r, qk_r, dc_r, out_s):
    lc = qc_ref.shape[1]
    ll = ql_ref.shape[1]
    lt = lc + ll
    n_c, n_l = lc // CHUNK, ll // CHUNK
    n_t = n_c + n_l
    rb = 256

    def l2n(x):
        return x * lax.rsqrt(jnp.sum(x * x, axis=-1, keepdims=True) + EPS)

    def prep(src_ref, cw_ref, dst, off, ls, kind):
        xpad[0:8, :] = jnp.zeros((8, LANES), F32)
        xpad[8:8 + ls, :] = src_ref[0].astype(F32)
        xpad[8 + ls:16 + ls, :] = jnp.zeros((8, LANES), F32)
        step = min(rb, ls)
        for r0 in range(0, ls, step):
            acc = jnp.zeros((step, LANES), F32)
            for t in range(GDN_CONV):
                s0 = r0 + 8 - GDN_CONV // 2 + t
                acc = acc + cw_ref[t:t + 1, :] * xpad[s0:s0 + step, :]
            y = acc * _sigmoid(acc)
            if kind == "q":
                y = l2n(y) * (GDN_DK ** -0.5)
            elif kind == "k":
                y = l2n(y)
            dst[off + r0:off + r0 + step, :] = y

    prep(qc_ref, cwq_ref, qs, 0, lc, "q")
    prep(kc_ref, cwk_ref, ks, 0, lc, "k")
    prep(vc_ref, cwv_ref, vs, 0, lc, "v")
    prep(ql_ref, cwq_ref, qs, lc, ll, "q")
    prep(kl_ref, cwk_ref, ks, lc, ll, "k")
    prep(vl_ref, cwv_ref, vs, lc, ll, "v")

    eye = (lax.broadcasted_iota(jnp.int32, (CHUNK, CHUNK), 0)
           == lax.broadcasted_iota(jnp.int32, (CHUNK, CHUNK), 1)).astype(F32)

    masks = (_dir_masks(0), _dir_masks(1))
    head = pl.program_id(1)

    def build_tables(src_ref, off, ls):
        step = min(rb, ls)
        for r0 in range(0, ls, step):
            picked = _lane_picks(src_ref[0, r0:r0 + step, :], [8 * d + head for d in range(2)]
                                 + [16 + 8 * d + head for d in range(2)])
            for d in range(2):
                beta_t[d, off + r0:off + r0 + step, :] = picked[d]
                cg_t[d, off + r0:off + r0 + step, :] = picked[2 + d]

    build_tables(gdc_ref, 0, lc)
    build_tables(gdl_ref, lc, ll)

    def bwd_chunk(t):
        return jnp.where(t < n_c, n_c - 1 - t, n_t + n_c - 1 - t)

    def prep_chain(t, d):
        incl, strict = masks[d]
        last = CHUNK - 1 if d == 0 else 0
        tc = jnp.minimum(t, n_t - 1)
        c = tc if d == 0 else bwd_chunk(tc)
        r0 = pl.multiple_of(c * CHUNK, CHUNK)
        q = qs[pl.ds(r0, CHUNK), :]
        k = ks[pl.ds(r0, CHUNK), :]
        v = vs[pl.ds(r0, CHUNK), :]
        beta = beta_t[d, pl.ds(r0, CHUNK), :]
        cgc = cg_t[d, pl.ds(r0, CHUNK), :]
        cgr = jnp.transpose(cgc)[0:CHUNK, :]
        kk = _dot_nt(k, k)
        qk = _dot_nt(q, k)
        slot = (t % RING) * 2 + d
        yield
        cg_last = cgc[last:last + 1, :]
        decay = jnp.exp(jnp.where(incl, cgc[:, 0:CHUNK] - cgr, -jnp.inf))
        ecg = jnp.exp(cgc)
        kb = k * beta
        qk_r[slot] = qk * decay
        wq_r[slot, CHUNK:2 * CHUNK, :] = q * ecg
        kd_r[slot] = k * jnp.exp(cg_last - cgc)
        dc_r[slot] = jnp.broadcast_to(jnp.exp(cg_last), (8, LANES))
        x = jnp.where(strict, -(beta[:, 0:CHUNK] * kk) * decay, 0.0)
        tinv = eye + x
        x = _times3(x, x)
        yield
        for _ in range(4):
            both = _times3(jnp.concatenate([tinv, x], axis=0), x)
            tinv, x = tinv + both[0:CHUNK, :], both[CHUNK:2 * CHUNK, :]
            yield
        tinv = tinv + _times3(tinv, x)
        yield
        wq_r[slot, 0:CHUNK, :] = _dot(tinv, kb * ecg)
        u_r[slot] = _dot(tinv, v * beta)

    out_s[...] = jnp.zeros(out_s.shape, F32)

    def scan_chain(d, t0, s, with_out, result):
        for j in range(GDN_STEPS):
            t = t0 + j
            slot = (t % RING) * 2 + d
            ws = _dot(wq_r[slot], s)
            yield
            v_new = u_r[slot] - ws[0:CHUNK, :]
            if with_out:
                c = t if d == 0 else bwd_chunk(t)
                o = ws[CHUNK:2 * CHUNK, :] + _dot(qk_r[slot], v_new)
                l0 = pl.multiple_of((c - n_c) * CHUNK, CHUNK)
                out_s[pl.ds(l0, CHUNK), :] += o
            s = s * dc_r[slot][0:1, :] + _dot_tn(kd_r[slot], v_new)
            yield
        result[d] = s

    def lockstep(chains):
        chains = list(chains)
        while chains:
            alive = []
            for ch in chains:
                try:
                    next(ch)
                    alive.append(ch)
                except StopIteration:
                    pass
            chains = alive

    def pair_body(i, carry, t_base, with_out):
        t0 = t_base + GDN_STEPS * i
        result = [None, None]
        lockstep([scan_chain(0, t0, carry[0], with_out, result), scan_chain(1, t0, carry[1], with_out, result)]
                 + [prep_chain(t0 + GDN_STEPS + j, d) for j in range(GDN_STEPS) for d in range(2)])
        return result[0], result[1]

    lockstep([prep_chain(j, d) for j in range(GDN_STEPS) for d in range(2)])
    s0 = jnp.zeros((GDN_DK, HEAD_V), F32)
    carry = lax.fori_loop(0, n_c // GDN_STEPS, functools.partial(pair_body, t_base=0, with_out=False), (s0, s0))
    lax.fori_loop(0, n_l // GDN_STEPS, functools.partial(pair_body, t_base=n_c, with_out=True), carry)


    def out_body(i, carry):
        r0 = pl.multiple_of(i * rb, rb)
        o = out_s[pl.ds(r0, rb), :]
        z = z_ref[0, pl.ds(r0, rb), :].astype(F32)
        y = o * lax.rsqrt(jnp.mean(o * o, axis=-1, keepdims=True) + EPS) * ng_ref[...]
        y_ref[0, pl.ds(r0, rb), :] = (y * (z * _sigmoid(z))).astype(y_ref.dtype)
        return carry

    lax.fori_loop(0, ll // rb, out_body, 0)


def _gdn(proj_c, proj_l, conv_w8, gd_c, gd_l, norm_g):
    b, lc, _ = proj_c.shape
    ll = proj_l.shape[1]
    lt = lc + ll
    n_t = lt // CHUNK
    qb, kb_, vb, zb = (COL_GDN_QKV // LANES, COL_GDN_QKV // LANES + HEADS, COL_GDN_QKV // LANES + 2 * HEADS,
                       COL_GDN_Z // LANES)

    def seq_spec(l, col0):
        return pl.BlockSpec((1, l, LANES), lambda i, h: (i, 0, col0 + h))

    def cw_spec(col0):
        return pl.BlockSpec((8, LANES), lambda i, h: (0, col0 + h))

    return pl.pallas_call(
        _gdn_kernel,
        grid=(b, HEADS),
        in_specs=[seq_spec(lc, qb), seq_spec(lc, kb_), seq_spec(lc, vb),
                  seq_spec(ll, qb), seq_spec(ll, kb_), seq_spec(ll, vb), seq_spec(ll, zb),
                  cw_spec(0), cw_spec(HEADS), cw_spec(2 * HEADS),
                  pl.BlockSpec((1, lc, LANES), lambda i, h: (i, 0, 0)),
                  pl.BlockSpec((1, ll, LANES), lambda i, h: (i, 0, 0)),
                  pl.BlockSpec((1, LANES), lambda i, h: (0, 0))],
        out_specs=pl.BlockSpec((1, ll, LANES), lambda i, h: (i, 0, h)),
        out_shape=jax.ShapeDtypeStruct((b, ll, HEADS * HEAD_V), BF16),
        scratch_shapes=[pltpu.VMEM((max(lc, ll) + 16, LANES), F32),
                        pltpu.VMEM((lt, LANES), F32), pltpu.VMEM((lt, LANES), F32), pltpu.VMEM((lt, LANES), F32),
                        pltpu.VMEM((2, lt, LANES), F32), pltpu.VMEM((2, lt, LANES), F32),
                        pltpu.VMEM((2 * RING, 2 * CHUNK, LANES), F32),
                        pltpu.VMEM((2 * RING, CHUNK, LANES), F32), pltpu.VMEM((2 * RING, CHUNK, LANES), F32),
                        pltpu.VMEM((2 * RING, CHUNK, CHUNK), F32),
                        pltpu.VMEM((2 * RING, 8, LANES), F32),
                        pltpu.VMEM((ll, LANES), F32)],
        compiler_params=_params(("arbitrary", "arbitrary")),
    )(proj_c, proj_c, proj_c, proj_l, proj_l, proj_l, proj_l, conv_w8, conv_w8, conv_w8,
      gd_c, gd_l, norm_g.reshape(1, LANES))


def _mlstm_kernel(qc_ref, kc_ref, vc_ref, ql_ref, kl_ref, vl_ref, mlc_ref, mll_ref, h_ref, out_s, tabs):
    lc = qc_ref.shape[1]
    ll = ql_ref.shape[1]
    n_c, n_l = lc // CHUNK, ll // CHUNK
    n_t = n_c + n_l
    lt = lc + ll
    pair = pl.program_id(1)
    lane = lax.broadcasted_iota(jnp.int32, (CHUNK, LANES), 1)
    ones_v = jnp.ones((CHUNK, HEAD_V), BF16)
    chains = [(hh, d) for hh in range(2) for d in range(2)]
    hmask = [((lane // ML_DK) == hh).astype(F32) for hh in range(2)]
    incl = [_dir_masks(d)[0] for d in range(2)]

    def build_tables(src_ref, off, ls):
        step = min(256, ls)
        for r0 in range(0, ls, step):
            lanes = [16 * j + 8 * d + 2 * pair + hh for hh, d in chains for j in range(3)]
            for g, tab in enumerate(_lane_picks(src_ref[0, r0:r0 + step, :], lanes)):
                tabs[g, off + r0:off + r0 + step, :] = tab

    build_tables(mlc_ref, 0, lc)
    build_tables(mll_ref, lc, ll)

    def wide(a):
        return jnp.concatenate([a, a], axis=1)

    def chain(hh, d, c, state, is_ctx, result):
        cs, ms = state
        last = CHUNK - 1 if d == 0 else 0
        if is_ctx:
            rows = pl.ds(pl.multiple_of(c * CHUNK, CHUNK), CHUNK)
            q_ref, k_ref, v_ref = qc_ref, kc_ref, vc_ref
        else:
            rows = pl.ds(pl.multiple_of((c - n_c) * CHUNK, CHUNK), CHUNK)
            q_ref, k_ref, v_ref = ql_ref, kl_ref, vl_ref
        q = (q_ref[0, rows, :].astype(F32) * hmask[hh]).astype(BF16)
        k = k_ref[0, rows, :].astype(F32) * (hmask[hh] * (ML_DK ** -0.5))
        v = jnp.concatenate([v_ref[0, rows, hh * HEAD_V:(hh + 1) * HEAD_V], ones_v], axis=1)
        n = chains.index((hh, d))
        trows = pl.ds(pl.multiple_of(c * CHUNK, CHUNK), CHUNK)
        gmb = tabs[3 * n, trows, :]
        gmb_t = jnp.transpose(gmb)[0:CHUNK, :]
        cm = tabs[3 * n + 1, trows, :]
        bc = tabs[3 * n + 2, trows, :]
        qk = _dot_nt(q, k.astype(BF16))
        yield
        cm_last = cm[last:last + 1, :]
        b_last = bc[last:last + 1, :]
        mm = jnp.maximum(ms, cm)
        p = jnp.where(incl[d], jnp.exp(gmb_t - mm[:, 0:CHUNK]), 0.0) * qk
        wk = (k * jnp.exp(gmb - cm_last)).astype(BF16)
        inter = _dot(q, cs.astype(BF16))
        intra = _dot(p.astype(BF16), v)
        c_loc = _dot_tn(wk, v)
        yield
        if not is_ctx:
            nd = wide(jnp.exp(ms - mm)) * inter + intra
            hv = nd[:, 0:HEAD_V] / jnp.maximum(jnp.abs(nd[:, HEAD_V:2 * HEAD_V]), jnp.exp(-(bc + mm)))
            l0 = pl.multiple_of((c - n_c) * CHUNK, CHUNK)
            out_s[pl.ds(l0, CHUNK), hh * HEAD_V:(hh + 1) * HEAD_V] += hv
        mx = jnp.maximum(ms, cm_last)
        result[hh, d] = (wide(jnp.exp(ms - mx)) * cs + wide(jnp.exp(cm_last - mx)) * c_loc, b_last + mx)

    out_s[...] = jnp.zeros(out_s.shape, F32)

    def run(chains_iter):
        live = list(chains_iter)
        while live:
            alive = []
            for ch in live:
                try:
                    next(ch)
                    alive.append(ch)
                except StopIteration:
                    pass
            live = alive

    def body(i, carry, is_ctx):
        result = {}
        gens = []
        for n, (hh, d) in enumerate(chains):
            if is_ctx:
                c = i if d == 0 else n_c - 1 - i
            else:
                c = n_c + i if d == 0 else n_t - 1 - i
            gens.append(chain(hh, d, c, carry[n], is_ctx, result))
        run(gens)
        return tuple(result[hd] for hd in chains)

    st0 = (jnp.zeros((LANES, 2 * HEAD_V), F32), jnp.zeros((1, LANES), F32))
    carry = lax.fori_loop(0, n_c, functools.partial(body, is_ctx=True), (st0,) * 4)
    lax.fori_loop(0, n_l, functools.partial(body, is_ctx=False), carry)
    h_ref[0] = out_s[...].astype(h_ref.dtype)


def _mlstm(proj_c, q_l, k_l, v_l, ml_c, ml_l):
    b, lc, _ = proj_c.shape
    ll = q_l.shape[1]
    lt = lc + ll
    qb, kb_, vb = COL_ML_Q // LANES, COL_ML_K // LANES, COL_ML_V // (2 * HEAD_V)
    return pl.pallas_call(
        _mlstm_kernel,
        grid=(b, HEADS // 2),
        in_specs=[pl.BlockSpec((1, lc, LANES), lambda i, p: (i, 0, qb + p)),
                  pl.BlockSpec((1, lc, LANES), lambda i, p: (i, 0, kb_ + p)),
                  pl.BlockSpec((1, lc, 2 * HEAD_V), lambda i, p: (i, 0, vb + p)),
                  pl.BlockSpec((1, ll, LANES), lambda i, p: (i, 0, p)),
                  pl.BlockSpec((1, ll, LANES), lambda i, p: (i, 0, p)),
                  pl.BlockSpec((1, ll, 2 * HEAD_V), lambda i, p: (i, 0, p)),
                  pl.BlockSpec((1, lc, LANES), lambda i, p: (i, 0, 0)),
                  pl.BlockSpec((1, ll, LANES), lambda i, p: (i, 0, 0))],
        out_specs=pl.BlockSpec((1, ll, 2 * HEAD_V), lambda i, p: (i, 0, p)),
        out_shape=jax.ShapeDtypeStruct((b, ll, HEADS * HEAD_V), BF16),
        scratch_shapes=[pltpu.VMEM((ll, 2 * HEAD_V), F32), pltpu.VMEM((12, lt, LANES), F32)],
        compiler_params=_params(("arbitrary", "arbitrary")),
    )(proj_c, proj_c, proj_c, q_l, k_l, v_l, ml_c, ml_l)


def _merge_kernel(yg_ref, hm_ref, o_ref, gg_ref, gm_ref, x_ref, mod_ref, mlg_ref, n2_ref,
                  wbg_ref, wbm_ref, wo_ref, wrh_ref, wrl_ref, x1_ref, h2_ref, sc_ref):
    d = x_ref.shape[1]
    o = o_ref[...].astype(F32)
    ym = _sigmoid(o) * hm_ref[...].astype(F32)
    segs = []
    for h in range(HEADS):
        seg = ym[:, h * HEAD_V:(h + 1) * HEAD_V]
        segs.append(seg * lax.rsqrt(jnp.mean(seg * seg, axis=-1, keepdims=True) + EPS))
    ymn = jnp.concatenate(segs, axis=1) * mlg_ref[...]
    y_gdn = _dot(yg_ref[...], wbg_ref[...])
    y_ml = _dot(ymn.astype(BF16), wbm_ref[...])
    mixed = _sigmoid(gg_ref[...].astype(F32)) * y_gdn + _sigmoid(gm_ref[...].astype(F32)) * y_ml
    y = _dot(mixed.astype(BF16), wo_ref[...])
    x1 = x_ref[...] + mod_ref[0, :, 2 * d:3 * d] * y
    x1_ref[...] = x1
    hn = x1 * lax.rsqrt(jnp.mean(x1 * x1, axis=-1, keepdims=True) + EPS) * n2_ref[...]
    h2 = hn * (1.0 + mod_ref[0, :, 4 * d:5 * d]) + mod_ref[0, :, 3 * d:4 * d]
    h2_hi = h2.astype(BF16)
    h2_ref[...] = h2_hi
    h2_lo = (h2 - h2_hi.astype(F32)).astype(BF16)
    logits = _dot_nt(wrh_ref[...], h2_hi) + (_dot_nt(wrl_ref[...], h2_hi) + _dot_nt(wrh_ref[...], h2_lo))
    sc_ref[...] = _sigmoid(logits)


def _merge(y_gdn, h_ml, proj_l2d, x2d, mod3, rows_per_mod, ml_norm_g, norm2_g, wbg, wbm, wo, wr_hi, wr_lo, tm=512):
    t, d = x2d.shape
    e = wr_hi.shape[0]
    row = lambda i: (i, 0)
    const = lambda i: (0, 0)
    return pl.pallas_call(
        _merge_kernel,
        grid=(t // tm,),
        in_specs=[pl.BlockSpec((tm, d), row), pl.BlockSpec((tm, d), row),
                  pl.BlockSpec((tm, d), lambda i: (i, COL_ML_O // d)),
                  pl.BlockSpec((tm, d), lambda i: (i, COL_MG_GDN // d)),
                  pl.BlockSpec((tm, d), lambda i: (i, COL_MG_ML // d)),
                  pl.BlockSpec((tm, d), row),
                  pl.BlockSpec((1, 1, mod3.shape[2]), lambda i: ((i * tm) // rows_per_mod, 0, 0)),
                  pl.BlockSpec((1, d), const), pl.BlockSpec((1, d), const),
                  pl.BlockSpec((d, d), const), pl.BlockSpec((d, d), const), pl.BlockSpec((d, d), const),
                  pl.BlockSpec((e, d), const), pl.BlockSpec((e, d), const)],
        out_specs=[pl.BlockSpec((tm, d), row), pl.BlockSpec((tm, d), row), pl.BlockSpec((e, tm), lambda i: (0, i))],
        out_shape=[jax.ShapeDtypeStruct((t, d), F32), jax.ShapeDtypeStruct((t, d), BF16),
                   jax.ShapeDtypeStruct((e, t), F32)],
        compiler_params=_params(("arbitrary",)),
    )(y_gdn, h_ml, proj_l2d, proj_l2d, proj_l2d, x2d, mod3, ml_norm_g.reshape(1, d), norm2_g.reshape(1, d),
      wbg, wbm, wo, wr_hi, wr_lo)


def _expert_kernel(be_ref, nu_ref, nx_ref, x_ref, wgu_hbm, wd_hbm, y_ref, land_gu, land_d, wgu_s, wd_s, sems):
    i = pl.program_id(0)
    de = wd_hbm.shape[1]
    used = i < nu_ref[0]
    first = jnp.logical_or(i == 0, be_ref[i] != be_ref[jnp.maximum(i - 1, 0)])

    def weight_copies(ex):
        return (pltpu.make_async_copy(wgu_hbm.at[ex], land_gu, sems.at[0]),
                pltpu.make_async_copy(wd_hbm.at[ex], land_d, sems.at[1]))

    @pl.when(i == 0)
    def _():
        for cp in weight_copies(be_ref[0]):
            cp.start()

    @pl.when(jnp.logical_and(first, used))
    def _():
        for cp in weight_copies(be_ref[i]):
            cp.wait()
        wgu_s[...] = land_gu[...].astype(BF16)
        wd_s[...] = land_d[...].astype(BF16)

        @pl.when(nx_ref[i] >= 0)
        def _():
            for cp in weight_copies(nx_ref[i]):
                cp.start()

    @pl.when(used)
    def _():
        gu = _dot(x_ref[...], wgu_s[...])
        g = gu[:, 0:de]
        act = (g * _sigmoid(g)) * gu[:, de:2 * de]
        y_ref[...] = _dot(act.astype(BF16), wd_s[...]).astype(y_ref.dtype)

    @pl.when(jnp.logical_not(used))
    def _():
        y_ref[...] = jnp.zeros(y_ref.shape, y_ref.dtype)


def _experts(xb, blk_expert, n_used, next_expert, w_gu, w_down):
    n_slots, d = xb.shape
    n_blocks = n_slots // EXPERT_ROWS
    e, _, de2 = w_gu.shape
    de = de2 // 2
    return pl.pallas_call(
        _expert_kernel,
        grid_spec=pltpu.PrefetchScalarGridSpec(
            num_scalar_prefetch=3,
            grid=(n_blocks,),
            in_specs=[pl.BlockSpec((EXPERT_ROWS, d), lambda i, be, nu, nx: (i, 0)),
                      pl.BlockSpec(memory_space=pl.ANY), pl.BlockSpec(memory_space=pl.ANY)],
            out_specs=pl.BlockSpec((EXPERT_ROWS, d), lambda i, be, nu, nx: (i, 0)),
            scratch_shapes=[pltpu.VMEM((d, de2), w_gu.dtype), pltpu.VMEM((de, d), w_down.dtype),
                            pltpu.VMEM((d, de2), BF16), pltpu.VMEM((de, d), BF16),
                            pltpu.SemaphoreType.DMA((2,))]),
        out_shape=jax.ShapeDtypeStruct((n_slots, d), BF16),
        compiler_params=_params(("arbitrary",)),
    )(blk_expert, n_used, next_expert, xb, w_gu, w_down)


def _final_kernel(x1_ref, h2_ref, yg_ref, wt_ref, mod_ref, wsg_ref, wsd_ref, fg_ref, o_ref):
    d = x1_ref.shape[1]
    ds_ = wsd_ref.shape[0]
    wt = wt_ref[...]
    routed = jnp.zeros(x1_ref.shape, F32)
    for k in range(TOP_K):
        routed = routed + wt[:, k:k + 1] * yg_ref[k].astype(F32)
    gu = _dot(h2_ref[...], wsg_ref[...])
    g = gu[:, 0:ds_]
    sh = _dot(((g * _sigmoid(g)) * gu[:, ds_:2 * ds_]).astype(BF16), wsd_ref[...])
    x2 = x1_ref[...] + mod_ref[0, :, 5 * d:6 * d] * (routed + sh)
    o_ref[...] = x2 * lax.rsqrt(jnp.mean(x2 * x2, axis=-1, keepdims=True) + EPS) * fg_ref[...]


def _final(x1, h2, yg, wts, mod3, rows_per_mod, w_sh_gu, w_sh_down, final_g, tok0, tm=256):
    tp = yg.shape[1]
    d = x1.shape[1]
    off = tok0 // tm
    row = lambda i: (i, 0)
    row_off = lambda i: (i + off, 0)
    const = lambda i: (0, 0)
    return pl.pallas_call(
        _final_kernel,
        grid=(tp // tm,),
        in_specs=[pl.BlockSpec((tm, d), row_off), pl.BlockSpec((tm, d), row_off),
                  pl.BlockSpec((TOP_K, tm, d), lambda i: (0, i, 0)), pl.BlockSpec((tm, TOP_K), row),
                  pl.BlockSpec((1, 1, mod3.shape[2]), lambda i: (((i + off) * tm) // rows_per_mod, 0, 0)),
                  pl.BlockSpec(w_sh_gu.shape, const), pl.BlockSpec(w_sh_down.shape, const),
                  pl.BlockSpec((1, d), const)],
        out_specs=pl.BlockSpec((tm, d), row),
        out_shape=jax.ShapeDtypeStruct((tp, d), F32),
        compiler_params=_params(("arbitrary",)),
    )(x1, h2, yg, wts, mod3, w_sh_gu, w_sh_down, final_g.reshape(1, d))


def _route_kernel(sc_ref, bias_ref, tri_ref, idx_ref, wt_ref, rk_ref, cnt_ref, base_s):
    @pl.when(pl.program_id(0) == 0)
    def _():
        base_s[...] = jnp.zeros(base_s.shape, F32)

    scores = sc_ref[...]
    e, tn = scores.shape
    gsz = e // N_GROUPS
    sel3 = (scores + bias_ref[...]).reshape(N_GROUPS, gsz, tn)
    m1 = jnp.max(sel3, axis=1)
    is_max = sel3 == m1[:, None, :]
    n_max = jnp.sum(is_max.astype(F32), axis=1)
    m2 = jnp.max(jnp.where(is_max, -jnp.inf, sel3), axis=1)
    grp = m1 + jnp.where(n_max >= 2.0, m1, m2)
    gi = lax.broadcasted_iota(jnp.int32, (N_GROUPS, tn), 0)
    ahead = jnp.zeros((N_GROUPS, tn), F32)
    for g in range(N_GROUPS):
        row = grp[g:g + 1, :]
        ahead = ahead + jnp.logical_or(row > grp, jnp.logical_and(row == grp, g < gi)).astype(F32)
    ahead3 = jnp.broadcast_to(ahead[:, None, :], (N_GROUPS, gsz, tn))
    selm = jnp.where(ahead3 < float(TOPK_GROUPS), sel3, -jnp.inf).reshape(e, tn)
    ri = lax.broadcasted_iota(jnp.int32, (e, tn), 0).astype(F32)
    member = jnp.zeros((e, tn), F32)
    idxs, ws = [], []
    for _ in range(TOP_K):
        m = jnp.max(selm, axis=0, keepdims=True)
        idx = jnp.min(jnp.where(selm == m, ri, float(e)), axis=0, keepdims=True)
        hit = ri == idx
        ws.append(jnp.sum(jnp.where(hit, scores, 0.0), axis=0, keepdims=True))
        idxs.append(idx)
        selm = jnp.where(hit, -jnp.inf, selm)
        member = jnp.where(hit, 1.0, member)
    w = jnp.concatenate(ws, axis=0)
    wt_ref[...] = w / jnp.sum(w, axis=0, keepdims=True) * ROUTED_SCALE
    idx_ref[...] = jnp.concatenate(idxs, axis=0).astype(jnp.int32)
    cum = _dot(member.astype(BF16), tri_ref[...]) + base_s[...]
    rk_ref[...] = jnp.concatenate(
        [jnp.sum(jnp.where(ri == idx, cum, 0.0), axis=0, keepdims=True) for idx in idxs], axis=0).astype(jnp.int32)
    total = base_s[...] + jnp.sum(member, axis=1, keepdims=True)
    base_s[...] = total
    cnt_ref[...] = total


def _route(scores_t, router_bias):
    e, t = scores_t.shape
    tn = LANES
    bias = jnp.broadcast_to(router_bias.astype(F32)[:, None], (e, tn))
    tri = (jnp.arange(tn)[:, None] < jnp.arange(tn)[None, :]).astype(BF16)
    tok = pl.BlockSpec((TOP_K, tn), lambda i: (0, i))
    const = lambda i: (0, 0)
    return pl.pallas_call(
        _route_kernel,
        grid=(t // tn,),
        in_specs=[pl.BlockSpec((e, tn), lambda i: (0, i)), pl.BlockSpec((e, tn), const),
                  pl.BlockSpec((tn, tn), const)],
        out_specs=[tok, tok, tok, pl.BlockSpec((e, tn), const)],
        out_shape=[jax.ShapeDtypeStruct((TOP_K, t), jnp.int32), jax.ShapeDtypeStruct((TOP_K, t), F32),
                   jax.ShapeDtypeStruct((TOP_K, t), jnp.int32), jax.ShapeDtypeStruct((e, tn), F32)],
        scratch_shapes=[pltpu.VMEM((e, tn), F32)],
        compiler_params=_params(("arbitrary",)),
    )(scores_t, bias, tri)


def _slot_kernel(idx_ref, rk_ref, ps_ref, pos_ref):
    e, tn = ps_ref.shape
    ri = lax.broadcasted_iota(jnp.int32, (e, tn), 0)
    ps = ps_ref[...]
    rows = [jnp.sum(jnp.where(ri == idx_ref[k:k + 1, :], ps, 0.0), axis=0, keepdims=True) for k in range(TOP_K)]
    pos_ref[...] = rk_ref[...] + jnp.concatenate(rows, axis=0).astype(jnp.int32)


def _slots(idx, rank, pstart):
    k, t = idx.shape
    e = pstart.shape[0]
    tn = LANES
    tok = pl.BlockSpec((k, tn), lambda i: (0, i))
    return pl.pallas_call(
        _slot_kernel,
        grid=(t // tn,),
        in_specs=[tok, tok, pl.BlockSpec((e, tn), lambda i: (0, 0))],
        out_specs=tok,
        out_shape=jax.ShapeDtypeStruct((k, t), jnp.int32),
        compiler_params=_params(("arbitrary",)),
    )(idx, rank, jnp.broadcast_to(pstart.astype(F32)[:, None], (e, tn)))


def _sc_scatter_rows(vals, idx, n_rows, window=LANES):
    n, width = vals.shape
    mesh = plsc.VectorSubcoreMesh(core_axis_name="core", subcore_axis_name="subcore")

    @pl.kernel(out_type=jax.ShapeDtypeStruct((n_rows, width), vals.dtype), mesh=mesh, scratch_types=[])
    def scatter(v_hbm, i_hbm, o_hbm):
        def body(v_vmem, i_vmem):
            pltpu.sync_copy(v_vmem, o_hbm.at[i_vmem.at[0]])

        pltpu.emit_pipeline(
            body,
            grid=(n // window,),
            in_specs=[pl.BlockSpec((window, width), lambda i: (i, 0)),
                      pl.BlockSpec((1, window), lambda i: (0, i))],
            out_specs=[],
            core_axis_name=("core", "subcore"),
            dimension_semantics=(pltpu.PARALLEL,),
        )(v_hbm, i_hbm)

    return scatter(vals, idx.reshape(1, n))


def _block_table(counts, n_blocks):
    padded = (counts + EXPERT_ROWS - 1) // EXPERT_ROWS * EXPERT_ROWS
    pend = jnp.cumsum(padded)
    blk_expert = jnp.minimum(jnp.searchsorted(pend, jnp.arange(n_blocks) * EXPERT_ROWS, side='right'),
                             N_EXPERTS - 1).astype(jnp.int32)
    e_ids = jnp.arange(N_EXPERTS, dtype=jnp.int32)
    later = lax.cummin(jnp.where(counts > 0, e_ids, N_EXPERTS)[::-1])[::-1]
    next_tab = jnp.concatenate([later[1:], jnp.full((1,), N_EXPERTS, jnp.int32)])
    next_tab = jnp.where(next_tab < N_EXPERTS, next_tab, -1)
    return (pend - padded, blk_expert, (pend[-1] // EXPERT_ROWS).astype(jnp.int32).reshape(1),
            next_tab[blk_expert].astype(jnp.int32))


def _col_major(t):
    b, l, f = t.shape
    rows = l // GRID_W
    return t.reshape(b, rows, GRID_W, f).transpose(0, 2, 1, 3).reshape(b, l, f)


def _row_major(t):
    b, l, f = t.shape
    rows = l // GRID_W
    return t.reshape(b, GRID_W, rows, f).transpose(0, 2, 1, 3).reshape(b, l, f)


def kernel(x, c, ctx, c_ctx, w_ada, b_ada, norm1_g, norm2_g, w_in, gdn_conv_w, gdn_a_log, gdn_dt_bias, gdn_norm_g,
           ml_i_bias, ml_f_bias, ml_norm_g, w_branch_gdn, w_branch_ml, w_out, w_router, router_bias, w_exp_gate_up,
           w_exp_down, w_sh_gate_up, w_sh_down, final_norm_g):
    b, l, d = x.shape
    lc = ctx.shape[1]
    t = b * l
    layer = 0

    w = w_in[layer]
    main_cols = [_ORIG[k] for k in ("gdn_qkv", "gdn_z", "ml_q", "ml_k", "ml_v", "ml_o", "mg_gdn", "mg_ml")]
    w_main = jnp.concatenate([w[:, a:e] for a, e in main_cols], axis=1).astype(BF16)
    w_gate = jnp.concatenate([w[:, _ORIG["gdn_gate"][0]:_ORIG["gdn_gate"][1]],
                              w[:, _ORIG["ml_gate"][0]:_ORIG["ml_gate"][1]],
                              jnp.zeros((d, LANES - 64), F32)], axis=1).astype(BF16)
    zeros16 = jnp.zeros((16,), F32)
    gp_add = jnp.concatenate([zeros16, gdn_dt_bias[layer].reshape(-1), ml_i_bias[layer].reshape(-1),
                              ml_f_bias[layer].reshape(-1), jnp.zeros((LANES - 64,), F32)])
    gp_mul = jnp.concatenate([zeros16, -jnp.exp(gdn_a_log[layer].astype(F32)).reshape(-1),
                              jnp.zeros((LANES - 32,), F32)])
    gparams = jnp.zeros((8, LANES), F32).at[0].set(gp_add).at[1].set(gp_mul)
    conv_w8 = jnp.zeros((8, gdn_conv_w.shape[2]), F32).at[0:GDN_CONV].set(gdn_conv_w[layer])
    wr = w_router[layer].T
    wr_hi = wr.astype(BF16)
    wr_lo = (wr - wr_hi.astype(F32)).astype(BF16)

    n_mod_rows = -(-(b + 1) // 8) * 8
    cc = jnp.zeros((n_mod_rows, d), F32).at[0:b].set(c).at[b].set(c_ctx)
    mod = _ada_mod(cc, w_ada[layer], b_ada[layer])
    mod3 = mod.reshape(n_mod_rows, 1, 6 * d)

    x2d = x.reshape(t, d)
    tm_l = min(1024, l)
    proj_l, gate_l = _project(x2d, mod3, lambda i: (i * tm_l) // l, norm1_g[layer], w_main, w_gate, tm_l)
    tm_c = min(1024, b * lc)
    proj_c, gate_c = _project(ctx.reshape(b * lc, d), mod3, lambda i: b, norm1_g[layer], w_main, w_gate, tm_c)
    proj_l3 = proj_l.reshape(b, l, N_MAIN)
    proj_c3 = proj_c.reshape(b, lc, N_MAIN)

    gate_l_cm = _col_major(gate_l.reshape(b, l, LANES)).reshape(t, LANES)
    gd_c, ml_c = _gate_prep(gate_c, gparams)
    gd_l, _ = _gate_prep(gate_l, gparams)
    _, ml_l = _gate_prep(gate_l_cm, gparams)

    y_gdn = _gdn(proj_c3, proj_l3, conv_w8, gd_c.reshape(b, lc, LANES), gd_l.reshape(b, l, LANES),
                 gdn_norm_g[layer])
    q_cm = _col_major(proj_l3[:, :, COL_ML_Q:COL_ML_Q + HEADS * ML_DK])
    k_cm = _col_major(proj_l3[:, :, COL_ML_K:COL_ML_K + HEADS * ML_DK])
    v_cm = _col_major(proj_l3[:, :, COL_ML_V:COL_ML_V + HEADS * HEAD_V])
    h_ml = _row_major(_mlstm(proj_c3, q_cm, k_cm, v_cm, ml_c.reshape(b, lc, LANES), ml_l.reshape(b, l, LANES)))

    x1, h2, scores_t = _merge(y_gdn.reshape(t, d), h_ml.reshape(t, d), proj_l, x2d, mod3, l, ml_norm_g[layer],
                            norm2_g[layer], w_branch_gdn[layer].astype(BF16), w_branch_ml[layer].astype(BF16),
                            w_out[layer].astype(BF16), wr_hi, wr_lo, tm=min(512, l))

    tp = t // MOE_PARTS
    n_assign = tp * TOP_K
    n_blocks = (n_assign + N_EXPERTS * (EXPERT_ROWS - 1)) // EXPERT_ROWS + 1
    n_slots = n_blocks * EXPERT_ROWS
    w_sh_gu, w_sh_dn = w_sh_gate_up[layer].astype(BF16), w_sh_down[layer].astype(BF16)
    outs = []
    for part in range(MOE_PARTS):
        tok0 = part * tp
        idx, wts, rank, cnt = _route(scores_t[:, tok0:tok0 + tp], router_bias[layer])
        counts = cnt[:, 0].astype(jnp.int32)
        pstart, blk_expert, n_used, next_expert = _block_table(counts, n_blocks)
        pos = _slots(idx, rank, pstart).reshape(n_assign)
        tok_ids = jnp.broadcast_to((jnp.arange(n_assign, dtype=jnp.int32) % tp)[:, None], (n_assign, LANES))
        scattered = _sc_scatter_rows(tok_ids, pos, n_slots)[:, 0]
        in_expert = (jnp.arange(n_slots, dtype=jnp.int32).reshape(n_blocks, EXPERT_ROWS)
                     - pstart[blk_expert][:, None])
        valid = (in_expert < counts[blk_expert][:, None]).reshape(n_slots)
        tok_slot = jnp.where(valid, scattered, jnp.arange(n_slots, dtype=jnp.int32) % tp)
        xb = h2.at[tok_slot + tok0].get(mode="promise_in_bounds")
        yb = _experts(xb, blk_expert, n_used, next_expert, w_exp_gate_up[layer], w_exp_down[layer])
        yg = yb.at[pos].get(mode="promise_in_bounds", unique_indices=True).reshape(TOP_K, tp, d)
        outs.append(_final(x1, h2, yg, wts.T, mod3, l, w_sh_gu, w_sh_dn, final_norm_g, tok0, tm=min(256, l)))
    return jnp.concatenate(outs, axis=0).reshape(b, l, d)
```

```python
import functools
import math

import jax
import jax.numpy as jnp
from jax import lax
from jax.experimental import pallas as pl
from jax.experimental.pallas import tpu as pltpu
from jax.experimental.pallas import tpu_sc as plsc

F32 = jnp.float32
BF16 = jnp.bfloat16
HI = lax.Precision.HIGHEST

EPS = 1e-6
CHUNK = 64
GRID_W = 64
HEADS = 8
HEAD_V = 128
GDN_DK = 128
ML_DK = 64
GDN_CONV = 5
N_EXPERTS = 256
TOP_K = 8
N_GROUPS = 8
TOPK_GROUPS = 4
ROUTED_SCALE = 2.5
EXPERT_ROWS = 512
MOE_PARTS = 1
GDN_STEPS = 4
GDN_CTX_STEPS = 4
RING = 2 * GDN_STEPS
LANES = 128
VMEM_LIMIT = 56 * 1024 * 1024

COL_GDN_QKV = 0
COL_GDN_Z = 3072
COL_ML_Q = 4096
COL_ML_K = 4608
COL_ML_V = 5120
COL_ML_O = 6144
COL_MG_GDN = 7168
COL_MG_ML = 8192
N_MAIN = 9216
_ORIG = dict(gdn_qkv=(0, 3072), gdn_z=(3072, 4096), gdn_gate=(4096, 4128), ml_q=(4128, 4640),
             ml_k=(4640, 5152), ml_v=(5152, 6176), ml_o=(6176, 7200), ml_gate=(7200, 7232),
             mg_gdn=(7232, 8256), mg_ml=(8256, 9280))


def _params(sem, vmem=VMEM_LIMIT):
    return pltpu.CompilerParams(dimension_semantics=sem, vmem_limit_bytes=vmem)


def _dot(a, b, precision=None):
    return jnp.dot(a, b, preferred_element_type=F32, precision=precision)


def _dot_nt(a, b, precision=None):
    return lax.dot_general(a, b, (((1,), (1,)), ((), ())), preferred_element_type=F32, precision=precision)


def _dot_tn(a, b, precision=None):
    return lax.dot_general(a, b, (((0,), (0,)), ((), ())), preferred_element_type=F32, precision=precision)


def _times3(a, b):
    ah = a.astype(BF16)
    al = (a - ah.astype(F32)).astype(BF16)
    bh = b.astype(BF16)
    bl = (b - bh.astype(F32)).astype(BF16)
    return _dot(ah, bh) + (_dot(ah, bl) + _dot(al, bh))


def _sigmoid(x):
    return 1.0 / (1.0 + jnp.exp(-x))


def _softplus(x):
    return jnp.maximum(x, 0.0) + jnp.log(1.0 + jnp.exp(-jnp.abs(x)))


def _ada_kernel(c_ref, w_ref, b_ref, o_ref):
    c = c_ref[...]
    sc = c * _sigmoid(c)
    o_ref[...] = _dot(sc, w_ref[...], HI) + b_ref[...]


def _ada_mod(cc, w_ada, b_ada, tn=1536):
    r, d = cc.shape
    n = w_ada.shape[1]
    return pl.pallas_call(
        _ada_kernel,
        grid=(n // tn,),
        in_specs=[pl.BlockSpec((r, d), lambda j: (0, 0)),
                  pl.BlockSpec((d, tn), lambda j: (0, j)),
                  pl.BlockSpec((1, tn), lambda j: (0, j))],
        out_specs=pl.BlockSpec((r, tn), lambda j: (0, j)),
        out_shape=jax.ShapeDtypeStruct((r, n), F32),
        compiler_params=_params(("arbitrary",)),
    )(cc, w_ada, b_ada.reshape(1, n))


def _proj_kernel(x_ref, mod_ref, g_ref, w_ref, wg_ref, o_ref, og_ref, hn_ref):
    d = x_ref.shape[1]

    @pl.when(pl.program_id(1) == 0)
    def _():
        x = x_ref[...]
        y = x * lax.rsqrt(jnp.mean(x * x, axis=-1, keepdims=True) + EPS) * g_ref[...]
        shift = mod_ref[0, :, 0:d]
        scale = mod_ref[0, :, d:2 * d]
        h = (y * (1.0 + scale) + shift).astype(BF16)
        hn_ref[...] = h
        og_ref[...] = _dot(h, wg_ref[...])

    o_ref[...] = _dot(hn_ref[...], w_ref[...]).astype(o_ref.dtype)


def _project(x2d, mod3, mod_row_of_tile, norm_g, w_main, w_gate, tm, tn=1024):
    t, d = x2d.shape
    n = w_main.shape[1]
    return pl.pallas_call(
        _proj_kernel,
        grid=(t // tm, n // tn),
        in_specs=[pl.BlockSpec((tm, d), lambda i, j: (i, 0)),
                  pl.BlockSpec((1, 1, mod3.shape[2]), lambda i, j: (mod_row_of_tile(i), 0, 0)),
                  pl.BlockSpec((1, d), lambda i, j: (0, 0)),
                  pl.BlockSpec((d, tn), lambda i, j: (0, j)),
                  pl.BlockSpec((d, LANES), lambda i, j: (0, 0))],
        out_specs=[pl.BlockSpec((tm, tn), lambda i, j: (i, j)),
                   pl.BlockSpec((tm, LANES), lambda i, j: (i, 0))],
        out_shape=[jax.ShapeDtypeStruct((t, n), BF16), jax.ShapeDtypeStruct((t, LANES), F32)],
        scratch_shapes=[pltpu.VMEM((tm, d), BF16)],
        compiler_params=_params(("arbitrary", "arbitrary")),
    )(x2d, mod3, norm_g.reshape(1, d), w_main, w_gate)


def _gate_kernel(g_ref, p_ref, gd_ref, ml_ref):
    rows = g_ref.shape[0]
    raw = g_ref[...] + p_ref[0:1, :]
    lane = lax.broadcasted_iota(jnp.int32, raw.shape, 1)
    sp = _softplus(raw)
    vals = jnp.where(lane < 16, _sigmoid(raw),
                     jnp.where(lane < 32, p_ref[1:2, :] * sp,
                               jnp.where(lane < 48, raw,
                                         jnp.where(lane < 64, -_softplus(-raw), 0.0))))
    ri = lax.broadcasted_iota(jnp.int32, (CHUNK, CHUNK), 0)
    ci = lax.broadcasted_iota(jnp.int32, (CHUNK, CHUNK), 1)
    tri_f = (ri >= ci).astype(F32)
    tri_b = (ri <= ci).astype(F32)
    lane_c = lax.broadcasted_iota(jnp.int32, (CHUNK, LANES), 1)
    row_c = lax.broadcasted_iota(jnp.int32, (CHUNK, LANES), 0)
    fwd_lane = (lane_c % 16) < 8
    for c in range(rows // CHUNK):
        blk = vals[c * CHUNK:(c + 1) * CHUNK, :]
        cum = jnp.where(fwd_lane, _dot(tri_f, blk, HI), _dot(tri_b, blk, HI))
        gd_ref[c * CHUNK:(c + 1) * CHUNK, :] = jnp.where(lane_c < 16, blk, jnp.where(lane_c < 32, cum, 0.0))
        bcum = pltpu.roll(cum, LANES - 16, axis=1)
        gmb = blk - bcum
        cmf, cmb = gmb, gmb
        for s in (1, 2, 4, 8, 16, 32):
            cmf = jnp.maximum(cmf, jnp.where(row_c >= s, pltpu.roll(cmf, s, axis=0), -jnp.inf))
            cmb = jnp.maximum(cmb, jnp.where(row_c < CHUNK - s, pltpu.roll(cmb, CHUNK - s, axis=0), -jnp.inf))
        cm = jnp.where(fwd_lane, cmf, cmb)
        ml = jnp.where(lane_c < 16, pltpu.roll(gmb, LANES - 32, axis=1),
                       jnp.where(lane_c < 32, pltpu.roll(cm, LANES - 16, axis=1),
                                 jnp.where(lane_c < 48, bcum, 0.0)))
        ml_ref[c * CHUNK:(c + 1) * CHUNK, :] = ml


def _gate_prep(graw, gparams, tm=256):
    t = graw.shape[0]
    spec = pl.BlockSpec((tm, LANES), lambda i: (i, 0))
    return pl.pallas_call(
        _gate_kernel,
        grid=(t // tm,),
        in_specs=[spec, pl.BlockSpec((8, LANES), lambda i: (0, 0))],
        out_specs=[spec, spec],
        out_shape=[jax.ShapeDtypeStruct((t, LANES), F32)] * 2,
        compiler_params=_params(("arbitrary",)),
    )(graw, gparams)


def _split3(a):
    h = a.astype(BF16)
    r = a - h.astype(F32)
    m = r.astype(BF16)
    return h, m, (r - m.astype(F32)).astype(BF16)


def _lane_picks(x, lanes):
    li = lax.broadcasted_iota(jnp.int32, (LANES, LANES), 0)
    ci = lax.broadcasted_iota(jnp.int32, (LANES, LANES), 1)
    want = jnp.full((LANES, LANES), -1, jnp.int32)
    for j, lane in enumerate(lanes):
        want = jnp.where(ci == j, lane, want)
    sel = (li == want).astype(BF16)
    h, m, lo = _split3(x)
    cols = _dot(h, sel) + (_dot(m, sel) + _dot(lo, sel))
    return [jnp.broadcast_to(cols[:, j:j + 1], x.shape) for j in range(len(lanes))]


def _dir_masks(direction):
    ri = lax.broadcasted_iota(jnp.int32, (CHUNK, CHUNK), 0)
    ci = lax.broadcasted_iota(jnp.int32, (CHUNK, CHUNK), 1)
    if direction == 0:
        return ri >= ci, ri > ci
    return ri <= ci, ri < ci


def _gdn_kernel(qc_ref, kc_ref, vc_ref, ql_ref, kl_ref, vl_ref, z_ref, cwq_ref, cwk_ref, cwv_ref,
                gdc_ref, gdl_ref, ng_ref, y_ref,
                xpad, qs, ks, vs, beta_t, cg_t, wq_r, u_r, kd_r, qk_r, dc_r, out_s):
    lc = qc_ref.shape[1]
    ll = ql_ref.shape[1]
    lt = lc + ll
    n_c, n_l = lc // CHUNK, ll // CHUNK
    n_t = n_c + n_l
    rb = 256

    def l2n(x):
        return x * lax.rsqrt(jnp.sum(x * x, axis=-1, keepdims=True) + EPS)

    def prep(src_ref, cw_ref, dst, off, ls, kind):
        xpad[0:8, :] = jnp.zeros((8, LANES), F32)
        xpad[8:8 + ls, :] = src_ref[0].astype(F32)
        xpad[8 + ls:16 + ls, :] = jnp.zeros((8, LANES), F32)
        step = min(rb, ls)
        for r0 in range(0, ls, step):
            acc = jnp.zeros((step, LANES), F32)
            for t in range(GDN_CONV):
                s0 = r0 + 8 - GDN_CONV // 2 + t
                acc = acc + cw_ref[t:t + 1, :] * xpad[s0:s0 + step, :]
            y = acc * _sigmoid(acc)
            if kind == "q":
                y = l2n(y) * (GDN_DK ** -0.5)
            elif kind == "k":
                y = l2n(y)
            dst[off + r0:off + r0 + step, :] = y

    prep(qc_ref, cwq_ref, qs, 0, lc, "q")
    prep(kc_ref, cwk_ref, ks, 0, lc, "k")
    prep(vc_ref, cwv_ref, vs, 0, lc, "v")
    prep(ql_ref, cwq_ref, qs, lc, ll, "q")
    prep(kl_ref, cwk_ref, ks, lc, ll, "k")
    prep(vl_ref, cwv_ref, vs, lc, ll, "v")

    head = pl.program_id(1)

    def build_tables(src_ref, off, ls):
        step = min(rb, ls)
        for r0 in range(0, ls, step):
            picked = _lane_picks(src_ref[0, r0:r0 + step, :], [8 * d + head for d in range(2)]
                                 + [16 + 8 * d + head for d in range(2)])
            for d in range(2):
                beta_t[d, off + r0:off + r0 + step, :] = picked[d]
                cg_t[d, off + r0:off + r0 + step, :] = picked[2 + d]

    build_tables(gdc_ref, 0, lc)
    build_tables(gdl_ref, lc, ll)

    def bwd_chunk(t):
        return jnp.where(t < n_c, n_c - 1 - t, n_t + n_c - 1 - t)

    row_p = lax.broadcasted_iota(jnp.int32, (CHUNK, LANES), 0)
    lane_p = lax.broadcasted_iota(jnp.int32, (CHUNK, LANES), 1)
    fwd_p = lane_p < CHUNK
    col_p = jnp.where(fwd_p, lane_p, lane_p - CHUNK)
    signed = jnp.where(fwd_p, row_p - col_p, col_p - row_p)
    incl_p = signed >= 0
    strict_p = signed > 0
    eye_p = (row_p == col_p).astype(F32)
    keep_f = fwd_p.astype(BF16)
    keep_b = (1.0 - fwd_p.astype(F32)).astype(BF16)

    def block_diag(top, bottom):
        zero = jnp.zeros(top.shape, top.dtype)
        return jnp.concatenate([jnp.concatenate([top, zero], axis=1), jnp.concatenate([zero, bottom], axis=1)], axis=0)

    def pair_diag(xp):
        return jnp.concatenate([xp * keep_f, xp * keep_b], axis=0)

    def pair_times3(a, xp):
        ah = a.astype(BF16)
        al = (a - ah.astype(F32)).astype(BF16)
        xh = xp.astype(BF16)
        xl = (xp - xh.astype(F32)).astype(BF16)
        dh = pair_diag(xh)
        return _dot(ah, dh) + (_dot(ah, pair_diag(xl)) + _dot(al, dh))

    def prep_chain(t):
        tc = jnp.minimum(t, n_t - 1)
        q, k, v, beta, cgc, kb, ecg = [], [], [], [], [], [], []
        for d in range(2):
            c = tc if d == 0 else bwd_chunk(tc)
            rows = pl.ds(pl.multiple_of(c * CHUNK, CHUNK), CHUNK)
            q.append(qs[rows, :])
            k.append(ks[rows, :])
            v.append(vs[rows, :])
            beta.append(beta_t[d, rows, :])
            cgc.append(cg_t[d, rows, :])
        kdiag = block_diag(k[0].astype(BF16), k[1].astype(BF16))
        kk = _dot_nt(jnp.concatenate([k[0], k[1]], axis=1).astype(BF16), kdiag)
        qk = _dot_nt(jnp.concatenate([q[0], q[1]], axis=1).astype(BF16), kdiag)
        yield
        beta_p = jnp.where(fwd_p, beta[0], beta[1])
        cg_p = jnp.where(fwd_p, cgc[0], cgc[1])
        cgr_p = jnp.transpose(jnp.concatenate([cgc[0], cgc[1]], axis=0))[0:CHUNK, :]
        decay = jnp.exp(jnp.where(incl_p, cg_p - cgr_p, -jnp.inf))
        qkd = qk * decay
        for d in range(2):
            slot = (t % RING) * 2 + d
            last = CHUNK - 1 if d == 0 else 0
            cg_last = cgc[d][last:last + 1, :]
            ecg.append(jnp.exp(cgc[d]))
            kb.append(k[d] * beta[d])
            qk_r[slot] = qkd[:, d * CHUNK:(d + 1) * CHUNK]
            wq_r[slot, CHUNK:2 * CHUNK, :] = q[d] * ecg[d]
            kd_r[slot] = k[d] * jnp.exp(cg_last - cgc[d])
            dc_r[slot] = jnp.broadcast_to(jnp.exp(cg_last), (8, LANES))
        x = jnp.where(strict_p, -(beta_p * kk) * decay, 0.0)
        tinv = eye_p + x
        x = pair_times3(x, x)
        yield
        for _ in range(4):
            both = pair_times3(jnp.concatenate([tinv, x], axis=0), x)
            tinv, x = tinv + both[0:CHUNK, :], both[CHUNK:2 * CHUNK, :]
            yield
        tinv = (tinv + pair_times3(tinv, x)).astype(BF16)
        yield
        w = _dot(tinv, block_diag((kb[0] * ecg[0]).astype(BF16), (kb[1] * ecg[1]).astype(BF16)))
        u = _dot(tinv, block_diag((v[0] * beta[0]).astype(BF16), (v[1] * beta[1]).astype(BF16)))
        for d in range(2):
            slot = (t % RING) * 2 + d
            wq_r[slot, 0:CHUNK, :] = w[:, d * LANES:(d + 1) * LANES]
            u_r[slot] = u[:, d * LANES:(d + 1) * LANES]

    out_s[...] = jnp.zeros(out_s.shape, F32)

    def scan_chain(d, t0, steps, s, with_out, result):
        for j in range(steps):
            t = t0 + j
            slot = (t % RING) * 2 + d
            ws = _dot(wq_r[slot], s)
            yield
            v_new = u_r[slot] - ws[0:CHUNK, :]
            if with_out:
                c = t if d == 0 else bwd_chunk(t)
                o = ws[CHUNK:2 * CHUNK, :] + _dot(qk_r[slot], v_new)
                l0 = pl.multiple_of((c - n_c) * CHUNK, CHUNK)
                out_s[pl.ds(l0, CHUNK), :] += o
            s = s * dc_r[slot][0:1, :] + _dot_tn(kd_r[slot], v_new)
            yield
        result[d] = s

    def lockstep(chains):
        chains = list(chains)
        while chains:
            alive = []
            for ch in chains:
                try:
                    next(ch)
                    alive.append(ch)
                except StopIteration:
                    pass
            chains = alive

    def group_body(i, carry, t_base, steps, ahead, with_out):
        t0 = t_base + steps * i
        result = [None, None]
        lockstep([scan_chain(d, t0, steps, carry[d], with_out, result) for d in range(2)]
                 + [prep_chain(t0 + steps + j) for j in range(ahead)])
        return result[0], result[1]

    def run_groups(carry, t_base, n_steps, steps, ahead_last, with_out):
        n_groups = n_steps // steps
        carry = lax.fori_loop(0, n_groups - 1, functools.partial(
            group_body, t_base=t_base, steps=steps, ahead=steps, with_out=with_out), carry)
        return group_body(n_groups - 1, carry, t_base, steps, ahead_last, with_out)

    lockstep([prep_chain(j) for j in range(GDN_CTX_STEPS)])
    s0 = jnp.zeros((GDN_DK, HEAD_V), F32)
    carry = run_groups((s0, s0), 0, n_c, GDN_CTX_STEPS, GDN_STEPS, False)
    run_groups(carry, n_c, n_l, GDN_STEPS, 0, True)


    def out_body(i, carry):
        r0 = pl.multiple_of(i * rb, rb)
        o = out_s[pl.ds(r0, rb), :]
        z = z_ref[0, pl.ds(r0, rb), :].astype(F32)
        y = o * lax.rsqrt(jnp.mean(o * o, axis=-1, keepdims=True) + EPS) * ng_ref[...]
        y_ref[0, pl.ds(r0, rb), :] = (y * (z * _sigmoid(z))).astype(y_ref.dtype)
        return carry

    lax.fori_loop(0, ll // rb, out_body, 0)


def _gdn(proj_c, proj_l, conv_w8, gd_c, gd_l, norm_g):
    b, lc, _ = proj_c.shape
    ll = proj_l.shape[1]
    lt = lc + ll
    n_t = lt // CHUNK
    qb, kb_, vb, zb = (COL_GDN_QKV // LANES, COL_GDN_QKV // LANES + HEADS, COL_GDN_QKV // LANES + 2 * HEADS,
                       COL_GDN_Z // LANES)

    def seq_spec(l, col0):
        return pl.BlockSpec((1, l, LANES), lambda i, h: (i, 0, col0 + h))

    def cw_spec(col0):
        return pl.BlockSpec((8, LANES), lambda i, h: (0, col0 + h))

    return pl.pallas_call(
        _gdn_kernel,
        grid=(b, HEADS),
        in_specs=[seq_spec(lc, qb), seq_spec(lc, kb_), seq_spec(lc, vb),
                  seq_spec(ll, qb), seq_spec(ll, kb_), seq_spec(ll, vb), seq_spec(ll, zb),
                  cw_spec(0), cw_spec(HEADS), cw_spec(2 * HEADS),
                  pl.BlockSpec((1, lc, LANES), lambda i, h: (i, 0, 0)),
                  pl.BlockSpec((1, ll, LANES), lambda i, h: (i, 0, 0)),
                  pl.BlockSpec((1, LANES), lambda i, h: (0, 0))],
        out_specs=pl.BlockSpec((1, ll, LANES), lambda i, h: (i, 0, h)),
        out_shape=jax.ShapeDtypeStruct((b, ll, HEADS * HEAD_V), BF16),
        scratch_shapes=[pltpu.VMEM((max(lc, ll) + 16, LANES), F32),
                        pltpu.VMEM((lt, LANES), F32), pltpu.VMEM((lt, LANES), F32), pltpu.VMEM((lt, LANES), F32),
                        pltpu.VMEM((2, lt, LANES), F32), pltpu.VMEM((2, lt, LANES), F32),
                        pltpu.VMEM((2 * RING, 2 * CHUNK, LANES), F32),
                        pltpu.VMEM((2 * RING, CHUNK, LANES), F32), pltpu.VMEM((2 * RING, CHUNK, LANES), F32),
                        pltpu.VMEM((2 * RING, CHUNK, CHUNK), F32),
                        pltpu.VMEM((2 * RING, 8, LANES), F32),
                        pltpu.VMEM((ll, LANES), F32)],
        compiler_params=_params(("arbitrary", "arbitrary")),
    )(proj_c, proj_c, proj_c, proj_l, proj_l, proj_l, proj_l, conv_w8, conv_w8, conv_w8,
      gd_c, gd_l, norm_g.reshape(1, LANES))


def _mlstm_kernel(qc_ref, kc_ref, vc_ref, ql_ref, kl_ref, vl_ref, mlc_ref, mll_ref, h_ref, out_s, tabs):
    lc = qc_ref.shape[1]
    ll = ql_ref.shape[1]
    n_c, n_l = lc // CHUNK, ll // CHUNK
    n_t = n_c + n_l
    lt = lc + ll
    pair = pl.program_id(1)
    lane = lax.broadcasted_iota(jnp.int32, (CHUNK, LANES), 1)
    ones_v = jnp.ones((CHUNK, HEAD_V), BF16)
    chains = [(hh, d) for hh in range(2) for d in range(2)]
    hmask = [((lane // ML_DK) == hh).astype(F32) for hh in range(2)]
    incl = [_dir_masks(d)[0] for d in range(2)]

    def build_tables(src_ref, off, ls):
        step = min(256, ls)
        for r0 in range(0, ls, step):
            lanes = [16 * j + 8 * d + 2 * pair + hh for hh, d in chains for j in range(3)]
            for g, tab in enumerate(_lane_picks(src_ref[0, r0:r0 + step, :], lanes)):
                tabs[g, off + r0:off + r0 + step, :] = tab

    build_tables(mlc_ref, 0, lc)
    build_tables(mll_ref, lc, ll)

    def wide(a):
        return jnp.concatenate([a, a], axis=1)

    def chain(hh, d, c, state, is_ctx, result):
        cs, ms = state
        last = CHUNK - 1 if d == 0 else 0
        if is_ctx:
            rows = pl.ds(pl.multiple_of(c * CHUNK, CHUNK), CHUNK)
            q_ref, k_ref, v_ref = qc_ref, kc_ref, vc_ref
        else:
            rows = pl.ds(pl.multiple_of((c - n_c) * CHUNK, CHUNK), CHUNK)
            q_ref, k_ref, v_ref = ql_ref, kl_ref, vl_ref
        q = (q_ref[0, rows, :].astype(F32) * hmask[hh]).astype(BF16)
        k = k_ref[0, rows, :].astype(F32) * (hmask[hh] * (ML_DK ** -0.5))
        v = jnp.concatenate([v_ref[0, rows, hh * HEAD_V:(hh + 1) * HEAD_V], ones_v], axis=1)
        n = chains.index((hh, d))
        trows = pl.ds(pl.multiple_of(c * CHUNK, CHUNK), CHUNK)
        gmb = tabs[3 * n, trows, :]
        gmb_t = jnp.transpose(gmb)[0:CHUNK, :]
        cm = tabs[3 * n + 1, trows, :]
        bc = tabs[3 * n + 2, trows, :]
        qk = _dot_nt(q, k.astype(BF16))
        yield
        cm_last = cm[last:last + 1, :]
        b_last = bc[last:last + 1, :]
        mm = jnp.maximum(ms, cm)
        p = jnp.where(incl[d], jnp.exp(gmb_t - mm[:, 0:CHUNK]), 0.0) * qk
        wk = (k * jnp.exp(gmb - cm_last)).astype(BF16)
        inter = _dot(q, cs.astype(BF16))
        intra = _dot(p.astype(BF16), v)
        c_loc = _dot_tn(wk, v)
        yield
        if not is_ctx:
            nd = wide(jnp.exp(ms - mm)) * inter + intra
            hv = nd[:, 0:HEAD_V] / jnp.maximum(jnp.abs(nd[:, HEAD_V:2 * HEAD_V]), jnp.exp(-(bc + mm)))
            l0 = pl.multiple_of((c - n_c) * CHUNK, CHUNK)
            out_s[pl.ds(l0, CHUNK), hh * HEAD_V:(hh + 1) * HEAD_V] += hv
        mx = jnp.maximum(ms, cm_last)
        result[hh, d] = (wide(jnp.exp(ms - mx)) * cs + wide(jnp.exp(cm_last - mx)) * c_loc, b_last + mx)

    out_s[...] = jnp.zeros(out_s.shape, F32)

    def run(chains_iter):
        live = list(chains_iter)
        while live:
            alive = []
            for ch in live:
                try:
                    next(ch)
                    alive.append(ch)
                except StopIteration:
                    pass
            live = alive

    def body(i, carry, is_ctx):
        result = {}
        gens = []
        for n, (hh, d) in enumerate(chains):
            if is_ctx:
                c = i if d == 0 else n_c - 1 - i
            else:
                c = n_c + i if d == 0 else n_t - 1 - i
            gens.append(chain(hh, d, c, carry[n], is_ctx, result))
        run(gens)
        return tuple(result[hd] for hd in chains)

    st0 = (jnp.zeros((LANES, 2 * HEAD_V), F32), jnp.zeros((1, LANES), F32))
    carry = lax.fori_loop(0, n_c, functools.partial(body, is_ctx=True), (st0,) * 4)
    lax.fori_loop(0, n_l, functools.partial(body, is_ctx=False), carry)
    h_ref[0] = out_s[...].astype(h_ref.dtype)


def _mlstm(proj_c, q_l, k_l, v_l, ml_c, ml_l):
    b, lc, _ = proj_c.shape
    ll = q_l.shape[1]
    lt = lc + ll
    qb, kb_, vb = COL_ML_Q // LANES, COL_ML_K // LANES, COL_ML_V // (2 * HEAD_V)
    return pl.pallas_call(
        _mlstm_kernel,
        grid=(b, HEADS // 2),
        in_specs=[pl.BlockSpec((1, lc, LANES), lambda i, p: (i, 0, qb + p)),
                  pl.BlockSpec((1, lc, LANES), lambda i, p: (i, 0, kb_ + p)),
                  pl.BlockSpec((1, lc, 2 * HEAD_V), lambda i, p: (i, 0, vb + p)),
                  pl.BlockSpec((1, ll, LANES), lambda i, p: (i, 0, p)),
                  pl.BlockSpec((1, ll, LANES), lambda i, p: (i, 0, p)),
                  pl.BlockSpec((1, ll, 2 * HEAD_V), lambda i, p: (i, 0, p)),
                  pl.BlockSpec((1, lc, LANES), lambda i, p: (i, 0, 0)),
                  pl.BlockSpec((1, ll, LANES), lambda i, p: (i, 0, 0))],
        out_specs=pl.BlockSpec((1, ll, 2 * HEAD_V), lambda i, p: (i, 0, p)),
        out_shape=jax.ShapeDtypeStruct((b, ll, HEADS * HEAD_V), BF16),
        scratch_shapes=[pltpu.VMEM((ll, 2 * HEAD_V), F32), pltpu.VMEM((12, lt, LANES), F32)],
        compiler_params=_params(("arbitrary", "arbitrary")),
    )(proj_c, proj_c, proj_c, q_l, k_l, v_l, ml_c, ml_l)


def _merge_kernel(yg_ref, hm_ref, o_ref, gg_ref, gm_ref, x_ref, mod_ref, mlg_ref, n2_ref,
                  wbg_ref, wbm_ref, wo_ref, wrh_ref, wrl_ref, x1_ref, h2_ref, sc_ref):
    d = x_ref.shape[1]
    o = o_ref[...].astype(F32)
    ym = _sigmoid(o) * hm_ref[...].astype(F32)
    segs = []
    for h in range(HEADS):
        seg = ym[:, h * HEAD_V:(h + 1) * HEAD_V]
        segs.append(seg * lax.rsqrt(jnp.mean(seg * seg, axis=-1, keepdims=True) + EPS))
    ymn = jnp.concatenate(segs, axis=1) * mlg_ref[...]
    y_gdn = _dot(yg_ref[...], wbg_ref[...])
    y_ml = _dot(ymn.astype(BF16), wbm_ref[...])
    mixed = _sigmoid(gg_ref[...].astype(F32)) * y_gdn + _sigmoid(gm_ref[...].astype(F32)) * y_ml
    y = _dot(mixed.astype(BF16), wo_ref[...])
    x1 = x_ref[...] + mod_ref[0, :, 2 * d:3 * d] * y
    x1_ref[...] = x1
    hn = x1 * lax.rsqrt(jnp.mean(x1 * x1, axis=-1, keepdims=True) + EPS) * n2_ref[...]
    h2 = hn * (1.0 + mod_ref[0, :, 4 * d:5 * d]) + mod_ref[0, :, 3 * d:4 * d]
    h2_hi = h2.astype(BF16)
    h2_ref[...] = h2_hi
    h2_lo = (h2 - h2_hi.astype(F32)).astype(BF16)
    logits = _dot_nt(wrh_ref[...], h2_hi) + (_dot_nt(wrl_ref[...], h2_hi) + _dot_nt(wrh_ref[...], h2_lo))
    sc_ref[...] = _sigmoid(logits)


def _merge(y_gdn, h_ml, proj_l2d, x2d, mod3, rows_per_mod, ml_norm_g, norm2_g, wbg, wbm, wo, wr_hi, wr_lo, tm=512):
    t, d = x2d.shape
    e = wr_hi.shape[0]
    row = lambda i: (i, 0)
    const = lambda i: (0, 0)
    return pl.pallas_call(
        _merge_kernel,
        grid=(t // tm,),
        in_specs=[pl.BlockSpec((tm, d), row), pl.BlockSpec((tm, d), row),
                  pl.BlockSpec((tm, d), lambda i: (i, COL_ML_O // d)),
                  pl.BlockSpec((tm, d), lambda i: (i, COL_MG_GDN // d)),
                  pl.BlockSpec((tm, d), lambda i: (i, COL_MG_ML // d)),
                  pl.BlockSpec((tm, d), row),
                  pl.BlockSpec((1, 1, mod3.shape[2]), lambda i: ((i * tm) // rows_per_mod, 0, 0)),
                  pl.BlockSpec((1, d), const), pl.BlockSpec((1, d), const),
                  pl.BlockSpec((d, d), const), pl.BlockSpec((d, d), const), pl.BlockSpec((d, d), const),
                  pl.BlockSpec((e, d), const), pl.BlockSpec((e, d), const)],
        out_specs=[pl.BlockSpec((tm, d), row), pl.BlockSpec((tm, d), row), pl.BlockSpec((e, tm), lambda i: (0, i))],
        out_shape=[jax.ShapeDtypeStruct((t, d), F32), jax.ShapeDtypeStruct((t, d), BF16),
                   jax.ShapeDtypeStruct((e, t), F32)],
        compiler_params=_params(("arbitrary",)),
    )(y_gdn, h_ml, proj_l2d, proj_l2d, proj_l2d, x2d, mod3, ml_norm_g.reshape(1, d), norm2_g.reshape(1, d),
      wbg, wbm, wo, wr_hi, wr_lo)


def _expert_kernel(be_ref, nu_ref, nx_ref, x_ref, wgu_hbm, wd_hbm, y_ref, land_gu, land_d, wgu_s, wd_s, sems):
    i = pl.program_id(0)
    de = wd_hbm.shape[1]
    used = i < nu_ref[0]
    first = jnp.logical_or(i == 0, be_ref[i] != be_ref[jnp.maximum(i - 1, 0)])

    def weight_copies(ex):
        return (pltpu.make_async_copy(wgu_hbm.at[ex], land_gu, sems.at[0]),
                pltpu.make_async_copy(wd_hbm.at[ex], land_d, sems.at[1]))

    @pl.when(i == 0)
    def _():
        for cp in weight_copies(be_ref[0]):
            cp.start()

    @pl.when(jnp.logical_and(first, used))
    def _():
        for cp in weight_copies(be_ref[i]):
            cp.wait()
        wgu_s[...] = land_gu[...].astype(BF16)
        wd_s[...] = land_d[...].astype(BF16)

        @pl.when(nx_ref[i] >= 0)
        def _():
            for cp in weight_copies(nx_ref[i]):
                cp.start()

    @pl.when(used)
    def _():
        gu = _dot(x_ref[...], wgu_s[...])
        g = gu[:, 0:de]
        act = (g * _sigmoid(g)) * gu[:, de:2 * de]
        y_ref[...] = _dot(act.astype(BF16), wd_s[...]).astype(y_ref.dtype)

    @pl.when(jnp.logical_not(used))
    def _():
        y_ref[...] = jnp.zeros(y_ref.shape, y_ref.dtype)


def _experts(xb, blk_expert, n_used, next_expert, w_gu, w_down):
    n_slots, d = xb.shape
    n_blocks = n_slots // EXPERT_ROWS
    e, _, de2 = w_gu.shape
    de = de2 // 2
    return pl.pallas_call(
        _expert_kernel,
        grid_spec=pltpu.PrefetchScalarGridSpec(
            num_scalar_prefetch=3,
            grid=(n_blocks,),
            in_specs=[pl.BlockSpec((EXPERT_ROWS, d), lambda i, be, nu, nx: (i, 0)),
                      pl.BlockSpec(memory_space=pl.ANY), pl.BlockSpec(memory_space=pl.ANY)],
            out_specs=pl.BlockSpec((EXPERT_ROWS, d), lambda i, be, nu, nx: (i, 0)),
            scratch_shapes=[pltpu.VMEM((d, de2), w_gu.dtype), pltpu.VMEM((de, d), w_down.dtype),
                            pltpu.VMEM((d, de2), BF16), pltpu.VMEM((de, d), BF16),
                            pltpu.SemaphoreType.DMA((2,))]),
        out_shape=jax.ShapeDtypeStruct((n_slots, d), BF16),
        compiler_params=_params(("arbitrary",)),
    )(blk_expert, n_used, next_expert, xb, w_gu, w_down)


def _final_kernel(x1_ref, h2_ref, yg_ref, wt_ref, mod_ref, wsg_ref, wsd_ref, fg_ref, o_ref):
    d = x1_ref.shape[1]
    ds_ = wsd_ref.shape[0]
    wt = wt_ref[...]
    routed = jnp.zeros(x1_ref.shape, F32)
    for k in range(TOP_K):
        routed = routed + wt[:, k:k + 1] * yg_ref[k].astype(F32)
    gu = _dot(h2_ref[...], wsg_ref[...])
    g = gu[:, 0:ds_]
    sh = _dot(((g * _sigmoid(g)) * gu[:, ds_:2 * ds_]).astype(BF16), wsd_ref[...])
    x2 = x1_ref[...] + mod_ref[0, :, 5 * d:6 * d] * (routed + sh)
    o_ref[...] = x2 * lax.rsqrt(jnp.mean(x2 * x2, axis=-1, keepdims=True) + EPS) * fg_ref[...]


def _final(x1, h2, yg, wts, mod3, rows_per_mod, w_sh_gu, w_sh_down, final_g, tok0, tm=256):
    tp = yg.shape[1]
    d = x1.shape[1]
    off = tok0 // tm
    row = lambda i: (i, 0)
    row_off = lambda i: (i + off, 0)
    const = lambda i: (0, 0)
    return pl.pallas_call(
        _final_kernel,
        grid=(tp // tm,),
        in_specs=[pl.BlockSpec((tm, d), row_off), pl.BlockSpec((tm, d), row_off),
                  pl.BlockSpec((TOP_K, tm, d), lambda i: (0, i, 0)), pl.BlockSpec((tm, TOP_K), row),
                  pl.BlockSpec((1, 1, mod3.shape[2]), lambda i: (((i + off) * tm) // rows_per_mod, 0, 0)),
                  pl.BlockSpec(w_sh_gu.shape, const), pl.BlockSpec(w_sh_down.shape, const),
                  pl.BlockSpec((1, d), const)],
        out_specs=pl.BlockSpec((tm, d), row),
        out_shape=jax.ShapeDtypeStruct((tp, d), F32),
        compiler_params=_params(("arbitrary",)),
    )(x1, h2, yg, wts, mod3, w_sh_gu, w_sh_down, final_g.reshape(1, d))


def _route_kernel(sc_ref, bias_ref, tri_ref, idx_ref, wt_ref, rk_ref, cnt_ref, base_s):
    @pl.when(pl.program_id(0) == 0)
    def _():
        base_s[...] = jnp.zeros(base_s.shape, F32)

    scores = sc_ref[...]
    e, tn = scores.shape
    gsz = e // N_GROUPS
    sel3 = (scores + bias_ref[...]).reshape(N_GROUPS, gsz, tn)
    m1 = jnp.max(sel3, axis=1)
    is_max = sel3 == m1[:, None, :]
    n_max = jnp.sum(is_max.astype(F32), axis=1)
    m2 = jnp.max(jnp.where(is_max, -jnp.inf, sel3), axis=1)
    grp = m1 + jnp.where(n_max >= 2.0, m1, m2)
    gi = lax.broadcasted_iota(jnp.int32, (N_GROUPS, tn), 0)
    ahead = jnp.zeros((N_GROUPS, tn), F32)
    for g in range(N_GROUPS):
        row = grp[g:g + 1, :]
        ahead = ahead + jnp.logical_or(row > grp, jnp.logical_and(row == grp, g < gi)).astype(F32)
    ahead3 = jnp.broadcast_to(ahead[:, None, :], (N_GROUPS, gsz, tn))
    selm = jnp.where(ahead3 < float(TOPK_GROUPS), sel3, -jnp.inf).reshape(e, tn)
    ri = lax.broadcasted_iota(jnp.int32, (e, tn), 0).astype(F32)
    member = jnp.zeros((e, tn), F32)
    idxs, ws = [], []
    for _ in range(TOP_K):
        m = jnp.max(selm, axis=0, keepdims=True)
        idx = jnp.min(jnp.where(selm == m, ri, float(e)), axis=0, keepdims=True)
        hit = ri == idx
        ws.append(jnp.sum(jnp.where(hit, scores, 0.0), axis=0, keepdims=True))
        idxs.append(idx)
        selm = jnp.where(hit, -jnp.inf, selm)
        member = jnp.where(hit, 1.0, member)
    w = jnp.concatenate(ws, axis=0)
    wt_ref[...] = w / jnp.sum(w, axis=0, keepdims=True) * ROUTED_SCALE
    idx_ref[...] = jnp.concatenate(idxs, axis=0).astype(jnp.int32)
    cum = _dot(member.astype(BF16), tri_ref[...]) + base_s[...]
    rk_ref[...] = jnp.concatenate(
        [jnp.sum(jnp.where(ri == idx, cum, 0.0), axis=0, keepdims=True) for idx in idxs], axis=0).astype(jnp.int32)
    total = base_s[...] + jnp.sum(member, axis=1, keepdims=True)
    base_s[...] = total
    cnt_ref[...] = total


def _route(scores_t, router_bias):
    e, t = scores_t.shape
    tn = LANES
    bias = jnp.broadcast_to(router_bias.astype(F32)[:, None], (e, tn))
    tri = (jnp.arange(tn)[:, None] < jnp.arange(tn)[None, :]).astype(BF16)
    tok = pl.BlockSpec((TOP_K, tn), lambda i: (0, i))
    const = lambda i: (0, 0)
    return pl.pallas_call(
        _route_kernel,
        grid=(t // tn,),
        in_specs=[pl.BlockSpec((e, tn), lambda i: (0, i)), pl.BlockSpec((e, tn), const),
                  pl.BlockSpec((tn, tn), const)],
        out_specs=[tok, tok, tok, pl.BlockSpec((e, tn), const)],
        out_shape=[jax.ShapeDtypeStruct((TOP_K, t), jnp.int32), jax.ShapeDtypeStruct((TOP_K, t), F32),
                   jax.ShapeDtypeStruct((TOP_K, t), jnp.int32), jax.ShapeDtypeStruct((e, tn), F32)],
        scratch_shapes=[pltpu.VMEM((e, tn), F32)],
        compiler_params=_params(("arbitrary",)),
    )(scores_t, bias, tri)


def _slot_kernel(idx_ref, rk_ref, ps_ref, pos_ref):
    e, tn = ps_ref.shape
    ri = lax.broadcasted_iota(jnp.int32, (e, tn), 0)
    ps = ps_ref[...]
    rows = [jnp.sum(jnp.where(ri == idx_ref[k:k + 1, :], ps, 0.0), axis=0, keepdims=True) for k in range(TOP_K)]
    pos_ref[...] = rk_ref[...] + jnp.concatenate(rows, axis=0).astype(jnp.int32)


def _slots(idx, rank, pstart):
    k, t = idx.shape
    e = pstart.shape[0]
    tn = LANES
    tok = pl.BlockSpec((k, tn), lambda i: (0, i))
    return pl.pallas_call(
        _slot_kernel,
        grid=(t // tn,),
        in_specs=[tok, tok, pl.BlockSpec((e, tn), lambda i: (0, 0))],
        out_specs=tok,
        out_shape=jax.ShapeDtypeStruct((k, t), jnp.int32),
        compiler_params=_params(("arbitrary",)),
    )(idx, rank, jnp.broadcast_to(pstart.astype(F32)[:, None], (e, tn)))


def _sc_scatter_rows(vals, idx, n_rows, window=LANES):
    n, width = vals.shape
    mesh = plsc.VectorSubcoreMesh(core_axis_name="core", subcore_axis_name="subcore")

    @pl.kernel(out_type=jax.ShapeDtypeStruct((n_rows, width), vals.dtype), mesh=mesh, scratch_types=[])
    def scatter(v_hbm, i_hbm, o_hbm):
        def body(v_vmem, i_vmem):
            pltpu.sync_copy(v_vmem, o_hbm.at[i_vmem.at[0]])

        pltpu.emit_pipeline(
            body,
            grid=(n // window,),
            in_specs=[pl.BlockSpec((window, width), lambda i: (i, 0)),
                      pl.BlockSpec((1, window), lambda i: (0, i))],
            out_specs=[],
            core_axis_name=("core", "subcore"),
            dimension_semantics=(pltpu.PARALLEL,),
        )(v_hbm, i_hbm)

    return scatter(vals, idx.reshape(1, n))


def _block_table(counts, n_blocks):
    padded = (counts + EXPERT_ROWS - 1) // EXPERT_ROWS * EXPERT_ROWS
    pend = jnp.cumsum(padded)
    blk_expert = jnp.minimum(jnp.searchsorted(pend, jnp.arange(n_blocks) * EXPERT_ROWS, side='right'),
                             N_EXPERTS - 1).astype(jnp.int32)
    e_ids = jnp.arange(N_EXPERTS, dtype=jnp.int32)
    later = lax.cummin(jnp.where(counts > 0, e_ids, N_EXPERTS)[::-1])[::-1]
    next_tab = jnp.concatenate([later[1:], jnp.full((1,), N_EXPERTS, jnp.int32)])
    next_tab = jnp.where(next_tab < N_EXPERTS, next_tab, -1)
    return (pend - padded, blk_expert, (pend[-1] // EXPERT_ROWS).astype(jnp.int32).reshape(1),
            next_tab[blk_expert].astype(jnp.int32))


def _col_major(t):
    b, l, f = t.shape
    rows = l // GRID_W
    return t.reshape(b, rows, GRID_W, f).transpose(0, 2, 1, 3).reshape(b, l, f)


def _row_major(t):
    b, l, f = t.shape
    rows = l // GRID_W
    return t.reshape(b, GRID_W, rows, f).transpose(0, 2, 1, 3).reshape(b, l, f)


def kernel(x, c, ctx, c_ctx, w_ada, b_ada, norm1_g, norm2_g, w_in, gdn_conv_w, gdn_a_log, gdn_dt_bias, gdn_norm_g,
           ml_i_bias, ml_f_bias, ml_norm_g, w_branch_gdn, w_branch_ml, w_out, w_router, router_bias, w_exp_gate_up,
           w_exp_down, w_sh_gate_up, w_sh_down, final_norm_g):
    b, l, d = x.shape
    lc = ctx.shape[1]
    t = b * l
    layer = 0

    w = w_in[layer]
    main_cols = [_ORIG[k] for k in ("gdn_qkv", "gdn_z", "ml_q", "ml_k", "ml_v", "ml_o", "mg_gdn", "mg_ml")]
    w_main = jnp.concatenate([w[:, a:e] for a, e in main_cols], axis=1).astype(BF16)
    w_gate = jnp.concatenate([w[:, _ORIG["gdn_gate"][0]:_ORIG["gdn_gate"][1]],
                              w[:, _ORIG["ml_gate"][0]:_ORIG["ml_gate"][1]],
                              jnp.zeros((d, LANES - 64), F32)], axis=1).astype(BF16)
    zeros16 = jnp.zeros((16,), F32)
    gp_add = jnp.concatenate([zeros16, gdn_dt_bias[layer].reshape(-1), ml_i_bias[layer].reshape(-1),
                              ml_f_bias[layer].reshape(-1), jnp.zeros((LANES - 64,), F32)])
    gp_mul = jnp.concatenate([zeros16, -jnp.exp(gdn_a_log[layer].astype(F32)).reshape(-1),
                              jnp.zeros((LANES - 32,), F32)])
    gparams = jnp.zeros((8, LANES), F32).at[0].set(gp_add).at[1].set(gp_mul)
    conv_w8 = jnp.zeros((8, gdn_conv_w.shape[2]), F32).at[0:GDN_CONV].set(gdn_conv_w[layer])
    wr = w_router[layer].T
    wr_hi = wr.astype(BF16)
    wr_lo = (wr - wr_hi.astype(F32)).astype(BF16)

    n_mod_rows = -(-(b + 1) // 8) * 8
    cc = jnp.zeros((n_mod_rows, d), F32).at[0:b].set(c).at[b].set(c_ctx)
    mod = _ada_mod(cc, w_ada[layer], b_ada[layer])
    mod3 = mod.reshape(n_mod_rows, 1, 6 * d)

    x2d = x.reshape(t, d)
    tm_l = min(1024, l)
    proj_l, gate_l = _project(x2d, mod3, lambda i: (i * tm_l) // l, norm1_g[layer], w_main, w_gate, tm_l)
    tm_c = min(1024, b * lc)
    proj_c, gate_c = _project(ctx.reshape(b * lc, d), mod3, lambda i: b, norm1_g[layer], w_main, w_gate, tm_c)
    proj_l3 = proj_l.reshape(b, l, N_MAIN)
    proj_c3 = proj_c.reshape(b, lc, N_MAIN)

    gate_l_cm = _col_major(gate_l.reshape(b, l, LANES)).reshape(t, LANES)
    gd_c, ml_c = _gate_prep(gate_c, gparams)
    gd_l, _ = _gate_prep(gate_l, gparams)
    _, ml_l = _gate_prep(gate_l_cm, gparams)

    y_gdn = _gdn(proj_c3, proj_l3, conv_w8, gd_c.reshape(b, lc, LANES), gd_l.reshape(b, l, LANES),
                 gdn_norm_g[layer])
    q_cm = _col_major(proj_l3[:, :, COL_ML_Q:COL_ML_Q + HEADS * ML_DK])
    k_cm = _col_major(proj_l3[:, :, COL_ML_K:COL_ML_K + HEADS * ML_DK])
    v_cm = _col_major(proj_l3[:, :, COL_ML_V:COL_ML_V + HEADS * HEAD_V])
    h_ml = _row_major(_mlstm(proj_c3, q_cm, k_cm, v_cm, ml_c.reshape(b, lc, LANES), ml_l.reshape(b, l, LANES)))

    x1, h2, scores_t = _merge(y_gdn.reshape(t, d), h_ml.reshape(t, d), proj_l, x2d, mod3, l, ml_norm_g[layer],
                            norm2_g[layer], w_branch_gdn[layer].astype(BF16), w_branch_ml[layer].astype(BF16),
                            w_out[layer].astype(BF16), wr_hi, wr_lo, tm=min(512, l))

    tp = t // MOE_PARTS
    n_assign = tp * TOP_K
    n_blocks = (n_assign + N_EXPERTS * (EXPERT_ROWS - 1)) // EXPERT_ROWS + 1
    n_slots = n_blocks * EXPERT_ROWS
    w_sh_gu, w_sh_dn = w_sh_gate_up[layer].astype(BF16), w_sh_down[layer].astype(BF16)
    outs = []
    for part in range(MOE_PARTS):
        tok0 = part * tp
        idx, wts, rank, cnt = _route(scores_t[:, tok0:tok0 + tp], router_bias[layer])
        counts = cnt[:, 0].astype(jnp.int32)
        pstart, blk_expert, n_used, next_expert = _block_table(counts, n_blocks)
        pos = _slots(idx, rank, pstart).reshape(n_assign)
        tok_ids = jnp.broadcast_to((jnp.arange(n_assign, dtype=jnp.int32) % tp)[:, None], (n_assign, LANES))
        scattered = _sc_scatter_rows(tok_ids, pos, n_slots)[:, 0]
        in_expert = (jnp.arange(n_slots, dtype=jnp.int32).reshape(n_blocks, EXPERT_ROWS)
                     - pstart[blk_expert][:, None])
        valid = (in_expert < counts[blk_expert][:, None]).reshape(n_slots)
        tok_slot = jnp.where(valid, scattered, jnp.arange(n_slots, dtype=jnp.int32) % tp)
        xb = h2.at[tok_slot + tok0].get(mode="promise_in_bounds")
        yb = _experts(xb, blk_expert, n_used, next_expert, w_exp_gate_up[layer], w_exp_down[layer])
        yg = yb.at[pos].get(mode="promise_in_bounds", unique_indices=True).reshape(TOP_K, tp, d)
        outs.append(_final(x1, h2, yg, wts.T, mod3, l, w_sh_gu, w_sh_dn, final_norm_g, tok0, tm=min(256, l)))
    return jnp.concatenate(outs, axis=0).reshape(b, l, d)
```

```python
import functools
import math

import jax
import jax.numpy as jnp
from jax import lax
from jax.experimental import pallas as pl
from jax.experimental.pallas import tpu as pltpu
from jax.experimental.pallas import tpu_sc as plsc

F32 = jnp.float32
BF16 = jnp.bfloat16
HI = lax.Precision.HIGHEST

EPS = 1e-6
CHUNK = 64
GRID_W = 64
HEADS = 8
HEAD_V = 128
GDN_DK = 128
ML_DK = 64
GDN_CONV = 5
N_EXPERTS = 256
TOP_K = 8
N_GROUPS = 8
TOPK_GROUPS = 4
ROUTED_SCALE = 2.5
EXPERT_ROWS = 256
BATCH_PARTS = 2
GDN_STEPS = 4
GDN_CTX_STEPS = 4
RING = 2 * GDN_STEPS
LANES = 128
VMEM_LIMIT = 56 * 1024 * 1024

COL_GDN_QKV = 0
COL_GDN_Z = 3072
COL_ML_Q = 4096
COL_ML_K = 4608
COL_ML_V = 5120
COL_ML_O = 6144
COL_MG_GDN = 7168
COL_MG_ML = 8192
N_MAIN = 9216
_ORIG = dict(gdn_qkv=(0, 3072), gdn_z=(3072, 4096), gdn_gate=(4096, 4128), ml_q=(4128, 4640),
             ml_k=(4640, 5152), ml_v=(5152, 6176), ml_o=(6176, 7200), ml_gate=(7200, 7232),
             mg_gdn=(7232, 8256), mg_ml=(8256, 9280))


def _params(sem, vmem=VMEM_LIMIT):
    return pltpu.CompilerParams(dimension_semantics=sem, vmem_limit_bytes=vmem)


def _dot(a, b, precision=None):
    return jnp.dot(a, b, preferred_element_type=F32, precision=precision)


def _dot_nt(a, b, precision=None):
    return lax.dot_general(a, b, (((1,), (1,)), ((), ())), preferred_element_type=F32, precision=precision)


def _dot_tn(a, b, precision=None):
    return lax.dot_general(a, b, (((0,), (0,)), ((), ())), preferred_element_type=F32, precision=precision)


def _times3(a, b):
    ah = a.astype(BF16)
    al = (a - ah.astype(F32)).astype(BF16)
    bh = b.astype(BF16)
    bl = (b - bh.astype(F32)).astype(BF16)
    return _dot(ah, bh) + (_dot(ah, bl) + _dot(al, bh))


def _sigmoid(x):
    return 1.0 / (1.0 + jnp.exp(-x))


def _softplus(x):
    return jnp.maximum(x, 0.0) + jnp.log(1.0 + jnp.exp(-jnp.abs(x)))


def _ada_kernel(c_ref, w_ref, b_ref, o_ref):
    c = c_ref[...]
    sc = c * _sigmoid(c)
    o_ref[...] = _dot(sc, w_ref[...], HI) + b_ref[...]


def _ada_mod(cc, w_ada, b_ada, tn=1536):
    r, d = cc.shape
    n = w_ada.shape[1]
    return pl.pallas_call(
        _ada_kernel,
        grid=(n // tn,),
        in_specs=[pl.BlockSpec((r, d), lambda j: (0, 0)),
                  pl.BlockSpec((d, tn), lambda j: (0, j)),
                  pl.BlockSpec((1, tn), lambda j: (0, j))],
        out_specs=pl.BlockSpec((r, tn), lambda j: (0, j)),
        out_shape=jax.ShapeDtypeStruct((r, n), F32),
        compiler_params=_params(("arbitrary",)),
    )(cc, w_ada, b_ada.reshape(1, n))


def _proj_kernel(x_ref, mod_ref, g_ref, w_ref, wg_ref, o_ref, og_ref, hn_ref):
    d = x_ref.shape[1]

    @pl.when(pl.program_id(1) == 0)
    def _():
        x = x_ref[...]
        y = x * lax.rsqrt(jnp.mean(x * x, axis=-1, keepdims=True) + EPS) * g_ref[...]
        shift = mod_ref[0, :, 0:d]
        scale = mod_ref[0, :, d:2 * d]
        h = (y * (1.0 + scale) + shift).astype(BF16)
        hn_ref[...] = h
        og_ref[...] = _dot(h, wg_ref[...])

    o_ref[...] = _dot(hn_ref[...], w_ref[...]).astype(o_ref.dtype)


def _project(x2d, mod3, mod_row_of_tile, norm_g, w_main, w_gate, tm, tn=1024):
    t, d = x2d.shape
    n = w_main.shape[1]
    return pl.pallas_call(
        _proj_kernel,
        grid=(t // tm, n // tn),
        in_specs=[pl.BlockSpec((tm, d), lambda i, j: (i, 0)),
                  pl.BlockSpec((1, 1, mod3.shape[2]), lambda i, j: (mod_row_of_tile(i), 0, 0)),
                  pl.BlockSpec((1, d), lambda i, j: (0, 0)),
                  pl.BlockSpec((d, tn), lambda i, j: (0, j)),
                  pl.BlockSpec((d, LANES), lambda i, j: (0, 0))],
        out_specs=[pl.BlockSpec((tm, tn), lambda i, j: (i, j)),
                   pl.BlockSpec((tm, LANES), lambda i, j: (i, 0))],
        out_shape=[jax.ShapeDtypeStruct((t, n), BF16), jax.ShapeDtypeStruct((t, LANES), F32)],
        scratch_shapes=[pltpu.VMEM((tm, d), BF16)],
        compiler_params=_params(("arbitrary", "arbitrary")),
    )(x2d, mod3, norm_g.reshape(1, d), w_main, w_gate)


def _gate_kernel(g_ref, p_ref, gd_ref, ml_ref):
    rows = g_ref.shape[0]
    raw = g_ref[...] + p_ref[0:1, :]
    lane = lax.broadcasted_iota(jnp.int32, raw.shape, 1)
    sp = _softplus(raw)
    vals = jnp.where(lane < 16, _sigmoid(raw),
                     jnp.where(lane < 32, p_ref[1:2, :] * sp,
                               jnp.where(lane < 48, raw,
                                         jnp.where(lane < 64, -_softplus(-raw), 0.0))))
    ri = lax.broadcasted_iota(jnp.int32, (CHUNK, CHUNK), 0)
    ci = lax.broadcasted_iota(jnp.int32, (CHUNK, CHUNK), 1)
    tri_f = (ri >= ci).astype(F32)
    tri_b = (ri <= ci).astype(F32)
    lane_c = lax.broadcasted_iota(jnp.int32, (CHUNK, LANES), 1)
    row_c = lax.broadcasted_iota(jnp.int32, (CHUNK, LANES), 0)
    fwd_lane = (lane_c % 16) < 8
    for c in range(rows // CHUNK):
        blk = vals[c * CHUNK:(c + 1) * CHUNK, :]
        cum = jnp.where(fwd_lane, _dot(tri_f, blk, HI), _dot(tri_b, blk, HI))
        gd_ref[c * CHUNK:(c + 1) * CHUNK, :] = jnp.where(lane_c < 16, blk, jnp.where(lane_c < 32, cum, 0.0))
        bcum = pltpu.roll(cum, LANES - 16, axis=1)
        gmb = blk - bcum
        cmf, cmb = gmb, gmb
        for s in (1, 2, 4, 8, 16, 32):
            cmf = jnp.maximum(cmf, jnp.where(row_c >= s, pltpu.roll(cmf, s, axis=0), -jnp.inf))
            cmb = jnp.maximum(cmb, jnp.where(row_c < CHUNK - s, pltpu.roll(cmb, CHUNK - s, axis=0), -jnp.inf))
        cm = jnp.where(fwd_lane, cmf, cmb)
        ml = jnp.where(lane_c < 16, pltpu.roll(gmb, LANES - 32, axis=1),
                       jnp.where(lane_c < 32, pltpu.roll(cm, LANES - 16, axis=1),
                                 jnp.where(lane_c < 48, bcum, 0.0)))
        ml_ref[c * CHUNK:(c + 1) * CHUNK, :] = ml


def _gate_prep(graw, gparams, tm=256):
    t = graw.shape[0]
    spec = pl.BlockSpec((tm, LANES), lambda i: (i, 0))
    return pl.pallas_call(
        _gate_kernel,
        grid=(t // tm,),
        in_specs=[spec, pl.BlockSpec((8, LANES), lambda i: (0, 0))],
        out_specs=[spec, spec],
        out_shape=[jax.ShapeDtypeStruct((t, LANES), F32)] * 2,
        compiler_params=_params(("arbitrary",)),
    )(graw, gparams)


def _split3(a):
    h = a.astype(BF16)
    r = a - h.astype(F32)
    m = r.astype(BF16)
    return h, m, (r - m.astype(F32)).astype(BF16)


def _lane_picks(x, lanes):
    li = lax.broadcasted_iota(jnp.int32, (LANES, LANES), 0)
    ci = lax.broadcasted_iota(jnp.int32, (LANES, LANES), 1)
    want = jnp.full((LANES, LANES), -1, jnp.int32)
    for j, lane in enumerate(lanes):
        want = jnp.where(ci == j, lane, want)
    sel = (li == want).astype(BF16)
    h, m, lo = _split3(x)
    cols = _dot(h, sel) + (_dot(m, sel) + _dot(lo, sel))
    return [jnp.broadcast_to(cols[:, j:j + 1], x.shape) for j in range(len(lanes))]


def _dir_masks(direction):
    ri = lax.broadcasted_iota(jnp.int32, (CHUNK, CHUNK), 0)
    ci = lax.broadcasted_iota(jnp.int32, (CHUNK, CHUNK), 1)
    if direction == 0:
        return ri >= ci, ri > ci
    return ri <= ci, ri < ci


def _gdn_kernel(qc_ref, kc_ref, vc_ref, ql_ref, kl_ref, vl_ref, z_ref, cwq_ref, cwk_ref, cwv_ref,
                gdc_ref, gdl_ref, ng_ref, y_ref,
                xpad, qs, ks, vs, beta_t, cg_t, wq_r, u_r, kd_r, qk_r, dc_r, out_s):
    lc = qc_ref.shape[1]
    ll = ql_ref.shape[1]
    lt = lc + ll
    n_c, n_l = lc // CHUNK, ll // CHUNK
    n_t = n_c + n_l
    rb = 256

    def l2n(x):
        return x * lax.rsqrt(jnp.sum(x * x, axis=-1, keepdims=True) + EPS)

    def prep(src_ref, cw_ref, dst, off, ls, kind):
        xpad[0:8, :] = jnp.zeros((8, LANES), F32)
        xpad[8:8 + ls, :] = src_ref[0].astype(F32)
        xpad[8 + ls:16 + ls, :] = jnp.zeros((8, LANES), F32)
        step = min(rb, ls)
        for r0 in range(0, ls, step):
            acc = jnp.zeros((step, LANES), F32)
            for t in range(GDN_CONV):
                s0 = r0 + 8 - GDN_CONV // 2 + t
                acc = acc + cw_ref[t:t + 1, :] * xpad[s0:s0 + step, :]
            y = acc * _sigmoid(acc)
            if kind == "q":
                y = l2n(y) * (GDN_DK ** -0.5)
            elif kind == "k":
                y = l2n(y)
            dst[off + r0:off + r0 + step, :] = y

    prep(qc_ref, cwq_ref, qs, 0, lc, "q")
    prep(kc_ref, cwk_ref, ks, 0, lc, "k")
    prep(vc_ref, cwv_ref, vs, 0, lc, "v")
    prep(ql_ref, cwq_ref, qs, lc, ll, "q")
    prep(kl_ref, cwk_ref, ks, lc, ll, "k")
    prep(vl_ref, cwv_ref, vs, lc, ll, "v")

    head = pl.program_id(1)

    def build_tables(src_ref, off, ls):
        step = min(rb, ls)
        for r0 in range(0, ls, step):
            picked = _lane_picks(src_ref[0, r0:r0 + step, :], [8 * d + head for d in range(2)]
                                 + [16 + 8 * d + head for d in range(2)])
            for d in range(2):
                beta_t[d, off + r0:off + r0 + step, :] = picked[d]
                cg_t[d, off + r0:off + r0 + step, :] = picked[2 + d]

    build_tables(gdc_ref, 0, lc)
    build_tables(gdl_ref, lc, ll)

    def bwd_chunk(t):
        return jnp.where(t < n_c, n_c - 1 - t, n_t + n_c - 1 - t)

    row_p = lax.broadcasted_iota(jnp.int32, (CHUNK, LANES), 0)
    lane_p = lax.broadcasted_iota(jnp.int32, (CHUNK, LANES), 1)
    fwd_p = lane_p < CHUNK
    col_p = jnp.where(fwd_p, lane_p, lane_p - CHUNK)
    signed = jnp.where(fwd_p, row_p - col_p, col_p - row_p)
    incl_p = signed >= 0
    strict_p = signed > 0
    eye_p = (row_p == col_p).astype(F32)
    keep_f = fwd_p.astype(BF16)
    keep_b = (1.0 - fwd_p.astype(F32)).astype(BF16)

    def block_diag(top, bottom):
        zero = jnp.zeros(top.shape, top.dtype)
        return jnp.concatenate([jnp.concatenate([top, zero], axis=1), jnp.concatenate([zero, bottom], axis=1)], axis=0)

    def pair_diag(xp):
        return jnp.concatenate([xp * keep_f, xp * keep_b], axis=0)

    def pair_times3(a, xp):
        ah = a.astype(BF16)
        al = (a - ah.astype(F32)).astype(BF16)
        xh = xp.astype(BF16)
        xl = (xp - xh.astype(F32)).astype(BF16)
        dh = pair_diag(xh)
        return _dot(ah, dh) + (_dot(ah, pair_diag(xl)) + _dot(al, dh))

    def prep_chain(t):
        tc = jnp.minimum(t, n_t - 1)
        q, k, v, beta, cgc, kb, ecg = [], [], [], [], [], [], []
        for d in range(2):
            c = tc if d == 0 else bwd_chunk(tc)
            rows = pl.ds(pl.multiple_of(c * CHUNK, CHUNK), CHUNK)
            q.append(qs[rows, :])
            k.append(ks[rows, :])
            v.append(vs[rows, :])
            beta.append(beta_t[d, rows, :])
            cgc.append(cg_t[d, rows, :])
        kdiag = block_diag(k[0].astype(BF16), k[1].astype(BF16))
        kk = _dot_nt(jnp.concatenate([k[0], k[1]], axis=1).astype(BF16), kdiag)
        qk = _dot_nt(jnp.concatenate([q[0], q[1]], axis=1).astype(BF16), kdiag)
        yield
        beta_p = jnp.where(fwd_p, beta[0], beta[1])
        cg_p = jnp.where(fwd_p, cgc[0], cgc[1])
        cgr_p = jnp.transpose(jnp.concatenate([cgc[0], cgc[1]], axis=0))[0:CHUNK, :]
        decay = jnp.exp(jnp.where(incl_p, cg_p - cgr_p, -jnp.inf))
        qkd = qk * decay
        for d in range(2):
            slot = (t % RING) * 2 + d
            last = CHUNK - 1 if d == 0 else 0
            cg_last = cgc[d][last:last + 1, :]
            ecg.append(jnp.exp(cgc[d]))
            kb.append(k[d] * beta[d])
            qk_r[slot] = qkd[:, d * CHUNK:(d + 1) * CHUNK]
            wq_r[slot, CHUNK:2 * CHUNK, :] = q[d] * ecg[d]
            kd_r[slot] = k[d] * jnp.exp(cg_last - cgc[d])
            dc_r[slot] = jnp.broadcast_to(jnp.exp(cg_last), (8, LANES))
        x = jnp.where(strict_p, -(beta_p * kk) * decay, 0.0)
        tinv = eye_p + x
        x = pair_times3(x, x)
        yield
        for _ in range(4):
            both = pair_times3(jnp.concatenate([tinv, x], axis=0), x)
            tinv, x = tinv + both[0:CHUNK, :], both[CHUNK:2 * CHUNK, :]
            yield
        tinv = (tinv + pair_times3(tinv, x)).astype(BF16)
        yield
        w = _dot(tinv, block_diag((kb[0] * ecg[0]).astype(BF16), (kb[1] * ecg[1]).astype(BF16)))
        u = _dot(tinv, block_diag((v[0] * beta[0]).astype(BF16), (v[1] * beta[1]).astype(BF16)))
        for d in range(2):
            slot = (t % RING) * 2 + d
            wq_r[slot, 0:CHUNK, :] = w[:, d * LANES:(d + 1) * LANES]
            u_r[slot] = u[:, d * LANES:(d + 1) * LANES]

    out_s[...] = jnp.zeros(out_s.shape, F32)

    def scan_chain(d, t0, steps, s, with_out, result):
        for j in range(steps):
            t = t0 + j
            slot = (t % RING) * 2 + d
            ws = _dot(wq_r[slot], s)
            yield
            v_new = u_r[slot] - ws[0:CHUNK, :]
            if with_out:
                c = t if d == 0 else bwd_chunk(t)
                o = ws[CHUNK:2 * CHUNK, :] + _dot(qk_r[slot], v_new)
                l0 = pl.multiple_of((c - n_c) * CHUNK, CHUNK)
                out_s[pl.ds(l0, CHUNK), :] += o
            s = s * dc_r[slot][0:1, :] + _dot_tn(kd_r[slot], v_new)
            yield
        result[d] = s

    def lockstep(chains):
        chains = list(chains)
        while chains:
            alive = []
            for ch in chains:
                try:
                    next(ch)
                    alive.append(ch)
                except StopIteration:
                    pass
            chains = alive

    def group_body(i, carry, t_base, steps, ahead, with_out):
        t0 = t_base + steps * i
        result = [None, None]
        lockstep([scan_chain(d, t0, steps, carry[d], with_out, result) for d in range(2)]
                 + [prep_chain(t0 + steps + j) for j in range(ahead)])
        return result[0], result[1]

    def run_groups(carry, t_base, n_steps, steps, ahead_last, with_out):
        n_groups = n_steps // steps
        carry = lax.fori_loop(0, n_groups - 1, functools.partial(
            group_body, t_base=t_base, steps=steps, ahead=steps, with_out=with_out), carry)
        return group_body(n_groups - 1, carry, t_base, steps, ahead_last, with_out)

    lockstep([prep_chain(j) for j in range(GDN_CTX_STEPS)])
    s0 = jnp.zeros((GDN_DK, HEAD_V), F32)
    carry = run_groups((s0, s0), 0, n_c, GDN_CTX_STEPS, GDN_STEPS, False)
    run_groups(carry, n_c, n_l, GDN_STEPS, 0, True)


    def out_body(i, carry):
        r0 = pl.multiple_of(i * rb, rb)
        o = out_s[pl.ds(r0, rb), :]
        z = z_ref[0, pl.ds(r0, rb), :].astype(F32)
        y = o * lax.rsqrt(jnp.mean(o * o, axis=-1, keepdims=True) + EPS) * ng_ref[...]
        y_ref[0, pl.ds(r0, rb), :] = (y * (z * _sigmoid(z))).astype(y_ref.dtype)
        return carry

    lax.fori_loop(0, ll // rb, out_body, 0)


def _gdn(proj_c, proj_l, conv_w8, gd_c, gd_l, norm_g, b0, nb):
    lc = proj_c.shape[1]
    ll = proj_l.shape[1]
    lt = lc + ll
    qb, kb_, vb, zb = (COL_GDN_QKV // LANES, COL_GDN_QKV // LANES + HEADS, COL_GDN_QKV // LANES + 2 * HEADS,
                       COL_GDN_Z // LANES)

    def seq_spec(l, col0):
        return pl.BlockSpec((1, l, LANES), lambda i, h: (i + b0, 0, col0 + h))

    def cw_spec(col0):
        return pl.BlockSpec((8, LANES), lambda i, h: (0, col0 + h))

    return pl.pallas_call(
        _gdn_kernel,
        grid=(nb, HEADS),
        in_specs=[seq_spec(lc, qb), seq_spec(lc, kb_), seq_spec(lc, vb),
                  seq_spec(ll, qb), seq_spec(ll, kb_), seq_spec(ll, vb), seq_spec(ll, zb),
                  cw_spec(0), cw_spec(HEADS), cw_spec(2 * HEADS),
                  pl.BlockSpec((1, lc, LANES), lambda i, h: (i + b0, 0, 0)),
                  pl.BlockSpec((1, ll, LANES), lambda i, h: (i + b0, 0, 0)),
                  pl.BlockSpec((1, LANES), lambda i, h: (0, 0))],
        out_specs=pl.BlockSpec((1, ll, LANES), lambda i, h: (i, 0, h)),
        out_shape=jax.ShapeDtypeStruct((nb, ll, HEADS * HEAD_V), BF16),
        scratch_shapes=[pltpu.VMEM((max(lc, ll) + 16, LANES), F32),
                        pltpu.VMEM((lt, LANES), F32), pltpu.VMEM((lt, LANES), F32), pltpu.VMEM((lt, LANES), F32),
                        pltpu.VMEM((2, lt, LANES), F32), pltpu.VMEM((2, lt, LANES), F32),
                        pltpu.VMEM((2 * RING, 2 * CHUNK, LANES), F32),
                        pltpu.VMEM((2 * RING, CHUNK, LANES), F32), pltpu.VMEM((2 * RING, CHUNK, LANES), F32),
                        pltpu.VMEM((2 * RING, CHUNK, CHUNK), F32),
                        pltpu.VMEM((2 * RING, 8, LANES), F32),
                        pltpu.VMEM((ll, LANES), F32)],
        compiler_params=_params(("arbitrary", "arbitrary")),
    )(proj_c, proj_c, proj_c, proj_l, proj_l, proj_l, proj_l, conv_w8, conv_w8, conv_w8,
      gd_c, gd_l, norm_g.reshape(1, LANES))


def _mlstm_kernel(qc_ref, kc_ref, vc_ref, ql_ref, kl_ref, vl_ref, mlc_ref, mll_ref, h_ref, out_s, tabs):
    lc = qc_ref.shape[1]
    ll = ql_ref.shape[1]
    n_c, n_l = lc // CHUNK, ll // CHUNK
    n_t = n_c + n_l
    lt = lc + ll
    pair = pl.program_id(1)
    lane = lax.broadcasted_iota(jnp.int32, (CHUNK, LANES), 1)
    ones_v = jnp.ones((CHUNK, HEAD_V), BF16)
    chains = [(hh, d) for hh in range(2) for d in range(2)]
    hmask = [((lane // ML_DK) == hh).astype(F32) for hh in range(2)]
    incl = [_dir_masks(d)[0] for d in range(2)]

    def build_tables(src_ref, off, ls):
        step = min(256, ls)
        for r0 in range(0, ls, step):
            lanes = [16 * j + 8 * d + 2 * pair + hh for hh, d in chains for j in range(3)]
            for g, tab in enumerate(_lane_picks(src_ref[0, r0:r0 + step, :], lanes)):
                tabs[g, off + r0:off + r0 + step, :] = tab

    build_tables(mlc_ref, 0, lc)
    build_tables(mll_ref, lc, ll)

    def wide(a):
        return jnp.concatenate([a, a], axis=1)

    def chain(hh, d, c, state, is_ctx, result):
        cs, ms = state
        last = CHUNK - 1 if d == 0 else 0
        if is_ctx:
            rows = pl.ds(pl.multiple_of(c * CHUNK, CHUNK), CHUNK)
            q_ref, k_ref, v_ref = qc_ref, kc_ref, vc_ref
        else:
            rows = pl.ds(pl.multiple_of((c - n_c) * CHUNK, CHUNK), CHUNK)
            q_ref, k_ref, v_ref = ql_ref, kl_ref, vl_ref
        q = (q_ref[0, rows, :].astype(F32) * hmask[hh]).astype(BF16)
        k = k_ref[0, rows, :].astype(F32) * (hmask[hh] * (ML_DK ** -0.5))
        v = jnp.concatenate([v_ref[0, rows, hh * HEAD_V:(hh + 1) * HEAD_V], ones_v], axis=1)
        n = chains.index((hh, d))
        trows = pl.ds(pl.multiple_of(c * CHUNK, CHUNK), CHUNK)
        gmb = tabs[3 * n, trows, :]
        gmb_t = jnp.transpose(gmb)[0:CHUNK, :]
        cm = tabs[3 * n + 1, trows, :]
        bc = tabs[3 * n + 2, trows, :]
        qk = _dot_nt(q, k.astype(BF16))
        yield
        cm_last = cm[last:last + 1, :]
        b_last = bc[last:last + 1, :]
        mm = jnp.maximum(ms, cm)
        p = jnp.where(incl[d], jnp.exp(gmb_t - mm[:, 0:CHUNK]), 0.0) * qk
        wk = (k * jnp.exp(gmb - cm_last)).astype(BF16)
        inter = _dot(q, cs.astype(BF16))
        intra = _dot(p.astype(BF16), v)
        c_loc = _dot_tn(wk, v)
        yield
        if not is_ctx:
            nd = wide(jnp.exp(ms - mm)) * inter + intra
            hv = nd[:, 0:HEAD_V] / jnp.maximum(jnp.abs(nd[:, HEAD_V:2 * HEAD_V]), jnp.exp(-(bc + mm)))
            l0 = pl.multiple_of((c - n_c) * CHUNK, CHUNK)
            out_s[pl.ds(l0, CHUNK), hh * HEAD_V:(hh + 1) * HEAD_V] += hv
        mx = jnp.maximum(ms, cm_last)
        result[hh, d] = (wide(jnp.exp(ms - mx)) * cs + wide(jnp.exp(cm_last - mx)) * c_loc, b_last + mx)

    out_s[...] = jnp.zeros(out_s.shape, F32)

    def run(chains_iter):
        live = list(chains_iter)
        while live:
            alive = []
            for ch in live:
                try:
                    next(ch)
                    alive.append(ch)
                except StopIteration:
                    pass
            live = alive

    def body(i, carry, is_ctx):
        result = {}
        gens = []
        for n, (hh, d) in enumerate(chains):
            if is_ctx:
                c = i if d == 0 else n_c - 1 - i
            else:
                c = n_c + i if d == 0 else n_t - 1 - i
            gens.append(chain(hh, d, c, carry[n], is_ctx, result))
        run(gens)
        return tuple(result[hd] for hd in chains)

    st0 = (jnp.zeros((LANES, 2 * HEAD_V), F32), jnp.zeros((1, LANES), F32))
    carry = lax.fori_loop(0, n_c, functools.partial(body, is_ctx=True), (st0,) * 4)
    lax.fori_loop(0, n_l, functools.partial(body, is_ctx=False), carry)
    h_ref[0] = out_s[...].astype(h_ref.dtype)


def _mlstm(proj_c, q_l, k_l, v_l, ml_c, ml_l, b0):
    lc = proj_c.shape[1]
    nb, ll, _ = q_l.shape
    lt = lc + ll
    qb, kb_, vb = COL_ML_Q // LANES, COL_ML_K // LANES, COL_ML_V // (2 * HEAD_V)
    return pl.pallas_call(
        _mlstm_kernel,
        grid=(nb, HEADS // 2),
        in_specs=[pl.BlockSpec((1, lc, LANES), lambda i, p: (i + b0, 0, qb + p)),
                  pl.BlockSpec((1, lc, LANES), lambda i, p: (i + b0, 0, kb_ + p)),
                  pl.BlockSpec((1, lc, 2 * HEAD_V), lambda i, p: (i + b0, 0, vb + p)),
                  pl.BlockSpec((1, ll, LANES), lambda i, p: (i, 0, p)),
                  pl.BlockSpec((1, ll, LANES), lambda i, p: (i, 0, p)),
                  pl.BlockSpec((1, ll, 2 * HEAD_V), lambda i, p: (i, 0, p)),
                  pl.BlockSpec((1, lc, LANES), lambda i, p: (i + b0, 0, 0)),
                  pl.BlockSpec((1, ll, LANES), lambda i, p: (i + b0, 0, 0))],
        out_specs=pl.BlockSpec((1, ll, 2 * HEAD_V), lambda i, p: (i, 0, p)),
        out_shape=jax.ShapeDtypeStruct((nb, ll, HEADS * HEAD_V), BF16),
        scratch_shapes=[pltpu.VMEM((ll, 2 * HEAD_V), F32), pltpu.VMEM((12, lt, LANES), F32)],
        compiler_params=_params(("arbitrary", "arbitrary")),
    )(proj_c, proj_c, proj_c, q_l, k_l, v_l, ml_c, ml_l)


def _merge_kernel(yg_ref, hm_ref, o_ref, gg_ref, gm_ref, x_ref, mod_ref, mlg_ref, n2_ref,
                  wbg_ref, wbm_ref, wo_ref, wrh_ref, wrl_ref, x1_ref, h2_ref, sc_ref):
    d = x_ref.shape[1]
    o = o_ref[...].astype(F32)
    ym = _sigmoid(o) * hm_ref[...].astype(F32)
    segs = []
    for h in range(HEADS):
        seg = ym[:, h * HEAD_V:(h + 1) * HEAD_V]
        segs.append(seg * lax.rsqrt(jnp.mean(seg * seg, axis=-1, keepdims=True) + EPS))
    ymn = jnp.concatenate(segs, axis=1) * mlg_ref[...]
    y_gdn = _dot(yg_ref[...], wbg_ref[...])
    y_ml = _dot(ymn.astype(BF16), wbm_ref[...])
    mixed = _sigmoid(gg_ref[...].astype(F32)) * y_gdn + _sigmoid(gm_ref[...].astype(F32)) * y_ml
    y = _dot(mixed.astype(BF16), wo_ref[...])
    x1 = x_ref[...] + mod_ref[0, :, 2 * d:3 * d] * y
    x1_ref[...] = x1
    hn = x1 * lax.rsqrt(jnp.mean(x1 * x1, axis=-1, keepdims=True) + EPS) * n2_ref[...]
    h2 = hn * (1.0 + mod_ref[0, :, 4 * d:5 * d]) + mod_ref[0, :, 3 * d:4 * d]
    h2_hi = h2.astype(BF16)
    h2_ref[...] = h2_hi
    h2_lo = (h2 - h2_hi.astype(F32)).astype(BF16)
    logits = _dot_nt(wrh_ref[...], h2_hi) + (_dot_nt(wrl_ref[...], h2_hi) + _dot_nt(wrh_ref[...], h2_lo))
    sc_ref[...] = _sigmoid(logits)


def _merge(y_gdn, h_ml, proj_l2d, x2d, mod3, rows_per_mod, ml_norm_g, norm2_g, wbg, wbm, wo, wr_hi, wr_lo, tok0,
           tm=512):
    t, d = y_gdn.shape
    e = wr_hi.shape[0]
    off = tok0 // tm
    row = lambda i: (i, 0)
    const = lambda i: (0, 0)
    return pl.pallas_call(
        _merge_kernel,
        grid=(t // tm,),
        in_specs=[pl.BlockSpec((tm, d), row), pl.BlockSpec((tm, d), row),
                  pl.BlockSpec((tm, d), lambda i: (i + off, COL_ML_O // d)),
                  pl.BlockSpec((tm, d), lambda i: (i + off, COL_MG_GDN // d)),
                  pl.BlockSpec((tm, d), lambda i: (i + off, COL_MG_ML // d)),
                  pl.BlockSpec((tm, d), lambda i: (i + off, 0)),
                  pl.BlockSpec((1, 1, mod3.shape[2]), lambda i: (((i + off) * tm) // rows_per_mod, 0, 0)),
                  pl.BlockSpec((1, d), const), pl.BlockSpec((1, d), const),
                  pl.BlockSpec((d, d), const), pl.BlockSpec((d, d), const), pl.BlockSpec((d, d), const),
                  pl.BlockSpec((e, d), const), pl.BlockSpec((e, d), const)],
        out_specs=[pl.BlockSpec((tm, d), row), pl.BlockSpec((tm, d), row), pl.BlockSpec((e, tm), lambda i: (0, i))],
        out_shape=[jax.ShapeDtypeStruct((t, d), F32), jax.ShapeDtypeStruct((t, d), BF16),
                   jax.ShapeDtypeStruct((e, t), F32)],
        compiler_params=_params(("arbitrary",)),
    )(y_gdn, h_ml, proj_l2d, proj_l2d, proj_l2d, x2d, mod3, ml_norm_g.reshape(1, d), norm2_g.reshape(1, d),
      wbg, wbm, wo, wr_hi, wr_lo)


def _expert_kernel(be_ref, nu_ref, nx_ref, x_ref, wgu_hbm, wd_hbm, y_ref, land_gu, land_d, wgu_s, wd_s, sems):
    i = pl.program_id(0)
    de = wd_hbm.shape[1]
    used = i < nu_ref[0]
    first = jnp.logical_or(i == 0, be_ref[i] != be_ref[jnp.maximum(i - 1, 0)])

    def weight_copies(ex):
        return (pltpu.make_async_copy(wgu_hbm.at[ex], land_gu, sems.at[0]),
                pltpu.make_async_copy(wd_hbm.at[ex], land_d, sems.at[1]))

    @pl.when(i == 0)
    def _():
        for cp in weight_copies(be_ref[0]):
            cp.start()

    @pl.when(jnp.logical_and(first, used))
    def _():
        for cp in weight_copies(be_ref[i]):
            cp.wait()
        wgu_s[...] = land_gu[...].astype(BF16)
        wd_s[...] = land_d[...].astype(BF16)

        @pl.when(nx_ref[i] >= 0)
        def _():
            for cp in weight_copies(nx_ref[i]):
                cp.start()

    @pl.when(used)
    def _():
        gu = _dot(x_ref[...], wgu_s[...])
        g = gu[:, 0:de]
        act = (g * _sigmoid(g)) * gu[:, de:2 * de]
        y_ref[...] = _dot(act.astype(BF16), wd_s[...]).astype(y_ref.dtype)

    @pl.when(jnp.logical_not(used))
    def _():
        y_ref[...] = jnp.zeros(y_ref.shape, y_ref.dtype)


def _experts(xb, blk_expert, n_used, next_expert, w_gu, w_down):
    n_slots, d = xb.shape
    n_blocks = n_slots // EXPERT_ROWS
    e, _, de2 = w_gu.shape
    de = de2 // 2
    return pl.pallas_call(
        _expert_kernel,
        grid_spec=pltpu.PrefetchScalarGridSpec(
            num_scalar_prefetch=3,
            grid=(n_blocks,),
            in_specs=[pl.BlockSpec((EXPERT_ROWS, d), lambda i, be, nu, nx: (i, 0)),
                      pl.BlockSpec(memory_space=pl.ANY), pl.BlockSpec(memory_space=pl.ANY)],
            out_specs=pl.BlockSpec((EXPERT_ROWS, d), lambda i, be, nu, nx: (i, 0)),
            scratch_shapes=[pltpu.VMEM((d, de2), w_gu.dtype), pltpu.VMEM((de, d), w_down.dtype),
                            pltpu.VMEM((d, de2), BF16), pltpu.VMEM((de, d), BF16),
                            pltpu.SemaphoreType.DMA((2,))]),
        out_shape=jax.ShapeDtypeStruct((n_slots, d), BF16),
        compiler_params=_params(("arbitrary",)),
    )(blk_expert, n_used, next_expert, xb, w_gu, w_down)


def _final_kernel(x1_ref, h2_ref, yg_ref, wt_ref, mod_ref, wsg_ref, wsd_ref, fg_ref, o_ref):
    d = x1_ref.shape[1]
    ds_ = wsd_ref.shape[0]
    wt = wt_ref[...]
    routed = jnp.zeros(x1_ref.shape, F32)
    for k in range(TOP_K):
        routed = routed + wt[:, k:k + 1] * yg_ref[k].astype(F32)
    gu = _dot(h2_ref[...], wsg_ref[...])
    g = gu[:, 0:ds_]
    sh = _dot(((g * _sigmoid(g)) * gu[:, ds_:2 * ds_]).astype(BF16), wsd_ref[...])
    x2 = x1_ref[...] + mod_ref[0, :, 5 * d:6 * d] * (routed + sh)
    o_ref[...] = x2 * lax.rsqrt(jnp.mean(x2 * x2, axis=-1, keepdims=True) + EPS) * fg_ref[...]


def _final(x1, h2, yg, wts, mod3, rows_per_mod, w_sh_gu, w_sh_down, final_g, tok0, tm=256):
    tp, d = x1.shape
    off = tok0 // tm
    row = lambda i: (i, 0)
    const = lambda i: (0, 0)
    return pl.pallas_call(
        _final_kernel,
        grid=(tp // tm,),
        in_specs=[pl.BlockSpec((tm, d), row), pl.BlockSpec((tm, d), row),
                  pl.BlockSpec((TOP_K, tm, d), lambda i: (0, i, 0)), pl.BlockSpec((tm, TOP_K), row),
                  pl.BlockSpec((1, 1, mod3.shape[2]), lambda i: (((i + off) * tm) // rows_per_mod, 0, 0)),
                  pl.BlockSpec(w_sh_gu.shape, const), pl.BlockSpec(w_sh_down.shape, const),
                  pl.BlockSpec((1, d), const)],
        out_specs=pl.BlockSpec((tm, d), row),
        out_shape=jax.ShapeDtypeStruct((tp, d), F32),
        compiler_params=_params(("arbitrary",)),
    )(x1, h2, yg, wts, mod3, w_sh_gu, w_sh_down, final_g.reshape(1, d))


def _route_kernel(sc_ref, bias_ref, tri_ref, idx_ref, wt_ref, rk_ref, cnt_ref, base_s):
    @pl.when(pl.program_id(0) == 0)
    def _():
        base_s[...] = jnp.zeros(base_s.shape, F32)

    scores = sc_ref[...]
    e, tn = scores.shape
    gsz = e // N_GROUPS
    sel3 = (scores + bias_ref[...]).reshape(N_GROUPS, gsz, tn)
    m1 = jnp.max(sel3, axis=1)
    is_max = sel3 == m1[:, None, :]
    n_max = jnp.sum(is_max.astype(F32), axis=1)
    m2 = jnp.max(jnp.where(is_max, -jnp.inf, sel3), axis=1)
    grp = m1 + jnp.where(n_max >= 2.0, m1, m2)
    gi = lax.broadcasted_iota(jnp.int32, (N_GROUPS, tn), 0)
    ahead = jnp.zeros((N_GROUPS, tn), F32)
    for g in range(N_GROUPS):
        row = grp[g:g + 1, :]
        ahead = ahead + jnp.logical_or(row > grp, jnp.logical_and(row == grp, g < gi)).astype(F32)
    ahead3 = jnp.broadcast_to(ahead[:, None, :], (N_GROUPS, gsz, tn))
    selm = jnp.where(ahead3 < float(TOPK_GROUPS), sel3, -jnp.inf).reshape(e, tn)
    ri = lax.broadcasted_iota(jnp.int32, (e, tn), 0).astype(F32)
    member = jnp.zeros((e, tn), F32)
    idxs, ws = [], []
    for _ in range(TOP_K):
        m = jnp.max(selm, axis=0, keepdims=True)
        idx = jnp.min(jnp.where(selm == m, ri, float(e)), axis=0, keepdims=True)
        hit = ri == idx
        ws.append(jnp.sum(jnp.where(hit, scores, 0.0), axis=0, keepdims=True))
        idxs.append(idx)
        selm = jnp.where(hit, -jnp.inf, selm)
        member = jnp.where(hit, 1.0, member)
    w = jnp.concatenate(ws, axis=0)
    wt_ref[...] = w / jnp.sum(w, axis=0, keepdims=True) * ROUTED_SCALE
    idx_ref[...] = jnp.concatenate(idxs, axis=0).astype(jnp.int32)
    cum = _dot(member.astype(BF16), tri_ref[...]) + base_s[...]
    rk_ref[...] = jnp.concatenate(
        [jnp.sum(jnp.where(ri == idx, cum, 0.0), axis=0, keepdims=True) for idx in idxs], axis=0).astype(jnp.int32)
    total = base_s[...] + jnp.sum(member, axis=1, keepdims=True)
    base_s[...] = total
    cnt_ref[...] = total


def _route(scores_t, router_bias):
    e, t = scores_t.shape
    tn = LANES
    bias = jnp.broadcast_to(router_bias.astype(F32)[:, None], (e, tn))
    tri = (jnp.arange(tn)[:, None] < jnp.arange(tn)[None, :]).astype(BF16)
    tok = pl.BlockSpec((TOP_K, tn), lambda i: (0, i))
    const = lambda i: (0, 0)
    return pl.pallas_call(
        _route_kernel,
        grid=(t // tn,),
        in_specs=[pl.BlockSpec((e, tn), lambda i: (0, i)), pl.BlockSpec((e, tn), const),
                  pl.BlockSpec((tn, tn), const)],
        out_specs=[tok, tok, tok, pl.BlockSpec((e, tn), const)],
        out_shape=[jax.ShapeDtypeStruct((TOP_K, t), jnp.int32), jax.ShapeDtypeStruct((TOP_K, t), F32),
                   jax.ShapeDtypeStruct((TOP_K, t), jnp.int32), jax.ShapeDtypeStruct((e, tn), F32)],
        scratch_shapes=[pltpu.VMEM((e, tn), F32)],
        compiler_params=_params(("arbitrary",)),
    )(scores_t, bias, tri)


def _slot_kernel(idx_ref, rk_ref, ps_ref, pos_ref):
    e, tn = ps_ref.shape
    ri = lax.broadcasted_iota(jnp.int32, (e, tn), 0)
    ps = ps_ref[...]
    rows = [jnp.sum(jnp.where(ri == idx_ref[k:k + 1, :], ps, 0.0), axis=0, keepdims=True) for k in range(TOP_K)]
    pos_ref[...] = rk_ref[...] + jnp.concatenate(rows, axis=0).astype(jnp.int32)


def _slots(idx, rank, pstart):
    k, t = idx.shape
    e = pstart.shape[0]
    tn = LANES
    tok = pl.BlockSpec((k, tn), lambda i: (0, i))
    return pl.pallas_call(
        _slot_kernel,
        grid=(t // tn,),
        in_specs=[tok, tok, pl.BlockSpec((e, tn), lambda i: (0, 0))],
        out_specs=tok,
        out_shape=jax.ShapeDtypeStruct((k, t), jnp.int32),
        compiler_params=_params(("arbitrary",)),
    )(idx, rank, jnp.broadcast_to(pstart.astype(F32)[:, None], (e, tn)))


def _sc_scatter_rows(vals, idx, n_rows, window=LANES):
    n, width = vals.shape
    mesh = plsc.VectorSubcoreMesh(core_axis_name="core", subcore_axis_name="subcore")

    @pl.kernel(out_type=jax.ShapeDtypeStruct((n_rows, width), vals.dtype), mesh=mesh, scratch_types=[])
    def scatter(v_hbm, i_hbm, o_hbm):
        def body(v_vmem, i_vmem):
            pltpu.sync_copy(v_vmem, o_hbm.at[i_vmem.at[0]])

        pltpu.emit_pipeline(
            body,
            grid=(n // window,),
            in_specs=[pl.BlockSpec((window, width), lambda i: (i, 0)),
                      pl.BlockSpec((1, window), lambda i: (0, i))],
            out_specs=[],
            core_axis_name=("core", "subcore"),
            dimension_semantics=(pltpu.PARALLEL,),
        )(v_hbm, i_hbm)

    return scatter(vals, idx.reshape(1, n))


def _block_table(counts, n_blocks):
    padded = (counts + EXPERT_ROWS - 1) // EXPERT_ROWS * EXPERT_ROWS
    pend = jnp.cumsum(padded)
    blk_expert = jnp.minimum(jnp.searchsorted(pend, jnp.arange(n_blocks) * EXPERT_ROWS, side='right'),
                             N_EXPERTS - 1).astype(jnp.int32)
    e_ids = jnp.arange(N_EXPERTS, dtype=jnp.int32)
    later = lax.cummin(jnp.where(counts > 0, e_ids, N_EXPERTS)[::-1])[::-1]
    next_tab = jnp.concatenate([later[1:], jnp.full((1,), N_EXPERTS, jnp.int32)])
    next_tab = jnp.where(next_tab < N_EXPERTS, next_tab, -1)
    return (pend - padded, blk_expert, (pend[-1] // EXPERT_ROWS).astype(jnp.int32).reshape(1),
            next_tab[blk_expert].astype(jnp.int32))


def _col_major(t):
    b, l, f = t.shape
    rows = l // GRID_W
    return t.reshape(b, rows, GRID_W, f).transpose(0, 2, 1, 3).reshape(b, l, f)


def _row_major(t):
    b, l, f = t.shape
    rows = l // GRID_W
    return t.reshape(b, GRID_W, rows, f).transpose(0, 2, 1, 3).reshape(b, l, f)


def kernel(x, c, ctx, c_ctx, w_ada, b_ada, norm1_g, norm2_g, w_in, gdn_conv_w, gdn_a_log, gdn_dt_bias, gdn_norm_g,
           ml_i_bias, ml_f_bias, ml_norm_g, w_branch_gdn, w_branch_ml, w_out, w_router, router_bias, w_exp_gate_up,
           w_exp_down, w_sh_gate_up, w_sh_down, final_norm_g):
    b, l, d = x.shape
    lc = ctx.shape[1]
    t = b * l
    layer = 0

    w = w_in[layer]
    main_cols = [_ORIG[k] for k in ("gdn_qkv", "gdn_z", "ml_q", "ml_k", "ml_v", "ml_o", "mg_gdn", "mg_ml")]
    w_main = jnp.concatenate([w[:, a:e] for a, e in main_cols], axis=1).astype(BF16)
    w_gate = jnp.concatenate([w[:, _ORIG["gdn_gate"][0]:_ORIG["gdn_gate"][1]],
                              w[:, _ORIG["ml_gate"][0]:_ORIG["ml_gate"][1]],
                              jnp.zeros((d, LANES - 64), F32)], axis=1).astype(BF16)
    zeros16 = jnp.zeros((16,), F32)
    gp_add = jnp.concatenate([zeros16, gdn_dt_bias[layer].reshape(-1), ml_i_bias[layer].reshape(-1),
                              ml_f_bias[layer].reshape(-1), jnp.zeros((LANES - 64,), F32)])
    gp_mul = jnp.concatenate([zeros16, -jnp.exp(gdn_a_log[layer].astype(F32)).reshape(-1),
                              jnp.zeros((LANES - 32,), F32)])
    gparams = jnp.zeros((8, LANES), F32).at[0].set(gp_add).at[1].set(gp_mul)
    conv_w8 = jnp.zeros((8, gdn_conv_w.shape[2]), F32).at[0:GDN_CONV].set(gdn_conv_w[layer])
    wr = w_router[layer].T
    wr_hi = wr.astype(BF16)
    wr_lo = (wr - wr_hi.astype(F32)).astype(BF16)

    n_mod_rows = -(-(b + 1) // 8) * 8
    cc = jnp.zeros((n_mod_rows, d), F32).at[0:b].set(c).at[b].set(c_ctx)
    mod = _ada_mod(cc, w_ada[layer], b_ada[layer])
    mod3 = mod.reshape(n_mod_rows, 1, 6 * d)

    x2d = x.reshape(t, d)
    tm_l = min(1024, l)
    proj_l, gate_l = _project(x2d, mod3, lambda i: (i * tm_l) // l, norm1_g[layer], w_main, w_gate, tm_l)
    tm_c = min(1024, b * lc)
    proj_c, gate_c = _project(ctx.reshape(b * lc, d), mod3, lambda i: b, norm1_g[layer], w_main, w_gate, tm_c)
    proj_l3 = proj_l.reshape(b, l, N_MAIN)
    proj_c3 = proj_c.reshape(b, lc, N_MAIN)

    gate_l_cm = _col_major(gate_l.reshape(b, l, LANES)).reshape(t, LANES)
    gd_c, ml_c = _gate_prep(gate_c, gparams)
    gd_l, _ = _gate_prep(gate_l, gparams)
    _, ml_l = _gate_prep(gate_l_cm, gparams)

    nb = b // BATCH_PARTS
    tp = nb * l
    n_assign = tp * TOP_K
    n_blocks = (n_assign + N_EXPERTS * (EXPERT_ROWS - 1)) // EXPERT_ROWS + 1
    n_slots = n_blocks * EXPERT_ROWS
    gd_c3, gd_l3 = gd_c.reshape(b, lc, LANES), gd_l.reshape(b, l, LANES)
    ml_c3, ml_l3 = ml_c.reshape(b, lc, LANES), ml_l.reshape(b, l, LANES)
    wbg, wbm, wo = (w_branch_gdn[layer].astype(BF16), w_branch_ml[layer].astype(BF16), w_out[layer].astype(BF16))
    w_sh_gu, w_sh_dn = w_sh_gate_up[layer].astype(BF16), w_sh_down[layer].astype(BF16)
    outs = []
    for part in range(BATCH_PARTS):
        b0, tok0 = part * nb, part * tp
        y_gdn = _gdn(proj_c3, proj_l3, conv_w8, gd_c3, gd_l3, gdn_norm_g[layer], b0, nb)
        q_cm = _col_major(proj_l3[b0:b0 + nb, :, COL_ML_Q:COL_ML_Q + HEADS * ML_DK])
        k_cm = _col_major(proj_l3[b0:b0 + nb, :, COL_ML_K:COL_ML_K + HEADS * ML_DK])
        v_cm = _col_major(proj_l3[b0:b0 + nb, :, COL_ML_V:COL_ML_V + HEADS * HEAD_V])
        h_ml = _row_major(_mlstm(proj_c3, q_cm, k_cm, v_cm, ml_c3, ml_l3, b0))
        x1, h2, scores_t = _merge(y_gdn.reshape(tp, d), h_ml.reshape(tp, d), proj_l, x2d, mod3, l, ml_norm_g[layer],
                                norm2_g[layer], wbg, wbm, wo, wr_hi, wr_lo, tok0, tm=min(512, l))
        idx, wts, rank, cnt = _route(scores_t, router_bias[layer])
        counts = cnt[:, 0].astype(jnp.int32)
        pstart, blk_expert, n_used, next_expert = _block_table(counts, n_blocks)
        pos = _slots(idx, rank, pstart).reshape(n_assign)
        tok_ids = jnp.broadcast_to((jnp.arange(n_assign, dtype=jnp.int32) % tp)[:, None], (n_assign, LANES))
        scattered = _sc_scatter_rows(tok_ids, pos, n_slots)[:, 0]
        in_expert = (jnp.arange(n_slots, dtype=jnp.int32).reshape(n_blocks, EXPERT_ROWS)
                     - pstart[blk_expert][:, None])
        valid = (in_expert < counts[blk_expert][:, None]).reshape(n_slots)
        tok_slot = jnp.where(valid, scattered, jnp.arange(n_slots, dtype=jnp.int32) % tp)
        xb = h2.at[tok_slot].get(mode="promise_in_bounds")
        yb = _experts(xb, blk_expert, n_used, next_expert, w_exp_gate_up[layer], w_exp_down[layer])
        yg = yb.at[pos].get(mode="promise_in_bounds", unique_indices=True).reshape(TOP_K, tp, d)
        outs.append(_final(x1, h2, yg, wts.T, mod3, l, w_sh_gu, w_sh_dn, final_norm_g, tok0, tm=min(256, l)))
    return jnp.concatenate(outs, axis=0).reshape(b, l, d)
```

```python
import functools
import math

import jax
import jax.numpy as jnp
from jax import lax
from jax.experimental import pallas as pl
from jax.experimental.pallas import tpu as pltpu
from jax.experimental.pallas import tpu_sc as plsc

F32 = jnp.float32
BF16 = jnp.bfloat16
HI = lax.Precision.HIGHEST

EPS = 1e-6
CHUNK = 64
GRID_W = 64
HEADS = 8
HEAD_V = 128
GDN_DK = 128
ML_DK = 64
GDN_CONV = 5
N_EXPERTS = 256
TOP_K = 8
N_GROUPS = 8
TOPK_GROUPS = 4
ROUTED_SCALE = 2.5
EXPERT_ROWS = 256
BATCH_PARTS = 2
GDN_STEPS = 4
GDN_CTX_STEPS = 4
RING = 2 * GDN_STEPS
LANES = 128
VMEM_LIMIT = 56 * 1024 * 1024

COL_GDN_QKV = 0
COL_GDN_Z = 3072
COL_ML_Q = 4096
COL_ML_K = 4608
COL_ML_V = 5120
COL_ML_O = 6144
COL_MG_GDN = 7168
COL_MG_ML = 8192
N_MAIN = 9216
_ORIG = dict(gdn_qkv=(0, 3072), gdn_z=(3072, 4096), gdn_gate=(4096, 4128), ml_q=(4128, 4640),
             ml_k=(4640, 5152), ml_v=(5152, 6176), ml_o=(6176, 7200), ml_gate=(7200, 7232),
             mg_gdn=(7232, 8256), mg_ml=(8256, 9280))


def _params(sem, vmem=VMEM_LIMIT):
    return pltpu.CompilerParams(dimension_semantics=sem, vmem_limit_bytes=vmem)


def _dot(a, b, precision=None):
    return jnp.dot(a, b, preferred_element_type=F32, precision=precision)


def _dot_nt(a, b, precision=None):
    return lax.dot_general(a, b, (((1,), (1,)), ((), ())), preferred_element_type=F32, precision=precision)


def _dot_tn(a, b, precision=None):
    return lax.dot_general(a, b, (((0,), (0,)), ((), ())), preferred_element_type=F32, precision=precision)


def _times3(a, b):
    ah = a.astype(BF16)
    al = (a - ah.astype(F32)).astype(BF16)
    bh = b.astype(BF16)
    bl = (b - bh.astype(F32)).astype(BF16)
    return _dot(ah, bh) + (_dot(ah, bl) + _dot(al, bh))


def _sigmoid(x):
    return 1.0 / (1.0 + jnp.exp(-x))


def _softplus(x):
    return jnp.maximum(x, 0.0) + jnp.log(1.0 + jnp.exp(-jnp.abs(x)))


def _ada_kernel(c_ref, w_ref, b_ref, o_ref):
    c = c_ref[...]
    sc = c * _sigmoid(c)
    o_ref[...] = _dot(sc, w_ref[...], HI) + b_ref[...]


def _ada_mod(cc, w_ada, b_ada, tn=1536):
    r, d = cc.shape
    n = w_ada.shape[1]
    return pl.pallas_call(
        _ada_kernel,
        grid=(n // tn,),
        in_specs=[pl.BlockSpec((r, d), lambda j: (0, 0)),
                  pl.BlockSpec((d, tn), lambda j: (0, j)),
                  pl.BlockSpec((1, tn), lambda j: (0, j))],
        out_specs=pl.BlockSpec((r, tn), lambda j: (0, j)),
        out_shape=jax.ShapeDtypeStruct((r, n), F32),
        compiler_params=_params(("arbitrary",)),
    )(cc, w_ada, b_ada.reshape(1, n))


def _proj_kernel(x_ref, mod_ref, g_ref, w_ref, wg_ref, o_ref, og_ref, hn_ref):
    d = x_ref.shape[1]

    @pl.when(pl.program_id(1) == 0)
    def _():
        x = x_ref[...]
        y = x * lax.rsqrt(jnp.mean(x * x, axis=-1, keepdims=True) + EPS) * g_ref[...]
        shift = mod_ref[0, :, 0:d]
        scale = mod_ref[0, :, d:2 * d]
        h = (y * (1.0 + scale) + shift).astype(BF16)
        hn_ref[...] = h
        og_ref[...] = _dot(h, wg_ref[...])

    o_ref[...] = _dot(hn_ref[...], w_ref[...]).astype(o_ref.dtype)


def _project(x2d, mod3, mod_row_of_tile, norm_g, w_main, w_gate, tm, tn=1024):
    t, d = x2d.shape
    n = w_main.shape[1]
    return pl.pallas_call(
        _proj_kernel,
        grid=(t // tm, n // tn),
        in_specs=[pl.BlockSpec((tm, d), lambda i, j: (i, 0)),
                  pl.BlockSpec((1, 1, mod3.shape[2]), lambda i, j: (mod_row_of_tile(i), 0, 0)),
                  pl.BlockSpec((1, d), lambda i, j: (0, 0)),
                  pl.BlockSpec((d, tn), lambda i, j: (0, j)),
                  pl.BlockSpec((d, LANES), lambda i, j: (0, 0))],
        out_specs=[pl.BlockSpec((tm, tn), lambda i, j: (i, j)),
                   pl.BlockSpec((tm, LANES), lambda i, j: (i, 0))],
        out_shape=[jax.ShapeDtypeStruct((t, n), BF16), jax.ShapeDtypeStruct((t, LANES), F32)],
        scratch_shapes=[pltpu.VMEM((tm, d), BF16)],
        compiler_params=_params(("arbitrary", "arbitrary")),
    )(x2d, mod3, norm_g.reshape(1, d), w_main, w_gate)


def _gate_kernel(g_ref, p_ref, gd_ref, ml_ref):
    rows = g_ref.shape[0]
    raw = g_ref[...] + p_ref[0:1, :]
    lane = lax.broadcasted_iota(jnp.int32, raw.shape, 1)
    sp = _softplus(raw)
    vals = jnp.where(lane < 16, _sigmoid(raw),
                     jnp.where(lane < 32, p_ref[1:2, :] * sp,
                               jnp.where(lane < 48, raw,
                                         jnp.where(lane < 64, -_softplus(-raw), 0.0))))
    ri = lax.broadcasted_iota(jnp.int32, (CHUNK, CHUNK), 0)
    ci = lax.broadcasted_iota(jnp.int32, (CHUNK, CHUNK), 1)
    tri_f = (ri >= ci).astype(F32)
    tri_b = (ri <= ci).astype(F32)
    lane_c = lax.broadcasted_iota(jnp.int32, (CHUNK, LANES), 1)
    row_c = lax.broadcasted_iota(jnp.int32, (CHUNK, LANES), 0)
    fwd_lane = (lane_c % 16) < 8
    for c in range(rows // CHUNK):
        blk = vals[c * CHUNK:(c + 1) * CHUNK, :]
        cum = jnp.where(fwd_lane, _dot(tri_f, blk, HI), _dot(tri_b, blk, HI))
        gd_ref[c * CHUNK:(c + 1) * CHUNK, :] = jnp.where(lane_c < 16, blk, jnp.where(lane_c < 32, cum, 0.0))
        bcum = pltpu.roll(cum, LANES - 16, axis=1)
        gmb = blk - bcum
        cmf, cmb = gmb, gmb
        for s in (1, 2, 4, 8, 16, 32):
            cmf = jnp.maximum(cmf, jnp.where(row_c >= s, pltpu.roll(cmf, s, axis=0), -jnp.inf))
            cmb = jnp.maximum(cmb, jnp.where(row_c < CHUNK - s, pltpu.roll(cmb, CHUNK - s, axis=0), -jnp.inf))
        cm = jnp.where(fwd_lane, cmf, cmb)
        ml = jnp.where(lane_c < 16, pltpu.roll(gmb, LANES - 32, axis=1),
                       jnp.where(lane_c < 32, pltpu.roll(cm, LANES - 16, axis=1),
                                 jnp.where(lane_c < 48, bcum, 0.0)))
        ml_ref[c * CHUNK:(c + 1) * CHUNK, :] = ml


def _gate_prep(graw, gparams, tm=256):
    t = graw.shape[0]
    spec = pl.BlockSpec((tm, LANES), lambda i: (i, 0))
    return pl.pallas_call(
        _gate_kernel,
        grid=(t // tm,),
        in_specs=[spec, pl.BlockSpec((8, LANES), lambda i: (0, 0))],
        out_specs=[spec, spec],
        out_shape=[jax.ShapeDtypeStruct((t, LANES), F32)] * 2,
        compiler_params=_params(("arbitrary",)),
    )(graw, gparams)


def _split3(a):
    h = a.astype(BF16)
    r = a - h.astype(F32)
    m = r.astype(BF16)
    return h, m, (r - m.astype(F32)).astype(BF16)


def _lane_picks(x, lanes):
    li = lax.broadcasted_iota(jnp.int32, (LANES, LANES), 0)
    ci = lax.broadcasted_iota(jnp.int32, (LANES, LANES), 1)
    want = jnp.full((LANES, LANES), -1, jnp.int32)
    for j, lane in enumerate(lanes):
        want = jnp.where(ci == j, lane, want)
    sel = (li == want).astype(BF16)
    h, m, lo = _split3(x)
    cols = _dot(h, sel) + (_dot(m, sel) + _dot(lo, sel))
    return [jnp.broadcast_to(cols[:, j:j + 1], x.shape) for j in range(len(lanes))]


def _dir_masks(direction):
    ri = lax.broadcasted_iota(jnp.int32, (CHUNK, CHUNK), 0)
    ci = lax.broadcasted_iota(jnp.int32, (CHUNK, CHUNK), 1)
    if direction == 0:
        return ri >= ci, ri > ci
    return ri <= ci, ri < ci


def _gdn_kernel(qc_ref, kc_ref, vc_ref, ql_ref, kl_ref, vl_ref, z_ref, cwq_ref, cwk_ref, cwv_ref,
                gdc_ref, gdl_ref, ng_ref, y_ref,
                xpad, qs, ks, vs, beta_t, cg_t, wq_r, u_r, kd_r, qk_r, dc_r, out_s):
    lc = qc_ref.shape[1]
    ll = ql_ref.shape[1]
    lt = lc + ll
    n_c, n_l = lc // CHUNK, ll // CHUNK
    n_t = n_c + n_l
    rb = 256

    def l2n(x):
        return x * lax.rsqrt(jnp.sum(x * x, axis=-1, keepdims=True) + EPS)

    def prep(src_ref, cw_ref, dst, off, ls, kind):
        xpad[0:8, :] = jnp.zeros((8, LANES), F32)
        xpad[8:8 + ls, :] = src_ref[0].astype(F32)
        xpad[8 + ls:16 + ls, :] = jnp.zeros((8, LANES), F32)
        step = min(rb, ls)
        for r0 in range(0, ls, step):
            acc = jnp.zeros((step, LANES), F32)
            for t in range(GDN_CONV):
                s0 = r0 + 8 - GDN_CONV // 2 + t
                acc = acc + cw_ref[t:t + 1, :] * xpad[s0:s0 + step, :]
            y = acc * _sigmoid(acc)
            if kind == "q":
                y = l2n(y) * (GDN_DK ** -0.5)
            elif kind == "k":
                y = l2n(y)
            dst[off + r0:off + r0 + step, :] = y

    prep(qc_ref, cwq_ref, qs, 0, lc, "q")
    prep(kc_ref, cwk_ref, ks, 0, lc, "k")
    prep(vc_ref, cwv_ref, vs, 0, lc, "v")
    prep(ql_ref, cwq_ref, qs, lc, ll, "q")
    prep(kl_ref, cwk_ref, ks, lc, ll, "k")
    prep(vl_ref, cwv_ref, vs, lc, ll, "v")

    head = pl.program_id(1)

    def build_tables(src_ref, off, ls):
        step = min(rb, ls)
        for r0 in range(0, ls, step):
            picked = _lane_picks(src_ref[0, r0:r0 + step, :], [8 * d + head for d in range(2)]
                                 + [16 + 8 * d + head for d in range(2)])
            for d in range(2):
                beta_t[d, off + r0:off + r0 + step, :] = picked[d]
                cg_t[d, off + r0:off + r0 + step, :] = picked[2 + d]

    build_tables(gdc_ref, 0, lc)
    build_tables(gdl_ref, lc, ll)

    def bwd_chunk(t):
        return jnp.where(t < n_c, n_c - 1 - t, n_t + n_c - 1 - t)

    row_p = lax.broadcasted_iota(jnp.int32, (CHUNK, LANES), 0)
    lane_p = lax.broadcasted_iota(jnp.int32, (CHUNK, LANES), 1)
    fwd_p = lane_p < CHUNK
    col_p = jnp.where(fwd_p, lane_p, lane_p - CHUNK)
    signed = jnp.where(fwd_p, row_p - col_p, col_p - row_p)
    incl_p = signed >= 0
    strict_p = signed > 0
    eye_p = (row_p == col_p).astype(F32)
    keep_f = fwd_p.astype(BF16)
    keep_b = (1.0 - fwd_p.astype(F32)).astype(BF16)

    def block_diag(top, bottom):
        zero = jnp.zeros(top.shape, top.dtype)
        return jnp.concatenate([jnp.concatenate([top, zero], axis=1), jnp.concatenate([zero, bottom], axis=1)], axis=0)

    def pair_diag(xp):
        return jnp.concatenate([xp * keep_f, xp * keep_b], axis=0)

    def pair_times3(a, xp):
        ah = a.astype(BF16)
        al = (a - ah.astype(F32)).astype(BF16)
        xh = xp.astype(BF16)
        xl = (xp - xh.astype(F32)).astype(BF16)
        dh = pair_diag(xh)
        return _dot(ah, dh) + (_dot(ah, pair_diag(xl)) + _dot(al, dh))

    def prep_chain(t):
        tc = jnp.minimum(t, n_t - 1)
        q, k, v, beta, cgc, kb, ecg = [], [], [], [], [], [], []
        for d in range(2):
            c = tc if d == 0 else bwd_chunk(tc)
            rows = pl.ds(pl.multiple_of(c * CHUNK, CHUNK), CHUNK)
            q.append(qs[rows, :])
            k.append(ks[rows, :])
            v.append(vs[rows, :])
            beta.append(beta_t[d, rows, :])
            cgc.append(cg_t[d, rows, :])
        kdiag = block_diag(k[0].astype(BF16), k[1].astype(BF16))
        kk = _dot_nt(jnp.concatenate([k[0], k[1]], axis=1).astype(BF16), kdiag)
        qk = _dot_nt(jnp.concatenate([q[0], q[1]], axis=1).astype(BF16), kdiag)
        yield
        beta_p = jnp.where(fwd_p, beta[0], beta[1])
        cg_p = jnp.where(fwd_p, cgc[0], cgc[1])
        cgr_p = jnp.transpose(jnp.concatenate([cgc[0], cgc[1]], axis=0))[0:CHUNK, :]
        decay = jnp.exp(jnp.where(incl_p, cg_p - cgr_p, -jnp.inf))
        qkd = qk * decay
        for d in range(2):
            slot = (t % RING) * 2 + d
            last = CHUNK - 1 if d == 0 else 0
            cg_last = cgc[d][last:last + 1, :]
            ecg.append(jnp.exp(cgc[d]))
            kb.append(k[d] * beta[d])
            qk_r[slot] = qkd[:, d * CHUNK:(d + 1) * CHUNK]
            wq_r[slot, CHUNK:2 * CHUNK, :] = q[d] * ecg[d]
            kd_r[slot] = k[d] * jnp.exp(cg_last - cgc[d])
            dc_r[slot] = jnp.broadcast_to(jnp.exp(cg_last), (8, LANES))
        x = jnp.where(strict_p, -(beta_p * kk) * decay, 0.0)
        tinv = eye_p + x
        x = pair_times3(x, x)
        yield
        for _ in range(4):
            both = pair_times3(jnp.concatenate([tinv, x], axis=0), x)
            tinv, x = tinv + both[0:CHUNK, :], both[CHUNK:2 * CHUNK, :]
            yield
        tinv = (tinv + pair_times3(tinv, x)).astype(BF16)
        yield
        w = _dot(tinv, block_diag((kb[0] * ecg[0]).astype(BF16), (kb[1] * ecg[1]).astype(BF16)))
        u = _dot(tinv, block_diag((v[0] * beta[0]).astype(BF16), (v[1] * beta[1]).astype(BF16)))
        for d in range(2):
            slot = (t % RING) * 2 + d
            wq_r[slot, 0:CHUNK, :] = w[:, d * LANES:(d + 1) * LANES]
            u_r[slot] = u[:, d * LANES:(d + 1) * LANES]

    out_s[...] = jnp.zeros(out_s.shape, F32)

    def scan_chain(d, t0, steps, s, with_out, result):
        for j in range(steps):
            t = t0 + j
            slot = (t % RING) * 2 + d
            ws = _dot(wq_r[slot], s)
            yield
            v_new = u_r[slot] - ws[0:CHUNK, :]
            if with_out:
                c = t if d == 0 else bwd_chunk(t)
                o = ws[CHUNK:2 * CHUNK, :] + _dot(qk_r[slot], v_new)
                l0 = pl.multiple_of((c - n_c) * CHUNK, CHUNK)
                out_s[pl.ds(l0, CHUNK), :] += o
            s = s * dc_r[slot][0:1, :] + _dot_tn(kd_r[slot], v_new)
            yield
        result[d] = s

    def lockstep(chains):
        chains = list(chains)
        while chains:
            alive = []
            for ch in chains:
                try:
                    next(ch)
                    alive.append(ch)
                except StopIteration:
                    pass
            chains = alive

    def group_body(i, carry, t_base, steps, ahead, with_out):
        t0 = t_base + steps * i
        result = [None, None]
        lockstep([scan_chain(d, t0, steps, carry[d], with_out, result) for d in range(2)]
                 + [prep_chain(t0 + steps + j) for j in range(ahead)])
        return result[0], result[1]

    def run_groups(carry, t_base, n_steps, steps, ahead_last, with_out):
        n_groups = n_steps // steps
        carry = lax.fori_loop(0, n_groups - 1, functools.partial(
            group_body, t_base=t_base, steps=steps, ahead=steps, with_out=with_out), carry)
        return group_body(n_groups - 1, carry, t_base, steps, ahead_last, with_out)

    lockstep([prep_chain(j) for j in range(GDN_CTX_STEPS)])
    s0 = jnp.zeros((GDN_DK, HEAD_V), F32)
    carry = run_groups((s0, s0), 0, n_c, GDN_CTX_STEPS, GDN_STEPS, False)
    run_groups(carry, n_c, n_l, GDN_STEPS, 0, True)


    def out_body(i, carry):
        r0 = pl.multiple_of(i * rb, rb)
        o = out_s[pl.ds(r0, rb), :]
        z = z_ref[0, pl.ds(r0, rb), :].astype(F32)
        y = o * lax.rsqrt(jnp.mean(o * o, axis=-1, keepdims=True) + EPS) * ng_ref[...]
        y_ref[0, pl.ds(r0, rb), :] = (y * (z * _sigmoid(z))).astype(y_ref.dtype)
        return carry

    lax.fori_loop(0, ll // rb, out_body, 0)


def _gdn(proj_c, proj_l, conv_w8, gd_c, gd_l, norm_g, b0, nb):
    lc = proj_c.shape[1]
    ll = proj_l.shape[1]
    lt = lc + ll
    qb, kb_, vb, zb = (COL_GDN_QKV // LANES, COL_GDN_QKV // LANES + HEADS, COL_GDN_QKV // LANES + 2 * HEADS,
                       COL_GDN_Z // LANES)

    def seq_spec(l, col0):
        return pl.BlockSpec((1, l, LANES), lambda i, h: (i + b0, 0, col0 + h))

    def cw_spec(col0):
        return pl.BlockSpec((8, LANES), lambda i, h: (0, col0 + h))

    return pl.pallas_call(
        _gdn_kernel,
        grid=(nb, HEADS),
        in_specs=[seq_spec(lc, qb), seq_spec(lc, kb_), seq_spec(lc, vb),
                  seq_spec(ll, qb), seq_spec(ll, kb_), seq_spec(ll, vb), seq_spec(ll, zb),
                  cw_spec(0), cw_spec(HEADS), cw_spec(2 * HEADS),
                  pl.BlockSpec((1, lc, LANES), lambda i, h: (i + b0, 0, 0)),
                  pl.BlockSpec((1, ll, LANES), lambda i, h: (i + b0, 0, 0)),
                  pl.BlockSpec((1, LANES), lambda i, h: (0, 0))],
        out_specs=pl.BlockSpec((1, ll, LANES), lambda i, h: (i, 0, h)),
        out_shape=jax.ShapeDtypeStruct((nb, ll, HEADS * HEAD_V), BF16),
        scratch_shapes=[pltpu.VMEM((max(lc, ll) + 16, LANES), F32),
                        pltpu.VMEM((lt, LANES), F32), pltpu.VMEM((lt, LANES), F32), pltpu.VMEM((lt, LANES), F32),
                        pltpu.VMEM((2, lt, LANES), F32), pltpu.VMEM((2, lt, LANES), F32),
                        pltpu.VMEM((2 * RING, 2 * CHUNK, LANES), F32),
                        pltpu.VMEM((2 * RING, CHUNK, LANES), F32), pltpu.VMEM((2 * RING, CHUNK, LANES), F32),
                        pltpu.VMEM((2 * RING, CHUNK, CHUNK), F32),
                        pltpu.VMEM((2 * RING, 8, LANES), F32),
                        pltpu.VMEM((ll, LANES), F32)],
        compiler_params=_params(("arbitrary", "arbitrary")),
    )(proj_c, proj_c, proj_c, proj_l, proj_l, proj_l, proj_l, conv_w8, conv_w8, conv_w8,
      gd_c, gd_l, norm_g.reshape(1, LANES))


def _mlstm_kernel(qc_ref, kc_ref, vc_ref, ql_ref, kl_ref, vl_ref, mlc_ref, mll_ref, h_ref, out_s, tabs):
    lc = qc_ref.shape[1]
    ll = ql_ref.shape[1]
    n_c, n_l = lc // CHUNK, ll // CHUNK
    n_t = n_c + n_l
    lt = lc + ll
    pair = pl.program_id(1)
    lane = lax.broadcasted_iota(jnp.int32, (CHUNK, LANES), 1)
    ones_v = jnp.ones((CHUNK, HEAD_V), BF16)
    chains = [(hh, d) for hh in range(2) for d in range(2)]
    hmask = [((lane // ML_DK) == hh).astype(F32) for hh in range(2)]
    incl = [_dir_masks(d)[0] for d in range(2)]

    def build_tables(src_ref, off, ls):
        step = min(256, ls)
        for r0 in range(0, ls, step):
            lanes = [16 * j + 8 * d + 2 * pair + hh for hh, d in chains for j in range(3)]
            for g, tab in enumerate(_lane_picks(src_ref[0, r0:r0 + step, :], lanes)):
                tabs[g, off + r0:off + r0 + step, :] = tab

    build_tables(mlc_ref, 0, lc)
    build_tables(mll_ref, lc, ll)

    def wide(a):
        return jnp.concatenate([a, a], axis=1)

    def chain(hh, d, c, state, is_ctx, result):
        cs, ms = state
        last = CHUNK - 1 if d == 0 else 0
        if is_ctx:
            rows = pl.ds(pl.multiple_of(c * CHUNK, CHUNK), CHUNK)
            q_ref, k_ref, v_ref = qc_ref, kc_ref, vc_ref
        else:
            rows = pl.ds(pl.multiple_of((c - n_c) * CHUNK, CHUNK), CHUNK)
            q_ref, k_ref, v_ref = ql_ref, kl_ref, vl_ref
        q = (q_ref[0, rows, :].astype(F32) * hmask[hh]).astype(BF16)
        k = k_ref[0, rows, :].astype(F32) * (hmask[hh] * (ML_DK ** -0.5))
        v = jnp.concatenate([v_ref[0, rows, hh * HEAD_V:(hh + 1) * HEAD_V], ones_v], axis=1)
        n = chains.index((hh, d))
        trows = pl.ds(pl.multiple_of(c * CHUNK, CHUNK), CHUNK)
        gmb = tabs[3 * n, trows, :]
        gmb_t = jnp.transpose(gmb)[0:CHUNK, :]
        cm = tabs[3 * n + 1, trows, :]
        bc = tabs[3 * n + 2, trows, :]
        qk = _dot_nt(q, k.astype(BF16))
        yield
        cm_last = cm[last:last + 1, :]
        b_last = bc[last:last + 1, :]
        mm = jnp.maximum(ms, cm)
        p = jnp.where(incl[d], jnp.exp(gmb_t - mm[:, 0:CHUNK]), 0.0) * qk
        wk = (k * jnp.exp(gmb - cm_last)).astype(BF16)
        inter = _dot(q, cs.astype(BF16))
        intra = _dot(p.astype(BF16), v)
        c_loc = _dot_tn(wk, v)
        yield
        if not is_ctx:
            nd = wide(jnp.exp(ms - mm)) * inter + intra
            hv = nd[:, 0:HEAD_V] / jnp.maximum(jnp.abs(nd[:, HEAD_V:2 * HEAD_V]), jnp.exp(-(bc + mm)))
            l0 = pl.multiple_of((c - n_c) * CHUNK, CHUNK)
            out_s[pl.ds(l0, CHUNK), hh * HEAD_V:(hh + 1) * HEAD_V] += hv
        mx = jnp.maximum(ms, cm_last)
        result[hh, d] = (wide(jnp.exp(ms - mx)) * cs + wide(jnp.exp(cm_last - mx)) * c_loc, b_last + mx)

    out_s[...] = jnp.zeros(out_s.shape, F32)

    def run(chains_iter):
        live = list(chains_iter)
        while live:
            alive = []
            for ch in live:
                try:
                    next(ch)
                    alive.append(ch)
                except StopIteration:
                    pass
            live = alive

    def body(i, carry, is_ctx):
        result = {}
        gens = []
        for n, (hh, d) in enumerate(chains):
            if is_ctx:
                c = i if d == 0 else n_c - 1 - i
            else:
                c = n_c + i if d == 0 else n_t - 1 - i
            gens.append(chain(hh, d, c, carry[n], is_ctx, result))
        run(gens)
        return tuple(result[hd] for hd in chains)

    st0 = (jnp.zeros((LANES, 2 * HEAD_V), F32), jnp.zeros((1, LANES), F32))
    carry = lax.fori_loop(0, n_c, functools.partial(body, is_ctx=True), (st0,) * 4)
    lax.fori_loop(0, n_l, functools.partial(body, is_ctx=False), carry)
    h_ref[0] = out_s[...].astype(h_ref.dtype)


def _mlstm(proj_c, q_l, k_l, v_l, ml_c, ml_l, b0):
    lc = proj_c.shape[1]
    nb, ll, _ = q_l.shape
    lt = lc + ll
    qb, kb_, vb = COL_ML_Q // LANES, COL_ML_K // LANES, COL_ML_V // (2 * HEAD_V)
    return pl.pallas_call(
        _mlstm_kernel,
        grid=(nb, HEADS // 2),
        in_specs=[pl.BlockSpec((1, lc, LANES), lambda i, p: (i + b0, 0, qb + p)),
                  pl.BlockSpec((1, lc, LANES), lambda i, p: (i + b0, 0, kb_ + p)),
                  pl.BlockSpec((1, lc, 2 * HEAD_V), lambda i, p: (i + b0, 0, vb + p)),
                  pl.BlockSpec((1, ll, LANES), lambda i, p: (i, 0, p)),
                  pl.BlockSpec((1, ll, LANES), lambda i, p: (i, 0, p)),
                  pl.BlockSpec((1, ll, 2 * HEAD_V), lambda i, p: (i, 0, p)),
                  pl.BlockSpec((1, lc, LANES), lambda i, p: (i + b0, 0, 0)),
                  pl.BlockSpec((1, ll, LANES), lambda i, p: (i + b0, 0, 0))],
        out_specs=pl.BlockSpec((1, ll, 2 * HEAD_V), lambda i, p: (i, 0, p)),
        out_shape=jax.ShapeDtypeStruct((nb, ll, HEADS * HEAD_V), BF16),
        scratch_shapes=[pltpu.VMEM((ll, 2 * HEAD_V), F32), pltpu.VMEM((12, lt, LANES), F32)],
        compiler_params=_params(("arbitrary", "arbitrary")),
    )(proj_c, proj_c, proj_c, q_l, k_l, v_l, ml_c, ml_l)


def _merge_kernel(yg_ref, hm_ref, o_ref, gg_ref, gm_ref, x_ref, mod_ref, mlg_ref, n2_ref,
                  wbg_ref, wbm_ref, wo_ref, wrh_ref, wrl_ref, x1_ref, h2_ref, sc_ref):
    d = x_ref.shape[1]
    o = o_ref[...].astype(F32)
    ym = _sigmoid(o) * hm_ref[...].astype(F32)
    segs = []
    for h in range(HEADS):
        seg = ym[:, h * HEAD_V:(h + 1) * HEAD_V]
        segs.append(seg * lax.rsqrt(jnp.mean(seg * seg, axis=-1, keepdims=True) + EPS))
    ymn = jnp.concatenate(segs, axis=1) * mlg_ref[...]
    y_gdn = _dot(yg_ref[...], wbg_ref[...])
    y_ml = _dot(ymn.astype(BF16), wbm_ref[...])
    mixed = _sigmoid(gg_ref[...].astype(F32)) * y_gdn + _sigmoid(gm_ref[...].astype(F32)) * y_ml
    y = _dot(mixed.astype(BF16), wo_ref[...])
    x1 = x_ref[...] + mod_ref[0, :, 2 * d:3 * d] * y
    x1_ref[...] = x1
    hn = x1 * lax.rsqrt(jnp.mean(x1 * x1, axis=-1, keepdims=True) + EPS) * n2_ref[...]
    h2 = hn * (1.0 + mod_ref[0, :, 4 * d:5 * d]) + mod_ref[0, :, 3 * d:4 * d]
    h2_hi = h2.astype(BF16)
    h2_ref[...] = h2_hi
    h2_lo = (h2 - h2_hi.astype(F32)).astype(BF16)
    logits = _dot_nt(wrh_ref[...], h2_hi) + (_dot_nt(wrl_ref[...], h2_hi) + _dot_nt(wrh_ref[...], h2_lo))
    sc_ref[...] = _sigmoid(logits)


def _merge(y_gdn, h_ml, proj_l2d, x2d, mod3, rows_per_mod, ml_norm_g, norm2_g, wbg, wbm, wo, wr_hi, wr_lo, tok0,
           tm=512):
    t, d = y_gdn.shape
    e = wr_hi.shape[0]
    off = tok0 // tm
    row = lambda i: (i, 0)
    const = lambda i: (0, 0)
    return pl.pallas_call(
        _merge_kernel,
        grid=(t // tm,),
        in_specs=[pl.BlockSpec((tm, d), row), pl.BlockSpec((tm, d), row),
                  pl.BlockSpec((tm, d), lambda i: (i + off, COL_ML_O // d)),
                  pl.BlockSpec((tm, d), lambda i: (i + off, COL_MG_GDN // d)),
                  pl.BlockSpec((tm, d), lambda i: (i + off, COL_MG_ML // d)),
                  pl.BlockSpec((tm, d), lambda i: (i + off, 0)),
                  pl.BlockSpec((1, 1, mod3.shape[2]), lambda i: (((i + off) * tm) // rows_per_mod, 0, 0)),
                  pl.BlockSpec((1, d), const), pl.BlockSpec((1, d), const),
                  pl.BlockSpec((d, d), const), pl.BlockSpec((d, d), const), pl.BlockSpec((d, d), const),
                  pl.BlockSpec((e, d), const), pl.BlockSpec((e, d), const)],
        out_specs=[pl.BlockSpec((tm, d), row), pl.BlockSpec((tm, d), lambda i: (i + off, 0)),
                   pl.BlockSpec((e, tm), lambda i: (0, i))],
        out_shape=[jax.ShapeDtypeStruct((t, d), F32), jax.ShapeDtypeStruct((x2d.shape[0], d), BF16),
                   jax.ShapeDtypeStruct((e, t), F32)],
        compiler_params=_params(("arbitrary",)),
    )(y_gdn, h_ml, proj_l2d, proj_l2d, proj_l2d, x2d, mod3, ml_norm_g.reshape(1, d), norm2_g.reshape(1, d),
      wbg, wbm, wo, wr_hi, wr_lo)


def _expert_kernel(be_ref, nu_ref, nx_ref, x_ref, wgu_hbm, wd_hbm, y_ref, land_gu, land_d, wgu_s, wd_s, sems):
    i = pl.program_id(0)
    de = wd_hbm.shape[1]
    used = i < nu_ref[0]
    first = jnp.logical_or(i == 0, be_ref[i] != be_ref[jnp.maximum(i - 1, 0)])

    def weight_copies(ex):
        return (pltpu.make_async_copy(wgu_hbm.at[ex], land_gu, sems.at[0]),
                pltpu.make_async_copy(wd_hbm.at[ex], land_d, sems.at[1]))

    @pl.when(i == 0)
    def _():
        for cp in weight_copies(be_ref[0]):
            cp.start()

    @pl.when(jnp.logical_and(first, used))
    def _():
        for cp in weight_copies(be_ref[i]):
            cp.wait()
        wgu_s[...] = land_gu[...].astype(BF16)
        wd_s[...] = land_d[...].astype(BF16)

        @pl.when(nx_ref[i] >= 0)
        def _():
            for cp in weight_copies(nx_ref[i]):
                cp.start()

    @pl.when(used)
    def _():
        gu = _dot(x_ref[...], wgu_s[...])
        g = gu[:, 0:de]
        act = (g * _sigmoid(g)) * gu[:, de:2 * de]
        y_ref[...] = _dot(act.astype(BF16), wd_s[...]).astype(y_ref.dtype)

    @pl.when(jnp.logical_not(used))
    def _():
        y_ref[...] = jnp.zeros(y_ref.shape, y_ref.dtype)


def _experts(xb, blk_expert, n_used, next_expert, w_gu, w_down):
    n_slots, d = xb.shape
    n_blocks = n_slots // EXPERT_ROWS
    e, _, de2 = w_gu.shape
    de = de2 // 2
    return pl.pallas_call(
        _expert_kernel,
        grid_spec=pltpu.PrefetchScalarGridSpec(
            num_scalar_prefetch=3,
            grid=(n_blocks,),
            in_specs=[pl.BlockSpec((EXPERT_ROWS, d), lambda i, be, nu, nx: (i, 0)),
                      pl.BlockSpec(memory_space=pl.ANY), pl.BlockSpec(memory_space=pl.ANY)],
            out_specs=pl.BlockSpec((EXPERT_ROWS, d), lambda i, be, nu, nx: (i, 0)),
            scratch_shapes=[pltpu.VMEM((d, de2), w_gu.dtype), pltpu.VMEM((de, d), w_down.dtype),
                            pltpu.VMEM((d, de2), BF16), pltpu.VMEM((de, d), BF16),
                            pltpu.SemaphoreType.DMA((2,))]),
        out_shape=jax.ShapeDtypeStruct((n_slots, d), BF16),
        compiler_params=_params(("arbitrary",)),
    )(blk_expert, n_used, next_expert, xb, w_gu, w_down)


def _final_kernel(x1_ref, h2_ref, yg_ref, wt_ref, mod_ref, wsg_ref, wsd_ref, fg_ref, o_ref):
    d = x1_ref.shape[1]
    ds_ = wsd_ref.shape[0]
    wt = wt_ref[...]
    routed = jnp.zeros(x1_ref.shape, F32)
    for k in range(TOP_K):
        routed = routed + wt[:, k:k + 1] * yg_ref[k].astype(F32)
    gu = _dot(h2_ref[...], wsg_ref[...])
    g = gu[:, 0:ds_]
    sh = _dot(((g * _sigmoid(g)) * gu[:, ds_:2 * ds_]).astype(BF16), wsd_ref[...])
    x2 = x1_ref[...] + mod_ref[0, :, 5 * d:6 * d] * (routed + sh)
    o_ref[...] = x2 * lax.rsqrt(jnp.mean(x2 * x2, axis=-1, keepdims=True) + EPS) * fg_ref[...]


def _final(x1, h2, yg, wts, mod3, rows_per_mod, w_sh_gu, w_sh_down, final_g, tok0, tm=256):
    tp, d = x1.shape
    off = tok0 // tm
    row = lambda i: (i, 0)
    const = lambda i: (0, 0)
    return pl.pallas_call(
        _final_kernel,
        grid=(tp // tm,),
        in_specs=[pl.BlockSpec((tm, d), row), pl.BlockSpec((tm, d), lambda i: (i + off, 0)),
                  pl.BlockSpec((TOP_K, tm, d), lambda i: (0, i, 0)), pl.BlockSpec((tm, TOP_K), row),
                  pl.BlockSpec((1, 1, mod3.shape[2]), lambda i: (((i + off) * tm) // rows_per_mod, 0, 0)),
                  pl.BlockSpec(w_sh_gu.shape, const), pl.BlockSpec(w_sh_down.shape, const),
                  pl.BlockSpec((1, d), const)],
        out_specs=pl.BlockSpec((tm, d), row),
        out_shape=jax.ShapeDtypeStruct((tp, d), F32),
        compiler_params=_params(("arbitrary",)),
    )(x1, h2, yg, wts, mod3, w_sh_gu, w_sh_down, final_g.reshape(1, d))


def _route_kernel(sc_ref, bias_ref, tri_ref, idx_ref, wt_ref, rk_ref, cnt_ref, base_s):
    @pl.when(pl.program_id(0) == 0)
    def _():
        base_s[...] = jnp.zeros(base_s.shape, F32)

    scores = sc_ref[...]
    e, tn = scores.shape
    gsz = e // N_GROUPS
    sel3 = (scores + bias_ref[...]).reshape(N_GROUPS, gsz, tn)
    m1 = jnp.max(sel3, axis=1)
    is_max = sel3 == m1[:, None, :]
    n_max = jnp.sum(is_max.astype(F32), axis=1)
    m2 = jnp.max(jnp.where(is_max, -jnp.inf, sel3), axis=1)
    grp = m1 + jnp.where(n_max >= 2.0, m1, m2)
    gi = lax.broadcasted_iota(jnp.int32, (N_GROUPS, tn), 0)
    ahead = jnp.zeros((N_GROUPS, tn), F32)
    for g in range(N_GROUPS):
        row = grp[g:g + 1, :]
        ahead = ahead + jnp.logical_or(row > grp, jnp.logical_and(row == grp, g < gi)).astype(F32)
    ahead3 = jnp.broadcast_to(ahead[:, None, :], (N_GROUPS, gsz, tn))
    selm = jnp.where(ahead3 < float(TOPK_GROUPS), sel3, -jnp.inf).reshape(e, tn)
    ri = lax.broadcasted_iota(jnp.int32, (e, tn), 0).astype(F32)
    member = jnp.zeros((e, tn), F32)
    idxs, ws = [], []
    for _ in range(TOP_K):
        m = jnp.max(selm, axis=0, keepdims=True)
        idx = jnp.min(jnp.where(selm == m, ri, float(e)), axis=0, keepdims=True)
        hit = ri == idx
        ws.append(jnp.sum(jnp.where(hit, scores, 0.0), axis=0, keepdims=True))
        idxs.append(idx)
        selm = jnp.where(hit, -jnp.inf, selm)
        member = jnp.where(hit, 1.0, member)
    w = jnp.concatenate(ws, axis=0)
    wt_ref[...] = w / jnp.sum(w, axis=0, keepdims=True) * ROUTED_SCALE
    idx_ref[...] = jnp.concatenate(idxs, axis=0).astype(jnp.int32)
    cum = _dot(member.astype(BF16), tri_ref[...]) + base_s[...]
    rk_ref[...] = jnp.concatenate(
        [jnp.sum(jnp.where(ri == idx, cum, 0.0), axis=0, keepdims=True) for idx in idxs], axis=0).astype(jnp.int32)
    total = base_s[...] + jnp.sum(member, axis=1, keepdims=True)
    base_s[...] = total
    cnt_ref[...] = total


def _route(scores_t, router_bias):
    e, t = scores_t.shape
    tn = LANES
    bias = jnp.broadcast_to(router_bias.astype(F32)[:, None], (e, tn))
    tri = (jnp.arange(tn)[:, None] < jnp.arange(tn)[None, :]).astype(BF16)
    tok = pl.BlockSpec((TOP_K, tn), lambda i: (0, i))
    const = lambda i: (0, 0)
    return pl.pallas_call(
        _route_kernel,
        grid=(t // tn,),
        in_specs=[pl.BlockSpec((e, tn), lambda i: (0, i)), pl.BlockSpec((e, tn), const),
                  pl.BlockSpec((tn, tn), const)],
        out_specs=[tok, tok, tok, pl.BlockSpec((e, tn), const)],
        out_shape=[jax.ShapeDtypeStruct((TOP_K, t), jnp.int32), jax.ShapeDtypeStruct((TOP_K, t), F32),
                   jax.ShapeDtypeStruct((TOP_K, t), jnp.int32), jax.ShapeDtypeStruct((e, tn), F32)],
        scratch_shapes=[pltpu.VMEM((e, tn), F32)],
        compiler_params=_params(("arbitrary",)),
    )(scores_t, bias, tri)


def _slot_kernel(idx_ref, rk_ref, ps_ref, pos_ref):
    e, tn = ps_ref.shape
    ri = lax.broadcasted_iota(jnp.int32, (e, tn), 0)
    ps = ps_ref[...]
    rows = [jnp.sum(jnp.where(ri == idx_ref[k:k + 1, :], ps, 0.0), axis=0, keepdims=True) for k in range(TOP_K)]
    pos_ref[...] = rk_ref[...] + jnp.concatenate(rows, axis=0).astype(jnp.int32)


def _slots(idx, rank, pstart):
    k, t = idx.shape
    e = pstart.shape[0]
    tn = LANES
    tok = pl.BlockSpec((k, tn), lambda i: (0, i))
    return pl.pallas_call(
        _slot_kernel,
        grid=(t // tn,),
        in_specs=[tok, tok, pl.BlockSpec((e, tn), lambda i: (0, 0))],
        out_specs=tok,
        out_shape=jax.ShapeDtypeStruct((k, t), jnp.int32),
        compiler_params=_params(("arbitrary",)),
    )(idx, rank, jnp.broadcast_to(pstart.astype(F32)[:, None], (e, tn)))


def _sc_scatter_rows(vals, idx, n_rows, window=LANES):
    n, width = vals.shape
    mesh = plsc.VectorSubcoreMesh(core_axis_name="core", subcore_axis_name="subcore")

    @pl.kernel(out_type=jax.ShapeDtypeStruct((n_rows, width), vals.dtype), mesh=mesh, scratch_types=[])
    def scatter(v_hbm, i_hbm, o_hbm):
        def body(v_vmem, i_vmem):
            pltpu.sync_copy(v_vmem, o_hbm.at[i_vmem.at[0]])

        pltpu.emit_pipeline(
            body,
            grid=(n // window,),
            in_specs=[pl.BlockSpec((window, width), lambda i: (i, 0)),
                      pl.BlockSpec((1, window), lambda i: (0, i))],
            out_specs=[],
            core_axis_name=("core", "subcore"),
            dimension_semantics=(pltpu.PARALLEL,),
        )(v_hbm, i_hbm)

    return scatter(vals, idx.reshape(1, n))


def _block_table(counts, n_blocks):
    padded = (counts + EXPERT_ROWS - 1) // EXPERT_ROWS * EXPERT_ROWS
    pend = jnp.cumsum(padded)
    blk_expert = jnp.minimum(jnp.searchsorted(pend, jnp.arange(n_blocks) * EXPERT_ROWS, side='right'),
                             N_EXPERTS - 1).astype(jnp.int32)
    e_ids = jnp.arange(N_EXPERTS, dtype=jnp.int32)
    later = lax.cummin(jnp.where(counts > 0, e_ids, N_EXPERTS)[::-1])[::-1]
    next_tab = jnp.concatenate([later[1:], jnp.full((1,), N_EXPERTS, jnp.int32)])
    next_tab = jnp.where(next_tab < N_EXPERTS, next_tab, -1)
    return (pend - padded, blk_expert, (pend[-1] // EXPERT_ROWS).astype(jnp.int32).reshape(1),
            next_tab[blk_expert].astype(jnp.int32))


def _col_major(t):
    b, l, f = t.shape
    rows = l // GRID_W
    return t.reshape(b, rows, GRID_W, f).transpose(0, 2, 1, 3).reshape(b, l, f)


def _row_major(t):
    b, l, f = t.shape
    rows = l // GRID_W
    return t.reshape(b, GRID_W, rows, f).transpose(0, 2, 1, 3).reshape(b, l, f)


def kernel(x, c, ctx, c_ctx, w_ada, b_ada, norm1_g, norm2_g, w_in, gdn_conv_w, gdn_a_log, gdn_dt_bias, gdn_norm_g,
           ml_i_bias, ml_f_bias, ml_norm_g, w_branch_gdn, w_branch_ml, w_out, w_router, router_bias, w_exp_gate_up,
           w_exp_down, w_sh_gate_up, w_sh_down, final_norm_g):
    b, l, d = x.shape
    lc = ctx.shape[1]
    t = b * l
    layer = 0

    w = w_in[layer]
    main_cols = [_ORIG[k] for k in ("gdn_qkv", "gdn_z", "ml_q", "ml_k", "ml_v", "ml_o", "mg_gdn", "mg_ml")]
    w_main = jnp.concatenate([w[:, a:e] for a, e in main_cols], axis=1).astype(BF16)
    w_gate = jnp.concatenate([w[:, _ORIG["gdn_gate"][0]:_ORIG["gdn_gate"][1]],
                              w[:, _ORIG["ml_gate"][0]:_ORIG["ml_gate"][1]],
                              jnp.zeros((d, LANES - 64), F32)], axis=1).astype(BF16)
    zeros16 = jnp.zeros((16,), F32)
    gp_add = jnp.concatenate([zeros16, gdn_dt_bias[layer].reshape(-1), ml_i_bias[layer].reshape(-1),
                              ml_f_bias[layer].reshape(-1), jnp.zeros((LANES - 64,), F32)])
    gp_mul = jnp.concatenate([zeros16, -jnp.exp(gdn_a_log[layer].astype(F32)).reshape(-1),
                              jnp.zeros((LANES - 32,), F32)])
    gparams = jnp.zeros((8, LANES), F32).at[0].set(gp_add).at[1].set(gp_mul)
    conv_w8 = jnp.zeros((8, gdn_conv_w.shape[2]), F32).at[0:GDN_CONV].set(gdn_conv_w[layer])
    wr = w_router[layer].T
    wr_hi = wr.astype(BF16)
    wr_lo = (wr - wr_hi.astype(F32)).astype(BF16)

    n_mod_rows = -(-(b + 1) // 8) * 8
    cc = jnp.zeros((n_mod_rows, d), F32).at[0:b].set(c).at[b].set(c_ctx)
    mod = _ada_mod(cc, w_ada[layer], b_ada[layer])
    mod3 = mod.reshape(n_mod_rows, 1, 6 * d)

    x2d = x.reshape(t, d)
    tm_l = min(1024, l)
    proj_l, gate_l = _project(x2d, mod3, lambda i: (i * tm_l) // l, norm1_g[layer], w_main, w_gate, tm_l)
    tm_c = min(1024, b * lc)
    proj_c, gate_c = _project(ctx.reshape(b * lc, d), mod3, lambda i: b, norm1_g[layer], w_main, w_gate, tm_c)
    proj_l3 = proj_l.reshape(b, l, N_MAIN)
    proj_c3 = proj_c.reshape(b, lc, N_MAIN)

    gate_l_cm = _col_major(gate_l.reshape(b, l, LANES)).reshape(t, LANES)
    gd_c, ml_c = _gate_prep(gate_c, gparams)
    gd_l, _ = _gate_prep(gate_l, gparams)
    _, ml_l = _gate_prep(gate_l_cm, gparams)

    nb = b // BATCH_PARTS
    tp = nb * l
    n_assign = tp * TOP_K
    n_blocks = (n_assign + N_EXPERTS * (EXPERT_ROWS - 1)) // EXPERT_ROWS + 1
    n_slots = n_blocks * EXPERT_ROWS
    gd_c3, gd_l3 = gd_c.reshape(b, lc, LANES), gd_l.reshape(b, l, LANES)
    ml_c3, ml_l3 = ml_c.reshape(b, lc, LANES), ml_l.reshape(b, l, LANES)
    wbg, wbm, wo = (w_branch_gdn[layer].astype(BF16), w_branch_ml[layer].astype(BF16), w_out[layer].astype(BF16))
    w_sh_gu, w_sh_dn = w_sh_gate_up[layer].astype(BF16), w_sh_down[layer].astype(BF16)
    outs = []
    for part in range(BATCH_PARTS):
        b0, tok0 = part * nb, part * tp
        y_gdn = _gdn(proj_c3, proj_l3, conv_w8, gd_c3, gd_l3, gdn_norm_g[layer], b0, nb)
        q_cm = _col_major(proj_l3[b0:b0 + nb, :, COL_ML_Q:COL_ML_Q + HEADS * ML_DK])
        k_cm = _col_major(proj_l3[b0:b0 + nb, :, COL_ML_K:COL_ML_K + HEADS * ML_DK])
        v_cm = _col_major(proj_l3[b0:b0 + nb, :, COL_ML_V:COL_ML_V + HEADS * HEAD_V])
        h_ml = _row_major(_mlstm(proj_c3, q_cm, k_cm, v_cm, ml_c3, ml_l3, b0))
        x1, h2, scores_t = _merge(y_gdn.reshape(tp, d), h_ml.reshape(tp, d), proj_l, x2d, mod3, l, ml_norm_g[layer],
                                norm2_g[layer], wbg, wbm, wo, wr_hi, wr_lo, tok0, tm=min(512, l))
        idx, wts, rank, cnt = _route(scores_t, router_bias[layer])
        counts = cnt[:, 0].astype(jnp.int32)
        pstart, blk_expert, n_used, next_expert = _block_table(counts, n_blocks)
        pos = _slots(idx, rank, pstart).reshape(n_assign)
        tok_ids = jnp.broadcast_to((jnp.arange(n_assign, dtype=jnp.int32) % tp)[:, None], (n_assign, LANES))
        scattered = _sc_scatter_rows(tok_ids, pos, n_slots)[:, 0]
        in_expert = (jnp.arange(n_slots, dtype=jnp.int32).reshape(n_blocks, EXPERT_ROWS)
                     - pstart[blk_expert][:, None])
        valid = (in_expert < counts[blk_expert][:, None]).reshape(n_slots)
        tok_slot = jnp.where(valid, scattered, jnp.arange(n_slots, dtype=jnp.int32) % tp)
        xb = h2.at[tok_slot + tok0].get(mode="promise_in_bounds")
        yb = _experts(xb, blk_expert, n_used, next_expert, w_exp_gate_up[layer], w_exp_down[layer])
        yg = yb.at[pos].get(mode="promise_in_bounds", unique_indices=True).reshape(TOP_K, tp, d)
        outs.append(_final(x1, h2, yg, wts.T, mod3, l, w_sh_gu, w_sh_dn, final_norm_g, tok0, tm=min(256, l)))
    return jnp.concatenate(outs, axis=0).reshape(b, l, d)
```

```python
import functools

import jax
import jax.numpy as jnp
from jax import lax
from jax.experimental import pallas as pl
from jax.experimental.pallas import tpu as pltpu
from jax.experimental.pallas import tpu_sc as plsc

F32 = jnp.float32
BF16 = jnp.bfloat16
HI = lax.Precision.HIGHEST

EPS = 1e-6
CHUNK = 64
GRID_W = 64
HEADS = 8
HEAD_V = 128
GDN_DK = 128
ML_DK = 64
GDN_CONV = 5
N_EXPERTS = 256
TOP_K = 8
N_GROUPS = 8
TOPK_GROUPS = 4
ROUTED_SCALE = 2.5
EXPERT_ROWS = 512
BATCH_PARTS = 1
GDN_STEPS = 4
GDN_CTX_STEPS = 4
RING = 2 * GDN_STEPS
LANES = 128
VMEM_LIMIT = 56 * 1024 * 1024

COL_GDN_QKV = 0
COL_GDN_Z = 3072
COL_ML_Q = 4096
COL_ML_K = 4608
COL_ML_V = 5120
COL_ML_O = 6144
COL_MG_GDN = 7168
COL_MG_ML = 8192
N_MAIN = 9216
_ORIG = dict(gdn_qkv=(0, 3072), gdn_z=(3072, 4096), gdn_gate=(4096, 4128), ml_q=(4128, 4640),
             ml_k=(4640, 5152), ml_v=(5152, 6176), ml_o=(6176, 7200), ml_gate=(7200, 7232),
             mg_gdn=(7232, 8256), mg_ml=(8256, 9280))


def _params(sem, vmem=VMEM_LIMIT):
    return pltpu.CompilerParams(dimension_semantics=sem, vmem_limit_bytes=vmem)


def _dot(a, b, precision=None):
    return jnp.dot(a, b, preferred_element_type=F32, precision=precision)


def _dot_nt(a, b, precision=None):
    return lax.dot_general(a, b, (((1,), (1,)), ((), ())), preferred_element_type=F32, precision=precision)


def _dot_tn(a, b, precision=None):
    return lax.dot_general(a, b, (((0,), (0,)), ((), ())), preferred_element_type=F32, precision=precision)


def _sigmoid(x):
    return 1.0 / (1.0 + jnp.exp(-x))


def _softplus(x):
    return jnp.maximum(x, 0.0) + jnp.log(1.0 + jnp.exp(-jnp.abs(x)))


def _ada_kernel(c_ref, w_ref, b_ref, o_ref):
    c = c_ref[...]
    sc = c * _sigmoid(c)
    o_ref[...] = _dot(sc, w_ref[...], HI) + b_ref[...]


def _ada_mod(cc, w_ada, b_ada, tn=1536):
    r, d = cc.shape
    n = w_ada.shape[1]
    return pl.pallas_call(
        _ada_kernel,
        grid=(n // tn,),
        in_specs=[pl.BlockSpec((r, d), lambda j: (0, 0)),
                  pl.BlockSpec((d, tn), lambda j: (0, j)),
                  pl.BlockSpec((1, tn), lambda j: (0, j))],
        out_specs=pl.BlockSpec((r, tn), lambda j: (0, j)),
        out_shape=jax.ShapeDtypeStruct((r, n), F32),
        compiler_params=_params(("arbitrary",)),
    )(cc, w_ada, b_ada.reshape(1, n))


def _proj_kernel(x_ref, mod_ref, g_ref, w_ref, wg_ref, o_ref, og_ref, hn_ref):
    d = x_ref.shape[1]

    @pl.when(pl.program_id(1) == 0)
    def _():
        x = x_ref[...]
        y = x * lax.rsqrt(jnp.mean(x * x, axis=-1, keepdims=True) + EPS) * g_ref[...]
        shift = mod_ref[0, :, 0:d]
        scale = mod_ref[0, :, d:2 * d]
        h = (y * (1.0 + scale) + shift).astype(BF16)
        hn_ref[...] = h
        og_ref[...] = _dot(h, wg_ref[...])

    o_ref[...] = _dot(hn_ref[...], w_ref[...]).astype(o_ref.dtype)


def _project(x2d, mod3, mod_row_of_tile, norm_g, w_main, w_gate, tm, tn=1024):
    t, d = x2d.shape
    n = w_main.shape[1]
    return pl.pallas_call(
        _proj_kernel,
        grid=(t // tm, n // tn),
        in_specs=[pl.BlockSpec((tm, d), lambda i, j: (i, 0)),
                  pl.BlockSpec((1, 1, mod3.shape[2]), lambda i, j: (mod_row_of_tile(i), 0, 0)),
                  pl.BlockSpec((1, d), lambda i, j: (0, 0)),
                  pl.BlockSpec((d, tn), lambda i, j: (0, j)),
                  pl.BlockSpec((d, LANES), lambda i, j: (0, 0))],
        out_specs=[pl.BlockSpec((tm, tn), lambda i, j: (i, j)),
                   pl.BlockSpec((tm, LANES), lambda i, j: (i, 0))],
        out_shape=[jax.ShapeDtypeStruct((t, n), BF16), jax.ShapeDtypeStruct((t, LANES), F32)],
        scratch_shapes=[pltpu.VMEM((tm, d), BF16)],
        compiler_params=_params(("arbitrary", "arbitrary")),
    )(x2d, mod3, norm_g.reshape(1, d), w_main, w_gate)


def _gate_kernel(g_ref, p_ref, gd_ref, ml_ref):
    rows = g_ref.shape[0]
    raw = g_ref[...] + p_ref[0:1, :]
    lane = lax.broadcasted_iota(jnp.int32, raw.shape, 1)
    sp = _softplus(raw)
    vals = jnp.where(lane < 16, _sigmoid(raw),
                     jnp.where(lane < 32, p_ref[1:2, :] * sp,
                               jnp.where(lane < 48, raw,
                                         jnp.where(lane < 64, -_softplus(-raw), 0.0))))
    ri = lax.broadcasted_iota(jnp.int32, (CHUNK, CHUNK), 0)
    ci = lax.broadcasted_iota(jnp.int32, (CHUNK, CHUNK), 1)
    tri_f = (ri >= ci).astype(F32)
    tri_b = (ri <= ci).astype(F32)
    lane_c = lax.broadcasted_iota(jnp.int32, (CHUNK, LANES), 1)
    row_c = lax.broadcasted_iota(jnp.int32, (CHUNK, LANES), 0)
    fwd_lane = (lane_c % 16) < 8
    for c in range(rows // CHUNK):
        blk = vals[c * CHUNK:(c + 1) * CHUNK, :]
        cum = jnp.where(fwd_lane, _dot(tri_f, blk, HI), _dot(tri_b, blk, HI))
        gd_ref[c * CHUNK:(c + 1) * CHUNK, :] = jnp.where(lane_c < 16, blk, jnp.where(lane_c < 32, cum, 0.0))
        bcum = pltpu.roll(cum, LANES - 16, axis=1)
        gmb = blk - bcum
        cmf, cmb = gmb, gmb
        for s in (1, 2, 4, 8, 16, 32):
            cmf = jnp.maximum(cmf, jnp.where(row_c >= s, pltpu.roll(cmf, s, axis=0), -jnp.inf))
            cmb = jnp.maximum(cmb, jnp.where(row_c < CHUNK - s, pltpu.roll(cmb, CHUNK - s, axis=0), -jnp.inf))
        cm = jnp.where(fwd_lane, cmf, cmb)
        ml = jnp.where(lane_c < 16, pltpu.roll(gmb, LANES - 32, axis=1),
                       jnp.where(lane_c < 32, pltpu.roll(cm, LANES - 16, axis=1),
                                 jnp.where(lane_c < 48, bcum, 0.0)))
        ml_ref[c * CHUNK:(c + 1) * CHUNK, :] = ml


def _gate_prep(graw, gparams, tm=256):
    t = graw.shape[0]
    spec = pl.BlockSpec((tm, LANES), lambda i: (i, 0))
    return pl.pallas_call(
        _gate_kernel,
        grid=(t // tm,),
        in_specs=[spec, pl.BlockSpec((8, LANES), lambda i: (0, 0))],
        out_specs=[spec, spec],
        out_shape=[jax.ShapeDtypeStruct((t, LANES), F32)] * 2,
        compiler_params=_params(("arbitrary",)),
    )(graw, gparams)


def _split3(a):
    h = a.astype(BF16)
    r = a - h.astype(F32)
    m = r.astype(BF16)
    return h, m, (r - m.astype(F32)).astype(BF16)


def _lane_picks(x, lanes):
    li = lax.broadcasted_iota(jnp.int32, (LANES, LANES), 0)
    ci = lax.broadcasted_iota(jnp.int32, (LANES, LANES), 1)
    want = jnp.full((LANES, LANES), -1, jnp.int32)
    for j, lane in enumerate(lanes):
        want = jnp.where(ci == j, lane, want)
    sel = (li == want).astype(BF16)
    h, m, lo = _split3(x)
    cols = _dot(h, sel) + (_dot(m, sel) + _dot(lo, sel))
    return [jnp.broadcast_to(cols[:, j:j + 1], x.shape) for j in range(len(lanes))]


def _dir_masks(direction):
    ri = lax.broadcasted_iota(jnp.int32, (CHUNK, CHUNK), 0)
    ci = lax.broadcasted_iota(jnp.int32, (CHUNK, CHUNK), 1)
    if direction == 0:
        return ri >= ci, ri > ci
    return ri <= ci, ri < ci


def _gdn_kernel(qc_ref, kc_ref, vc_ref, ql_ref, kl_ref, vl_ref, z_ref, cwq_ref, cwk_ref, cwv_ref,
                gdc_ref, gdl_ref, ng_ref, y_ref,
                xpad, qs, ks, vs, beta_t, cg_t, wq_r, u_r, kd_r, qk_r, dc_r, out_s):
    lc = qc_ref.shape[1]
    ll = ql_ref.shape[1]
    lt = lc + ll
    n_c, n_l = lc // CHUNK, ll // CHUNK
    n_t = n_c + n_l
    rb = 256

    def l2n(x):
        return x * lax.rsqrt(jnp.sum(x * x, axis=-1, keepdims=True) + EPS)

    def prep(src_ref, cw_ref, dst, off, ls, kind):
        xpad[0:8, :] = jnp.zeros((8, LANES), F32)
        xpad[8:8 + ls, :] = src_ref[0].astype(F32)
        xpad[8 + ls:16 + ls, :] = jnp.zeros((8, LANES), F32)
        step = min(rb, ls)
        for r0 in range(0, ls, step):
            acc = jnp.zeros((step, LANES), F32)
            for t in range(GDN_CONV):
                s0 = r0 + 8 - GDN_CONV // 2 + t
                acc = acc + cw_ref[t:t + 1, :] * xpad[s0:s0 + step, :]
            y = acc * _sigmoid(acc)
            if kind == "q":
                y = l2n(y) * (GDN_DK ** -0.5)
            elif kind == "k":
                y = l2n(y)
            dst[off + r0:off + r0 + step, :] = y

    prep(qc_ref, cwq_ref, qs, 0, lc, "q")
    prep(kc_ref, cwk_ref, ks, 0, lc, "k")
    prep(vc_ref, cwv_ref, vs, 0, lc, "v")
    prep(ql_ref, cwq_ref, qs, lc, ll, "q")
    prep(kl_ref, cwk_ref, ks, lc, ll, "k")
    prep(vl_ref, cwv_ref, vs, lc, ll, "v")

    head = pl.program_id(1)

    def build_tables(src_ref, off, ls):
        step = min(rb, ls)
        for r0 in range(0, ls, step):
            picked = _lane_picks(src_ref[0, r0:r0 + step, :], [8 * d + head for d in range(2)]
                                 + [16 + 8 * d + head for d in range(2)])
            for d in range(2):
                beta_t[d, off + r0:off + r0 + step, :] = picked[d]
                cg_t[d, off + r0:off + r0 + step, :] = picked[2 + d]

    build_tables(gdc_ref, 0, lc)
    build_tables(gdl_ref, lc, ll)

    def bwd_chunk(t):
        return jnp.where(t < n_c, n_c - 1 - t, n_t + n_c - 1 - t)

    row_p = lax.broadcasted_iota(jnp.int32, (CHUNK, LANES), 0)
    lane_p = lax.broadcasted_iota(jnp.int32, (CHUNK, LANES), 1)
    fwd_p = lane_p < CHUNK
    col_p = jnp.where(fwd_p, lane_p, lane_p - CHUNK)
    signed = jnp.where(fwd_p, row_p - col_p, col_p - row_p)
    incl_p = signed >= 0
    strict_p = signed > 0
    eye_p = (row_p == col_p).astype(F32)
    keep_f = fwd_p.astype(BF16)
    keep_b = (1.0 - fwd_p.astype(F32)).astype(BF16)

    def block_diag(top, bottom):
        zero = jnp.zeros(top.shape, top.dtype)
        return jnp.concatenate([jnp.concatenate([top, zero], axis=1), jnp.concatenate([zero, bottom], axis=1)], axis=0)

    def pair_diag(xp):
        return jnp.concatenate([xp * keep_f, xp * keep_b], axis=0)

    def pair_times3(a, xp):
        ah = a.astype(BF16)
        al = (a - ah.astype(F32)).astype(BF16)
        xh = xp.astype(BF16)
        xl = (xp - xh.astype(F32)).astype(BF16)
        dh = pair_diag(xh)
        return _dot(ah, dh) + (_dot(ah, pair_diag(xl)) + _dot(al, dh))

    def prep_chain(t):
        tc = jnp.minimum(t, n_t - 1)
        q, k, v, beta, cgc, kb, ecg = [], [], [], [], [], [], []
        for d in range(2):
            c = tc if d == 0 else bwd_chunk(tc)
            rows = pl.ds(pl.multiple_of(c * CHUNK, CHUNK), CHUNK)
            q.append(qs[rows, :])
            k.append(ks[rows, :])
            v.append(vs[rows, :])
            beta.append(beta_t[d, rows, :])
            cgc.append(cg_t[d, rows, :])
        kdiag = block_diag(k[0].astype(BF16), k[1].astype(BF16))
        kk = _dot_nt(jnp.concatenate([k[0], k[1]], axis=1).astype(BF16), kdiag)
        qk = _dot_nt(jnp.concatenate([q[0], q[1]], axis=1).astype(BF16), kdiag)
        yield
        beta_p = jnp.where(fwd_p, beta[0], beta[1])
        cg_p = jnp.where(fwd_p, cgc[0], cgc[1])
        cgr_p = jnp.transpose(jnp.concatenate([cgc[0], cgc[1]], axis=0))[0:CHUNK, :]
        decay = jnp.exp(jnp.where(incl_p, cg_p - cgr_p, -jnp.inf))
        qkd = qk * decay
        for d in range(2):
            slot = (t % RING) * 2 + d
            last = CHUNK - 1 if d == 0 else 0
            cg_last = cgc[d][last:last + 1, :]
            ecg.append(jnp.exp(cgc[d]))
            kb.append(k[d] * beta[d])
            qk_r[slot] = qkd[:, d * CHUNK:(d + 1) * CHUNK]
            wq_r[slot, CHUNK:2 * CHUNK, :] = q[d] * ecg[d]
            kd_r[slot] = k[d] * jnp.exp(cg_last - cgc[d])
            dc_r[slot] = jnp.broadcast_to(jnp.exp(cg_last), (8, LANES))
        x = jnp.where(strict_p, -(beta_p * kk) * decay, 0.0)
        tinv = eye_p + x
        x = pair_times3(x, x)
        yield
        for _ in range(4):
            both = pair_times3(jnp.concatenate([tinv, x], axis=0), x)
            tinv, x = tinv + both[0:CHUNK, :], both[CHUNK:2 * CHUNK, :]
            yield
        tinv = (tinv + pair_times3(tinv, x)).astype(BF16)
        yield
        w = _dot(tinv, block_diag((kb[0] * ecg[0]).astype(BF16), (kb[1] * ecg[1]).astype(BF16)))
        u = _dot(tinv, block_diag((v[0] * beta[0]).astype(BF16), (v[1] * beta[1]).astype(BF16)))
        for d in range(2):
            slot = (t % RING) * 2 + d
            wq_r[slot, 0:CHUNK, :] = w[:, d * LANES:(d + 1) * LANES]
            u_r[slot] = u[:, d * LANES:(d + 1) * LANES]

    out_s[...] = jnp.zeros(out_s.shape, F32)

    def scan_chain(d, t0, steps, s, with_out, result):
        for j in range(steps):
            t = t0 + j
            slot = (t % RING) * 2 + d
            ws = _dot(wq_r[slot], s)
            yield
            v_new = u_r[slot] - ws[0:CHUNK, :]
            if with_out:
                c = t if d == 0 else bwd_chunk(t)
                o = ws[CHUNK:2 * CHUNK, :] + _dot(qk_r[slot], v_new)
                l0 = pl.multiple_of((c - n_c) * CHUNK, CHUNK)
                out_s[pl.ds(l0, CHUNK), :] += o
            s = s * dc_r[slot][0:1, :] + _dot_tn(kd_r[slot], v_new)
            yield
        result[d] = s

    def lockstep(chains):
        chains = list(chains)
        while chains:
            alive = []
            for ch in chains:
                try:
                    next(ch)
                    alive.append(ch)
                except StopIteration:
                    pass
            chains = alive

    def group_body(i, carry, t_base, steps, ahead, with_out):
        t0 = t_base + steps * i
        result = [None, None]
        lockstep([scan_chain(d, t0, steps, carry[d], with_out, result) for d in range(2)]
                 + [prep_chain(t0 + steps + j) for j in range(ahead)])
        return result[0], result[1]

    def run_groups(carry, t_base, n_steps, steps, ahead_last, with_out):
        n_groups = n_steps // steps
        carry = lax.fori_loop(0, n_groups - 1, functools.partial(
            group_body, t_base=t_base, steps=steps, ahead=steps, with_out=with_out), carry)
        return group_body(n_groups - 1, carry, t_base, steps, ahead_last, with_out)

    lockstep([prep_chain(j) for j in range(GDN_CTX_STEPS)])
    s0 = jnp.zeros((GDN_DK, HEAD_V), F32)
    carry = run_groups((s0, s0), 0, n_c, GDN_CTX_STEPS, GDN_STEPS, False)
    run_groups(carry, n_c, n_l, GDN_STEPS, 0, True)


    def out_body(i, carry):
        r0 = pl.multiple_of(i * rb, rb)
        o = out_s[pl.ds(r0, rb), :]
        z = z_ref[0, pl.ds(r0, rb), :].astype(F32)
        y = o * lax.rsqrt(jnp.mean(o * o, axis=-1, keepdims=True) + EPS) * ng_ref[...]
        y_ref[0, pl.ds(r0, rb), :] = (y * (z * _sigmoid(z))).astype(y_ref.dtype)
        return carry

    lax.fori_loop(0, ll // rb, out_body, 0)


def _gdn(proj_c, proj_l, conv_w8, gd_c, gd_l, norm_g, b0, nb):
    lc = proj_c.shape[1]
    ll = proj_l.shape[1]
    lt = lc + ll
    qb, kb_, vb, zb = (COL_GDN_QKV // LANES, COL_GDN_QKV // LANES + HEADS, COL_GDN_QKV // LANES + 2 * HEADS,
                       COL_GDN_Z // LANES)

    def seq_spec(l, col0):
        return pl.BlockSpec((1, l, LANES), lambda i, h: (i + b0, 0, col0 + h))

    def cw_spec(col0):
        return pl.BlockSpec((8, LANES), lambda i, h: (0, col0 + h))

    return pl.pallas_call(
        _gdn_kernel,
        grid=(nb, HEADS),
        in_specs=[seq_spec(lc, qb), seq_spec(lc, kb_), seq_spec(lc, vb),
                  seq_spec(ll, qb), seq_spec(ll, kb_), seq_spec(ll, vb), seq_spec(ll, zb),
                  cw_spec(0), cw_spec(HEADS), cw_spec(2 * HEADS),
                  pl.BlockSpec((1, lc, LANES), lambda i, h: (i + b0, 0, 0)),
                  pl.BlockSpec((1, ll, LANES), lambda i, h: (i + b0, 0, 0)),
                  pl.BlockSpec((1, LANES), lambda i, h: (0, 0))],
        out_specs=pl.BlockSpec((1, ll, LANES), lambda i, h: (i, 0, h)),
        out_shape=jax.ShapeDtypeStruct((nb, ll, HEADS * HEAD_V), BF16),
        scratch_shapes=[pltpu.VMEM((max(lc, ll) + 16, LANES), F32),
                        pltpu.VMEM((lt, LANES), F32), pltpu.VMEM((lt, LANES), F32), pltpu.VMEM((lt, LANES), F32),
                        pltpu.VMEM((2, lt, LANES), F32), pltpu.VMEM((2, lt, LANES), F32),
                        pltpu.VMEM((2 * RING, 2 * CHUNK, LANES), F32),
                        pltpu.VMEM((2 * RING, CHUNK, LANES), F32), pltpu.VMEM((2 * RING, CHUNK, LANES), F32),
                        pltpu.VMEM((2 * RING, CHUNK, CHUNK), F32),
                        pltpu.VMEM((2 * RING, 8, LANES), F32),
                        pltpu.VMEM((ll, LANES), F32)],
        compiler_params=_params(("arbitrary", "arbitrary")),
    )(proj_c, proj_c, proj_c, proj_l, proj_l, proj_l, proj_l, conv_w8, conv_w8, conv_w8,
      gd_c, gd_l, norm_g.reshape(1, LANES))


def _mlstm_kernel(qc_ref, kc_ref, vc_ref, ql_ref, kl_ref, vl_ref, mlc_ref, mll_ref, h_ref, out_s, tabs):
    lc = qc_ref.shape[1]
    ll = ql_ref.shape[1]
    n_c, n_l = lc // CHUNK, ll // CHUNK
    n_t = n_c + n_l
    lt = lc + ll
    pair = pl.program_id(1)
    lane = lax.broadcasted_iota(jnp.int32, (CHUNK, LANES), 1)
    ones_v = jnp.ones((CHUNK, HEAD_V), BF16)
    chains = [(hh, d) for hh in range(2) for d in range(2)]
    hmask = [((lane // ML_DK) == hh).astype(F32) for hh in range(2)]
    incl = [_dir_masks(d)[0] for d in range(2)]

    def build_tables(src_ref, off, ls):
        step = min(256, ls)
        for r0 in range(0, ls, step):
            lanes = [16 * j + 8 * d + 2 * pair + hh for hh, d in chains for j in range(3)]
            for g, tab in enumerate(_lane_picks(src_ref[0, r0:r0 + step, :], lanes)):
                tabs[g, off + r0:off + r0 + step, :] = tab

    build_tables(mlc_ref, 0, lc)
    build_tables(mll_ref, lc, ll)

    def wide(a):
        return jnp.concatenate([a, a], axis=1)

    def chain(hh, d, c, state, is_ctx, result):
        cs, ms = state
        last = CHUNK - 1 if d == 0 else 0
        if is_ctx:
            rows = pl.ds(pl.multiple_of(c * CHUNK, CHUNK), CHUNK)
            q_ref, k_ref, v_ref = qc_ref, kc_ref, vc_ref
        else:
            rows = pl.ds(pl.multiple_of((c - n_c) * CHUNK, CHUNK), CHUNK)
            q_ref, k_ref, v_ref = ql_ref, kl_ref, vl_ref
        q = (q_ref[0, rows, :].astype(F32) * hmask[hh]).astype(BF16)
        k = k_ref[0, rows, :].astype(F32) * (hmask[hh] * (ML_DK ** -0.5))
        v = jnp.concatenate([v_ref[0, rows, hh * HEAD_V:(hh + 1) * HEAD_V], ones_v], axis=1)
        n = chains.index((hh, d))
        trows = pl.ds(pl.multiple_of(c * CHUNK, CHUNK), CHUNK)
        gmb = tabs[3 * n, trows, :]
        gmb_t = jnp.transpose(gmb)[0:CHUNK, :]
        cm = tabs[3 * n + 1, trows, :]
        bc = tabs[3 * n + 2, trows, :]
        qk = _dot_nt(q, k.astype(BF16))
        yield
        cm_last = cm[last:last + 1, :]
        b_last = bc[last:last + 1, :]
        mm = jnp.maximum(ms, cm)
        p = jnp.where(incl[d], jnp.exp(gmb_t - mm[:, 0:CHUNK]), 0.0) * qk
        wk = (k * jnp.exp(gmb - cm_last)).astype(BF16)
        inter = _dot(q, cs.astype(BF16))
        intra = _dot(p.astype(BF16), v)
        c_loc = _dot_tn(wk, v)
        yield
        if not is_ctx:
            nd = wide(jnp.exp(ms - mm)) * inter + intra
            hv = nd[:, 0:HEAD_V] / jnp.maximum(jnp.abs(nd[:, HEAD_V:2 * HEAD_V]), jnp.exp(-(bc + mm)))
            l0 = pl.multiple_of((c - n_c) * CHUNK, CHUNK)
            out_s[pl.ds(l0, CHUNK), hh * HEAD_V:(hh + 1) * HEAD_V] += hv
        mx = jnp.maximum(ms, cm_last)
        result[hh, d] = (wide(jnp.exp(ms - mx)) * cs + wide(jnp.exp(cm_last - mx)) * c_loc, b_last + mx)

    out_s[...] = jnp.zeros(out_s.shape, F32)

    def run(chains_iter):
        live = list(chains_iter)
        while live:
            alive = []
            for ch in live:
                try:
                    next(ch)
                    alive.append(ch)
                except StopIteration:
                    pass
            live = alive

    def body(i, carry, is_ctx):
        result = {}
        gens = []
        for n, (hh, d) in enumerate(chains):
            if is_ctx:
                c = i if d == 0 else n_c - 1 - i
            else:
                c = n_c + i if d == 0 else n_t - 1 - i
            gens.append(chain(hh, d, c, carry[n], is_ctx, result))
        run(gens)
        return tuple(result[hd] for hd in chains)

    st0 = (jnp.zeros((LANES, 2 * HEAD_V), F32), jnp.zeros((1, LANES), F32))
    carry = lax.fori_loop(0, n_c, functools.partial(body, is_ctx=True), (st0,) * 4)
    lax.fori_loop(0, n_l, functools.partial(body, is_ctx=False), carry)
    h_ref[0] = out_s[...].astype(h_ref.dtype)


def _mlstm(proj_c, q_l, k_l, v_l, ml_c, ml_l, b0):
    lc = proj_c.shape[1]
    nb, ll, _ = q_l.shape
    lt = lc + ll
    qb, kb_, vb = COL_ML_Q // LANES, COL_ML_K // LANES, COL_ML_V // (2 * HEAD_V)
    return pl.pallas_call(
        _mlstm_kernel,
        grid=(nb, HEADS // 2),
        in_specs=[pl.BlockSpec((1, lc, LANES), lambda i, p: (i + b0, 0, qb + p)),
                  pl.BlockSpec((1, lc, LANES), lambda i, p: (i + b0, 0, kb_ + p)),
                  pl.BlockSpec((1, lc, 2 * HEAD_V), lambda i, p: (i + b0, 0, vb + p)),
                  pl.BlockSpec((1, ll, LANES), lambda i, p: (i, 0, p)),
                  pl.BlockSpec((1, ll, LANES), lambda i, p: (i, 0, p)),
                  pl.BlockSpec((1, ll, 2 * HEAD_V), lambda i, p: (i, 0, p)),
                  pl.BlockSpec((1, lc, LANES), lambda i, p: (i + b0, 0, 0)),
                  pl.BlockSpec((1, ll, LANES), lambda i, p: (i + b0, 0, 0))],
        out_specs=pl.BlockSpec((1, ll, 2 * HEAD_V), lambda i, p: (i, 0, p)),
        out_shape=jax.ShapeDtypeStruct((nb, ll, HEADS * HEAD_V), BF16),
        scratch_shapes=[pltpu.VMEM((ll, 2 * HEAD_V), F32), pltpu.VMEM((12, lt, LANES), F32)],
        compiler_params=_params(("arbitrary", "arbitrary")),
    )(proj_c, proj_c, proj_c, q_l, k_l, v_l, ml_c, ml_l)


def _merge_kernel(yg_ref, hm_ref, o_ref, gg_ref, gm_ref, x_ref, mod_ref, mlg_ref, n2_ref,
                  wbg_ref, wbm_ref, wo_ref, wrh_ref, wrl_ref, x1_ref, h2_ref, sc_ref):
    d = x_ref.shape[1]
    o = o_ref[...].astype(F32)
    ym = _sigmoid(o) * hm_ref[...].astype(F32)
    segs = []
    for h in range(HEADS):
        seg = ym[:, h * HEAD_V:(h + 1) * HEAD_V]
        segs.append(seg * lax.rsqrt(jnp.mean(seg * seg, axis=-1, keepdims=True) + EPS))
    ymn = jnp.concatenate(segs, axis=1) * mlg_ref[...]
    y_gdn = _dot(yg_ref[...], wbg_ref[...])
    y_ml = _dot(ymn.astype(BF16), wbm_ref[...])
    mixed = _sigmoid(gg_ref[...].astype(F32)) * y_gdn + _sigmoid(gm_ref[...].astype(F32)) * y_ml
    y = _dot(mixed.astype(BF16), wo_ref[...])
    x1 = x_ref[...] + mod_ref[0, :, 2 * d:3 * d] * y
    x1_ref[...] = x1
    hn = x1 * lax.rsqrt(jnp.mean(x1 * x1, axis=-1, keepdims=True) + EPS) * n2_ref[...]
    h2 = hn * (1.0 + mod_ref[0, :, 4 * d:5 * d]) + mod_ref[0, :, 3 * d:4 * d]
    h2_hi = h2.astype(BF16)
    h2_ref[...] = h2_hi
    h2_lo = (h2 - h2_hi.astype(F32)).astype(BF16)
    logits = _dot_nt(wrh_ref[...], h2_hi) + (_dot_nt(wrl_ref[...], h2_hi) + _dot_nt(wrh_ref[...], h2_lo))
    sc_ref[...] = _sigmoid(logits)


def _merge(y_gdn, h_ml, proj_l2d, x2d, mod3, rows_per_mod, ml_norm_g, norm2_g, wbg, wbm, wo, wr_hi, wr_lo, tok0,
           tm=512):
    t, d = y_gdn.shape
    e = wr_hi.shape[0]
    off = tok0 // tm
    row = lambda i: (i, 0)
    const = lambda i: (0, 0)
    return pl.pallas_call(
        _merge_kernel,
        grid=(t // tm,),
        in_specs=[pl.BlockSpec((tm, d), row), pl.BlockSpec((tm, d), row),
                  pl.BlockSpec((tm, d), lambda i: (i + off, COL_ML_O // d)),
                  pl.BlockSpec((tm, d), lambda i: (i + off, COL_MG_GDN // d)),
                  pl.BlockSpec((tm, d), lambda i: (i + off, COL_MG_ML // d)),
                  pl.BlockSpec((tm, d), lambda i: (i + off, 0)),
                  pl.BlockSpec((1, 1, mod3.shape[2]), lambda i: (((i + off) * tm) // rows_per_mod, 0, 0)),
                  pl.BlockSpec((1, d), const), pl.BlockSpec((1, d), const),
                  pl.BlockSpec((d, d), const), pl.BlockSpec((d, d), const), pl.BlockSpec((d, d), const),
                  pl.BlockSpec((e, d), const), pl.BlockSpec((e, d), const)],
        out_specs=[pl.BlockSpec((tm, d), row), pl.BlockSpec((tm, d), lambda i: (i + off, 0)),
                   pl.BlockSpec((e, tm), lambda i: (0, i))],
        out_shape=[jax.ShapeDtypeStruct((t, d), F32), jax.ShapeDtypeStruct((x2d.shape[0], d), BF16),
                   jax.ShapeDtypeStruct((e, t), F32)],
        compiler_params=_params(("arbitrary",)),
    )(y_gdn, h_ml, proj_l2d, proj_l2d, proj_l2d, x2d, mod3, ml_norm_g.reshape(1, d), norm2_g.reshape(1, d),
      wbg, wbm, wo, wr_hi, wr_lo)


def _expert_kernel(be_ref, nu_ref, nx_ref, x_ref, wgu_hbm, wd_hbm, y_ref, land_gu, land_d, wgu_s, wd_s, sems):
    i = pl.program_id(0)
    de = wd_hbm.shape[1]
    used = i < nu_ref[0]
    first = jnp.logical_or(i == 0, be_ref[i] != be_ref[jnp.maximum(i - 1, 0)])

    def weight_copies(ex):
        return (pltpu.make_async_copy(wgu_hbm.at[ex], land_gu, sems.at[0]),
                pltpu.make_async_copy(wd_hbm.at[ex], land_d, sems.at[1]))

    @pl.when(i == 0)
    def _():
        for cp in weight_copies(be_ref[0]):
            cp.start()

    @pl.when(jnp.logical_and(first, used))
    def _():
        for cp in weight_copies(be_ref[i]):
            cp.wait()
        wgu_s[...] = land_gu[...].astype(BF16)
        wd_s[...] = land_d[...].astype(BF16)

        @pl.when(nx_ref[i] >= 0)
        def _():
            for cp in weight_copies(nx_ref[i]):
                cp.start()

    @pl.when(used)
    def _():
        gu = _dot(x_ref[...], wgu_s[...])
        g = gu[:, 0:de]
        act = (g * _sigmoid(g)) * gu[:, de:2 * de]
        y_ref[...] = _dot(act.astype(BF16), wd_s[...]).astype(y_ref.dtype)

    @pl.when(jnp.logical_not(used))
    def _():
        y_ref[...] = jnp.zeros(y_ref.shape, y_ref.dtype)


def _experts(xb, blk_expert, n_used, next_expert, w_gu, w_down):
    n_slots, d = xb.shape
    n_blocks = n_slots // EXPERT_ROWS
    e, _, de2 = w_gu.shape
    de = de2 // 2
    return pl.pallas_call(
        _expert_kernel,
        grid_spec=pltpu.PrefetchScalarGridSpec(
            num_scalar_prefetch=3,
            grid=(n_blocks,),
            in_specs=[pl.BlockSpec((EXPERT_ROWS, d), lambda i, be, nu, nx: (i, 0)),
                      pl.BlockSpec(memory_space=pl.ANY), pl.BlockSpec(memory_space=pl.ANY)],
            out_specs=pl.BlockSpec((EXPERT_ROWS, d), lambda i, be, nu, nx: (i, 0)),
            scratch_shapes=[pltpu.VMEM((d, de2), w_gu.dtype), pltpu.VMEM((de, d), w_down.dtype),
                            pltpu.VMEM((d, de2), BF16), pltpu.VMEM((de, d), BF16),
                            pltpu.SemaphoreType.DMA((2,))]),
        out_shape=jax.ShapeDtypeStruct((n_slots, d), BF16),
        compiler_params=_params(("arbitrary",)),
    )(blk_expert, n_used, next_expert, xb, w_gu, w_down)


def _final_kernel(x1_ref, h2_ref, yg_ref, wt_ref, mod_ref, wsg_ref, wsd_ref, fg_ref, o_ref):
    d = x1_ref.shape[1]
    ds_ = wsd_ref.shape[0]
    wt = wt_ref[...]
    routed = jnp.zeros(x1_ref.shape, F32)
    for k in range(TOP_K):
        routed = routed + wt[:, k:k + 1] * yg_ref[k].astype(F32)
    gu = _dot(h2_ref[...], wsg_ref[...])
    g = gu[:, 0:ds_]
    sh = _dot(((g * _sigmoid(g)) * gu[:, ds_:2 * ds_]).astype(BF16), wsd_ref[...])
    x2 = x1_ref[...] + mod_ref[0, :, 5 * d:6 * d] * (routed + sh)
    o_ref[...] = x2 * lax.rsqrt(jnp.mean(x2 * x2, axis=-1, keepdims=True) + EPS) * fg_ref[...]


def _final(x1, h2, yg, wts, mod3, rows_per_mod, w_sh_gu, w_sh_down, final_g, tok0, tm=256):
    tp, d = x1.shape
    off = tok0 // tm
    row = lambda i: (i, 0)
    const = lambda i: (0, 0)
    return pl.pallas_call(
        _final_kernel,
        grid=(tp // tm,),
        in_specs=[pl.BlockSpec((tm, d), row), pl.BlockSpec((tm, d), lambda i: (i + off, 0)),
                  pl.BlockSpec((TOP_K, tm, d), lambda i: (0, i, 0)), pl.BlockSpec((tm, TOP_K), row),
                  pl.BlockSpec((1, 1, mod3.shape[2]), lambda i: (((i + off) * tm) // rows_per_mod, 0, 0)),
                  pl.BlockSpec(w_sh_gu.shape, const), pl.BlockSpec(w_sh_down.shape, const),
                  pl.BlockSpec((1, d), const)],
        out_specs=pl.BlockSpec((tm, d), row),
        out_shape=jax.ShapeDtypeStruct((tp, d), F32),
        compiler_params=_params(("arbitrary",)),
    )(x1, h2, yg, wts, mod3, w_sh_gu, w_sh_down, final_g.reshape(1, d))


def _route_kernel(sc_ref, bias_ref, tri_ref, idx_ref, wt_ref, rk_ref, cnt_ref, base_s):
    @pl.when(pl.program_id(0) == 0)
    def _():
        base_s[...] = jnp.zeros(base_s.shape, F32)

    scores = sc_ref[...]
    e, tn = scores.shape
    gsz = e // N_GROUPS
    sel3 = (scores + bias_ref[...]).reshape(N_GROUPS, gsz, tn)
    m1 = jnp.max(sel3, axis=1)
    is_max = sel3 == m1[:, None, :]
    n_max = jnp.sum(is_max.astype(F32), axis=1)
    m2 = jnp.max(jnp.where(is_max, -jnp.inf, sel3), axis=1)
    grp = m1 + jnp.where(n_max >= 2.0, m1, m2)
    gi = lax.broadcasted_iota(jnp.int32, (N_GROUPS, tn), 0)
    ahead = jnp.zeros((N_GROUPS, tn), F32)
    for g in range(N_GROUPS):
        row = grp[g:g + 1, :]
        ahead = ahead + jnp.logical_or(row > grp, jnp.logical_and(row == grp, g < gi)).astype(F32)
    ahead3 = jnp.broadcast_to(ahead[:, None, :], (N_GROUPS, gsz, tn))
    selm = jnp.where(ahead3 < float(TOPK_GROUPS), sel3, -jnp.inf).reshape(e, tn)
    ri = lax.broadcasted_iota(jnp.int32, (e, tn), 0).astype(F32)
    member = jnp.zeros((e, tn), F32)
    idxs, ws = [], []
    for _ in range(TOP_K):
        m = jnp.max(selm, axis=0, keepdims=True)
        idx = jnp.min(jnp.where(selm == m, ri, float(e)), axis=0, keepdims=True)
        hit = ri == idx
        ws.append(jnp.sum(jnp.where(hit, scores, 0.0), axis=0, keepdims=True))
        idxs.append(idx)
        selm = jnp.where(hit, -jnp.inf, selm)
        member = jnp.where(hit, 1.0, member)
    w = jnp.concatenate(ws, axis=0)
    wt_ref[...] = w / jnp.sum(w, axis=0, keepdims=True) * ROUTED_SCALE
    idx_ref[...] = jnp.concatenate(idxs, axis=0).astype(jnp.int32)
    cum = _dot(member.astype(BF16), tri_ref[...]) + base_s[...]
    rk_ref[...] = jnp.concatenate(
        [jnp.sum(jnp.where(ri == idx, cum, 0.0), axis=0, keepdims=True) for idx in idxs], axis=0).astype(jnp.int32)
    total = base_s[...] + jnp.sum(member, axis=1, keepdims=True)
    base_s[...] = total
    cnt_ref[...] = total


def _route(scores_t, router_bias):
    e, t = scores_t.shape
    tn = LANES
    bias = jnp.broadcast_to(router_bias.astype(F32)[:, None], (e, tn))
    tri = (jnp.arange(tn)[:, None] < jnp.arange(tn)[None, :]).astype(BF16)
    tok = pl.BlockSpec((TOP_K, tn), lambda i: (0, i))
    const = lambda i: (0, 0)
    return pl.pallas_call(
        _route_kernel,
        grid=(t // tn,),
        in_specs=[pl.BlockSpec((e, tn), lambda i: (0, i)), pl.BlockSpec((e, tn), const),
                  pl.BlockSpec((tn, tn), const)],
        out_specs=[tok, tok, tok, pl.BlockSpec((e, tn), const)],
        out_shape=[jax.ShapeDtypeStruct((TOP_K, t), jnp.int32), jax.ShapeDtypeStruct((TOP_K, t), F32),
                   jax.ShapeDtypeStruct((TOP_K, t), jnp.int32), jax.ShapeDtypeStruct((e, tn), F32)],
        scratch_shapes=[pltpu.VMEM((e, tn), F32)],
        compiler_params=_params(("arbitrary",)),
    )(scores_t, bias, tri)


def _slot_kernel(idx_ref, rk_ref, ps_ref, pos_ref):
    e, tn = ps_ref.shape
    ri = lax.broadcasted_iota(jnp.int32, (e, tn), 0)
    ps = ps_ref[...]
    rows = [jnp.sum(jnp.where(ri == idx_ref[k:k + 1, :], ps, 0.0), axis=0, keepdims=True) for k in range(TOP_K)]
    pos_ref[...] = rk_ref[...] + jnp.concatenate(rows, axis=0).astype(jnp.int32)


def _slots(idx, rank, pstart):
    k, t = idx.shape
    e = pstart.shape[0]
    tn = LANES
    tok = pl.BlockSpec((k, tn), lambda i: (0, i))
    return pl.pallas_call(
        _slot_kernel,
        grid=(t // tn,),
        in_specs=[tok, tok, pl.BlockSpec((e, tn), lambda i: (0, 0))],
        out_specs=tok,
        out_shape=jax.ShapeDtypeStruct((k, t), jnp.int32),
        compiler_params=_params(("arbitrary",)),
    )(idx, rank, jnp.broadcast_to(pstart.astype(F32)[:, None], (e, tn)))


def _sc_scatter_rows(vals, idx, n_rows, window=LANES):
    n, width = vals.shape
    mesh = plsc.VectorSubcoreMesh(core_axis_name="core", subcore_axis_name="subcore")

    @pl.kernel(out_type=jax.ShapeDtypeStruct((n_rows, width), vals.dtype), mesh=mesh, scratch_types=[])
    def scatter(v_hbm, i_hbm, o_hbm):
        def body(v_vmem, i_vmem):
            pltpu.sync_copy(v_vmem, o_hbm.at[i_vmem.at[0]])

        pltpu.emit_pipeline(
            body,
            grid=(n // window,),
            in_specs=[pl.BlockSpec((window, width), lambda i: (i, 0)),
                      pl.BlockSpec((1, window), lambda i: (0, i))],
            out_specs=[],
            core_axis_name=("core", "subcore"),
            dimension_semantics=(pltpu.PARALLEL,),
        )(v_hbm, i_hbm)

    return scatter(vals, idx.reshape(1, n))


def _block_table(counts, n_blocks):
    padded = (counts + EXPERT_ROWS - 1) // EXPERT_ROWS * EXPERT_ROWS
    pend = jnp.cumsum(padded)
    blk_expert = jnp.minimum(jnp.searchsorted(pend, jnp.arange(n_blocks) * EXPERT_ROWS, side='right'),
                             N_EXPERTS - 1).astype(jnp.int32)
    e_ids = jnp.arange(N_EXPERTS, dtype=jnp.int32)
    later = lax.cummin(jnp.where(counts > 0, e_ids, N_EXPERTS)[::-1])[::-1]
    next_tab = jnp.concatenate([later[1:], jnp.full((1,), N_EXPERTS, jnp.int32)])
    next_tab = jnp.where(next_tab < N_EXPERTS, next_tab, -1)
    return (pend - padded, blk_expert, (pend[-1] // EXPERT_ROWS).astype(jnp.int32).reshape(1),
            next_tab[blk_expert].astype(jnp.int32))


def _col_major(t):
    b, l, f = t.shape
    rows = l // GRID_W
    return t.reshape(b, rows, GRID_W, f).transpose(0, 2, 1, 3).reshape(b, l, f)


def _row_major(t):
    b, l, f = t.shape
    rows = l // GRID_W
    return t.reshape(b, GRID_W, rows, f).transpose(0, 2, 1, 3).reshape(b, l, f)


def kernel(x, c, ctx, c_ctx, w_ada, b_ada, norm1_g, norm2_g, w_in, gdn_conv_w, gdn_a_log, gdn_dt_bias, gdn_norm_g,
           ml_i_bias, ml_f_bias, ml_norm_g, w_branch_gdn, w_branch_ml, w_out, w_router, router_bias, w_exp_gate_up,
           w_exp_down, w_sh_gate_up, w_sh_down, final_norm_g):
    b, l, d = x.shape
    lc = ctx.shape[1]
    t = b * l
    layer = 0

    w = w_in[layer]
    main_cols = [_ORIG[k] for k in ("gdn_qkv", "gdn_z", "ml_q", "ml_k", "ml_v", "ml_o", "mg_gdn", "mg_ml")]
    w_main = jnp.concatenate([w[:, a:e] for a, e in main_cols], axis=1).astype(BF16)
    w_gate = jnp.concatenate([w[:, _ORIG["gdn_gate"][0]:_ORIG["gdn_gate"][1]],
                              w[:, _ORIG["ml_gate"][0]:_ORIG["ml_gate"][1]],
                              jnp.zeros((d, LANES - 64), F32)], axis=1).astype(BF16)
    zeros16 = jnp.zeros((16,), F32)
    gp_add = jnp.concatenate([zeros16, gdn_dt_bias[layer].reshape(-1), ml_i_bias[layer].reshape(-1),
                              ml_f_bias[layer].reshape(-1), jnp.zeros((LANES - 64,), F32)])
    gp_mul = jnp.concatenate([zeros16, -jnp.exp(gdn_a_log[layer].astype(F32)).reshape(-1),
                              jnp.zeros((LANES - 32,), F32)])
    gparams = jnp.zeros((8, LANES), F32).at[0].set(gp_add).at[1].set(gp_mul)
    conv_w8 = jnp.zeros((8, gdn_conv_w.shape[2]), F32).at[0:GDN_CONV].set(gdn_conv_w[layer])
    wr = w_router[layer].T
    wr_hi = wr.astype(BF16)
    wr_lo = (wr - wr_hi.astype(F32)).astype(BF16)

    n_mod_rows = -(-(b + 1) // 8) * 8
    cc = jnp.zeros((n_mod_rows, d), F32).at[0:b].set(c).at[b].set(c_ctx)
    mod = _ada_mod(cc, w_ada[layer], b_ada[layer])
    mod3 = mod.reshape(n_mod_rows, 1, 6 * d)

    x2d = x.reshape(t, d)
    tm_l = min(1024, l)
    proj_l, gate_l = _project(x2d, mod3, lambda i: (i * tm_l) // l, norm1_g[layer], w_main, w_gate, tm_l)
    tm_c = min(1024, b * lc)
    proj_c, gate_c = _project(ctx.reshape(b * lc, d), mod3, lambda i: b, norm1_g[layer], w_main, w_gate, tm_c)
    proj_l3 = proj_l.reshape(b, l, N_MAIN)
    proj_c3 = proj_c.reshape(b, lc, N_MAIN)

    gate_l_cm = _col_major(gate_l.reshape(b, l, LANES)).reshape(t, LANES)
    gd_c, ml_c = _gate_prep(gate_c, gparams)
    gd_l, _ = _gate_prep(gate_l, gparams)
    _, ml_l = _gate_prep(gate_l_cm, gparams)

    nb = b // BATCH_PARTS
    tp = nb * l
    n_assign = tp * TOP_K
    n_blocks = (n_assign + N_EXPERTS * (EXPERT_ROWS - 1)) // EXPERT_ROWS + 1
    n_slots = n_blocks * EXPERT_ROWS
    gd_c3, gd_l3 = gd_c.reshape(b, lc, LANES), gd_l.reshape(b, l, LANES)
    ml_c3, ml_l3 = ml_c.reshape(b, lc, LANES), ml_l.reshape(b, l, LANES)
    wbg, wbm, wo = (w_branch_gdn[layer].astype(BF16), w_branch_ml[layer].astype(BF16), w_out[layer].astype(BF16))
    w_sh_gu, w_sh_dn = w_sh_gate_up[layer].astype(BF16), w_sh_down[layer].astype(BF16)
    outs = []
    for part in range(BATCH_PARTS):
        b0, tok0 = part * nb, part * tp
        y_gdn = _gdn(proj_c3, proj_l3, conv_w8, gd_c3, gd_l3, gdn_norm_g[layer], b0, nb)
        q_cm = _col_major(proj_l3[b0:b0 + nb, :, COL_ML_Q:COL_ML_Q + HEADS * ML_DK])
        k_cm = _col_major(proj_l3[b0:b0 + nb, :, COL_ML_K:COL_ML_K + HEADS * ML_DK])
        v_cm = _col_major(proj_l3[b0:b0 + nb, :, COL_ML_V:COL_ML_V + HEADS * HEAD_V])
        h_ml = _row_major(_mlstm(proj_c3, q_cm, k_cm, v_cm, ml_c3, ml_l3, b0))
        x1, h2, scores_t = _merge(y_gdn.reshape(tp, d), h_ml.reshape(tp, d), proj_l, x2d, mod3, l, ml_norm_g[layer],
                                norm2_g[layer], wbg, wbm, wo, wr_hi, wr_lo, tok0, tm=min(512, l))
        idx, wts, rank, cnt = _route(scores_t, router_bias[layer])
        counts = cnt[:, 0].astype(jnp.int32)
        pstart, blk_expert, n_used, next_expert = _block_table(counts, n_blocks)
        pos = _slots(idx, rank, pstart).reshape(n_assign)
        tok_ids = jnp.broadcast_to((jnp.arange(n_assign, dtype=jnp.int32) % tp)[:, None], (n_assign, LANES))
        scattered = _sc_scatter_rows(tok_ids, pos, n_slots)[:, 0]
        in_expert = (jnp.arange(n_slots, dtype=jnp.int32).reshape(n_blocks, EXPERT_ROWS)
                     - pstart[blk_expert][:, None])
        valid = (in_expert < counts[blk_expert][:, None]).reshape(n_slots)
        tok_slot = jnp.where(valid, scattered, jnp.arange(n_slots, dtype=jnp.int32) % tp)
        xb = h2.at[tok_slot + tok0].get(mode="promise_in_bounds")
        yb = _experts(xb, blk_expert, n_used, next_expert, w_exp_gate_up[layer], w_exp_down[layer])
        yg = yb.at[pos].get(mode="promise_in_bounds", unique_indices=True).reshape(TOP_K, tp, d)
        outs.append(_final(x1, h2, yg, wts.T, mod3, l, w_sh_gu, w_sh_dn, final_norm_g, tok0, tm=min(256, l)))
    return jnp.concatenate(outs, axis=0).reshape(b, l, d)
```

```python
import functools

import jax
import jax.numpy as jnp
from jax import lax
from jax.experimental import pallas as pl
from jax.experimental.pallas import tpu as pltpu
from jax.experimental.pallas import tpu_sc as plsc

F32 = jnp.float32
BF16 = jnp.bfloat16
HI = lax.Precision.HIGHEST

EPS = 1e-6
CHUNK = 64
GRID_W = 64
HEADS = 8
HEAD_V = 128
GDN_DK = 128
ML_DK = 64
GDN_CONV = 5
N_EXPERTS = 256
TOP_K = 8
N_GROUPS = 8
TOPK_GROUPS = 4
ROUTED_SCALE = 2.5
EXPERT_ROWS = 512
BATCH_PARTS = 1
GDN_STEPS = 4
GDN_CTX_STEPS = 4
RING = 2 * GDN_STEPS
LANES = 128
VMEM_LIMIT = 56 * 1024 * 1024

COL_GDN_QKV = 0
COL_GDN_Z = 3072
COL_ML_Q = 4096
COL_ML_K = 4608
COL_ML_V = 5120
COL_ML_O = 6144
COL_MG_GDN = 7168
COL_MG_ML = 8192
N_MAIN = 9216
_ORIG = dict(gdn_qkv=(0, 3072), gdn_z=(3072, 4096), gdn_gate=(4096, 4128), ml_q=(4128, 4640),
             ml_k=(4640, 5152), ml_v=(5152, 6176), ml_o=(6176, 7200), ml_gate=(7200, 7232),
             mg_gdn=(7232, 8256), mg_ml=(8256, 9280))


def _params(sem, vmem=VMEM_LIMIT):
    return pltpu.CompilerParams(dimension_semantics=sem, vmem_limit_bytes=vmem)


def _dot(a, b, precision=None):
    return jnp.dot(a, b, preferred_element_type=F32, precision=precision)


def _dot_nt(a, b, precision=None):
    return lax.dot_general(a, b, (((1,), (1,)), ((), ())), preferred_element_type=F32, precision=precision)


def _dot_tn(a, b, precision=None):
    return lax.dot_general(a, b, (((0,), (0,)), ((), ())), preferred_element_type=F32, precision=precision)


def _sigmoid(x):
    return 1.0 / (1.0 + jnp.exp(-x))


def _softplus(x):
    return jnp.maximum(x, 0.0) + jnp.log(1.0 + jnp.exp(-jnp.abs(x)))


def _ada_kernel(c_ref, w_ref, b_ref, o_ref):
    c = c_ref[...]
    sc = c * _sigmoid(c)
    o_ref[...] = _dot(sc, w_ref[...], HI) + b_ref[...]


def _ada_mod(cc, w_ada, b_ada, tn=1536):
    r, d = cc.shape
    n = w_ada.shape[1]
    return pl.pallas_call(
        _ada_kernel,
        grid=(n // tn,),
        in_specs=[pl.BlockSpec((r, d), lambda j: (0, 0)),
                  pl.BlockSpec((d, tn), lambda j: (0, j)),
                  pl.BlockSpec((1, tn), lambda j: (0, j))],
        out_specs=pl.BlockSpec((r, tn), lambda j: (0, j)),
        out_shape=jax.ShapeDtypeStruct((r, n), F32),
        compiler_params=_params(("arbitrary",)),
    )(cc, w_ada, b_ada.reshape(1, n))


def _proj_kernel(x_ref, mod_ref, g_ref, w_ref, wg_ref, o_ref, og_ref, hn_ref):
    d = x_ref.shape[1]

    @pl.when(pl.program_id(1) == 0)
    def _():
        x = x_ref[...]
        y = x * lax.rsqrt(jnp.mean(x * x, axis=-1, keepdims=True) + EPS) * g_ref[...]
        shift = mod_ref[0, :, 0:d]
        scale = mod_ref[0, :, d:2 * d]
        h = (y * (1.0 + scale) + shift).astype(BF16)
        hn_ref[...] = h
        og_ref[...] = _dot(h, wg_ref[...])

    o_ref[...] = _dot(hn_ref[...], w_ref[...]).astype(o_ref.dtype)


def _project(x2d, mod3, mod_row_of_tile, norm_g, w_main, w_gate, tm, tn=1024):
    t, d = x2d.shape
    n = w_main.shape[1]
    return pl.pallas_call(
        _proj_kernel,
        grid=(t // tm, n // tn),
        in_specs=[pl.BlockSpec((tm, d), lambda i, j: (i, 0)),
                  pl.BlockSpec((1, 1, mod3.shape[2]), lambda i, j: (mod_row_of_tile(i), 0, 0)),
                  pl.BlockSpec((1, d), lambda i, j: (0, 0)),
                  pl.BlockSpec((d, tn), lambda i, j: (0, j)),
                  pl.BlockSpec((d, LANES), lambda i, j: (0, 0))],
        out_specs=[pl.BlockSpec((tm, tn), lambda i, j: (i, j)),
                   pl.BlockSpec((tm, LANES), lambda i, j: (i, 0))],
        out_shape=[jax.ShapeDtypeStruct((t, n), BF16), jax.ShapeDtypeStruct((t, LANES), F32)],
        scratch_shapes=[pltpu.VMEM((tm, d), BF16)],
        compiler_params=_params(("arbitrary", "arbitrary")),
    )(x2d, mod3, norm_g.reshape(1, d), w_main, w_gate)


def _gate_kernel(g_ref, p_ref, *out_refs, want_gd, want_ml):
    gd_ref = out_refs[0] if want_gd else None
    ml_ref = out_refs[-1] if want_ml else None
    rows = g_ref.shape[0]
    raw = g_ref[...] + p_ref[0:1, :]
    lane = lax.broadcasted_iota(jnp.int32, raw.shape, 1)
    sp = _softplus(raw)
    vals = jnp.where(lane < 16, _sigmoid(raw),
                     jnp.where(lane < 32, p_ref[1:2, :] * sp,
                               jnp.where(lane < 48, raw,
                                         jnp.where(lane < 64, -_softplus(-raw), 0.0))))
    ri = lax.broadcasted_iota(jnp.int32, (CHUNK, CHUNK), 0)
    ci = lax.broadcasted_iota(jnp.int32, (CHUNK, CHUNK), 1)
    tri_f = (ri >= ci).astype(BF16)
    tri_b = (ri <= ci).astype(BF16)
    lane_c = lax.broadcasted_iota(jnp.int32, (CHUNK, LANES), 1)
    row_c = lax.broadcasted_iota(jnp.int32, (CHUNK, LANES), 0)
    fwd_lane = (lane_c % 16) < 8

    def cumsum(tri, parts):
        h, m, lo = parts
        return _dot(tri, h) + (_dot(tri, m) + _dot(tri, lo))

    for c in range(rows // CHUNK):
        blk = vals[c * CHUNK:(c + 1) * CHUNK, :]
        parts = _split3(blk)
        cum = jnp.where(fwd_lane, cumsum(tri_f, parts), cumsum(tri_b, parts))
        if want_gd:
            gd_ref[c * CHUNK:(c + 1) * CHUNK, :] = jnp.where(lane_c < 16, blk, jnp.where(lane_c < 32, cum, 0.0))
        if want_ml:
            bcum = pltpu.roll(cum, LANES - 16, axis=1)
            gmb = blk - bcum
            cmf, cmb = gmb, gmb
            for s in (1, 2, 4, 8, 16, 32):
                cmf = jnp.maximum(cmf, jnp.where(row_c >= s, pltpu.roll(cmf, s, axis=0), -jnp.inf))
                cmb = jnp.maximum(cmb, jnp.where(row_c < CHUNK - s, pltpu.roll(cmb, CHUNK - s, axis=0), -jnp.inf))
            cm = jnp.where(fwd_lane, cmf, cmb)
            ml = jnp.where(lane_c < 16, pltpu.roll(gmb, LANES - 32, axis=1),
                           jnp.where(lane_c < 32, pltpu.roll(cm, LANES - 16, axis=1),
                                     jnp.where(lane_c < 48, bcum, 0.0)))
            ml_ref[c * CHUNK:(c + 1) * CHUNK, :] = ml


def _gate_prep(graw, gparams, want_gd, want_ml, tm=256):
    t = graw.shape[0]
    spec = pl.BlockSpec((tm, LANES), lambda i: (i, 0))
    n_out = int(want_gd) + int(want_ml)
    return pl.pallas_call(
        functools.partial(_gate_kernel, want_gd=want_gd, want_ml=want_ml),
        grid=(t // tm,),
        in_specs=[spec, pl.BlockSpec((8, LANES), lambda i: (0, 0))],
        out_specs=[spec] * n_out,
        out_shape=[jax.ShapeDtypeStruct((t, LANES), F32)] * n_out,
        compiler_params=_params(("arbitrary",)),
    )(graw, gparams)


def _split3(a):
    h = a.astype(BF16)
    r = a - h.astype(F32)
    m = r.astype(BF16)
    return h, m, (r - m.astype(F32)).astype(BF16)


def _lane_picks(x, lanes):
    li = lax.broadcasted_iota(jnp.int32, (LANES, LANES), 0)
    ci = lax.broadcasted_iota(jnp.int32, (LANES, LANES), 1)
    want = jnp.full((LANES, LANES), -1, jnp.int32)
    for j, lane in enumerate(lanes):
        want = jnp.where(ci == j, lane, want)
    sel = (li == want).astype(BF16)
    h, m, lo = _split3(x)
    cols = _dot(h, sel) + (_dot(m, sel) + _dot(lo, sel))
    return [jnp.broadcast_to(cols[:, j:j + 1], x.shape) for j in range(len(lanes))]


def _dir_masks(direction):
    ri = lax.broadcasted_iota(jnp.int32, (CHUNK, CHUNK), 0)
    ci = lax.broadcasted_iota(jnp.int32, (CHUNK, CHUNK), 1)
    if direction == 0:
        return ri >= ci, ri > ci
    return ri <= ci, ri < ci


def _gdn_kernel(qc_ref, kc_ref, vc_ref, ql_ref, kl_ref, vl_ref, z_ref, cwq_ref, cwk_ref, cwv_ref,
                gdc_ref, gdl_ref, ng_ref, y_ref,
                xpad, qs, ks, vs, beta_t, cg_t, wq_r, u_r, kd_r, qk_r, dc_r, out_s):
    lc = qc_ref.shape[1]
    ll = ql_ref.shape[1]
    lt = lc + ll
    n_c, n_l = lc // CHUNK, ll // CHUNK
    n_t = n_c + n_l
    rb = 256

    def l2n(x):
        return x * lax.rsqrt(jnp.sum(x * x, axis=-1, keepdims=True) + EPS)

    def prep(src_ref, cw_ref, dst, off, ls, kind):
        xpad[0:8, :] = jnp.zeros((8, LANES), F32)
        xpad[8:8 + ls, :] = src_ref[0].astype(F32)
        xpad[8 + ls:16 + ls, :] = jnp.zeros((8, LANES), F32)
        step = min(rb, ls)
        for r0 in range(0, ls, step):
            acc = jnp.zeros((step, LANES), F32)
            for t in range(GDN_CONV):
                s0 = r0 + 8 - GDN_CONV // 2 + t
                acc = acc + cw_ref[t:t + 1, :] * xpad[s0:s0 + step, :]
            y = acc * _sigmoid(acc)
            if kind == "q":
                y = l2n(y) * (GDN_DK ** -0.5)
            elif kind == "k":
                y = l2n(y)
            dst[off + r0:off + r0 + step, :] = y

    prep(qc_ref, cwq_ref, qs, 0, lc, "q")
    prep(kc_ref, cwk_ref, ks, 0, lc, "k")
    prep(vc_ref, cwv_ref, vs, 0, lc, "v")
    prep(ql_ref, cwq_ref, qs, lc, ll, "q")
    prep(kl_ref, cwk_ref, ks, lc, ll, "k")
    prep(vl_ref, cwv_ref, vs, lc, ll, "v")

    head = pl.program_id(1)

    def build_tables(src_ref, off, ls):
        step = min(rb, ls)
        for r0 in range(0, ls, step):
            picked = _lane_picks(src_ref[0, r0:r0 + step, :], [8 * d + head for d in range(2)]
                                 + [16 + 8 * d + head for d in range(2)])
            for d in range(2):
                beta_t[d, off + r0:off + r0 + step, :] = picked[d]
                cg_t[d, off + r0:off + r0 + step, :] = picked[2 + d]

    build_tables(gdc_ref, 0, lc)
    build_tables(gdl_ref, lc, ll)

    def bwd_chunk(t):
        return jnp.where(t < n_c, n_c - 1 - t, n_t + n_c - 1 - t)

    row_p = lax.broadcasted_iota(jnp.int32, (CHUNK, LANES), 0)
    lane_p = lax.broadcasted_iota(jnp.int32, (CHUNK, LANES), 1)
    fwd_p = lane_p < CHUNK
    col_p = jnp.where(fwd_p, lane_p, lane_p - CHUNK)
    signed = jnp.where(fwd_p, row_p - col_p, col_p - row_p)
    incl_p = signed >= 0
    strict_p = signed > 0
    eye_p = (row_p == col_p).astype(F32)
    keep_f = fwd_p.astype(BF16)
    keep_b = (1.0 - fwd_p.astype(F32)).astype(BF16)

    def block_diag(top, bottom):
        zero = jnp.zeros(top.shape, top.dtype)
        return jnp.concatenate([jnp.concatenate([top, zero], axis=1), jnp.concatenate([zero, bottom], axis=1)], axis=0)

    def pair_diag(xp):
        return jnp.concatenate([xp * keep_f, xp * keep_b], axis=0)

    def pair_times3(a, xp):
        ah = a.astype(BF16)
        al = (a - ah.astype(F32)).astype(BF16)
        xh = xp.astype(BF16)
        xl = (xp - xh.astype(F32)).astype(BF16)
        dh = pair_diag(xh)
        return _dot(ah, dh) + (_dot(ah, pair_diag(xl)) + _dot(al, dh))

    def prep_chain(t):
        tc = jnp.minimum(t, n_t - 1)
        q, k, v, beta, cgc, kb, ecg = [], [], [], [], [], [], []
        for d in range(2):
            c = tc if d == 0 else bwd_chunk(tc)
            rows = pl.ds(pl.multiple_of(c * CHUNK, CHUNK), CHUNK)
            q.append(qs[rows, :])
            k.append(ks[rows, :])
            v.append(vs[rows, :])
            beta.append(beta_t[d, rows, :])
            cgc.append(cg_t[d, rows, :])
        kdiag = block_diag(k[0].astype(BF16), k[1].astype(BF16))
        kk = _dot_nt(jnp.concatenate([k[0], k[1]], axis=1).astype(BF16), kdiag)
        qk = _dot_nt(jnp.concatenate([q[0], q[1]], axis=1).astype(BF16), kdiag)
        yield
        beta_p = jnp.where(fwd_p, beta[0], beta[1])
        cg_p = jnp.where(fwd_p, cgc[0], cgc[1])
        cgr_p = jnp.transpose(jnp.concatenate([cgc[0], cgc[1]], axis=0))[0:CHUNK, :]
        decay = jnp.exp(jnp.where(incl_p, cg_p - cgr_p, -jnp.inf))
        qkd = qk * decay
        for d in range(2):
            slot = (t % RING) * 2 + d
            last = CHUNK - 1 if d == 0 else 0
            cg_last = cgc[d][last:last + 1, :]
            ecg.append(jnp.exp(cgc[d]))
            kb.append(k[d] * beta[d])
            qk_r[slot] = qkd[:, d * CHUNK:(d + 1) * CHUNK]
            wq_r[slot, CHUNK:2 * CHUNK, :] = q[d] * ecg[d]
            kd_r[slot] = k[d] * jnp.exp(cg_last - cgc[d])
            dc_r[slot] = jnp.broadcast_to(jnp.exp(cg_last), (8, LANES))
        x = jnp.where(strict_p, -(beta_p * kk) * decay, 0.0)
        tinv = eye_p + x
        x = pair_times3(x, x)
        yield
        for _ in range(4):
            both = pair_times3(jnp.concatenate([tinv, x], axis=0), x)
            tinv, x = tinv + both[0:CHUNK, :], both[CHUNK:2 * CHUNK, :]
            yield
        tinv = (tinv + pair_times3(tinv, x)).astype(BF16)
        yield
        w = _dot(tinv, block_diag((kb[0] * ecg[0]).astype(BF16), (kb[1] * ecg[1]).astype(BF16)))
        u = _dot(tinv, block_diag((v[0] * beta[0]).astype(BF16), (v[1] * beta[1]).astype(BF16)))
        for d in range(2):
            slot = (t % RING) * 2 + d
            wq_r[slot, 0:CHUNK, :] = w[:, d * LANES:(d + 1) * LANES]
            u_r[slot] = u[:, d * LANES:(d + 1) * LANES]

    out_s[...] = jnp.zeros(out_s.shape, F32)

    def scan_chain(d, t0, steps, s, with_out, result):
        for j in range(steps):
            t = t0 + j
            slot = (t % RING) * 2 + d
            ws = _dot(wq_r[slot], s)
            yield
            v_new = u_r[slot] - ws[0:CHUNK, :]
            if with_out:
                c = t if d == 0 else bwd_chunk(t)
                o = ws[CHUNK:2 * CHUNK, :] + _dot(qk_r[slot], v_new)
                l0 = pl.multiple_of((c - n_c) * CHUNK, CHUNK)
                out_s[pl.ds(l0, CHUNK), :] += o
            s = s * dc_r[slot][0:1, :] + _dot_tn(kd_r[slot], v_new)
            yield
        result[d] = s

    def lockstep(chains):
        chains = list(chains)
        while chains:
            alive = []
            for ch in chains:
                try:
                    next(ch)
                    alive.append(ch)
                except StopIteration:
                    pass
            chains = alive

    def group_body(i, carry, t_base, steps, ahead, with_out):
        t0 = t_base + steps * i
        result = [None, None]
        lockstep([scan_chain(d, t0, steps, carry[d], with_out, result) for d in range(2)]
                 + [prep_chain(t0 + steps + j) for j in range(ahead)])
        return result[0], result[1]

    def run_groups(carry, t_base, n_steps, steps, ahead_last, with_out):
        n_groups = n_steps // steps
        carry = lax.fori_loop(0, n_groups - 1, functools.partial(
            group_body, t_base=t_base, steps=steps, ahead=steps, with_out=with_out), carry)
        return group_body(n_groups - 1, carry, t_base, steps, ahead_last, with_out)

    lockstep([prep_chain(j) for j in range(GDN_CTX_STEPS)])
    s0 = jnp.zeros((GDN_DK, HEAD_V), F32)
    carry = run_groups((s0, s0), 0, n_c, GDN_CTX_STEPS, GDN_STEPS, False)
    run_groups(carry, n_c, n_l, GDN_STEPS, 0, True)


    def out_body(i, carry):
        r0 = pl.multiple_of(i * rb, rb)
        o = out_s[pl.ds(r0, rb), :]
        z = z_ref[0, pl.ds(r0, rb), :].astype(F32)
        y = o * lax.rsqrt(jnp.mean(o * o, axis=-1, keepdims=True) + EPS) * ng_ref[...]
        y_ref[0, pl.ds(r0, rb), :] = (y * (z * _sigmoid(z))).astype(y_ref.dtype)
        return carry

    lax.fori_loop(0, ll // rb, out_body, 0)


def _gdn(proj_c, proj_l, conv_w8, gd_c, gd_l, norm_g, b0, nb):
    lc = proj_c.shape[1]
    ll = proj_l.shape[1]
    lt = lc + ll
    qb, kb_, vb, zb = (COL_GDN_QKV // LANES, COL_GDN_QKV // LANES + HEADS, COL_GDN_QKV // LANES + 2 * HEADS,
                       COL_GDN_Z // LANES)

    def seq_spec(l, col0):
        return pl.BlockSpec((1, l, LANES), lambda i, h: (i + b0, 0, col0 + h))

    def cw_spec(col0):
        return pl.BlockSpec((8, LANES), lambda i, h: (0, col0 + h))

    return pl.pallas_call(
        _gdn_kernel,
        grid=(nb, HEADS),
        in_specs=[seq_spec(lc, qb), seq_spec(lc, kb_), seq_spec(lc, vb),
                  seq_spec(ll, qb), seq_spec(ll, kb_), seq_spec(ll, vb), seq_spec(ll, zb),
                  cw_spec(0), cw_spec(HEADS), cw_spec(2 * HEADS),
                  pl.BlockSpec((1, lc, LANES), lambda i, h: (i + b0, 0, 0)),
                  pl.BlockSpec((1, ll, LANES), lambda i, h: (i + b0, 0, 0)),
                  pl.BlockSpec((1, LANES), lambda i, h: (0, 0))],
        out_specs=pl.BlockSpec((1, ll, LANES), lambda i, h: (i, 0, h)),
        out_shape=jax.ShapeDtypeStruct((nb, ll, HEADS * HEAD_V), BF16),
        scratch_shapes=[pltpu.VMEM((max(lc, ll) + 16, LANES), F32),
                        pltpu.VMEM((lt, LANES), F32), pltpu.VMEM((lt, LANES), F32), pltpu.VMEM((lt, LANES), F32),
                        pltpu.VMEM((2, lt, LANES), F32), pltpu.VMEM((2, lt, LANES), F32),
                        pltpu.VMEM((2 * RING, 2 * CHUNK, LANES), F32),
                        pltpu.VMEM((2 * RING, CHUNK, LANES), F32), pltpu.VMEM((2 * RING, CHUNK, LANES), F32),
                        pltpu.VMEM((2 * RING, CHUNK, CHUNK), F32),
                        pltpu.VMEM((2 * RING, 8, LANES), F32),
                        pltpu.VMEM((ll, LANES), F32)],
        compiler_params=_params(("arbitrary", "arbitrary")),
    )(proj_c, proj_c, proj_c, proj_l, proj_l, proj_l, proj_l, conv_w8, conv_w8, conv_w8,
      gd_c, gd_l, norm_g.reshape(1, LANES))


def _mlstm_kernel(qc_ref, kc_ref, vc_ref, ql_ref, kl_ref, vl_ref, mlc_ref, mll_ref, h_ref, out_s, tabs):
    lc = qc_ref.shape[1]
    ll = ql_ref.shape[1]
    n_c, n_l = lc // CHUNK, ll // CHUNK
    n_t = n_c + n_l
    lt = lc + ll
    pair = pl.program_id(1)
    lane = lax.broadcasted_iota(jnp.int32, (CHUNK, LANES), 1)
    ones_v = jnp.ones((CHUNK, HEAD_V), BF16)
    chains = [(hh, d) for hh in range(2) for d in range(2)]
    hmask = [((lane // ML_DK) == hh).astype(F32) for hh in range(2)]
    incl = [_dir_masks(d)[0] for d in range(2)]

    def build_tables(src_ref, off, ls):
        step = min(256, ls)
        for r0 in range(0, ls, step):
            lanes = [16 * j + 8 * d + 2 * pair + hh for hh, d in chains for j in range(3)]
            for g, tab in enumerate(_lane_picks(src_ref[0, r0:r0 + step, :], lanes)):
                tabs[g, off + r0:off + r0 + step, :] = tab

    build_tables(mlc_ref, 0, lc)
    build_tables(mll_ref, lc, ll)

    def wide(a):
        return jnp.concatenate([a, a], axis=1)

    def chain(hh, d, c, state, is_ctx, result):
        cs, ms = state
        last = CHUNK - 1 if d == 0 else 0
        if is_ctx:
            rows = pl.ds(pl.multiple_of(c * CHUNK, CHUNK), CHUNK)
            q_ref, k_ref, v_ref = qc_ref, kc_ref, vc_ref
        else:
            rows = pl.ds(pl.multiple_of((c - n_c) * CHUNK, CHUNK), CHUNK)
            q_ref, k_ref, v_ref = ql_ref, kl_ref, vl_ref
        q = (q_ref[0, rows, :].astype(F32) * hmask[hh]).astype(BF16)
        k = k_ref[0, rows, :].astype(F32) * (hmask[hh] * (ML_DK ** -0.5))
        v = jnp.concatenate([v_ref[0, rows, hh * HEAD_V:(hh + 1) * HEAD_V], ones_v], axis=1)
        n = chains.index((hh, d))
        trows = pl.ds(pl.multiple_of(c * CHUNK, CHUNK), CHUNK)
        gmb = tabs[3 * n, trows, :]
        gmb_t = jnp.transpose(gmb)[0:CHUNK, :]
        cm = tabs[3 * n + 1, trows, :]
        bc = tabs[3 * n + 2, trows, :]
        qk = _dot_nt(q, k.astype(BF16))
        yield
        cm_last = cm[last:last + 1, :]
        b_last = bc[last:last + 1, :]
        mm = jnp.maximum(ms, cm)
        p = jnp.where(incl[d], jnp.exp(gmb_t - mm[:, 0:CHUNK]), 0.0) * qk
        wk = (k * jnp.exp(gmb - cm_last)).astype(BF16)
        inter = _dot(q, cs.astype(BF16))
        intra = _dot(p.astype(BF16), v)
        c_loc = _dot_tn(wk, v)
        yield
        if not is_ctx:
            nd = wide(jnp.exp(ms - mm)) * inter + intra
            hv = nd[:, 0:HEAD_V] / jnp.maximum(jnp.abs(nd[:, HEAD_V:2 * HEAD_V]), jnp.exp(-(bc + mm)))
            l0 = pl.multiple_of((c - n_c) * CHUNK, CHUNK)
            out_s[pl.ds(l0, CHUNK), hh * HEAD_V:(hh + 1) * HEAD_V] += hv
        mx = jnp.maximum(ms, cm_last)
        result[hh, d] = (wide(jnp.exp(ms - mx)) * cs + wide(jnp.exp(cm_last - mx)) * c_loc, b_last + mx)

    out_s[...] = jnp.zeros(out_s.shape, F32)

    def run(chains_iter):
        live = list(chains_iter)
        while live:
            alive = []
            for ch in live:
                try:
                    next(ch)
                    alive.append(ch)
                except StopIteration:
                    pass
            live = alive

    def body(i, carry, is_ctx):
        result = {}
        gens = []
        for n, (hh, d) in enumerate(chains):
            if is_ctx:
                c = i if d == 0 else n_c - 1 - i
            else:
                c = n_c + i if d == 0 else n_t - 1 - i
            gens.append(chain(hh, d, c, carry[n], is_ctx, result))
        run(gens)
        return tuple(result[hd] for hd in chains)

    st0 = (jnp.zeros((LANES, 2 * HEAD_V), F32), jnp.zeros((1, LANES), F32))
    carry = lax.fori_loop(0, n_c, functools.partial(body, is_ctx=True), (st0,) * 4)
    lax.fori_loop(0, n_l, functools.partial(body, is_ctx=False), carry)
    h_ref[0] = out_s[...].astype(h_ref.dtype)


def _mlstm(proj_c, q_l, k_l, v_l, ml_c, ml_l, b0):
    lc = proj_c.shape[1]
    nb, ll, _ = q_l.shape
    lt = lc + ll
    qb, kb_, vb = COL_ML_Q // LANES, COL_ML_K // LANES, COL_ML_V // (2 * HEAD_V)
    return pl.pallas_call(
        _mlstm_kernel,
        grid=(nb, HEADS // 2),
        in_specs=[pl.BlockSpec((1, lc, LANES), lambda i, p: (i + b0, 0, qb + p)),
                  pl.BlockSpec((1, lc, LANES), lambda i, p: (i + b0, 0, kb_ + p)),
                  pl.BlockSpec((1, lc, 2 * HEAD_V), lambda i, p: (i + b0, 0, vb + p)),
                  pl.BlockSpec((1, ll, LANES), lambda i, p: (i, 0, p)),
                  pl.BlockSpec((1, ll, LANES), lambda i, p: (i, 0, p)),
                  pl.BlockSpec((1, ll, 2 * HEAD_V), lambda i, p: (i, 0, p)),
                  pl.BlockSpec((1, lc, LANES), lambda i, p: (i + b0, 0, 0)),
                  pl.BlockSpec((1, ll, LANES), lambda i, p: (i + b0, 0, 0))],
        out_specs=pl.BlockSpec((1, ll, 2 * HEAD_V), lambda i, p: (i, 0, p)),
        out_shape=jax.ShapeDtypeStruct((nb, ll, HEADS * HEAD_V), BF16),
        scratch_shapes=[pltpu.VMEM((ll, 2 * HEAD_V), F32), pltpu.VMEM((12, lt, LANES), F32)],
        compiler_params=_params(("arbitrary", "arbitrary")),
    )(proj_c, proj_c, proj_c, q_l, k_l, v_l, ml_c, ml_l)


def _merge_kernel(yg_ref, hm_ref, o_ref, gg_ref, gm_ref, x_ref, mod_ref, mlg_ref, n2_ref,
                  wbg_ref, wbm_ref, wo_ref, wrh_ref, wrl_ref, x1_ref, h2_ref, sc_ref):
    d = x_ref.shape[1]
    o = o_ref[...].astype(F32)
    ym = _sigmoid(o) * hm_ref[...].astype(F32)
    segs = []
    for h in range(HEADS):
        seg = ym[:, h * HEAD_V:(h + 1) * HEAD_V]
        segs.append(seg * lax.rsqrt(jnp.mean(seg * seg, axis=-1, keepdims=True) + EPS))
    ymn = jnp.concatenate(segs, axis=1) * mlg_ref[...]
    y_gdn = _dot(yg_ref[...], wbg_ref[...])
    y_ml = _dot(ymn.astype(BF16), wbm_ref[...])
    mixed = _sigmoid(gg_ref[...].astype(F32)) * y_gdn + _sigmoid(gm_ref[...].astype(F32)) * y_ml
    y = _dot(mixed.astype(BF16), wo_ref[...])
    x1 = x_ref[...] + mod_ref[0, :, 2 * d:3 * d] * y
    x1_ref[...] = x1
    hn = x1 * lax.rsqrt(jnp.mean(x1 * x1, axis=-1, keepdims=True) + EPS) * n2_ref[...]
    h2 = hn * (1.0 + mod_ref[0, :, 4 * d:5 * d]) + mod_ref[0, :, 3 * d:4 * d]
    h2_hi = h2.astype(BF16)
    h2_ref[...] = h2_hi
    h2_lo = (h2 - h2_hi.astype(F32)).astype(BF16)
    logits = _dot_nt(wrh_ref[...], h2_hi) + (_dot_nt(wrl_ref[...], h2_hi) + _dot_nt(wrh_ref[...], h2_lo))
    sc_ref[...] = _sigmoid(logits)


def _merge(y_gdn, h_ml, proj_l2d, x2d, mod3, rows_per_mod, ml_norm_g, norm2_g, wbg, wbm, wo, wr_hi, wr_lo, tok0,
           tm=512):
    t, d = y_gdn.shape
    e = wr_hi.shape[0]
    off = tok0 // tm
    row = lambda i: (i, 0)
    const = lambda i: (0, 0)
    return pl.pallas_call(
        _merge_kernel,
        grid=(t // tm,),
        in_specs=[pl.BlockSpec((tm, d), row), pl.BlockSpec((tm, d), row),
                  pl.BlockSpec((tm, d), lambda i: (i + off, COL_ML_O // d)),
                  pl.BlockSpec((tm, d), lambda i: (i + off, COL_MG_GDN // d)),
                  pl.BlockSpec((tm, d), lambda i: (i + off, COL_MG_ML // d)),
                  pl.BlockSpec((tm, d), lambda i: (i + off, 0)),
                  pl.BlockSpec((1, 1, mod3.shape[2]), lambda i: (((i + off) * tm) // rows_per_mod, 0, 0)),
                  pl.BlockSpec((1, d), const), pl.BlockSpec((1, d), const),
                  pl.BlockSpec((d, d), const), pl.BlockSpec((d, d), const), pl.BlockSpec((d, d), const),
                  pl.BlockSpec((e, d), const), pl.BlockSpec((e, d), const)],
        out_specs=[pl.BlockSpec((tm, d), row), pl.BlockSpec((tm, d), lambda i: (i + off, 0)),
                   pl.BlockSpec((e, tm), lambda i: (0, i))],
        out_shape=[jax.ShapeDtypeStruct((t, d), F32), jax.ShapeDtypeStruct((x2d.shape[0], d), BF16),
                   jax.ShapeDtypeStruct((e, t), F32)],
        compiler_params=_params(("arbitrary",)),
    )(y_gdn, h_ml, proj_l2d, proj_l2d, proj_l2d, x2d, mod3, ml_norm_g.reshape(1, d), norm2_g.reshape(1, d),
      wbg, wbm, wo, wr_hi, wr_lo)


def _expert_kernel(be_ref, nu_ref, nx_ref, x_ref, wgu_hbm, wd_hbm, y_ref, land_gu, land_d, wgu_s, wd_s, sems):
    i = pl.program_id(0)
    de = wd_hbm.shape[1]
    used = i < nu_ref[0]
    first = jnp.logical_or(i == 0, be_ref[i] != be_ref[jnp.maximum(i - 1, 0)])

    def weight_copies(ex):
        return (pltpu.make_async_copy(wgu_hbm.at[ex], land_gu, sems.at[0]),
                pltpu.make_async_copy(wd_hbm.at[ex], land_d, sems.at[1]))

    @pl.when(i == 0)
    def _():
        for cp in weight_copies(be_ref[0]):
            cp.start()

    @pl.when(jnp.logical_and(first, used))
    def _():
        for cp in weight_copies(be_ref[i]):
            cp.wait()
        wgu_s[...] = land_gu[...].astype(BF16)
        wd_s[...] = land_d[...].astype(BF16)

        @pl.when(nx_ref[i] >= 0)
        def _():
            for cp in weight_copies(nx_ref[i]):
                cp.start()

    @pl.when(used)
    def _():
        gu = _dot(x_ref[...], wgu_s[...])
        g = gu[:, 0:de]
        act = (g * _sigmoid(g)) * gu[:, de:2 * de]
        y_ref[...] = _dot(act.astype(BF16), wd_s[...]).astype(y_ref.dtype)

    @pl.when(jnp.logical_not(used))
    def _():
        y_ref[...] = jnp.zeros(y_ref.shape, y_ref.dtype)


def _experts(xb, blk_expert, n_used, next_expert, w_gu, w_down):
    n_slots, d = xb.shape
    n_blocks = n_slots // EXPERT_ROWS
    e, _, de2 = w_gu.shape
    de = de2 // 2
    return pl.pallas_call(
        _expert_kernel,
        grid_spec=pltpu.PrefetchScalarGridSpec(
            num_scalar_prefetch=3,
            grid=(n_blocks,),
            in_specs=[pl.BlockSpec((EXPERT_ROWS, d), lambda i, be, nu, nx: (i, 0)),
                      pl.BlockSpec(memory_space=pl.ANY), pl.BlockSpec(memory_space=pl.ANY)],
            out_specs=pl.BlockSpec((EXPERT_ROWS, d), lambda i, be, nu, nx: (i, 0)),
            scratch_shapes=[pltpu.VMEM((d, de2), w_gu.dtype), pltpu.VMEM((de, d), w_down.dtype),
                            pltpu.VMEM((d, de2), BF16), pltpu.VMEM((de, d), BF16),
                            pltpu.SemaphoreType.DMA((2,))]),
        out_shape=jax.ShapeDtypeStruct((n_slots, d), BF16),
        compiler_params=_params(("arbitrary",)),
    )(blk_expert, n_used, next_expert, xb, w_gu, w_down)


def _final_kernel(x1_ref, h2_ref, yg_ref, wt_ref, mod_ref, wsg_ref, wsd_ref, fg_ref, o_ref):
    d = x1_ref.shape[1]
    ds_ = wsd_ref.shape[0]
    wt = wt_ref[...]
    routed = jnp.zeros(x1_ref.shape, F32)
    for k in range(TOP_K):
        routed = routed + wt[:, k:k + 1] * yg_ref[k].astype(F32)
    gu = _dot(h2_ref[...], wsg_ref[...])
    g = gu[:, 0:ds_]
    sh = _dot(((g * _sigmoid(g)) * gu[:, ds_:2 * ds_]).astype(BF16), wsd_ref[...])
    x2 = x1_ref[...] + mod_ref[0, :, 5 * d:6 * d] * (routed + sh)
    o_ref[...] = x2 * lax.rsqrt(jnp.mean(x2 * x2, axis=-1, keepdims=True) + EPS) * fg_ref[...]


def _final(x1, h2, yg, wts, mod3, rows_per_mod, w_sh_gu, w_sh_down, final_g, tok0, tm=256):
    tp, d = x1.shape
    off = tok0 // tm
    row = lambda i: (i, 0)
    const = lambda i: (0, 0)
    return pl.pallas_call(
        _final_kernel,
        grid=(tp // tm,),
        in_specs=[pl.BlockSpec((tm, d), row), pl.BlockSpec((tm, d), lambda i: (i + off, 0)),
                  pl.BlockSpec((TOP_K, tm, d), lambda i: (0, i, 0)), pl.BlockSpec((tm, TOP_K), row),
                  pl.BlockSpec((1, 1, mod3.shape[2]), lambda i: (((i + off) * tm) // rows_per_mod, 0, 0)),
                  pl.BlockSpec(w_sh_gu.shape, const), pl.BlockSpec(w_sh_down.shape, const),
                  pl.BlockSpec((1, d), const)],
        out_specs=pl.BlockSpec((tm, d), row),
        out_shape=jax.ShapeDtypeStruct((tp, d), F32),
        compiler_params=_params(("arbitrary",)),
    )(x1, h2, yg, wts, mod3, w_sh_gu, w_sh_down, final_g.reshape(1, d))


def _route_kernel(sc_ref, bias_ref, tri_ref, idx_ref, wt_ref, rk_ref, cnt_ref, base_s):
    @pl.when(pl.program_id(0) == 0)
    def _():
        base_s[...] = jnp.zeros(base_s.shape, F32)

    scores = sc_ref[...]
    e, tn = scores.shape
    gsz = e // N_GROUPS
    sel3 = (scores + bias_ref[...]).reshape(N_GROUPS, gsz, tn)
    m1 = jnp.max(sel3, axis=1)
    is_max = sel3 == m1[:, None, :]
    n_max = jnp.sum(is_max.astype(F32), axis=1)
    m2 = jnp.max(jnp.where(is_max, -jnp.inf, sel3), axis=1)
    grp = m1 + jnp.where(n_max >= 2.0, m1, m2)
    gi = lax.broadcasted_iota(jnp.int32, (N_GROUPS, tn), 0)
    ahead = jnp.zeros((N_GROUPS, tn), F32)
    for g in range(N_GROUPS):
        row = grp[g:g + 1, :]
        ahead = ahead + jnp.logical_or(row > grp, jnp.logical_and(row == grp, g < gi)).astype(F32)
    ahead3 = jnp.broadcast_to(ahead[:, None, :], (N_GROUPS, gsz, tn))
    selm = jnp.where(ahead3 < float(TOPK_GROUPS), sel3, -jnp.inf).reshape(e, tn)
    ri = lax.broadcasted_iota(jnp.int32, (e, tn), 0).astype(F32)
    member = jnp.zeros((e, tn), F32)
    idxs, ws = [], []
    for _ in range(TOP_K):
        m = jnp.max(selm, axis=0, keepdims=True)
        idx = jnp.min(jnp.where(selm == m, ri, float(e)), axis=0, keepdims=True)
        hit = ri == idx
        ws.append(jnp.sum(jnp.where(hit, scores, 0.0), axis=0, keepdims=True))
        idxs.append(idx)
        selm = jnp.where(hit, -jnp.inf, selm)
        member = jnp.where(hit, 1.0, member)
    w = jnp.concatenate(ws, axis=0)
    wt_ref[...] = w / jnp.sum(w, axis=0, keepdims=True) * ROUTED_SCALE
    idx_ref[...] = jnp.concatenate(idxs, axis=0).astype(jnp.int32)
    cum = _dot(member.astype(BF16), tri_ref[...]) + base_s[...]
    rk_ref[...] = jnp.concatenate(
        [jnp.sum(jnp.where(ri == idx, cum, 0.0), axis=0, keepdims=True) for idx in idxs], axis=0).astype(jnp.int32)
    total = base_s[...] + jnp.sum(member, axis=1, keepdims=True)
    base_s[...] = total
    cnt_ref[...] = total


def _route(scores_t, router_bias):
    e, t = scores_t.shape
    tn = LANES
    bias = jnp.broadcast_to(router_bias.astype(F32)[:, None], (e, tn))
    tri = (jnp.arange(tn)[:, None] < jnp.arange(tn)[None, :]).astype(BF16)
    tok = pl.BlockSpec((TOP_K, tn), lambda i: (0, i))
    const = lambda i: (0, 0)
    return pl.pallas_call(
        _route_kernel,
        grid=(t // tn,),
        in_specs=[pl.BlockSpec((e, tn), lambda i: (0, i)), pl.BlockSpec((e, tn), const),
                  pl.BlockSpec((tn, tn), const)],
        out_specs=[tok, tok, tok, pl.BlockSpec((e, tn), const)],
        out_shape=[jax.ShapeDtypeStruct((TOP_K, t), jnp.int32), jax.ShapeDtypeStruct((TOP_K, t), F32),
                   jax.ShapeDtypeStruct((TOP_K, t), jnp.int32), jax.ShapeDtypeStruct((e, tn), F32)],
        scratch_shapes=[pltpu.VMEM((e, tn), F32)],
        compiler_params=_params(("arbitrary",)),
    )(scores_t, bias, tri)


def _slot_kernel(idx_ref, rk_ref, ps_ref, pos_ref):
    e, tn = ps_ref.shape
    ri = lax.broadcasted_iota(jnp.int32, (e, tn), 0)
    ps = ps_ref[...]
    rows = [jnp.sum(jnp.where(ri == idx_ref[k:k + 1, :], ps, 0.0), axis=0, keepdims=True) for k in range(TOP_K)]
    pos_ref[...] = rk_ref[...] + jnp.concatenate(rows, axis=0).astype(jnp.int32)


def _slots(idx, rank, pstart):
    k, t = idx.shape
    e = pstart.shape[0]
    tn = LANES
    tok = pl.BlockSpec((k, tn), lambda i: (0, i))
    return pl.pallas_call(
        _slot_kernel,
        grid=(t // tn,),
        in_specs=[tok, tok, pl.BlockSpec((e, tn), lambda i: (0, 0))],
        out_specs=tok,
        out_shape=jax.ShapeDtypeStruct((k, t), jnp.int32),
        compiler_params=_params(("arbitrary",)),
    )(idx, rank, jnp.broadcast_to(pstart.astype(F32)[:, None], (e, tn)))


def _sc_scatter_rows(vals, idx, n_rows, window=LANES):
    n, width = vals.shape
    mesh = plsc.VectorSubcoreMesh(core_axis_name="core", subcore_axis_name="subcore")

    @pl.kernel(out_type=jax.ShapeDtypeStruct((n_rows, width), vals.dtype), mesh=mesh, scratch_types=[])
    def scatter(v_hbm, i_hbm, o_hbm):
        def body(v_vmem, i_vmem):
            pltpu.sync_copy(v_vmem, o_hbm.at[i_vmem.at[0]])

        pltpu.emit_pipeline(
            body,
            grid=(n // window,),
            in_specs=[pl.BlockSpec((window, width), lambda i: (i, 0)),
                      pl.BlockSpec((1, window), lambda i: (0, i))],
            out_specs=[],
            core_axis_name=("core", "subcore"),
            dimension_semantics=(pltpu.PARALLEL,),
        )(v_hbm, i_hbm)

    return scatter(vals, idx.reshape(1, n))


def _block_table(counts, n_blocks):
    padded = (counts + EXPERT_ROWS - 1) // EXPERT_ROWS * EXPERT_ROWS
    pend = jnp.cumsum(padded)
    first_slot = jnp.arange(n_blocks, dtype=jnp.int32) * EXPERT_ROWS
    blk_expert = jnp.minimum(jnp.sum((pend[None, :] <= first_slot[:, None]).astype(jnp.int32), axis=1),
                             N_EXPERTS - 1)
    e_ids = jnp.arange(N_EXPERTS, dtype=jnp.int32)
    later = lax.cummin(jnp.where(counts > 0, e_ids, N_EXPERTS)[::-1])[::-1]
    next_tab = jnp.concatenate([later[1:], jnp.full((1,), N_EXPERTS, jnp.int32)])
    next_tab = jnp.where(next_tab < N_EXPERTS, next_tab, -1)
    return (pend - padded, blk_expert, (pend[-1] // EXPERT_ROWS).astype(jnp.int32).reshape(1),
            next_tab[blk_expert].astype(jnp.int32))


def _col_major(t):
    b, l, f = t.shape
    rows = l // GRID_W
    return t.reshape(b, rows, GRID_W, f).transpose(0, 2, 1, 3).reshape(b, l, f)


def _row_major(t):
    b, l, f = t.shape
    rows = l // GRID_W
    return t.reshape(b, GRID_W, rows, f).transpose(0, 2, 1, 3).reshape(b, l, f)


def kernel(x, c, ctx, c_ctx, w_ada, b_ada, norm1_g, norm2_g, w_in, gdn_conv_w, gdn_a_log, gdn_dt_bias, gdn_norm_g,
           ml_i_bias, ml_f_bias, ml_norm_g, w_branch_gdn, w_branch_ml, w_out, w_router, router_bias, w_exp_gate_up,
           w_exp_down, w_sh_gate_up, w_sh_down, final_norm_g):
    b, l, d = x.shape
    lc = ctx.shape[1]
    t = b * l
    layer = 0

    w = w_in[layer]
    main_cols = [_ORIG[k] for k in ("gdn_qkv", "gdn_z", "ml_q", "ml_k", "ml_v", "ml_o", "mg_gdn", "mg_ml")]
    w_main = jnp.concatenate([w[:, a:e] for a, e in main_cols], axis=1).astype(BF16)
    w_gate = jnp.concatenate([w[:, _ORIG["gdn_gate"][0]:_ORIG["gdn_gate"][1]],
                              w[:, _ORIG["ml_gate"][0]:_ORIG["ml_gate"][1]],
                              jnp.zeros((d, LANES - 64), F32)], axis=1).astype(BF16)
    zeros16 = jnp.zeros((16,), F32)
    gp_add = jnp.concatenate([zeros16, gdn_dt_bias[layer].reshape(-1), ml_i_bias[layer].reshape(-1),
                              ml_f_bias[layer].reshape(-1), jnp.zeros((LANES - 64,), F32)])
    gp_mul = jnp.concatenate([zeros16, -jnp.exp(gdn_a_log[layer].astype(F32)).reshape(-1),
                              jnp.zeros((LANES - 32,), F32)])
    gparams = jnp.zeros((8, LANES), F32).at[0].set(gp_add).at[1].set(gp_mul)
    conv_w8 = jnp.zeros((8, gdn_conv_w.shape[2]), F32).at[0:GDN_CONV].set(gdn_conv_w[layer])
    wr = w_router[layer].T
    wr_hi = wr.astype(BF16)
    wr_lo = (wr - wr_hi.astype(F32)).astype(BF16)

    n_mod_rows = -(-(b + 1) // 8) * 8
    cc = jnp.zeros((n_mod_rows, d), F32).at[0:b].set(c).at[b].set(c_ctx)
    mod = _ada_mod(cc, w_ada[layer], b_ada[layer])
    mod3 = mod.reshape(n_mod_rows, 1, 6 * d)

    x2d = x.reshape(t, d)
    tm_l = min(1024, l)
    proj_l, gate_l = _project(x2d, mod3, lambda i: (i * tm_l) // l, norm1_g[layer], w_main, w_gate, tm_l)
    tm_c = min(1024, b * lc)
    proj_c, gate_c = _project(ctx.reshape(b * lc, d), mod3, lambda i: b, norm1_g[layer], w_main, w_gate, tm_c)
    proj_l3 = proj_l.reshape(b, l, N_MAIN)
    proj_c3 = proj_c.reshape(b, lc, N_MAIN)

    gate_l_cm = _col_major(gate_l.reshape(b, l, LANES)).reshape(t, LANES)
    gd_c, ml_c = _gate_prep(gate_c, gparams, True, True)
    gd_l, = _gate_prep(gate_l, gparams, True, False)
    ml_l, = _gate_prep(gate_l_cm, gparams, False, True)

    nb = b // BATCH_PARTS
    tp = nb * l
    n_assign = tp * TOP_K
    n_blocks = (n_assign + N_EXPERTS * (EXPERT_ROWS - 1)) // EXPERT_ROWS + 1
    n_slots = n_blocks * EXPERT_ROWS
    gd_c3, gd_l3 = gd_c.reshape(b, lc, LANES), gd_l.reshape(b, l, LANES)
    ml_c3, ml_l3 = ml_c.reshape(b, lc, LANES), ml_l.reshape(b, l, LANES)
    wbg, wbm, wo = (w_branch_gdn[layer].astype(BF16), w_branch_ml[layer].astype(BF16), w_out[layer].astype(BF16))
    w_sh_gu, w_sh_dn = w_sh_gate_up[layer].astype(BF16), w_sh_down[layer].astype(BF16)
    outs = []
    for part in range(BATCH_PARTS):
        b0, tok0 = part * nb, part * tp
        y_gdn = _gdn(proj_c3, proj_l3, conv_w8, gd_c3, gd_l3, gdn_norm_g[layer], b0, nb)
        q_cm = _col_major(proj_l3[b0:b0 + nb, :, COL_ML_Q:COL_ML_Q + HEADS * ML_DK])
        k_cm = _col_major(proj_l3[b0:b0 + nb, :, COL_ML_K:COL_ML_K + HEADS * ML_DK])
        v_cm = _col_major(proj_l3[b0:b0 + nb, :, COL_ML_V:COL_ML_V + HEADS * HEAD_V])
        h_ml = _row_major(_mlstm(proj_c3, q_cm, k_cm, v_cm, ml_c3, ml_l3, b0))
        x1, h2, scores_t = _merge(y_gdn.reshape(tp, d), h_ml.reshape(tp, d), proj_l, x2d, mod3, l, ml_norm_g[layer],
                                norm2_g[layer], wbg, wbm, wo, wr_hi, wr_lo, tok0, tm=min(512, l))
        idx, wts, rank, cnt = _route(scores_t, router_bias[layer])
        counts = cnt[:, 0].astype(jnp.int32)
        pstart, blk_expert, n_used, next_expert = _block_table(counts, n_blocks)
        pos = _slots(idx, rank, pstart).reshape(n_assign)
        tok_ids = jnp.broadcast_to((jnp.arange(n_assign, dtype=jnp.int32) % tp)[:, None], (n_assign, LANES))
        scattered = _sc_scatter_rows(tok_ids, pos, n_slots)[:, 0]
        in_expert = (jnp.arange(n_slots, dtype=jnp.int32).reshape(n_blocks, EXPERT_ROWS)
                     - pstart[blk_expert][:, None])
        valid = (in_expert < counts[blk_expert][:, None]).reshape(n_slots)
        tok_slot = jnp.where(valid, scattered, jnp.arange(n_slots, dtype=jnp.int32) % tp)
        xb = h2.at[tok_slot + tok0].get(mode="promise_in_bounds")
        yb = _experts(xb, blk_expert, n_used, next_expert, w_exp_gate_up[layer], w_exp_down[layer])
        yg = yb.at[pos].get(mode="promise_in_bounds", unique_indices=True).reshape(TOP_K, tp, d)
        outs.append(_final(x1, h2, yg, wts.T, mod3, l, w_sh_gu, w_sh_dn, final_norm_g, tok0, tm=min(256, l)))
    return jnp.concatenate(outs, axis=0).reshape(b, l, d)
```

```python
import functools

import jax
import jax.numpy as jnp
from jax import lax
from jax.experimental import pallas as pl
from jax.experimental.pallas import tpu as pltpu
from jax.experimental.pallas import tpu_sc as plsc

F32 = jnp.float32
BF16 = jnp.bfloat16
HI = lax.Precision.HIGHEST

EPS = 1e-6
CHUNK = 64
GRID_W = 64
HEADS = 8
HEAD_V = 128
GDN_DK = 128
ML_DK = 64
GDN_CONV = 5
N_EXPERTS = 256
TOP_K = 8
N_GROUPS = 8
TOPK_GROUPS = 4
ROUTED_SCALE = 2.5
EXPERT_ROWS = 512
BATCH_PARTS = 1
GDN_STEPS = 4
GDN_CTX_STEPS = 4
RING = 2 * GDN_STEPS
LANES = 128
VMEM_LIMIT = 56 * 1024 * 1024

COL_GDN_QKV = 0
COL_GDN_Z = 3072
COL_ML_Q = 4096
COL_ML_K = 4608
COL_ML_V = 5120
COL_ML_O = 6144
COL_MG_GDN = 7168
COL_MG_ML = 8192
N_MAIN = 9216
_ORIG = dict(gdn_qkv=(0, 3072), gdn_z=(3072, 4096), gdn_gate=(4096, 4128), ml_q=(4128, 4640),
             ml_k=(4640, 5152), ml_v=(5152, 6176), ml_o=(6176, 7200), ml_gate=(7200, 7232),
             mg_gdn=(7232, 8256), mg_ml=(8256, 9280))


def _params(sem, vmem=VMEM_LIMIT):
    return pltpu.CompilerParams(dimension_semantics=sem, vmem_limit_bytes=vmem)


def _dot(a, b, precision=None):
    return jnp.dot(a, b, preferred_element_type=F32, precision=precision)


def _dot_nt(a, b, precision=None):
    return lax.dot_general(a, b, (((1,), (1,)), ((), ())), preferred_element_type=F32, precision=precision)


def _dot_tn(a, b, precision=None):
    return lax.dot_general(a, b, (((0,), (0,)), ((), ())), preferred_element_type=F32, precision=precision)


def _sigmoid(x):
    return 1.0 / (1.0 + jnp.exp(-x))


def _softplus(x):
    return jnp.maximum(x, 0.0) + jnp.log(1.0 + jnp.exp(-jnp.abs(x)))


def _ada_kernel(c_ref, w_ref, b_ref, o_ref):
    c = c_ref[...]
    sc = c * _sigmoid(c)
    o_ref[...] = _dot(sc, w_ref[...], HI) + b_ref[...]


def _ada_mod(cc, w_ada, b_ada, tn=1536):
    r, d = cc.shape
    n = w_ada.shape[1]
    return pl.pallas_call(
        _ada_kernel,
        grid=(n // tn,),
        in_specs=[pl.BlockSpec((r, d), lambda j: (0, 0)),
                  pl.BlockSpec((d, tn), lambda j: (0, j)),
                  pl.BlockSpec((1, tn), lambda j: (0, j))],
        out_specs=pl.BlockSpec((r, tn), lambda j: (0, j)),
        out_shape=jax.ShapeDtypeStruct((r, n), F32),
        compiler_params=_params(("arbitrary",)),
    )(cc, w_ada, b_ada.reshape(1, n))


def _proj_kernel(x_ref, mod_ref, g_ref, w_ref, wg_ref, o_ref, og_ref, hn_ref):
    d = x_ref.shape[1]

    @pl.when(pl.program_id(1) == 0)
    def _():
        x = x_ref[...]
        y = x * lax.rsqrt(jnp.mean(x * x, axis=-1, keepdims=True) + EPS) * g_ref[...]
        shift = mod_ref[0, :, 0:d]
        scale = mod_ref[0, :, d:2 * d]
        h = (y * (1.0 + scale) + shift).astype(BF16)
        hn_ref[...] = h
        og_ref[...] = _dot(h, wg_ref[...])

    o_ref[...] = _dot(hn_ref[...], w_ref[...]).astype(o_ref.dtype)


def _project(x2d, mod3, mod_row_of_tile, norm_g, w_main, w_gate, tm, tn=2304):
    t, d = x2d.shape
    n = w_main.shape[1]
    return pl.pallas_call(
        _proj_kernel,
        grid=(t // tm, n // tn),
        in_specs=[pl.BlockSpec((tm, d), lambda i, j: (i, 0)),
                  pl.BlockSpec((1, 1, mod3.shape[2]), lambda i, j: (mod_row_of_tile(i), 0, 0)),
                  pl.BlockSpec((1, d), lambda i, j: (0, 0)),
                  pl.BlockSpec((d, tn), lambda i, j: (0, j)),
                  pl.BlockSpec((d, LANES), lambda i, j: (0, 0))],
        out_specs=[pl.BlockSpec((tm, tn), lambda i, j: (i, j)),
                   pl.BlockSpec((tm, LANES), lambda i, j: (i, 0))],
        out_shape=[jax.ShapeDtypeStruct((t, n), BF16), jax.ShapeDtypeStruct((t, LANES), F32)],
        scratch_shapes=[pltpu.VMEM((tm, d), BF16)],
        compiler_params=_params(("arbitrary", "arbitrary")),
    )(x2d, mod3, norm_g.reshape(1, d), w_main, w_gate)


def _gate_kernel(g_ref, p_ref, *out_refs, want_gd, want_ml):
    gd_ref = out_refs[0] if want_gd else None
    ml_ref = out_refs[-1] if want_ml else None
    rows = g_ref.shape[0]
    raw = g_ref[...] + p_ref[0:1, :]
    lane = lax.broadcasted_iota(jnp.int32, raw.shape, 1)
    sp = _softplus(raw)
    vals = jnp.where(lane < 16, _sigmoid(raw),
                     jnp.where(lane < 32, p_ref[1:2, :] * sp,
                               jnp.where(lane < 48, raw,
                                         jnp.where(lane < 64, -_softplus(-raw), 0.0))))
    ri = lax.broadcasted_iota(jnp.int32, (CHUNK, CHUNK), 0)
    ci = lax.broadcasted_iota(jnp.int32, (CHUNK, CHUNK), 1)
    tri_f = (ri >= ci).astype(BF16)
    tri_b = (ri <= ci).astype(BF16)
    lane_c = lax.broadcasted_iota(jnp.int32, (CHUNK, LANES), 1)
    row_c = lax.broadcasted_iota(jnp.int32, (CHUNK, LANES), 0)
    fwd_lane = (lane_c % 16) < 8

    def cumsum(tri, parts):
        h, m, lo = parts
        return _dot(tri, h) + (_dot(tri, m) + _dot(tri, lo))

    for c in range(rows // CHUNK):
        blk = vals[c * CHUNK:(c + 1) * CHUNK, :]
        parts = _split3(blk)
        cum = jnp.where(fwd_lane, cumsum(tri_f, parts), cumsum(tri_b, parts))
        if want_gd:
            gd_ref[c * CHUNK:(c + 1) * CHUNK, :] = jnp.where(lane_c < 16, blk, jnp.where(lane_c < 32, cum, 0.0))
        if want_ml:
            bcum = pltpu.roll(cum, LANES - 16, axis=1)
            gmb = blk - bcum
            cmf, cmb = gmb, gmb
            for s in (1, 2, 4, 8, 16, 32):
                cmf = jnp.maximum(cmf, jnp.where(row_c >= s, pltpu.roll(cmf, s, axis=0), -jnp.inf))
                cmb = jnp.maximum(cmb, jnp.where(row_c < CHUNK - s, pltpu.roll(cmb, CHUNK - s, axis=0), -jnp.inf))
            cm = jnp.where(fwd_lane, cmf, cmb)
            ml = jnp.where(lane_c < 16, pltpu.roll(gmb, LANES - 32, axis=1),
                           jnp.where(lane_c < 32, pltpu.roll(cm, LANES - 16, axis=1),
                                     jnp.where(lane_c < 48, bcum, 0.0)))
            ml_ref[c * CHUNK:(c + 1) * CHUNK, :] = ml


def _gate_prep(graw, gparams, want_gd, want_ml, tm=256):
    t = graw.shape[0]
    spec = pl.BlockSpec((tm, LANES), lambda i: (i, 0))
    n_out = int(want_gd) + int(want_ml)
    return pl.pallas_call(
        functools.partial(_gate_kernel, want_gd=want_gd, want_ml=want_ml),
        grid=(t // tm,),
        in_specs=[spec, pl.BlockSpec((8, LANES), lambda i: (0, 0))],
        out_specs=[spec] * n_out,
        out_shape=[jax.ShapeDtypeStruct((t, LANES), F32)] * n_out,
        compiler_params=_params(("arbitrary",)),
    )(graw, gparams)


def _split3(a):
    h = a.astype(BF16)
    r = a - h.astype(F32)
    m = r.astype(BF16)
    return h, m, (r - m.astype(F32)).astype(BF16)


def _lane_picks(x, lanes):
    li = lax.broadcasted_iota(jnp.int32, (LANES, LANES), 0)
    ci = lax.broadcasted_iota(jnp.int32, (LANES, LANES), 1)
    want = jnp.full((LANES, LANES), -1, jnp.int32)
    for j, lane in enumerate(lanes):
        want = jnp.where(ci == j, lane, want)
    sel = (li == want).astype(BF16)
    h, m, lo = _split3(x)
    cols = _dot(h, sel) + (_dot(m, sel) + _dot(lo, sel))
    return [jnp.broadcast_to(cols[:, j:j + 1], x.shape) for j in range(len(lanes))]


def _dir_masks(direction):
    ri = lax.broadcasted_iota(jnp.int32, (CHUNK, CHUNK), 0)
    ci = lax.broadcasted_iota(jnp.int32, (CHUNK, CHUNK), 1)
    if direction == 0:
        return ri >= ci, ri > ci
    return ri <= ci, ri < ci


def _gdn_kernel(qc_ref, kc_ref, vc_ref, ql_ref, kl_ref, vl_ref, z_ref, cwq_ref, cwk_ref, cwv_ref,
                gdc_ref, gdl_ref, ng_ref, y_ref,
                xpad, qs, ks, vs, beta_t, cg_t, wq_r, u_r, kd_r, qk_r, dc_r, out_s):
    lc = qc_ref.shape[1]
    ll = ql_ref.shape[1]
    lt = lc + ll
    n_c, n_l = lc // CHUNK, ll // CHUNK
    n_t = n_c + n_l
    rb = 256

    def l2n(x):
        return x * lax.rsqrt(jnp.sum(x * x, axis=-1, keepdims=True) + EPS)

    def prep(src_ref, cw_ref, dst, off, ls, kind):
        xpad[0:8, :] = jnp.zeros((8, LANES), F32)
        xpad[8:8 + ls, :] = src_ref[0].astype(F32)
        xpad[8 + ls:16 + ls, :] = jnp.zeros((8, LANES), F32)
        step = min(rb, ls)
        for r0 in range(0, ls, step):
            acc = jnp.zeros((step, LANES), F32)
            for t in range(GDN_CONV):
                s0 = r0 + 8 - GDN_CONV // 2 + t
                acc = acc + cw_ref[t:t + 1, :] * xpad[s0:s0 + step, :]
            y = acc * _sigmoid(acc)
            if kind == "q":
                y = l2n(y) * (GDN_DK ** -0.5)
            elif kind == "k":
                y = l2n(y)
            dst[off + r0:off + r0 + step, :] = y

    prep(qc_ref, cwq_ref, qs, 0, lc, "q")
    prep(kc_ref, cwk_ref, ks, 0, lc, "k")
    prep(vc_ref, cwv_ref, vs, 0, lc, "v")
    prep(ql_ref, cwq_ref, qs, lc, ll, "q")
    prep(kl_ref, cwk_ref, ks, lc, ll, "k")
    prep(vl_ref, cwv_ref, vs, lc, ll, "v")

    head = pl.program_id(1)

    def build_tables(src_ref, off, ls):
        step = min(rb, ls)
        for r0 in range(0, ls, step):
            picked = _lane_picks(src_ref[0, r0:r0 + step, :], [8 * d + head for d in range(2)]
                                 + [16 + 8 * d + head for d in range(2)])
            for d in range(2):
                beta_t[d, off + r0:off + r0 + step, :] = picked[d]
                cg_t[d, off + r0:off + r0 + step, :] = picked[2 + d]

    build_tables(gdc_ref, 0, lc)
    build_tables(gdl_ref, lc, ll)

    def bwd_chunk(t):
        return jnp.where(t < n_c, n_c - 1 - t, n_t + n_c - 1 - t)

    row_p = lax.broadcasted_iota(jnp.int32, (CHUNK, LANES), 0)
    lane_p = lax.broadcasted_iota(jnp.int32, (CHUNK, LANES), 1)
    fwd_p = lane_p < CHUNK
    col_p = jnp.where(fwd_p, lane_p, lane_p - CHUNK)
    signed = jnp.where(fwd_p, row_p - col_p, col_p - row_p)
    incl_p = signed >= 0
    strict_p = signed > 0
    eye_p = (row_p == col_p).astype(F32)
    keep_f = fwd_p.astype(BF16)
    keep_b = (1.0 - fwd_p.astype(F32)).astype(BF16)

    def block_diag(top, bottom):
        zero = jnp.zeros(top.shape, top.dtype)
        return jnp.concatenate([jnp.concatenate([top, zero], axis=1), jnp.concatenate([zero, bottom], axis=1)], axis=0)

    def pair_diag(xp):
        return jnp.concatenate([xp * keep_f, xp * keep_b], axis=0)

    def pair_times3(a, xp):
        ah = a.astype(BF16)
        al = (a - ah.astype(F32)).astype(BF16)
        xh = xp.astype(BF16)
        xl = (xp - xh.astype(F32)).astype(BF16)
        dh = pair_diag(xh)
        return _dot(ah, dh) + (_dot(ah, pair_diag(xl)) + _dot(al, dh))

    def prep_chain(t):
        tc = jnp.minimum(t, n_t - 1)
        q, k, v, beta, cgc, kb, ecg = [], [], [], [], [], [], []
        for d in range(2):
            c = tc if d == 0 else bwd_chunk(tc)
            rows = pl.ds(pl.multiple_of(c * CHUNK, CHUNK), CHUNK)
            q.append(qs[rows, :])
            k.append(ks[rows, :])
            v.append(vs[rows, :])
            beta.append(beta_t[d, rows, :])
            cgc.append(cg_t[d, rows, :])
        kdiag = block_diag(k[0].astype(BF16), k[1].astype(BF16))
        kk = _dot_nt(jnp.concatenate([k[0], k[1]], axis=1).astype(BF16), kdiag)
        qk = _dot_nt(jnp.concatenate([q[0], q[1]], axis=1).astype(BF16), kdiag)
        yield
        beta_p = jnp.where(fwd_p, beta[0], beta[1])
        cg_p = jnp.where(fwd_p, cgc[0], cgc[1])
        cgr_p = jnp.transpose(jnp.concatenate([cgc[0], cgc[1]], axis=0))[0:CHUNK, :]
        decay = jnp.exp(jnp.where(incl_p, cg_p - cgr_p, -jnp.inf))
        qkd = qk * decay
        for d in range(2):
            slot = (t % RING) * 2 + d
            last = CHUNK - 1 if d == 0 else 0
            cg_last = cgc[d][last:last + 1, :]
            ecg.append(jnp.exp(cgc[d]))
            kb.append(k[d] * beta[d])
            qk_r[slot] = qkd[:, d * CHUNK:(d + 1) * CHUNK]
            wq_r[slot, CHUNK:2 * CHUNK, :] = q[d] * ecg[d]
            kd_r[slot] = k[d] * jnp.exp(cg_last - cgc[d])
            dc_r[slot] = jnp.broadcast_to(jnp.exp(cg_last), (8, LANES))
        x = jnp.where(strict_p, -(beta_p * kk) * decay, 0.0)
        tinv = eye_p + x
        x = pair_times3(x, x)
        yield
        for _ in range(4):
            both = pair_times3(jnp.concatenate([tinv, x], axis=0), x)
            tinv, x = tinv + both[0:CHUNK, :], both[CHUNK:2 * CHUNK, :]
            yield
        tinv = (tinv + pair_times3(tinv, x)).astype(BF16)
        yield
        w = _dot(tinv, block_diag((kb[0] * ecg[0]).astype(BF16), (kb[1] * ecg[1]).astype(BF16)))
        u = _dot(tinv, block_diag((v[0] * beta[0]).astype(BF16), (v[1] * beta[1]).astype(BF16)))
        for d in range(2):
            slot = (t % RING) * 2 + d
            wq_r[slot, 0:CHUNK, :] = w[:, d * LANES:(d + 1) * LANES]
            u_r[slot] = u[:, d * LANES:(d + 1) * LANES]

    out_s[...] = jnp.zeros(out_s.shape, F32)

    def scan_chain(d, t0, steps, s, with_out, result):
        for j in range(steps):
            t = t0 + j
            slot = (t % RING) * 2 + d
            ws = _dot(wq_r[slot], s)
            yield
            v_new = u_r[slot] - ws[0:CHUNK, :]
            if with_out:
                c = t if d == 0 else bwd_chunk(t)
                o = ws[CHUNK:2 * CHUNK, :] + _dot(qk_r[slot], v_new)
                l0 = pl.multiple_of((c - n_c) * CHUNK, CHUNK)
                out_s[pl.ds(l0, CHUNK), :] += o
            s = s * dc_r[slot][0:1, :] + _dot_tn(kd_r[slot], v_new)
            yield
        result[d] = s

    def lockstep(chains):
        chains = list(chains)
        while chains:
            alive = []
            for ch in chains:
                try:
                    next(ch)
                    alive.append(ch)
                except StopIteration:
                    pass
            chains = alive

    def group_body(i, carry, t_base, steps, ahead, with_out):
        t0 = t_base + steps * i
        result = [None, None]
        lockstep([scan_chain(d, t0, steps, carry[d], with_out, result) for d in range(2)]
                 + [prep_chain(t0 + steps + j) for j in range(ahead)])
        return result[0], result[1]

    def run_groups(carry, t_base, n_steps, steps, ahead_last, with_out):
        n_groups = n_steps // steps
        carry = lax.fori_loop(0, n_groups - 1, functools.partial(
            group_body, t_base=t_base, steps=steps, ahead=steps, with_out=with_out), carry)
        return group_body(n_groups - 1, carry, t_base, steps, ahead_last, with_out)

    lockstep([prep_chain(j) for j in range(GDN_CTX_STEPS)])
    s0 = jnp.zeros((GDN_DK, HEAD_V), F32)
    carry = run_groups((s0, s0), 0, n_c, GDN_CTX_STEPS, GDN_STEPS, False)
    run_groups(carry, n_c, n_l, GDN_STEPS, 0, True)


    def out_body(i, carry):
        r0 = pl.multiple_of(i * rb, rb)
        o = out_s[pl.ds(r0, rb), :]
        z = z_ref[0, pl.ds(r0, rb), :].astype(F32)
        y = o * lax.rsqrt(jnp.mean(o * o, axis=-1, keepdims=True) + EPS) * ng_ref[...]
        y_ref[0, pl.ds(r0, rb), :] = (y * (z * _sigmoid(z))).astype(y_ref.dtype)
        return carry

    lax.fori_loop(0, ll // rb, out_body, 0)


def _gdn(proj_c, proj_l, conv_w8, gd_c, gd_l, norm_g, b0, nb):
    lc = proj_c.shape[1]
    ll = proj_l.shape[1]
    lt = lc + ll
    qb, kb_, vb, zb = (COL_GDN_QKV // LANES, COL_GDN_QKV // LANES + HEADS, COL_GDN_QKV // LANES + 2 * HEADS,
                       COL_GDN_Z // LANES)

    def seq_spec(l, col0):
        return pl.BlockSpec((1, l, LANES), lambda i, h: (i + b0, 0, col0 + h))

    def cw_spec(col0):
        return pl.BlockSpec((8, LANES), lambda i, h: (0, col0 + h))

    return pl.pallas_call(
        _gdn_kernel,
        grid=(nb, HEADS),
        in_specs=[seq_spec(lc, qb), seq_spec(lc, kb_), seq_spec(lc, vb),
                  seq_spec(ll, qb), seq_spec(ll, kb_), seq_spec(ll, vb), seq_spec(ll, zb),
                  cw_spec(0), cw_spec(HEADS), cw_spec(2 * HEADS),
                  pl.BlockSpec((1, lc, LANES), lambda i, h: (i + b0, 0, 0)),
                  pl.BlockSpec((1, ll, LANES), lambda i, h: (i + b0, 0, 0)),
                  pl.BlockSpec((1, LANES), lambda i, h: (0, 0))],
        out_specs=pl.BlockSpec((1, ll, LANES), lambda i, h: (i, 0, h)),
        out_shape=jax.ShapeDtypeStruct((nb, ll, HEADS * HEAD_V), BF16),
        scratch_shapes=[pltpu.VMEM((max(lc, ll) + 16, LANES), F32),
                        pltpu.VMEM((lt, LANES), F32), pltpu.VMEM((lt, LANES), F32), pltpu.VMEM((lt, LANES), F32),
                        pltpu.VMEM((2, lt, LANES), F32), pltpu.VMEM((2, lt, LANES), F32),
                        pltpu.VMEM((2 * RING, 2 * CHUNK, LANES), F32),
                        pltpu.VMEM((2 * RING, CHUNK, LANES), F32), pltpu.VMEM((2 * RING, CHUNK, LANES), F32),
                        pltpu.VMEM((2 * RING, CHUNK, CHUNK), F32),
                        pltpu.VMEM((2 * RING, 8, LANES), F32),
                        pltpu.VMEM((ll, LANES), F32)],
        compiler_params=_params(("arbitrary", "arbitrary")),
    )(proj_c, proj_c, proj_c, proj_l, proj_l, proj_l, proj_l, conv_w8, conv_w8, conv_w8,
      gd_c, gd_l, norm_g.reshape(1, LANES))


def _mlstm_kernel(qc_ref, kc_ref, vc_ref, ql_ref, kl_ref, vl_ref, mlc_ref, mll_ref, h_ref, out_s, tabs):
    lc = qc_ref.shape[1]
    ll = ql_ref.shape[1]
    n_c, n_l = lc // CHUNK, ll // CHUNK
    n_t = n_c + n_l
    lt = lc + ll
    pair = pl.program_id(1)
    lane = lax.broadcasted_iota(jnp.int32, (CHUNK, LANES), 1)
    ones_v = jnp.ones((CHUNK, HEAD_V), BF16)
    chains = [(hh, d) for hh in range(2) for d in range(2)]
    hmask = [((lane // ML_DK) == hh).astype(F32) for hh in range(2)]
    incl = [_dir_masks(d)[0] for d in range(2)]

    def build_tables(src_ref, off, ls):
        step = min(256, ls)
        for r0 in range(0, ls, step):
            lanes = [16 * j + 8 * d + 2 * pair + hh for hh, d in chains for j in range(3)]
            for g, tab in enumerate(_lane_picks(src_ref[0, r0:r0 + step, :], lanes)):
                tabs[g, off + r0:off + r0 + step, :] = tab

    build_tables(mlc_ref, 0, lc)
    build_tables(mll_ref, lc, ll)

    def wide(a):
        return jnp.concatenate([a, a], axis=1)

    def chain(hh, d, c, state, is_ctx, result):
        cs, ms = state
        last = CHUNK - 1 if d == 0 else 0
        if is_ctx:
            rows = pl.ds(pl.multiple_of(c * CHUNK, CHUNK), CHUNK)
            q_ref, k_ref, v_ref = qc_ref, kc_ref, vc_ref
        else:
            rows = pl.ds(pl.multiple_of((c - n_c) * CHUNK, CHUNK), CHUNK)
            q_ref, k_ref, v_ref = ql_ref, kl_ref, vl_ref
        q = (q_ref[0, rows, :].astype(F32) * hmask[hh]).astype(BF16)
        k = k_ref[0, rows, :].astype(F32) * (hmask[hh] * (ML_DK ** -0.5))
        v = jnp.concatenate([v_ref[0, rows, hh * HEAD_V:(hh + 1) * HEAD_V], ones_v], axis=1)
        n = chains.index((hh, d))
        trows = pl.ds(pl.multiple_of(c * CHUNK, CHUNK), CHUNK)
        gmb = tabs[3 * n, trows, :]
        gmb_t = jnp.transpose(gmb)[0:CHUNK, :]
        cm = tabs[3 * n + 1, trows, :]
        bc = tabs[3 * n + 2, trows, :]
        qk = _dot_nt(q, k.astype(BF16))
        yield
        cm_last = cm[last:last + 1, :]
        b_last = bc[last:last + 1, :]
        mm = jnp.maximum(ms, cm)
        p = jnp.where(incl[d], jnp.exp(gmb_t - mm[:, 0:CHUNK]), 0.0) * qk
        wk = (k * jnp.exp(gmb - cm_last)).astype(BF16)
        inter = _dot(q, cs.astype(BF16))
        intra = _dot(p.astype(BF16), v)
        c_loc = _dot_tn(wk, v)
        yield
        if not is_ctx:
            nd = wide(jnp.exp(ms - mm)) * inter + intra
            hv = nd[:, 0:HEAD_V] / jnp.maximum(jnp.abs(nd[:, HEAD_V:2 * HEAD_V]), jnp.exp(-(bc + mm)))
            l0 = pl.multiple_of((c - n_c) * CHUNK, CHUNK)
            out_s[pl.ds(l0, CHUNK), hh * HEAD_V:(hh + 1) * HEAD_V] += hv
        mx = jnp.maximum(ms, cm_last)
        result[hh, d] = (wide(jnp.exp(ms - mx)) * cs + wide(jnp.exp(cm_last - mx)) * c_loc, b_last + mx)

    out_s[...] = jnp.zeros(out_s.shape, F32)

    def run(chains_iter):
        live = list(chains_iter)
        while live:
            alive = []
            for ch in live:
                try:
                    next(ch)
                    alive.append(ch)
                except StopIteration:
                    pass
            live = alive

    def body(i, carry, is_ctx):
        result = {}
        gens = []
        for n, (hh, d) in enumerate(chains):
            if is_ctx:
                c = i if d == 0 else n_c - 1 - i
            else:
                c = n_c + i if d == 0 else n_t - 1 - i
            gens.append(chain(hh, d, c, carry[n], is_ctx, result))
        run(gens)
        return tuple(result[hd] for hd in chains)

    st0 = (jnp.zeros((LANES, 2 * HEAD_V), F32), jnp.zeros((1, LANES), F32))
    carry = lax.fori_loop(0, n_c, functools.partial(body, is_ctx=True), (st0,) * 4)
    lax.fori_loop(0, n_l, functools.partial(body, is_ctx=False), carry)
    h_ref[0] = out_s[...].astype(h_ref.dtype)


def _mlstm(proj_c, q_l, k_l, v_l, ml_c, ml_l, b0):
    lc = proj_c.shape[1]
    nb, ll, _ = q_l.shape
    lt = lc + ll
    qb, kb_, vb = COL_ML_Q // LANES, COL_ML_K // LANES, COL_ML_V // (2 * HEAD_V)
    return pl.pallas_call(
        _mlstm_kernel,
        grid=(nb, HEADS // 2),
        in_specs=[pl.BlockSpec((1, lc, LANES), lambda i, p: (i + b0, 0, qb + p)),
                  pl.BlockSpec((1, lc, LANES), lambda i, p: (i + b0, 0, kb_ + p)),
                  pl.BlockSpec((1, lc, 2 * HEAD_V), lambda i, p: (i + b0, 0, vb + p)),
                  pl.BlockSpec((1, ll, LANES), lambda i, p: (i, 0, p)),
                  pl.BlockSpec((1, ll, LANES), lambda i, p: (i, 0, p)),
                  pl.BlockSpec((1, ll, 2 * HEAD_V), lambda i, p: (i, 0, p)),
                  pl.BlockSpec((1, lc, LANES), lambda i, p: (i + b0, 0, 0)),
                  pl.BlockSpec((1, ll, LANES), lambda i, p: (i + b0, 0, 0))],
        out_specs=pl.BlockSpec((1, ll, 2 * HEAD_V), lambda i, p: (i, 0, p)),
        out_shape=jax.ShapeDtypeStruct((nb, ll, HEADS * HEAD_V), BF16),
        scratch_shapes=[pltpu.VMEM((ll, 2 * HEAD_V), F32), pltpu.VMEM((12, lt, LANES), F32)],
        compiler_params=_params(("arbitrary", "arbitrary")),
    )(proj_c, proj_c, proj_c, q_l, k_l, v_l, ml_c, ml_l)


def _merge_kernel(yg_ref, hm_ref, o_ref, gg_ref, gm_ref, x_ref, mod_ref, mlg_ref, n2_ref,
                  wbg_ref, wbm_ref, wo_ref, wrh_ref, wrl_ref, x1_ref, h2_ref, sc_ref):
    d = x_ref.shape[1]
    o = o_ref[...].astype(F32)
    ym = _sigmoid(o) * hm_ref[...].astype(F32)
    segs = []
    for h in range(HEADS):
        seg = ym[:, h * HEAD_V:(h + 1) * HEAD_V]
        segs.append(seg * lax.rsqrt(jnp.mean(seg * seg, axis=-1, keepdims=True) + EPS))
    ymn = jnp.concatenate(segs, axis=1) * mlg_ref[...]
    y_gdn = _dot(yg_ref[...], wbg_ref[...])
    y_ml = _dot(ymn.astype(BF16), wbm_ref[...])
    mixed = _sigmoid(gg_ref[...].astype(F32)) * y_gdn + _sigmoid(gm_ref[...].astype(F32)) * y_ml
    y = _dot(mixed.astype(BF16), wo_ref[...])
    x1 = x_ref[...] + mod_ref[0, :, 2 * d:3 * d] * y
    x1_ref[...] = x1
    hn = x1 * lax.rsqrt(jnp.mean(x1 * x1, axis=-1, keepdims=True) + EPS) * n2_ref[...]
    h2 = hn * (1.0 + mod_ref[0, :, 4 * d:5 * d]) + mod_ref[0, :, 3 * d:4 * d]
    h2_hi = h2.astype(BF16)
    h2_ref[...] = h2_hi
    h2_lo = (h2 - h2_hi.astype(F32)).astype(BF16)
    logits = _dot_nt(wrh_ref[...], h2_hi) + (_dot_nt(wrl_ref[...], h2_hi) + _dot_nt(wrh_ref[...], h2_lo))
    sc_ref[...] = _sigmoid(logits)


def _merge(y_gdn, h_ml, proj_l2d, x2d, mod3, rows_per_mod, ml_norm_g, norm2_g, wbg, wbm, wo, wr_hi, wr_lo, tok0,
           tm=512):
    t, d = y_gdn.shape
    e = wr_hi.shape[0]
    off = tok0 // tm
    row = lambda i: (i, 0)
    const = lambda i: (0, 0)
    return pl.pallas_call(
        _merge_kernel,
        grid=(t // tm,),
        in_specs=[pl.BlockSpec((tm, d), row), pl.BlockSpec((tm, d), row),
                  pl.BlockSpec((tm, d), lambda i: (i + off, COL_ML_O // d)),
                  pl.BlockSpec((tm, d), lambda i: (i + off, COL_MG_GDN // d)),
                  pl.BlockSpec((tm, d), lambda i: (i + off, COL_MG_ML // d)),
                  pl.BlockSpec((tm, d), lambda i: (i + off, 0)),
                  pl.BlockSpec((1, 1, mod3.shape[2]), lambda i: (((i + off) * tm) // rows_per_mod, 0, 0)),
                  pl.BlockSpec((1, d), const), pl.BlockSpec((1, d), const),
                  pl.BlockSpec((d, d), const), pl.BlockSpec((d, d), const), pl.BlockSpec((d, d), const),
                  pl.BlockSpec((e, d), const), pl.BlockSpec((e, d), const)],
        out_specs=[pl.BlockSpec((tm, d), row), pl.BlockSpec((tm, d), lambda i: (i + off, 0)),
                   pl.BlockSpec((e, tm), lambda i: (0, i))],
        out_shape=[jax.ShapeDtypeStruct((t, d), F32), jax.ShapeDtypeStruct((x2d.shape[0], d), BF16),
                   jax.ShapeDtypeStruct((e, t), F32)],
        compiler_params=_params(("arbitrary",)),
    )(y_gdn, h_ml, proj_l2d, proj_l2d, proj_l2d, x2d, mod3, ml_norm_g.reshape(1, d), norm2_g.reshape(1, d),
      wbg, wbm, wo, wr_hi, wr_lo)


def _expert_kernel(be_ref, nu_ref, nx_ref, x_ref, wgu_hbm, wd_hbm, y_ref, land_gu, land_d, wgu_s, wd_s, sems):
    i = pl.program_id(0)
    de = wd_hbm.shape[1]
    used = i < nu_ref[0]
    first = jnp.logical_or(i == 0, be_ref[i] != be_ref[jnp.maximum(i - 1, 0)])

    def weight_copies(ex):
        return (pltpu.make_async_copy(wgu_hbm.at[ex], land_gu, sems.at[0]),
                pltpu.make_async_copy(wd_hbm.at[ex], land_d, sems.at[1]))

    @pl.when(i == 0)
    def _():
        for cp in weight_copies(be_ref[0]):
            cp.start()

    @pl.when(jnp.logical_and(first, used))
    def _():
        for cp in weight_copies(be_ref[i]):
            cp.wait()
        wgu_s[...] = land_gu[...].astype(BF16)
        wd_s[...] = land_d[...].astype(BF16)

        @pl.when(nx_ref[i] >= 0)
        def _():
            for cp in weight_copies(nx_ref[i]):
                cp.start()

    @pl.when(used)
    def _():
        gu = _dot(x_ref[...], wgu_s[...])
        g = gu[:, 0:de]
        act = (g * _sigmoid(g)) * gu[:, de:2 * de]
        y_ref[...] = _dot(act.astype(BF16), wd_s[...]).astype(y_ref.dtype)

    @pl.when(jnp.logical_not(used))
    def _():
        y_ref[...] = jnp.zeros(y_ref.shape, y_ref.dtype)


def _experts(xb, blk_expert, n_used, next_expert, w_gu, w_down):
    n_slots, d = xb.shape
    n_blocks = n_slots // EXPERT_ROWS
    e, _, de2 = w_gu.shape
    de = de2 // 2
    return pl.pallas_call(
        _expert_kernel,
        grid_spec=pltpu.PrefetchScalarGridSpec(
            num_scalar_prefetch=3,
            grid=(n_blocks,),
            in_specs=[pl.BlockSpec((EXPERT_ROWS, d), lambda i, be, nu, nx: (i, 0)),
                      pl.BlockSpec(memory_space=pl.ANY), pl.BlockSpec(memory_space=pl.ANY)],
            out_specs=pl.BlockSpec((EXPERT_ROWS, d), lambda i, be, nu, nx: (i, 0)),
            scratch_shapes=[pltpu.VMEM((d, de2), w_gu.dtype), pltpu.VMEM((de, d), w_down.dtype),
                            pltpu.VMEM((d, de2), BF16), pltpu.VMEM((de, d), BF16),
                            pltpu.SemaphoreType.DMA((2,))]),
        out_shape=jax.ShapeDtypeStruct((n_slots, d), BF16),
        compiler_params=_params(("arbitrary",)),
    )(blk_expert, n_used, next_expert, xb, w_gu, w_down)


def _final_kernel(x1_ref, h2_ref, yg_ref, wt_ref, mod_ref, wsg_ref, wsd_ref, fg_ref, o_ref):
    d = x1_ref.shape[1]
    ds_ = wsd_ref.shape[0]
    wt = wt_ref[...]
    routed = jnp.zeros(x1_ref.shape, F32)
    for k in range(TOP_K):
        routed = routed + wt[:, k:k + 1] * yg_ref[k].astype(F32)
    gu = _dot(h2_ref[...], wsg_ref[...])
    g = gu[:, 0:ds_]
    sh = _dot(((g * _sigmoid(g)) * gu[:, ds_:2 * ds_]).astype(BF16), wsd_ref[...])
    x2 = x1_ref[...] + mod_ref[0, :, 5 * d:6 * d] * (routed + sh)
    o_ref[...] = x2 * lax.rsqrt(jnp.mean(x2 * x2, axis=-1, keepdims=True) + EPS) * fg_ref[...]


def _final(x1, h2, yg, wts, mod3, rows_per_mod, w_sh_gu, w_sh_down, final_g, tok0, tm=256):
    tp, d = x1.shape
    off = tok0 // tm
    row = lambda i: (i, 0)
    const = lambda i: (0, 0)
    return pl.pallas_call(
        _final_kernel,
        grid=(tp // tm,),
        in_specs=[pl.BlockSpec((tm, d), row), pl.BlockSpec((tm, d), lambda i: (i + off, 0)),
                  pl.BlockSpec((TOP_K, tm, d), lambda i: (0, i, 0)), pl.BlockSpec((tm, TOP_K), row),
                  pl.BlockSpec((1, 1, mod3.shape[2]), lambda i: (((i + off) * tm) // rows_per_mod, 0, 0)),
                  pl.BlockSpec(w_sh_gu.shape, const), pl.BlockSpec(w_sh_down.shape, const),
                  pl.BlockSpec((1, d), const)],
        out_specs=pl.BlockSpec((tm, d), row),
        out_shape=jax.ShapeDtypeStruct((tp, d), F32),
        compiler_params=_params(("arbitrary",)),
    )(x1, h2, yg, wts, mod3, w_sh_gu, w_sh_down, final_g.reshape(1, d))


def _route_kernel(sc_ref, bias_ref, tri_ref, idx_ref, wt_ref, rk_ref, cnt_ref, base_s):
    @pl.when(pl.program_id(0) == 0)
    def _():
        base_s[...] = jnp.zeros(base_s.shape, F32)

    scores = sc_ref[...]
    e, tn = scores.shape
    gsz = e // N_GROUPS
    sel3 = (scores + bias_ref[...]).reshape(N_GROUPS, gsz, tn)
    m1 = jnp.max(sel3, axis=1)
    is_max = sel3 == m1[:, None, :]
    n_max = jnp.sum(is_max.astype(F32), axis=1)
    m2 = jnp.max(jnp.where(is_max, -jnp.inf, sel3), axis=1)
    grp = m1 + jnp.where(n_max >= 2.0, m1, m2)
    gi = lax.broadcasted_iota(jnp.int32, (N_GROUPS, tn), 0)
    ahead = jnp.zeros((N_GROUPS, tn), F32)
    for g in range(N_GROUPS):
        row = grp[g:g + 1, :]
        ahead = ahead + jnp.logical_or(row > grp, jnp.logical_and(row == grp, g < gi)).astype(F32)
    ahead3 = jnp.broadcast_to(ahead[:, None, :], (N_GROUPS, gsz, tn))
    selm = jnp.where(ahead3 < float(TOPK_GROUPS), sel3, -jnp.inf).reshape(e, tn)
    ri = lax.broadcasted_iota(jnp.int32, (e, tn), 0).astype(F32)
    member = jnp.zeros((e, tn), F32)
    idxs, ws = [], []
    for _ in range(TOP_K):
        m = jnp.max(selm, axis=0, keepdims=True)
        idx = jnp.min(jnp.where(selm == m, ri, float(e)), axis=0, keepdims=True)
        hit = ri == idx
        ws.append(jnp.sum(jnp.where(hit, scores, 0.0), axis=0, keepdims=True))
        idxs.append(idx)
        selm = jnp.where(hit, -jnp.inf, selm)
        member = jnp.where(hit, 1.0, member)
    w = jnp.concatenate(ws, axis=0)
    wt_ref[...] = w / jnp.sum(w, axis=0, keepdims=True) * ROUTED_SCALE
    idx_ref[...] = jnp.concatenate(idxs, axis=0).astype(jnp.int32)
    cum = _dot(member.astype(BF16), tri_ref[...]) + base_s[...]
    rk_ref[...] = jnp.concatenate(
        [jnp.sum(jnp.where(ri == idx, cum, 0.0), axis=0, keepdims=True) for idx in idxs], axis=0).astype(jnp.int32)
    total = base_s[...] + jnp.sum(member, axis=1, keepdims=True)
    base_s[...] = total
    cnt_ref[...] = total


def _route(scores_t, router_bias):
    e, t = scores_t.shape
    tn = LANES
    bias = jnp.broadcast_to(router_bias.astype(F32)[:, None], (e, tn))
    tri = (jnp.arange(tn)[:, None] < jnp.arange(tn)[None, :]).astype(BF16)
    tok = pl.BlockSpec((TOP_K, tn), lambda i: (0, i))
    const = lambda i: (0, 0)
    return pl.pallas_call(
        _route_kernel,
        grid=(t // tn,),
        in_specs=[pl.BlockSpec((e, tn), lambda i: (0, i)), pl.BlockSpec((e, tn), const),
                  pl.BlockSpec((tn, tn), const)],
        out_specs=[tok, tok, tok, pl.BlockSpec((e, tn), const)],
        out_shape=[jax.ShapeDtypeStruct((TOP_K, t), jnp.int32), jax.ShapeDtypeStruct((TOP_K, t), F32),
                   jax.ShapeDtypeStruct((TOP_K, t), jnp.int32), jax.ShapeDtypeStruct((e, tn), F32)],
        scratch_shapes=[pltpu.VMEM((e, tn), F32)],
        compiler_params=_params(("arbitrary",)),
    )(scores_t, bias, tri)


def _slot_kernel(idx_ref, rk_ref, ps_ref, pos_ref):
    e, tn = ps_ref.shape
    ri = lax.broadcasted_iota(jnp.int32, (e, tn), 0)
    ps = ps_ref[...]
    rows = [jnp.sum(jnp.where(ri == idx_ref[k:k + 1, :], ps, 0.0), axis=0, keepdims=True) for k in range(TOP_K)]
    pos_ref[...] = rk_ref[...] + jnp.concatenate(rows, axis=0).astype(jnp.int32)


def _slots(idx, rank, pstart):
    k, t = idx.shape
    e = pstart.shape[0]
    tn = LANES
    tok = pl.BlockSpec((k, tn), lambda i: (0, i))
    return pl.pallas_call(
        _slot_kernel,
        grid=(t // tn,),
        in_specs=[tok, tok, pl.BlockSpec((e, tn), lambda i: (0, 0))],
        out_specs=tok,
        out_shape=jax.ShapeDtypeStruct((k, t), jnp.int32),
        compiler_params=_params(("arbitrary",)),
    )(idx, rank, jnp.broadcast_to(pstart.astype(F32)[:, None], (e, tn)))


def _sc_scatter_rows(vals, idx, n_rows, window=LANES):
    n, width = vals.shape
    mesh = plsc.VectorSubcoreMesh(core_axis_name="core", subcore_axis_name="subcore")

    @pl.kernel(out_type=jax.ShapeDtypeStruct((n_rows, width), vals.dtype), mesh=mesh, scratch_types=[])
    def scatter(v_hbm, i_hbm, o_hbm):
        def body(v_vmem, i_vmem):
            pltpu.sync_copy(v_vmem, o_hbm.at[i_vmem.at[0]])

        pltpu.emit_pipeline(
            body,
            grid=(n // window,),
            in_specs=[pl.BlockSpec((window, width), lambda i: (i, 0)),
                      pl.BlockSpec((1, window), lambda i: (0, i))],
            out_specs=[],
            core_axis_name=("core", "subcore"),
            dimension_semantics=(pltpu.PARALLEL,),
        )(v_hbm, i_hbm)

    return scatter(vals, idx.reshape(1, n))


def _block_table(counts, n_blocks):
    padded = (counts + EXPERT_ROWS - 1) // EXPERT_ROWS * EXPERT_ROWS
    pend = jnp.cumsum(padded)
    first_slot = jnp.arange(n_blocks, dtype=jnp.int32) * EXPERT_ROWS
    blk_expert = jnp.minimum(jnp.sum((pend[None, :] <= first_slot[:, None]).astype(jnp.int32), axis=1),
                             N_EXPERTS - 1)
    e_ids = jnp.arange(N_EXPERTS, dtype=jnp.int32)
    later = lax.cummin(jnp.where(counts > 0, e_ids, N_EXPERTS)[::-1])[::-1]
    next_tab = jnp.concatenate([later[1:], jnp.full((1,), N_EXPERTS, jnp.int32)])
    next_tab = jnp.where(next_tab < N_EXPERTS, next_tab, -1)
    return (pend - padded, blk_expert, (pend[-1] // EXPERT_ROWS).astype(jnp.int32).reshape(1),
            next_tab[blk_expert].astype(jnp.int32))


def _col_major(t):
    b, l, f = t.shape
    rows = l // GRID_W
    return t.reshape(b, rows, GRID_W, f).transpose(0, 2, 1, 3).reshape(b, l, f)


def _row_major(t):
    b, l, f = t.shape
    rows = l // GRID_W
    return t.reshape(b, GRID_W, rows, f).transpose(0, 2, 1, 3).reshape(b, l, f)


def kernel(x, c, ctx, c_ctx, w_ada, b_ada, norm1_g, norm2_g, w_in, gdn_conv_w, gdn_a_log, gdn_dt_bias, gdn_norm_g,
           ml_i_bias, ml_f_bias, ml_norm_g, w_branch_gdn, w_branch_ml, w_out, w_router, router_bias, w_exp_gate_up,
           w_exp_down, w_sh_gate_up, w_sh_down, final_norm_g):
    b, l, d = x.shape
    lc = ctx.shape[1]
    t = b * l
    layer = 0

    w = w_in[layer]
    main_cols = [_ORIG[k] for k in ("gdn_qkv", "gdn_z", "ml_q", "ml_k", "ml_v", "ml_o", "mg_gdn", "mg_ml")]
    w_main = jnp.concatenate([w[:, a:e] for a, e in main_cols], axis=1).astype(BF16)
    w_gate = jnp.concatenate([w[:, _ORIG["gdn_gate"][0]:_ORIG["gdn_gate"][1]],
                              w[:, _ORIG["ml_gate"][0]:_ORIG["ml_gate"][1]],
                              jnp.zeros((d, LANES - 64), F32)], axis=1).astype(BF16)
    zeros16 = jnp.zeros((16,), F32)
    gp_add = jnp.concatenate([zeros16, gdn_dt_bias[layer].reshape(-1), ml_i_bias[layer].reshape(-1),
                              ml_f_bias[layer].reshape(-1), jnp.zeros((LANES - 64,), F32)])
    gp_mul = jnp.concatenate([zeros16, -jnp.exp(gdn_a_log[layer].astype(F32)).reshape(-1),
                              jnp.zeros((LANES - 32,), F32)])
    gparams = jnp.zeros((8, LANES), F32).at[0].set(gp_add).at[1].set(gp_mul)
    conv_w8 = jnp.zeros((8, gdn_conv_w.shape[2]), F32).at[0:GDN_CONV].set(gdn_conv_w[layer])
    wr = w_router[layer].T
    wr_hi = wr.astype(BF16)
    wr_lo = (wr - wr_hi.astype(F32)).astype(BF16)

    n_mod_rows = -(-(b + 1) // 8) * 8
    cc = jnp.zeros((n_mod_rows, d), F32).at[0:b].set(c).at[b].set(c_ctx)
    mod = _ada_mod(cc, w_ada[layer], b_ada[layer])
    mod3 = mod.reshape(n_mod_rows, 1, 6 * d)

    x2d = x.reshape(t, d)
    tm_l = min(1024, l)
    proj_l, gate_l = _project(x2d, mod3, lambda i: (i * tm_l) // l, norm1_g[layer], w_main, w_gate, tm_l)
    tm_c = min(1024, b * lc)
    proj_c, gate_c = _project(ctx.reshape(b * lc, d), mod3, lambda i: b, norm1_g[layer], w_main, w_gate, tm_c)
    proj_l3 = proj_l.reshape(b, l, N_MAIN)
    proj_c3 = proj_c.reshape(b, lc, N_MAIN)

    gate_l_cm = _col_major(gate_l.reshape(b, l, LANES)).reshape(t, LANES)
    gd_c, ml_c = _gate_prep(gate_c, gparams, True, True)
    gd_l, = _gate_prep(gate_l, gparams, True, False)
    ml_l, = _gate_prep(gate_l_cm, gparams, False, True)

    nb = b // BATCH_PARTS
    tp = nb * l
    n_assign = tp * TOP_K
    n_blocks = (n_assign + N_EXPERTS * (EXPERT_ROWS - 1)) // EXPERT_ROWS + 1
    n_slots = n_blocks * EXPERT_ROWS
    gd_c3, gd_l3 = gd_c.reshape(b, lc, LANES), gd_l.reshape(b, l, LANES)
    ml_c3, ml_l3 = ml_c.reshape(b, lc, LANES), ml_l.reshape(b, l, LANES)
    wbg, wbm, wo = (w_branch_gdn[layer].astype(BF16), w_branch_ml[layer].astype(BF16), w_out[layer].astype(BF16))
    w_sh_gu, w_sh_dn = w_sh_gate_up[layer].astype(BF16), w_sh_down[layer].astype(BF16)
    outs = []
    for part in range(BATCH_PARTS):
        b0, tok0 = part * nb, part * tp
        y_gdn = _gdn(proj_c3, proj_l3, conv_w8, gd_c3, gd_l3, gdn_norm_g[layer], b0, nb)
        q_cm = _col_major(proj_l3[b0:b0 + nb, :, COL_ML_Q:COL_ML_Q + HEADS * ML_DK])
        k_cm = _col_major(proj_l3[b0:b0 + nb, :, COL_ML_K:COL_ML_K + HEADS * ML_DK])
        v_cm = _col_major(proj_l3[b0:b0 + nb, :, COL_ML_V:COL_ML_V + HEADS * HEAD_V])
        h_ml = _row_major(_mlstm(proj_c3, q_cm, k_cm, v_cm, ml_c3, ml_l3, b0))
        x1, h2, scores_t = _merge(y_gdn.reshape(tp, d), h_ml.reshape(tp, d), proj_l, x2d, mod3, l, ml_norm_g[layer],
                                norm2_g[layer], wbg, wbm, wo, wr_hi, wr_lo, tok0, tm=min(512, l))
        idx, wts, rank, cnt = _route(scores_t, router_bias[layer])
        counts = cnt[:, 0].astype(jnp.int32)
        pstart, blk_expert, n_used, next_expert = _block_table(counts, n_blocks)
        pos = _slots(idx, rank, pstart).reshape(n_assign)
        tok_ids = jnp.broadcast_to((jnp.arange(n_assign, dtype=jnp.int32) % tp)[:, None], (n_assign, LANES))
        scattered = _sc_scatter_rows(tok_ids, pos, n_slots)[:, 0]
        in_expert = (jnp.arange(n_slots, dtype=jnp.int32).reshape(n_blocks, EXPERT_ROWS)
                     - pstart[blk_expert][:, None])
        valid = (in_expert < counts[blk_expert][:, None]).reshape(n_slots)
        tok_slot = jnp.where(valid, scattered, jnp.arange(n_slots, dtype=jnp.int32) % tp)
        xb = h2.at[tok_slot + tok0].get(mode="promise_in_bounds")
        yb = _experts(xb, blk_expert, n_used, next_expert, w_exp_gate_up[layer], w_exp_down[layer])
        yg = yb.at[pos].get(mode="promise_in_bounds", unique_indices=True).reshape(TOP_K, tp, d)
        outs.append(_final(x1, h2, yg, wts.T, mod3, l, w_sh_gu, w_sh_dn, final_norm_g, tok0, tm=min(512, l)))
    return jnp.concatenate(outs, axis=0).reshape(b, l, d)
```

```python
import functools

import jax
import jax.numpy as jnp
from jax import lax
from jax.experimental import pallas as pl
from jax.experimental.pallas import tpu as pltpu
from jax.experimental.pallas import tpu_sc as plsc

F32 = jnp.float32
BF16 = jnp.bfloat16
HI = lax.Precision.HIGHEST

EPS = 1e-6
CHUNK = 64
GRID_W = 64
HEADS = 8
HEAD_V = 128
GDN_DK = 128
ML_DK = 64
GDN_CONV = 5
N_EXPERTS = 256
TOP_K = 8
N_GROUPS = 8
TOPK_GROUPS = 4
ROUTED_SCALE = 2.5
EXPERT_ROWS = 512
BATCH_PARTS = 1
GDN_STEPS = 4
GDN_CTX_STEPS = 4
ML_STEPS = 4
RING = 2 * GDN_STEPS
LANES = 128
VMEM_LIMIT = 56 * 1024 * 1024

COL_GDN_QKV = 0
COL_GDN_Z = 3072
COL_ML_Q = 4096
COL_ML_K = 4608
COL_ML_V = 5120
COL_ML_O = 6144
COL_MG_GDN = 7168
COL_MG_ML = 8192
N_MAIN = 9216
_ORIG = dict(gdn_qkv=(0, 3072), gdn_z=(3072, 4096), gdn_gate=(4096, 4128), ml_q=(4128, 4640),
             ml_k=(4640, 5152), ml_v=(5152, 6176), ml_o=(6176, 7200), ml_gate=(7200, 7232),
             mg_gdn=(7232, 8256), mg_ml=(8256, 9280))


def _params(sem, vmem=VMEM_LIMIT):
    return pltpu.CompilerParams(dimension_semantics=sem, vmem_limit_bytes=vmem)


def _dot(a, b, precision=None):
    return jnp.dot(a, b, preferred_element_type=F32, precision=precision)


def _dot_nt(a, b, precision=None):
    return lax.dot_general(a, b, (((1,), (1,)), ((), ())), preferred_element_type=F32, precision=precision)


def _dot_tn(a, b, precision=None):
    return lax.dot_general(a, b, (((0,), (0,)), ((), ())), preferred_element_type=F32, precision=precision)


def _sigmoid(x):
    return 1.0 / (1.0 + jnp.exp(-x))


def _softplus(x):
    return jnp.maximum(x, 0.0) + jnp.log(1.0 + jnp.exp(-jnp.abs(x)))


def _ada_kernel(c_ref, w_ref, b_ref, o_ref):
    c = c_ref[...]
    sc = c * _sigmoid(c)
    o_ref[...] = _dot(sc, w_ref[...], HI) + b_ref[...]


def _ada_mod(cc, w_ada, b_ada, tn=1536):
    r, d = cc.shape
    n = w_ada.shape[1]
    return pl.pallas_call(
        _ada_kernel,
        grid=(n // tn,),
        in_specs=[pl.BlockSpec((r, d), lambda j: (0, 0)),
                  pl.BlockSpec((d, tn), lambda j: (0, j)),
                  pl.BlockSpec((1, tn), lambda j: (0, j))],
        out_specs=pl.BlockSpec((r, tn), lambda j: (0, j)),
        out_shape=jax.ShapeDtypeStruct((r, n), F32),
        compiler_params=_params(("arbitrary",)),
    )(cc, w_ada, b_ada.reshape(1, n))


def _proj_kernel(x_ref, mod_ref, g_ref, w_ref, wg_ref, o_ref, og_ref, hn_ref):
    d = x_ref.shape[1]

    @pl.when(pl.program_id(1) == 0)
    def _():
        x = x_ref[...]
        y = x * lax.rsqrt(jnp.mean(x * x, axis=-1, keepdims=True) + EPS) * g_ref[...]
        shift = mod_ref[0, :, 0:d]
        scale = mod_ref[0, :, d:2 * d]
        h = (y * (1.0 + scale) + shift).astype(BF16)
        hn_ref[...] = h
        og_ref[...] = _dot(h, wg_ref[...])

    o_ref[...] = _dot(hn_ref[...], w_ref[...]).astype(o_ref.dtype)


def _project(x2d, mod3, mod_row_of_tile, norm_g, w_main, w_gate, tm, tn=2304):
    t, d = x2d.shape
    n = w_main.shape[1]
    return pl.pallas_call(
        _proj_kernel,
        grid=(t // tm, n // tn),
        in_specs=[pl.BlockSpec((tm, d), lambda i, j: (i, 0)),
                  pl.BlockSpec((1, 1, mod3.shape[2]), lambda i, j: (mod_row_of_tile(i), 0, 0)),
                  pl.BlockSpec((1, d), lambda i, j: (0, 0)),
                  pl.BlockSpec((d, tn), lambda i, j: (0, j)),
                  pl.BlockSpec((d, LANES), lambda i, j: (0, 0))],
        out_specs=[pl.BlockSpec((tm, tn), lambda i, j: (i, j)),
                   pl.BlockSpec((tm, LANES), lambda i, j: (i, 0))],
        out_shape=[jax.ShapeDtypeStruct((t, n), BF16), jax.ShapeDtypeStruct((t, LANES), F32)],
        scratch_shapes=[pltpu.VMEM((tm, d), BF16)],
        compiler_params=_params(("arbitrary", "arbitrary")),
    )(x2d, mod3, norm_g.reshape(1, d), w_main, w_gate)


def _gate_kernel(g_ref, p_ref, *out_refs, want_gd, want_ml):
    gd_ref = out_refs[0] if want_gd else None
    ml_ref = out_refs[-1] if want_ml else None
    rows = g_ref.shape[0]
    raw = g_ref[...] + p_ref[0:1, :]
    lane = lax.broadcasted_iota(jnp.int32, raw.shape, 1)
    sp = _softplus(raw)
    vals = jnp.where(lane < 16, _sigmoid(raw),
                     jnp.where(lane < 32, p_ref[1:2, :] * sp,
                               jnp.where(lane < 48, raw,
                                         jnp.where(lane < 64, -_softplus(-raw), 0.0))))
    ri = lax.broadcasted_iota(jnp.int32, (CHUNK, CHUNK), 0)
    ci = lax.broadcasted_iota(jnp.int32, (CHUNK, CHUNK), 1)
    tri_f = (ri >= ci).astype(BF16)
    tri_b = (ri <= ci).astype(BF16)
    lane_c = lax.broadcasted_iota(jnp.int32, (CHUNK, LANES), 1)
    row_c = lax.broadcasted_iota(jnp.int32, (CHUNK, LANES), 0)
    fwd_lane = (lane_c % 16) < 8

    def cumsum(tri, parts):
        h, m, lo = parts
        return _dot(tri, h) + (_dot(tri, m) + _dot(tri, lo))

    for c in range(rows // CHUNK):
        blk = vals[c * CHUNK:(c + 1) * CHUNK, :]
        parts = _split3(blk)
        cum = jnp.where(fwd_lane, cumsum(tri_f, parts), cumsum(tri_b, parts))
        if want_gd:
            gd_ref[c * CHUNK:(c + 1) * CHUNK, :] = jnp.where(lane_c < 16, blk, jnp.where(lane_c < 32, cum, 0.0))
        if want_ml:
            bcum = pltpu.roll(cum, LANES - 16, axis=1)
            gmb = blk - bcum
            cmf, cmb = gmb, gmb
            for s in (1, 2, 4, 8, 16, 32):
                cmf = jnp.maximum(cmf, jnp.where(row_c >= s, pltpu.roll(cmf, s, axis=0), -jnp.inf))
                cmb = jnp.maximum(cmb, jnp.where(row_c < CHUNK - s, pltpu.roll(cmb, CHUNK - s, axis=0), -jnp.inf))
            cm = jnp.where(fwd_lane, cmf, cmb)
            ml = jnp.where(lane_c < 16, pltpu.roll(gmb, LANES - 32, axis=1),
                           jnp.where(lane_c < 32, pltpu.roll(cm, LANES - 16, axis=1),
                                     jnp.where(lane_c < 48, bcum, 0.0)))
            ml_ref[c * CHUNK:(c + 1) * CHUNK, :] = ml


def _gate_prep(graw, gparams, want_gd, want_ml, tm=256):
    t = graw.shape[0]
    spec = pl.BlockSpec((tm, LANES), lambda i: (i, 0))
    n_out = int(want_gd) + int(want_ml)
    return pl.pallas_call(
        functools.partial(_gate_kernel, want_gd=want_gd, want_ml=want_ml),
        grid=(t // tm,),
        in_specs=[spec, pl.BlockSpec((8, LANES), lambda i: (0, 0))],
        out_specs=[spec] * n_out,
        out_shape=[jax.ShapeDtypeStruct((t, LANES), F32)] * n_out,
        compiler_params=_params(("arbitrary",)),
    )(graw, gparams)


def _split3(a):
    h = a.astype(BF16)
    r = a - h.astype(F32)
    m = r.astype(BF16)
    return h, m, (r - m.astype(F32)).astype(BF16)


def _lane_picks(x, lanes):
    li = lax.broadcasted_iota(jnp.int32, (LANES, LANES), 0)
    ci = lax.broadcasted_iota(jnp.int32, (LANES, LANES), 1)
    want = jnp.full((LANES, LANES), -1, jnp.int32)
    for j, lane in enumerate(lanes):
        want = jnp.where(ci == j, lane, want)
    sel = (li == want).astype(BF16)
    h, m, lo = _split3(x)
    cols = _dot(h, sel) + (_dot(m, sel) + _dot(lo, sel))
    return [jnp.broadcast_to(cols[:, j:j + 1], x.shape) for j in range(len(lanes))]


def _dir_masks(direction):
    ri = lax.broadcasted_iota(jnp.int32, (CHUNK, CHUNK), 0)
    ci = lax.broadcasted_iota(jnp.int32, (CHUNK, CHUNK), 1)
    if direction == 0:
        return ri >= ci, ri > ci
    return ri <= ci, ri < ci


def _gdn_kernel(qc_ref, kc_ref, vc_ref, ql_ref, kl_ref, vl_ref, z_ref, cwq_ref, cwk_ref, cwv_ref,
                gdc_ref, gdl_ref, ng_ref, y_ref,
                xpad, qs, ks, vs, beta_t, cg_t, wq_r, u_r, kd_r, qk_r, dc_r, out_s):
    lc = qc_ref.shape[1]
    ll = ql_ref.shape[1]
    lt = lc + ll
    n_c, n_l = lc // CHUNK, ll // CHUNK
    n_t = n_c + n_l
    rb = 256

    def l2n(x):
        return x * lax.rsqrt(jnp.sum(x * x, axis=-1, keepdims=True) + EPS)

    def prep(src_ref, cw_ref, dst, off, ls, kind):
        xpad[0:8, :] = jnp.zeros((8, LANES), F32)
        xpad[8:8 + ls, :] = src_ref[0].astype(F32)
        xpad[8 + ls:16 + ls, :] = jnp.zeros((8, LANES), F32)
        step = min(rb, ls)
        for r0 in range(0, ls, step):
            acc = jnp.zeros((step, LANES), F32)
            for t in range(GDN_CONV):
                s0 = r0 + 8 - GDN_CONV // 2 + t
                acc = acc + cw_ref[t:t + 1, :] * xpad[s0:s0 + step, :]
            y = acc * _sigmoid(acc)
            if kind == "q":
                y = l2n(y) * (GDN_DK ** -0.5)
            elif kind == "k":
                y = l2n(y)
            dst[off + r0:off + r0 + step, :] = y

    prep(qc_ref, cwq_ref, qs, 0, lc, "q")
    prep(kc_ref, cwk_ref, ks, 0, lc, "k")
    prep(vc_ref, cwv_ref, vs, 0, lc, "v")
    prep(ql_ref, cwq_ref, qs, lc, ll, "q")
    prep(kl_ref, cwk_ref, ks, lc, ll, "k")
    prep(vl_ref, cwv_ref, vs, lc, ll, "v")

    head = pl.program_id(1)

    def build_tables(src_ref, off, ls):
        step = min(rb, ls)
        for r0 in range(0, ls, step):
            picked = _lane_picks(src_ref[0, r0:r0 + step, :], [8 * d + head for d in range(2)]
                                 + [16 + 8 * d + head for d in range(2)])
            for d in range(2):
                beta_t[d, off + r0:off + r0 + step, :] = picked[d]
                cg_t[d, off + r0:off + r0 + step, :] = picked[2 + d]

    build_tables(gdc_ref, 0, lc)
    build_tables(gdl_ref, lc, ll)

    def bwd_chunk(t):
        return jnp.where(t < n_c, n_c - 1 - t, n_t + n_c - 1 - t)

    row_p = lax.broadcasted_iota(jnp.int32, (CHUNK, LANES), 0)
    lane_p = lax.broadcasted_iota(jnp.int32, (CHUNK, LANES), 1)
    fwd_p = lane_p < CHUNK
    col_p = jnp.where(fwd_p, lane_p, lane_p - CHUNK)
    signed = jnp.where(fwd_p, row_p - col_p, col_p - row_p)
    incl_p = signed >= 0
    strict_p = signed > 0
    eye_p = (row_p == col_p).astype(F32)
    keep_f = fwd_p.astype(BF16)
    keep_b = (1.0 - fwd_p.astype(F32)).astype(BF16)

    def block_diag(top, bottom):
        zero = jnp.zeros(top.shape, top.dtype)
        return jnp.concatenate([jnp.concatenate([top, zero], axis=1), jnp.concatenate([zero, bottom], axis=1)], axis=0)

    def pair_diag(xp):
        return jnp.concatenate([xp * keep_f, xp * keep_b], axis=0)

    def pair_times3(a, xp):
        ah = a.astype(BF16)
        al = (a - ah.astype(F32)).astype(BF16)
        xh = xp.astype(BF16)
        xl = (xp - xh.astype(F32)).astype(BF16)
        dh = pair_diag(xh)
        return _dot(ah, dh) + (_dot(ah, pair_diag(xl)) + _dot(al, dh))

    def prep_chain(t):
        tc = jnp.minimum(t, n_t - 1)
        q, k, v, beta, cgc, kb, ecg = [], [], [], [], [], [], []
        for d in range(2):
            c = tc if d == 0 else bwd_chunk(tc)
            rows = pl.ds(pl.multiple_of(c * CHUNK, CHUNK), CHUNK)
            q.append(qs[rows, :])
            k.append(ks[rows, :])
            v.append(vs[rows, :])
            beta.append(beta_t[d, rows, :])
            cgc.append(cg_t[d, rows, :])
        kdiag = block_diag(k[0].astype(BF16), k[1].astype(BF16))
        kk = _dot_nt(jnp.concatenate([k[0], k[1]], axis=1).astype(BF16), kdiag)
        qk = _dot_nt(jnp.concatenate([q[0], q[1]], axis=1).astype(BF16), kdiag)
        yield
        beta_p = jnp.where(fwd_p, beta[0], beta[1])
        cg_p = jnp.where(fwd_p, cgc[0], cgc[1])
        cgr_p = jnp.transpose(jnp.concatenate([cgc[0], cgc[1]], axis=0))[0:CHUNK, :]
        decay = jnp.exp(jnp.where(incl_p, cg_p - cgr_p, -jnp.inf))
        qkd = qk * decay
        for d in range(2):
            slot = (t % RING) * 2 + d
            last = CHUNK - 1 if d == 0 else 0
            cg_last = cgc[d][last:last + 1, :]
            ecg.append(jnp.exp(cgc[d]))
            kb.append(k[d] * beta[d])
            qk_r[slot] = qkd[:, d * CHUNK:(d + 1) * CHUNK]
            wq_r[slot, CHUNK:2 * CHUNK, :] = q[d] * ecg[d]
            kd_r[slot] = k[d] * jnp.exp(cg_last - cgc[d])
            dc_r[slot] = jnp.broadcast_to(jnp.exp(cg_last), (8, LANES))
        x = jnp.where(strict_p, -(beta_p * kk) * decay, 0.0)
        tinv = eye_p + x
        x = pair_times3(x, x)
        yield
        for _ in range(4):
            both = pair_times3(jnp.concatenate([tinv, x], axis=0), x)
            tinv, x = tinv + both[0:CHUNK, :], both[CHUNK:2 * CHUNK, :]
            yield
        tinv = (tinv + pair_times3(tinv, x)).astype(BF16)
        yield
        w = _dot(tinv, block_diag((kb[0] * ecg[0]).astype(BF16), (kb[1] * ecg[1]).astype(BF16)))
        u = _dot(tinv, block_diag((v[0] * beta[0]).astype(BF16), (v[1] * beta[1]).astype(BF16)))
        for d in range(2):
            slot = (t % RING) * 2 + d
            wq_r[slot, 0:CHUNK, :] = w[:, d * LANES:(d + 1) * LANES]
            u_r[slot] = u[:, d * LANES:(d + 1) * LANES]

    out_s[...] = jnp.zeros(out_s.shape, F32)

    def scan_chain(d, t0, steps, s, with_out, result):
        for j in range(steps):
            t = t0 + j
            slot = (t % RING) * 2 + d
            ws = _dot(wq_r[slot], s)
            yield
            v_new = u_r[slot] - ws[0:CHUNK, :]
            if with_out:
                c = t if d == 0 else bwd_chunk(t)
                o = ws[CHUNK:2 * CHUNK, :] + _dot(qk_r[slot], v_new)
                l0 = pl.multiple_of((c - n_c) * CHUNK, CHUNK)
                out_s[pl.ds(l0, CHUNK), :] += o
            s = s * dc_r[slot][0:1, :] + _dot_tn(kd_r[slot], v_new)
            yield
        result[d] = s

    def lockstep(chains):
        chains = list(chains)
        while chains:
            alive = []
            for ch in chains:
                try:
                    next(ch)
                    alive.append(ch)
                except StopIteration:
                    pass
            chains = alive

    def group_body(i, carry, t_base, steps, ahead, with_out):
        t0 = t_base + steps * i
        result = [None, None]
        lockstep([scan_chain(d, t0, steps, carry[d], with_out, result) for d in range(2)]
                 + [prep_chain(t0 + steps + j) for j in range(ahead)])
        return result[0], result[1]

    def run_groups(carry, t_base, n_steps, steps, ahead_last, with_out):
        n_groups = n_steps // steps
        carry = lax.fori_loop(0, n_groups - 1, functools.partial(
            group_body, t_base=t_base, steps=steps, ahead=steps, with_out=with_out), carry)
        return group_body(n_groups - 1, carry, t_base, steps, ahead_last, with_out)

    lockstep([prep_chain(j) for j in range(GDN_CTX_STEPS)])
    s0 = jnp.zeros((GDN_DK, HEAD_V), F32)
    carry = run_groups((s0, s0), 0, n_c, GDN_CTX_STEPS, GDN_STEPS, False)
    run_groups(carry, n_c, n_l, GDN_STEPS, 0, True)


    def out_body(i, carry):
        r0 = pl.multiple_of(i * rb, rb)
        o = out_s[pl.ds(r0, rb), :]
        z = z_ref[0, pl.ds(r0, rb), :].astype(F32)
        y = o * lax.rsqrt(jnp.mean(o * o, axis=-1, keepdims=True) + EPS) * ng_ref[...]
        y_ref[0, pl.ds(r0, rb), :] = (y * (z * _sigmoid(z))).astype(y_ref.dtype)
        return carry

    lax.fori_loop(0, ll // rb, out_body, 0)


def _gdn(proj_c, proj_l, conv_w8, gd_c, gd_l, norm_g, b0, nb):
    lc = proj_c.shape[1]
    ll = proj_l.shape[1]
    lt = lc + ll
    qb, kb_, vb, zb = (COL_GDN_QKV // LANES, COL_GDN_QKV // LANES + HEADS, COL_GDN_QKV // LANES + 2 * HEADS,
                       COL_GDN_Z // LANES)

    def seq_spec(l, col0):
        return pl.BlockSpec((1, l, LANES), lambda i, h: (i + b0, 0, col0 + h))

    def cw_spec(col0):
        return pl.BlockSpec((8, LANES), lambda i, h: (0, col0 + h))

    return pl.pallas_call(
        _gdn_kernel,
        grid=(nb, HEADS),
        in_specs=[seq_spec(lc, qb), seq_spec(lc, kb_), seq_spec(lc, vb),
                  seq_spec(ll, qb), seq_spec(ll, kb_), seq_spec(ll, vb), seq_spec(ll, zb),
                  cw_spec(0), cw_spec(HEADS), cw_spec(2 * HEADS),
                  pl.BlockSpec((1, lc, LANES), lambda i, h: (i + b0, 0, 0)),
                  pl.BlockSpec((1, ll, LANES), lambda i, h: (i + b0, 0, 0)),
                  pl.BlockSpec((1, LANES), lambda i, h: (0, 0))],
        out_specs=pl.BlockSpec((1, ll, LANES), lambda i, h: (i, 0, h)),
        out_shape=jax.ShapeDtypeStruct((nb, ll, HEADS * HEAD_V), BF16),
        scratch_shapes=[pltpu.VMEM((max(lc, ll) + 16, LANES), F32),
                        pltpu.VMEM((lt, LANES), F32), pltpu.VMEM((lt, LANES), F32), pltpu.VMEM((lt, LANES), F32),
                        pltpu.VMEM((2, lt, LANES), F32), pltpu.VMEM((2, lt, LANES), F32),
                        pltpu.VMEM((2 * RING, 2 * CHUNK, LANES), F32),
                        pltpu.VMEM((2 * RING, CHUNK, LANES), F32), pltpu.VMEM((2 * RING, CHUNK, LANES), F32),
                        pltpu.VMEM((2 * RING, CHUNK, CHUNK), F32),
                        pltpu.VMEM((2 * RING, 8, LANES), F32),
                        pltpu.VMEM((ll, LANES), F32)],
        compiler_params=_params(("arbitrary", "arbitrary")),
    )(proj_c, proj_c, proj_c, proj_l, proj_l, proj_l, proj_l, conv_w8, conv_w8, conv_w8,
      gd_c, gd_l, norm_g.reshape(1, LANES))


def _mlstm_kernel(qc_ref, kc_ref, vc_ref, ql_ref, kl_ref, vl_ref, mlc_ref, mll_ref, h_ref, out_s, tabs):
    lc = qc_ref.shape[1]
    ll = ql_ref.shape[1]
    n_c, n_l = lc // CHUNK, ll // CHUNK
    n_t = n_c + n_l
    lt = lc + ll
    pair = pl.program_id(1)
    lane = lax.broadcasted_iota(jnp.int32, (CHUNK, LANES), 1)
    ones_v = jnp.ones((CHUNK, HEAD_V), BF16)
    chains = [(hh, d) for hh in range(2) for d in range(2)]
    hmask = [((lane // ML_DK) == hh).astype(F32) for hh in range(2)]
    incl = [_dir_masks(d)[0] for d in range(2)]

    def build_tables(src_ref, off, ls):
        step = min(256, ls)
        for r0 in range(0, ls, step):
            lanes = [16 * j + 8 * d + 2 * pair + hh for hh, d in chains for j in range(3)]
            for g, tab in enumerate(_lane_picks(src_ref[0, r0:r0 + step, :], lanes)):
                tabs[g, off + r0:off + r0 + step, :] = tab

    build_tables(mlc_ref, 0, lc)
    build_tables(mll_ref, lc, ll)

    def wide(a):
        return jnp.concatenate([a, a], axis=1)

    def chain(hh, d, c, get_state, is_ctx, result):
        last = CHUNK - 1 if d == 0 else 0
        if is_ctx:
            rows = pl.ds(pl.multiple_of(c * CHUNK, CHUNK), CHUNK)
            q_ref, k_ref, v_ref = qc_ref, kc_ref, vc_ref
        else:
            rows = pl.ds(pl.multiple_of((c - n_c) * CHUNK, CHUNK), CHUNK)
            q_ref, k_ref, v_ref = ql_ref, kl_ref, vl_ref
        q = (q_ref[0, rows, :].astype(F32) * hmask[hh]).astype(BF16)
        k = k_ref[0, rows, :].astype(F32) * (hmask[hh] * (ML_DK ** -0.5))
        v = jnp.concatenate([v_ref[0, rows, hh * HEAD_V:(hh + 1) * HEAD_V], ones_v], axis=1)
        n = chains.index((hh, d))
        trows = pl.ds(pl.multiple_of(c * CHUNK, CHUNK), CHUNK)
        gmb = tabs[3 * n, trows, :]
        gmb_t = jnp.transpose(gmb)[0:CHUNK, :]
        cm = tabs[3 * n + 1, trows, :]
        bc = tabs[3 * n + 2, trows, :]
        qk = _dot_nt(q, k.astype(BF16))
        cm_last = cm[last:last + 1, :]
        b_last = bc[last:last + 1, :]
        wk = (k * jnp.exp(gmb - cm_last)).astype(BF16)
        c_loc = _dot_tn(wk, v)
        yield
        while get_state() is None:
            yield
        cs, ms = get_state()
        mm = jnp.maximum(ms, cm)
        p = jnp.where(incl[d], jnp.exp(gmb_t - mm[:, 0:CHUNK]), 0.0) * qk
        inter = _dot(q, cs.astype(BF16))
        intra = _dot(p.astype(BF16), v)
        yield
        if not is_ctx:
            nd = wide(jnp.exp(ms - mm)) * inter + intra
            hv = nd[:, 0:HEAD_V] / jnp.maximum(jnp.abs(nd[:, HEAD_V:2 * HEAD_V]), jnp.exp(-(bc + mm)))
            l0 = pl.multiple_of((c - n_c) * CHUNK, CHUNK)
            out_s[pl.ds(l0, CHUNK), hh * HEAD_V:(hh + 1) * HEAD_V] += hv
        mx = jnp.maximum(ms, cm_last)
        result[hh, d] = (wide(jnp.exp(ms - mx)) * cs + wide(jnp.exp(cm_last - mx)) * c_loc, b_last + mx)

    out_s[...] = jnp.zeros(out_s.shape, F32)

    def run(chains_iter):
        live = list(chains_iter)
        while live:
            alive = []
            for ch in live:
                try:
                    next(ch)
                    alive.append(ch)
                except StopIteration:
                    pass
            live = alive

    def body(i, carry, is_ctx):
        results = [{} for _ in range(ML_STEPS)]
        gens = []
        for j in range(ML_STEPS):
            t = ML_STEPS * i + j
            for n, (hh, d) in enumerate(chains):
                if is_ctx:
                    c = t if d == 0 else n_c - 1 - t
                else:
                    c = n_c + t if d == 0 else n_t - 1 - t
                if j == 0:
                    get_state = functools.partial(lambda n: carry[n], n)
                else:
                    get_state = functools.partial(results[j - 1].get, (hh, d))
                gens.append(chain(hh, d, c, get_state, is_ctx, results[j]))
        run(gens)
        return tuple(results[-1][hd] for hd in chains)

    st0 = (jnp.zeros((LANES, 2 * HEAD_V), F32), jnp.zeros((1, LANES), F32))
    carry = lax.fori_loop(0, n_c // ML_STEPS, functools.partial(body, is_ctx=True), (st0,) * 4)
    lax.fori_loop(0, n_l // ML_STEPS, functools.partial(body, is_ctx=False), carry)
    h_ref[0] = out_s[...].astype(h_ref.dtype)


def _mlstm(proj_c, q_l, k_l, v_l, ml_c, ml_l, b0):
    lc = proj_c.shape[1]
    nb, ll, _ = q_l.shape
    lt = lc + ll
    qb, kb_, vb = COL_ML_Q // LANES, COL_ML_K // LANES, COL_ML_V // (2 * HEAD_V)
    return pl.pallas_call(
        _mlstm_kernel,
        grid=(nb, HEADS // 2),
        in_specs=[pl.BlockSpec((1, lc, LANES), lambda i, p: (i + b0, 0, qb + p)),
                  pl.BlockSpec((1, lc, LANES), lambda i, p: (i + b0, 0, kb_ + p)),
                  pl.BlockSpec((1, lc, 2 * HEAD_V), lambda i, p: (i + b0, 0, vb + p)),
                  pl.BlockSpec((1, ll, LANES), lambda i, p: (i, 0, p)),
                  pl.BlockSpec((1, ll, LANES), lambda i, p: (i, 0, p)),
                  pl.BlockSpec((1, ll, 2 * HEAD_V), lambda i, p: (i, 0, p)),
                  pl.BlockSpec((1, lc, LANES), lambda i, p: (i + b0, 0, 0)),
                  pl.BlockSpec((1, ll, LANES), lambda i, p: (i + b0, 0, 0))],
        out_specs=pl.BlockSpec((1, ll, 2 * HEAD_V), lambda i, p: (i, 0, p)),
        out_shape=jax.ShapeDtypeStruct((nb, ll, HEADS * HEAD_V), BF16),
        scratch_shapes=[pltpu.VMEM((ll, 2 * HEAD_V), F32), pltpu.VMEM((12, lt, LANES), F32)],
        compiler_params=_params(("arbitrary", "arbitrary")),
    )(proj_c, proj_c, proj_c, q_l, k_l, v_l, ml_c, ml_l)


def _merge_kernel(yg_ref, hm_ref, o_ref, gg_ref, gm_ref, x_ref, mod_ref, mlg_ref, n2_ref,
                  wbg_ref, wbm_ref, wo_ref, wrh_ref, wrl_ref, x1_ref, h2_ref, sc_ref):
    d = x_ref.shape[1]
    o = o_ref[...].astype(F32)
    ym = _sigmoid(o) * hm_ref[...].astype(F32)
    segs = []
    for h in range(HEADS):
        seg = ym[:, h * HEAD_V:(h + 1) * HEAD_V]
        segs.append(seg * lax.rsqrt(jnp.mean(seg * seg, axis=-1, keepdims=True) + EPS))
    ymn = jnp.concatenate(segs, axis=1) * mlg_ref[...]
    y_gdn = _dot(yg_ref[...], wbg_ref[...])
    y_ml = _dot(ymn.astype(BF16), wbm_ref[...])
    mixed = _sigmoid(gg_ref[...].astype(F32)) * y_gdn + _sigmoid(gm_ref[...].astype(F32)) * y_ml
    y = _dot(mixed.astype(BF16), wo_ref[...])
    x1 = x_ref[...] + mod_ref[0, :, 2 * d:3 * d] * y
    x1_ref[...] = x1
    hn = x1 * lax.rsqrt(jnp.mean(x1 * x1, axis=-1, keepdims=True) + EPS) * n2_ref[...]
    h2 = hn * (1.0 + mod_ref[0, :, 4 * d:5 * d]) + mod_ref[0, :, 3 * d:4 * d]
    h2_hi = h2.astype(BF16)
    h2_ref[...] = h2_hi
    h2_lo = (h2 - h2_hi.astype(F32)).astype(BF16)
    logits = _dot_nt(wrh_ref[...], h2_hi) + (_dot_nt(wrl_ref[...], h2_hi) + _dot_nt(wrh_ref[...], h2_lo))
    sc_ref[...] = _sigmoid(logits)


def _merge(y_gdn, h_ml, proj_l2d, x2d, mod3, rows_per_mod, ml_norm_g, norm2_g, wbg, wbm, wo, wr_hi, wr_lo, tok0,
           tm=512):
    t, d = y_gdn.shape
    e = wr_hi.shape[0]
    off = tok0 // tm
    row = lambda i: (i, 0)
    const = lambda i: (0, 0)
    return pl.pallas_call(
        _merge_kernel,
        grid=(t // tm,),
        in_specs=[pl.BlockSpec((tm, d), row), pl.BlockSpec((tm, d), row),
                  pl.BlockSpec((tm, d), lambda i: (i + off, COL_ML_O // d)),
                  pl.BlockSpec((tm, d), lambda i: (i + off, COL_MG_GDN // d)),
                  pl.BlockSpec((tm, d), lambda i: (i + off, COL_MG_ML // d)),
                  pl.BlockSpec((tm, d), lambda i: (i + off, 0)),
                  pl.BlockSpec((1, 1, mod3.shape[2]), lambda i: (((i + off) * tm) // rows_per_mod, 0, 0)),
                  pl.BlockSpec((1, d), const), pl.BlockSpec((1, d), const),
                  pl.BlockSpec((d, d), const), pl.BlockSpec((d, d), const), pl.BlockSpec((d, d), const),
                  pl.BlockSpec((e, d), const), pl.BlockSpec((e, d), const)],
        out_specs=[pl.BlockSpec((tm, d), row), pl.BlockSpec((tm, d), lambda i: (i + off, 0)),
                   pl.BlockSpec((e, tm), lambda i: (0, i))],
        out_shape=[jax.ShapeDtypeStruct((t, d), F32), jax.ShapeDtypeStruct((x2d.shape[0], d), BF16),
                   jax.ShapeDtypeStruct((e, t), F32)],
        compiler_params=_params(("arbitrary",)),
    )(y_gdn, h_ml, proj_l2d, proj_l2d, proj_l2d, x2d, mod3, ml_norm_g.reshape(1, d), norm2_g.reshape(1, d),
      wbg, wbm, wo, wr_hi, wr_lo)


def _expert_kernel(be_ref, nu_ref, nx_ref, x_ref, wgu_hbm, wd_hbm, y_ref, land_gu, land_d, wgu_s, wd_s, sems):
    i = pl.program_id(0)
    de = wd_hbm.shape[1]
    used = i < nu_ref[0]
    first = jnp.logical_or(i == 0, be_ref[i] != be_ref[jnp.maximum(i - 1, 0)])

    def weight_copies(ex):
        return (pltpu.make_async_copy(wgu_hbm.at[ex], land_gu, sems.at[0]),
                pltpu.make_async_copy(wd_hbm.at[ex], land_d, sems.at[1]))

    @pl.when(i == 0)
    def _():
        for cp in weight_copies(be_ref[0]):
            cp.start()

    @pl.when(jnp.logical_and(first, used))
    def _():
        for cp in weight_copies(be_ref[i]):
            cp.wait()
        wgu_s[...] = land_gu[...].astype(BF16)
        wd_s[...] = land_d[...].astype(BF16)

        @pl.when(nx_ref[i] >= 0)
        def _():
            for cp in weight_copies(nx_ref[i]):
                cp.start()

    @pl.when(used)
    def _():
        gu = _dot(x_ref[...], wgu_s[...])
        g = gu[:, 0:de]
        act = (g * _sigmoid(g)) * gu[:, de:2 * de]
        y_ref[...] = _dot(act.astype(BF16), wd_s[...]).astype(y_ref.dtype)

    @pl.when(jnp.logical_not(used))
    def _():
        y_ref[...] = jnp.zeros(y_ref.shape, y_ref.dtype)


def _experts(xb, blk_expert, n_used, next_expert, w_gu, w_down):
    n_slots, d = xb.shape
    n_blocks = n_slots // EXPERT_ROWS
    e, _, de2 = w_gu.shape
    de = de2 // 2
    return pl.pallas_call(
        _expert_kernel,
        grid_spec=pltpu.PrefetchScalarGridSpec(
            num_scalar_prefetch=3,
            grid=(n_blocks,),
            in_specs=[pl.BlockSpec((EXPERT_ROWS, d), lambda i, be, nu, nx: (i, 0)),
                      pl.BlockSpec(memory_space=pl.ANY), pl.BlockSpec(memory_space=pl.ANY)],
            out_specs=pl.BlockSpec((EXPERT_ROWS, d), lambda i, be, nu, nx: (i, 0)),
            scratch_shapes=[pltpu.VMEM((d, de2), w_gu.dtype), pltpu.VMEM((de, d), w_down.dtype),
                            pltpu.VMEM((d, de2), BF16), pltpu.VMEM((de, d), BF16),
                            pltpu.SemaphoreType.DMA((2,))]),
        out_shape=jax.ShapeDtypeStruct((n_slots, d), BF16),
        compiler_params=_params(("arbitrary",)),
    )(blk_expert, n_used, next_expert, xb, w_gu, w_down)


def _final_kernel(x1_ref, h2_ref, yg_ref, wt_ref, mod_ref, wsg_ref, wsd_ref, fg_ref, o_ref):
    d = x1_ref.shape[1]
    ds_ = wsd_ref.shape[0]
    wt = wt_ref[...]
    routed = jnp.zeros(x1_ref.shape, F32)
    for k in range(TOP_K):
        routed = routed + wt[:, k:k + 1] * yg_ref[k].astype(F32)
    gu = _dot(h2_ref[...], wsg_ref[...])
    g = gu[:, 0:ds_]
    sh = _dot(((g * _sigmoid(g)) * gu[:, ds_:2 * ds_]).astype(BF16), wsd_ref[...])
    x2 = x1_ref[...] + mod_ref[0, :, 5 * d:6 * d] * (routed + sh)
    o_ref[...] = x2 * lax.rsqrt(jnp.mean(x2 * x2, axis=-1, keepdims=True) + EPS) * fg_ref[...]


def _final(x1, h2, yg, wts, mod3, rows_per_mod, w_sh_gu, w_sh_down, final_g, tok0, tm=256):
    tp, d = x1.shape
    off = tok0 // tm
    row = lambda i: (i, 0)
    const = lambda i: (0, 0)
    return pl.pallas_call(
        _final_kernel,
        grid=(tp // tm,),
        in_specs=[pl.BlockSpec((tm, d), row), pl.BlockSpec((tm, d), lambda i: (i + off, 0)),
                  pl.BlockSpec((TOP_K, tm, d), lambda i: (0, i, 0)), pl.BlockSpec((tm, TOP_K), row),
                  pl.BlockSpec((1, 1, mod3.shape[2]), lambda i: (((i + off) * tm) // rows_per_mod, 0, 0)),
                  pl.BlockSpec(w_sh_gu.shape, const), pl.BlockSpec(w_sh_down.shape, const),
                  pl.BlockSpec((1, d), const)],
        out_specs=pl.BlockSpec((tm, d), row),
        out_shape=jax.ShapeDtypeStruct((tp, d), F32),
        compiler_params=_params(("arbitrary",)),
    )(x1, h2, yg, wts, mod3, w_sh_gu, w_sh_down, final_g.reshape(1, d))


def _route_kernel(sc_ref, bias_ref, tri_ref, idx_ref, wt_ref, rk_ref, cnt_ref, base_s):
    @pl.when(pl.program_id(0) == 0)
    def _():
        base_s[...] = jnp.zeros(base_s.shape, F32)

    scores = sc_ref[...]
    e, tn = scores.shape
    gsz = e // N_GROUPS
    sel3 = (scores + bias_ref[...]).reshape(N_GROUPS, gsz, tn)
    m1 = jnp.max(sel3, axis=1)
    is_max = sel3 == m1[:, None, :]
    n_max = jnp.sum(is_max.astype(F32), axis=1)
    m2 = jnp.max(jnp.where(is_max, -jnp.inf, sel3), axis=1)
    grp = m1 + jnp.where(n_max >= 2.0, m1, m2)
    gi = lax.broadcasted_iota(jnp.int32, (N_GROUPS, tn), 0)
    ahead = jnp.zeros((N_GROUPS, tn), F32)
    for g in range(N_GROUPS):
        row = grp[g:g + 1, :]
        ahead = ahead + jnp.logical_or(row > grp, jnp.logical_and(row == grp, g < gi)).astype(F32)
    ahead3 = jnp.broadcast_to(ahead[:, None, :], (N_GROUPS, gsz, tn))
    selm = jnp.where(ahead3 < float(TOPK_GROUPS), sel3, -jnp.inf).reshape(e, tn)
    ri = lax.broadcasted_iota(jnp.int32, (e, tn), 0).astype(F32)
    member = jnp.zeros((e, tn), F32)
    idxs, ws = [], []
    for _ in range(TOP_K):
        m = jnp.max(selm, axis=0, keepdims=True)
        idx = jnp.min(jnp.where(selm == m, ri, float(e)), axis=0, keepdims=True)
        hit = ri == idx
        ws.append(jnp.sum(jnp.where(hit, scores, 0.0), axis=0, keepdims=True))
        idxs.append(idx)
        selm = jnp.where(hit, -jnp.inf, selm)
        member = jnp.where(hit, 1.0, member)
    w = jnp.concatenate(ws, axis=0)
    wt_ref[...] = w / jnp.sum(w, axis=0, keepdims=True) * ROUTED_SCALE
    idx_ref[...] = jnp.concatenate(idxs, axis=0).astype(jnp.int32)
    cum = _dot(member.astype(BF16), tri_ref[...]) + base_s[...]
    rk_ref[...] = jnp.concatenate(
        [jnp.sum(jnp.where(ri == idx, cum, 0.0), axis=0, keepdims=True) for idx in idxs], axis=0).astype(jnp.int32)
    total = base_s[...] + jnp.sum(member, axis=1, keepdims=True)
    base_s[...] = total
    cnt_ref[...] = total


def _route(scores_t, router_bias):
    e, t = scores_t.shape
    tn = LANES
    bias = jnp.broadcast_to(router_bias.astype(F32)[:, None], (e, tn))
    tri = (jnp.arange(tn)[:, None] < jnp.arange(tn)[None, :]).astype(BF16)
    tok = pl.BlockSpec((TOP_K, tn), lambda i: (0, i))
    const = lambda i: (0, 0)
    return pl.pallas_call(
        _route_kernel,
        grid=(t // tn,),
        in_specs=[pl.BlockSpec((e, tn), lambda i: (0, i)), pl.BlockSpec((e, tn), const),
                  pl.BlockSpec((tn, tn), const)],
        out_specs=[tok, tok, tok, pl.BlockSpec((e, tn), const)],
        out_shape=[jax.ShapeDtypeStruct((TOP_K, t), jnp.int32), jax.ShapeDtypeStruct((TOP_K, t), F32),
                   jax.ShapeDtypeStruct((TOP_K, t), jnp.int32), jax.ShapeDtypeStruct((e, tn), F32)],
        scratch_shapes=[pltpu.VMEM((e, tn), F32)],
        compiler_params=_params(("arbitrary",)),
    )(scores_t, bias, tri)


def _slot_kernel(idx_ref, rk_ref, ps_ref, pos_ref):
    e, tn = ps_ref.shape
    ri = lax.broadcasted_iota(jnp.int32, (e, tn), 0)
    ps = ps_ref[...]
    rows = [jnp.sum(jnp.where(ri == idx_ref[k:k + 1, :], ps, 0.0), axis=0, keepdims=True) for k in range(TOP_K)]
    pos_ref[...] = rk_ref[...] + jnp.concatenate(rows, axis=0).astype(jnp.int32)


def _slots(idx, rank, pstart):
    k, t = idx.shape
    e = pstart.shape[0]
    tn = LANES
    tok = pl.BlockSpec((k, tn), lambda i: (0, i))
    return pl.pallas_call(
        _slot_kernel,
        grid=(t // tn,),
        in_specs=[tok, tok, pl.BlockSpec((e, tn), lambda i: (0, 0))],
        out_specs=tok,
        out_shape=jax.ShapeDtypeStruct((k, t), jnp.int32),
        compiler_params=_params(("arbitrary",)),
    )(idx, rank, jnp.broadcast_to(pstart.astype(F32)[:, None], (e, tn)))


def _sc_scatter_rows(vals, idx, n_rows, window=LANES):
    n, width = vals.shape
    mesh = plsc.VectorSubcoreMesh(core_axis_name="core", subcore_axis_name="subcore")

    @pl.kernel(out_type=jax.ShapeDtypeStruct((n_rows, width), vals.dtype), mesh=mesh, scratch_types=[])
    def scatter(v_hbm, i_hbm, o_hbm):
        def body(v_vmem, i_vmem):
            pltpu.sync_copy(v_vmem, o_hbm.at[i_vmem.at[0]])

        pltpu.emit_pipeline(
            body,
            grid=(n // window,),
            in_specs=[pl.BlockSpec((window, width), lambda i: (i, 0)),
                      pl.BlockSpec((1, window), lambda i: (0, i))],
            out_specs=[],
            core_axis_name=("core", "subcore"),
            dimension_semantics=(pltpu.PARALLEL,),
        )(v_hbm, i_hbm)

    return scatter(vals, idx.reshape(1, n))


def _block_table(counts, n_blocks):
    padded = (counts + EXPERT_ROWS - 1) // EXPERT_ROWS * EXPERT_ROWS
    pend = jnp.cumsum(padded)
    first_slot = jnp.arange(n_blocks, dtype=jnp.int32) * EXPERT_ROWS
    blk_expert = jnp.minimum(jnp.sum((pend[None, :] <= first_slot[:, None]).astype(jnp.int32), axis=1),
                             N_EXPERTS - 1)
    e_ids = jnp.arange(N_EXPERTS, dtype=jnp.int32)
    later = lax.cummin(jnp.where(counts > 0, e_ids, N_EXPERTS)[::-1])[::-1]
    next_tab = jnp.concatenate([later[1:], jnp.full((1,), N_EXPERTS, jnp.int32)])
    next_tab = jnp.where(next_tab < N_EXPERTS, next_tab, -1)
    return (pend - padded, blk_expert, (pend[-1] // EXPERT_ROWS).astype(jnp.int32).reshape(1),
            next_tab[blk_expert].astype(jnp.int32))


def _col_major(t):
    b, l, f = t.shape
    rows = l // GRID_W
    return t.reshape(b, rows, GRID_W, f).transpose(0, 2, 1, 3).reshape(b, l, f)


def _row_major(t):
    b, l, f = t.shape
    rows = l // GRID_W
    return t.reshape(b, GRID_W, rows, f).transpose(0, 2, 1, 3).reshape(b, l, f)


def kernel(x, c, ctx, c_ctx, w_ada, b_ada, norm1_g, norm2_g, w_in, gdn_conv_w, gdn_a_log, gdn_dt_bias, gdn_norm_g,
           ml_i_bias, ml_f_bias, ml_norm_g, w_branch_gdn, w_branch_ml, w_out, w_router, router_bias, w_exp_gate_up,
           w_exp_down, w_sh_gate_up, w_sh_down, final_norm_g):
    b, l, d = x.shape
    lc = ctx.shape[1]
    t = b * l
    layer = 0

    w = w_in[layer]
    main_cols = [_ORIG[k] for k in ("gdn_qkv", "gdn_z", "ml_q", "ml_k", "ml_v", "ml_o", "mg_gdn", "mg_ml")]
    w_main = jnp.concatenate([w[:, a:e] for a, e in main_cols], axis=1).astype(BF16)
    w_gate = jnp.concatenate([w[:, _ORIG["gdn_gate"][0]:_ORIG["gdn_gate"][1]],
                              w[:, _ORIG["ml_gate"][0]:_ORIG["ml_gate"][1]],
                              jnp.zeros((d, LANES - 64), F32)], axis=1).astype(BF16)
    zeros16 = jnp.zeros((16,), F32)
    gp_add = jnp.concatenate([zeros16, gdn_dt_bias[layer].reshape(-1), ml_i_bias[layer].reshape(-1),
                              ml_f_bias[layer].reshape(-1), jnp.zeros((LANES - 64,), F32)])
    gp_mul = jnp.concatenate([zeros16, -jnp.exp(gdn_a_log[layer].astype(F32)).reshape(-1),
                              jnp.zeros((LANES - 32,), F32)])
    gparams = jnp.zeros((8, LANES), F32).at[0].set(gp_add).at[1].set(gp_mul)
    conv_w8 = jnp.zeros((8, gdn_conv_w.shape[2]), F32).at[0:GDN_CONV].set(gdn_conv_w[layer])
    wr = w_router[layer].T
    wr_hi = wr.astype(BF16)
    wr_lo = (wr - wr_hi.astype(F32)).astype(BF16)

    n_mod_rows = -(-(b + 1) // 8) * 8
    cc = jnp.zeros((n_mod_rows, d), F32).at[0:b].set(c).at[b].set(c_ctx)
    mod = _ada_mod(cc, w_ada[layer], b_ada[layer])
    mod3 = mod.reshape(n_mod_rows, 1, 6 * d)

    x2d = x.reshape(t, d)
    tm_l = min(1024, l)
    proj_l, gate_l = _project(x2d, mod3, lambda i: (i * tm_l) // l, norm1_g[layer], w_main, w_gate, tm_l)
    tm_c = min(1024, b * lc)
    proj_c, gate_c = _project(ctx.reshape(b * lc, d), mod3, lambda i: b, norm1_g[layer], w_main, w_gate, tm_c)
    proj_l3 = proj_l.reshape(b, l, N_MAIN)
    proj_c3 = proj_c.reshape(b, lc, N_MAIN)

    gate_l_cm = _col_major(gate_l.reshape(b, l, LANES)).reshape(t, LANES)
    gd_c, ml_c = _gate_prep(gate_c, gparams, True, True)
    gd_l, = _gate_prep(gate_l, gparams, True, False)
    ml_l, = _gate_prep(gate_l_cm, gparams, False, True)

    nb = b // BATCH_PARTS
    tp = nb * l
    n_assign = tp * TOP_K
    n_blocks = (n_assign + N_EXPERTS * (EXPERT_ROWS - 1)) // EXPERT_ROWS + 1
    n_slots = n_blocks * EXPERT_ROWS
    gd_c3, gd_l3 = gd_c.reshape(b, lc, LANES), gd_l.reshape(b, l, LANES)
    ml_c3, ml_l3 = ml_c.reshape(b, lc, LANES), ml_l.reshape(b, l, LANES)
    wbg, wbm, wo = (w_branch_gdn[layer].astype(BF16), w_branch_ml[layer].astype(BF16), w_out[layer].astype(BF16))
    w_sh_gu, w_sh_dn = w_sh_gate_up[layer].astype(BF16), w_sh_down[layer].astype(BF16)
    outs = []
    for part in range(BATCH_PARTS):
        b0, tok0 = part * nb, part * tp
        y_gdn = _gdn(proj_c3, proj_l3, conv_w8, gd_c3, gd_l3, gdn_norm_g[layer], b0, nb)
        q_cm = _col_major(proj_l3[b0:b0 + nb, :, COL_ML_Q:COL_ML_Q + HEADS * ML_DK])
        k_cm = _col_major(proj_l3[b0:b0 + nb, :, COL_ML_K:COL_ML_K + HEADS * ML_DK])
        v_cm = _col_major(proj_l3[b0:b0 + nb, :, COL_ML_V:COL_ML_V + HEADS * HEAD_V])
        h_ml = _row_major(_mlstm(proj_c3, q_cm, k_cm, v_cm, ml_c3, ml_l3, b0))
        x1, h2, scores_t = _merge(y_gdn.reshape(tp, d), h_ml.reshape(tp, d), proj_l, x2d, mod3, l, ml_norm_g[layer],
                                norm2_g[layer], wbg, wbm, wo, wr_hi, wr_lo, tok0, tm=min(512, l))
        idx, wts, rank, cnt = _route(scores_t, router_bias[layer])
        counts = cnt[:, 0].astype(jnp.int32)
        pstart, blk_expert, n_used, next_expert = _block_table(counts, n_blocks)
        pos = _slots(idx, rank, pstart).reshape(n_assign)
        tok_ids = jnp.broadcast_to((jnp.arange(n_assign, dtype=jnp.int32) % tp)[:, None], (n_assign, LANES))
        scattered = _sc_scatter_rows(tok_ids, pos, n_slots)[:, 0]
        in_expert = (jnp.arange(n_slots, dtype=jnp.int32).reshape(n_blocks, EXPERT_ROWS)
                     - pstart[blk_expert][:, None])
        valid = (in_expert < counts[blk_expert][:, None]).reshape(n_slots)
        tok_slot = jnp.where(valid, scattered, jnp.arange(n_slots, dtype=jnp.int32) % tp)
        xb = h2.at[tok_slot + tok0].get(mode="promise_in_bounds")
        yb = _experts(xb, blk_expert, n_used, next_expert, w_exp_gate_up[layer], w_exp_down[layer])
        yg = yb.at[pos].get(mode="promise_in_bounds", unique_indices=True).reshape(TOP_K, tp, d)
        outs.append(_final(x1, h2, yg, wts.T, mod3, l, w_sh_gu, w_sh_dn, final_norm_g, tok0, tm=min(512, l)))
    return jnp.concatenate(outs, axis=0).reshape(b, l, d)
```

```python
import functools

import jax
import jax.numpy as jnp
from jax import lax
from jax.experimental import pallas as pl
from jax.experimental.pallas import tpu as pltpu
from jax.experimental.pallas import tpu_sc as plsc

F32 = jnp.float32
BF16 = jnp.bfloat16
HI = lax.Precision.HIGHEST

EPS = 1e-6
CHUNK = 64
GRID_W = 64
HEADS = 8
HEAD_V = 128
GDN_DK = 128
ML_DK = 64
GDN_CONV = 5
N_EXPERTS = 256
TOP_K = 8
N_GROUPS = 8
TOPK_GROUPS = 4
ROUTED_SCALE = 2.5
EXPERT_ROWS = 512
BATCH_PARTS = 1
GDN_STEPS = 4
GDN_CTX_STEPS = 4
ML_STEPS = 4
RING = 2 * GDN_STEPS
LANES = 128
VMEM_LIMIT = 56 * 1024 * 1024

COL_GDN_QKV = 0
COL_GDN_Z = 3072
COL_ML_Q = 4096
COL_ML_K = 4608
COL_ML_V = 5120
COL_ML_O = 6144
COL_MG_GDN = 7168
COL_MG_ML = 8192
N_MAIN = 9216
_ORIG = dict(gdn_qkv=(0, 3072), gdn_z=(3072, 4096), gdn_gate=(4096, 4128), ml_q=(4128, 4640),
             ml_k=(4640, 5152), ml_v=(5152, 6176), ml_o=(6176, 7200), ml_gate=(7200, 7232),
             mg_gdn=(7232, 8256), mg_ml=(8256, 9280))


def _params(sem, vmem=VMEM_LIMIT):
    return pltpu.CompilerParams(dimension_semantics=sem, vmem_limit_bytes=vmem)


def _dot(a, b, precision=None):
    return jnp.dot(a, b, preferred_element_type=F32, precision=precision)


def _dot_nt(a, b, precision=None):
    return lax.dot_general(a, b, (((1,), (1,)), ((), ())), preferred_element_type=F32, precision=precision)


def _dot_tn(a, b, precision=None):
    return lax.dot_general(a, b, (((0,), (0,)), ((), ())), preferred_element_type=F32, precision=precision)


def _sigmoid(x):
    return 1.0 / (1.0 + jnp.exp(-x))


def _softplus(x):
    return jnp.maximum(x, 0.0) + jnp.log(1.0 + jnp.exp(-jnp.abs(x)))


def _ada_kernel(c_ref, w_ref, b_ref, o_ref):
    c = c_ref[...]
    sc = c * _sigmoid(c)
    o_ref[...] = _dot(sc, w_ref[...], HI) + b_ref[...]


def _ada_mod(cc, w_ada, b_ada, tn=1536):
    r, d = cc.shape
    n = w_ada.shape[1]
    return pl.pallas_call(
        _ada_kernel,
        grid=(n // tn,),
        in_specs=[pl.BlockSpec((r, d), lambda j: (0, 0)),
                  pl.BlockSpec((d, tn), lambda j: (0, j)),
                  pl.BlockSpec((1, tn), lambda j: (0, j))],
        out_specs=pl.BlockSpec((r, tn), lambda j: (0, j)),
        out_shape=jax.ShapeDtypeStruct((r, n), F32),
        compiler_params=_params(("arbitrary",)),
    )(cc, w_ada, b_ada.reshape(1, n))


def _proj_kernel(x_ref, mod_ref, g_ref, w_ref, wg_ref, o_ref, og_ref, hn_ref):
    d = x_ref.shape[1]

    @pl.when(pl.program_id(1) == 0)
    def _():
        x = x_ref[...]
        y = x * lax.rsqrt(jnp.mean(x * x, axis=-1, keepdims=True) + EPS) * g_ref[...]
        shift = mod_ref[0, :, 0:d]
        scale = mod_ref[0, :, d:2 * d]
        h = (y * (1.0 + scale) + shift).astype(BF16)
        hn_ref[...] = h
        og_ref[...] = _dot(h, wg_ref[...])

    o_ref[...] = _dot(hn_ref[...], w_ref[...]).astype(o_ref.dtype)


def _project(x2d, mod3, mod_row_of_tile, norm_g, w_main, w_gate, tm, tn=2304):
    t, d = x2d.shape
    n = w_main.shape[1]
    return pl.pallas_call(
        _proj_kernel,
        grid=(t // tm, n // tn),
        in_specs=[pl.BlockSpec((tm, d), lambda i, j: (i, 0)),
                  pl.BlockSpec((1, 1, mod3.shape[2]), lambda i, j: (mod_row_of_tile(i), 0, 0)),
                  pl.BlockSpec((1, d), lambda i, j: (0, 0)),
                  pl.BlockSpec((d, tn), lambda i, j: (0, j)),
                  pl.BlockSpec((d, LANES), lambda i, j: (0, 0))],
        out_specs=[pl.BlockSpec((tm, tn), lambda i, j: (i, j)),
                   pl.BlockSpec((tm, LANES), lambda i, j: (i, 0))],
        out_shape=[jax.ShapeDtypeStruct((t, n), BF16), jax.ShapeDtypeStruct((t, LANES), F32)],
        scratch_shapes=[pltpu.VMEM((tm, d), BF16)],
        compiler_params=_params(("arbitrary", "arbitrary")),
    )(x2d, mod3, norm_g.reshape(1, d), w_main, w_gate)


def _gate_kernel(g_ref, p_ref, *out_refs, want_gd, want_ml):
    gd_ref = out_refs[0] if want_gd else None
    ml_ref = out_refs[-1] if want_ml else None
    rows = g_ref.shape[0]
    raw = g_ref[...] + p_ref[0:1, :]
    lane = lax.broadcasted_iota(jnp.int32, raw.shape, 1)
    sp = _softplus(raw)
    vals = jnp.where(lane < 16, _sigmoid(raw),
                     jnp.where(lane < 32, p_ref[1:2, :] * sp,
                               jnp.where(lane < 48, raw,
                                         jnp.where(lane < 64, -_softplus(-raw), 0.0))))
    ri = lax.broadcasted_iota(jnp.int32, (CHUNK, CHUNK), 0)
    ci = lax.broadcasted_iota(jnp.int32, (CHUNK, CHUNK), 1)
    tri_f = (ri >= ci).astype(BF16)
    tri_b = (ri <= ci).astype(BF16)
    lane_c = lax.broadcasted_iota(jnp.int32, (CHUNK, LANES), 1)
    row_c = lax.broadcasted_iota(jnp.int32, (CHUNK, LANES), 0)
    fwd_lane = (lane_c % 16) < 8

    def cumsum(tri, parts):
        h, m, lo = parts
        return _dot(tri, h) + (_dot(tri, m) + _dot(tri, lo))

    for c in range(rows // CHUNK):
        blk = vals[c * CHUNK:(c + 1) * CHUNK, :]
        parts = _split3(blk)
        cum = jnp.where(fwd_lane, cumsum(tri_f, parts), cumsum(tri_b, parts))
        if want_gd:
            gd_ref[c * CHUNK:(c + 1) * CHUNK, :] = jnp.where(lane_c < 16, blk, jnp.where(lane_c < 32, cum, 0.0))
        if want_ml:
            bcum = pltpu.roll(cum, LANES - 16, axis=1)
            gmb = blk - bcum
            cmf, cmb = gmb, gmb
            for s in (1, 2, 4, 8, 16, 32):
                cmf = jnp.maximum(cmf, jnp.where(row_c >= s, pltpu.roll(cmf, s, axis=0), -jnp.inf))
                cmb = jnp.maximum(cmb, jnp.where(row_c < CHUNK - s, pltpu.roll(cmb, CHUNK - s, axis=0), -jnp.inf))
            cm = jnp.where(fwd_lane, cmf, cmb)
            ml = jnp.where(lane_c < 16, pltpu.roll(gmb, LANES - 32, axis=1),
                           jnp.where(lane_c < 32, pltpu.roll(cm, LANES - 16, axis=1),
                                     jnp.where(lane_c < 48, bcum, 0.0)))
            ml_ref[c * CHUNK:(c + 1) * CHUNK, :] = ml


def _gate_prep(graw, gparams, want_gd, want_ml, tm=256):
    t = graw.shape[0]
    spec = pl.BlockSpec((tm, LANES), lambda i: (i, 0))
    n_out = int(want_gd) + int(want_ml)
    return pl.pallas_call(
        functools.partial(_gate_kernel, want_gd=want_gd, want_ml=want_ml),
        grid=(t // tm,),
        in_specs=[spec, pl.BlockSpec((8, LANES), lambda i: (0, 0))],
        out_specs=[spec] * n_out,
        out_shape=[jax.ShapeDtypeStruct((t, LANES), F32)] * n_out,
        compiler_params=_params(("arbitrary",)),
    )(graw, gparams)


def _split3(a):
    h = a.astype(BF16)
    r = a - h.astype(F32)
    m = r.astype(BF16)
    return h, m, (r - m.astype(F32)).astype(BF16)


def _lane_picks(x, lanes, n_by_matmul=0):
    li = lax.broadcasted_iota(jnp.int32, (LANES, LANES), 0)
    ci = lax.broadcasted_iota(jnp.int32, (LANES, LANES), 1)
    want = jnp.full((LANES, LANES), -1, jnp.int32)
    for j, lane in enumerate(lanes):
        want = jnp.where(ci == j, lane, want)
    sel = (li == want).astype(BF16)
    h, m, lo = _split3(x)
    cols = _dot(h, sel) + (_dot(m, sel) + _dot(lo, sel))
    out = [jnp.broadcast_to(cols[:, j:j + 1], x.shape) for j in range(len(lanes) - n_by_matmul)]
    for lane in lanes[len(lanes) - n_by_matmul:]:
        rep = (li == lane).astype(BF16)
        out.append(_dot(h, rep) + (_dot(m, rep) + _dot(lo, rep)))
    return out


def _dir_masks(direction):
    ri = lax.broadcasted_iota(jnp.int32, (CHUNK, CHUNK), 0)
    ci = lax.broadcasted_iota(jnp.int32, (CHUNK, CHUNK), 1)
    if direction == 0:
        return ri >= ci, ri > ci
    return ri <= ci, ri < ci


def _gdn_kernel(qc_ref, kc_ref, vc_ref, ql_ref, kl_ref, vl_ref, z_ref, cwq_ref, cwk_ref, cwv_ref,
                gdc_ref, gdl_ref, ng_ref, y_ref,
                xpad, qs, ks, vs, beta_t, cg_t, wq_r, u_r, kd_r, qk_r, dc_r, out_s):
    lc = qc_ref.shape[1]
    ll = ql_ref.shape[1]
    lt = lc + ll
    n_c, n_l = lc // CHUNK, ll // CHUNK
    n_t = n_c + n_l
    rb = 256

    def l2n(x):
        return x * lax.rsqrt(jnp.sum(x * x, axis=-1, keepdims=True) + EPS)

    def conv_chain(src_ref, cw_ref, dst, off, ls, kind, which):
        pad = xpad.at[which]
        pad[0:8, :] = jnp.zeros((8, LANES), F32)
        pad[8:8 + ls, :] = src_ref[0].astype(F32)
        pad[8 + ls:16 + ls, :] = jnp.zeros((8, LANES), F32)
        step = min(rb, ls)
        for r0 in range(0, ls, step):
            acc = jnp.zeros((step, LANES), F32)
            for t in range(GDN_CONV):
                s0 = r0 + 8 - GDN_CONV // 2 + t
                acc = acc + cw_ref[t:t + 1, :] * pad[s0:s0 + step, :]
            y = acc * _sigmoid(acc)
            if kind == "q":
                y = l2n(y) * (GDN_DK ** -0.5)
            elif kind == "k":
                y = l2n(y)
            dst[off + r0:off + r0 + step, :] = y
            yield

    def conv_chains(q_ref, k_ref, v_ref, off, ls):
        return [conv_chain(q_ref, cwq_ref, qs, off, ls, "q", 0), conv_chain(k_ref, cwk_ref, ks, off, ls, "k", 1),
                conv_chain(v_ref, cwv_ref, vs, off, ls, "v", 2)]

    def lockstep(chains):
        chains = list(chains)
        while chains:
            alive = []
            for ch in chains:
                try:
                    next(ch)
                    alive.append(ch)
                except StopIteration:
                    pass
            chains = alive

    lockstep(conv_chains(qc_ref, kc_ref, vc_ref, 0, lc))

    head = pl.program_id(1)

    def build_tables(src_ref, off, ls):
        step = min(rb, ls)
        for r0 in range(0, ls, step):
            picked = _lane_picks(src_ref[0, r0:r0 + step, :], [8 * d + head for d in range(2)]
                                 + [16 + 8 * d + head for d in range(2)])
            for d in range(2):
                beta_t[d, off + r0:off + r0 + step, :] = picked[d]
                cg_t[d, off + r0:off + r0 + step, :] = picked[2 + d]

    build_tables(gdc_ref, 0, lc)
    build_tables(gdl_ref, lc, ll)

    def bwd_chunk(t):
        return jnp.where(t < n_c, n_c - 1 - t, n_t + n_c - 1 - t)

    row_p = lax.broadcasted_iota(jnp.int32, (CHUNK, LANES), 0)
    lane_p = lax.broadcasted_iota(jnp.int32, (CHUNK, LANES), 1)
    fwd_p = lane_p < CHUNK
    col_p = jnp.where(fwd_p, lane_p, lane_p - CHUNK)
    signed = jnp.where(fwd_p, row_p - col_p, col_p - row_p)
    incl_p = signed >= 0
    strict_p = signed > 0
    eye_p = (row_p == col_p).astype(F32)
    keep_f = fwd_p.astype(BF16)
    keep_b = (1.0 - fwd_p.astype(F32)).astype(BF16)

    def block_diag(top, bottom):
        zero = jnp.zeros(top.shape, top.dtype)
        return jnp.concatenate([jnp.concatenate([top, zero], axis=1), jnp.concatenate([zero, bottom], axis=1)], axis=0)

    def pair_diag(xp):
        return jnp.concatenate([xp * keep_f, xp * keep_b], axis=0)

    def pair_times3(a, xp):
        ah = a.astype(BF16)
        al = (a - ah.astype(F32)).astype(BF16)
        xh = xp.astype(BF16)
        xl = (xp - xh.astype(F32)).astype(BF16)
        dh = pair_diag(xh)
        return _dot(ah, dh) + (_dot(ah, pair_diag(xl)) + _dot(al, dh))

    def prep_chain(t):
        tc = jnp.minimum(t, n_t - 1)
        q, k, v, beta, cgc, kb, ecg = [], [], [], [], [], [], []
        for d in range(2):
            c = tc if d == 0 else bwd_chunk(tc)
            rows = pl.ds(pl.multiple_of(c * CHUNK, CHUNK), CHUNK)
            q.append(qs[rows, :])
            k.append(ks[rows, :])
            v.append(vs[rows, :])
            beta.append(beta_t[d, rows, :])
            cgc.append(cg_t[d, rows, :])
        kdiag = block_diag(k[0].astype(BF16), k[1].astype(BF16))
        kk = _dot_nt(jnp.concatenate([k[0], k[1]], axis=1).astype(BF16), kdiag)
        qk = _dot_nt(jnp.concatenate([q[0], q[1]], axis=1).astype(BF16), kdiag)
        yield
        beta_p = jnp.where(fwd_p, beta[0], beta[1])
        cg_p = jnp.where(fwd_p, cgc[0], cgc[1])
        cgr_p = jnp.transpose(jnp.concatenate([cgc[0], cgc[1]], axis=0))[0:CHUNK, :]
        decay = jnp.exp(jnp.where(incl_p, cg_p - cgr_p, -jnp.inf))
        qkd = qk * decay
        for d in range(2):
            slot = (t % RING) * 2 + d
            last = CHUNK - 1 if d == 0 else 0
            cg_last = cgc[d][last:last + 1, :]
            ecg.append(jnp.exp(cgc[d]))
            kb.append(k[d] * beta[d])
            qk_r[slot] = qkd[:, d * CHUNK:(d + 1) * CHUNK]
            wq_r[slot, CHUNK:2 * CHUNK, :] = q[d] * ecg[d]
            kd_r[slot] = k[d] * jnp.exp(cg_last - cgc[d])
            dc_r[slot] = jnp.broadcast_to(jnp.exp(cg_last), (8, LANES))
        x = jnp.where(strict_p, -(beta_p * kk) * decay, 0.0)
        tinv = eye_p + x
        x = pair_times3(x, x)
        yield
        for _ in range(4):
            both = pair_times3(jnp.concatenate([tinv, x], axis=0), x)
            tinv, x = tinv + both[0:CHUNK, :], both[CHUNK:2 * CHUNK, :]
            yield
        tinv = (tinv + pair_times3(tinv, x)).astype(BF16)
        yield
        w = _dot(tinv, block_diag((kb[0] * ecg[0]).astype(BF16), (kb[1] * ecg[1]).astype(BF16)))
        u = _dot(tinv, block_diag((v[0] * beta[0]).astype(BF16), (v[1] * beta[1]).astype(BF16)))
        for d in range(2):
            slot = (t % RING) * 2 + d
            wq_r[slot, 0:CHUNK, :] = w[:, d * LANES:(d + 1) * LANES]
            u_r[slot] = u[:, d * LANES:(d + 1) * LANES]

    out_s[...] = jnp.zeros(out_s.shape, F32)

    def scan_chain(d, t0, steps, s, with_out, result):
        for j in range(steps):
            t = t0 + j
            slot = (t % RING) * 2 + d
            ws = _dot(wq_r[slot], s)
            yield
            v_new = u_r[slot] - ws[0:CHUNK, :]
            if with_out:
                c = t if d == 0 else bwd_chunk(t)
                o = ws[CHUNK:2 * CHUNK, :] + _dot(qk_r[slot], v_new)
                l0 = pl.multiple_of((c - n_c) * CHUNK, CHUNK)
                out_s[pl.ds(l0, CHUNK), :] += o
            s = s * dc_r[slot][0:1, :] + _dot_tn(kd_r[slot], v_new)
            yield
        result[d] = s

    def group_body(i, carry, t_base, steps, ahead, with_out, beside=()):
        t0 = t_base + steps * i
        result = [None, None]
        lockstep([scan_chain(d, t0, steps, carry[d], with_out, result) for d in range(2)]
                 + [prep_chain(t0 + steps + j) for j in range(ahead)] + list(beside))
        return result[0], result[1]

    def run_groups(carry, t_base, n_steps, steps, ahead_last, with_out, beside_last=()):
        n_groups = n_steps // steps
        carry = lax.fori_loop(0, n_groups - 1, functools.partial(
            group_body, t_base=t_base, steps=steps, ahead=steps, with_out=with_out), carry)
        return group_body(n_groups - 1, carry, t_base, steps, ahead_last, with_out, beside_last)

    def out_chain(blocks):
        for j in blocks:
            o = out_s[j * rb:(j + 1) * rb, :]
            z = z_ref[0, j * rb:(j + 1) * rb, :].astype(F32)
            y = o * lax.rsqrt(jnp.mean(o * o, axis=-1, keepdims=True) + EPS) * ng_ref[...]
            y_ref[0, j * rb:(j + 1) * rb, :] = (y * (z * _sigmoid(z))).astype(y_ref.dtype)
            yield

    per_block = rb // CHUNK
    n_blocks = ll // rb
    early = [j for j in range(n_blocks)
             if n_l > GDN_STEPS and j * per_block >= GDN_STEPS and (j + 1) * per_block <= n_l - GDN_STEPS]
    late = [j for j in range(n_blocks) if j not in early]

    lockstep([prep_chain(j) for j in range(GDN_CTX_STEPS)] + conv_chains(ql_ref, kl_ref, vl_ref, lc, ll))
    s0 = jnp.zeros((GDN_DK, HEAD_V), F32)
    carry = run_groups((s0, s0), 0, n_c, GDN_CTX_STEPS, GDN_STEPS, False)
    run_groups(carry, n_c, n_l, GDN_STEPS, 0, True, beside_last=[out_chain(early)])
    lockstep([out_chain(late)])


def _gdn(proj_c, proj_l, conv_w8, gd_c, gd_l, norm_g, b0, nb):
    lc = proj_c.shape[1]
    ll = proj_l.shape[1]
    lt = lc + ll
    qb, kb_, vb, zb = (COL_GDN_QKV // LANES, COL_GDN_QKV // LANES + HEADS, COL_GDN_QKV // LANES + 2 * HEADS,
                       COL_GDN_Z // LANES)

    def seq_spec(l, col0):
        return pl.BlockSpec((1, l, LANES), lambda i, h: (i + b0, 0, col0 + h))

    def cw_spec(col0):
        return pl.BlockSpec((8, LANES), lambda i, h: (0, col0 + h))

    return pl.pallas_call(
        _gdn_kernel,
        grid=(nb, HEADS),
        in_specs=[seq_spec(lc, qb), seq_spec(lc, kb_), seq_spec(lc, vb),
                  seq_spec(ll, qb), seq_spec(ll, kb_), seq_spec(ll, vb), seq_spec(ll, zb),
                  cw_spec(0), cw_spec(HEADS), cw_spec(2 * HEADS),
                  pl.BlockSpec((1, lc, LANES), lambda i, h: (i + b0, 0, 0)),
                  pl.BlockSpec((1, ll, LANES), lambda i, h: (i + b0, 0, 0)),
                  pl.BlockSpec((1, LANES), lambda i, h: (0, 0))],
        out_specs=pl.BlockSpec((1, ll, LANES), lambda i, h: (i, 0, h)),
        out_shape=jax.ShapeDtypeStruct((nb, ll, HEADS * HEAD_V), BF16),
        scratch_shapes=[pltpu.VMEM((3, max(lc, ll) + 16, LANES), F32),
                        pltpu.VMEM((lt, LANES), F32), pltpu.VMEM((lt, LANES), F32), pltpu.VMEM((lt, LANES), F32),
                        pltpu.VMEM((2, lt, LANES), F32), pltpu.VMEM((2, lt, LANES), F32),
                        pltpu.VMEM((2 * RING, 2 * CHUNK, LANES), F32),
                        pltpu.VMEM((2 * RING, CHUNK, LANES), F32), pltpu.VMEM((2 * RING, CHUNK, LANES), F32),
                        pltpu.VMEM((2 * RING, CHUNK, CHUNK), F32),
                        pltpu.VMEM((2 * RING, 8, LANES), F32),
                        pltpu.VMEM((ll, LANES), F32)],
        compiler_params=_params(("arbitrary", "arbitrary")),
    )(proj_c, proj_c, proj_c, proj_l, proj_l, proj_l, proj_l, conv_w8, conv_w8, conv_w8,
      gd_c, gd_l, norm_g.reshape(1, LANES))


def _mlstm_kernel(qc_ref, kc_ref, vc_ref, ql_ref, kl_ref, vl_ref, mlc_ref, mll_ref, h_ref, out_s, tabs):
    lc = qc_ref.shape[1]
    ll = ql_ref.shape[1]
    n_c, n_l = lc // CHUNK, ll // CHUNK
    n_t = n_c + n_l
    lt = lc + ll
    pair = pl.program_id(1)
    lane = lax.broadcasted_iota(jnp.int32, (CHUNK, LANES), 1)
    ones_v = jnp.ones((CHUNK, HEAD_V), BF16)
    chains = [(hh, d) for hh in range(2) for d in range(2)]
    hmask = [((lane // ML_DK) == hh).astype(F32) for hh in range(2)]
    incl = [_dir_masks(d)[0] for d in range(2)]

    def build_tables(src_ref, off, ls):
        step = min(256, ls)
        for r0 in range(0, ls, step):
            lanes = [16 * j + 8 * d + 2 * pair + hh for hh, d in chains for j in range(3)]
            for g, tab in enumerate(_lane_picks(src_ref[0, r0:r0 + step, :], lanes, n_by_matmul=5)):
                tabs[g, off + r0:off + r0 + step, :] = tab

    build_tables(mlc_ref, 0, lc)
    build_tables(mll_ref, lc, ll)

    def wide(a):
        return jnp.concatenate([a, a], axis=1)

    def chain(hh, d, c, get_state, is_ctx, result):
        last = CHUNK - 1 if d == 0 else 0
        if is_ctx:
            rows = pl.ds(pl.multiple_of(c * CHUNK, CHUNK), CHUNK)
            q_ref, k_ref, v_ref = qc_ref, kc_ref, vc_ref
        else:
            rows = pl.ds(pl.multiple_of((c - n_c) * CHUNK, CHUNK), CHUNK)
            q_ref, k_ref, v_ref = ql_ref, kl_ref, vl_ref
        q = (q_ref[0, rows, :].astype(F32) * hmask[hh]).astype(BF16)
        k = k_ref[0, rows, :].astype(F32) * (hmask[hh] * (ML_DK ** -0.5))
        v = jnp.concatenate([v_ref[0, rows, hh * HEAD_V:(hh + 1) * HEAD_V], ones_v], axis=1)
        n = chains.index((hh, d))
        trows = pl.ds(pl.multiple_of(c * CHUNK, CHUNK), CHUNK)
        gmb = tabs[3 * n, trows, :]
        gmb_t = jnp.transpose(gmb)[0:CHUNK, :]
        cm = tabs[3 * n + 1, trows, :]
        bc = tabs[3 * n + 2, trows, :]
        qk = _dot_nt(q, k.astype(BF16))
        cm_last = cm[last:last + 1, :]
        b_last = bc[last:last + 1, :]
        wk = (k * jnp.exp(gmb - cm_last)).astype(BF16)
        c_loc = _dot_tn(wk, v)
        yield
        while get_state() is None:
            yield
        cs, ms = get_state()
        mm = jnp.maximum(ms, cm)
        p = jnp.where(incl[d], jnp.exp(gmb_t - mm[:, 0:CHUNK]), 0.0) * qk
        inter = _dot(q, cs.astype(BF16))
        intra = _dot(p.astype(BF16), v)
        yield
        if not is_ctx:
            nd = wide(jnp.exp(ms - mm)) * inter + intra
            hv = nd[:, 0:HEAD_V] / jnp.maximum(jnp.abs(nd[:, HEAD_V:2 * HEAD_V]), jnp.exp(-(bc + mm)))
            l0 = pl.multiple_of((c - n_c) * CHUNK, CHUNK)
            out_s[pl.ds(l0, CHUNK), hh * HEAD_V:(hh + 1) * HEAD_V] += hv
        mx = jnp.maximum(ms, cm_last)
        result[hh, d] = (wide(jnp.exp(ms - mx)) * cs + wide(jnp.exp(cm_last - mx)) * c_loc, b_last + mx)

    out_s[...] = jnp.zeros(out_s.shape, F32)

    def run(chains_iter):
        live = list(chains_iter)
        while live:
            alive = []
            for ch in live:
                try:
                    next(ch)
                    alive.append(ch)
                except StopIteration:
                    pass
            live = alive

    def body(i, carry, is_ctx):
        results = [{} for _ in range(ML_STEPS)]
        gens = []
        for j in range(ML_STEPS):
            t = ML_STEPS * i + j
            for n, (hh, d) in enumerate(chains):
                if is_ctx:
                    c = t if d == 0 else n_c - 1 - t
                else:
                    c = n_c + t if d == 0 else n_t - 1 - t
                if j == 0:
                    get_state = functools.partial(lambda n: carry[n], n)
                else:
                    get_state = functools.partial(results[j - 1].get, (hh, d))
                gens.append(chain(hh, d, c, get_state, is_ctx, results[j]))
        run(gens)
        return tuple(results[-1][hd] for hd in chains)

    st0 = (jnp.zeros((LANES, 2 * HEAD_V), F32), jnp.zeros((1, LANES), F32))
    carry = lax.fori_loop(0, n_c // ML_STEPS, functools.partial(body, is_ctx=True), (st0,) * 4)
    lax.fori_loop(0, n_l // ML_STEPS, functools.partial(body, is_ctx=False), carry)
    h_ref[0] = out_s[...].astype(h_ref.dtype)


def _mlstm(proj_c, q_l, k_l, v_l, ml_c, ml_l, b0):
    lc = proj_c.shape[1]
    nb, ll, _ = q_l.shape
    lt = lc + ll
    qb, kb_, vb = COL_ML_Q // LANES, COL_ML_K // LANES, COL_ML_V // (2 * HEAD_V)
    return pl.pallas_call(
        _mlstm_kernel,
        grid=(nb, HEADS // 2),
        in_specs=[pl.BlockSpec((1, lc, LANES), lambda i, p: (i + b0, 0, qb + p)),
                  pl.BlockSpec((1, lc, LANES), lambda i, p: (i + b0, 0, kb_ + p)),
                  pl.BlockSpec((1, lc, 2 * HEAD_V), lambda i, p: (i + b0, 0, vb + p)),
                  pl.BlockSpec((1, ll, LANES), lambda i, p: (i, 0, p)),
                  pl.BlockSpec((1, ll, LANES), lambda i, p: (i, 0, p)),
                  pl.BlockSpec((1, ll, 2 * HEAD_V), lambda i, p: (i, 0, p)),
                  pl.BlockSpec((1, lc, LANES), lambda i, p: (i + b0, 0, 0)),
                  pl.BlockSpec((1, ll, LANES), lambda i, p: (i + b0, 0, 0))],
        out_specs=pl.BlockSpec((1, ll, 2 * HEAD_V), lambda i, p: (i, 0, p)),
        out_shape=jax.ShapeDtypeStruct((nb, ll, HEADS * HEAD_V), BF16),
        scratch_shapes=[pltpu.VMEM((ll, 2 * HEAD_V), F32), pltpu.VMEM((12, lt, LANES), F32)],
        compiler_params=_params(("arbitrary", "arbitrary")),
    )(proj_c, proj_c, proj_c, q_l, k_l, v_l, ml_c, ml_l)


def _merge_kernel(yg_ref, hm_ref, o_ref, gg_ref, gm_ref, x_ref, mod_ref, mlg_ref, n2_ref,
                  wbg_ref, wbm_ref, wo_ref, wrh_ref, wrl_ref, x1_ref, h2_ref, sc_ref):
    d = x_ref.shape[1]
    o = o_ref[...].astype(F32)
    ym = _sigmoid(o) * hm_ref[...].astype(F32)
    segs = []
    for h in range(HEADS):
        seg = ym[:, h * HEAD_V:(h + 1) * HEAD_V]
        segs.append(seg * lax.rsqrt(jnp.mean(seg * seg, axis=-1, keepdims=True) + EPS))
    ymn = jnp.concatenate(segs, axis=1) * mlg_ref[...]
    y_gdn = _dot(yg_ref[...], wbg_ref[...])
    y_ml = _dot(ymn.astype(BF16), wbm_ref[...])
    mixed = _sigmoid(gg_ref[...].astype(F32)) * y_gdn + _sigmoid(gm_ref[...].astype(F32)) * y_ml
    y = _dot(mixed.astype(BF16), wo_ref[...])
    x1 = x_ref[...] + mod_ref[0, :, 2 * d:3 * d] * y
    x1_ref[...] = x1
    hn = x1 * lax.rsqrt(jnp.mean(x1 * x1, axis=-1, keepdims=True) + EPS) * n2_ref[...]
    h2 = hn * (1.0 + mod_ref[0, :, 4 * d:5 * d]) + mod_ref[0, :, 3 * d:4 * d]
    h2_hi = h2.astype(BF16)
    h2_ref[...] = h2_hi
    h2_lo = (h2 - h2_hi.astype(F32)).astype(BF16)
    logits = _dot_nt(wrh_ref[...], h2_hi) + (_dot_nt(wrl_ref[...], h2_hi) + _dot_nt(wrh_ref[...], h2_lo))
    sc_ref[...] = _sigmoid(logits)


def _merge(y_gdn, h_ml, proj_l2d, x2d, mod3, rows_per_mod, ml_norm_g, norm2_g, wbg, wbm, wo, wr_hi, wr_lo, tok0,
           tm=512):
    t, d = y_gdn.shape
    e = wr_hi.shape[0]
    off = tok0 // tm
    row = lambda i: (i, 0)
    const = lambda i: (0, 0)
    return pl.pallas_call(
        _merge_kernel,
        grid=(t // tm,),
        in_specs=[pl.BlockSpec((tm, d), row), pl.BlockSpec((tm, d), row),
                  pl.BlockSpec((tm, d), lambda i: (i + off, COL_ML_O // d)),
                  pl.BlockSpec((tm, d), lambda i: (i + off, COL_MG_GDN // d)),
                  pl.BlockSpec((tm, d), lambda i: (i + off, COL_MG_ML // d)),
                  pl.BlockSpec((tm, d), lambda i: (i + off, 0)),
                  pl.BlockSpec((1, 1, mod3.shape[2]), lambda i: (((i + off) * tm) // rows_per_mod, 0, 0)),
                  pl.BlockSpec((1, d), const), pl.BlockSpec((1, d), const),
                  pl.BlockSpec((d, d), const), pl.BlockSpec((d, d), const), pl.BlockSpec((d, d), const),
                  pl.BlockSpec((e, d), const), pl.BlockSpec((e, d), const)],
        out_specs=[pl.BlockSpec((tm, d), row), pl.BlockSpec((tm, d), lambda i: (i + off, 0)),
                   pl.BlockSpec((e, tm), lambda i: (0, i))],
        out_shape=[jax.ShapeDtypeStruct((t, d), F32), jax.ShapeDtypeStruct((x2d.shape[0], d), BF16),
                   jax.ShapeDtypeStruct((e, t), F32)],
        compiler_params=_params(("arbitrary",)),
    )(y_gdn, h_ml, proj_l2d, proj_l2d, proj_l2d, x2d, mod3, ml_norm_g.reshape(1, d), norm2_g.reshape(1, d),
      wbg, wbm, wo, wr_hi, wr_lo)


def _expert_kernel(be_ref, nu_ref, nx_ref, x_ref, wgu_hbm, wd_hbm, y_ref, land_gu, land_d, wgu_s, wd_s, sems):
    i = pl.program_id(0)
    de = wd_hbm.shape[1]
    used = i < nu_ref[0]
    first = jnp.logical_or(i == 0, be_ref[i] != be_ref[jnp.maximum(i - 1, 0)])

    def weight_copies(ex):
        return (pltpu.make_async_copy(wgu_hbm.at[ex], land_gu, sems.at[0]),
                pltpu.make_async_copy(wd_hbm.at[ex], land_d, sems.at[1]))

    @pl.when(i == 0)
    def _():
        for cp in weight_copies(be_ref[0]):
            cp.start()

    @pl.when(jnp.logical_and(first, used))
    def _():
        for cp in weight_copies(be_ref[i]):
            cp.wait()
        wgu_s[...] = land_gu[...].astype(BF16)
        wd_s[...] = land_d[...].astype(BF16)

        @pl.when(nx_ref[i] >= 0)
        def _():
            for cp in weight_copies(nx_ref[i]):
                cp.start()

    @pl.when(used)
    def _():
        gu = _dot(x_ref[...], wgu_s[...])
        g = gu[:, 0:de]
        act = (g * _sigmoid(g)) * gu[:, de:2 * de]
        y_ref[...] = _dot(act.astype(BF16), wd_s[...]).astype(y_ref.dtype)

    @pl.when(jnp.logical_not(used))
    def _():
        y_ref[...] = jnp.zeros(y_ref.shape, y_ref.dtype)


def _experts(xb, blk_expert, n_used, next_expert, w_gu, w_down):
    n_slots, d = xb.shape
    n_blocks = n_slots // EXPERT_ROWS
    e, _, de2 = w_gu.shape
    de = de2 // 2
    return pl.pallas_call(
        _expert_kernel,
        grid_spec=pltpu.PrefetchScalarGridSpec(
            num_scalar_prefetch=3,
            grid=(n_blocks,),
            in_specs=[pl.BlockSpec((EXPERT_ROWS, d), lambda i, be, nu, nx: (i, 0)),
                      pl.BlockSpec(memory_space=pl.ANY), pl.BlockSpec(memory_space=pl.ANY)],
            out_specs=pl.BlockSpec((EXPERT_ROWS, d), lambda i, be, nu, nx: (i, 0)),
            scratch_shapes=[pltpu.VMEM((d, de2), w_gu.dtype), pltpu.VMEM((de, d), w_down.dtype),
                            pltpu.VMEM((d, de2), BF16), pltpu.VMEM((de, d), BF16),
                            pltpu.SemaphoreType.DMA((2,))]),
        out_shape=jax.ShapeDtypeStruct((n_slots, d), BF16),
        compiler_params=_params(("arbitrary",)),
    )(blk_expert, n_used, next_expert, xb, w_gu, w_down)


def _final_kernel(x1_ref, h2_ref, yg_ref, wt_ref, mod_ref, wsg_ref, wsd_ref, fg_ref, o_ref):
    d = x1_ref.shape[1]
    ds_ = wsd_ref.shape[0]
    wt = wt_ref[...]
    routed = jnp.zeros(x1_ref.shape, F32)
    for k in range(TOP_K):
        routed = routed + wt[:, k:k + 1] * yg_ref[k].astype(F32)
    gu = _dot(h2_ref[...], wsg_ref[...])
    g = gu[:, 0:ds_]
    sh = _dot(((g * _sigmoid(g)) * gu[:, ds_:2 * ds_]).astype(BF16), wsd_ref[...])
    x2 = x1_ref[...] + mod_ref[0, :, 5 * d:6 * d] * (routed + sh)
    o_ref[...] = x2 * lax.rsqrt(jnp.mean(x2 * x2, axis=-1, keepdims=True) + EPS) * fg_ref[...]


def _final(x1, h2, yg, wts, mod3, rows_per_mod, w_sh_gu, w_sh_down, final_g, tok0, tm=256):
    tp, d = x1.shape
    off = tok0 // tm
    row = lambda i: (i, 0)
    const = lambda i: (0, 0)
    return pl.pallas_call(
        _final_kernel,
        grid=(tp // tm,),
        in_specs=[pl.BlockSpec((tm, d), row), pl.BlockSpec((tm, d), lambda i: (i + off, 0)),
                  pl.BlockSpec((TOP_K, tm, d), lambda i: (0, i, 0)), pl.BlockSpec((tm, TOP_K), row),
                  pl.BlockSpec((1, 1, mod3.shape[2]), lambda i: (((i + off) * tm) // rows_per_mod, 0, 0)),
                  pl.BlockSpec(w_sh_gu.shape, const), pl.BlockSpec(w_sh_down.shape, const),
                  pl.BlockSpec((1, d), const)],
        out_specs=pl.BlockSpec((tm, d), row),
        out_shape=jax.ShapeDtypeStruct((tp, d), F32),
        compiler_params=_params(("arbitrary",)),
    )(x1, h2, yg, wts, mod3, w_sh_gu, w_sh_down, final_g.reshape(1, d))


def _route_kernel(sc_ref, bias_ref, tri_ref, idx_ref, wt_ref, rk_ref, cnt_ref, base_s):
    @pl.when(pl.program_id(0) == 0)
    def _():
        base_s[...] = jnp.zeros(base_s.shape, F32)

    scores = sc_ref[...]
    e, tn = scores.shape
    gsz = e // N_GROUPS
    sel3 = (scores + bias_ref[...]).reshape(N_GROUPS, gsz, tn)
    m1 = jnp.max(sel3, axis=1)
    is_max = sel3 == m1[:, None, :]
    n_max = jnp.sum(is_max.astype(F32), axis=1)
    m2 = jnp.max(jnp.where(is_max, -jnp.inf, sel3), axis=1)
    grp = m1 + jnp.where(n_max >= 2.0, m1, m2)
    gi = lax.broadcasted_iota(jnp.int32, (N_GROUPS, tn), 0)
    ahead = jnp.zeros((N_GROUPS, tn), F32)
    for g in range(N_GROUPS):
        row = grp[g:g + 1, :]
        ahead = ahead + jnp.logical_or(row > grp, jnp.logical_and(row == grp, g < gi)).astype(F32)
    ahead3 = jnp.broadcast_to(ahead[:, None, :], (N_GROUPS, gsz, tn))
    selm = jnp.where(ahead3 < float(TOPK_GROUPS), sel3, -jnp.inf).reshape(e, tn)
    ri = lax.broadcasted_iota(jnp.int32, (e, tn), 0).astype(F32)
    member = jnp.zeros((e, tn), F32)
    idxs, ws = [], []
    for _ in range(TOP_K):
        m = jnp.max(selm, axis=0, keepdims=True)
        idx = jnp.min(jnp.where(selm == m, ri, float(e)), axis=0, keepdims=True)
        hit = ri == idx
        ws.append(jnp.sum(jnp.where(hit, scores, 0.0), axis=0, keepdims=True))
        idxs.append(idx)
        selm = jnp.where(hit, -jnp.inf, selm)
        member = jnp.where(hit, 1.0, member)
    w = jnp.concatenate(ws, axis=0)
    wt_ref[...] = w / jnp.sum(w, axis=0, keepdims=True) * ROUTED_SCALE
    idx_ref[...] = jnp.concatenate(idxs, axis=0).astype(jnp.int32)
    cum = _dot(member.astype(BF16), tri_ref[...]) + base_s[...]
    rk_ref[...] = jnp.concatenate(
        [jnp.sum(jnp.where(ri == idx, cum, 0.0), axis=0, keepdims=True) for idx in idxs], axis=0).astype(jnp.int32)
    total = base_s[...] + jnp.sum(member, axis=1, keepdims=True)
    base_s[...] = total
    cnt_ref[...] = total


def _route(scores_t, router_bias):
    e, t = scores_t.shape
    tn = LANES
    bias = jnp.broadcast_to(router_bias.astype(F32)[:, None], (e, tn))
    tri = (jnp.arange(tn)[:, None] < jnp.arange(tn)[None, :]).astype(BF16)
    tok = pl.BlockSpec((TOP_K, tn), lambda i: (0, i))
    const = lambda i: (0, 0)
    return pl.pallas_call(
        _route_kernel,
        grid=(t // tn,),
        in_specs=[pl.BlockSpec((e, tn), lambda i: (0, i)), pl.BlockSpec((e, tn), const),
                  pl.BlockSpec((tn, tn), const)],
        out_specs=[tok, tok, tok, pl.BlockSpec((e, tn), const)],
        out_shape=[jax.ShapeDtypeStruct((TOP_K, t), jnp.int32), jax.ShapeDtypeStruct((TOP_K, t), F32),
                   jax.ShapeDtypeStruct((TOP_K, t), jnp.int32), jax.ShapeDtypeStruct((e, tn), F32)],
        scratch_shapes=[pltpu.VMEM((e, tn), F32)],
        compiler_params=_params(("arbitrary",)),
    )(scores_t, bias, tri)


def _slot_kernel(idx_ref, rk_ref, ps_ref, pos_ref):
    e, tn = ps_ref.shape
    ri = lax.broadcasted_iota(jnp.int32, (e, tn), 0)
    ps = ps_ref[...]
    rows = [jnp.sum(jnp.where(ri == idx_ref[k:k + 1, :], ps, 0.0), axis=0, keepdims=True) for k in range(TOP_K)]
    pos_ref[...] = rk_ref[...] + jnp.concatenate(rows, axis=0).astype(jnp.int32)


def _slots(idx, rank, pstart):
    k, t = idx.shape
    e = pstart.shape[0]
    tn = LANES
    tok = pl.BlockSpec((k, tn), lambda i: (0, i))
    return pl.pallas_call(
        _slot_kernel,
        grid=(t // tn,),
        in_specs=[tok, tok, pl.BlockSpec((e, tn), lambda i: (0, 0))],
        out_specs=tok,
        out_shape=jax.ShapeDtypeStruct((k, t), jnp.int32),
        compiler_params=_params(("arbitrary",)),
    )(idx, rank, jnp.broadcast_to(pstart.astype(F32)[:, None], (e, tn)))


def _sc_scatter_rows(vals, idx, n_rows, window=LANES):
    n, width = vals.shape
    mesh = plsc.VectorSubcoreMesh(core_axis_name="core", subcore_axis_name="subcore")

    @pl.kernel(out_type=jax.ShapeDtypeStruct((n_rows, width), vals.dtype), mesh=mesh, scratch_types=[])
    def scatter(v_hbm, i_hbm, o_hbm):
        def body(v_vmem, i_vmem):
            pltpu.sync_copy(v_vmem, o_hbm.at[i_vmem.at[0]])

        pltpu.emit_pipeline(
            body,
            grid=(n // window,),
            in_specs=[pl.BlockSpec((window, width), lambda i: (i, 0)),
                      pl.BlockSpec((1, window), lambda i: (0, i))],
            out_specs=[],
            core_axis_name=("core", "subcore"),
            dimension_semantics=(pltpu.PARALLEL,),
        )(v_hbm, i_hbm)

    return scatter(vals, idx.reshape(1, n))


def _block_table(counts, n_blocks):
    padded = (counts + EXPERT_ROWS - 1) // EXPERT_ROWS * EXPERT_ROWS
    pend = jnp.cumsum(padded)
    first_slot = jnp.arange(n_blocks, dtype=jnp.int32) * EXPERT_ROWS
    blk_expert = jnp.minimum(jnp.sum((pend[None, :] <= first_slot[:, None]).astype(jnp.int32), axis=1),
                             N_EXPERTS - 1)
    e_ids = jnp.arange(N_EXPERTS, dtype=jnp.int32)
    later = lax.cummin(jnp.where(counts > 0, e_ids, N_EXPERTS)[::-1])[::-1]
    next_tab = jnp.concatenate([later[1:], jnp.full((1,), N_EXPERTS, jnp.int32)])
    next_tab = jnp.where(next_tab < N_EXPERTS, next_tab, -1)
    return (pend - padded, blk_expert, (pend[-1] // EXPERT_ROWS).astype(jnp.int32).reshape(1),
            next_tab[blk_expert].astype(jnp.int32))


def _col_major(t):
    b, l, f = t.shape
    rows = l // GRID_W
    return t.reshape(b, rows, GRID_W, f).transpose(0, 2, 1, 3).reshape(b, l, f)


def _row_major(t):
    b, l, f = t.shape
    rows = l // GRID_W
    return t.reshape(b, GRID_W, rows, f).transpose(0, 2, 1, 3).reshape(b, l, f)


def kernel(x, c, ctx, c_ctx, w_ada, b_ada, norm1_g, norm2_g, w_in, gdn_conv_w, gdn_a_log, gdn_dt_bias, gdn_norm_g,
           ml_i_bias, ml_f_bias, ml_norm_g, w_branch_gdn, w_branch_ml, w_out, w_router, router_bias, w_exp_gate_up,
           w_exp_down, w_sh_gate_up, w_sh_down, final_norm_g):
    b, l, d = x.shape
    lc = ctx.shape[1]
    t = b * l
    layer = 0

    w = w_in[layer]
    main_cols = [_ORIG[k] for k in ("gdn_qkv", "gdn_z", "ml_q", "ml_k", "ml_v", "ml_o", "mg_gdn", "mg_ml")]
    w_main = jnp.concatenate([w[:, a:e] for a, e in main_cols], axis=1).astype(BF16)
    w_gate = jnp.concatenate([w[:, _ORIG["gdn_gate"][0]:_ORIG["gdn_gate"][1]],
                              w[:, _ORIG["ml_gate"][0]:_ORIG["ml_gate"][1]],
                              jnp.zeros((d, LANES - 64), F32)], axis=1).astype(BF16)
    zeros16 = jnp.zeros((16,), F32)
    gp_add = jnp.concatenate([zeros16, gdn_dt_bias[layer].reshape(-1), ml_i_bias[layer].reshape(-1),
                              ml_f_bias[layer].reshape(-1), jnp.zeros((LANES - 64,), F32)])
    gp_mul = jnp.concatenate([zeros16, -jnp.exp(gdn_a_log[layer].astype(F32)).reshape(-1),
                              jnp.zeros((LANES - 32,), F32)])
    gparams = jnp.zeros((8, LANES), F32).at[0].set(gp_add).at[1].set(gp_mul)
    conv_w8 = jnp.zeros((8, gdn_conv_w.shape[2]), F32).at[0:GDN_CONV].set(gdn_conv_w[layer])
    wr = w_router[layer].T
    wr_hi = wr.astype(BF16)
    wr_lo = (wr - wr_hi.astype(F32)).astype(BF16)

    n_mod_rows = -(-(b + 1) // 8) * 8
    cc = jnp.zeros((n_mod_rows, d), F32).at[0:b].set(c).at[b].set(c_ctx)
    mod = _ada_mod(cc, w_ada[layer], b_ada[layer])
    mod3 = mod.reshape(n_mod_rows, 1, 6 * d)

    x2d = x.reshape(t, d)
    tm_l = min(1024, l)
    proj_l, gate_l = _project(x2d, mod3, lambda i: (i * tm_l) // l, norm1_g[layer], w_main, w_gate, tm_l)
    tm_c = min(1024, b * lc)
    proj_c, gate_c = _project(ctx.reshape(b * lc, d), mod3, lambda i: b, norm1_g[layer], w_main, w_gate, tm_c)
    proj_l3 = proj_l.reshape(b, l, N_MAIN)
    proj_c3 = proj_c.reshape(b, lc, N_MAIN)

    gate_l_cm = _col_major(gate_l.reshape(b, l, LANES)).reshape(t, LANES)
    gd_c, ml_c = _gate_prep(gate_c, gparams, True, True)
    gd_l, = _gate_prep(gate_l, gparams, True, False)
    ml_l, = _gate_prep(gate_l_cm, gparams, False, True)

    nb = b // BATCH_PARTS
    tp = nb * l
    n_assign = tp * TOP_K
    n_blocks = (n_assign + N_EXPERTS * (EXPERT_ROWS - 1)) // EXPERT_ROWS + 1
    n_slots = n_blocks * EXPERT_ROWS
    gd_c3, gd_l3 = gd_c.reshape(b, lc, LANES), gd_l.reshape(b, l, LANES)
    ml_c3, ml_l3 = ml_c.reshape(b, lc, LANES), ml_l.reshape(b, l, LANES)
    wbg, wbm, wo = (w_branch_gdn[layer].astype(BF16), w_branch_ml[layer].astype(BF16), w_out[layer].astype(BF16))
    w_sh_gu, w_sh_dn = w_sh_gate_up[layer].astype(BF16), w_sh_down[layer].astype(BF16)
    outs = []
    for part in range(BATCH_PARTS):
        b0, tok0 = part * nb, part * tp
        y_gdn = _gdn(proj_c3, proj_l3, conv_w8, gd_c3, gd_l3, gdn_norm_g[layer], b0, nb)
        q_cm = _col_major(proj_l3[b0:b0 + nb, :, COL_ML_Q:COL_ML_Q + HEADS * ML_DK])
        k_cm = _col_major(proj_l3[b0:b0 + nb, :, COL_ML_K:COL_ML_K + HEADS * ML_DK])
        v_cm = _col_major(proj_l3[b0:b0 + nb, :, COL_ML_V:COL_ML_V + HEADS * HEAD_V])
        h_ml = _row_major(_mlstm(proj_c3, q_cm, k_cm, v_cm, ml_c3, ml_l3, b0))
        x1, h2, scores_t = _merge(y_gdn.reshape(tp, d), h_ml.reshape(tp, d), proj_l, x2d, mod3, l, ml_norm_g[layer],
                                norm2_g[layer], wbg, wbm, wo, wr_hi, wr_lo, tok0, tm=min(512, l))
        idx, wts, rank, cnt = _route(scores_t, router_bias[layer])
        counts = cnt[:, 0].astype(jnp.int32)
        pstart, blk_expert, n_used, next_expert = _block_table(counts, n_blocks)
        pos = _slots(idx, rank, pstart).reshape(n_assign)
        tok_ids = jnp.broadcast_to((jnp.arange(n_assign, dtype=jnp.int32) % tp)[:, None], (n_assign, LANES))
        scattered = _sc_scatter_rows(tok_ids, pos, n_slots)[:, 0]
        in_expert = (jnp.arange(n_slots, dtype=jnp.int32).reshape(n_blocks, EXPERT_ROWS)
                     - pstart[blk_expert][:, None])
        valid = (in_expert < counts[blk_expert][:, None]).reshape(n_slots)
        tok_slot = jnp.where(valid, scattered, jnp.arange(n_slots, dtype=jnp.int32) % tp)
        xb = h2.at[tok_slot + tok0].get(mode="promise_in_bounds")
        yb = _experts(xb, blk_expert, n_used, next_expert, w_exp_gate_up[layer], w_exp_down[layer])
        yg = yb.at[pos].get(mode="promise_in_bounds", unique_indices=True).reshape(TOP_K, tp, d)
        outs.append(_final(x1, h2, yg, wts.T, mod3, l, w_sh_gu, w_sh_dn, final_norm_g, tok0, tm=min(512, l)))
    return jnp.concatenate(outs, axis=0).reshape(b, l, d)
```

```python
import functools

import jax
import jax.numpy as jnp
from jax import lax
from jax.experimental import pallas as pl
from jax.experimental.pallas import tpu as pltpu
from jax.experimental.pallas import tpu_sc as plsc

F32 = jnp.float32
BF16 = jnp.bfloat16
HI = lax.Precision.HIGHEST

EPS = 1e-6
CHUNK = 64
GRID_W = 64
HEADS = 8
HEAD_V = 128
GDN_DK = 128
ML_DK = 64
GDN_CONV = 5
N_EXPERTS = 256
TOP_K = 8
N_GROUPS = 8
TOPK_GROUPS = 4
ROUTED_SCALE = 2.5
EXPERT_ROWS = 512
BATCH_PARTS = 1
GDN_STEPS = 4
GDN_CTX_STEPS = 4
ML_STEPS = 4
RING = 2 * GDN_STEPS
LANES = 128
VMEM_LIMIT = 56 * 1024 * 1024

COL_GDN_QKV = 0
COL_ML_Q = 3072
COL_ML_K = 3584
COL_ML_V = 4096
N_SCAN = 5120
COL_GDN_Z = 5120
COL_ML_O = 6144
COL_MG_GDN = 7168
COL_MG_ML = 8192
N_MAIN = 9216
_ORIG = dict(gdn_qkv=(0, 3072), gdn_z=(3072, 4096), gdn_gate=(4096, 4128), ml_q=(4128, 4640),
             ml_k=(4640, 5152), ml_v=(5152, 6176), ml_o=(6176, 7200), ml_gate=(7200, 7232),
             mg_gdn=(7232, 8256), mg_ml=(8256, 9280))


def _params(sem, vmem=VMEM_LIMIT):
    return pltpu.CompilerParams(dimension_semantics=sem, vmem_limit_bytes=vmem)


def _dot(a, b, precision=None):
    return jnp.dot(a, b, preferred_element_type=F32, precision=precision)


def _dot_nt(a, b, precision=None):
    return lax.dot_general(a, b, (((1,), (1,)), ((), ())), preferred_element_type=F32, precision=precision)


def _dot_tn(a, b, precision=None):
    return lax.dot_general(a, b, (((0,), (0,)), ((), ())), preferred_element_type=F32, precision=precision)


def _sigmoid(x):
    return 1.0 / (1.0 + jnp.exp(-x))


def _softplus(x):
    return jnp.maximum(x, 0.0) + jnp.log(1.0 + jnp.exp(-jnp.abs(x)))


def _ada_kernel(c_ref, w_ref, b_ref, o_ref):
    c = c_ref[...]
    sc = c * _sigmoid(c)
    o_ref[...] = _dot(sc, w_ref[...], HI) + b_ref[...]


def _ada_mod(cc, w_ada, b_ada, tn=1536):
    r, d = cc.shape
    n = w_ada.shape[1]
    return pl.pallas_call(
        _ada_kernel,
        grid=(n // tn,),
        in_specs=[pl.BlockSpec((r, d), lambda j: (0, 0)),
                  pl.BlockSpec((d, tn), lambda j: (0, j)),
                  pl.BlockSpec((1, tn), lambda j: (0, j))],
        out_specs=pl.BlockSpec((r, tn), lambda j: (0, j)),
        out_shape=jax.ShapeDtypeStruct((r, n), F32),
        compiler_params=_params(("arbitrary",)),
    )(cc, w_ada, b_ada.reshape(1, n))


def _proj_kernel(x_ref, mod_ref, g_ref, w_ref, wg_ref, o_ref, og_ref, hn_ref):
    d = x_ref.shape[1]

    @pl.when(pl.program_id(1) == 0)
    def _():
        x = x_ref[...]
        y = x * lax.rsqrt(jnp.mean(x * x, axis=-1, keepdims=True) + EPS) * g_ref[...]
        shift = mod_ref[0, :, 0:d]
        scale = mod_ref[0, :, d:2 * d]
        h = (y * (1.0 + scale) + shift).astype(BF16)
        hn_ref[...] = h
        og_ref[...] = _dot(h, wg_ref[...])

    o_ref[...] = _dot(hn_ref[...], w_ref[...]).astype(o_ref.dtype)


def _project(x2d, mod3, mod_row_of_tile, norm_g, w_main, w_gate, tm, tn=2304):
    t, d = x2d.shape
    n = w_main.shape[1]
    return pl.pallas_call(
        _proj_kernel,
        grid=(t // tm, n // tn),
        in_specs=[pl.BlockSpec((tm, d), lambda i, j: (i, 0)),
                  pl.BlockSpec((1, 1, mod3.shape[2]), lambda i, j: (mod_row_of_tile(i), 0, 0)),
                  pl.BlockSpec((1, d), lambda i, j: (0, 0)),
                  pl.BlockSpec((d, tn), lambda i, j: (0, j)),
                  pl.BlockSpec((d, LANES), lambda i, j: (0, 0))],
        out_specs=[pl.BlockSpec((tm, tn), lambda i, j: (i, j)),
                   pl.BlockSpec((tm, LANES), lambda i, j: (i, 0))],
        out_shape=[jax.ShapeDtypeStruct((t, n), BF16), jax.ShapeDtypeStruct((t, LANES), F32)],
        scratch_shapes=[pltpu.VMEM((tm, d), BF16)],
        compiler_params=_params(("arbitrary", "arbitrary")),
    )(x2d, mod3, norm_g.reshape(1, d), w_main, w_gate)


def _gate_kernel(g_ref, p_ref, *out_refs, want_gd, want_ml):
    gd_ref = out_refs[0] if want_gd else None
    ml_ref = out_refs[-1] if want_ml else None
    rows = g_ref.shape[0]
    raw = g_ref[...] + p_ref[0:1, :]
    lane = lax.broadcasted_iota(jnp.int32, raw.shape, 1)
    sp = _softplus(raw)
    vals = jnp.where(lane < 16, _sigmoid(raw),
                     jnp.where(lane < 32, p_ref[1:2, :] * sp,
                               jnp.where(lane < 48, raw,
                                         jnp.where(lane < 64, -_softplus(-raw), 0.0))))
    ri = lax.broadcasted_iota(jnp.int32, (CHUNK, CHUNK), 0)
    ci = lax.broadcasted_iota(jnp.int32, (CHUNK, CHUNK), 1)
    tri_f = (ri >= ci).astype(BF16)
    tri_b = (ri <= ci).astype(BF16)
    lane_c = lax.broadcasted_iota(jnp.int32, (CHUNK, LANES), 1)
    row_c = lax.broadcasted_iota(jnp.int32, (CHUNK, LANES), 0)
    fwd_lane = (lane_c % 16) < 8

    def cumsum(tri, parts):
        h, m, lo = parts
        return _dot(tri, h) + (_dot(tri, m) + _dot(tri, lo))

    for c in range(rows // CHUNK):
        blk = vals[c * CHUNK:(c + 1) * CHUNK, :]
        parts = _split3(blk)
        cum = jnp.where(fwd_lane, cumsum(tri_f, parts), cumsum(tri_b, parts))
        if want_gd:
            gd_ref[c * CHUNK:(c + 1) * CHUNK, :] = jnp.where(lane_c < 16, blk, jnp.where(lane_c < 32, cum, 0.0))
        if want_ml:
            bcum = pltpu.roll(cum, LANES - 16, axis=1)
            gmb = blk - bcum
            cmf, cmb = gmb, gmb
            for s in (1, 2, 4, 8, 16, 32):
                cmf = jnp.maximum(cmf, jnp.where(row_c >= s, pltpu.roll(cmf, s, axis=0), -jnp.inf))
                cmb = jnp.maximum(cmb, jnp.where(row_c < CHUNK - s, pltpu.roll(cmb, CHUNK - s, axis=0), -jnp.inf))
            cm = jnp.where(fwd_lane, cmf, cmb)
            ml = jnp.where(lane_c < 16, pltpu.roll(gmb, LANES - 32, axis=1),
                           jnp.where(lane_c < 32, pltpu.roll(cm, LANES - 16, axis=1),
                                     jnp.where(lane_c < 48, bcum, 0.0)))
            ml_ref[c * CHUNK:(c + 1) * CHUNK, :] = ml


def _gate_prep(graw, gparams, want_gd, want_ml, tm=256):
    t = graw.shape[0]
    spec = pl.BlockSpec((tm, LANES), lambda i: (i, 0))
    n_out = int(want_gd) + int(want_ml)
    return pl.pallas_call(
        functools.partial(_gate_kernel, want_gd=want_gd, want_ml=want_ml),
        grid=(t // tm,),
        in_specs=[spec, pl.BlockSpec((8, LANES), lambda i: (0, 0))],
        out_specs=[spec] * n_out,
        out_shape=[jax.ShapeDtypeStruct((t, LANES), F32)] * n_out,
        compiler_params=_params(("arbitrary",)),
    )(graw, gparams)


def _split3(a):
    h = a.astype(BF16)
    r = a - h.astype(F32)
    m = r.astype(BF16)
    return h, m, (r - m.astype(F32)).astype(BF16)


def _lane_picks(x, lanes, n_by_matmul=0):
    li = lax.broadcasted_iota(jnp.int32, (LANES, LANES), 0)
    ci = lax.broadcasted_iota(jnp.int32, (LANES, LANES), 1)
    want = jnp.full((LANES, LANES), -1, jnp.int32)
    for j, lane in enumerate(lanes):
        want = jnp.where(ci == j, lane, want)
    sel = (li == want).astype(BF16)
    h, m, lo = _split3(x)
    cols = _dot(h, sel) + (_dot(m, sel) + _dot(lo, sel))
    out = [jnp.broadcast_to(cols[:, j:j + 1], x.shape) for j in range(len(lanes) - n_by_matmul)]
    for lane in lanes[len(lanes) - n_by_matmul:]:
        rep = (li == lane).astype(BF16)
        out.append(_dot(h, rep) + (_dot(m, rep) + _dot(lo, rep)))
    return out


def _dir_masks(direction):
    ri = lax.broadcasted_iota(jnp.int32, (CHUNK, CHUNK), 0)
    ci = lax.broadcasted_iota(jnp.int32, (CHUNK, CHUNK), 1)
    if direction == 0:
        return ri >= ci, ri > ci
    return ri <= ci, ri < ci


def _gdn_kernel(qc_ref, kc_ref, vc_ref, ql_ref, kl_ref, vl_ref, z_ref, cwq_ref, cwk_ref, cwv_ref,
                gdc_ref, gdl_ref, ng_ref, y_ref,
                xpad, qs, ks, vs, beta_t, cg_t, wq_r, u_r, kd_r, qk_r, dc_r, out_s):
    lc = qc_ref.shape[1]
    ll = ql_ref.shape[1]
    lt = lc + ll
    n_c, n_l = lc // CHUNK, ll // CHUNK
    n_t = n_c + n_l
    rb = 256

    def l2n(x):
        return x * lax.rsqrt(jnp.sum(x * x, axis=-1, keepdims=True) + EPS)

    def conv_chain(src_ref, cw_ref, dst, off, ls, kind, which):
        pad = xpad.at[which]
        pad[0:8, :] = jnp.zeros((8, LANES), F32)
        pad[8:8 + ls, :] = src_ref[0].astype(F32)
        pad[8 + ls:16 + ls, :] = jnp.zeros((8, LANES), F32)
        step = min(rb, ls)
        for r0 in range(0, ls, step):
            acc = jnp.zeros((step, LANES), F32)
            for t in range(GDN_CONV):
                s0 = r0 + 8 - GDN_CONV // 2 + t
                acc = acc + cw_ref[t:t + 1, :] * pad[s0:s0 + step, :]
            y = acc * _sigmoid(acc)
            if kind == "q":
                y = l2n(y) * (GDN_DK ** -0.5)
            elif kind == "k":
                y = l2n(y)
            dst[off + r0:off + r0 + step, :] = y
            yield

    def conv_chains(q_ref, k_ref, v_ref, off, ls):
        return [conv_chain(q_ref, cwq_ref, qs, off, ls, "q", 0), conv_chain(k_ref, cwk_ref, ks, off, ls, "k", 1),
                conv_chain(v_ref, cwv_ref, vs, off, ls, "v", 2)]

    def lockstep(chains):
        chains = list(chains)
        while chains:
            alive = []
            for ch in chains:
                try:
                    next(ch)
                    alive.append(ch)
                except StopIteration:
                    pass
            chains = alive

    lockstep(conv_chains(qc_ref, kc_ref, vc_ref, 0, lc))

    head = pl.program_id(1)

    def build_tables(src_ref, off, ls):
        step = min(rb, ls)
        for r0 in range(0, ls, step):
            picked = _lane_picks(src_ref[0, r0:r0 + step, :], [8 * d + head for d in range(2)]
                                 + [16 + 8 * d + head for d in range(2)])
            for d in range(2):
                beta_t[d, off + r0:off + r0 + step, :] = picked[d]
                cg_t[d, off + r0:off + r0 + step, :] = picked[2 + d]

    build_tables(gdc_ref, 0, lc)
    build_tables(gdl_ref, lc, ll)

    def bwd_chunk(t):
        return jnp.where(t < n_c, n_c - 1 - t, n_t + n_c - 1 - t)

    row_p = lax.broadcasted_iota(jnp.int32, (CHUNK, LANES), 0)
    lane_p = lax.broadcasted_iota(jnp.int32, (CHUNK, LANES), 1)
    fwd_p = lane_p < CHUNK
    col_p = jnp.where(fwd_p, lane_p, lane_p - CHUNK)
    signed = jnp.where(fwd_p, row_p - col_p, col_p - row_p)
    incl_p = signed >= 0
    strict_p = signed > 0
    eye_p = (row_p == col_p).astype(F32)
    keep_f = fwd_p.astype(BF16)
    keep_b = (1.0 - fwd_p.astype(F32)).astype(BF16)

    def block_diag(top, bottom):
        zero = jnp.zeros(top.shape, top.dtype)
        return jnp.concatenate([jnp.concatenate([top, zero], axis=1), jnp.concatenate([zero, bottom], axis=1)], axis=0)

    def pair_diag(xp):
        return jnp.concatenate([xp * keep_f, xp * keep_b], axis=0)

    def pair_times3(a, xp):
        ah = a.astype(BF16)
        al = (a - ah.astype(F32)).astype(BF16)
        xh = xp.astype(BF16)
        xl = (xp - xh.astype(F32)).astype(BF16)
        dh = pair_diag(xh)
        return _dot(ah, dh) + (_dot(ah, pair_diag(xl)) + _dot(al, dh))

    def prep_chain(t):
        tc = jnp.minimum(t, n_t - 1)
        q, k, v, beta, cgc, kb, ecg = [], [], [], [], [], [], []
        for d in range(2):
            c = tc if d == 0 else bwd_chunk(tc)
            rows = pl.ds(pl.multiple_of(c * CHUNK, CHUNK), CHUNK)
            q.append(qs[rows, :])
            k.append(ks[rows, :])
            v.append(vs[rows, :])
            beta.append(beta_t[d, rows, :])
            cgc.append(cg_t[d, rows, :])
        kdiag = block_diag(k[0].astype(BF16), k[1].astype(BF16))
        kk = _dot_nt(jnp.concatenate([k[0], k[1]], axis=1).astype(BF16), kdiag)
        qk = _dot_nt(jnp.concatenate([q[0], q[1]], axis=1).astype(BF16), kdiag)
        yield
        beta_p = jnp.where(fwd_p, beta[0], beta[1])
        cg_p = jnp.where(fwd_p, cgc[0], cgc[1])
        cgr_p = jnp.transpose(jnp.concatenate([cgc[0], cgc[1]], axis=0))[0:CHUNK, :]
        decay = jnp.exp(jnp.where(incl_p, cg_p - cgr_p, -jnp.inf))
        qkd = qk * decay
        for d in range(2):
            slot = (t % RING) * 2 + d
            last = CHUNK - 1 if d == 0 else 0
            cg_last = cgc[d][last:last + 1, :]
            ecg.append(jnp.exp(cgc[d]))
            kb.append(k[d] * beta[d])
            qk_r[slot] = qkd[:, d * CHUNK:(d + 1) * CHUNK]
            wq_r[slot, CHUNK:2 * CHUNK, :] = q[d] * ecg[d]
            kd_r[slot] = k[d] * jnp.exp(cg_last - cgc[d])
            dc_r[slot] = jnp.broadcast_to(jnp.exp(cg_last), (8, LANES))
        x = jnp.where(strict_p, -(beta_p * kk) * decay, 0.0)
        tinv = eye_p + x
        x = pair_times3(x, x)
        yield
        for _ in range(4):
            both = pair_times3(jnp.concatenate([tinv, x], axis=0), x)
            tinv, x = tinv + both[0:CHUNK, :], both[CHUNK:2 * CHUNK, :]
            yield
        tinv = (tinv + pair_times3(tinv, x)).astype(BF16)
        yield
        w = _dot(tinv, block_diag((kb[0] * ecg[0]).astype(BF16), (kb[1] * ecg[1]).astype(BF16)))
        u = _dot(tinv, block_diag((v[0] * beta[0]).astype(BF16), (v[1] * beta[1]).astype(BF16)))
        for d in range(2):
            slot = (t % RING) * 2 + d
            wq_r[slot, 0:CHUNK, :] = w[:, d * LANES:(d + 1) * LANES]
            u_r[slot] = u[:, d * LANES:(d + 1) * LANES]

    out_s[...] = jnp.zeros(out_s.shape, F32)

    def scan_chain(d, t0, steps, s, with_out, result):
        for j in range(steps):
            t = t0 + j
            slot = (t % RING) * 2 + d
            ws = _dot(wq_r[slot], s)
            yield
            v_new = u_r[slot] - ws[0:CHUNK, :]
            if with_out:
                c = t if d == 0 else bwd_chunk(t)
                o = ws[CHUNK:2 * CHUNK, :] + _dot(qk_r[slot], v_new)
                l0 = pl.multiple_of((c - n_c) * CHUNK, CHUNK)
                out_s[pl.ds(l0, CHUNK), :] += o
            s = s * dc_r[slot][0:1, :] + _dot_tn(kd_r[slot], v_new)
            yield
        result[d] = s

    def group_body(i, carry, t_base, steps, ahead, with_out, beside=()):
        t0 = t_base + steps * i
        result = [None, None]
        lockstep([scan_chain(d, t0, steps, carry[d], with_out, result) for d in range(2)]
                 + [prep_chain(t0 + steps + j) for j in range(ahead)] + list(beside))
        return result[0], result[1]

    def run_groups(carry, t_base, n_steps, steps, ahead_last, with_out, beside_last=()):
        n_groups = n_steps // steps
        carry = lax.fori_loop(0, n_groups - 1, functools.partial(
            group_body, t_base=t_base, steps=steps, ahead=steps, with_out=with_out), carry)
        return group_body(n_groups - 1, carry, t_base, steps, ahead_last, with_out, beside_last)

    def out_chain(blocks):
        for j in blocks:
            o = out_s[j * rb:(j + 1) * rb, :]
            z = z_ref[0, j * rb:(j + 1) * rb, :].astype(F32)
            y = o * lax.rsqrt(jnp.mean(o * o, axis=-1, keepdims=True) + EPS) * ng_ref[...]
            y_ref[0, j * rb:(j + 1) * rb, :] = (y * (z * _sigmoid(z))).astype(y_ref.dtype)
            yield

    per_block = rb // CHUNK
    n_blocks = ll // rb
    early = [j for j in range(n_blocks)
             if n_l > GDN_STEPS and j * per_block >= GDN_STEPS and (j + 1) * per_block <= n_l - GDN_STEPS]
    late = [j for j in range(n_blocks) if j not in early]

    lockstep([prep_chain(j) for j in range(GDN_CTX_STEPS)] + conv_chains(ql_ref, kl_ref, vl_ref, lc, ll))
    s0 = jnp.zeros((GDN_DK, HEAD_V), F32)
    carry = run_groups((s0, s0), 0, n_c, GDN_CTX_STEPS, GDN_STEPS, False)
    run_groups(carry, n_c, n_l, GDN_STEPS, 0, True, beside_last=[out_chain(early)])
    lockstep([out_chain(late)])


def _gdn(proj_c, proj_l, conv_w8, gd_c, gd_l, norm_g, b0, nb):
    lc = proj_c.shape[1]
    ll = proj_l.shape[1]
    lt = lc + ll
    qb, kb_, vb, zb = (COL_GDN_QKV // LANES, COL_GDN_QKV // LANES + HEADS, COL_GDN_QKV // LANES + 2 * HEADS,
                       COL_GDN_Z // LANES)

    def seq_spec(l, col0):
        return pl.BlockSpec((1, l, LANES), lambda i, h: (i + b0, 0, col0 + h))

    def cw_spec(col0):
        return pl.BlockSpec((8, LANES), lambda i, h: (0, col0 + h))

    return pl.pallas_call(
        _gdn_kernel,
        grid=(nb, HEADS),
        in_specs=[seq_spec(lc, qb), seq_spec(lc, kb_), seq_spec(lc, vb),
                  seq_spec(ll, qb), seq_spec(ll, kb_), seq_spec(ll, vb), seq_spec(ll, zb),
                  cw_spec(0), cw_spec(HEADS), cw_spec(2 * HEADS),
                  pl.BlockSpec((1, lc, LANES), lambda i, h: (i + b0, 0, 0)),
                  pl.BlockSpec((1, ll, LANES), lambda i, h: (i + b0, 0, 0)),
                  pl.BlockSpec((1, LANES), lambda i, h: (0, 0))],
        out_specs=pl.BlockSpec((1, ll, LANES), lambda i, h: (i, 0, h)),
        out_shape=jax.ShapeDtypeStruct((nb, ll, HEADS * HEAD_V), BF16),
        scratch_shapes=[pltpu.VMEM((3, max(lc, ll) + 16, LANES), F32),
                        pltpu.VMEM((lt, LANES), F32), pltpu.VMEM((lt, LANES), F32), pltpu.VMEM((lt, LANES), F32),
                        pltpu.VMEM((2, lt, LANES), F32), pltpu.VMEM((2, lt, LANES), F32),
                        pltpu.VMEM((2 * RING, 2 * CHUNK, LANES), F32),
                        pltpu.VMEM((2 * RING, CHUNK, LANES), F32), pltpu.VMEM((2 * RING, CHUNK, LANES), F32),
                        pltpu.VMEM((2 * RING, CHUNK, CHUNK), F32),
                        pltpu.VMEM((2 * RING, 8, LANES), F32),
                        pltpu.VMEM((ll, LANES), F32)],
        compiler_params=_params(("arbitrary", "arbitrary")),
    )(proj_c, proj_c, proj_c, proj_l, proj_l, proj_l, proj_l, conv_w8, conv_w8, conv_w8,
      gd_c, gd_l, norm_g.reshape(1, LANES))


def _mlstm_kernel(qc_ref, kc_ref, vc_ref, ql_ref, kl_ref, vl_ref, mlc_ref, mll_ref, h_ref, out_s, tabs):
    lc = qc_ref.shape[1]
    ll = ql_ref.shape[1]
    n_c, n_l = lc // CHUNK, ll // CHUNK
    n_t = n_c + n_l
    lt = lc + ll
    pair = pl.program_id(1)
    lane = lax.broadcasted_iota(jnp.int32, (CHUNK, LANES), 1)
    ones_v = jnp.ones((CHUNK, HEAD_V), BF16)
    chains = [(hh, d) for hh in range(2) for d in range(2)]
    hmask = [((lane // ML_DK) == hh).astype(F32) for hh in range(2)]
    incl = [_dir_masks(d)[0] for d in range(2)]

    def build_tables(src_ref, off, ls):
        step = min(256, ls)
        for r0 in range(0, ls, step):
            lanes = [16 * j + 8 * d + 2 * pair + hh for hh, d in chains for j in range(3)]
            for g, tab in enumerate(_lane_picks(src_ref[0, r0:r0 + step, :], lanes, n_by_matmul=5)):
                tabs[g, off + r0:off + r0 + step, :] = tab

    build_tables(mlc_ref, 0, lc)
    build_tables(mll_ref, lc, ll)

    def wide(a):
        return jnp.concatenate([a, a], axis=1)

    def chain(hh, d, c, get_state, is_ctx, result):
        last = CHUNK - 1 if d == 0 else 0
        if is_ctx:
            rows = pl.ds(pl.multiple_of(c * CHUNK, CHUNK), CHUNK)
            q_ref, k_ref, v_ref = qc_ref, kc_ref, vc_ref
        else:
            rows = pl.ds(pl.multiple_of((c - n_c) * CHUNK, CHUNK), CHUNK)
            q_ref, k_ref, v_ref = ql_ref, kl_ref, vl_ref
        q = (q_ref[0, rows, :].astype(F32) * hmask[hh]).astype(BF16)
        k = k_ref[0, rows, :].astype(F32) * (hmask[hh] * (ML_DK ** -0.5))
        v = jnp.concatenate([v_ref[0, rows, hh * HEAD_V:(hh + 1) * HEAD_V], ones_v], axis=1)
        n = chains.index((hh, d))
        trows = pl.ds(pl.multiple_of(c * CHUNK, CHUNK), CHUNK)
        gmb = tabs[3 * n, trows, :]
        gmb_t = jnp.transpose(gmb)[0:CHUNK, :]
        cm = tabs[3 * n + 1, trows, :]
        bc = tabs[3 * n + 2, trows, :]
        qk = _dot_nt(q, k.astype(BF16))
        cm_last = cm[last:last + 1, :]
        b_last = bc[last:last + 1, :]
        wk = (k * jnp.exp(gmb - cm_last)).astype(BF16)
        c_loc = _dot_tn(wk, v)
        yield
        while get_state() is None:
            yield
        cs, ms = get_state()
        mm = jnp.maximum(ms, cm)
        p = jnp.where(incl[d], jnp.exp(gmb_t - mm[:, 0:CHUNK]), 0.0) * qk
        inter = _dot(q, cs.astype(BF16))
        intra = _dot(p.astype(BF16), v)
        yield
        if not is_ctx:
            nd = wide(jnp.exp(ms - mm)) * inter + intra
            hv = nd[:, 0:HEAD_V] / jnp.maximum(jnp.abs(nd[:, HEAD_V:2 * HEAD_V]), jnp.exp(-(bc + mm)))
            l0 = pl.multiple_of((c - n_c) * CHUNK, CHUNK)
            out_s[pl.ds(l0, CHUNK), hh * HEAD_V:(hh + 1) * HEAD_V] += hv
        mx = jnp.maximum(ms, cm_last)
        result[hh, d] = (wide(jnp.exp(ms - mx)) * cs + wide(jnp.exp(cm_last - mx)) * c_loc, b_last + mx)

    out_s[...] = jnp.zeros(out_s.shape, F32)

    def run(chains_iter):
        live = list(chains_iter)
        while live:
            alive = []
            for ch in live:
                try:
                    next(ch)
                    alive.append(ch)
                except StopIteration:
                    pass
            live = alive

    def body(i, carry, is_ctx):
        results = [{} for _ in range(ML_STEPS)]
        gens = []
        for j in range(ML_STEPS):
            t = ML_STEPS * i + j
            for n, (hh, d) in enumerate(chains):
                if is_ctx:
                    c = t if d == 0 else n_c - 1 - t
                else:
                    c = n_c + t if d == 0 else n_t - 1 - t
                if j == 0:
                    get_state = functools.partial(lambda n: carry[n], n)
                else:
                    get_state = functools.partial(results[j - 1].get, (hh, d))
                gens.append(chain(hh, d, c, get_state, is_ctx, results[j]))
        run(gens)
        return tuple(results[-1][hd] for hd in chains)

    st0 = (jnp.zeros((LANES, 2 * HEAD_V), F32), jnp.zeros((1, LANES), F32))
    carry = lax.fori_loop(0, n_c // ML_STEPS, functools.partial(body, is_ctx=True), (st0,) * 4)
    lax.fori_loop(0, n_l // ML_STEPS, functools.partial(body, is_ctx=False), carry)
    h_ref[0] = out_s[...].astype(h_ref.dtype)


def _mlstm(proj_c, q_l, k_l, v_l, ml_c, ml_l, b0):
    lc = proj_c.shape[1]
    nb, ll, _ = q_l.shape
    lt = lc + ll
    qb, kb_, vb = COL_ML_Q // LANES, COL_ML_K // LANES, COL_ML_V // (2 * HEAD_V)
    return pl.pallas_call(
        _mlstm_kernel,
        grid=(nb, HEADS // 2),
        in_specs=[pl.BlockSpec((1, lc, LANES), lambda i, p: (i + b0, 0, qb + p)),
                  pl.BlockSpec((1, lc, LANES), lambda i, p: (i + b0, 0, kb_ + p)),
                  pl.BlockSpec((1, lc, 2 * HEAD_V), lambda i, p: (i + b0, 0, vb + p)),
                  pl.BlockSpec((1, ll, LANES), lambda i, p: (i, 0, p)),
                  pl.BlockSpec((1, ll, LANES), lambda i, p: (i, 0, p)),
                  pl.BlockSpec((1, ll, 2 * HEAD_V), lambda i, p: (i, 0, p)),
                  pl.BlockSpec((1, lc, LANES), lambda i, p: (i + b0, 0, 0)),
                  pl.BlockSpec((1, ll, LANES), lambda i, p: (i + b0, 0, 0))],
        out_specs=pl.BlockSpec((1, ll, 2 * HEAD_V), lambda i, p: (i, 0, p)),
        out_shape=jax.ShapeDtypeStruct((nb, ll, HEADS * HEAD_V), BF16),
        scratch_shapes=[pltpu.VMEM((ll, 2 * HEAD_V), F32), pltpu.VMEM((12, lt, LANES), F32)],
        compiler_params=_params(("arbitrary", "arbitrary")),
    )(proj_c, proj_c, proj_c, q_l, k_l, v_l, ml_c, ml_l)


def _merge_kernel(yg_ref, hm_ref, o_ref, gg_ref, gm_ref, x_ref, mod_ref, mlg_ref, n2_ref,
                  wbg_ref, wbm_ref, wo_ref, wrh_ref, wrl_ref, x1_ref, h2_ref, sc_ref):
    d = x_ref.shape[1]
    o = o_ref[...].astype(F32)
    ym = _sigmoid(o) * hm_ref[...].astype(F32)
    segs = []
    for h in range(HEADS):
        seg = ym[:, h * HEAD_V:(h + 1) * HEAD_V]
        segs.append(seg * lax.rsqrt(jnp.mean(seg * seg, axis=-1, keepdims=True) + EPS))
    ymn = jnp.concatenate(segs, axis=1) * mlg_ref[...]
    y_gdn = _dot(yg_ref[...], wbg_ref[...])
    y_ml = _dot(ymn.astype(BF16), wbm_ref[...])
    mixed = _sigmoid(gg_ref[...].astype(F32)) * y_gdn + _sigmoid(gm_ref[...].astype(F32)) * y_ml
    y = _dot(mixed.astype(BF16), wo_ref[...])
    x1 = x_ref[...] + mod_ref[0, :, 2 * d:3 * d] * y
    x1_ref[...] = x1
    hn = x1 * lax.rsqrt(jnp.mean(x1 * x1, axis=-1, keepdims=True) + EPS) * n2_ref[...]
    h2 = hn * (1.0 + mod_ref[0, :, 4 * d:5 * d]) + mod_ref[0, :, 3 * d:4 * d]
    h2_hi = h2.astype(BF16)
    h2_ref[...] = h2_hi
    h2_lo = (h2 - h2_hi.astype(F32)).astype(BF16)
    logits = _dot_nt(wrh_ref[...], h2_hi) + (_dot_nt(wrl_ref[...], h2_hi) + _dot_nt(wrh_ref[...], h2_lo))
    sc_ref[...] = _sigmoid(logits)


def _merge(y_gdn, h_ml, proj_l2d, x2d, mod3, rows_per_mod, ml_norm_g, norm2_g, wbg, wbm, wo, wr_hi, wr_lo, tok0,
           tm=512):
    t, d = y_gdn.shape
    e = wr_hi.shape[0]
    off = tok0 // tm
    row = lambda i: (i, 0)
    const = lambda i: (0, 0)
    return pl.pallas_call(
        _merge_kernel,
        grid=(t // tm,),
        in_specs=[pl.BlockSpec((tm, d), row), pl.BlockSpec((tm, d), row),
                  pl.BlockSpec((tm, d), lambda i: (i + off, COL_ML_O // d)),
                  pl.BlockSpec((tm, d), lambda i: (i + off, COL_MG_GDN // d)),
                  pl.BlockSpec((tm, d), lambda i: (i + off, COL_MG_ML // d)),
                  pl.BlockSpec((tm, d), lambda i: (i + off, 0)),
                  pl.BlockSpec((1, 1, mod3.shape[2]), lambda i: (((i + off) * tm) // rows_per_mod, 0, 0)),
                  pl.BlockSpec((1, d), const), pl.BlockSpec((1, d), const),
                  pl.BlockSpec((d, d), const), pl.BlockSpec((d, d), const), pl.BlockSpec((d, d), const),
                  pl.BlockSpec((e, d), const), pl.BlockSpec((e, d), const)],
        out_specs=[pl.BlockSpec((tm, d), row), pl.BlockSpec((tm, d), lambda i: (i + off, 0)),
                   pl.BlockSpec((e, tm), lambda i: (0, i))],
        out_shape=[jax.ShapeDtypeStruct((t, d), F32), jax.ShapeDtypeStruct((x2d.shape[0], d), BF16),
                   jax.ShapeDtypeStruct((e, t), F32)],
        compiler_params=_params(("arbitrary",)),
    )(y_gdn, h_ml, proj_l2d, proj_l2d, proj_l2d, x2d, mod3, ml_norm_g.reshape(1, d), norm2_g.reshape(1, d),
      wbg, wbm, wo, wr_hi, wr_lo)


def _expert_kernel(be_ref, nu_ref, nx_ref, x_ref, wgu_hbm, wd_hbm, y_ref, land_gu, land_d, wgu_s, wd_s, sems):
    i = pl.program_id(0)
    de = wd_hbm.shape[1]
    used = i < nu_ref[0]
    first = jnp.logical_or(i == 0, be_ref[i] != be_ref[jnp.maximum(i - 1, 0)])

    def weight_copies(ex):
        return (pltpu.make_async_copy(wgu_hbm.at[ex], land_gu, sems.at[0]),
                pltpu.make_async_copy(wd_hbm.at[ex], land_d, sems.at[1]))

    @pl.when(i == 0)
    def _():
        for cp in weight_copies(be_ref[0]):
            cp.start()

    @pl.when(jnp.logical_and(first, used))
    def _():
        for cp in weight_copies(be_ref[i]):
            cp.wait()
        wgu_s[...] = land_gu[...].astype(BF16)
        wd_s[...] = land_d[...].astype(BF16)

        @pl.when(nx_ref[i] >= 0)
        def _():
            for cp in weight_copies(nx_ref[i]):
                cp.start()

    @pl.when(used)
    def _():
        gu = _dot(x_ref[...], wgu_s[...])
        g = gu[:, 0:de]
        act = (g * _sigmoid(g)) * gu[:, de:2 * de]
        y_ref[...] = _dot(act.astype(BF16), wd_s[...]).astype(y_ref.dtype)

    @pl.when(jnp.logical_not(used))
    def _():
        y_ref[...] = jnp.zeros(y_ref.shape, y_ref.dtype)


def _experts(xb, blk_expert, n_used, next_expert, w_gu, w_down):
    n_slots, d = xb.shape
    n_blocks = n_slots // EXPERT_ROWS
    e, _, de2 = w_gu.shape
    de = de2 // 2
    return pl.pallas_call(
        _expert_kernel,
        grid_spec=pltpu.PrefetchScalarGridSpec(
            num_scalar_prefetch=3,
            grid=(n_blocks,),
            in_specs=[pl.BlockSpec((EXPERT_ROWS, d), lambda i, be, nu, nx: (i, 0)),
                      pl.BlockSpec(memory_space=pl.ANY), pl.BlockSpec(memory_space=pl.ANY)],
            out_specs=pl.BlockSpec((EXPERT_ROWS, d), lambda i, be, nu, nx: (i, 0)),
            scratch_shapes=[pltpu.VMEM((d, de2), w_gu.dtype), pltpu.VMEM((de, d), w_down.dtype),
                            pltpu.VMEM((d, de2), BF16), pltpu.VMEM((de, d), BF16),
                            pltpu.SemaphoreType.DMA((2,))]),
        out_shape=jax.ShapeDtypeStruct((n_slots, d), BF16),
        compiler_params=_params(("arbitrary",)),
    )(blk_expert, n_used, next_expert, xb, w_gu, w_down)


def _final_kernel(x1_ref, h2_ref, yg_ref, wt_ref, mod_ref, wsg_ref, wsd_ref, fg_ref, o_ref):
    d = x1_ref.shape[1]
    ds_ = wsd_ref.shape[0]
    wt = wt_ref[...]
    routed = jnp.zeros(x1_ref.shape, F32)
    for k in range(TOP_K):
        routed = routed + wt[:, k:k + 1] * yg_ref[k].astype(F32)
    gu = _dot(h2_ref[...], wsg_ref[...])
    g = gu[:, 0:ds_]
    sh = _dot(((g * _sigmoid(g)) * gu[:, ds_:2 * ds_]).astype(BF16), wsd_ref[...])
    x2 = x1_ref[...] + mod_ref[0, :, 5 * d:6 * d] * (routed + sh)
    o_ref[...] = x2 * lax.rsqrt(jnp.mean(x2 * x2, axis=-1, keepdims=True) + EPS) * fg_ref[...]


def _final(x1, h2, yg, wts, mod3, rows_per_mod, w_sh_gu, w_sh_down, final_g, tok0, tm=256):
    tp, d = x1.shape
    off = tok0 // tm
    row = lambda i: (i, 0)
    const = lambda i: (0, 0)
    return pl.pallas_call(
        _final_kernel,
        grid=(tp // tm,),
        in_specs=[pl.BlockSpec((tm, d), row), pl.BlockSpec((tm, d), lambda i: (i + off, 0)),
                  pl.BlockSpec((TOP_K, tm, d), lambda i: (0, i, 0)), pl.BlockSpec((tm, TOP_K), row),
                  pl.BlockSpec((1, 1, mod3.shape[2]), lambda i: (((i + off) * tm) // rows_per_mod, 0, 0)),
                  pl.BlockSpec(w_sh_gu.shape, const), pl.BlockSpec(w_sh_down.shape, const),
                  pl.BlockSpec((1, d), const)],
        out_specs=pl.BlockSpec((tm, d), row),
        out_shape=jax.ShapeDtypeStruct((tp, d), F32),
        compiler_params=_params(("arbitrary",)),
    )(x1, h2, yg, wts, mod3, w_sh_gu, w_sh_down, final_g.reshape(1, d))


def _route_kernel(sc_ref, bias_ref, tri_ref, idx_ref, wt_ref, rk_ref, cnt_ref, base_s):
    @pl.when(pl.program_id(0) == 0)
    def _():
        base_s[...] = jnp.zeros(base_s.shape, F32)

    scores = sc_ref[...]
    e, tn = scores.shape
    gsz = e // N_GROUPS
    sel3 = (scores + bias_ref[...]).reshape(N_GROUPS, gsz, tn)
    m1 = jnp.max(sel3, axis=1)
    is_max = sel3 == m1[:, None, :]
    n_max = jnp.sum(is_max.astype(F32), axis=1)
    m2 = jnp.max(jnp.where(is_max, -jnp.inf, sel3), axis=1)
    grp = m1 + jnp.where(n_max >= 2.0, m1, m2)
    gi = lax.broadcasted_iota(jnp.int32, (N_GROUPS, tn), 0)
    ahead = jnp.zeros((N_GROUPS, tn), F32)
    for g in range(N_GROUPS):
        row = grp[g:g + 1, :]
        ahead = ahead + jnp.logical_or(row > grp, jnp.logical_and(row == grp, g < gi)).astype(F32)
    ahead3 = jnp.broadcast_to(ahead[:, None, :], (N_GROUPS, gsz, tn))
    selm = jnp.where(ahead3 < float(TOPK_GROUPS), sel3, -jnp.inf).reshape(e, tn)
    ri = lax.broadcasted_iota(jnp.int32, (e, tn), 0).astype(F32)
    member = jnp.zeros((e, tn), F32)
    idxs, ws = [], []
    for _ in range(TOP_K):
        m = jnp.max(selm, axis=0, keepdims=True)
        idx = jnp.min(jnp.where(selm == m, ri, float(e)), axis=0, keepdims=True)
        hit = ri == idx
        ws.append(jnp.sum(jnp.where(hit, scores, 0.0), axis=0, keepdims=True))
        idxs.append(idx)
        selm = jnp.where(hit, -jnp.inf, selm)
        member = jnp.where(hit, 1.0, member)
    w = jnp.concatenate(ws, axis=0)
    wt_ref[...] = w / jnp.sum(w, axis=0, keepdims=True) * ROUTED_SCALE
    idx_ref[...] = jnp.concatenate(idxs, axis=0).astype(jnp.int32)
    cum = _dot(member.astype(BF16), tri_ref[...]) + base_s[...]
    rk_ref[...] = jnp.concatenate(
        [jnp.sum(jnp.where(ri == idx, cum, 0.0), axis=0, keepdims=True) for idx in idxs], axis=0).astype(jnp.int32)
    total = base_s[...] + jnp.sum(member, axis=1, keepdims=True)
    base_s[...] = total
    cnt_ref[...] = total


def _route(scores_t, router_bias):
    e, t = scores_t.shape
    tn = LANES
    bias = jnp.broadcast_to(router_bias.astype(F32)[:, None], (e, tn))
    tri = (jnp.arange(tn)[:, None] < jnp.arange(tn)[None, :]).astype(BF16)
    tok = pl.BlockSpec((TOP_K, tn), lambda i: (0, i))
    const = lambda i: (0, 0)
    return pl.pallas_call(
        _route_kernel,
        grid=(t // tn,),
        in_specs=[pl.BlockSpec((e, tn), lambda i: (0, i)), pl.BlockSpec((e, tn), const),
                  pl.BlockSpec((tn, tn), const)],
        out_specs=[tok, tok, tok, pl.BlockSpec((e, tn), const)],
        out_shape=[jax.ShapeDtypeStruct((TOP_K, t), jnp.int32), jax.ShapeDtypeStruct((TOP_K, t), F32),
                   jax.ShapeDtypeStruct((TOP_K, t), jnp.int32), jax.ShapeDtypeStruct((e, tn), F32)],
        scratch_shapes=[pltpu.VMEM((e, tn), F32)],
        compiler_params=_params(("arbitrary",)),
    )(scores_t, bias, tri)


def _slot_kernel(idx_ref, rk_ref, ps_ref, pos_ref):
    e, tn = ps_ref.shape
    ri = lax.broadcasted_iota(jnp.int32, (e, tn), 0)
    ps = ps_ref[...]
    rows = [jnp.sum(jnp.where(ri == idx_ref[k:k + 1, :], ps, 0.0), axis=0, keepdims=True) for k in range(TOP_K)]
    pos_ref[...] = rk_ref[...] + jnp.concatenate(rows, axis=0).astype(jnp.int32)


def _slots(idx, rank, pstart):
    k, t = idx.shape
    e = pstart.shape[0]
    tn = LANES
    tok = pl.BlockSpec((k, tn), lambda i: (0, i))
    return pl.pallas_call(
        _slot_kernel,
        grid=(t // tn,),
        in_specs=[tok, tok, pl.BlockSpec((e, tn), lambda i: (0, 0))],
        out_specs=tok,
        out_shape=jax.ShapeDtypeStruct((k, t), jnp.int32),
        compiler_params=_params(("arbitrary",)),
    )(idx, rank, jnp.broadcast_to(pstart.astype(F32)[:, None], (e, tn)))


def _sc_scatter_rows(vals, idx, n_rows, window=LANES):
    n, width = vals.shape
    mesh = plsc.VectorSubcoreMesh(core_axis_name="core", subcore_axis_name="subcore")

    @pl.kernel(out_type=jax.ShapeDtypeStruct((n_rows, width), vals.dtype), mesh=mesh, scratch_types=[])
    def scatter(v_hbm, i_hbm, o_hbm):
        def body(v_vmem, i_vmem):
            pltpu.sync_copy(v_vmem, o_hbm.at[i_vmem.at[0]])

        pltpu.emit_pipeline(
            body,
            grid=(n // window,),
            in_specs=[pl.BlockSpec((window, width), lambda i: (i, 0)),
                      pl.BlockSpec((1, window), lambda i: (0, i))],
            out_specs=[],
            core_axis_name=("core", "subcore"),
            dimension_semantics=(pltpu.PARALLEL,),
        )(v_hbm, i_hbm)

    return scatter(vals, idx.reshape(1, n))


def _block_table(counts, n_blocks):
    padded = (counts + EXPERT_ROWS - 1) // EXPERT_ROWS * EXPERT_ROWS
    pend = jnp.cumsum(padded)
    first_slot = jnp.arange(n_blocks, dtype=jnp.int32) * EXPERT_ROWS
    blk_expert = jnp.minimum(jnp.sum((pend[None, :] <= first_slot[:, None]).astype(jnp.int32), axis=1),
                             N_EXPERTS - 1)
    e_ids = jnp.arange(N_EXPERTS, dtype=jnp.int32)
    later = lax.cummin(jnp.where(counts > 0, e_ids, N_EXPERTS)[::-1])[::-1]
    next_tab = jnp.concatenate([later[1:], jnp.full((1,), N_EXPERTS, jnp.int32)])
    next_tab = jnp.where(next_tab < N_EXPERTS, next_tab, -1)
    return (pend - padded, blk_expert, (pend[-1] // EXPERT_ROWS).astype(jnp.int32).reshape(1),
            next_tab[blk_expert].astype(jnp.int32))


def _col_major(t):
    b, l, f = t.shape
    rows = l // GRID_W
    return t.reshape(b, rows, GRID_W, f).transpose(0, 2, 1, 3).reshape(b, l, f)


def _row_major(t):
    b, l, f = t.shape
    rows = l // GRID_W
    return t.reshape(b, GRID_W, rows, f).transpose(0, 2, 1, 3).reshape(b, l, f)


def kernel(x, c, ctx, c_ctx, w_ada, b_ada, norm1_g, norm2_g, w_in, gdn_conv_w, gdn_a_log, gdn_dt_bias, gdn_norm_g,
           ml_i_bias, ml_f_bias, ml_norm_g, w_branch_gdn, w_branch_ml, w_out, w_router, router_bias, w_exp_gate_up,
           w_exp_down, w_sh_gate_up, w_sh_down, final_norm_g):
    b, l, d = x.shape
    lc = ctx.shape[1]
    t = b * l
    layer = 0

    w = w_in[layer]
    main_cols = [_ORIG[k] for k in ("gdn_qkv", "ml_q", "ml_k", "ml_v", "gdn_z", "ml_o", "mg_gdn", "mg_ml")]
    w_main = jnp.concatenate([w[:, a:e] for a, e in main_cols], axis=1).astype(BF16)
    w_gate = jnp.concatenate([w[:, _ORIG["gdn_gate"][0]:_ORIG["gdn_gate"][1]],
                              w[:, _ORIG["ml_gate"][0]:_ORIG["ml_gate"][1]],
                              jnp.zeros((d, LANES - 64), F32)], axis=1).astype(BF16)
    zeros16 = jnp.zeros((16,), F32)
    gp_add = jnp.concatenate([zeros16, gdn_dt_bias[layer].reshape(-1), ml_i_bias[layer].reshape(-1),
                              ml_f_bias[layer].reshape(-1), jnp.zeros((LANES - 64,), F32)])
    gp_mul = jnp.concatenate([zeros16, -jnp.exp(gdn_a_log[layer].astype(F32)).reshape(-1),
                              jnp.zeros((LANES - 32,), F32)])
    gparams = jnp.zeros((8, LANES), F32).at[0].set(gp_add).at[1].set(gp_mul)
    conv_w8 = jnp.zeros((8, gdn_conv_w.shape[2]), F32).at[0:GDN_CONV].set(gdn_conv_w[layer])
    wr = w_router[layer].T
    wr_hi = wr.astype(BF16)
    wr_lo = (wr - wr_hi.astype(F32)).astype(BF16)

    n_mod_rows = -(-(b + 1) // 8) * 8
    cc = jnp.zeros((n_mod_rows, d), F32).at[0:b].set(c).at[b].set(c_ctx)
    mod = _ada_mod(cc, w_ada[layer], b_ada[layer])
    mod3 = mod.reshape(n_mod_rows, 1, 6 * d)

    x2d = x.reshape(t, d)
    tm_l = min(1024, l)
    proj_l, gate_l = _project(x2d, mod3, lambda i: (i * tm_l) // l, norm1_g[layer], w_main, w_gate, tm_l)
    tm_c = min(1024, b * lc)
    proj_c, gate_c = _project(ctx.reshape(b * lc, d), mod3, lambda i: b, norm1_g[layer], w_main[:, 0:N_SCAN],
                              w_gate, tm_c, tn=N_SCAN // 2)
    proj_l3 = proj_l.reshape(b, l, N_MAIN)
    proj_c3 = proj_c.reshape(b, lc, N_SCAN)

    gate_l_cm = _col_major(gate_l.reshape(b, l, LANES)).reshape(t, LANES)
    gd_c, ml_c = _gate_prep(gate_c, gparams, True, True)
    gd_l, = _gate_prep(gate_l, gparams, True, False)
    ml_l, = _gate_prep(gate_l_cm, gparams, False, True)

    nb = b // BATCH_PARTS
    tp = nb * l
    n_assign = tp * TOP_K
    n_blocks = (n_assign + N_EXPERTS * (EXPERT_ROWS - 1)) // EXPERT_ROWS + 1
    n_slots = n_blocks * EXPERT_ROWS
    gd_c3, gd_l3 = gd_c.reshape(b, lc, LANES), gd_l.reshape(b, l, LANES)
    ml_c3, ml_l3 = ml_c.reshape(b, lc, LANES), ml_l.reshape(b, l, LANES)
    wbg, wbm, wo = (w_branch_gdn[layer].astype(BF16), w_branch_ml[layer].astype(BF16), w_out[layer].astype(BF16))
    w_sh_gu, w_sh_dn = w_sh_gate_up[layer].astype(BF16), w_sh_down[layer].astype(BF16)
    outs = []
    for part in range(BATCH_PARTS):
        b0, tok0 = part * nb, part * tp
        y_gdn = _gdn(proj_c3, proj_l3, conv_w8, gd_c3, gd_l3, gdn_norm_g[layer], b0, nb)
        q_cm = _col_major(proj_l3[b0:b0 + nb, :, COL_ML_Q:COL_ML_Q + HEADS * ML_DK])
        k_cm = _col_major(proj_l3[b0:b0 + nb, :, COL_ML_K:COL_ML_K + HEADS * ML_DK])
        v_cm = _col_major(proj_l3[b0:b0 + nb, :, COL_ML_V:COL_ML_V + HEADS * HEAD_V])
        h_ml = _row_major(_mlstm(proj_c3, q_cm, k_cm, v_cm, ml_c3, ml_l3, b0))
        x1, h2, scores_t = _merge(y_gdn.reshape(tp, d), h_ml.reshape(tp, d), proj_l, x2d, mod3, l, ml_norm_g[layer],
                                norm2_g[layer], wbg, wbm, wo, wr_hi, wr_lo, tok0, tm=min(512, l))
        idx, wts, rank, cnt = _route(scores_t, router_bias[layer])
        counts = cnt[:, 0].astype(jnp.int32)
        pstart, blk_expert, n_used, next_expert = _block_table(counts, n_blocks)
        pos = _slots(idx, rank, pstart).reshape(n_assign)
        tok_ids = jnp.broadcast_to((jnp.arange(n_assign, dtype=jnp.int32) % tp)[:, None], (n_assign, LANES))
        scattered = _sc_scatter_rows(tok_ids, pos, n_slots)[:, 0]
        in_expert = (jnp.arange(n_slots, dtype=jnp.int32).reshape(n_blocks, EXPERT_ROWS)
                     - pstart[blk_expert][:, None])
        valid = (in_expert < counts[blk_expert][:, None]).reshape(n_slots)
        tok_slot = jnp.where(valid, scattered, jnp.arange(n_slots, dtype=jnp.int32) % tp)
        xb = h2.at[tok_slot + tok0].get(mode="promise_in_bounds")
        yb = _experts(xb, blk_expert, n_used, next_expert, w_exp_gate_up[layer], w_exp_down[layer])
        yg = yb.at[pos].get(mode="promise_in_bounds", unique_indices=True).reshape(TOP_K, tp, d)
        outs.append(_final(x1, h2, yg, wts.T, mod3, l, w_sh_gu, w_sh_dn, final_norm_g, tok0, tm=min(512, l)))
    return jnp.concatenate(outs, axis=0).reshape(b, l, d)
```

```python
import functools

import jax
import jax.numpy as jnp
from jax import lax
from jax.experimental import pallas as pl
from jax.experimental.pallas import tpu as pltpu
from jax.experimental.pallas import tpu_sc as plsc

F32 = jnp.float32
BF16 = jnp.bfloat16
HI = lax.Precision.HIGHEST

EPS = 1e-6
CHUNK = 64
GRID_W = 64
HEADS = 8
HEAD_V = 128
GDN_DK = 128
ML_DK = 64
GDN_CONV = 5
N_EXPERTS = 256
TOP_K = 8
N_GROUPS = 8
TOPK_GROUPS = 4
ROUTED_SCALE = 2.5
EXPERT_ROWS = 512
BATCH_PARTS = 1
GDN_STEPS = 4
GDN_CTX_STEPS = 4
ML_STEPS = 4
RING = 2 * GDN_STEPS
LANES = 128
VMEM_LIMIT = 56 * 1024 * 1024

COL_GDN_QKV = 0
COL_ML_Q = 3072
COL_ML_K = 3584
COL_ML_V = 4096
N_SCAN = 5120
COL_GDN_Z = 5120
COL_ML_O = 6144
COL_MG_GDN = 7168
COL_MG_ML = 8192
N_MAIN = 9216
_ORIG = dict(gdn_qkv=(0, 3072), gdn_z=(3072, 4096), gdn_gate=(4096, 4128), ml_q=(4128, 4640),
             ml_k=(4640, 5152), ml_v=(5152, 6176), ml_o=(6176, 7200), ml_gate=(7200, 7232),
             mg_gdn=(7232, 8256), mg_ml=(8256, 9280))


def _params(sem, vmem=VMEM_LIMIT):
    return pltpu.CompilerParams(dimension_semantics=sem, vmem_limit_bytes=vmem)


def _dot(a, b, precision=None):
    return jnp.dot(a, b, preferred_element_type=F32, precision=precision)


def _dot_nt(a, b, precision=None):
    return lax.dot_general(a, b, (((1,), (1,)), ((), ())), preferred_element_type=F32, precision=precision)


def _dot_tn(a, b, precision=None):
    return lax.dot_general(a, b, (((0,), (0,)), ((), ())), preferred_element_type=F32, precision=precision)


def _sigmoid(x):
    return 1.0 / (1.0 + jnp.exp(-x))


def _ada_kernel(c_ref, w_ref, b_ref, o_ref):
    c = c_ref[...]
    sc = c * _sigmoid(c)
    o_ref[...] = _dot(sc, w_ref[...], HI) + b_ref[...]


def _ada_mod(cc, w_ada, b_ada, tn=1536):
    r, d = cc.shape
    n = w_ada.shape[1]
    return pl.pallas_call(
        _ada_kernel,
        grid=(n // tn,),
        in_specs=[pl.BlockSpec((r, d), lambda j: (0, 0)),
                  pl.BlockSpec((d, tn), lambda j: (0, j)),
                  pl.BlockSpec((1, tn), lambda j: (0, j))],
        out_specs=pl.BlockSpec((r, tn), lambda j: (0, j)),
        out_shape=jax.ShapeDtypeStruct((r, n), F32),
        compiler_params=_params(("arbitrary",)),
    )(cc, w_ada, b_ada.reshape(1, n))


def _proj_kernel(x_ref, mod_ref, g_ref, w_ref, wg_ref, o_ref, og_ref, hn_ref):
    d = x_ref.shape[1]

    @pl.when(pl.program_id(1) == 0)
    def _():
        x = x_ref[...]
        y = x * lax.rsqrt(jnp.mean(x * x, axis=-1, keepdims=True) + EPS) * g_ref[...]
        shift = mod_ref[0, :, 0:d]
        scale = mod_ref[0, :, d:2 * d]
        h = (y * (1.0 + scale) + shift).astype(BF16)
        hn_ref[...] = h
        og_ref[...] = _dot(h, wg_ref[...])

    o_ref[...] = _dot(hn_ref[...], w_ref[...]).astype(o_ref.dtype)


def _project(x2d, mod3, mod_row_of_tile, norm_g, w_main, w_gate, tm, tn=2304):
    t, d = x2d.shape
    n = w_main.shape[1]
    return pl.pallas_call(
        _proj_kernel,
        grid=(t // tm, n // tn),
        in_specs=[pl.BlockSpec((tm, d), lambda i, j: (i, 0)),
                  pl.BlockSpec((1, 1, mod3.shape[2]), lambda i, j: (mod_row_of_tile(i), 0, 0)),
                  pl.BlockSpec((1, d), lambda i, j: (0, 0)),
                  pl.BlockSpec((d, tn), lambda i, j: (0, j)),
                  pl.BlockSpec((d, LANES), lambda i, j: (0, 0))],
        out_specs=[pl.BlockSpec((tm, tn), lambda i, j: (i, j)),
                   pl.BlockSpec((tm, LANES), lambda i, j: (i, 0))],
        out_shape=[jax.ShapeDtypeStruct((t, n), BF16), jax.ShapeDtypeStruct((t, LANES), F32)],
        scratch_shapes=[pltpu.VMEM((tm, d), BF16)],
        compiler_params=_params(("arbitrary", "arbitrary")),
    )(x2d, mod3, norm_g.reshape(1, d), w_main, w_gate)


def _gate_kernel(g_ref, p_ref, *out_refs, want_gd, want_ml):
    gd_ref = out_refs[0] if want_gd else None
    ml_ref = out_refs[-1] if want_ml else None
    rows = g_ref.shape[0]
    raw = g_ref[...] + p_ref[0:1, :]
    lane = lax.broadcasted_iota(jnp.int32, raw.shape, 1)
    u = jnp.exp(-jnp.abs(raw))
    log1pu = jnp.log(1.0 + u)
    sig = jnp.where(raw >= 0.0, 1.0, u) / (1.0 + u)
    vals = jnp.where(lane < 16, sig,
                     jnp.where(lane < 32, p_ref[1:2, :] * (jnp.maximum(raw, 0.0) + log1pu),
                               jnp.where(lane < 48, raw,
                                         jnp.where(lane < 64, jnp.minimum(raw, 0.0) - log1pu, 0.0))))
    ri = lax.broadcasted_iota(jnp.int32, (CHUNK, CHUNK), 0)
    ci = lax.broadcasted_iota(jnp.int32, (CHUNK, CHUNK), 1)
    tri_f = (ri >= ci).astype(BF16)
    tri_b = (ri <= ci).astype(BF16)
    lane_c = lax.broadcasted_iota(jnp.int32, (CHUNK, LANES), 1)
    row_c = lax.broadcasted_iota(jnp.int32, (CHUNK, LANES), 0)
    fwd_lane = (lane_c % 16) < 8

    def cumsum(tri, parts):
        h, m, lo = parts
        return _dot(tri, h) + (_dot(tri, m) + _dot(tri, lo))

    for c in range(rows // CHUNK):
        blk = vals[c * CHUNK:(c + 1) * CHUNK, :]
        parts = _split3(blk)
        cum = jnp.where(fwd_lane, cumsum(tri_f, parts), cumsum(tri_b, parts))
        if want_gd:
            gd_ref[c * CHUNK:(c + 1) * CHUNK, :] = jnp.where(lane_c < 16, blk, jnp.where(lane_c < 32, cum, 0.0))
        if want_ml:
            bcum = pltpu.roll(cum, LANES - 16, axis=1)
            gmb = blk - bcum
            cmf, cmb = gmb, gmb
            for s in (1, 2, 4, 8, 16, 32):
                cmf = jnp.maximum(cmf, jnp.where(row_c >= s, pltpu.roll(cmf, s, axis=0), -jnp.inf))
                cmb = jnp.maximum(cmb, jnp.where(row_c < CHUNK - s, pltpu.roll(cmb, CHUNK - s, axis=0), -jnp.inf))
            cm = jnp.where(fwd_lane, cmf, cmb)
            ml = jnp.where(lane_c < 16, pltpu.roll(gmb, LANES - 32, axis=1),
                           jnp.where(lane_c < 32, pltpu.roll(cm, LANES - 16, axis=1),
                                     jnp.where(lane_c < 48, bcum, 0.0)))
            ml_ref[c * CHUNK:(c + 1) * CHUNK, :] = ml


def _gate_prep(graw, gparams, want_gd, want_ml, tm=256):
    t = graw.shape[0]
    spec = pl.BlockSpec((tm, LANES), lambda i: (i, 0))
    n_out = int(want_gd) + int(want_ml)
    return pl.pallas_call(
        functools.partial(_gate_kernel, want_gd=want_gd, want_ml=want_ml),
        grid=(t // tm,),
        in_specs=[spec, pl.BlockSpec((8, LANES), lambda i: (0, 0))],
        out_specs=[spec] * n_out,
        out_shape=[jax.ShapeDtypeStruct((t, LANES), F32)] * n_out,
        compiler_params=_params(("arbitrary",)),
    )(graw, gparams)


def _split3(a):
    h = a.astype(BF16)
    r = a - h.astype(F32)
    m = r.astype(BF16)
    return h, m, (r - m.astype(F32)).astype(BF16)


def _lane_picks(x, lanes, n_by_matmul=0):
    li = lax.broadcasted_iota(jnp.int32, (LANES, LANES), 0)
    ci = lax.broadcasted_iota(jnp.int32, (LANES, LANES), 1)
    want = jnp.full((LANES, LANES), -1, jnp.int32)
    for j, lane in enumerate(lanes):
        want = jnp.where(ci == j, lane, want)
    sel = (li == want).astype(BF16)
    h, m, lo = _split3(x)
    cols = _dot(h, sel) + (_dot(m, sel) + _dot(lo, sel))
    out = [jnp.broadcast_to(cols[:, j:j + 1], x.shape) for j in range(len(lanes) - n_by_matmul)]
    for lane in lanes[len(lanes) - n_by_matmul:]:
        rep = (li == lane).astype(BF16)
        out.append(_dot(h, rep) + (_dot(m, rep) + _dot(lo, rep)))
    return out


def _dir_masks(direction):
    ri = lax.broadcasted_iota(jnp.int32, (CHUNK, CHUNK), 0)
    ci = lax.broadcasted_iota(jnp.int32, (CHUNK, CHUNK), 1)
    if direction == 0:
        return ri >= ci, ri > ci
    return ri <= ci, ri < ci


def _gdn_kernel(qc_ref, kc_ref, vc_ref, ql_ref, kl_ref, vl_ref, z_ref, cwq_ref, cwk_ref, cwv_ref,
                gdc_ref, gdl_ref, ng_ref, y_ref,
                xpad, qs, ks, vs, beta_t, cg_t, wq_r, u_r, kd_r, qk_r, dc_r, out_s):
    lc = qc_ref.shape[1]
    ll = ql_ref.shape[1]
    lt = lc + ll
    n_c, n_l = lc // CHUNK, ll // CHUNK
    n_t = n_c + n_l
    rb = 256

    def l2n(x):
        return x * lax.rsqrt(jnp.sum(x * x, axis=-1, keepdims=True) + EPS)

    def conv_chain(src_ref, cw_ref, dst, off, ls, kind, which):
        pad = xpad.at[which]
        pad[0:8, :] = jnp.zeros((8, LANES), F32)
        pad[8:8 + ls, :] = src_ref[0].astype(F32)
        pad[8 + ls:16 + ls, :] = jnp.zeros((8, LANES), F32)
        step = min(rb, ls)
        for r0 in range(0, ls, step):
            acc = jnp.zeros((step, LANES), F32)
            for t in range(GDN_CONV):
                s0 = r0 + 8 - GDN_CONV // 2 + t
                acc = acc + cw_ref[t:t + 1, :] * pad[s0:s0 + step, :]
            y = acc * _sigmoid(acc)
            if kind == "q":
                y = l2n(y) * (GDN_DK ** -0.5)
            elif kind == "k":
                y = l2n(y)
            dst[off + r0:off + r0 + step, :] = y
            yield

    def conv_chains(q_ref, k_ref, v_ref, off, ls):
        return [conv_chain(q_ref, cwq_ref, qs, off, ls, "q", 0), conv_chain(k_ref, cwk_ref, ks, off, ls, "k", 1),
                conv_chain(v_ref, cwv_ref, vs, off, ls, "v", 2)]

    def lockstep(chains):
        chains = list(chains)
        while chains:
            alive = []
            for ch in chains:
                try:
                    next(ch)
                    alive.append(ch)
                except StopIteration:
                    pass
            chains = alive

    lockstep(conv_chains(qc_ref, kc_ref, vc_ref, 0, lc))

    head = pl.program_id(1)

    def build_tables(src_ref, off, ls):
        step = min(rb, ls)
        for r0 in range(0, ls, step):
            picked = _lane_picks(src_ref[0, r0:r0 + step, :], [8 * d + head for d in range(2)]
                                 + [16 + 8 * d + head for d in range(2)])
            for d in range(2):
                beta_t[d, off + r0:off + r0 + step, :] = picked[d]
                cg_t[d, off + r0:off + r0 + step, :] = picked[2 + d]

    build_tables(gdc_ref, 0, lc)
    build_tables(gdl_ref, lc, ll)

    def bwd_chunk(t):
        return jnp.where(t < n_c, n_c - 1 - t, n_t + n_c - 1 - t)

    row_p = lax.broadcasted_iota(jnp.int32, (CHUNK, LANES), 0)
    lane_p = lax.broadcasted_iota(jnp.int32, (CHUNK, LANES), 1)
    fwd_p = lane_p < CHUNK
    col_p = jnp.where(fwd_p, lane_p, lane_p - CHUNK)
    signed = jnp.where(fwd_p, row_p - col_p, col_p - row_p)
    incl_p = signed >= 0
    strict_p = signed > 0
    eye_p = (row_p == col_p).astype(F32)
    keep_f = fwd_p.astype(BF16)
    keep_b = (1.0 - fwd_p.astype(F32)).astype(BF16)

    def block_diag(top, bottom):
        zero = jnp.zeros(top.shape, top.dtype)
        return jnp.concatenate([jnp.concatenate([top, zero], axis=1), jnp.concatenate([zero, bottom], axis=1)], axis=0)

    def pair_diag(xp):
        return jnp.concatenate([xp * keep_f, xp * keep_b], axis=0)

    def pair_times3(a, xp):
        ah = a.astype(BF16)
        al = (a - ah.astype(F32)).astype(BF16)
        xh = xp.astype(BF16)
        xl = (xp - xh.astype(F32)).astype(BF16)
        dh = pair_diag(xh)
        return _dot(ah, dh) + (_dot(ah, pair_diag(xl)) + _dot(al, dh))

    def prep_chain(t):
        tc = jnp.minimum(t, n_t - 1)
        q, k, v, beta, cgc, kb, ecg = [], [], [], [], [], [], []
        for d in range(2):
            c = tc if d == 0 else bwd_chunk(tc)
            rows = pl.ds(pl.multiple_of(c * CHUNK, CHUNK), CHUNK)
            q.append(qs[rows, :])
            k.append(ks[rows, :])
            v.append(vs[rows, :])
            beta.append(beta_t[d, rows, :])
            cgc.append(cg_t[d, rows, :])
        kdiag = block_diag(k[0].astype(BF16), k[1].astype(BF16))
        kk = _dot_nt(jnp.concatenate([k[0], k[1]], axis=1).astype(BF16), kdiag)
        qk = _dot_nt(jnp.concatenate([q[0], q[1]], axis=1).astype(BF16), kdiag)
        yield
        beta_p = jnp.where(fwd_p, beta[0], beta[1])
        cg_p = jnp.where(fwd_p, cgc[0], cgc[1])
        cgr_p = jnp.transpose(jnp.concatenate([cgc[0], cgc[1]], axis=0))[0:CHUNK, :]
        decay = jnp.exp(jnp.where(incl_p, cg_p - cgr_p, -jnp.inf))
        qkd = qk * decay
        for d in range(2):
            slot = (t % RING) * 2 + d
            last = CHUNK - 1 if d == 0 else 0
            cg_last = cgc[d][last:last + 1, :]
            ecg.append(jnp.exp(cgc[d]))
            kb.append(k[d] * beta[d])
            qk_r[slot] = qkd[:, d * CHUNK:(d + 1) * CHUNK]
            wq_r[slot, CHUNK:2 * CHUNK, :] = q[d] * ecg[d]
            kd_r[slot] = k[d] * jnp.exp(cg_last - cgc[d])
            dc_r[slot] = jnp.broadcast_to(jnp.exp(cg_last), (8, LANES))
        x = jnp.where(strict_p, -(beta_p * kk) * decay, 0.0)
        tinv = eye_p + x
        x = pair_times3(x, x)
        yield
        for _ in range(4):
            both = pair_times3(jnp.concatenate([tinv, x], axis=0), x)
            tinv, x = tinv + both[0:CHUNK, :], both[CHUNK:2 * CHUNK, :]
            yield
        tinv = (tinv + pair_times3(tinv, x)).astype(BF16)
        yield
        w = _dot(tinv, block_diag((kb[0] * ecg[0]).astype(BF16), (kb[1] * ecg[1]).astype(BF16)))
        u = _dot(tinv, block_diag((v[0] * beta[0]).astype(BF16), (v[1] * beta[1]).astype(BF16)))
        for d in range(2):
            slot = (t % RING) * 2 + d
            wq_r[slot, 0:CHUNK, :] = w[:, d * LANES:(d + 1) * LANES]
            u_r[slot] = u[:, d * LANES:(d + 1) * LANES]

    out_s[...] = jnp.zeros(out_s.shape, F32)

    def scan_chain(d, t0, steps, s, with_out, result):
        for j in range(steps):
            t = t0 + j
            slot = (t % RING) * 2 + d
            ws = _dot(wq_r[slot], s)
            yield
            v_new = u_r[slot] - ws[0:CHUNK, :]
            if with_out:
                c = t if d == 0 else bwd_chunk(t)
                o = ws[CHUNK:2 * CHUNK, :] + _dot(qk_r[slot], v_new)
                l0 = pl.multiple_of((c - n_c) * CHUNK, CHUNK)
                out_s[pl.ds(l0, CHUNK), :] += o
            s = s * dc_r[slot][0:1, :] + _dot_tn(kd_r[slot], v_new)
            yield
        result[d] = s

    def group_body(i, carry, t_base, steps, ahead, with_out, beside=()):
        t0 = t_base + steps * i
        result = [None, None]
        lockstep([scan_chain(d, t0, steps, carry[d], with_out, result) for d in range(2)]
                 + [prep_chain(t0 + steps + j) for j in range(ahead)] + list(beside))
        return result[0], result[1]

    def run_groups(carry, t_base, n_steps, steps, ahead_last, with_out, beside_last=()):
        n_groups = n_steps // steps
        carry = lax.fori_loop(0, n_groups - 1, functools.partial(
            group_body, t_base=t_base, steps=steps, ahead=steps, with_out=with_out), carry)
        return group_body(n_groups - 1, carry, t_base, steps, ahead_last, with_out, beside_last)

    def out_chain(blocks):
        for j in blocks:
            o = out_s[j * rb:(j + 1) * rb, :]
            z = z_ref[0, j * rb:(j + 1) * rb, :].astype(F32)
            y = o * lax.rsqrt(jnp.mean(o * o, axis=-1, keepdims=True) + EPS) * ng_ref[...]
            y_ref[0, j * rb:(j + 1) * rb, :] = (y * (z * _sigmoid(z))).astype(y_ref.dtype)
            yield

    per_block = rb // CHUNK
    n_blocks = ll // rb
    early = [j for j in range(n_blocks)
             if n_l > GDN_STEPS and j * per_block >= GDN_STEPS and (j + 1) * per_block <= n_l - GDN_STEPS]
    late = [j for j in range(n_blocks) if j not in early]

    lockstep([prep_chain(j) for j in range(GDN_CTX_STEPS)] + conv_chains(ql_ref, kl_ref, vl_ref, lc, ll))
    s0 = jnp.zeros((GDN_DK, HEAD_V), F32)
    carry = run_groups((s0, s0), 0, n_c, GDN_CTX_STEPS, GDN_STEPS, False)
    run_groups(carry, n_c, n_l, GDN_STEPS, 0, True, beside_last=[out_chain(early)])
    lockstep([out_chain(late)])


def _gdn(proj_c, proj_l, conv_w8, gd_c, gd_l, norm_g, b0, nb):
    lc = proj_c.shape[1]
    ll = proj_l.shape[1]
    lt = lc + ll
    qb, kb_, vb, zb = (COL_GDN_QKV // LANES, COL_GDN_QKV // LANES + HEADS, COL_GDN_QKV // LANES + 2 * HEADS,
                       COL_GDN_Z // LANES)

    def seq_spec(l, col0):
        return pl.BlockSpec((1, l, LANES), lambda i, h: (i + b0, 0, col0 + h))

    def cw_spec(col0):
        return pl.BlockSpec((8, LANES), lambda i, h: (0, col0 + h))

    return pl.pallas_call(
        _gdn_kernel,
        grid=(nb, HEADS),
        in_specs=[seq_spec(lc, qb), seq_spec(lc, kb_), seq_spec(lc, vb),
                  seq_spec(ll, qb), seq_spec(ll, kb_), seq_spec(ll, vb), seq_spec(ll, zb),
                  cw_spec(0), cw_spec(HEADS), cw_spec(2 * HEADS),
                  pl.BlockSpec((1, lc, LANES), lambda i, h: (i + b0, 0, 0)),
                  pl.BlockSpec((1, ll, LANES), lambda i, h: (i + b0, 0, 0)),
                  pl.BlockSpec((1, LANES), lambda i, h: (0, 0))],
        out_specs=pl.BlockSpec((1, ll, LANES), lambda i, h: (i, 0, h)),
        out_shape=jax.ShapeDtypeStruct((nb, ll, HEADS * HEAD_V), BF16),
        scratch_shapes=[pltpu.VMEM((3, max(lc, ll) + 16, LANES), F32),
                        pltpu.VMEM((lt, LANES), F32), pltpu.VMEM((lt, LANES), F32), pltpu.VMEM((lt, LANES), F32),
                        pltpu.VMEM((2, lt, LANES), F32), pltpu.VMEM((2, lt, LANES), F32),
                        pltpu.VMEM((2 * RING, 2 * CHUNK, LANES), F32),
                        pltpu.VMEM((2 * RING, CHUNK, LANES), F32), pltpu.VMEM((2 * RING, CHUNK, LANES), F32),
                        pltpu.VMEM((2 * RING, CHUNK, CHUNK), F32),
                        pltpu.VMEM((2 * RING, 8, LANES), F32),
                        pltpu.VMEM((ll, LANES), F32)],
        compiler_params=_params(("arbitrary", "arbitrary")),
    )(proj_c, proj_c, proj_c, proj_l, proj_l, proj_l, proj_l, conv_w8, conv_w8, conv_w8,
      gd_c, gd_l, norm_g.reshape(1, LANES))


def _mlstm_kernel(qc_ref, kc_ref, vc_ref, ql_ref, kl_ref, vl_ref, mlc_ref, mll_ref, h_ref, out_s, tabs):
    lc = qc_ref.shape[1]
    ll = ql_ref.shape[1]
    n_c, n_l = lc // CHUNK, ll // CHUNK
    n_t = n_c + n_l
    lt = lc + ll
    pair = pl.program_id(1)
    lane = lax.broadcasted_iota(jnp.int32, (CHUNK, LANES), 1)
    ones_v = jnp.ones((CHUNK, HEAD_V), BF16)
    chains = [(hh, d) for hh in range(2) for d in range(2)]
    hmask = [((lane // ML_DK) == hh).astype(F32) for hh in range(2)]
    incl = [_dir_masks(d)[0] for d in range(2)]

    def build_tables(src_ref, off, ls):
        step = min(256, ls)
        for r0 in range(0, ls, step):
            lanes = [16 * j + 8 * d + 2 * pair + hh for hh, d in chains for j in range(3)]
            for g, tab in enumerate(_lane_picks(src_ref[0, r0:r0 + step, :], lanes, n_by_matmul=5)):
                tabs[g, off + r0:off + r0 + step, :] = tab

    build_tables(mlc_ref, 0, lc)
    build_tables(mll_ref, lc, ll)

    def wide(a):
        return jnp.concatenate([a, a], axis=1)

    def chain(hh, d, c, get_state, is_ctx, result):
        last = CHUNK - 1 if d == 0 else 0
        if is_ctx:
            rows = pl.ds(pl.multiple_of(c * CHUNK, CHUNK), CHUNK)
            q_ref, k_ref, v_ref = qc_ref, kc_ref, vc_ref
        else:
            rows = pl.ds(pl.multiple_of((c - n_c) * CHUNK, CHUNK), CHUNK)
            q_ref, k_ref, v_ref = ql_ref, kl_ref, vl_ref
        q = (q_ref[0, rows, :].astype(F32) * hmask[hh]).astype(BF16)
        k = k_ref[0, rows, :].astype(F32) * (hmask[hh] * (ML_DK ** -0.5))
        v = jnp.concatenate([v_ref[0, rows, hh * HEAD_V:(hh + 1) * HEAD_V], ones_v], axis=1)
        n = chains.index((hh, d))
        trows = pl.ds(pl.multiple_of(c * CHUNK, CHUNK), CHUNK)
        gmb = tabs[3 * n, trows, :]
        gmb_t = jnp.transpose(gmb)[0:CHUNK, :]
        cm = tabs[3 * n + 1, trows, :]
        bc = tabs[3 * n + 2, trows, :]
        qk = _dot_nt(q, k.astype(BF16))
        cm_last = cm[last:last + 1, :]
        b_last = bc[last:last + 1, :]
        wk = (k * jnp.exp(gmb - cm_last)).astype(BF16)
        c_loc = _dot_tn(wk, v)
        yield
        while get_state() is None:
            yield
        cs, ms = get_state()
        mm = jnp.maximum(ms, cm)
        p = jnp.where(incl[d], jnp.exp(gmb_t - mm[:, 0:CHUNK]), 0.0) * qk
        inter = _dot(q, cs.astype(BF16))
        intra = _dot(p.astype(BF16), v)
        yield
        if not is_ctx:
            nd = wide(jnp.exp(ms - mm)) * inter + intra
            hv = nd[:, 0:HEAD_V] / jnp.maximum(jnp.abs(nd[:, HEAD_V:2 * HEAD_V]), jnp.exp(-(bc + mm)))
            l0 = pl.multiple_of((c - n_c) * CHUNK, CHUNK)
            out_s[pl.ds(l0, CHUNK), hh * HEAD_V:(hh + 1) * HEAD_V] += hv
        mx = jnp.maximum(ms, cm_last)
        result[hh, d] = (wide(jnp.exp(ms - mx)) * cs + wide(jnp.exp(cm_last - mx)) * c_loc, b_last + mx)

    out_s[...] = jnp.zeros(out_s.shape, F32)

    def run(chains_iter):
        live = list(chains_iter)
        while live:
            alive = []
            for ch in live:
                try:
                    next(ch)
                    alive.append(ch)
                except StopIteration:
                    pass
            live = alive

    def body(i, carry, is_ctx):
        results = [{} for _ in range(ML_STEPS)]
        gens = []
        for j in range(ML_STEPS):
            t = ML_STEPS * i + j
            for n, (hh, d) in enumerate(chains):
                if is_ctx:
                    c = t if d == 0 else n_c - 1 - t
                else:
                    c = n_c + t if d == 0 else n_t - 1 - t
                if j == 0:
                    get_state = functools.partial(lambda n: carry[n], n)
                else:
                    get_state = functools.partial(results[j - 1].get, (hh, d))
                gens.append(chain(hh, d, c, get_state, is_ctx, results[j]))
        run(gens)
        return tuple(results[-1][hd] for hd in chains)

    st0 = (jnp.zeros((LANES, 2 * HEAD_V), F32), jnp.zeros((1, LANES), F32))
    carry = lax.fori_loop(0, n_c // ML_STEPS, functools.partial(body, is_ctx=True), (st0,) * 4)
    lax.fori_loop(0, n_l // ML_STEPS, functools.partial(body, is_ctx=False), carry)
    h_ref[0] = out_s[...].astype(h_ref.dtype)


def _mlstm(proj_c, q_l, k_l, v_l, ml_c, ml_l, b0):
    lc = proj_c.shape[1]
    nb, ll, _ = q_l.shape
    lt = lc + ll
    qb, kb_, vb = COL_ML_Q // LANES, COL_ML_K // LANES, COL_ML_V // (2 * HEAD_V)
    return pl.pallas_call(
        _mlstm_kernel,
        grid=(nb, HEADS // 2),
        in_specs=[pl.BlockSpec((1, lc, LANES), lambda i, p: (i + b0, 0, qb + p)),
                  pl.BlockSpec((1, lc, LANES), lambda i, p: (i + b0, 0, kb_ + p)),
                  pl.BlockSpec((1, lc, 2 * HEAD_V), lambda i, p: (i + b0, 0, vb + p)),
                  pl.BlockSpec((1, ll, LANES), lambda i, p: (i, 0, p)),
                  pl.BlockSpec((1, ll, LANES), lambda i, p: (i, 0, p)),
                  pl.BlockSpec((1, ll, 2 * HEAD_V), lambda i, p: (i, 0, p)),
                  pl.BlockSpec((1, lc, LANES), lambda i, p: (i + b0, 0, 0)),
                  pl.BlockSpec((1, ll, LANES), lambda i, p: (i + b0, 0, 0))],
        out_specs=pl.BlockSpec((1, ll, 2 * HEAD_V), lambda i, p: (i, 0, p)),
        out_shape=jax.ShapeDtypeStruct((nb, ll, HEADS * HEAD_V), BF16),
        scratch_shapes=[pltpu.VMEM((ll, 2 * HEAD_V), F32), pltpu.VMEM((12, lt, LANES), F32)],
        compiler_params=_params(("arbitrary", "arbitrary")),
    )(proj_c, proj_c, proj_c, q_l, k_l, v_l, ml_c, ml_l)


def _merge_kernel(yg_ref, hm_ref, o_ref, gg_ref, gm_ref, x_ref, mod_ref, mlg_ref, n2_ref,
                  wbg_ref, wbm_ref, wo_ref, wrh_ref, wrl_ref, x1_ref, h2_ref, sc_ref):
    d = x_ref.shape[1]
    o = o_ref[...].astype(F32)
    ym = _sigmoid(o) * hm_ref[...].astype(F32)
    segs = []
    for h in range(HEADS):
        seg = ym[:, h * HEAD_V:(h + 1) * HEAD_V]
        segs.append(seg * lax.rsqrt(jnp.mean(seg * seg, axis=-1, keepdims=True) + EPS))
    ymn = jnp.concatenate(segs, axis=1) * mlg_ref[...]
    y_gdn = _dot(yg_ref[...], wbg_ref[...])
    y_ml = _dot(ymn.astype(BF16), wbm_ref[...])
    mixed = _sigmoid(gg_ref[...].astype(F32)) * y_gdn + _sigmoid(gm_ref[...].astype(F32)) * y_ml
    y = _dot(mixed.astype(BF16), wo_ref[...])
    x1 = x_ref[...] + mod_ref[0, :, 2 * d:3 * d] * y
    x1_ref[...] = x1
    hn = x1 * lax.rsqrt(jnp.mean(x1 * x1, axis=-1, keepdims=True) + EPS) * n2_ref[...]
    h2 = hn * (1.0 + mod_ref[0, :, 4 * d:5 * d]) + mod_ref[0, :, 3 * d:4 * d]
    h2_hi = h2.astype(BF16)
    h2_ref[...] = h2_hi
    h2_lo = (h2 - h2_hi.astype(F32)).astype(BF16)
    logits = _dot_nt(wrh_ref[...], h2_hi) + (_dot_nt(wrl_ref[...], h2_hi) + _dot_nt(wrh_ref[...], h2_lo))
    sc_ref[...] = _sigmoid(logits)


def _merge(y_gdn, h_ml, proj_l2d, x2d, mod3, rows_per_mod, ml_norm_g, norm2_g, wbg, wbm, wo, wr_hi, wr_lo, tok0,
           tm=512):
    t, d = y_gdn.shape
    e = wr_hi.shape[0]
    off = tok0 // tm
    row = lambda i: (i, 0)
    const = lambda i: (0, 0)
    return pl.pallas_call(
        _merge_kernel,
        grid=(t // tm,),
        in_specs=[pl.BlockSpec((tm, d), row), pl.BlockSpec((tm, d), row),
                  pl.BlockSpec((tm, d), lambda i: (i + off, COL_ML_O // d)),
                  pl.BlockSpec((tm, d), lambda i: (i + off, COL_MG_GDN // d)),
                  pl.BlockSpec((tm, d), lambda i: (i + off, COL_MG_ML // d)),
                  pl.BlockSpec((tm, d), lambda i: (i + off, 0)),
                  pl.BlockSpec((1, 1, mod3.shape[2]), lambda i: (((i + off) * tm) // rows_per_mod, 0, 0)),
                  pl.BlockSpec((1, d), const), pl.BlockSpec((1, d), const),
                  pl.BlockSpec((d, d), const), pl.BlockSpec((d, d), const), pl.BlockSpec((d, d), const),
                  pl.BlockSpec((e, d), const), pl.BlockSpec((e, d), const)],
        out_specs=[pl.BlockSpec((tm, d), row), pl.BlockSpec((tm, d), lambda i: (i + off, 0)),
                   pl.BlockSpec((e, tm), lambda i: (0, i))],
        out_shape=[jax.ShapeDtypeStruct((t, d), F32), jax.ShapeDtypeStruct((x2d.shape[0], d), BF16),
                   jax.ShapeDtypeStruct((e, t), F32)],
        compiler_params=_params(("arbitrary",)),
    )(y_gdn, h_ml, proj_l2d, proj_l2d, proj_l2d, x2d, mod3, ml_norm_g.reshape(1, d), norm2_g.reshape(1, d),
      wbg, wbm, wo, wr_hi, wr_lo)


def _expert_kernel(be_ref, nu_ref, nx_ref, x_ref, wgu_hbm, wd_hbm, y_ref, land_gu, land_d, wgu_s, wd_s, sems):
    i = pl.program_id(0)
    de = wd_hbm.shape[1]
    used = i < nu_ref[0]
    first = jnp.logical_or(i == 0, be_ref[i] != be_ref[jnp.maximum(i - 1, 0)])

    def weight_copies(ex):
        return (pltpu.make_async_copy(wgu_hbm.at[ex], land_gu, sems.at[0]),
                pltpu.make_async_copy(wd_hbm.at[ex], land_d, sems.at[1]))

    @pl.when(i == 0)
    def _():
        for cp in weight_copies(be_ref[0]):
            cp.start()

    @pl.when(jnp.logical_and(first, used))
    def _():
        for cp in weight_copies(be_ref[i]):
            cp.wait()
        wgu_s[...] = land_gu[...].astype(BF16)
        wd_s[...] = land_d[...].astype(BF16)

        @pl.when(nx_ref[i] >= 0)
        def _():
            for cp in weight_copies(nx_ref[i]):
                cp.start()

    @pl.when(used)
    def _():
        gu = _dot(x_ref[...], wgu_s[...])
        g = gu[:, 0:de]
        act = (g * _sigmoid(g)) * gu[:, de:2 * de]
        y_ref[...] = _dot(act.astype(BF16), wd_s[...]).astype(y_ref.dtype)

    @pl.when(jnp.logical_not(used))
    def _():
        y_ref[...] = jnp.zeros(y_ref.shape, y_ref.dtype)


def _experts(xb, blk_expert, n_used, next_expert, w_gu, w_down):
    n_slots, d = xb.shape
    n_blocks = n_slots // EXPERT_ROWS
    e, _, de2 = w_gu.shape
    de = de2 // 2
    return pl.pallas_call(
        _expert_kernel,
        grid_spec=pltpu.PrefetchScalarGridSpec(
            num_scalar_prefetch=3,
            grid=(n_blocks,),
            in_specs=[pl.BlockSpec((EXPERT_ROWS, d), lambda i, be, nu, nx: (i, 0)),
                      pl.BlockSpec(memory_space=pl.ANY), pl.BlockSpec(memory_space=pl.ANY)],
            out_specs=pl.BlockSpec((EXPERT_ROWS, d), lambda i, be, nu, nx: (i, 0)),
            scratch_shapes=[pltpu.VMEM((d, de2), w_gu.dtype), pltpu.VMEM((de, d), w_down.dtype),
                            pltpu.VMEM((d, de2), BF16), pltpu.VMEM((de, d), BF16),
                            pltpu.SemaphoreType.DMA((2,))]),
        out_shape=jax.ShapeDtypeStruct((n_slots, d), BF16),
        compiler_params=_params(("arbitrary",)),
    )(blk_expert, n_used, next_expert, xb, w_gu, w_down)


def _final_kernel(x1_ref, h2_ref, yg_ref, wt_ref, mod_ref, wsg_ref, wsd_ref, fg_ref, o_ref):
    d = x1_ref.shape[1]
    ds_ = wsd_ref.shape[0]
    wt = wt_ref[...]
    routed = jnp.zeros(x1_ref.shape, F32)
    for k in range(TOP_K):
        routed = routed + wt[:, k:k + 1] * yg_ref[k].astype(F32)
    gu = _dot(h2_ref[...], wsg_ref[...])
    g = gu[:, 0:ds_]
    sh = _dot(((g * _sigmoid(g)) * gu[:, ds_:2 * ds_]).astype(BF16), wsd_ref[...])
    x2 = x1_ref[...] + mod_ref[0, :, 5 * d:6 * d] * (routed + sh)
    o_ref[...] = x2 * lax.rsqrt(jnp.mean(x2 * x2, axis=-1, keepdims=True) + EPS) * fg_ref[...]


def _final(x1, h2, yg, wts, mod3, rows_per_mod, w_sh_gu, w_sh_down, final_g, tok0, tm=256):
    tp, d = x1.shape
    off = tok0 // tm
    row = lambda i: (i, 0)
    const = lambda i: (0, 0)
    return pl.pallas_call(
        _final_kernel,
        grid=(tp // tm,),
        in_specs=[pl.BlockSpec((tm, d), row), pl.BlockSpec((tm, d), lambda i: (i + off, 0)),
                  pl.BlockSpec((TOP_K, tm, d), lambda i: (0, i, 0)), pl.BlockSpec((tm, TOP_K), row),
                  pl.BlockSpec((1, 1, mod3.shape[2]), lambda i: (((i + off) * tm) // rows_per_mod, 0, 0)),
                  pl.BlockSpec(w_sh_gu.shape, const), pl.BlockSpec(w_sh_down.shape, const),
                  pl.BlockSpec((1, d), const)],
        out_specs=pl.BlockSpec((tm, d), row),
        out_shape=jax.ShapeDtypeStruct((tp, d), F32),
        compiler_params=_params(("arbitrary",)),
    )(x1, h2, yg, wts, mod3, w_sh_gu, w_sh_down, final_g.reshape(1, d))


def _route_kernel(sc_ref, bias_ref, tri_ref, idx_ref, wt_ref, rk_ref, cnt_ref, base_s):
    @pl.when(pl.program_id(0) == 0)
    def _():
        base_s[...] = jnp.zeros(base_s.shape, F32)

    scores = sc_ref[...]
    e, tn = scores.shape
    gsz = e // N_GROUPS
    sel3 = (scores + bias_ref[...]).reshape(N_GROUPS, gsz, tn)
    m1 = jnp.max(sel3, axis=1)
    is_max = sel3 == m1[:, None, :]
    n_max = jnp.sum(is_max.astype(F32), axis=1)
    m2 = jnp.max(jnp.where(is_max, -jnp.inf, sel3), axis=1)
    grp = m1 + jnp.where(n_max >= 2.0, m1, m2)
    gi = lax.broadcasted_iota(jnp.int32, (N_GROUPS, tn), 0)
    ahead = jnp.zeros((N_GROUPS, tn), F32)
    for g in range(N_GROUPS):
        row = grp[g:g + 1, :]
        ahead = ahead + jnp.logical_or(row > grp, jnp.logical_and(row == grp, g < gi)).astype(F32)
    ahead3 = jnp.broadcast_to(ahead[:, None, :], (N_GROUPS, gsz, tn))
    selm = jnp.where(ahead3 < float(TOPK_GROUPS), sel3, -jnp.inf).reshape(e, tn)
    ri = lax.broadcasted_iota(jnp.int32, (e, tn), 0).astype(F32)
    member = jnp.zeros((e, tn), F32)
    idxs, ws = [], []
    for _ in range(TOP_K):
        m = jnp.max(selm, axis=0, keepdims=True)
        idx = jnp.min(jnp.where(selm == m, ri, float(e)), axis=0, keepdims=True)
        hit = ri == idx
        ws.append(jnp.sum(jnp.where(hit, scores, 0.0), axis=0, keepdims=True))
        idxs.append(idx)
        selm = jnp.where(hit, -jnp.inf, selm)
        member = jnp.where(hit, 1.0, member)
    w = jnp.concatenate(ws, axis=0)
    wt_ref[...] = w / jnp.sum(w, axis=0, keepdims=True) * ROUTED_SCALE
    idx_ref[...] = jnp.concatenate(idxs, axis=0).astype(jnp.int32)
    cum = _dot(member.astype(BF16), tri_ref[...]) + base_s[...]
    rk_ref[...] = jnp.concatenate(
        [jnp.sum(jnp.where(ri == idx, cum, 0.0), axis=0, keepdims=True) for idx in idxs], axis=0).astype(jnp.int32)
    total = base_s[...] + jnp.sum(member, axis=1, keepdims=True)
    base_s[...] = total
    cnt_ref[...] = total


def _route(scores_t, router_bias):
    e, t = scores_t.shape
    tn = LANES
    bias = jnp.broadcast_to(router_bias.astype(F32)[:, None], (e, tn))
    tri = (jnp.arange(tn)[:, None] < jnp.arange(tn)[None, :]).astype(BF16)
    tok = pl.BlockSpec((TOP_K, tn), lambda i: (0, i))
    const = lambda i: (0, 0)
    return pl.pallas_call(
        _route_kernel,
        grid=(t // tn,),
        in_specs=[pl.BlockSpec((e, tn), lambda i: (0, i)), pl.BlockSpec((e, tn), const),
                  pl.BlockSpec((tn, tn), const)],
        out_specs=[tok, tok, tok, pl.BlockSpec((e, tn), const)],
        out_shape=[jax.ShapeDtypeStruct((TOP_K, t), jnp.int32), jax.ShapeDtypeStruct((TOP_K, t), F32),
                   jax.ShapeDtypeStruct((TOP_K, t), jnp.int32), jax.ShapeDtypeStruct((e, tn), F32)],
        scratch_shapes=[pltpu.VMEM((e, tn), F32)],
        compiler_params=_params(("arbitrary",)),
    )(scores_t, bias, tri)


def _slot_kernel(idx_ref, rk_ref, ps_ref, pos_ref):
    e, tn = ps_ref.shape
    ri = lax.broadcasted_iota(jnp.int32, (e, tn), 0)
    ps = ps_ref[...]
    rows = [jnp.sum(jnp.where(ri == idx_ref[k:k + 1, :], ps, 0.0), axis=0, keepdims=True) for k in range(TOP_K)]
    pos_ref[...] = rk_ref[...] + jnp.concatenate(rows, axis=0).astype(jnp.int32)


def _slots(idx, rank, pstart):
    k, t = idx.shape
    e = pstart.shape[0]
    tn = LANES
    tok = pl.BlockSpec((k, tn), lambda i: (0, i))
    return pl.pallas_call(
        _slot_kernel,
        grid=(t // tn,),
        in_specs=[tok, tok, pl.BlockSpec((e, tn), lambda i: (0, 0))],
        out_specs=tok,
        out_shape=jax.ShapeDtypeStruct((k, t), jnp.int32),
        compiler_params=_params(("arbitrary",)),
    )(idx, rank, jnp.broadcast_to(pstart.astype(F32)[:, None], (e, tn)))


def _sc_scatter_rows(vals, idx, n_rows, window=LANES):
    n, width = vals.shape
    mesh = plsc.VectorSubcoreMesh(core_axis_name="core", subcore_axis_name="subcore")

    @pl.kernel(out_type=jax.ShapeDtypeStruct((n_rows, width), vals.dtype), mesh=mesh, scratch_types=[])
    def scatter(v_hbm, i_hbm, o_hbm):
        def body(v_vmem, i_vmem):
            pltpu.sync_copy(v_vmem, o_hbm.at[i_vmem.at[0]])

        pltpu.emit_pipeline(
            body,
            grid=(n // window,),
            in_specs=[pl.BlockSpec((window, width), lambda i: (i, 0)),
                      pl.BlockSpec((1, window), lambda i: (0, i))],
            out_specs=[],
            core_axis_name=("core", "subcore"),
            dimension_semantics=(pltpu.PARALLEL,),
        )(v_hbm, i_hbm)

    return scatter(vals, idx.reshape(1, n))


def _block_table(counts, n_blocks):
    padded = (counts + EXPERT_ROWS - 1) // EXPERT_ROWS * EXPERT_ROWS
    pend = jnp.cumsum(padded)
    first_slot = jnp.arange(n_blocks, dtype=jnp.int32) * EXPERT_ROWS
    blk_expert = jnp.minimum(jnp.sum((pend[None, :] <= first_slot[:, None]).astype(jnp.int32), axis=1),
                             N_EXPERTS - 1)
    e_ids = jnp.arange(N_EXPERTS, dtype=jnp.int32)
    later = lax.cummin(jnp.where(counts > 0, e_ids, N_EXPERTS)[::-1])[::-1]
    next_tab = jnp.concatenate([later[1:], jnp.full((1,), N_EXPERTS, jnp.int32)])
    next_tab = jnp.where(next_tab < N_EXPERTS, next_tab, -1)
    return (pend - padded, blk_expert, (pend[-1] // EXPERT_ROWS).astype(jnp.int32).reshape(1),
            next_tab[blk_expert].astype(jnp.int32))


def _col_major(t):
    b, l, f = t.shape
    rows = l // GRID_W
    return t.reshape(b, rows, GRID_W, f).transpose(0, 2, 1, 3).reshape(b, l, f)


def _row_major(t):
    b, l, f = t.shape
    rows = l // GRID_W
    return t.reshape(b, GRID_W, rows, f).transpose(0, 2, 1, 3).reshape(b, l, f)


def kernel(x, c, ctx, c_ctx, w_ada, b_ada, norm1_g, norm2_g, w_in, gdn_conv_w, gdn_a_log, gdn_dt_bias, gdn_norm_g,
           ml_i_bias, ml_f_bias, ml_norm_g, w_branch_gdn, w_branch_ml, w_out, w_router, router_bias, w_exp_gate_up,
           w_exp_down, w_sh_gate_up, w_sh_down, final_norm_g):
    b, l, d = x.shape
    lc = ctx.shape[1]
    t = b * l
    layer = 0

    w = w_in[layer]
    main_cols = [_ORIG[k] for k in ("gdn_qkv", "ml_q", "ml_k", "ml_v", "gdn_z", "ml_o", "mg_gdn", "mg_ml")]
    w_main = jnp.concatenate([w[:, a:e] for a, e in main_cols], axis=1).astype(BF16)
    w_gate = jnp.concatenate([w[:, _ORIG["gdn_gate"][0]:_ORIG["gdn_gate"][1]],
                              w[:, _ORIG["ml_gate"][0]:_ORIG["ml_gate"][1]],
                              jnp.zeros((d, LANES - 64), F32)], axis=1).astype(BF16)
    zeros16 = jnp.zeros((16,), F32)
    gp_add = jnp.concatenate([zeros16, gdn_dt_bias[layer].reshape(-1), ml_i_bias[layer].reshape(-1),
                              ml_f_bias[layer].reshape(-1), jnp.zeros((LANES - 64,), F32)])
    gp_mul = jnp.concatenate([zeros16, -jnp.exp(gdn_a_log[layer].astype(F32)).reshape(-1),
                              jnp.zeros((LANES - 32,), F32)])
    gparams = jnp.zeros((8, LANES), F32).at[0].set(gp_add).at[1].set(gp_mul)
    conv_w8 = jnp.zeros((8, gdn_conv_w.shape[2]), F32).at[0:GDN_CONV].set(gdn_conv_w[layer])
    wr = w_router[layer].T
    wr_hi = wr.astype(BF16)
    wr_lo = (wr - wr_hi.astype(F32)).astype(BF16)

    n_mod_rows = -(-(b + 1) // 8) * 8
    cc = jnp.zeros((n_mod_rows, d), F32).at[0:b].set(c).at[b].set(c_ctx)
    mod = _ada_mod(cc, w_ada[layer], b_ada[layer])
    mod3 = mod.reshape(n_mod_rows, 1, 6 * d)

    x2d = x.reshape(t, d)
    tm_l = min(1024, l)
    proj_l, gate_l = _project(x2d, mod3, lambda i: (i * tm_l) // l, norm1_g[layer], w_main, w_gate, tm_l)
    tm_c = min(1024, b * lc)
    proj_c, gate_c = _project(ctx.reshape(b * lc, d), mod3, lambda i: b, norm1_g[layer], w_main[:, 0:N_SCAN],
                              w_gate, tm_c, tn=N_SCAN // 2)
    proj_l3 = proj_l.reshape(b, l, N_MAIN)
    proj_c3 = proj_c.reshape(b, lc, N_SCAN)

    gate_l_cm = _col_major(gate_l.reshape(b, l, LANES)).reshape(t, LANES)
    gd_c, ml_c = _gate_prep(gate_c, gparams, True, True)
    gd_l, = _gate_prep(gate_l, gparams, True, False)
    ml_l, = _gate_prep(gate_l_cm, gparams, False, True)

    nb = b // BATCH_PARTS
    tp = nb * l
    n_assign = tp * TOP_K
    n_blocks = (n_assign + N_EXPERTS * (EXPERT_ROWS - 1)) // EXPERT_ROWS + 1
    n_slots = n_blocks * EXPERT_ROWS
    gd_c3, gd_l3 = gd_c.reshape(b, lc, LANES), gd_l.reshape(b, l, LANES)
    ml_c3, ml_l3 = ml_c.reshape(b, lc, LANES), ml_l.reshape(b, l, LANES)
    wbg, wbm, wo = (w_branch_gdn[layer].astype(BF16), w_branch_ml[layer].astype(BF16), w_out[layer].astype(BF16))
    w_sh_gu, w_sh_dn = w_sh_gate_up[layer].astype(BF16), w_sh_down[layer].astype(BF16)
    outs = []
    for part in range(BATCH_PARTS):
        b0, tok0 = part * nb, part * tp
        y_gdn = _gdn(proj_c3, proj_l3, conv_w8, gd_c3, gd_l3, gdn_norm_g[layer], b0, nb)
        q_cm = _col_major(proj_l3[b0:b0 + nb, :, COL_ML_Q:COL_ML_Q + HEADS * ML_DK])
        k_cm = _col_major(proj_l3[b0:b0 + nb, :, COL_ML_K:COL_ML_K + HEADS * ML_DK])
        v_cm = _col_major(proj_l3[b0:b0 + nb, :, COL_ML_V:COL_ML_V + HEADS * HEAD_V])
        h_ml = _row_major(_mlstm(proj_c3, q_cm, k_cm, v_cm, ml_c3, ml_l3, b0))
        x1, h2, scores_t = _merge(y_gdn.reshape(tp, d), h_ml.reshape(tp, d), proj_l, x2d, mod3, l, ml_norm_g[layer],
                                norm2_g[layer], wbg, wbm, wo, wr_hi, wr_lo, tok0, tm=min(512, l))
        idx, wts, rank, cnt = _route(scores_t, router_bias[layer])
        counts = cnt[:, 0].astype(jnp.int32)
        pstart, blk_expert, n_used, next_expert = _block_table(counts, n_blocks)
        pos = _slots(idx, rank, pstart).reshape(n_assign)
        tok_ids = jnp.broadcast_to((jnp.arange(n_assign, dtype=jnp.int32) % tp)[:, None], (n_assign, LANES))
        scattered = _sc_scatter_rows(tok_ids, pos, n_slots)[:, 0]
        in_expert = (jnp.arange(n_slots, dtype=jnp.int32).reshape(n_blocks, EXPERT_ROWS)
                     - pstart[blk_expert][:, None])
        valid = (in_expert < counts[blk_expert][:, None]).reshape(n_slots)
        tok_slot = jnp.where(valid, scattered, jnp.arange(n_slots, dtype=jnp.int32) % tp)
        xb = h2.at[tok_slot + tok0].get(mode="promise_in_bounds")
        yb = _experts(xb, blk_expert, n_used, next_expert, w_exp_gate_up[layer], w_exp_down[layer])
        yg = yb.at[pos].get(mode="promise_in_bounds", unique_indices=True).reshape(TOP_K, tp, d)
        outs.append(_final(x1, h2, yg, wts.T, mod3, l, w_sh_gu, w_sh_dn, final_norm_g, tok0, tm=min(512, l)))
    return jnp.concatenate(outs, axis=0).reshape(b, l, d)
```

```python
import functools

import jax
import jax.numpy as jnp
from jax import lax
from jax.experimental import pallas as pl
from jax.experimental.pallas import tpu as pltpu
from jax.experimental.pallas import tpu_sc as plsc

F32 = jnp.float32
BF16 = jnp.bfloat16
HI = lax.Precision.HIGHEST

EPS = 1e-6
CHUNK = 64
GRID_W = 64
HEADS = 8
HEAD_V = 128
GDN_DK = 128
ML_DK = 64
GDN_CONV = 5
N_EXPERTS = 256
TOP_K = 8
N_GROUPS = 8
TOPK_GROUPS = 4
ROUTED_SCALE = 2.5
EXPERT_ROWS = 256
BATCH_PARTS = 1
GDN_STEPS = 4
GDN_CTX_STEPS = 4
ML_STEPS = 4
RING = 2 * GDN_STEPS
LANES = 128
VMEM_LIMIT = 56 * 1024 * 1024

COL_GDN_QKV = 0
COL_ML_Q = 3072
COL_ML_K = 3584
COL_ML_V = 4096
N_SCAN = 5120
COL_GDN_Z = 5120
COL_ML_O = 6144
COL_MG_GDN = 7168
COL_MG_ML = 8192
N_MAIN = 9216
_ORIG = dict(gdn_qkv=(0, 3072), gdn_z=(3072, 4096), gdn_gate=(4096, 4128), ml_q=(4128, 4640),
             ml_k=(4640, 5152), ml_v=(5152, 6176), ml_o=(6176, 7200), ml_gate=(7200, 7232),
             mg_gdn=(7232, 8256), mg_ml=(8256, 9280))


def _params(sem, vmem=VMEM_LIMIT):
    return pltpu.CompilerParams(dimension_semantics=sem, vmem_limit_bytes=vmem)


def _dot(a, b, precision=None):
    return jnp.dot(a, b, preferred_element_type=F32, precision=precision)


def _dot_nt(a, b, precision=None):
    return lax.dot_general(a, b, (((1,), (1,)), ((), ())), preferred_element_type=F32, precision=precision)


def _dot_tn(a, b, precision=None):
    return lax.dot_general(a, b, (((0,), (0,)), ((), ())), preferred_element_type=F32, precision=precision)


def _sigmoid(x):
    return 1.0 / (1.0 + jnp.exp(-x))


def _softplus(x):
    return jnp.maximum(x, 0.0) + jnp.log(1.0 + jnp.exp(-jnp.abs(x)))


def _ada_kernel(c_ref, w_ref, b_ref, o_ref):
    c = c_ref[...]
    sc = c * _sigmoid(c)
    o_ref[...] = _dot(sc, w_ref[...], HI) + b_ref[...]


def _ada_mod(cc, w_ada, b_ada, tn=1536):
    r, d = cc.shape
    n = w_ada.shape[1]
    return pl.pallas_call(
        _ada_kernel,
        grid=(n // tn,),
        in_specs=[pl.BlockSpec((r, d), lambda j: (0, 0)),
                  pl.BlockSpec((d, tn), lambda j: (0, j)),
                  pl.BlockSpec((1, tn), lambda j: (0, j))],
        out_specs=pl.BlockSpec((r, tn), lambda j: (0, j)),
        out_shape=jax.ShapeDtypeStruct((r, n), F32),
        compiler_params=_params(("arbitrary",)),
    )(cc, w_ada, b_ada.reshape(1, n))


def _proj_kernel(x_ref, mod_ref, g_ref, w_ref, wg_ref, o_ref, og_ref, hn_ref):
    d = x_ref.shape[1]

    @pl.when(pl.program_id(1) == 0)
    def _():
        x = x_ref[...]
        y = x * lax.rsqrt(jnp.mean(x * x, axis=-1, keepdims=True) + EPS) * g_ref[...]
        shift = mod_ref[0, :, 0:d]
        scale = mod_ref[0, :, d:2 * d]
        h = (y * (1.0 + scale) + shift).astype(BF16)
        hn_ref[...] = h
        og_ref[...] = _dot(h, wg_ref[...])

    o_ref[...] = _dot(hn_ref[...], w_ref[...]).astype(o_ref.dtype)


def _project(x2d, mod3, mod_row_of_tile, norm_g, w_main, w_gate, tm, tn=2304):
    t, d = x2d.shape
    n = w_main.shape[1]
    return pl.pallas_call(
        _proj_kernel,
        grid=(t // tm, n // tn),
        in_specs=[pl.BlockSpec((tm, d), lambda i, j: (i, 0)),
                  pl.BlockSpec((1, 1, mod3.shape[2]), lambda i, j: (mod_row_of_tile(i), 0, 0)),
                  pl.BlockSpec((1, d), lambda i, j: (0, 0)),
                  pl.BlockSpec((d, tn), lambda i, j: (0, j)),
                  pl.BlockSpec((d, LANES), lambda i, j: (0, 0))],
        out_specs=[pl.BlockSpec((tm, tn), lambda i, j: (i, j)),
                   pl.BlockSpec((tm, LANES), lambda i, j: (i, 0))],
        out_shape=[jax.ShapeDtypeStruct((t, n), BF16), jax.ShapeDtypeStruct((t, LANES), F32)],
        scratch_shapes=[pltpu.VMEM((tm, d), BF16)],
        compiler_params=_params(("arbitrary", "arbitrary")),
    )(x2d, mod3, norm_g.reshape(1, d), w_main, w_gate)


def _gate_kernel(g_ref, p_ref, *out_refs, want_gd, want_ml):
    gd_ref = out_refs[0] if want_gd else None
    ml_ref = out_refs[-1] if want_ml else None
    rows = g_ref.shape[0]
    raw = g_ref[...] + p_ref[0:1, :]
    lane = lax.broadcasted_iota(jnp.int32, raw.shape, 1)
    sp = _softplus(raw)
    vals = jnp.where(lane < 16, _sigmoid(raw),
                     jnp.where(lane < 32, p_ref[1:2, :] * sp,
                               jnp.where(lane < 48, raw,
                                         jnp.where(lane < 64, -_softplus(-raw), 0.0))))
    ri = lax.broadcasted_iota(jnp.int32, (CHUNK, CHUNK), 0)
    ci = lax.broadcasted_iota(jnp.int32, (CHUNK, CHUNK), 1)
    tri_f = (ri >= ci).astype(BF16)
    tri_b = (ri <= ci).astype(BF16)
    lane_c = lax.broadcasted_iota(jnp.int32, (CHUNK, LANES), 1)
    row_c = lax.broadcasted_iota(jnp.int32, (CHUNK, LANES), 0)
    fwd_lane = (lane_c % 16) < 8

    def cumsum(tri, parts):
        h, m, lo = parts
        return _dot(tri, h) + (_dot(tri, m) + _dot(tri, lo))

    for c in range(rows // CHUNK):
        blk = vals[c * CHUNK:(c + 1) * CHUNK, :]
        parts = _split3(blk)
        cum = jnp.where(fwd_lane, cumsum(tri_f, parts), cumsum(tri_b, parts))
        if want_gd:
            gd_ref[c * CHUNK:(c + 1) * CHUNK, :] = jnp.where(lane_c < 16, blk, jnp.where(lane_c < 32, cum, 0.0))
        if want_ml:
            bcum = pltpu.roll(cum, LANES - 16, axis=1)
            gmb = blk - bcum
            cmf, cmb = gmb, gmb
            for s in (1, 2, 4, 8, 16, 32):
                cmf = jnp.maximum(cmf, jnp.where(row_c >= s, pltpu.roll(cmf, s, axis=0), -jnp.inf))
                cmb = jnp.maximum(cmb, jnp.where(row_c < CHUNK - s, pltpu.roll(cmb, CHUNK - s, axis=0), -jnp.inf))
            cm = jnp.where(fwd_lane, cmf, cmb)
            ml = jnp.where(lane_c < 16, pltpu.roll(gmb, LANES - 32, axis=1),
                           jnp.where(lane_c < 32, pltpu.roll(cm, LANES - 16, axis=1),
                                     jnp.where(lane_c < 48, bcum, 0.0)))
            ml_ref[c * CHUNK:(c + 1) * CHUNK, :] = ml


def _gate_prep(graw, gparams, want_gd, want_ml, tm=256):
    t = graw.shape[0]
    spec = pl.BlockSpec((tm, LANES), lambda i: (i, 0))
    n_out = int(want_gd) + int(want_ml)
    return pl.pallas_call(
        functools.partial(_gate_kernel, want_gd=want_gd, want_ml=want_ml),
        grid=(t // tm,),
        in_specs=[spec, pl.BlockSpec((8, LANES), lambda i: (0, 0))],
        out_specs=[spec] * n_out,
        out_shape=[jax.ShapeDtypeStruct((t, LANES), F32)] * n_out,
        compiler_params=_params(("arbitrary",)),
    )(graw, gparams)


def _split3(a):
    h = a.astype(BF16)
    r = a - h.astype(F32)
    m = r.astype(BF16)
    return h, m, (r - m.astype(F32)).astype(BF16)


def _lane_picks(x, lanes, n_by_matmul=0):
    li = lax.broadcasted_iota(jnp.int32, (LANES, LANES), 0)
    ci = lax.broadcasted_iota(jnp.int32, (LANES, LANES), 1)
    want = jnp.full((LANES, LANES), -1, jnp.int32)
    for j, lane in enumerate(lanes):
        want = jnp.where(ci == j, lane, want)
    sel = (li == want).astype(BF16)
    h, m, lo = _split3(x)
    cols = _dot(h, sel) + (_dot(m, sel) + _dot(lo, sel))
    out = [jnp.broadcast_to(cols[:, j:j + 1], x.shape) for j in range(len(lanes) - n_by_matmul)]
    for lane in lanes[len(lanes) - n_by_matmul:]:
        rep = (li == lane).astype(BF16)
        out.append(_dot(h, rep) + (_dot(m, rep) + _dot(lo, rep)))
    return out


def _dir_masks(direction):
    ri = lax.broadcasted_iota(jnp.int32, (CHUNK, CHUNK), 0)
    ci = lax.broadcasted_iota(jnp.int32, (CHUNK, CHUNK), 1)
    if direction == 0:
        return ri >= ci, ri > ci
    return ri <= ci, ri < ci


def _gdn_kernel(qc_ref, kc_ref, vc_ref, ql_ref, kl_ref, vl_ref, z_ref, cwq_ref, cwk_ref, cwv_ref,
                gdc_ref, gdl_ref, ng_ref, y_ref,
                xpad, qs, ks, vs, beta_t, cg_t, wq_r, u_r, kd_r, qk_r, dc_r, out_s):
    lc = qc_ref.shape[1]
    ll = ql_ref.shape[1]
    lt = lc + ll
    n_c, n_l = lc // CHUNK, ll // CHUNK
    n_t = n_c + n_l
    rb = 256

    def l2n(x):
        return x * lax.rsqrt(jnp.sum(x * x, axis=-1, keepdims=True) + EPS)

    def conv_chain(src_ref, cw_ref, dst, off, ls, kind, which):
        pad = xpad.at[which]
        pad[0:8, :] = jnp.zeros((8, LANES), F32)
        pad[8:8 + ls, :] = src_ref[0].astype(F32)
        pad[8 + ls:16 + ls, :] = jnp.zeros((8, LANES), F32)
        step = min(rb, ls)
        for r0 in range(0, ls, step):
            acc = jnp.zeros((step, LANES), F32)
            for t in range(GDN_CONV):
                s0 = r0 + 8 - GDN_CONV // 2 + t
                acc = acc + cw_ref[t:t + 1, :] * pad[s0:s0 + step, :]
            y = acc * _sigmoid(acc)
            if kind == "q":
                y = l2n(y) * (GDN_DK ** -0.5)
            elif kind == "k":
                y = l2n(y)
            dst[off + r0:off + r0 + step, :] = y
            yield

    def conv_chains(q_ref, k_ref, v_ref, off, ls):
        return [conv_chain(q_ref, cwq_ref, qs, off, ls, "q", 0), conv_chain(k_ref, cwk_ref, ks, off, ls, "k", 1),
                conv_chain(v_ref, cwv_ref, vs, off, ls, "v", 2)]

    def lockstep(chains):
        chains = list(chains)
        while chains:
            alive = []
            for ch in chains:
                try:
                    next(ch)
                    alive.append(ch)
                except StopIteration:
                    pass
            chains = alive

    lockstep(conv_chains(qc_ref, kc_ref, vc_ref, 0, lc))

    head = pl.program_id(1)

    def build_tables(src_ref, off, ls):
        step = min(rb, ls)
        for r0 in range(0, ls, step):
            picked = _lane_picks(src_ref[0, r0:r0 + step, :], [8 * d + head for d in range(2)]
                                 + [16 + 8 * d + head for d in range(2)])
            for d in range(2):
                beta_t[d, off + r0:off + r0 + step, :] = picked[d]
                cg_t[d, off + r0:off + r0 + step, :] = picked[2 + d]

    build_tables(gdc_ref, 0, lc)
    build_tables(gdl_ref, lc, ll)

    def bwd_chunk(t):
        return jnp.where(t < n_c, n_c - 1 - t, n_t + n_c - 1 - t)

    row_p = lax.broadcasted_iota(jnp.int32, (CHUNK, LANES), 0)
    lane_p = lax.broadcasted_iota(jnp.int32, (CHUNK, LANES), 1)
    fwd_p = lane_p < CHUNK
    col_p = jnp.where(fwd_p, lane_p, lane_p - CHUNK)
    signed = jnp.where(fwd_p, row_p - col_p, col_p - row_p)
    incl_p = signed >= 0
    strict_p = signed > 0
    eye_p = (row_p == col_p).astype(F32)
    keep_f = fwd_p.astype(BF16)
    keep_b = (1.0 - fwd_p.astype(F32)).astype(BF16)

    def block_diag(top, bottom):
        zero = jnp.zeros(top.shape, top.dtype)
        return jnp.concatenate([jnp.concatenate([top, zero], axis=1), jnp.concatenate([zero, bottom], axis=1)], axis=0)

    def pair_diag(xp):
        return jnp.concatenate([xp * keep_f, xp * keep_b], axis=0)

    def pair_times3(a, xp):
        ah = a.astype(BF16)
        al = (a - ah.astype(F32)).astype(BF16)
        xh = xp.astype(BF16)
        xl = (xp - xh.astype(F32)).astype(BF16)
        dh = pair_diag(xh)
        return _dot(ah, dh) + (_dot(ah, pair_diag(xl)) + _dot(al, dh))

    def prep_chain(t):
        tc = jnp.minimum(t, n_t - 1)
        q, k, v, beta, cgc, kb, ecg = [], [], [], [], [], [], []
        for d in range(2):
            c = tc if d == 0 else bwd_chunk(tc)
            rows = pl.ds(pl.multiple_of(c * CHUNK, CHUNK), CHUNK)
            q.append(qs[rows, :])
            k.append(ks[rows, :])
            v.append(vs[rows, :])
            beta.append(beta_t[d, rows, :])
            cgc.append(cg_t[d, rows, :])
        kdiag = block_diag(k[0].astype(BF16), k[1].astype(BF16))
        kk = _dot_nt(jnp.concatenate([k[0], k[1]], axis=1).astype(BF16), kdiag)
        qk = _dot_nt(jnp.concatenate([q[0], q[1]], axis=1).astype(BF16), kdiag)
        yield
        beta_p = jnp.where(fwd_p, beta[0], beta[1])
        cg_p = jnp.where(fwd_p, cgc[0], cgc[1])
        cgr_p = jnp.transpose(jnp.concatenate([cgc[0], cgc[1]], axis=0))[0:CHUNK, :]
        decay = jnp.exp(jnp.where(incl_p, cg_p - cgr_p, -jnp.inf))
        qkd = qk * decay
        for d in range(2):
            slot = (t % RING) * 2 + d
            last = CHUNK - 1 if d == 0 else 0
            cg_last = cgc[d][last:last + 1, :]
            ecg.append(jnp.exp(cgc[d]))
            kb.append(k[d] * beta[d])
            qk_r[slot] = qkd[:, d * CHUNK:(d + 1) * CHUNK]
            wq_r[slot, CHUNK:2 * CHUNK, :] = q[d] * ecg[d]
            kd_r[slot] = k[d] * jnp.exp(cg_last - cgc[d])
            dc_r[slot] = jnp.broadcast_to(jnp.exp(cg_last), (8, LANES))
        x = jnp.where(strict_p, -(beta_p * kk) * decay, 0.0)
        tinv = eye_p + x
        x = pair_times3(x, x)
        yield
        for _ in range(4):
            both = pair_times3(jnp.concatenate([tinv, x], axis=0), x)
            tinv, x = tinv + both[0:CHUNK, :], both[CHUNK:2 * CHUNK, :]
            yield
        tinv = (tinv + pair_times3(tinv, x)).astype(BF16)
        yield
        w = _dot(tinv, block_diag((kb[0] * ecg[0]).astype(BF16), (kb[1] * ecg[1]).astype(BF16)))
        u = _dot(tinv, block_diag((v[0] * beta[0]).astype(BF16), (v[1] * beta[1]).astype(BF16)))
        for d in range(2):
            slot = (t % RING) * 2 + d
            wq_r[slot, 0:CHUNK, :] = w[:, d * LANES:(d + 1) * LANES]
            u_r[slot] = u[:, d * LANES:(d + 1) * LANES]

    out_s[...] = jnp.zeros(out_s.shape, F32)

    def scan_chain(d, t0, steps, s, with_out, result):
        for j in range(steps):
            t = t0 + j
            slot = (t % RING) * 2 + d
            ws = _dot(wq_r[slot], s)
            yield
            v_new = u_r[slot] - ws[0:CHUNK, :]
            if with_out:
                c = t if d == 0 else bwd_chunk(t)
                o = ws[CHUNK:2 * CHUNK, :] + _dot(qk_r[slot], v_new)
                l0 = pl.multiple_of((c - n_c) * CHUNK, CHUNK)
                out_s[pl.ds(l0, CHUNK), :] += o
            s = s * dc_r[slot][0:1, :] + _dot_tn(kd_r[slot], v_new)
            yield
        result[d] = s

    def group_body(i, carry, t_base, steps, ahead, with_out, beside=()):
        t0 = t_base + steps * i
        result = [None, None]
        lockstep([scan_chain(d, t0, steps, carry[d], with_out, result) for d in range(2)]
                 + [prep_chain(t0 + steps + j) for j in range(ahead)] + list(beside))
        return result[0], result[1]

    def run_groups(carry, t_base, n_steps, steps, ahead_last, with_out, beside_last=()):
        n_groups = n_steps // steps
        carry = lax.fori_loop(0, n_groups - 1, functools.partial(
            group_body, t_base=t_base, steps=steps, ahead=steps, with_out=with_out), carry)
        return group_body(n_groups - 1, carry, t_base, steps, ahead_last, with_out, beside_last)

    def out_chain(blocks):
        for j in blocks:
            o = out_s[j * rb:(j + 1) * rb, :]
            z = z_ref[0, j * rb:(j + 1) * rb, :].astype(F32)
            y = o * lax.rsqrt(jnp.mean(o * o, axis=-1, keepdims=True) + EPS) * ng_ref[...]
            y_ref[0, j * rb:(j + 1) * rb, :] = (y * (z * _sigmoid(z))).astype(y_ref.dtype)
            yield

    per_block = rb // CHUNK
    n_blocks = ll // rb
    early = [j for j in range(n_blocks)
             if n_l > GDN_STEPS and j * per_block >= GDN_STEPS and (j + 1) * per_block <= n_l - GDN_STEPS]
    late = [j for j in range(n_blocks) if j not in early]

    lockstep([prep_chain(j) for j in range(GDN_CTX_STEPS)] + conv_chains(ql_ref, kl_ref, vl_ref, lc, ll))
    s0 = jnp.zeros((GDN_DK, HEAD_V), F32)
    carry = run_groups((s0, s0), 0, n_c, GDN_CTX_STEPS, GDN_STEPS, False)
    run_groups(carry, n_c, n_l, GDN_STEPS, 0, True, beside_last=[out_chain(early)])
    lockstep([out_chain(late)])


def _gdn(proj_c, proj_l, conv_w8, gd_c, gd_l, norm_g, b0, nb):
    lc = proj_c.shape[1]
    ll = proj_l.shape[1]
    lt = lc + ll
    qb, kb_, vb, zb = (COL_GDN_QKV // LANES, COL_GDN_QKV // LANES + HEADS, COL_GDN_QKV // LANES + 2 * HEADS,
                       COL_GDN_Z // LANES)

    def seq_spec(l, col0):
        return pl.BlockSpec((1, l, LANES), lambda i, h: (i + b0, 0, col0 + h))

    def cw_spec(col0):
        return pl.BlockSpec((8, LANES), lambda i, h: (0, col0 + h))

    return pl.pallas_call(
        _gdn_kernel,
        grid=(nb, HEADS),
        in_specs=[seq_spec(lc, qb), seq_spec(lc, kb_), seq_spec(lc, vb),
                  seq_spec(ll, qb), seq_spec(ll, kb_), seq_spec(ll, vb), seq_spec(ll, zb),
                  cw_spec(0), cw_spec(HEADS), cw_spec(2 * HEADS),
                  pl.BlockSpec((1, lc, LANES), lambda i, h: (i + b0, 0, 0)),
                  pl.BlockSpec((1, ll, LANES), lambda i, h: (i + b0, 0, 0)),
                  pl.BlockSpec((1, LANES), lambda i, h: (0, 0))],
        out_specs=pl.BlockSpec((1, ll, LANES), lambda i, h: (i, 0, h)),
        out_shape=jax.ShapeDtypeStruct((nb, ll, HEADS * HEAD_V), BF16),
        scratch_shapes=[pltpu.VMEM((3, max(lc, ll) + 16, LANES), F32),
                        pltpu.VMEM((lt, LANES), F32), pltpu.VMEM((lt, LANES), F32), pltpu.VMEM((lt, LANES), F32),
                        pltpu.VMEM((2, lt, LANES), F32), pltpu.VMEM((2, lt, LANES), F32),
                        pltpu.VMEM((2 * RING, 2 * CHUNK, LANES), F32),
                        pltpu.VMEM((2 * RING, CHUNK, LANES), F32), pltpu.VMEM((2 * RING, CHUNK, LANES), F32),
                        pltpu.VMEM((2 * RING, CHUNK, CHUNK), F32),
                        pltpu.VMEM((2 * RING, 8, LANES), F32),
                        pltpu.VMEM((ll, LANES), F32)],
        compiler_params=_params(("arbitrary", "arbitrary")),
    )(proj_c, proj_c, proj_c, proj_l, proj_l, proj_l, proj_l, conv_w8, conv_w8, conv_w8,
      gd_c, gd_l, norm_g.reshape(1, LANES))


def _mlstm_kernel(qc_ref, kc_ref, vc_ref, ql_ref, kl_ref, vl_ref, mlc_ref, mll_ref, h_ref, out_s, tabs):
    lc = qc_ref.shape[1]
    ll = ql_ref.shape[1]
    n_c, n_l = lc // CHUNK, ll // CHUNK
    n_t = n_c + n_l
    lt = lc + ll
    pair = pl.program_id(1)
    lane = lax.broadcasted_iota(jnp.int32, (CHUNK, LANES), 1)
    ones_v = jnp.ones((CHUNK, HEAD_V), BF16)
    chains = [(hh, d) for hh in range(2) for d in range(2)]
    hmask = [((lane // ML_DK) == hh).astype(F32) for hh in range(2)]
    incl = [_dir_masks(d)[0] for d in range(2)]

    def build_tables(src_ref, off, ls):
        step = min(256, ls)
        for r0 in range(0, ls, step):
            lanes = [16 * j + 8 * d + 2 * pair + hh for hh, d in chains for j in range(3)]
            for g, tab in enumerate(_lane_picks(src_ref[0, r0:r0 + step, :], lanes, n_by_matmul=5)):
                tabs[g, off + r0:off + r0 + step, :] = tab

    build_tables(mlc_ref, 0, lc)
    build_tables(mll_ref, lc, ll)

    def wide(a):
        return jnp.concatenate([a, a], axis=1)

    def chain(hh, d, c, get_state, is_ctx, result):
        last = CHUNK - 1 if d == 0 else 0
        if is_ctx:
            rows = pl.ds(pl.multiple_of(c * CHUNK, CHUNK), CHUNK)
            q_ref, k_ref, v_ref = qc_ref, kc_ref, vc_ref
        else:
            rows = pl.ds(pl.multiple_of((c - n_c) * CHUNK, CHUNK), CHUNK)
            q_ref, k_ref, v_ref = ql_ref, kl_ref, vl_ref
        q = (q_ref[0, rows, :].astype(F32) * hmask[hh]).astype(BF16)
        k = k_ref[0, rows, :].astype(F32) * (hmask[hh] * (ML_DK ** -0.5))
        v = jnp.concatenate([v_ref[0, rows, hh * HEAD_V:(hh + 1) * HEAD_V], ones_v], axis=1)
        n = chains.index((hh, d))
        trows = pl.ds(pl.multiple_of(c * CHUNK, CHUNK), CHUNK)
        gmb = tabs[3 * n, trows, :]
        gmb_t = jnp.transpose(gmb)[0:CHUNK, :]
        cm = tabs[3 * n + 1, trows, :]
        bc = tabs[3 * n + 2, trows, :]
        qk = _dot_nt(q, k.astype(BF16))
        cm_last = cm[last:last + 1, :]
        b_last = bc[last:last + 1, :]
        wk = (k * jnp.exp(gmb - cm_last)).astype(BF16)
        c_loc = _dot_tn(wk, v)
        yield
        while get_state() is None:
            yield
        cs, ms = get_state()
        mm = jnp.maximum(ms, cm)
        p = jnp.where(incl[d], jnp.exp(gmb_t - mm[:, 0:CHUNK]), 0.0) * qk
        inter = _dot(q, cs.astype(BF16))
        intra = _dot(p.astype(BF16), v)
        yield
        if not is_ctx:
            nd = wide(jnp.exp(ms - mm)) * inter + intra
            hv = nd[:, 0:HEAD_V] / jnp.maximum(jnp.abs(nd[:, HEAD_V:2 * HEAD_V]), jnp.exp(-(bc + mm)))
            l0 = pl.multiple_of((c - n_c) * CHUNK, CHUNK)
            out_s[pl.ds(l0, CHUNK), hh * HEAD_V:(hh + 1) * HEAD_V] += hv
        mx = jnp.maximum(ms, cm_last)
        result[hh, d] = (wide(jnp.exp(ms - mx)) * cs + wide(jnp.exp(cm_last - mx)) * c_loc, b_last + mx)

    out_s[...] = jnp.zeros(out_s.shape, F32)

    def run(chains_iter):
        live = list(chains_iter)
        while live:
            alive = []
            for ch in live:
                try:
                    next(ch)
                    alive.append(ch)
                except StopIteration:
                    pass
            live = alive

    def body(i, carry, is_ctx):
        results = [{} for _ in range(ML_STEPS)]
        gens = []
        for j in range(ML_STEPS):
            t = ML_STEPS * i + j
            for n, (hh, d) in enumerate(chains):
                if is_ctx:
                    c = t if d == 0 else n_c - 1 - t
                else:
                    c = n_c + t if d == 0 else n_t - 1 - t
                if j == 0:
                    get_state = functools.partial(lambda n: carry[n], n)
                else:
                    get_state = functools.partial(results[j - 1].get, (hh, d))
                gens.append(chain(hh, d, c, get_state, is_ctx, results[j]))
        run(gens)
        return tuple(results[-1][hd] for hd in chains)

    st0 = (jnp.zeros((LANES, 2 * HEAD_V), F32), jnp.zeros((1, LANES), F32))
    carry = lax.fori_loop(0, n_c // ML_STEPS, functools.partial(body, is_ctx=True), (st0,) * 4)
    lax.fori_loop(0, n_l // ML_STEPS, functools.partial(body, is_ctx=False), carry)
    h_ref[0] = out_s[...].astype(h_ref.dtype)


def _mlstm(proj_c, q_l, k_l, v_l, ml_c, ml_l, b0):
    lc = proj_c.shape[1]
    nb, ll, _ = q_l.shape
    lt = lc + ll
    qb, kb_, vb = COL_ML_Q // LANES, COL_ML_K // LANES, COL_ML_V // (2 * HEAD_V)
    return pl.pallas_call(
        _mlstm_kernel,
        grid=(nb, HEADS // 2),
        in_specs=[pl.BlockSpec((1, lc, LANES), lambda i, p: (i + b0, 0, qb + p)),
                  pl.BlockSpec((1, lc, LANES), lambda i, p: (i + b0, 0, kb_ + p)),
                  pl.BlockSpec((1, lc, 2 * HEAD_V), lambda i, p: (i + b0, 0, vb + p)),
                  pl.BlockSpec((1, ll, LANES), lambda i, p: (i, 0, p)),
                  pl.BlockSpec((1, ll, LANES), lambda i, p: (i, 0, p)),
                  pl.BlockSpec((1, ll, 2 * HEAD_V), lambda i, p: (i, 0, p)),
                  pl.BlockSpec((1, lc, LANES), lambda i, p: (i + b0, 0, 0)),
                  pl.BlockSpec((1, ll, LANES), lambda i, p: (i + b0, 0, 0))],
        out_specs=pl.BlockSpec((1, ll, 2 * HEAD_V), lambda i, p: (i, 0, p)),
        out_shape=jax.ShapeDtypeStruct((nb, ll, HEADS * HEAD_V), BF16),
        scratch_shapes=[pltpu.VMEM((ll, 2 * HEAD_V), F32), pltpu.VMEM((12, lt, LANES), F32)],
        compiler_params=_params(("arbitrary", "arbitrary")),
    )(proj_c, proj_c, proj_c, q_l, k_l, v_l, ml_c, ml_l)


def _merge_kernel(yg_ref, hm_ref, o_ref, gg_ref, gm_ref, x_ref, mod_ref, mlg_ref, n2_ref,
                  wbg_ref, wbm_ref, wo_ref, wrh_ref, wrl_ref, x1_ref, h2_ref, sc_ref):
    d = x_ref.shape[1]
    o = o_ref[...].astype(F32)
    ym = _sigmoid(o) * hm_ref[...].astype(F32)
    segs = []
    for h in range(HEADS):
        seg = ym[:, h * HEAD_V:(h + 1) * HEAD_V]
        segs.append(seg * lax.rsqrt(jnp.mean(seg * seg, axis=-1, keepdims=True) + EPS))
    ymn = jnp.concatenate(segs, axis=1) * mlg_ref[...]
    y_gdn = _dot(yg_ref[...], wbg_ref[...])
    y_ml = _dot(ymn.astype(BF16), wbm_ref[...])
    mixed = _sigmoid(gg_ref[...].astype(F32)) * y_gdn + _sigmoid(gm_ref[...].astype(F32)) * y_ml
    y = _dot(mixed.astype(BF16), wo_ref[...])
    x1 = x_ref[...] + mod_ref[0, :, 2 * d:3 * d] * y
    x1_ref[...] = x1
    hn = x1 * lax.rsqrt(jnp.mean(x1 * x1, axis=-1, keepdims=True) + EPS) * n2_ref[...]
    h2 = hn * (1.0 + mod_ref[0, :, 4 * d:5 * d]) + mod_ref[0, :, 3 * d:4 * d]
    h2_hi = h2.astype(BF16)
    h2_ref[...] = h2_hi
    h2_lo = (h2 - h2_hi.astype(F32)).astype(BF16)
    logits = _dot_nt(wrh_ref[...], h2_hi) + (_dot_nt(wrl_ref[...], h2_hi) + _dot_nt(wrh_ref[...], h2_lo))
    sc_ref[...] = _sigmoid(logits)


def _merge(y_gdn, h_ml, proj_l2d, x2d, mod3, rows_per_mod, ml_norm_g, norm2_g, wbg, wbm, wo, wr_hi, wr_lo, tok0,
           tm=512):
    t, d = y_gdn.shape
    e = wr_hi.shape[0]
    off = tok0 // tm
    row = lambda i: (i, 0)
    const = lambda i: (0, 0)
    return pl.pallas_call(
        _merge_kernel,
        grid=(t // tm,),
        in_specs=[pl.BlockSpec((tm, d), row), pl.BlockSpec((tm, d), row),
                  pl.BlockSpec((tm, d), lambda i: (i + off, COL_ML_O // d)),
                  pl.BlockSpec((tm, d), lambda i: (i + off, COL_MG_GDN // d)),
                  pl.BlockSpec((tm, d), lambda i: (i + off, COL_MG_ML // d)),
                  pl.BlockSpec((tm, d), lambda i: (i + off, 0)),
                  pl.BlockSpec((1, 1, mod3.shape[2]), lambda i: (((i + off) * tm) // rows_per_mod, 0, 0)),
                  pl.BlockSpec((1, d), const), pl.BlockSpec((1, d), const),
                  pl.BlockSpec((d, d), const), pl.BlockSpec((d, d), const), pl.BlockSpec((d, d), const),
                  pl.BlockSpec((e, d), const), pl.BlockSpec((e, d), const)],
        out_specs=[pl.BlockSpec((tm, d), row), pl.BlockSpec((tm, d), lambda i: (i + off, 0)),
                   pl.BlockSpec((e, tm), lambda i: (0, i))],
        out_shape=[jax.ShapeDtypeStruct((t, d), F32), jax.ShapeDtypeStruct((x2d.shape[0], d), BF16),
                   jax.ShapeDtypeStruct((e, t), F32)],
        compiler_params=_params(("arbitrary",)),
    )(y_gdn, h_ml, proj_l2d, proj_l2d, proj_l2d, x2d, mod3, ml_norm_g.reshape(1, d), norm2_g.reshape(1, d),
      wbg, wbm, wo, wr_hi, wr_lo)


def _expert_kernel(be_ref, nu_ref, nx_ref, x_ref, wgu_hbm, wd_hbm, y_ref, land_gu, land_d, wgu_s, wd_s, sems):
    i = pl.program_id(0)
    de = wd_hbm.shape[1]
    used = i < nu_ref[0]
    first = jnp.logical_or(i == 0, be_ref[i] != be_ref[jnp.maximum(i - 1, 0)])

    def weight_copies(ex):
        return (pltpu.make_async_copy(wgu_hbm.at[ex], land_gu, sems.at[0]),
                pltpu.make_async_copy(wd_hbm.at[ex], land_d, sems.at[1]))

    @pl.when(i == 0)
    def _():
        for cp in weight_copies(be_ref[0]):
            cp.start()

    @pl.when(jnp.logical_and(first, used))
    def _():
        for cp in weight_copies(be_ref[i]):
            cp.wait()
        wgu_s[...] = land_gu[...].astype(BF16)
        wd_s[...] = land_d[...].astype(BF16)

        @pl.when(nx_ref[i] >= 0)
        def _():
            for cp in weight_copies(nx_ref[i]):
                cp.start()

    @pl.when(used)
    def _():
        gu = _dot(x_ref[...], wgu_s[...])
        g = gu[:, 0:de]
        act = (g * _sigmoid(g)) * gu[:, de:2 * de]
        y_ref[...] = _dot(act.astype(BF16), wd_s[...]).astype(y_ref.dtype)

    @pl.when(jnp.logical_not(used))
    def _():
        y_ref[...] = jnp.zeros(y_ref.shape, y_ref.dtype)


def _experts(xb, blk_expert, n_used, next_expert, w_gu, w_down):
    n_slots, d = xb.shape
    n_blocks = n_slots // EXPERT_ROWS
    e, _, de2 = w_gu.shape
    de = de2 // 2
    return pl.pallas_call(
        _expert_kernel,
        grid_spec=pltpu.PrefetchScalarGridSpec(
            num_scalar_prefetch=3,
            grid=(n_blocks,),
            in_specs=[pl.BlockSpec((EXPERT_ROWS, d), lambda i, be, nu, nx: (i, 0)),
                      pl.BlockSpec(memory_space=pl.ANY), pl.BlockSpec(memory_space=pl.ANY)],
            out_specs=pl.BlockSpec((EXPERT_ROWS, d), lambda i, be, nu, nx: (i, 0)),
            scratch_shapes=[pltpu.VMEM((d, de2), w_gu.dtype), pltpu.VMEM((de, d), w_down.dtype),
                            pltpu.VMEM((d, de2), BF16), pltpu.VMEM((de, d), BF16),
                            pltpu.SemaphoreType.DMA((2,))]),
        out_shape=jax.ShapeDtypeStruct((n_slots, d), BF16),
        compiler_params=_params(("arbitrary",)),
    )(blk_expert, n_used, next_expert, xb, w_gu, w_down)


def _final_kernel(x1_ref, h2_ref, yg_ref, wt_ref, mod_ref, wsg_ref, wsd_ref, fg_ref, o_ref):
    d = x1_ref.shape[1]
    ds_ = wsd_ref.shape[0]
    wt = wt_ref[...]
    routed = jnp.zeros(x1_ref.shape, F32)
    for k in range(TOP_K):
        routed = routed + wt[:, k:k + 1] * yg_ref[k].astype(F32)
    gu = _dot(h2_ref[...], wsg_ref[...])
    g = gu[:, 0:ds_]
    sh = _dot(((g * _sigmoid(g)) * gu[:, ds_:2 * ds_]).astype(BF16), wsd_ref[...])
    x2 = x1_ref[...] + mod_ref[0, :, 5 * d:6 * d] * (routed + sh)
    o_ref[...] = x2 * lax.rsqrt(jnp.mean(x2 * x2, axis=-1, keepdims=True) + EPS) * fg_ref[...]


def _final(x1, h2, yg, wts, mod3, rows_per_mod, w_sh_gu, w_sh_down, final_g, tok0, tm=256):
    tp, d = x1.shape
    off = tok0 // tm
    row = lambda i: (i, 0)
    const = lambda i: (0, 0)
    return pl.pallas_call(
        _final_kernel,
        grid=(tp // tm,),
        in_specs=[pl.BlockSpec((tm, d), row), pl.BlockSpec((tm, d), lambda i: (i + off, 0)),
                  pl.BlockSpec((TOP_K, tm, d), lambda i: (0, i, 0)), pl.BlockSpec((tm, TOP_K), row),
                  pl.BlockSpec((1, 1, mod3.shape[2]), lambda i: (((i + off) * tm) // rows_per_mod, 0, 0)),
                  pl.BlockSpec(w_sh_gu.shape, const), pl.BlockSpec(w_sh_down.shape, const),
                  pl.BlockSpec((1, d), const)],
        out_specs=pl.BlockSpec((tm, d), row),
        out_shape=jax.ShapeDtypeStruct((tp, d), F32),
        compiler_params=_params(("arbitrary",)),
    )(x1, h2, yg, wts, mod3, w_sh_gu, w_sh_down, final_g.reshape(1, d))


def _route_kernel(sc_ref, bias_ref, tri_ref, idx_ref, wt_ref, rk_ref, cnt_ref, base_s):
    @pl.when(pl.program_id(0) == 0)
    def _():
        base_s[...] = jnp.zeros(base_s.shape, F32)

    scores = sc_ref[...]
    e, tn = scores.shape
    gsz = e // N_GROUPS
    sel3 = (scores + bias_ref[...]).reshape(N_GROUPS, gsz, tn)
    m1 = jnp.max(sel3, axis=1)
    is_max = sel3 == m1[:, None, :]
    n_max = jnp.sum(is_max.astype(F32), axis=1)
    m2 = jnp.max(jnp.where(is_max, -jnp.inf, sel3), axis=1)
    grp = m1 + jnp.where(n_max >= 2.0, m1, m2)
    gi = lax.broadcasted_iota(jnp.int32, (N_GROUPS, tn), 0)
    ahead = jnp.zeros((N_GROUPS, tn), F32)
    for g in range(N_GROUPS):
        row = grp[g:g + 1, :]
        ahead = ahead + jnp.logical_or(row > grp, jnp.logical_and(row == grp, g < gi)).astype(F32)
    ahead3 = jnp.broadcast_to(ahead[:, None, :], (N_GROUPS, gsz, tn))
    selm = jnp.where(ahead3 < float(TOPK_GROUPS), sel3, -jnp.inf).reshape(e, tn)
    ri = lax.broadcasted_iota(jnp.int32, (e, tn), 0).astype(F32)
    member = jnp.zeros((e, tn), F32)
    idxs, ws = [], []
    for _ in range(TOP_K):
        m = jnp.max(selm, axis=0, keepdims=True)
        idx = jnp.min(jnp.where(selm == m, ri, float(e)), axis=0, keepdims=True)
        hit = ri == idx
        ws.append(jnp.sum(jnp.where(hit, scores, 0.0), axis=0, keepdims=True))
        idxs.append(idx)
        selm = jnp.where(hit, -jnp.inf, selm)
        member = jnp.where(hit, 1.0, member)
    w = jnp.concatenate(ws, axis=0)
    wt_ref[...] = w / jnp.sum(w, axis=0, keepdims=True) * ROUTED_SCALE
    idx_ref[...] = jnp.concatenate(idxs, axis=0).astype(jnp.int32)
    cum = _dot(member.astype(BF16), tri_ref[...]) + base_s[...]
    rk_ref[...] = jnp.concatenate(
        [jnp.sum(jnp.where(ri == idx, cum, 0.0), axis=0, keepdims=True) for idx in idxs], axis=0).astype(jnp.int32)
    total = base_s[...] + jnp.sum(member, axis=1, keepdims=True)
    base_s[...] = total
    cnt_ref[...] = total


def _route(scores_t, router_bias):
    e, t = scores_t.shape
    tn = LANES
    bias = jnp.broadcast_to(router_bias.astype(F32)[:, None], (e, tn))
    tri = (jnp.arange(tn)[:, None] < jnp.arange(tn)[None, :]).astype(BF16)
    tok = pl.BlockSpec((TOP_K, tn), lambda i: (0, i))
    const = lambda i: (0, 0)
    return pl.pallas_call(
        _route_kernel,
        grid=(t // tn,),
        in_specs=[pl.BlockSpec((e, tn), lambda i: (0, i)), pl.BlockSpec((e, tn), const),
                  pl.BlockSpec((tn, tn), const)],
        out_specs=[tok, tok, tok, pl.BlockSpec((e, tn), const)],
        out_shape=[jax.ShapeDtypeStruct((TOP_K, t), jnp.int32), jax.ShapeDtypeStruct((TOP_K, t), F32),
                   jax.ShapeDtypeStruct((TOP_K, t), jnp.int32), jax.ShapeDtypeStruct((e, tn), F32)],
        scratch_shapes=[pltpu.VMEM((e, tn), F32)],
        compiler_params=_params(("arbitrary",)),
    )(scores_t, bias, tri)


def _slot_kernel(idx_ref, rk_ref, ps_ref, pos_ref):
    e, tn = ps_ref.shape
    ri = lax.broadcasted_iota(jnp.int32, (e, tn), 0)
    ps = ps_ref[...]
    rows = [jnp.sum(jnp.where(ri == idx_ref[k:k + 1, :], ps, 0.0), axis=0, keepdims=True) for k in range(TOP_K)]
    pos_ref[...] = rk_ref[...] + jnp.concatenate(rows, axis=0).astype(jnp.int32)


def _slots(idx, rank, pstart):
    k, t = idx.shape
    e = pstart.shape[0]
    tn = LANES
    tok = pl.BlockSpec((k, tn), lambda i: (0, i))
    return pl.pallas_call(
        _slot_kernel,
        grid=(t // tn,),
        in_specs=[tok, tok, pl.BlockSpec((e, tn), lambda i: (0, 0))],
        out_specs=tok,
        out_shape=jax.ShapeDtypeStruct((k, t), jnp.int32),
        compiler_params=_params(("arbitrary",)),
    )(idx, rank, jnp.broadcast_to(pstart.astype(F32)[:, None], (e, tn)))


def _sc_scatter_rows(vals, idx, n_rows, window=LANES):
    n, width = vals.shape
    mesh = plsc.VectorSubcoreMesh(core_axis_name="core", subcore_axis_name="subcore")

    @pl.kernel(out_type=jax.ShapeDtypeStruct((n_rows, width), vals.dtype), mesh=mesh, scratch_types=[])
    def scatter(v_hbm, i_hbm, o_hbm):
        def body(v_vmem, i_vmem):
            pltpu.sync_copy(v_vmem, o_hbm.at[i_vmem.at[0]])

        pltpu.emit_pipeline(
            body,
            grid=(n // window,),
            in_specs=[pl.BlockSpec((window, width), lambda i: (i, 0)),
                      pl.BlockSpec((1, window), lambda i: (0, i))],
            out_specs=[],
            core_axis_name=("core", "subcore"),
            dimension_semantics=(pltpu.PARALLEL,),
        )(v_hbm, i_hbm)

    return scatter(vals, idx.reshape(1, n))


def _block_table(counts, n_blocks):
    padded = (counts + EXPERT_ROWS - 1) // EXPERT_ROWS * EXPERT_ROWS
    pend = jnp.cumsum(padded)
    first_slot = jnp.arange(n_blocks, dtype=jnp.int32) * EXPERT_ROWS
    blk_expert = jnp.minimum(jnp.sum((pend[None, :] <= first_slot[:, None]).astype(jnp.int32), axis=1),
                             N_EXPERTS - 1)
    e_ids = jnp.arange(N_EXPERTS, dtype=jnp.int32)
    later = lax.cummin(jnp.where(counts > 0, e_ids, N_EXPERTS)[::-1])[::-1]
    next_tab = jnp.concatenate([later[1:], jnp.full((1,), N_EXPERTS, jnp.int32)])
    next_tab = jnp.where(next_tab < N_EXPERTS, next_tab, -1)
    return (pend - padded, blk_expert, (pend[-1] // EXPERT_ROWS).astype(jnp.int32).reshape(1),
            next_tab[blk_expert].astype(jnp.int32))


def _col_major(t):
    b, l, f = t.shape
    rows = l // GRID_W
    return t.reshape(b, rows, GRID_W, f).transpose(0, 2, 1, 3).reshape(b, l, f)


def _row_major(t):
    b, l, f = t.shape
    rows = l // GRID_W
    return t.reshape(b, GRID_W, rows, f).transpose(0, 2, 1, 3).reshape(b, l, f)


def kernel(x, c, ctx, c_ctx, w_ada, b_ada, norm1_g, norm2_g, w_in, gdn_conv_w, gdn_a_log, gdn_dt_bias, gdn_norm_g,
           ml_i_bias, ml_f_bias, ml_norm_g, w_branch_gdn, w_branch_ml, w_out, w_router, router_bias, w_exp_gate_up,
           w_exp_down, w_sh_gate_up, w_sh_down, final_norm_g):
    b, l, d = x.shape
    lc = ctx.shape[1]
    t = b * l
    layer = 0

    w = w_in[layer]
    main_cols = [_ORIG[k] for k in ("gdn_qkv", "ml_q", "ml_k", "ml_v", "gdn_z", "ml_o", "mg_gdn", "mg_ml")]
    w_main = jnp.concatenate([w[:, a:e] for a, e in main_cols], axis=1).astype(BF16)
    w_gate = jnp.concatenate([w[:, _ORIG["gdn_gate"][0]:_ORIG["gdn_gate"][1]],
                              w[:, _ORIG["ml_gate"][0]:_ORIG["ml_gate"][1]],
                              jnp.zeros((d, LANES - 64), F32)], axis=1).astype(BF16)
    zeros16 = jnp.zeros((16,), F32)
    gp_add = jnp.concatenate([zeros16, gdn_dt_bias[layer].reshape(-1), ml_i_bias[layer].reshape(-1),
                              ml_f_bias[layer].reshape(-1), jnp.zeros((LANES - 64,), F32)])
    gp_mul = jnp.concatenate([zeros16, -jnp.exp(gdn_a_log[layer].astype(F32)).reshape(-1),
                              jnp.zeros((LANES - 32,), F32)])
    gparams = jnp.zeros((8, LANES), F32).at[0].set(gp_add).at[1].set(gp_mul)
    conv_w8 = jnp.zeros((8, gdn_conv_w.shape[2]), F32).at[0:GDN_CONV].set(gdn_conv_w[layer])
    wr = w_router[layer].T
    wr_hi = wr.astype(BF16)
    wr_lo = (wr - wr_hi.astype(F32)).astype(BF16)

    n_mod_rows = -(-(b + 1) // 8) * 8
    cc = jnp.zeros((n_mod_rows, d), F32).at[0:b].set(c).at[b].set(c_ctx)
    mod = _ada_mod(cc, w_ada[layer], b_ada[layer])
    mod3 = mod.reshape(n_mod_rows, 1, 6 * d)

    x2d = x.reshape(t, d)
    tm_l = min(1024, l)
    proj_l, gate_l = _project(x2d, mod3, lambda i: (i * tm_l) // l, norm1_g[layer], w_main, w_gate, tm_l)
    tm_c = min(1024, b * lc)
    proj_c, gate_c = _project(ctx.reshape(b * lc, d), mod3, lambda i: b, norm1_g[layer], w_main[:, 0:N_SCAN],
                              w_gate, tm_c, tn=N_SCAN // 2)
    proj_l3 = proj_l.reshape(b, l, N_MAIN)
    proj_c3 = proj_c.reshape(b, lc, N_SCAN)

    gate_l_cm = _col_major(gate_l.reshape(b, l, LANES)).reshape(t, LANES)
    gd_c, ml_c = _gate_prep(gate_c, gparams, True, True)
    gd_l, = _gate_prep(gate_l, gparams, True, False)
    ml_l, = _gate_prep(gate_l_cm, gparams, False, True)

    nb = b // BATCH_PARTS
    tp = nb * l
    n_assign = tp * TOP_K
    n_blocks = (n_assign + N_EXPERTS * (EXPERT_ROWS - 1)) // EXPERT_ROWS + 1
    n_slots = n_blocks * EXPERT_ROWS
    gd_c3, gd_l3 = gd_c.reshape(b, lc, LANES), gd_l.reshape(b, l, LANES)
    ml_c3, ml_l3 = ml_c.reshape(b, lc, LANES), ml_l.reshape(b, l, LANES)
    wbg, wbm, wo = (w_branch_gdn[layer].astype(BF16), w_branch_ml[layer].astype(BF16), w_out[layer].astype(BF16))
    w_sh_gu, w_sh_dn = w_sh_gate_up[layer].astype(BF16), w_sh_down[layer].astype(BF16)
    outs = []
    for part in range(BATCH_PARTS):
        b0, tok0 = part * nb, part * tp
        y_gdn = _gdn(proj_c3, proj_l3, conv_w8, gd_c3, gd_l3, gdn_norm_g[layer], b0, nb)
        q_cm = _col_major(proj_l3[b0:b0 + nb, :, COL_ML_Q:COL_ML_Q + HEADS * ML_DK])
        k_cm = _col_major(proj_l3[b0:b0 + nb, :, COL_ML_K:COL_ML_K + HEADS * ML_DK])
        v_cm = _col_major(proj_l3[b0:b0 + nb, :, COL_ML_V:COL_ML_V + HEADS * HEAD_V])
        h_ml = _row_major(_mlstm(proj_c3, q_cm, k_cm, v_cm, ml_c3, ml_l3, b0))
        x1, h2, scores_t = _merge(y_gdn.reshape(tp, d), h_ml.reshape(tp, d), proj_l, x2d, mod3, l, ml_norm_g[layer],
                                norm2_g[layer], wbg, wbm, wo, wr_hi, wr_lo, tok0, tm=min(512, l))
        idx, wts, rank, cnt = _route(scores_t, router_bias[layer])
        counts = cnt[:, 0].astype(jnp.int32)
        pstart, blk_expert, n_used, next_expert = _block_table(counts, n_blocks)
        pos = _slots(idx, rank, pstart).reshape(n_assign)
        tok_ids = jnp.broadcast_to((jnp.arange(n_assign, dtype=jnp.int32) % tp)[:, None], (n_assign, LANES))
        scattered = _sc_scatter_rows(tok_ids, pos, n_slots)[:, 0]
        in_expert = (jnp.arange(n_slots, dtype=jnp.int32).reshape(n_blocks, EXPERT_ROWS)
                     - pstart[blk_expert][:, None])
        valid = (in_expert < counts[blk_expert][:, None]).reshape(n_slots)
        tok_slot = jnp.where(valid, scattered, jnp.arange(n_slots, dtype=jnp.int32) % tp)
        xb = h2.at[tok_slot + tok0].get(mode="promise_in_bounds")
        yb = _experts(xb, blk_expert, n_used, next_expert, w_exp_gate_up[layer], w_exp_down[layer])
        yg = yb.at[pos].get(mode="promise_in_bounds", unique_indices=True).reshape(TOP_K, tp, d)
        outs.append(_final(x1, h2, yg, wts.T, mod3, l, w_sh_gu, w_sh_dn, final_norm_g, tok0, tm=min(512, l)))
    return jnp.concatenate(outs, axis=0).reshape(b, l, d)
```
